```python
import jax, jax.numpy as jnp
from jax import lax
import numpy as np

D_MODEL = 1024
BATCH = 2
SEQ = 16384
DEPTH = 1
DEC_BATCH = 32
DEC_SEQ = 32
PAST_LEN = 1024

CHUNK = 64
QBLK = 128
MIX_WIDTH = D_MODEL
MLA_HEADS = 8
MLA_NOPE = 64
MLA_ROPE = 32
MLA_V = 64
Q_LORA = 384
KV_LORA = 256
ROPE_BASE = 10000.0
MLA_OUT = MLA_HEADS * MLA_V
MLA_IN = Q_LORA + KV_LORA + MLA_ROPE
MLA_SCALE = (MLA_NOPE + MLA_ROPE) ** -0.5
RWKV_HEADS = 8
RWKV_N = 64
RWKV_DIM = RWKV_HEADS * RWKV_N
DECAY_LORA = 64
AAA_LORA = 64
GATE_LORA = 128
RWKV_IN = 3 * RWKV_DIM + DECAY_LORA + AAA_LORA + GATE_LORA
IN_WIDTH = MLA_IN + RWKV_IN
N_GROUPS = 4
EXPERTS_PER_GROUP = 8
N_EXPERTS = N_GROUPS * EXPERTS_PER_GROUP
TOP_K = 2
D_EXPERT = 256
MOE_BLK = 256
LN_EPS = 1e-5
RMS_EPS = 1e-6
GN_EPS = 64e-5
NEG_INF = -1e30
DN_ALPHA = (2 * DEPTH) ** 0.25
DN_BETA = (8 * DEPTH) ** -0.25

kernel_name = "hybrid_mla_rwkv7_hmoe_stream_step"

F32 = jnp.float32


def rms_norm(x, g):
    xf = x.astype(F32)
    y = xf * lax.rsqrt(jnp.mean(xf * xf, -1, keepdims=True) + RMS_EPS)
    return (y * g.astype(F32)).astype(x.dtype)


def layer_norm(x, g, b):
    xf = x.astype(F32)
    xc = xf - jnp.mean(xf, -1, keepdims=True)
    var = jnp.mean(xc * xc, -1, keepdims=True)
    return (xc * lax.rsqrt(var + LN_EPS) * g.astype(F32) + b.astype(F32)).astype(x.dtype)


def rope_tables(pos):
    inv = ROPE_BASE ** (-jnp.arange(0, MLA_ROPE, 2, dtype=F32) / MLA_ROPE)
    ang = pos.astype(F32)[:, None] * inv[None, :]
    return jnp.cos(ang), jnp.sin(ang)


def apply_rope(x, cos, sin):
    half = MLA_ROPE // 2
    xf = x.astype(F32)
    x1, x2 = xf[..., :half], xf[..., half:]
    return jnp.concatenate([x1 * cos - x2 * sin, x1 * sin + x2 * cos], -1).astype(x.dtype)


def chunk_mask(qpos, kpos):
    return (kpos // CHUNK)[None, :] <= (qpos // CHUNK)[:, None]


def mla_block_attend(qn, qp, kn, kp, v, qpos, kpos):
    s = jnp.einsum('bqhd,bkhd->bhqk', qn, kn) + jnp.einsum('bqhr,bkr->bhqk', qp, kp)
    s = jnp.where(chunk_mask(qpos, kpos), s.astype(F32) * MLA_SCALE, NEG_INF)
    p = jax.nn.softmax(s, axis=-1).astype(v.dtype)
    return jnp.einsum('bhqk,bkhd->bqhd', p, v)


def mla_prompt(q_nope, q_pe, ckv, kpe, w_ukv, pos):
    kv = jnp.einsum('bsl,lhe->bshe', ckv, w_ukv)
    k_nope, v = kv[..., :MLA_NOPE], kv[..., MLA_NOPE:]
    outs = []
    for i in range(q_nope.shape[1] // QBLK):
        lo, hi = i * QBLK, (i + 1) * QBLK
        outs.append(mla_block_attend(q_nope[:, lo:hi], q_pe[:, lo:hi], k_nope[:, :hi], kpe[:, :hi],
                                     v[:, :hi], pos[lo:hi], pos[:hi]))
    return jnp.concatenate(outs, axis=1)


def mla_sample(q_nope, q_pe, ckv, kpe, past_ckv, past_kpe, w_ukv, pos):
    ckv_all = jnp.concatenate([past_ckv.astype(ckv.dtype), ckv], axis=1)
    kpe_all = jnp.concatenate([past_kpe.astype(kpe.dtype), kpe], axis=1)
    kpos = jnp.arange(ckv_all.shape[1], dtype=jnp.int32)
    w_uk, w_uv = w_ukv[..., :MLA_NOPE], w_ukv[..., MLA_NOPE:]
    q_lat = jnp.einsum('bqhd,lhd->bqhl', q_nope, w_uk)
    s = jnp.einsum('bqhl,bkl->bhqk', q_lat, ckv_all) + jnp.einsum('bqhr,bkr->bhqk', q_pe, kpe_all)
    s = jnp.where(chunk_mask(pos, kpos), s.astype(F32) * MLA_SCALE, NEG_INF)
    p = jax.nn.softmax(s, axis=-1).astype(ckv_all.dtype)
    o_lat = jnp.einsum('bhqk,bkl->bqhl', p, ckv_all)
    return jnp.einsum('bqhl,lhd->bqhd', o_lat, w_uv)


def wkv_scan(r, w, k, v, a, b, s0):
    def step(s, inp):
        rt, wt, kt, vt, at, bt = inp
        sa = jnp.einsum('bhij,bhj->bhi', s, at)
        s = s * wt[:, :, None, :] + sa[..., None] * bt[:, :, None, :] + vt[..., None] * kt[:, :, None, :]
        return s, jnp.einsum('bhij,bhj->bhi', s, rt)
    xs = tuple(jnp.moveaxis(t.astype(F32), 1, 0) for t in (r, w, k, v, a, b))
    s_fin, y = lax.scan(step, s0.astype(F32), xs, unroll=8)
    return s_fin, jnp.moveaxis(y, 0, 1)


def rwkv7_mixer(pr, wkv0, shift0, mu_shift, w0, w2, a0, a2, g2, k_k, k_a, r_k, lnx_g, lnx_b):
    B, S, _ = pr.shape
    prev = jnp.concatenate([shift0[:, None, :].astype(pr.dtype), pr[:, :-1]], axis=1)
    u = pr + mu_shift * (prev - pr)
    o1, o2, o3 = RWKV_DIM, 2 * RWKV_DIM, 3 * RWKV_DIM
    o4, o5 = o3 + DECAY_LORA, o3 + DECAY_LORA + AAA_LORA
    r, k, v = u[..., :o1], u[..., o1:o2], u[..., o2:o3]
    w_lo, a_lo, g_lo = u[..., o3:o4], u[..., o4:o5], u[..., o5:]
    wl = (w0 + jnp.tanh(w_lo) @ w2).astype(F32)
    decay = jnp.exp(-jnp.exp(-jax.nn.softplus(-wl) - 0.5))
    a = jax.nn.sigmoid((a0 + a_lo @ a2).astype(F32))
    g = jax.nn.sigmoid(g_lo) @ g2
    hs = (B, S, RWKV_HEADS, RWKV_N)
    kf = k.astype(F32)
    kk = (kf * k_k).reshape(hs)
    kk = kk / jnp.maximum(jnp.sqrt(jnp.sum(kk * kk, -1, keepdims=True)), 1e-12)
    kh = (kf * (1.0 + (a - 1.0) * k_a)).reshape(hs)
    rf = r.astype(F32).reshape(hs)
    vh = v.astype(F32).reshape(hs)
    s_fin, y = wkv_scan(rf, decay.reshape(hs), kh, vh, -kk, kk * a.reshape(hs), wkv0)
    yc = y - jnp.mean(y, -1, keepdims=True)
    y = yc * lax.rsqrt(jnp.mean(yc * yc, -1, keepdims=True) + GN_EPS)
    y = y.reshape(B, S, RWKV_DIM) * lnx_g + lnx_b
    bonus = jnp.sum(rf * kh * r_k, -1, keepdims=True) * vh
    y = y + bonus.reshape(B, S, RWKV_DIM)
    return (y * g.astype(F32)).astype(pr.dtype), s_fin


def token_mixers(x, pos0, past_ckv, past_kpe, wkv0, shift0, mix_w):
    (w_in, q_norm_g, kv_norm_g, w_uq, w_ukv, mu_shift, w0, w2, a0, a2, g2,
     k_k, k_a, r_k, lnx_g, lnx_b, w_o) = mix_w
    B, S, _ = x.shape
    proj = jnp.einsum('bsd,de->bse', x, w_in)
    c_q = proj[..., :Q_LORA]
    c_kv = proj[..., Q_LORA:Q_LORA + KV_LORA]
    k_pe = proj[..., Q_LORA + KV_LORA:MLA_IN]
    pr = proj[..., MLA_IN:]
    q = jnp.einsum('bsl,lhe->bshe', rms_norm(c_q, q_norm_g), w_uq)
    ckv = rms_norm(c_kv, kv_norm_g)
    pos = pos0 + jnp.arange(S, dtype=jnp.int32)
    cos, sin = rope_tables(pos)
    q_nope = q[..., :MLA_NOPE]
    q_pe = apply_rope(q[..., MLA_NOPE:], cos[:, None, :], sin[:, None, :])
    kpe = apply_rope(k_pe, cos, sin)
    if past_ckv is None:
        attn = mla_prompt(q_nope, q_pe, ckv, kpe, w_ukv, pos)
    else:
        attn = mla_sample(q_nope, q_pe, ckv, kpe, past_ckv, past_kpe, w_ukv, pos)
    rw, wkv_fin = rwkv7_mixer(pr, wkv0, shift0, mu_shift, w0, w2, a0, a2, g2,
                              k_k, k_a, r_k, lnx_g, lnx_b)
    mixed = jnp.concatenate([attn.reshape(B, S, MLA_OUT).astype(x.dtype), rw], axis=-1)
    out = jnp.einsum('bsm,md->bsd', mixed, w_o)
    return out, ckv, kpe, wkv_fin, pr[:, -1]


def hier_moe(h, moe_w):
    w_gr, b_gr, w_er, b_er, w_eg, w_eu, w_ed = moe_w
    T, D = h.shape
    g_logits = (h @ w_gr + b_gr).astype(F32)
    g_prob = jax.nn.softmax(g_logits, -1)
    grp = jnp.argmax(g_logits, -1).astype(jnp.int32)
    p_grp = jnp.take_along_axis(g_prob, grp[:, None], axis=-1)
    e_logits = (h @ w_er + b_er).astype(F32).reshape(T, N_GROUPS, EXPERTS_PER_GROUP)
    e_in = jnp.take_along_axis(e_logits, grp[:, None, None], axis=1)[:, 0]
    top_p, top_i = lax.top_k(jax.nn.softmax(e_in, -1), TOP_K)
    gate = top_p / jnp.sum(top_p, -1, keepdims=True) * p_grp
    expert = grp[:, None] * EXPERTS_PER_GROUP + top_i.astype(jnp.int32)
    A = T * TOP_K
    flat_e = expert.reshape(A)
    order = jnp.argsort(flat_e)
    e_sorted = flat_e[order]
    tok = order // TOP_K
    counts = jnp.zeros((N_EXPERTS,), jnp.int32).at[flat_e].add(1)
    blocks_per_e = (counts + MOE_BLK - 1) // MOE_BLK
    blk_end = jnp.cumsum(blocks_per_e)
    blk_start = blk_end - blocks_per_e
    row_start = jnp.cumsum(counts) - counts
    dest = blk_start[e_sorted] * MOE_BLK + (jnp.arange(A, dtype=jnp.int32) - row_start[e_sorted])
    n_blk = -(-A // MOE_BLK) + N_EXPERTS
    block_e = jnp.minimum(jnp.searchsorted(blk_end, jnp.arange(n_blk, dtype=jnp.int32), side='right'),
                          N_EXPERTS - 1)
    xs = jnp.zeros((n_blk * MOE_BLK, D), h.dtype).at[dest].set(h[tok])

    def run_block(args):
        xb, e = args
        return (jax.nn.silu(xb @ w_eg[e]) * (xb @ w_eu[e])) @ w_ed[e]

    ys = lax.map(run_block, (xs.reshape(n_blk, MOE_BLK, D), block_e)).reshape(n_blk * MOE_BLK, D)
    y_a = ys[dest] * gate.reshape(A)[order][:, None].astype(h.dtype)
    return jnp.zeros((T, D), h.dtype).at[tok].add(y_a)


def trunk_layer(x, pos0, past_ckv, past_kpe, wkv0, shift0, mix_w, moe_w, ln1_g, ln1_b, ln2_g, ln2_b):
    m, ckv, kpe, wkv, sh = token_mixers(x, pos0, past_ckv, past_kpe, wkv0, shift0, mix_w)
    h = layer_norm(DN_ALPHA * x + m, ln1_g, ln1_b)
    B, S, D = h.shape
    f = hier_moe(h.reshape(B * S, D), moe_w).reshape(B, S, D)
    y = layer_norm(DN_ALPHA * h + f, ln2_g, ln2_b)
    return y, ckv, kpe, wkv, sh


def setup_inputs(seed: int = 0) -> dict:
    key = jax.random.key(seed)
    ks = iter(jax.random.split(key, 48))

    def nrm(shape, scale):
        return jax.random.normal(next(ks), shape, F32) * scale

    def unif(shape, lo, hi):
        return jax.random.uniform(next(ks), shape, F32, lo, hi)

    L = DEPTH
    return {
        "x_prompt": nrm((BATCH, SEQ, D_MODEL), 1.0),
        "x_sample": nrm((DEC_BATCH, DEC_SEQ, D_MODEL), 1.0),
        "cache_ckv": nrm((L, DEC_BATCH, PAST_LEN, KV_LORA), 1.0),
        "cache_kpe": nrm((L, DEC_BATCH, PAST_LEN, MLA_ROPE), 1.0),
        "state_wkv": nrm((L, DEC_BATCH, RWKV_HEADS, RWKV_N, RWKV_N), 0.3),
        "state_shift": nrm((L, DEC_BATCH, RWKV_IN), 1.0),
        "w_in": nrm((L, D_MODEL, IN_WIDTH), D_MODEL ** -0.5),
        "q_norm_g": 1.0 + nrm((L, Q_LORA), 0.05),
        "kv_norm_g": 1.0 + nrm((L, KV_LORA), 0.05),
        "w_uq": nrm((L, Q_LORA, MLA_HEADS, MLA_NOPE + MLA_ROPE), Q_LORA ** -0.5),
        "w_ukv": nrm((L, KV_LORA, MLA_HEADS, MLA_NOPE + MLA_V), KV_LORA ** -0.5),
        "mu_shift": unif((L, RWKV_IN), 0.0, 1.0),
        "w0": unif((L, RWKV_DIM), -6.0, -1.5),
        "w2": nrm((L, DECAY_LORA, RWKV_DIM), 0.5 * DECAY_LORA ** -0.5),
        "a0": nrm((L, RWKV_DIM), 0.5),
        "a2": nrm((L, AAA_LORA, RWKV_DIM), AAA_LORA ** -0.5),
        "g2": nrm((L, GATE_LORA, RWKV_DIM), GATE_LORA ** -0.5),
        "k_k": 0.85 + nrm((L, RWKV_DIM), 0.05),
        "k_a": 1.0 + nrm((L, RWKV_DIM), 0.05),
        "r_k": nrm((L, RWKV_HEADS, RWKV_N), 0.1),
        "lnx_g": 1.0 + nrm((L, RWKV_DIM), 0.05),
        "lnx_b": nrm((L, RWKV_DIM), 0.01),
        "w_o": nrm((L, MIX_WIDTH, D_MODEL), MIX_WIDTH ** -0.5 * DN_BETA),
        "ln1_g": 1.0 + nrm((L, D_MODEL), 0.05),
        "ln1_b": nrm((L, D_MODEL), 0.01),
        "w_gr": nrm((L, D_MODEL, N_GROUPS), D_MODEL ** -0.5),
        "b_gr": nrm((L, N_GROUPS), 0.01),
        "w_er": nrm((L, D_MODEL, N_EXPERTS), D_MODEL ** -0.5),
        "b_er": nrm((L, N_EXPERTS), 0.01),
        "w_eg": nrm((L, N_EXPERTS, D_MODEL, D_EXPERT), D_MODEL ** -0.5),
        "w_eu": nrm((L, N_EXPERTS, D_MODEL, D_EXPERT), D_MODEL ** -0.5),
        "w_ed": nrm((L, N_EXPERTS, D_EXPERT, D_MODEL), D_EXPERT ** -0.5 * DN_BETA),
        "ln2_g": 1.0 + nrm((L, D_MODEL), 0.05),
        "ln2_b": nrm((L, D_MODEL), 0.01),
    }


def reference(x_prompt, x_sample, cache_ckv, cache_kpe, state_wkv, state_shift,
              w_in, q_norm_g, kv_norm_g, w_uq, w_ukv, mu_shift, w0, w2, a0, a2, g2,
              k_k, k_a, r_k, lnx_g, lnx_b, w_o, ln1_g, ln1_b,
              w_gr, b_gr, w_er, b_er, w_eg, w_eu, w_ed, ln2_g, ln2_b):
    B = x_prompt.shape[0]
    past = cache_ckv.shape[2]
    hp, hs = x_prompt, x_sample
    p_ckv, p_kpe, p_wkv, p_sh = [], [], [], []
    s_ckv, s_kpe, s_wkv, s_sh = [], [], [], []
    for l in range(DEPTH):
        mix_w = (w_in[l], q_norm_g[l], kv_norm_g[l], w_uq[l], w_ukv[l], mu_shift[l], w0[l], w2[l],
                 a0[l], a2[l], g2[l], k_k[l], k_a[l], r_k[l], lnx_g[l], lnx_b[l], w_o[l])
        moe_w = (w_gr[l], b_gr[l], w_er[l], b_er[l], w_eg[l], w_eu[l], w_ed[l])
        wkv_zero = jnp.zeros((B, RWKV_HEADS, RWKV_N, RWKV_N), F32)
        sh_zero = jnp.zeros((B, RWKV_IN), hp.dtype)
        hp, c1, k1, w1_, s1 = trunk_layer(hp, 0, None, None, wkv_zero, sh_zero, mix_w, moe_w,
                                          ln1_g[l], ln1_b[l], ln2_g[l], ln2_b[l])
        hs, c2, k2, w2_, s2 = trunk_layer(hs, past, cache_ckv[l], cache_kpe[l], state_wkv[l],
                                          state_shift[l], mix_w, moe_w,
                                          ln1_g[l], ln1_b[l], ln2_g[l], ln2_b[l])
        p_ckv.append(c1); p_kpe.append(k1); p_wkv.append(w1_); p_sh.append(s1)
        s_ckv.append(c2); s_kpe.append(k2); s_wkv.append(w2_); s_sh.append(s2)
    return (hp, hs,
            jnp.stack(p_ckv), jnp.stack(p_kpe), jnp.stack(p_wkv), jnp.stack(p_sh),
            jnp.stack(s_ckv), jnp.stack(s_kpe), jnp.stack(s_wkv), jnp.stack(s_sh))
```

```python
import functools
import math

import numpy as np
import jax
import jax.numpy as jnp
from jax import lax
from jax.experimental import pallas as pl
from jax.experimental.pallas import tpu as pltpu

F32 = jnp.float32
BF16 = jnp.bfloat16

D_MODEL = 1024
CHUNK = 64
MLA_HEADS = 8
MLA_NOPE = 64
MLA_ROPE = 32
MLA_V = 64
Q_LORA = 384
KV_LORA = 256
ROPE_BASE = 10000.0
MLA_IN = Q_LORA + KV_LORA + MLA_ROPE
MLA_SCALE = (MLA_NOPE + MLA_ROPE) ** -0.5
RWKV_HEADS = 8
RWKV_N = 64
RWKV_DIM = RWKV_HEADS * RWKV_N
DECAY_LORA = 64
AAA_LORA = 64
GATE_LORA = 128
RWKV_IN = 3 * RWKV_DIM + DECAY_LORA + AAA_LORA + GATE_LORA
N_GROUPS = 4
EXPERTS_PER_GROUP = 8
N_EXPERTS = N_GROUPS * EXPERTS_PER_GROUP
TOP_K = 2
D_EXPERT = 256
MOE_BLK = 256
LN_EPS = 1e-5
RMS_EPS = 1e-6
GN_EPS = 64e-5
NEG_INF = -1e30
DEPTH = 1
DN_ALPHA = (2 * DEPTH) ** 0.25

LANE = 128
HEAD_PAD = 128
PROJ_W = 768 + RWKV_IN
SHIFT_GROUP = 32
TOKEN_TILE = 512
ATTN_TILE = 512
VMEM_LIMIT = 48 * 1024 * 1024


def _cparams(sem):
    return pltpu.CompilerParams(dimension_semantics=sem, vmem_limit_bytes=VMEM_LIMIT)


def _split3(x):
    hi = x.astype(BF16)
    r1 = x - hi.astype(F32)
    mid = r1.astype(BF16)
    lo = (r1 - mid.astype(F32)).astype(BF16)
    return hi, mid, lo


def _dot(a, b):
    return jnp.dot(a, b, preferred_element_type=F32)


def _dot_nt(a, b):
    return lax.dot_general(a, b, (((1,), (1,)), ((), ())), preferred_element_type=F32)


def _dot_exact_rhs(x, w):
    hi, mid, lo = _split3(x)
    return _dot(hi, w) + _dot(mid, w) + _dot(lo, w)


def _dot_exact_lhs(w, x):
    hi, mid, lo = _split3(x)
    return _dot(w, hi) + _dot(w, mid) + _dot(w, lo)


def _sigmoid(x):
    return 1.0 / (1.0 + jnp.exp(-x))


def _layer_norm(x, g, b):
    xc = x - jnp.mean(x, -1, keepdims=True)
    var = jnp.mean(xc * xc, -1, keepdims=True)
    return xc * lax.rsqrt(var + LN_EPS) * g + b


def _proj_kernel(x_ref, rope_ref, w1_ref, gq_ref, gkv_ref, wqa_ref, wqb_ref, wk_ref, pk_ref, wv_ref,
                 q_ref, k_ref, v_ref, ckv_ref, kpe_ref, pr_ref):
    x = x_ref[...].astype(BF16)
    proj = _dot(x, w1_ref[...])
    c_q = proj[:, :Q_LORA]
    c_kv = proj[:, Q_LORA:Q_LORA + KV_LORA]
    kp = proj[:, 640:768]
    pr_ref[...] = proj[:, 768:]

    cqn = c_q * lax.rsqrt(jnp.mean(c_q * c_q, -1, keepdims=True) + RMS_EPS) * gq_ref[...]
    ckv = c_kv * lax.rsqrt(jnp.mean(c_kv * c_kv, -1, keepdims=True) + RMS_EPS) * gkv_ref[...]
    ckv_ref[...] = ckv

    rope = rope_ref[...]
    cq = rope[:, :LANE]
    sq = rope[:, LANE:2 * LANE]
    kt = rope[:, 2 * LANE:]
    prod = kp * kt
    kpe = prod[:, :MLA_ROPE] + prod[:, MLA_ROPE:2 * MLA_ROPE]
    kpe_ref[...] = kpe

    cqb = cqn.astype(BF16)
    qa = _dot(cqb, wqa_ref[...])
    qb = _dot(cqb, wqb_ref[...])
    for h in range(MLA_HEADS):
        sl = slice(h * HEAD_PAD, (h + 1) * HEAD_PAD)
        q_ref[:, sl] = (qa[:, sl] * cq + qb[:, sl] * sq).astype(BF16)

    ckv_b = ckv.astype(BF16)
    k = _dot(ckv_b, wk_ref[...]) + _dot(kpe.astype(BF16), pk_ref[...])
    k_ref[...] = k.astype(BF16)
    v_ref[...] = _dot(ckv_b, wv_ref[...]).astype(BF16)


def _proj_call(x, rope, w1, gq, gkv, wqa, wqb, wk, pk, wv):
    T = x.shape[0]
    tm = TOKEN_TILE
    row = lambda i: (i, 0)
    full = lambda i: (0, 0)
    wide = MLA_HEADS * HEAD_PAD
    return pl.pallas_call(
        _proj_kernel,
        grid=(T // tm,),
        in_specs=[
            pl.BlockSpec((tm, D_MODEL), row),
            pl.BlockSpec((tm, 3 * LANE), row),
            pl.BlockSpec((D_MODEL, PROJ_W), full),
            pl.BlockSpec((1, Q_LORA), full),
            pl.BlockSpec((1, KV_LORA), full),
            pl.BlockSpec((Q_LORA, wide), full),
            pl.BlockSpec((Q_LORA, wide), full),
            pl.BlockSpec((KV_LORA, wide), full),
            pl.BlockSpec((MLA_ROPE, wide), full),
            pl.BlockSpec((KV_LORA, wide), full),
        ],
        out_specs=[
            pl.BlockSpec((tm, wide), row),
            pl.BlockSpec((tm, wide), row),
            pl.BlockSpec((tm, wide), row),
            pl.BlockSpec((tm, KV_LORA), row),
            pl.BlockSpec((tm, MLA_ROPE), row),
            pl.BlockSpec((tm, RWKV_IN), row),
        ],
        out_shape=[
            jax.ShapeDtypeStruct((T, wide), BF16),
            jax.ShapeDtypeStruct((T, wide), BF16),
            jax.ShapeDtypeStruct((T, wide), BF16),
            jax.ShapeDtypeStruct((T, KV_LORA), F32),
            jax.ShapeDtypeStruct((T, MLA_ROPE), F32),
            jax.ShapeDtypeStruct((T, RWKV_IN), F32),
        ],
        compiler_params=_cparams(("parallel",)),
        name="proj",
    )(x, rope, w1, gq, gkv, wqa, wqb, wk, pk, wv)


def _attn_kernel(qi_tab, ki_tab, q_ref, k_ref, v_ref, o_ref, m_scr, l_scr, acc_scr, *, tile):
    p = pl.program_id(2)
    qi = qi_tab[p]
    ki = ki_tab[p]

    @pl.when(ki == 0)
    def _():
        m_scr[...] = jnp.full(m_scr.shape, NEG_INF, F32)
        l_scr[...] = jnp.zeros(l_scr.shape, F32)
        acc_scr[...] = jnp.zeros(acc_scr.shape, F32)

    def step(masked):
        if masked:
            r = lax.broadcasted_iota(jnp.int32, (tile, tile), 0) // CHUNK
            c = lax.broadcasted_iota(jnp.int32, (tile, tile), 1) // CHUNK
            visible = c <= r
        for h in range(2):
            sl = slice(h * HEAD_PAD, (h + 1) * HEAD_PAD)
            s = _dot_nt(q_ref[:, sl], k_ref[:, sl])
            if masked:
                s = jnp.where(visible, s, NEG_INF)
            m_prev = m_scr[h]
            m_new = jnp.maximum(m_prev, jnp.max(s, axis=1, keepdims=True))
            alpha = jnp.exp(m_prev - m_new)
            pexp = jnp.exp(s - m_new)
            l_scr[h] = alpha * l_scr[h] + jnp.sum(pexp, axis=1, keepdims=True)
            acc_scr[h] = alpha * acc_scr[h] + _dot(pexp.astype(BF16), v_ref[:, sl])
            m_scr[h] = m_new

    @pl.when(ki < qi)
    def _():
        step(False)

    @pl.when(ki == qi)
    def _():
        step(True)
        o = acc_scr[0] / l_scr[0] + acc_scr[1] / l_scr[1]
        o_ref[...] = o.astype(o_ref.dtype)


def _attn_call(q, k, v, n_batch, seq, t_total):
    tile = ATTN_TILE
    nq = seq // tile
    qi_np, ki_np = [], []
    for i in range(nq):
        for j in range(i + 1):
            qi_np.append(i)
            ki_np.append(j)
    qi_tab = jnp.asarray(np.array(qi_np, np.int32))
    ki_tab = jnp.asarray(np.array(ki_np, np.int32))
    n_pairs = len(qi_np)
    hp = MLA_HEADS // 2
    grid_spec = pltpu.PrefetchScalarGridSpec(
        num_scalar_prefetch=2,
        grid=(n_batch, hp, n_pairs),
        in_specs=[
            pl.BlockSpec((tile, 2 * HEAD_PAD), lambda b, h, p, qt, kt: (b * nq + qt[p], h)),
            pl.BlockSpec((tile, 2 * HEAD_PAD), lambda b, h, p, qt, kt: (b * nq + kt[p], h)),
            pl.BlockSpec((tile, 2 * HEAD_PAD), lambda b, h, p, qt, kt: (b * nq + kt[p], h)),
        ],
        out_specs=pl.BlockSpec((tile, LANE), lambda b, h, p, qt, kt: (b * nq + qt[p], h)),
        scratch_shapes=[
            pltpu.VMEM((2, tile, 1), F32),
            pltpu.VMEM((2, tile, 1), F32),
            pltpu.VMEM((2, tile, LANE), F32),
        ],
    )
    return pl.pallas_call(
        functools.partial(_attn_kernel, tile=tile),
        grid_spec=grid_spec,
        out_shape=jax.ShapeDtypeStruct((t_total, hp * LANE), BF16),
        compiler_params=_cparams(("parallel", "parallel", "arbitrary")),
        name="attn",
    )(qi_tab, ki_tab, q, k, v)


def _mla_sample_kernel(q_ref, cn_ref, kn_ref, cp_ref, kp_ref, wuk_ref, wuv_ref, attn_in_ref, o_ref,
                       *, past, dec):
    del attn_in_ref
    cp = cp_ref[0].astype(BF16)
    kp = kp_ref[0].astype(BF16)
    cn = cn_ref[...].astype(BF16)
    kn = kn_ref[...].astype(BF16)
    qpos = (past + lax.broadcasted_iota(jnp.int32, (dec, past), 0)) // CHUNK
    vis_p = (lax.broadcasted_iota(jnp.int32, (dec, past), 1) // CHUNK) <= qpos
    qpos_n = (past + lax.broadcasted_iota(jnp.int32, (dec, dec), 0)) // CHUNK
    vis_n = ((past + lax.broadcasted_iota(jnp.int32, (dec, dec), 1)) // CHUNK) <= qpos_n
    out = jnp.zeros((dec, MLA_HEADS * MLA_V), F32)
    for h in range(MLA_HEADS):
        qn = q_ref[:, h * HEAD_PAD:h * HEAD_PAD + MLA_NOPE]
        qp = q_ref[:, h * HEAD_PAD + MLA_NOPE:h * HEAD_PAD + MLA_NOPE + MLA_ROPE]
        q_lat = _dot(qn, wuk_ref[h]).astype(BF16)
        s_p = jnp.where(vis_p, _dot_nt(q_lat, cp) + _dot_nt(qp, kp), NEG_INF)
        s_n = jnp.where(vis_n, _dot_nt(q_lat, cn) + _dot_nt(qp, kn), NEG_INF)
        m = jnp.maximum(jnp.max(s_p, axis=1, keepdims=True), jnp.max(s_n, axis=1, keepdims=True))
        e_p = jnp.exp(s_p - m)
        e_n = jnp.exp(s_n - m)
        l = jnp.sum(e_p, axis=1, keepdims=True) + jnp.sum(e_n, axis=1, keepdims=True)
        o_lat = (_dot(e_p.astype(BF16), cp) + _dot(e_n.astype(BF16), cn)) / l
        out = out + _dot(o_lat.astype(BF16), wuv_ref[h])
    o_ref[...] = out.astype(o_ref.dtype)


def _mla_sample_call(q, ckv, kpe, cache_ckv, cache_kpe, wuk, wuv, attn, row0, n_seq, dec):
    past = cache_ckv.shape[1]
    blk0 = row0 // dec
    wide = MLA_HEADS * HEAD_PAD
    return pl.pallas_call(
        functools.partial(_mla_sample_kernel, past=past, dec=dec),
        grid=(n_seq,),
        in_specs=[
            pl.BlockSpec((dec, wide), lambda b: (blk0 + b, 0)),
            pl.BlockSpec((dec, KV_LORA), lambda b: (blk0 + b, 0)),
            pl.BlockSpec((dec, MLA_ROPE), lambda b: (blk0 + b, 0)),
            pl.BlockSpec((1, past, KV_LORA), lambda b: (b, 0, 0)),
            pl.BlockSpec((1, past, MLA_ROPE), lambda b: (b, 0, 0)),
            pl.BlockSpec((MLA_HEADS, MLA_NOPE, KV_LORA), lambda b: (0, 0, 0)),
            pl.BlockSpec((MLA_HEADS, KV_LORA, MLA_HEADS * MLA_V), lambda b: (0, 0, 0)),
            pl.BlockSpec(memory_space=pl.ANY),
        ],
        out_specs=pl.BlockSpec((dec, MLA_HEADS * MLA_V), lambda b: (blk0 + b, 0)),
        out_shape=jax.ShapeDtypeStruct(attn.shape, attn.dtype),
        input_output_aliases={7: 0},
        compiler_params=_cparams(("parallel",)),
        name="mla_sample",
    )(q, ckv, kpe, cache_ckv, cache_kpe, wuk, wuv, attn)


def _prep_kernel(pr_ref, bnd_ref, mu_ref, w0_ref, a0_ref, kk_ref, ka_ref, rk_ref, w2_ref, a2_ref, g2_ref,
                 seg_ref, r_ref, lw_ref, kh_ref, v_ref, na_ref, b_ref, bonus_ref, g_ref, *, tm):
    ng = tm // SHIFT_GROUP
    pr = pr_ref[...]
    pr3 = pr.reshape(ng, SHIFT_GROUP, RWKV_IN)
    rolled = pltpu.roll(pr3, 1, 1)
    first = lax.broadcasted_iota(jnp.int32, pr3.shape, 1) == 0
    prev = jnp.where(first, bnd_ref[...], rolled).reshape(tm, RWKV_IN)
    u = pr + mu_ref[...] * (prev - pr)
    o1, o2, o3 = RWKV_DIM, 2 * RWKV_DIM, 3 * RWKV_DIM
    o4, o5 = o3 + DECAY_LORA, o3 + DECAY_LORA + AAA_LORA
    r, k, v = u[:, :o1], u[:, o1:o2], u[:, o2:o3]
    w_lo, a_lo, g_lo = u[:, o3:o4], u[:, o4:o5], u[:, o5:]
    wl = w0_ref[...] + _dot(jnp.tanh(w_lo).astype(BF16), w2_ref[...])
    lw_ref[...] = -math.exp(-0.5) * _sigmoid(wl)
    a = _sigmoid(a0_ref[...] + _dot(a_lo.astype(BF16), a2_ref[...]))
    g_ref[...] = _dot(_sigmoid(g_lo).astype(BF16), g2_ref[...])
    seg = seg_ref[...]
    kk = k * kk_ref[...]
    kk = kk / jnp.maximum(jnp.sqrt(_dot_exact_rhs(kk * kk, seg)), 1e-12)
    kh = k * (1.0 + (a - 1.0) * ka_ref[...])
    r_ref[...] = r
    kh_ref[...] = kh
    v_ref[...] = v
    na_ref[...] = -kk
    b_ref[...] = kk * a
    bonus_ref[...] = _dot_exact_rhs(r * kh * rk_ref[...], seg) * v


def _prep_call(pr, bnd, mu, w0, a0, k_k, k_a, r_k, w2, a2, g2, seg):
    T = pr.shape[0]
    tm = TOKEN_TILE
    row = lambda i: (i, 0)
    full = lambda i: (0, 0)
    vec = pl.BlockSpec((1, RWKV_DIM), full)
    out = pl.BlockSpec((tm, RWKV_DIM), row)
    return pl.pallas_call(
        functools.partial(_prep_kernel, tm=tm),
        grid=(T // tm,),
        in_specs=[
            pl.BlockSpec((tm, RWKV_IN), row),
            pl.BlockSpec((tm // SHIFT_GROUP, 1, RWKV_IN), lambda i: (i, 0, 0)),
            pl.BlockSpec((1, RWKV_IN), full),
            vec, vec, vec, vec, vec,
            pl.BlockSpec((DECAY_LORA, RWKV_DIM), full),
            pl.BlockSpec((AAA_LORA, RWKV_DIM), full),
            pl.BlockSpec((GATE_LORA, RWKV_DIM), full),
            pl.BlockSpec((RWKV_DIM, RWKV_DIM), full),
        ],
        out_specs=[out] * 8,
        out_shape=[jax.ShapeDtypeStruct((T, RWKV_DIM), F32)] * 8,
        compiler_params=_cparams(("parallel",)),
        name="rwkv_prep",
    )(pr, bnd, mu, w0, a0, k_k, k_a, r_k, w2, a2, g2, seg)


def _wkv_kernel(r_ref, lw_ref, k_ref, v_ref, a_ref, b_ref, h0_ref, *rest, C, aliased):
    if aliased:
        _, y_ref, hT_ref, h_scr = rest
    else:
        y_ref, hT_ref, h_scr = rest
    c = pl.program_id(1)
    N = RWKV_N

    @pl.when(c == 0)
    def _():
        h_scr[...] = h0_ref[0]

    row = lax.broadcasted_iota(jnp.int32, (C, C), 0)
    col = lax.broadcasted_iota(jnp.int32, (C, C), 1)
    lower = col <= row
    strict = col < row
    tri = jnp.where(lower, 1.0, 0.0).astype(BF16)
    eye_n = (lax.broadcasted_iota(jnp.int32, (N, N), 0) == lax.broadcasted_iota(jnp.int32, (N, N), 1))
    eye_c = jnp.where(col == row, 1.0, 0.0)

    lw = lw_ref[...]
    cum = _dot_exact_lhs(tri, lw)
    cum_end = cum[C - 1:C, :]
    e_neg = jnp.exp(-cum)
    e_end = jnp.exp(cum_end - cum)
    a_t = (a_ref[...] * jnp.exp(cum - lw)).astype(BF16)
    r_t = (r_ref[...] * jnp.exp(cum)).astype(BF16)
    b_in = b_ref[...]
    k_in = k_ref[...]
    b_t = (b_in * e_neg).astype(BF16)
    k_t = (k_in * e_neg).astype(BF16)
    b_e = (b_in * e_end).astype(BF16)
    k_e = (k_in * e_end).astype(BF16)
    v_b = v_ref[...].astype(BF16)
    p_end = jnp.exp(cum_end)
    eye_bf = jnp.where(eye_n, 1.0, 0.0).astype(BF16)

    n_lev = int(round(math.log2(C))) - 1
    for h in range(RWKV_HEADS):
        sl = slice(h * N, (h + 1) * N)
        a_h, r_h, v_h = a_t[:, sl], r_t[:, sl], v_b[:, sl]
        ar = jnp.concatenate([a_h, r_h], axis=0)
        m_b = _dot_nt(ar, b_t[:, sl])
        m_k = _dot_nt(ar, k_t[:, sl])
        l_ab = jnp.where(strict, m_b[:C], 0.0)
        a_ak = jnp.where(strict, m_k[:C], 0.0).astype(BF16)
        a_rb = jnp.where(lower, m_b[C:], 0.0).astype(BF16)
        a_rk = jnp.where(lower, m_k[C:], 0.0).astype(BF16)
        t_inv = eye_c + l_ab
        l_pow = l_ab.astype(BF16)
        for _ in range(n_lev):
            l_sq = _dot(l_pow, l_pow)
            l_pow = l_sq.astype(BF16)
            t_inv = t_inv + _dot(t_inv.astype(BF16), l_pow)
        h0 = h_scr[h]
        h0_b = h0.astype(BF16)
        x = _dot(a_h, h0_b) + _dot(a_ak, v_h)
        u = _dot(t_inv.astype(BF16), x.astype(BF16)).astype(BF16)
        y = _dot(r_h, h0_b) + _dot(a_rb, u) + _dot(a_rk, v_h)
        y_ref[:, sl] = y
        b_eT = _dot_nt(eye_bf, b_e[:, sl]).astype(BF16)
        k_eT = _dot_nt(eye_bf, k_e[:, sl]).astype(BF16)
        p_col = jnp.sum(jnp.where(eye_n, p_end[:, sl], 0.0), axis=1, keepdims=True)
        h_new = p_col * h0 + _dot(b_eT, u) + _dot(k_eT, v_h)
        h_scr[h] = h_new

    @pl.when(c == pl.num_programs(1) - 1)
    def _():
        hT_ref[0] = h_scr[...]


def _wkv_call(arrs, h0, y_prev, row0, n_seq, n_chunk, C, t_total):
    blk0 = row0 // C
    tok = pl.BlockSpec((C, RWKV_DIM), lambda b, c: (blk0 + b * n_chunk + c, 0))
    st = pl.BlockSpec((1, RWKV_HEADS, RWKV_N, RWKV_N), lambda b, c: (b, 0, 0, 0))
    in_specs = [tok] * 6 + [st]
    args = list(arrs) + [h0]
    aliases = {}
    if y_prev is not None:
        in_specs.append(pl.BlockSpec(memory_space=pl.ANY))
        args.append(y_prev)
        aliases = {7: 0}
    return pl.pallas_call(
        functools.partial(_wkv_kernel, C=C, aliased=y_prev is not None),
        grid=(n_seq, n_chunk),
        in_specs=in_specs,
        out_specs=[tok, st],
        out_shape=[
            jax.ShapeDtypeStruct((t_total, RWKV_DIM), F32),
            jax.ShapeDtypeStruct((n_seq, RWKV_HEADS, RWKV_N, RWKV_N), F32),
        ],
        scratch_shapes=[pltpu.VMEM((RWKV_HEADS, RWKV_N, RWKV_N), F32)],
        input_output_aliases=aliases,
        compiler_params=_cparams(("parallel", "arbitrary")),
        name="wkv_c%d" % C,
    )(*args)


def _mix_kernel(x_ref, attn_ref, y_ref, bonus_ref, g_ref, seg_ref, lng_ref, lnb_ref, woa_ref, wob_ref,
                g1_ref, b1_ref, wr_ref, br_ref, h_ref, hb_ref, route_ref):
    seg = seg_ref[...]
    y = y_ref[...]
    inv_n = 1.0 / RWKV_N
    yc = y - _dot_exact_rhs(y, seg) * inv_n
    var = _dot_exact_rhs(yc * yc, seg) * inv_n
    yn = yc * lax.rsqrt(var + GN_EPS) * lng_ref[...] + lnb_ref[...]
    rw = ((yn + bonus_ref[...]) * g_ref[...]).astype(BF16)
    m = _dot(attn_ref[...], woa_ref[...]) + _dot(rw, wob_ref[...])
    h = _layer_norm(DN_ALPHA * x_ref[...] + m, g1_ref[...], b1_ref[...])
    h_ref[...] = h
    hb_ref[...] = h.astype(BF16)

    h_hi = h.astype(BF16)
    h_lo = (h - h_hi.astype(F32)).astype(BF16)
    logits = _dot(h_hi, wr_ref[0]) + _dot(h_lo, wr_ref[0]) + _dot(h_hi, wr_ref[1]) + br_ref[...]
    lane = lax.broadcasted_iota(jnp.int32, logits.shape, 1)
    big = jnp.int32(LANE)
    gl = jnp.where(lane < N_GROUPS, logits, NEG_INF)
    gmax = jnp.max(gl, axis=1, keepdims=True)
    grp = jnp.min(jnp.where(gl == gmax, lane, big), axis=1, keepdims=True)
    p_grp = 1.0 / jnp.sum(jnp.exp(gl - gmax), axis=1, keepdims=True)
    e_idx = lane - N_GROUPS
    in_grp = (lane >= N_GROUPS) & (lane < N_GROUPS + N_EXPERTS) & ((e_idx // EXPERTS_PER_GROUP) == grp)
    el = jnp.where(in_grp, logits, NEG_INF)
    m1 = jnp.max(el, axis=1, keepdims=True)
    i1 = jnp.min(jnp.where(el == m1, lane, big), axis=1, keepdims=True)
    el2 = jnp.where(lane == i1, NEG_INF, el)
    m2 = jnp.max(el2, axis=1, keepdims=True)
    i2 = jnp.min(jnp.where(el2 == m2, lane, big), axis=1, keepdims=True)
    t = jnp.exp(m2 - m1)
    g1 = p_grp / (1.0 + t)
    g2 = g1 * t
    e1 = (i1 - N_GROUPS).astype(F32)
    e2 = (i2 - N_GROUPS).astype(F32)
    route_ref[...] = jnp.where(lane == 0, e1, jnp.where(lane == 1, e2, jnp.where(lane == 2, g1,
                               jnp.where(lane == 3, g2, 0.0))))


def _mix_call(x, attn, y, bonus, g, seg, lnx_g, lnx_b, woa, wob, ln1_g, ln1_b, wr, br):
    T = x.shape[0]
    tm = TOKEN_TILE
    row = lambda i: (i, 0)
    full = lambda i: (0, 0)
    half = pl.BlockSpec((tm, RWKV_DIM), row)
    vec5 = pl.BlockSpec((1, RWKV_DIM), full)
    vec10 = pl.BlockSpec((1, D_MODEL), full)
    return pl.pallas_call(
        _mix_kernel,
        grid=(T // tm,),
        in_specs=[
            pl.BlockSpec((tm, D_MODEL), row), half, half, half, half,
            pl.BlockSpec((RWKV_DIM, RWKV_DIM), full), vec5, vec5,
            pl.BlockSpec((RWKV_DIM, D_MODEL), full), pl.BlockSpec((RWKV_DIM, D_MODEL), full),
            vec10, vec10,
            pl.BlockSpec((2, D_MODEL, LANE), lambda i: (0, 0, 0)), pl.BlockSpec((1, LANE), full),
        ],
        out_specs=[pl.BlockSpec((tm, D_MODEL), row), pl.BlockSpec((tm, D_MODEL), row),
                   pl.BlockSpec((tm, LANE), row)],
        out_shape=[jax.ShapeDtypeStruct((T, D_MODEL), F32), jax.ShapeDtypeStruct((T, D_MODEL), BF16),
                   jax.ShapeDtypeStruct((T, LANE), F32)],
        compiler_params=_cparams(("parallel",)),
        name="mix",
    )(x, attn, y, bonus, g, seg, lnx_g, lnx_b, woa, wob, ln1_g, ln1_b, wr, br)


def _expert_kernel(be_ref, nu_ref, xs_ref, wgu_ref, wd_ref, ys_ref):
    i = pl.program_id(0)

    @pl.when(i < nu_ref[0])
    def _():
        gu = _dot(xs_ref[...], wgu_ref[0])
        gate, up = gu[:, :D_EXPERT], gu[:, D_EXPERT:]
        act = (gate * _sigmoid(gate) * up).astype(BF16)
        ys_ref[...] = _dot(act, wd_ref[0])

    @pl.when(i >= nu_ref[0])
    def _():
        ys_ref[...] = jnp.zeros(ys_ref.shape, ys_ref.dtype)


def _expert_call(block_e, n_used, xs, wgu, wd):
    n_blk = xs.shape[0] // MOE_BLK
    grid_spec = pltpu.PrefetchScalarGridSpec(
        num_scalar_prefetch=2,
        grid=(n_blk,),
        in_specs=[
            pl.BlockSpec((MOE_BLK, D_MODEL), lambda i, be, nu: (i, 0)),
            pl.BlockSpec((1, D_MODEL, 2 * D_EXPERT), lambda i, be, nu: (be[i], 0, 0)),
            pl.BlockSpec((1, D_EXPERT, D_MODEL), lambda i, be, nu: (be[i], 0, 0)),
        ],
        out_specs=pl.BlockSpec((MOE_BLK, D_MODEL), lambda i, be, nu: (i, 0)),
    )
    return pl.pallas_call(
        _expert_kernel,
        grid_spec=grid_spec,
        out_shape=jax.ShapeDtypeStruct((n_blk * MOE_BLK, D_MODEL), F32),
        compiler_params=_cparams(("arbitrary",)),
        name="experts",
    )(block_e, n_used, xs, wgu, wd)


def _combine_kernel(h_ref, ya_ref, yb_ref, route_ref, g2_ref, b2_ref, o_ref):
    route = route_ref[...]
    f = ya_ref[...] * route[:, 2:3] + yb_ref[...] * route[:, 3:4]
    o_ref[...] = _layer_norm(DN_ALPHA * h_ref[...] + f, g2_ref[...], b2_ref[...])


def _combine_call(h, ya, yb, route, ln2_g, ln2_b):
    T = h.shape[0]
    tm = TOKEN_TILE
    row = lambda i: (i, 0)
    full = lambda i: (0, 0)
    big = pl.BlockSpec((tm, D_MODEL), row)
    return pl.pallas_call(
        _combine_kernel,
        grid=(T // tm,),
        in_specs=[big, big, big, pl.BlockSpec((tm, LANE), row),
                  pl.BlockSpec((1, D_MODEL), full), pl.BlockSpec((1, D_MODEL), full)],
        out_specs=big,
        out_shape=jax.ShapeDtypeStruct((T, D_MODEL), F32),
        compiler_params=_cparams(("parallel",)),
        name="combine",
    )(h, ya, yb, route, ln2_g, ln2_b)


def _prep_weights(w_in, w_uq, w_ukv):
    half = MLA_ROPE // 2
    kpe_w = w_in[:, Q_LORA + KV_LORA:MLA_IN]
    kpe_b = jnp.concatenate([-kpe_w[:, half:], kpe_w[:, :half]], axis=1)
    w1 = jnp.concatenate([w_in[:, :Q_LORA + KV_LORA], kpe_w, kpe_b,
                          jnp.zeros((D_MODEL, 64), F32), w_in[:, MLA_IN:]], axis=1).astype(BF16)
    pad_q = jnp.zeros((Q_LORA, MLA_HEADS, HEAD_PAD - MLA_NOPE - MLA_ROPE), F32)
    wqa = jnp.concatenate([w_uq, pad_q], axis=2).reshape(Q_LORA, -1).astype(BF16)
    rot = jnp.concatenate([jnp.zeros((Q_LORA, MLA_HEADS, MLA_NOPE), F32),
                           -w_uq[:, :, MLA_NOPE + half:], w_uq[:, :, MLA_NOPE:MLA_NOPE + half], pad_q], axis=2)
    wqb = rot.reshape(Q_LORA, -1).astype(BF16)
    w_uk, w_uv = w_ukv[:, :, :MLA_NOPE], w_ukv[:, :, MLA_NOPE:]
    wk = jnp.concatenate([w_uk, jnp.zeros((KV_LORA, MLA_HEADS, HEAD_PAD - MLA_NOPE), F32)], axis=2)
    wk = wk.reshape(KV_LORA, -1).astype(BF16)
    pk_np = np.zeros((MLA_ROPE, MLA_HEADS * HEAD_PAD), np.float32)
    for h in range(MLA_HEADS):
        for i in range(MLA_ROPE):
            pk_np[i, h * HEAD_PAD + MLA_NOPE + i] = 1.0
    pk = jnp.asarray(pk_np).astype(BF16)
    zv = jnp.zeros((KV_LORA, MLA_HEADS // 2, MLA_V), F32)
    wv4 = w_uv.reshape(KV_LORA, MLA_HEADS // 2, 2, MLA_V)
    wv = jnp.stack([jnp.concatenate([wv4[:, :, 0], zv], axis=2),
                    jnp.concatenate([zv, wv4[:, :, 1]], axis=2)], axis=2)
    wv = wv.reshape(KV_LORA, -1).astype(BF16)
    wuk = jnp.transpose(w_uk, (1, 2, 0)).astype(BF16)
    wuv_np = np.zeros((MLA_HEADS, MLA_HEADS * MLA_V), np.float32)
    for h in range(MLA_HEADS):
        wuv_np[h, h * MLA_V:(h + 1) * MLA_V] = 1.0
    wuv = jnp.transpose(w_uv, (1, 0, 2))
    wuv = (jnp.tile(wuv, (1, 1, MLA_HEADS)) * jnp.asarray(wuv_np)[:, None, :]).astype(BF16)
    return w1, wqa, wqb, wk, pk, wv, wuk, wuv


def _rope_table(pos):
    inv = ROPE_BASE ** (-jnp.arange(0, MLA_ROPE, 2, dtype=F32) / MLA_ROPE)
    ang = pos.astype(F32)[:, None] * inv[None, :]
    cos, sin = jnp.cos(ang), jnp.sin(ang)
    n = pos.shape[0]
    one = jnp.ones((n, MLA_NOPE), F32)
    z32 = jnp.zeros((n, HEAD_PAD - MLA_NOPE - MLA_ROPE), F32)
    z64 = jnp.zeros((n, MLA_NOPE), F32)
    cq = jnp.concatenate([one, cos, cos, z32], axis=1) * MLA_SCALE
    sq = jnp.concatenate([z64, sin, sin, z32], axis=1) * MLA_SCALE
    kt = jnp.concatenate([cos, cos, sin, sin, z64], axis=1)
    return jnp.concatenate([cq, sq, kt], axis=1)


def _seg_ones():
    idx = np.arange(RWKV_DIM) // RWKV_N
    return jnp.asarray((idx[:, None] == idx[None, :]).astype(np.float32)).astype(BF16)


def _dispatch(route, t_total):
    A = t_total * TOP_K
    flat_e = route[:, :TOP_K].astype(jnp.int32).reshape(A)
    onehot = (flat_e[:, None] == jnp.arange(N_EXPERTS, dtype=jnp.int32)[None, :]).astype(jnp.int32)
    csum = jnp.cumsum(onehot, axis=0)
    counts = csum[-1]
    rank = jnp.take_along_axis(csum, flat_e[:, None], axis=1)[:, 0] - 1
    blocks_per_e = (counts + MOE_BLK - 1) // MOE_BLK
    blk_end = jnp.cumsum(blocks_per_e)
    blk_start = blk_end - blocks_per_e
    dest = blk_start[flat_e] * MOE_BLK + rank
    n_blk = -(-A // MOE_BLK) + N_EXPERTS
    block_e = jnp.minimum(jnp.searchsorted(blk_end, jnp.arange(n_blk, dtype=jnp.int32), side='right'),
                          N_EXPERTS - 1).astype(jnp.int32)
    n_used = blk_end[-1:].astype(jnp.int32)
    slot_tok = jnp.zeros((n_blk * MOE_BLK,), jnp.int32).at[dest].set(
        jnp.arange(A, dtype=jnp.int32) // TOP_K)
    return dest.reshape(t_total, TOP_K), block_e, n_used, slot_tok


def kernel(x_prompt, x_sample, cache_ckv, cache_kpe, state_wkv, state_shift, w_in, q_norm_g, kv_norm_g, w_uq,
           w_ukv, mu_shift, w0, w2, a0, a2, g2, k_k, k_a, r_k, lnx_g, lnx_b, w_o, ln1_g, ln1_b, w_gr, b_gr,
           w_er, b_er, w_eg, w_eu, w_ed, ln2_g, ln2_b):
    B, S, D = x_prompt.shape
    DB, DS, _ = x_sample.shape
    past = cache_ckv.shape[2]
    Tp, Ts = B * S, DB * DS
    T = Tp + Ts
    assert D == D_MODEL and DS == SHIFT_GROUP and S % ATTN_TILE == 0 and S % CHUNK == 0
    assert Tp % TOKEN_TILE == 0 and T % TOKEN_TILE == 0 and w_in.shape[0] == DEPTH

    l = 0
    x = jnp.concatenate([x_prompt.reshape(Tp, D), x_sample.reshape(Ts, D)], axis=0)
    w1, wqa, wqb, wk, pk, wv, wuk, wuv = _prep_weights(w_in[l], w_uq[l], w_ukv[l])
    pos = jnp.concatenate([jnp.tile(jnp.arange(S, dtype=jnp.int32), B),
                           jnp.tile(past + jnp.arange(DS, dtype=jnp.int32), DB)])
    rope = _rope_table(pos)

    q, kcat, vcat, ckv, kpe, pr = _proj_call(x, rope, w1, q_norm_g[l][None], kv_norm_g[l][None],
                                             wqa, wqb, wk, pk, wv)

    attn = _attn_call(q, kcat, vcat, B, S, T)
    attn = _mla_sample_call(q, ckv, kpe, cache_ckv[l], cache_kpe[l], wuk, wuv, attn, Tp, DB, DS)

    last_rows = pr[SHIFT_GROUP - 1::SHIFT_GROUP]
    bnd = jnp.concatenate([jnp.zeros((1, RWKV_IN), F32), last_rows[:-1]], axis=0)
    gidx = jnp.arange(T // SHIFT_GROUP)
    seq_start = (gidx < Tp // SHIFT_GROUP) & (gidx % (S // SHIFT_GROUP) == 0)
    bnd = jnp.where(seq_start[:, None], 0.0, bnd)
    bnd = jnp.concatenate([bnd[:Tp // SHIFT_GROUP], state_shift[l]], axis=0)[:, None, :]

    seg = _seg_ones()
    vec = lambda a: a.reshape(1, -1)
    r, lw, kh, v, na, b, bonus, g = _prep_call(
        pr, bnd, vec(mu_shift[l]), vec(w0[l]), vec(a0[l]), vec(k_k[l]), vec(k_a[l]), vec(r_k[l]),
        w2[l].astype(BF16), a2[l].astype(BF16), g2[l].astype(BF16), seg)

    scan_in = (r, lw, kh, v, na, b)
    h0_p = jnp.zeros((B, RWKV_HEADS, RWKV_N, RWKV_N), F32)
    y, hT_p = _wkv_call(scan_in, h0_p, None, 0, B, S // CHUNK, CHUNK, T)
    h0_s = jnp.swapaxes(state_wkv[l], -1, -2)
    y, hT_s = _wkv_call(scan_in, h0_s, y, Tp, DB, 1, DS, T)

    wo_b = w_o[l].astype(BF16)
    wr = jnp.concatenate([w_gr[l], w_er[l], jnp.zeros((D, LANE - N_GROUPS - N_EXPERTS), F32)], axis=1)
    wr_hi = wr.astype(BF16)
    wr_lo = (wr - wr_hi.astype(F32)).astype(BF16)
    br = jnp.concatenate([b_gr[l], b_er[l], jnp.zeros((LANE - N_GROUPS - N_EXPERTS,), F32)])[None]
    h, hb, route = _mix_call(x, attn, y, bonus, g, seg, vec(lnx_g[l]), vec(lnx_b[l]),
                             wo_b[:MLA_HEADS * MLA_V], wo_b[MLA_HEADS * MLA_V:], vec(ln1_g[l]), vec(ln1_b[l]),
                             jnp.stack([wr_hi, wr_lo]), br)

    dest, block_e, n_used, slot_tok = _dispatch(route, T)
    xs = hb[slot_tok]
    wgu = jnp.concatenate([w_eg[l], w_eu[l]], axis=2).astype(BF16)
    ys = _expert_call(block_e, n_used, xs, wgu, w_ed[l].astype(BF16))
    out = _combine_call(h, ys[dest[:, 0]], ys[dest[:, 1]], route, vec(ln2_g[l]), vec(ln2_b[l]))

    y_prompt = out[:Tp].reshape(B, S, D)
    y_sample = out[Tp:].reshape(DB, DS, D)
    p_ckv = ckv[:Tp].reshape(1, B, S, KV_LORA)
    p_kpe = kpe[:Tp].reshape(1, B, S, MLA_ROPE)
    s_ckv = ckv[Tp:].reshape(1, DB, DS, KV_LORA)
    s_kpe = kpe[Tp:].reshape(1, DB, DS, MLA_ROPE)
    p_wkv = jnp.swapaxes(hT_p, -1, -2)[None]
    s_wkv = jnp.swapaxes(hT_s, -1, -2)[None]
    p_sh = pr[:Tp].reshape(B, S, RWKV_IN)[:, -1][None]
    s_sh = pr[Tp:].reshape(DB, DS, RWKV_IN)[:, -1][None]
    return (y_prompt, y_sample, p_ckv, p_kpe, p_wkv, p_sh, s_ckv, s_kpe, s_wkv, s_sh)
```

```python
import functools
import math

import numpy as np
import jax
import jax.numpy as jnp
from jax import lax
from jax.experimental import pallas as pl
from jax.experimental.pallas import tpu as pltpu

F32 = jnp.float32
BF16 = jnp.bfloat16

D_MODEL = 1024
CHUNK = 64
MLA_HEADS = 8
MLA_NOPE = 64
MLA_ROPE = 32
MLA_V = 64
Q_LORA = 384
KV_LORA = 256
ROPE_BASE = 10000.0
MLA_IN = Q_LORA + KV_LORA + MLA_ROPE
MLA_SCALE = (MLA_NOPE + MLA_ROPE) ** -0.5
RWKV_HEADS = 8
RWKV_N = 64
RWKV_DIM = RWKV_HEADS * RWKV_N
DECAY_LORA = 64
AAA_LORA = 64
GATE_LORA = 128
RWKV_IN = 3 * RWKV_DIM + DECAY_LORA + AAA_LORA + GATE_LORA
N_GROUPS = 4
EXPERTS_PER_GROUP = 8
N_EXPERTS = N_GROUPS * EXPERTS_PER_GROUP
TOP_K = 2
D_EXPERT = 256
MOE_BLK = 256
LN_EPS = 1e-5
RMS_EPS = 1e-6
GN_EPS = 64e-5
NEG_INF = -1e30
DEPTH = 1
DN_ALPHA = (2 * DEPTH) ** 0.25

LANE = 128
HEAD_PAD = 128
PROJ_W = 768 + RWKV_IN
SHIFT_GROUP = 32
TOKEN_TILE = 512
ATTN_TILE = 512
V_ONE_LANE = (MLA_V, 0)
LOG2E = math.log2(math.e)
VMEM_LIMIT = 48 * 1024 * 1024


def _cparams(sem):
    return pltpu.CompilerParams(dimension_semantics=sem, vmem_limit_bytes=VMEM_LIMIT)


def _split3(x):
    hi = x.astype(BF16)
    r1 = x - hi.astype(F32)
    mid = r1.astype(BF16)
    lo = (r1 - mid.astype(F32)).astype(BF16)
    return hi, mid, lo


def _dot(a, b):
    return jnp.dot(a, b, preferred_element_type=F32)


def _dot_nt(a, b):
    return lax.dot_general(a, b, (((1,), (1,)), ((), ())), preferred_element_type=F32)


def _dot_exact_rhs(x, w):
    hi, mid, lo = _split3(x)
    return _dot(hi, w) + _dot(mid, w) + _dot(lo, w)


def _dot_exact_lhs(w, x):
    hi, mid, lo = _split3(x)
    return _dot(w, hi) + _dot(w, mid) + _dot(w, lo)


def _sigmoid(x):
    return 1.0 / (1.0 + jnp.exp(-x))


def _layer_norm(x, g, b):
    xc = x - jnp.mean(x, -1, keepdims=True)
    var = jnp.mean(xc * xc, -1, keepdims=True)
    return xc * lax.rsqrt(var + LN_EPS) * g + b


def _proj_kernel(x_ref, rope_ref, w1_ref, gq_ref, gkv_ref, wqa_ref, wqb_ref, wk_ref, pk_ref, wv_ref,
                 q_ref, k_ref, v_ref, ckv_ref, kpe_ref, pr_ref):
    x = x_ref[...].astype(BF16)
    proj = _dot(x, w1_ref[...])
    c_q = proj[:, :Q_LORA]
    c_kv = proj[:, Q_LORA:Q_LORA + KV_LORA]
    kp = proj[:, 640:768]
    pr_ref[...] = proj[:, 768:]

    cqn = c_q * lax.rsqrt(jnp.mean(c_q * c_q, -1, keepdims=True) + RMS_EPS) * gq_ref[...]
    ckv = c_kv * lax.rsqrt(jnp.mean(c_kv * c_kv, -1, keepdims=True) + RMS_EPS) * gkv_ref[...]
    ckv_ref[...] = ckv

    rope = rope_ref[...]
    cq = rope[:, :LANE]
    sq = rope[:, LANE:2 * LANE]
    kt = rope[:, 2 * LANE:]
    prod = kp * kt
    kpe = prod[:, :MLA_ROPE] + prod[:, MLA_ROPE:2 * MLA_ROPE]
    kpe_ref[...] = kpe

    cqb = cqn.astype(BF16)
    qa = _dot(cqb, wqa_ref[...])
    qb = _dot(cqb, wqb_ref[...])
    for h in range(MLA_HEADS):
        sl = slice(h * HEAD_PAD, (h + 1) * HEAD_PAD)
        q_ref[:, sl] = (qa[:, sl] * cq + qb[:, sl] * sq).astype(BF16)

    ckv_b = ckv.astype(BF16)
    k = _dot(ckv_b, wk_ref[...]) + _dot(kpe.astype(BF16), pk_ref[...])
    k_ref[...] = k.astype(BF16)
    lane = lax.broadcasted_iota(jnp.int32, (1, MLA_HEADS * HEAD_PAD), 1)
    odd = (lane // HEAD_PAD) % 2
    one_lane = jnp.where(odd == 1, V_ONE_LANE[1], V_ONE_LANE[0])
    v_one = jnp.where(lane % HEAD_PAD == one_lane, 1.0, 0.0)
    v_ref[...] = (_dot(ckv_b, wv_ref[...]) + v_one).astype(BF16)


def _proj_call(x, rope, w1, gq, gkv, wqa, wqb, wk, pk, wv):
    T = x.shape[0]
    tm = TOKEN_TILE
    row = lambda i: (i, 0)
    full = lambda i: (0, 0)
    wide = MLA_HEADS * HEAD_PAD
    return pl.pallas_call(
        _proj_kernel,
        grid=(T // tm,),
        in_specs=[
            pl.BlockSpec((tm, D_MODEL), row),
            pl.BlockSpec((tm, 3 * LANE), row),
            pl.BlockSpec((D_MODEL, PROJ_W), full),
            pl.BlockSpec((1, Q_LORA), full),
            pl.BlockSpec((1, KV_LORA), full),
            pl.BlockSpec((Q_LORA, wide), full),
            pl.BlockSpec((Q_LORA, wide), full),
            pl.BlockSpec((KV_LORA, wide), full),
            pl.BlockSpec((MLA_ROPE, wide), full),
            pl.BlockSpec((KV_LORA, wide), full),
        ],
        out_specs=[
            pl.BlockSpec((tm, wide), row),
            pl.BlockSpec((tm, wide), row),
            pl.BlockSpec((tm, wide), row),
            pl.BlockSpec((tm, KV_LORA), row),
            pl.BlockSpec((tm, MLA_ROPE), row),
            pl.BlockSpec((tm, RWKV_IN), row),
        ],
        out_shape=[
            jax.ShapeDtypeStruct((T, wide), BF16),
            jax.ShapeDtypeStruct((T, wide), BF16),
            jax.ShapeDtypeStruct((T, wide), BF16),
            jax.ShapeDtypeStruct((T, KV_LORA), F32),
            jax.ShapeDtypeStruct((T, MLA_ROPE), F32),
            jax.ShapeDtypeStruct((T, RWKV_IN), F32),
        ],
        compiler_params=_cparams(("parallel",)),
        name="proj",
    )(x, rope, w1, gq, gkv, wqa, wqb, wk, pk, wv)


def _attn_kernel(q_ref, k_ref, v_ref, o_ref, m_scr, acc_scr, *, tile):
    qi = pl.program_id(2)
    m_scr[...] = jnp.full(m_scr.shape, NEG_INF, F32)
    acc_scr[...] = jnp.zeros(acc_scr.shape, F32)
    reps = tile // LANE

    def kv_block(kj, masked):
        k0 = pl.multiple_of(kj * tile, tile)
        if masked:
            r = lax.broadcasted_iota(jnp.int32, (tile, tile), 0) // CHUNK
            c = lax.broadcasted_iota(jnp.int32, (tile, tile), 1) // CHUNK
            visible = c <= r
        for h in range(2):
            sl = slice(h * HEAD_PAD, (h + 1) * HEAD_PAD)
            s = _dot_nt(q_ref[:, sl], k_ref[pl.ds(k0, tile), sl])
            if masked:
                s = jnp.where(visible, s, NEG_INF)
            m_prev = m_scr[h]
            m_new = jnp.maximum(m_prev, jnp.max(s, axis=1, keepdims=True))
            alpha = jnp.exp2(m_prev - m_new)
            pexp = jnp.exp2(s - jnp.tile(m_new, (1, reps)))
            acc_scr[h] = alpha * acc_scr[h] + _dot(pexp.astype(BF16), v_ref[pl.ds(k0, tile), sl])
            m_scr[h] = m_new

    def body(kj, carry):
        kv_block(kj, False)
        return carry

    lax.fori_loop(0, qi, body, 0)
    kv_block(qi, True)
    acc0, acc1 = acc_scr[0], acc_scr[1]
    lane = lax.broadcasted_iota(jnp.int32, acc0.shape, 1)
    l0 = acc0[:, V_ONE_LANE[0]:V_ONE_LANE[0] + 1]
    l1 = acc1[:, V_ONE_LANE[1]:V_ONE_LANE[1] + 1]
    o_ref[...] = jnp.where(lane < MLA_V, acc0 / l0, acc1 / l1).astype(o_ref.dtype)


def _attn_call(q, k, v, n_batch, seq, t_total):
    tile = ATTN_TILE
    nq = seq // tile
    hp = MLA_HEADS // 2
    resident = pl.BlockSpec((seq, 2 * HEAD_PAD), lambda b, h, i: (b, h), pipeline_mode=pl.Buffered(1))
    return pl.pallas_call(
        functools.partial(_attn_kernel, tile=tile),
        grid=(n_batch, hp, nq),
        in_specs=[
            pl.BlockSpec((tile, 2 * HEAD_PAD), lambda b, h, i: (b * nq + i, h)),
            resident,
            resident,
        ],
        out_specs=pl.BlockSpec((tile, LANE), lambda b, h, i: (b * nq + i, h)),
        out_shape=jax.ShapeDtypeStruct((t_total, hp * LANE), BF16),
        scratch_shapes=[
            pltpu.VMEM((2, tile, LANE), F32),
            pltpu.VMEM((2, tile, LANE), F32),
        ],
        compiler_params=_cparams(("parallel", "parallel", "arbitrary")),
        name="attn",
    )(q, k, v)


def _mla_sample_kernel(q_ref, cn_ref, kn_ref, cp_ref, kp_ref, wuk_ref, wuv_ref, attn_in_ref, o_ref,
                       *, past, dec):
    del attn_in_ref
    cp = cp_ref[0].astype(BF16)
    kp = kp_ref[0].astype(BF16)
    cn = cn_ref[...].astype(BF16)
    kn = kn_ref[...].astype(BF16)
    qpos = (past + lax.broadcasted_iota(jnp.int32, (dec, past), 0)) // CHUNK
    vis_p = (lax.broadcasted_iota(jnp.int32, (dec, past), 1) // CHUNK) <= qpos
    qpos_n = (past + lax.broadcasted_iota(jnp.int32, (dec, dec), 0)) // CHUNK
    vis_n = ((past + lax.broadcasted_iota(jnp.int32, (dec, dec), 1)) // CHUNK) <= qpos_n
    out = jnp.zeros((dec, MLA_HEADS * MLA_V), F32)
    for h in range(MLA_HEADS):
        qn = q_ref[:, h * HEAD_PAD:h * HEAD_PAD + MLA_NOPE]
        qp = q_ref[:, h * HEAD_PAD + MLA_NOPE:h * HEAD_PAD + MLA_NOPE + MLA_ROPE]
        q_lat = _dot(qn, wuk_ref[h]).astype(BF16)
        s_p = jnp.where(vis_p, _dot_nt(q_lat, cp) + _dot_nt(qp, kp), NEG_INF)
        s_n = jnp.where(vis_n, _dot_nt(q_lat, cn) + _dot_nt(qp, kn), NEG_INF)
        m = jnp.maximum(jnp.max(s_p, axis=1, keepdims=True), jnp.max(s_n, axis=1, keepdims=True))
        e_p = jnp.exp2(s_p - m)
        e_n = jnp.exp2(s_n - m)
        l = jnp.sum(e_p, axis=1, keepdims=True) + jnp.sum(e_n, axis=1, keepdims=True)
        o_lat = (_dot(e_p.astype(BF16), cp) + _dot(e_n.astype(BF16), cn)) / l
        out = out + _dot(o_lat.astype(BF16), wuv_ref[h])
    o_ref[...] = out.astype(o_ref.dtype)


def _mla_sample_call(q, ckv, kpe, cache_ckv, cache_kpe, wuk, wuv, attn, row0, n_seq, dec):
    past = cache_ckv.shape[1]
    blk0 = row0 // dec
    wide = MLA_HEADS * HEAD_PAD
    return pl.pallas_call(
        functools.partial(_mla_sample_kernel, past=past, dec=dec),
        grid=(n_seq,),
        in_specs=[
            pl.BlockSpec((dec, wide), lambda b: (blk0 + b, 0)),
            pl.BlockSpec((dec, KV_LORA), lambda b: (blk0 + b, 0)),
            pl.BlockSpec((dec, MLA_ROPE), lambda b: (blk0 + b, 0)),
            pl.BlockSpec((1, past, KV_LORA), lambda b: (b, 0, 0)),
            pl.BlockSpec((1, past, MLA_ROPE), lambda b: (b, 0, 0)),
            pl.BlockSpec((MLA_HEADS, MLA_NOPE, KV_LORA), lambda b: (0, 0, 0)),
            pl.BlockSpec((MLA_HEADS, KV_LORA, MLA_HEADS * MLA_V), lambda b: (0, 0, 0)),
            pl.BlockSpec(memory_space=pl.ANY),
        ],
        out_specs=pl.BlockSpec((dec, MLA_HEADS * MLA_V), lambda b: (blk0 + b, 0)),
        out_shape=jax.ShapeDtypeStruct(attn.shape, attn.dtype),
        input_output_aliases={7: 0},
        compiler_params=_cparams(("parallel",)),
        name="mla_sample",
    )(q, ckv, kpe, cache_ckv, cache_kpe, wuk, wuv, attn)


def _prep_kernel(pr_ref, bnd_ref, mu_ref, w0_ref, a0_ref, kk_ref, ka_ref, rk_ref, w2_ref, a2_ref, g2_ref,
                 seg_ref, r_ref, lw_ref, kh_ref, v_ref, na_ref, b_ref, bonus_ref, g_ref, *, tm):
    ng = tm // SHIFT_GROUP
    pr = pr_ref[...]
    pr3 = pr.reshape(ng, SHIFT_GROUP, RWKV_IN)
    rolled = pltpu.roll(pr3, 1, 1)
    first = lax.broadcasted_iota(jnp.int32, pr3.shape, 1) == 0
    prev = jnp.where(first, bnd_ref[...], rolled).reshape(tm, RWKV_IN)
    u = pr + mu_ref[...] * (prev - pr)
    o1, o2, o3 = RWKV_DIM, 2 * RWKV_DIM, 3 * RWKV_DIM
    o4, o5 = o3 + DECAY_LORA, o3 + DECAY_LORA + AAA_LORA
    r, k, v = u[:, :o1], u[:, o1:o2], u[:, o2:o3]
    w_lo, a_lo, g_lo = u[:, o3:o4], u[:, o4:o5], u[:, o5:]
    wl = w0_ref[...] + _dot(jnp.tanh(w_lo).astype(BF16), w2_ref[...])
    lw_ref[...] = -math.exp(-0.5) * _sigmoid(wl)
    a = _sigmoid(a0_ref[...] + _dot(a_lo.astype(BF16), a2_ref[...]))
    g_ref[...] = _dot(_sigmoid(g_lo).astype(BF16), g2_ref[...])
    seg = seg_ref[...]
    kk = k * kk_ref[...]
    kk = kk / jnp.maximum(jnp.sqrt(_dot_exact_rhs(kk * kk, seg)), 1e-12)
    kh = k * (1.0 + (a - 1.0) * ka_ref[...])
    r_ref[...] = r
    kh_ref[...] = kh
    v_ref[...] = v
    na_ref[...] = -kk
    b_ref[...] = kk * a
    bonus_ref[...] = _dot_exact_rhs(r * kh * rk_ref[...], seg) * v


def _prep_call(pr, bnd, mu, w0, a0, k_k, k_a, r_k, w2, a2, g2, seg):
    T = pr.shape[0]
    tm = TOKEN_TILE
    row = lambda i: (i, 0)
    full = lambda i: (0, 0)
    vec = pl.BlockSpec((1, RWKV_DIM), full)
    out = pl.BlockSpec((tm, RWKV_DIM), row)
    return pl.pallas_call(
        functools.partial(_prep_kernel, tm=tm),
        grid=(T // tm,),
        in_specs=[
            pl.BlockSpec((tm, RWKV_IN), row),
            pl.BlockSpec((tm // SHIFT_GROUP, 1, RWKV_IN), lambda i: (i, 0, 0)),
            pl.BlockSpec((1, RWKV_IN), full),
            vec, vec, vec, vec, vec,
            pl.BlockSpec((DECAY_LORA, RWKV_DIM), full),
            pl.BlockSpec((AAA_LORA, RWKV_DIM), full),
            pl.BlockSpec((GATE_LORA, RWKV_DIM), full),
            pl.BlockSpec((RWKV_DIM, RWKV_DIM), full),
        ],
        out_specs=[out] * 8,
        out_shape=[jax.ShapeDtypeStruct((T, RWKV_DIM), F32)] * 8,
        compiler_params=_cparams(("parallel",)),
        name="rwkv_prep",
    )(pr, bnd, mu, w0, a0, k_k, k_a, r_k, w2, a2, g2, seg)


def _wkv_kernel(r_ref, lw_ref, k_ref, v_ref, a_ref, b_ref, h0_ref, *rest, C, aliased):
    if aliased:
        _, y_ref, hT_ref, h_scr = rest
    else:
        y_ref, hT_ref, h_scr = rest
    c = pl.program_id(1)
    N = RWKV_N

    @pl.when(c == 0)
    def _():
        h_scr[...] = h0_ref[0]

    row = lax.broadcasted_iota(jnp.int32, (C, C), 0)
    col = lax.broadcasted_iota(jnp.int32, (C, C), 1)
    lower = col <= row
    strict = col < row
    tri = jnp.where(lower, 1.0, 0.0).astype(BF16)
    eye_n = (lax.broadcasted_iota(jnp.int32, (N, N), 0) == lax.broadcasted_iota(jnp.int32, (N, N), 1))
    eye_c = jnp.where(col == row, 1.0, 0.0)

    lw = lw_ref[...]
    cum = _dot_exact_lhs(tri, lw)
    cum_end = cum[C - 1:C, :]
    e_neg = jnp.exp(-cum)
    e_end = jnp.exp(cum_end - cum)
    a_t = (a_ref[...] * jnp.exp(cum - lw)).astype(BF16)
    r_t = (r_ref[...] * jnp.exp(cum)).astype(BF16)
    b_in = b_ref[...]
    k_in = k_ref[...]
    b_t = (b_in * e_neg).astype(BF16)
    k_t = (k_in * e_neg).astype(BF16)
    b_e = (b_in * e_end).astype(BF16)
    k_e = (k_in * e_end).astype(BF16)
    v_b = v_ref[...].astype(BF16)
    p_end = jnp.exp(cum_end)
    eye_bf = jnp.where(eye_n, 1.0, 0.0).astype(BF16)

    n_lev = int(round(math.log2(C))) - 1
    for h in range(RWKV_HEADS):
        sl = slice(h * N, (h + 1) * N)
        a_h, r_h, v_h = a_t[:, sl], r_t[:, sl], v_b[:, sl]
        ar = jnp.concatenate([a_h, r_h], axis=0)
        m_b = _dot_nt(ar, b_t[:, sl])
        m_k = _dot_nt(ar, k_t[:, sl])
        l_ab = jnp.where(strict, m_b[:C], 0.0)
        a_ak = jnp.where(strict, m_k[:C], 0.0).astype(BF16)
        a_rb = jnp.where(lower, m_b[C:], 0.0).astype(BF16)
        a_rk = jnp.where(lower, m_k[C:], 0.0).astype(BF16)
        t_inv = eye_c + l_ab
        l_pow = l_ab.astype(BF16)
        for _ in range(n_lev):
            l_sq = _dot(l_pow, l_pow)
            l_pow = l_sq.astype(BF16)
            t_inv = t_inv + _dot(t_inv.astype(BF16), l_pow)
        h0 = h_scr[h]
        h0_b = h0.astype(BF16)
        x = _dot(a_h, h0_b) + _dot(a_ak, v_h)
        u = _dot(t_inv.astype(BF16), x.astype(BF16)).astype(BF16)
        y = _dot(r_h, h0_b) + _dot(a_rb, u) + _dot(a_rk, v_h)
        y_ref[:, sl] = y
        b_eT = _dot_nt(eye_bf, b_e[:, sl]).astype(BF16)
        k_eT = _dot_nt(eye_bf, k_e[:, sl]).astype(BF16)
        p_col = jnp.sum(jnp.where(eye_n, p_end[:, sl], 0.0), axis=1, keepdims=True)
        h_new = p_col * h0 + _dot(b_eT, u) + _dot(k_eT, v_h)
        h_scr[h] = h_new

    @pl.when(c == pl.num_programs(1) - 1)
    def _():
        hT_ref[0] = h_scr[...]


def _wkv_call(arrs, h0, y_prev, row0, n_seq, n_chunk, C, t_total):
    blk0 = row0 // C
    tok = pl.BlockSpec((C, RWKV_DIM), lambda b, c: (blk0 + b * n_chunk + c, 0))
    st = pl.BlockSpec((1, RWKV_HEADS, RWKV_N, RWKV_N), lambda b, c: (b, 0, 0, 0))
    in_specs = [tok] * 6 + [st]
    args = list(arrs) + [h0]
    aliases = {}
    if y_prev is not None:
        in_specs.append(pl.BlockSpec(memory_space=pl.ANY))
        args.append(y_prev)
        aliases = {7: 0}
    return pl.pallas_call(
        functools.partial(_wkv_kernel, C=C, aliased=y_prev is not None),
        grid=(n_seq, n_chunk),
        in_specs=in_specs,
        out_specs=[tok, st],
        out_shape=[
            jax.ShapeDtypeStruct((t_total, RWKV_DIM), F32),
            jax.ShapeDtypeStruct((n_seq, RWKV_HEADS, RWKV_N, RWKV_N), F32),
        ],
        scratch_shapes=[pltpu.VMEM((RWKV_HEADS, RWKV_N, RWKV_N), F32)],
        input_output_aliases=aliases,
        compiler_params=_cparams(("parallel", "arbitrary")),
        name="wkv_c%d" % C,
    )(*args)


def _mix_kernel(x_ref, attn_ref, y_ref, bonus_ref, g_ref, seg_ref, lng_ref, lnb_ref, woa_ref, wob_ref,
                g1_ref, b1_ref, wr_ref, br_ref, h_ref, hb_ref, route_ref):
    seg = seg_ref[...]
    y = y_ref[...]
    inv_n = 1.0 / RWKV_N
    yc = y - _dot_exact_rhs(y, seg) * inv_n
    var = _dot_exact_rhs(yc * yc, seg) * inv_n
    yn = yc * lax.rsqrt(var + GN_EPS) * lng_ref[...] + lnb_ref[...]
    rw = ((yn + bonus_ref[...]) * g_ref[...]).astype(BF16)
    m = _dot(attn_ref[...], woa_ref[...]) + _dot(rw, wob_ref[...])
    h = _layer_norm(DN_ALPHA * x_ref[...] + m, g1_ref[...], b1_ref[...])
    h_ref[...] = h
    hb_ref[...] = h.astype(BF16)

    h_hi = h.astype(BF16)
    h_lo = (h - h_hi.astype(F32)).astype(BF16)
    logits = _dot(h_hi, wr_ref[0]) + _dot(h_lo, wr_ref[0]) + _dot(h_hi, wr_ref[1]) + br_ref[...]
    lane = lax.broadcasted_iota(jnp.int32, logits.shape, 1)
    big = jnp.int32(LANE)
    gl = jnp.where(lane < N_GROUPS, logits, NEG_INF)
    gmax = jnp.max(gl, axis=1, keepdims=True)
    grp = jnp.min(jnp.where(gl == gmax, lane, big), axis=1, keepdims=True)
    p_grp = 1.0 / jnp.sum(jnp.exp(gl - gmax), axis=1, keepdims=True)
    e_idx = lane - N_GROUPS
    in_grp = (lane >= N_GROUPS) & (lane < N_GROUPS + N_EXPERTS) & ((e_idx // EXPERTS_PER_GROUP) == grp)
    el = jnp.where(in_grp, logits, NEG_INF)
    m1 = jnp.max(el, axis=1, keepdims=True)
    i1 = jnp.min(jnp.where(el == m1, lane, big), axis=1, keepdims=True)
    el2 = jnp.where(lane == i1, NEG_INF, el)
    m2 = jnp.max(el2, axis=1, keepdims=True)
    i2 = jnp.min(jnp.where(el2 == m2, lane, big), axis=1, keepdims=True)
    t = jnp.exp(m2 - m1)
    g1 = p_grp / (1.0 + t)
    g2 = g1 * t
    e1 = (i1 - N_GROUPS).astype(F32)
    e2 = (i2 - N_GROUPS).astype(F32)
    route_ref[...] = jnp.where(lane == 0, e1, jnp.where(lane == 1, e2, jnp.where(lane == 2, g1,
                               jnp.where(lane == 3, g2, 0.0))))


def _mix_call(x, attn, y, bonus, g, seg, lnx_g, lnx_b, woa, wob, ln1_g, ln1_b, wr, br):
    T = x.shape[0]
    tm = TOKEN_TILE
    row = lambda i: (i, 0)
    full = lambda i: (0, 0)
    half = pl.BlockSpec((tm, RWKV_DIM), row)
    vec5 = pl.BlockSpec((1, RWKV_DIM), full)
    vec10 = pl.BlockSpec((1, D_MODEL), full)
    return pl.pallas_call(
        _mix_kernel,
        grid=(T // tm,),
        in_specs=[
            pl.BlockSpec((tm, D_MODEL), row), half, half, half, half,
            pl.BlockSpec((RWKV_DIM, RWKV_DIM), full), vec5, vec5,
            pl.BlockSpec((RWKV_DIM, D_MODEL), full), pl.BlockSpec((RWKV_DIM, D_MODEL), full),
            vec10, vec10,
            pl.BlockSpec((2, D_MODEL, LANE), lambda i: (0, 0, 0)), pl.BlockSpec((1, LANE), full),
        ],
        out_specs=[pl.BlockSpec((tm, D_MODEL), row), pl.BlockSpec((tm, D_MODEL), row),
                   pl.BlockSpec((tm, LANE), row)],
        out_shape=[jax.ShapeDtypeStruct((T, D_MODEL), F32), jax.ShapeDtypeStruct((T, D_MODEL), BF16),
                   jax.ShapeDtypeStruct((T, LANE), F32)],
        compiler_params=_cparams(("parallel",)),
        name="mix",
    )(x, attn, y, bonus, g, seg, lnx_g, lnx_b, woa, wob, ln1_g, ln1_b, wr, br)


def _expert_kernel(be_ref, nu_ref, xs_ref, wgu_ref, wd_ref, ys_ref):
    i = pl.program_id(0)

    @pl.when(i < nu_ref[0])
    def _():
        gu = _dot(xs_ref[...], wgu_ref[0])
        gate, up = gu[:, :D_EXPERT], gu[:, D_EXPERT:]
        act = (gate * _sigmoid(gate) * up).astype(BF16)
        ys_ref[...] = _dot(act, wd_ref[0])

    @pl.when(i >= nu_ref[0])
    def _():
        ys_ref[...] = jnp.zeros(ys_ref.shape, ys_ref.dtype)


def _expert_call(block_e, n_used, xs, wgu, wd):
    n_blk = xs.shape[0] // MOE_BLK
    grid_spec = pltpu.PrefetchScalarGridSpec(
        num_scalar_prefetch=2,
        grid=(n_blk,),
        in_specs=[
            pl.BlockSpec((MOE_BLK, D_MODEL), lambda i, be, nu: (i, 0)),
            pl.BlockSpec((1, D_MODEL, 2 * D_EXPERT), lambda i, be, nu: (be[i], 0, 0)),
            pl.BlockSpec((1, D_EXPERT, D_MODEL), lambda i, be, nu: (be[i], 0, 0)),
        ],
        out_specs=pl.BlockSpec((MOE_BLK, D_MODEL), lambda i, be, nu: (i, 0)),
    )
    return pl.pallas_call(
        _expert_kernel,
        grid_spec=grid_spec,
        out_shape=jax.ShapeDtypeStruct((n_blk * MOE_BLK, D_MODEL), F32),
        compiler_params=_cparams(("arbitrary",)),
        name="experts",
    )(block_e, n_used, xs, wgu, wd)


def _combine_kernel(h_ref, ya_ref, yb_ref, route_ref, g2_ref, b2_ref, o_ref):
    route = route_ref[...]
    f = ya_ref[...] * route[:, 2:3] + yb_ref[...] * route[:, 3:4]
    o_ref[...] = _layer_norm(DN_ALPHA * h_ref[...] + f, g2_ref[...], b2_ref[...])


def _combine_call(h, ya, yb, route, ln2_g, ln2_b):
    T = h.shape[0]
    tm = TOKEN_TILE
    row = lambda i: (i, 0)
    full = lambda i: (0, 0)
    big = pl.BlockSpec((tm, D_MODEL), row)
    return pl.pallas_call(
        _combine_kernel,
        grid=(T // tm,),
        in_specs=[big, big, big, pl.BlockSpec((tm, LANE), row),
                  pl.BlockSpec((1, D_MODEL), full), pl.BlockSpec((1, D_MODEL), full)],
        out_specs=big,
        out_shape=jax.ShapeDtypeStruct((T, D_MODEL), F32),
        compiler_params=_cparams(("parallel",)),
        name="combine",
    )(h, ya, yb, route, ln2_g, ln2_b)


def _prep_weights(w_in, w_uq, w_ukv):
    half = MLA_ROPE // 2
    kpe_w = w_in[:, Q_LORA + KV_LORA:MLA_IN]
    kpe_b = jnp.concatenate([-kpe_w[:, half:], kpe_w[:, :half]], axis=1)
    w1 = jnp.concatenate([w_in[:, :Q_LORA + KV_LORA], kpe_w, kpe_b,
                          jnp.zeros((D_MODEL, 64), F32), w_in[:, MLA_IN:]], axis=1).astype(BF16)
    pad_q = jnp.zeros((Q_LORA, MLA_HEADS, HEAD_PAD - MLA_NOPE - MLA_ROPE), F32)
    wqa = jnp.concatenate([w_uq, pad_q], axis=2).reshape(Q_LORA, -1).astype(BF16)
    rot = jnp.concatenate([jnp.zeros((Q_LORA, MLA_HEADS, MLA_NOPE), F32),
                           -w_uq[:, :, MLA_NOPE + half:], w_uq[:, :, MLA_NOPE:MLA_NOPE + half], pad_q], axis=2)
    wqb = rot.reshape(Q_LORA, -1).astype(BF16)
    w_uk, w_uv = w_ukv[:, :, :MLA_NOPE], w_ukv[:, :, MLA_NOPE:]
    wk = jnp.concatenate([w_uk, jnp.zeros((KV_LORA, MLA_HEADS, HEAD_PAD - MLA_NOPE), F32)], axis=2)
    wk = wk.reshape(KV_LORA, -1).astype(BF16)
    pk_np = np.zeros((MLA_ROPE, MLA_HEADS * HEAD_PAD), np.float32)
    for h in range(MLA_HEADS):
        for i in range(MLA_ROPE):
            pk_np[i, h * HEAD_PAD + MLA_NOPE + i] = 1.0
    pk = jnp.asarray(pk_np).astype(BF16)
    zv = jnp.zeros((KV_LORA, MLA_HEADS // 2, MLA_V), F32)
    wv4 = w_uv.reshape(KV_LORA, MLA_HEADS // 2, 2, MLA_V)
    wv = jnp.stack([jnp.concatenate([wv4[:, :, 0], zv], axis=2),
                    jnp.concatenate([zv, wv4[:, :, 1]], axis=2)], axis=2)
    wv = wv.reshape(KV_LORA, -1).astype(BF16)
    wuk = jnp.transpose(w_uk, (1, 2, 0)).astype(BF16)
    wuv_np = np.zeros((MLA_HEADS, MLA_HEADS * MLA_V), np.float32)
    for h in range(MLA_HEADS):
        wuv_np[h, h * MLA_V:(h + 1) * MLA_V] = 1.0
    wuv = jnp.transpose(w_uv, (1, 0, 2))
    wuv = (jnp.tile(wuv, (1, 1, MLA_HEADS)) * jnp.asarray(wuv_np)[:, None, :]).astype(BF16)
    return w1, wqa, wqb, wk, pk, wv, wuk, wuv


def _rope_table(pos):
    inv = ROPE_BASE ** (-jnp.arange(0, MLA_ROPE, 2, dtype=F32) / MLA_ROPE)
    ang = pos.astype(F32)[:, None] * inv[None, :]
    cos, sin = jnp.cos(ang), jnp.sin(ang)
    n = pos.shape[0]
    one = jnp.ones((n, MLA_NOPE), F32)
    z32 = jnp.zeros((n, HEAD_PAD - MLA_NOPE - MLA_ROPE), F32)
    z64 = jnp.zeros((n, MLA_NOPE), F32)
    cq = jnp.concatenate([one, cos, cos, z32], axis=1) * (MLA_SCALE * LOG2E)
    sq = jnp.concatenate([z64, sin, sin, z32], axis=1) * (MLA_SCALE * LOG2E)
    kt = jnp.concatenate([cos, cos, sin, sin, z64], axis=1)
    return jnp.concatenate([cq, sq, kt], axis=1)


def _seg_ones():
    idx = np.arange(RWKV_DIM) // RWKV_N
    return jnp.asarray((idx[:, None] == idx[None, :]).astype(np.float32)).astype(BF16)


def _dispatch(route, t_total):
    A = t_total * TOP_K
    flat_e = route[:, :TOP_K].astype(jnp.int32).reshape(A)
    onehot = (flat_e[:, None] == jnp.arange(N_EXPERTS, dtype=jnp.int32)[None, :]).astype(jnp.int32)
    csum = jnp.cumsum(onehot, axis=0)
    counts = csum[-1]
    rank = jnp.take_along_axis(csum, flat_e[:, None], axis=1)[:, 0] - 1
    blocks_per_e = (counts + MOE_BLK - 1) // MOE_BLK
    blk_end = jnp.cumsum(blocks_per_e)
    blk_start = blk_end - blocks_per_e
    dest = blk_start[flat_e] * MOE_BLK + rank
    n_blk = -(-A // MOE_BLK) + N_EXPERTS
    block_e = jnp.minimum(jnp.searchsorted(blk_end, jnp.arange(n_blk, dtype=jnp.int32), side='right'),
                          N_EXPERTS - 1).astype(jnp.int32)
    n_used = blk_end[-1:].astype(jnp.int32)
    slot_tok = jnp.zeros((n_blk * MOE_BLK,), jnp.int32).at[dest].set(
        jnp.arange(A, dtype=jnp.int32) // TOP_K)
    return dest.reshape(t_total, TOP_K), block_e, n_used, slot_tok


def kernel(x_prompt, x_sample, cache_ckv, cache_kpe, state_wkv, state_shift, w_in, q_norm_g, kv_norm_g, w_uq,
           w_ukv, mu_shift, w0, w2, a0, a2, g2, k_k, k_a, r_k, lnx_g, lnx_b, w_o, ln1_g, ln1_b, w_gr, b_gr,
           w_er, b_er, w_eg, w_eu, w_ed, ln2_g, ln2_b):
    B, S, D = x_prompt.shape
    DB, DS, _ = x_sample.shape
    past = cache_ckv.shape[2]
    Tp, Ts = B * S, DB * DS
    T = Tp + Ts
    assert D == D_MODEL and DS == SHIFT_GROUP and S % ATTN_TILE == 0 and S % CHUNK == 0
    assert Tp % TOKEN_TILE == 0 and T % TOKEN_TILE == 0 and w_in.shape[0] == DEPTH

    l = 0
    x = jnp.concatenate([x_prompt.reshape(Tp, D), x_sample.reshape(Ts, D)], axis=0)
    w1, wqa, wqb, wk, pk, wv, wuk, wuv = _prep_weights(w_in[l], w_uq[l], w_ukv[l])
    pos = jnp.concatenate([jnp.tile(jnp.arange(S, dtype=jnp.int32), B),
                           jnp.tile(past + jnp.arange(DS, dtype=jnp.int32), DB)])
    rope = _rope_table(pos)

    q, kcat, vcat, ckv, kpe, pr = _proj_call(x, rope, w1, q_norm_g[l][None], kv_norm_g[l][None],
                                             wqa, wqb, wk, pk, wv)

    attn = _attn_call(q, kcat, vcat, B, S, T)
    attn = _mla_sample_call(q, ckv, kpe, cache_ckv[l], cache_kpe[l], wuk, wuv, attn, Tp, DB, DS)

    last_rows = pr[SHIFT_GROUP - 1::SHIFT_GROUP]
    bnd = jnp.concatenate([jnp.zeros((1, RWKV_IN), F32), last_rows[:-1]], axis=0)
    gidx = jnp.arange(T // SHIFT_GROUP)
    seq_start = (gidx < Tp // SHIFT_GROUP) & (gidx % (S // SHIFT_GROUP) == 0)
    bnd = jnp.where(seq_start[:, None], 0.0, bnd)
    bnd = jnp.concatenate([bnd[:Tp // SHIFT_GROUP], state_shift[l]], axis=0)[:, None, :]

    seg = _seg_ones()
    vec = lambda a: a.reshape(1, -1)
    r, lw, kh, v, na, b, bonus, g = _prep_call(
        pr, bnd, vec(mu_shift[l]), vec(w0[l]), vec(a0[l]), vec(k_k[l]), vec(k_a[l]), vec(r_k[l]),
        w2[l].astype(BF16), a2[l].astype(BF16), g2[l].astype(BF16), seg)

    scan_in = (r, lw, kh, v, na, b)
    h0_p = jnp.zeros((B, RWKV_HEADS, RWKV_N, RWKV_N), F32)
    y, hT_p = _wkv_call(scan_in, h0_p, None, 0, B, S // CHUNK, CHUNK, T)
    h0_s = jnp.swapaxes(state_wkv[l], -1, -2)
    y, hT_s = _wkv_call(scan_in, h0_s, y, Tp, DB, 1, DS, T)

    wo_b = w_o[l].astype(BF16)
    wr = jnp.concatenate([w_gr[l], w_er[l], jnp.zeros((D, LANE - N_GROUPS - N_EXPERTS), F32)], axis=1)
    wr_hi = wr.astype(BF16)
    wr_lo = (wr - wr_hi.astype(F32)).astype(BF16)
    br = jnp.concatenate([b_gr[l], b_er[l], jnp.zeros((LANE - N_GROUPS - N_EXPERTS,), F32)])[None]
    h, hb, route = _mix_call(x, attn, y, bonus, g, seg, vec(lnx_g[l]), vec(lnx_b[l]),
                             wo_b[:MLA_HEADS * MLA_V], wo_b[MLA_HEADS * MLA_V:], vec(ln1_g[l]), vec(ln1_b[l]),
                             jnp.stack([wr_hi, wr_lo]), br)

    dest, block_e, n_used, slot_tok = _dispatch(route, T)
    xs = hb[slot_tok]
    wgu = jnp.concatenate([w_eg[l], w_eu[l]], axis=2).astype(BF16)
    ys = _expert_call(block_e, n_used, xs, wgu, w_ed[l].astype(BF16))
    out = _combine_call(h, ys[dest[:, 0]], ys[dest[:, 1]], route, vec(ln2_g[l]), vec(ln2_b[l]))

    y_prompt = out[:Tp].reshape(B, S, D)
    y_sample = out[Tp:].reshape(DB, DS, D)
    p_ckv = ckv[:Tp].reshape(1, B, S, KV_LORA)
    p_kpe = kpe[:Tp].reshape(1, B, S, MLA_ROPE)
    s_ckv = ckv[Tp:].reshape(1, DB, DS, KV_LORA)
    s_kpe = kpe[Tp:].reshape(1, DB, DS, MLA_ROPE)
    p_wkv = jnp.swapaxes(hT_p, -1, -2)[None]
    s_wkv = jnp.swapaxes(hT_s, -1, -2)[None]
    p_sh = pr[:Tp].reshape(B, S, RWKV_IN)[:, -1][None]
    s_sh = pr[Tp:].reshape(DB, DS, RWKV_IN)[:, -1][None]
    return (y_prompt, y_sample, p_ckv, p_kpe, p_wkv, p_sh, s_ckv, s_kpe, s_wkv, s_sh)
```

```python
import functools
import math

import numpy as np
import jax
import jax.numpy as jnp
from jax import lax
from jax.experimental import pallas as pl
from jax.experimental.pallas import tpu as pltpu

F32 = jnp.float32
BF16 = jnp.bfloat16

D_MODEL = 1024
CHUNK = 64
MLA_HEADS = 8
MLA_NOPE = 64
MLA_ROPE = 32
MLA_V = 64
Q_LORA = 384
KV_LORA = 256
ROPE_BASE = 10000.0
MLA_IN = Q_LORA + KV_LORA + MLA_ROPE
MLA_SCALE = (MLA_NOPE + MLA_ROPE) ** -0.5
RWKV_HEADS = 8
RWKV_N = 64
RWKV_DIM = RWKV_HEADS * RWKV_N
DECAY_LORA = 64
AAA_LORA = 64
GATE_LORA = 128
RWKV_IN = 3 * RWKV_DIM + DECAY_LORA + AAA_LORA + GATE_LORA
N_GROUPS = 4
EXPERTS_PER_GROUP = 8
N_EXPERTS = N_GROUPS * EXPERTS_PER_GROUP
TOP_K = 2
D_EXPERT = 256
MOE_BLK = 256
LN_EPS = 1e-5
RMS_EPS = 1e-6
GN_EPS = 64e-5
NEG_INF = -1e30
DEPTH = 1
DN_ALPHA = (2 * DEPTH) ** 0.25

LANE = 128
HEAD_PAD = 128
PROJ_W = 768 + RWKV_IN
SHIFT_GROUP = 32
TOKEN_TILE = 512
ATTN_TQ = 1024
ATTN_TK = 512
V_ONE_LANE = (MLA_V, 0)
LOG2E = math.log2(math.e)
VMEM_LIMIT = 48 * 1024 * 1024
WKV_GROUP = 4
WKV_SUB = 4


def _cparams(sem):
    return pltpu.CompilerParams(dimension_semantics=sem, vmem_limit_bytes=VMEM_LIMIT)


def _split3(x):
    hi = x.astype(BF16)
    r1 = x - hi.astype(F32)
    mid = r1.astype(BF16)
    lo = (r1 - mid.astype(F32)).astype(BF16)
    return hi, mid, lo


def _dot(a, b):
    return jnp.dot(a, b, preferred_element_type=F32)


def _dot_nt(a, b):
    return lax.dot_general(a, b, (((1,), (1,)), ((), ())), preferred_element_type=F32)


def _dot_exact_rhs(x, w):
    hi, mid, lo = _split3(x)
    return _dot(hi, w) + _dot(mid, w) + _dot(lo, w)


def _dot_exact_lhs(w, x):
    hi, mid, lo = _split3(x)
    return _dot(w, hi) + _dot(w, mid) + _dot(w, lo)


def _sigmoid(x):
    return 1.0 / (1.0 + jnp.exp(-x))


def _layer_norm(x, g, b):
    xc = x - jnp.mean(x, -1, keepdims=True)
    var = jnp.mean(xc * xc, -1, keepdims=True)
    return xc * lax.rsqrt(var + LN_EPS) * g + b


def _proj_kernel(x_ref, rope_ref, w1_ref, gq_ref, gkv_ref, wqa_ref, wqb_ref, wk_ref, pk_ref, wv_ref,
                 q_ref, k_ref, v_ref, ckv_ref, kpe_ref, pr_ref):
    x = x_ref[...].astype(BF16)
    proj = _dot(x, w1_ref[...])
    c_q = proj[:, :Q_LORA]
    c_kv = proj[:, Q_LORA:Q_LORA + KV_LORA]
    kp = proj[:, 640:768]
    pr_ref[...] = proj[:, 768:]

    cqn = c_q * lax.rsqrt(jnp.mean(c_q * c_q, -1, keepdims=True) + RMS_EPS) * gq_ref[...]
    ckv = c_kv * lax.rsqrt(jnp.mean(c_kv * c_kv, -1, keepdims=True) + RMS_EPS) * gkv_ref[...]
    ckv_ref[...] = ckv

    rope = rope_ref[...]
    cq = rope[:, :LANE]
    sq = rope[:, LANE:2 * LANE]
    kt = rope[:, 2 * LANE:]
    prod = kp * kt
    kpe = prod[:, :MLA_ROPE] + prod[:, MLA_ROPE:2 * MLA_ROPE]
    kpe_ref[...] = kpe

    cqb = cqn.astype(BF16)
    qa = _dot(cqb, wqa_ref[...])
    qb = _dot(cqb, wqb_ref[...])
    for h in range(MLA_HEADS):
        sl = slice(h * HEAD_PAD, (h + 1) * HEAD_PAD)
        q_ref[:, sl] = (qa[:, sl] * cq + qb[:, sl] * sq).astype(BF16)

    ckv_b = ckv.astype(BF16)
    k = _dot(ckv_b, wk_ref[...]) + _dot(kpe.astype(BF16), pk_ref[...])
    k_ref[...] = k.astype(BF16)
    lane = lax.broadcasted_iota(jnp.int32, (1, MLA_HEADS * HEAD_PAD), 1)
    odd = (lane // HEAD_PAD) % 2
    one_lane = jnp.where(odd == 1, V_ONE_LANE[1], V_ONE_LANE[0])
    v_one = jnp.where(lane % HEAD_PAD == one_lane, 1.0, 0.0)
    v_ref[...] = (_dot(ckv_b, wv_ref[...]) + v_one).astype(BF16)


def _proj_call(x, rope, w1, gq, gkv, wqa, wqb, wk, pk, wv):
    T = x.shape[0]
    tm = TOKEN_TILE
    row = lambda i: (i, 0)
    full = lambda i: (0, 0)
    wide = MLA_HEADS * HEAD_PAD
    return pl.pallas_call(
        _proj_kernel,
        grid=(T // tm,),
        in_specs=[
            pl.BlockSpec((tm, D_MODEL), row),
            pl.BlockSpec((tm, 3 * LANE), row),
            pl.BlockSpec((D_MODEL, PROJ_W), full),
            pl.BlockSpec((1, Q_LORA), full),
            pl.BlockSpec((1, KV_LORA), full),
            pl.BlockSpec((Q_LORA, wide), full),
            pl.BlockSpec((Q_LORA, wide), full),
            pl.BlockSpec((KV_LORA, wide), full),
            pl.BlockSpec((MLA_ROPE, wide), full),
            pl.BlockSpec((KV_LORA, wide), full),
        ],
        out_specs=[
            pl.BlockSpec((tm, wide), row),
            pl.BlockSpec((tm, wide), row),
            pl.BlockSpec((tm, wide), row),
            pl.BlockSpec((tm, KV_LORA), row),
            pl.BlockSpec((tm, MLA_ROPE), row),
            pl.BlockSpec((tm, RWKV_IN), row),
        ],
        out_shape=[
            jax.ShapeDtypeStruct((T, wide), BF16),
            jax.ShapeDtypeStruct((T, wide), BF16),
            jax.ShapeDtypeStruct((T, wide), BF16),
            jax.ShapeDtypeStruct((T, KV_LORA), F32),
            jax.ShapeDtypeStruct((T, MLA_ROPE), F32),
            jax.ShapeDtypeStruct((T, RWKV_IN), F32),
        ],
        compiler_params=_cparams(("parallel",)),
        name="proj",
    )(x, rope, w1, gq, gkv, wqa, wqb, wk, pk, wv)


def _attn_kernel(q_ref, k_ref, v_ref, o_ref, m_scr, acc_scr, *, tq, tk):
    qi = pl.program_id(2)
    m_scr[...] = jnp.full(m_scr.shape, NEG_INF, F32)
    acc_scr[...] = jnp.zeros(acc_scr.shape, F32)
    reps = tk // LANE
    n_diag = tq // tk

    def kv_block(kj, masked):
        k0 = pl.multiple_of(kj * tk, tk)
        if masked:
            r = (qi * tq + lax.broadcasted_iota(jnp.int32, (tq, tk), 0)) // CHUNK
            c = (k0 + lax.broadcasted_iota(jnp.int32, (tq, tk), 1)) // CHUNK
            visible = c <= r
        H = range(2)
        sls = [slice(h * HEAD_PAD, (h + 1) * HEAD_PAD) for h in H]
        s = [_dot_nt(q_ref[:, sl], k_ref[pl.ds(k0, tk), sl]) for sl in sls]
        if masked:
            s = [jnp.where(visible, s[h], NEG_INF) for h in H]
        m_prev = [m_scr[h] for h in H]
        m_new = [jnp.maximum(m_prev[h], jnp.max(s[h], axis=1, keepdims=True)) for h in H]
        pexp = [jnp.exp2(s[h] - jnp.tile(m_new[h], (1, reps))).astype(BF16) for h in H]
        pv = [_dot(pexp[h], v_ref[pl.ds(k0, tk), sls[h]]) for h in H]
        for h in H:
            acc_scr[h] = jnp.exp2(m_prev[h] - m_new[h]) * acc_scr[h] + pv[h]
            m_scr[h] = m_new[h]

    def body(kj, carry):
        kv_block(kj, False)
        return carry

    lax.fori_loop(0, qi * n_diag, body, 0)
    for d in range(n_diag):
        kv_block(qi * n_diag + d, True)
    acc0, acc1 = acc_scr[0], acc_scr[1]
    lane = lax.broadcasted_iota(jnp.int32, acc0.shape, 1)
    l0 = acc0[:, V_ONE_LANE[0]:V_ONE_LANE[0] + 1]
    l1 = acc1[:, V_ONE_LANE[1]:V_ONE_LANE[1] + 1]
    o_ref[...] = jnp.where(lane < MLA_V, acc0 / l0, acc1 / l1).astype(o_ref.dtype)


def _attn_call(q, k, v, n_batch, seq):
    tq, tk = ATTN_TQ, ATTN_TK
    nq = seq // tq
    hp = MLA_HEADS // 2
    resident = pl.BlockSpec((seq, 2 * HEAD_PAD), lambda b, h, i: (b, h), pipeline_mode=pl.Buffered(1))
    return pl.pallas_call(
        functools.partial(_attn_kernel, tq=tq, tk=tk),
        grid=(n_batch, hp, nq),
        in_specs=[
            pl.BlockSpec((tq, 2 * HEAD_PAD), lambda b, h, i: (b * nq + i, h)),
            resident,
            resident,
        ],
        out_specs=pl.BlockSpec((tq, LANE), lambda b, h, i: (b * nq + i, h)),
        out_shape=jax.ShapeDtypeStruct((n_batch * seq, hp * LANE), BF16),
        scratch_shapes=[
            pltpu.VMEM((2, tq, LANE), F32),
            pltpu.VMEM((2, tq, LANE), F32),
        ],
        compiler_params=_cparams(("parallel", "parallel", "arbitrary")),
        name="attn",
    )(q, k, v)


def _mla_sample_kernel(q_ref, cn_ref, kn_ref, cp_ref, kp_ref, wuk_ref, wuv_ref, o_ref, *, past, dec):
    cp = cp_ref[0].astype(BF16)
    kp = kp_ref[0].astype(BF16)
    cn = cn_ref[...].astype(BF16)
    kn = kn_ref[...].astype(BF16)
    qpos = (past + lax.broadcasted_iota(jnp.int32, (dec, past), 0)) // CHUNK
    vis_p = (lax.broadcasted_iota(jnp.int32, (dec, past), 1) // CHUNK) <= qpos
    qpos_n = (past + lax.broadcasted_iota(jnp.int32, (dec, dec), 0)) // CHUNK
    vis_n = ((past + lax.broadcasted_iota(jnp.int32, (dec, dec), 1)) // CHUNK) <= qpos_n
    out = jnp.zeros((dec, MLA_HEADS * MLA_V), F32)
    for h in range(MLA_HEADS):
        qn = q_ref[:, h * HEAD_PAD:h * HEAD_PAD + MLA_NOPE]
        qp = q_ref[:, h * HEAD_PAD + MLA_NOPE:h * HEAD_PAD + MLA_NOPE + MLA_ROPE]
        q_lat = _dot(qn, wuk_ref[h]).astype(BF16)
        s_p = jnp.where(vis_p, _dot_nt(q_lat, cp) + _dot_nt(qp, kp), NEG_INF)
        s_n = jnp.where(vis_n, _dot_nt(q_lat, cn) + _dot_nt(qp, kn), NEG_INF)
        m = jnp.maximum(jnp.max(s_p, axis=1, keepdims=True), jnp.max(s_n, axis=1, keepdims=True))
        e_p = jnp.exp2(s_p - m)
        e_n = jnp.exp2(s_n - m)
        l = jnp.sum(e_p, axis=1, keepdims=True) + jnp.sum(e_n, axis=1, keepdims=True)
        o_lat = (_dot(e_p.astype(BF16), cp) + _dot(e_n.astype(BF16), cn)) / l
        out = out + _dot(o_lat.astype(BF16), wuv_ref[h])
    o_ref[...] = out.astype(o_ref.dtype)


def _mla_sample_call(q, ckv, kpe, cache_ckv, cache_kpe, wuk, wuv, row0, n_seq, dec):
    past = cache_ckv.shape[1]
    blk0 = row0 // dec
    wide = MLA_HEADS * HEAD_PAD
    return pl.pallas_call(
        functools.partial(_mla_sample_kernel, past=past, dec=dec),
        grid=(n_seq,),
        in_specs=[
            pl.BlockSpec((dec, wide), lambda b: (blk0 + b, 0)),
            pl.BlockSpec((dec, KV_LORA), lambda b: (blk0 + b, 0)),
            pl.BlockSpec((dec, MLA_ROPE), lambda b: (blk0 + b, 0)),
            pl.BlockSpec((1, past, KV_LORA), lambda b: (b, 0, 0)),
            pl.BlockSpec((1, past, MLA_ROPE), lambda b: (b, 0, 0)),
            pl.BlockSpec((MLA_HEADS, MLA_NOPE, KV_LORA), lambda b: (0, 0, 0)),
            pl.BlockSpec((MLA_HEADS, KV_LORA, MLA_HEADS * MLA_V), lambda b: (0, 0, 0)),
        ],
        out_specs=pl.BlockSpec((dec, MLA_HEADS * MLA_V), lambda b: (b, 0)),
        out_shape=jax.ShapeDtypeStruct((n_seq * dec, MLA_HEADS * MLA_V), BF16),
        compiler_params=_cparams(("parallel",)),
        name="mla_sample",
    )(q, ckv, kpe, cache_ckv, cache_kpe, wuk, wuv)


def _prep_kernel(pr_ref, bnd_ref, mu_ref, w0_ref, a0_ref, kk_ref, ka_ref, rk_ref, w2_ref, a2_ref, g2_ref,
                 seg_ref, r_ref, lw_ref, kh_ref, v_ref, na_ref, b_ref, bonus_ref, g_ref, *, tm):
    ng = tm // SHIFT_GROUP
    pr = pr_ref[...]
    pr3 = pr.reshape(ng, SHIFT_GROUP, RWKV_IN)
    rolled = pltpu.roll(pr3, 1, 1)
    first = lax.broadcasted_iota(jnp.int32, pr3.shape, 1) == 0
    prev = jnp.where(first, bnd_ref[...], rolled).reshape(tm, RWKV_IN)
    u = pr + mu_ref[...] * (prev - pr)
    o1, o2, o3 = RWKV_DIM, 2 * RWKV_DIM, 3 * RWKV_DIM
    o4, o5 = o3 + DECAY_LORA, o3 + DECAY_LORA + AAA_LORA
    r, k, v = u[:, :o1], u[:, o1:o2], u[:, o2:o3]
    w_lo, a_lo, g_lo = u[:, o3:o4], u[:, o4:o5], u[:, o5:]
    wl = w0_ref[...] + _dot(jnp.tanh(w_lo).astype(BF16), w2_ref[...])
    lw_ref[...] = -math.exp(-0.5) * _sigmoid(wl)
    a = _sigmoid(a0_ref[...] + _dot(a_lo.astype(BF16), a2_ref[...]))
    g_ref[...] = _dot(_sigmoid(g_lo).astype(BF16), g2_ref[...])
    seg = seg_ref[...]
    kk = k * kk_ref[...]
    kk = kk / jnp.maximum(jnp.sqrt(_dot_exact_rhs(kk * kk, seg)), 1e-12)
    kh = k * (1.0 + (a - 1.0) * ka_ref[...])
    r_ref[...] = r
    kh_ref[...] = kh
    v_ref[...] = v
    na_ref[...] = -kk
    b_ref[...] = kk * a
    bonus_ref[...] = _dot_exact_rhs(r * kh * rk_ref[...], seg) * v


def _prep_call(pr, bnd, mu, w0, a0, k_k, k_a, r_k, w2, a2, g2, seg):
    T = pr.shape[0]
    tm = TOKEN_TILE
    row = lambda i: (i, 0)
    full = lambda i: (0, 0)
    vec = pl.BlockSpec((1, RWKV_DIM), full)
    out = pl.BlockSpec((tm, RWKV_DIM), row)
    return pl.pallas_call(
        functools.partial(_prep_kernel, tm=tm),
        grid=(T // tm,),
        in_specs=[
            pl.BlockSpec((tm, RWKV_IN), row),
            pl.BlockSpec((tm // SHIFT_GROUP, 1, RWKV_IN), lambda i: (i, 0, 0)),
            pl.BlockSpec((1, RWKV_IN), full),
            vec, vec, vec, vec, vec,
            pl.BlockSpec((DECAY_LORA, RWKV_DIM), full),
            pl.BlockSpec((AAA_LORA, RWKV_DIM), full),
            pl.BlockSpec((GATE_LORA, RWKV_DIM), full),
            pl.BlockSpec((RWKV_DIM, RWKV_DIM), full),
        ],
        out_specs=[out] * 8,
        out_shape=[jax.ShapeDtypeStruct((T, RWKV_DIM), F32)] * 8,
        compiler_params=_cparams(("parallel",)),
        name="rwkv_prep",
    )(pr, bnd, mu, w0, a0, k_k, k_a, r_k, w2, a2, g2, seg)


def _wkv_kernel(r_ref, lw_ref, k_ref, v_ref, a_ref, b_ref, h0_ref, y_ref, hT_ref, h_scr, *, C, n_sub):
    GW = WKV_GROUP * RWKV_N
    R = WKV_GROUP * C
    n_grp = RWKV_HEADS // WKV_GROUP
    n_lev = int(round(math.log2(C))) - 1
    c = pl.program_id(1)

    @pl.when(c == 0)
    def _():
        h_scr[...] = h0_ref[0]

    row = lax.broadcasted_iota(jnp.int32, (C, C), 0)
    col = lax.broadcasted_iota(jnp.int32, (C, C), 1)
    tri = jnp.where(col <= row, 1.0, 0.0).astype(BF16)
    rr = lax.broadcasted_iota(jnp.int32, (R, R), 0)
    cc = lax.broadcasted_iota(jnp.int32, (R, R), 1)
    same = (rr // C) == (cc // C)
    strict = same & (cc < rr)
    lower = same & (cc <= rr)
    eye_r = jnp.where(rr == cc, 1.0, 0.0)
    keep = (lax.broadcasted_iota(jnp.int32, (R, GW), 0) // C
            == lax.broadcasted_iota(jnp.int32, (R, GW), 1) // RWKV_N)
    eye_g = (lax.broadcasted_iota(jnp.int32, (GW, GW), 0) == lax.broadcasted_iota(jnp.int32, (GW, GW), 1))
    eye_g_bf = jnp.where(eye_g, 1.0, 0.0).astype(BF16)

    def stack(x4):
        return jnp.where(keep, jnp.concatenate([x4] * WKV_GROUP, axis=0), jnp.zeros((), x4.dtype))

    sls = [slice(g * GW, (g + 1) * GW) for g in range(n_grp)]
    J = [(ci, g) for ci in range(n_sub) for g in range(n_grp)]
    ops, p_end = {}, {}
    for ci in range(n_sub):
        rows = slice(ci * C, (ci + 1) * C)
        lw = lw_ref[rows, :]
        cum = _dot_exact_lhs(tri, lw)
        cum_end = cum[C - 1:C, :]
        e_neg = jnp.exp(-cum)
        e_end = jnp.exp(cum_end - cum)
        b_in = b_ref[rows, :]
        k_in = k_ref[rows, :]
        full = ((a_ref[rows, :] * jnp.exp(cum - lw)).astype(BF16),
                (r_ref[rows, :] * jnp.exp(cum)).astype(BF16),
                (b_in * e_neg).astype(BF16), (k_in * e_neg).astype(BF16),
                (b_in * e_end).astype(BF16), (k_in * e_end).astype(BF16),
                v_ref[rows, :].astype(BF16))
        p_end[ci] = jnp.exp(cum_end)
        for g in range(n_grp):
            ops[ci, g] = [stack(t[:, sls[g]]) for t in full]
    a_s, r_s, b_s, k_s, be_s, ke_s, v_s = [{j: ops[j][i] for j in J} for i in range(7)]
    m = {j: _dot_nt(jnp.concatenate([a_s[j], r_s[j]], axis=0), jnp.concatenate([b_s[j], k_s[j]], axis=0))
         for j in J}
    l_ab = {j: jnp.where(strict, m[j][:R, :R], 0.0) for j in J}
    a_ak = {j: jnp.where(strict, m[j][:R, R:], 0.0).astype(BF16) for j in J}
    a_rb = {j: jnp.where(lower, m[j][R:, :R], 0.0).astype(BF16) for j in J}
    a_rk = {j: jnp.where(lower, m[j][R:, R:], 0.0).astype(BF16) for j in J}
    t_inv = {j: eye_r + l_ab[j] for j in J}
    l_pow = {j: l_ab[j].astype(BF16) for j in J}
    for _ in range(n_lev):
        l_pow = {j: _dot(l_pow[j], l_pow[j]).astype(BF16) for j in J}
        t_inv = {j: t_inv[j] + _dot(t_inv[j].astype(BF16), l_pow[j]) for j in J}
    t_b = {j: t_inv[j].astype(BF16) for j in J}
    bke_t = {j: _dot_nt(eye_g_bf, jnp.concatenate([be_s[j], ke_s[j]], axis=0)).astype(BF16) for j in J}

    G = range(n_grp)
    h_cur = [h_scr[g] for g in G]
    for ci in range(n_sub):
        rows = slice(ci * C, (ci + 1) * C)
        h0_b = [h_cur[g].astype(BF16) for g in G]
        x = [_dot(a_s[ci, g], h0_b[g]) + _dot(a_ak[ci, g], v_s[ci, g]) for g in G]
        u = [_dot(t_b[ci, g], x[g].astype(BF16)).astype(BF16) for g in G]
        y_bd = [_dot(r_s[ci, g], h0_b[g]) + _dot(a_rb[ci, g], u[g]) + _dot(a_rk[ci, g], v_s[ci, g]) for g in G]
        h_add = [_dot(bke_t[ci, g], jnp.concatenate([u[g], v_s[ci, g]], axis=0)) for g in G]
        for g in G:
            y4 = y_bd[g][:C]
            for hh in range(1, WKV_GROUP):
                y4 = y4 + y_bd[g][hh * C:(hh + 1) * C]
            y_ref[rows, sls[g]] = y4
            p_col = jnp.sum(jnp.where(eye_g, p_end[ci][:, sls[g]], 0.0), axis=1, keepdims=True)
            h_cur[g] = p_col * h_cur[g] + h_add[g]

    for g in G:
        h_scr[g] = h_cur[g]

    @pl.when(c == pl.num_programs(1) - 1)
    def _():
        hT_ref[0] = h_scr[...]


def _wkv_call(arrs, h0, row0, n_seq, n_chunk, C, n_sub):
    rows = C * n_sub
    steps = n_chunk // n_sub
    blk0 = row0 // rows
    GW = WKV_GROUP * RWKV_N
    n_grp = RWKV_HEADS // WKV_GROUP
    tok = pl.BlockSpec((rows, RWKV_DIM), lambda b, c: (blk0 + b * steps + c, 0))
    out_tok = pl.BlockSpec((rows, RWKV_DIM), lambda b, c: (b * steps + c, 0))
    st = pl.BlockSpec((1, n_grp, GW, GW), lambda b, c: (b, 0, 0, 0))
    return pl.pallas_call(
        functools.partial(_wkv_kernel, C=C, n_sub=n_sub),
        grid=(n_seq, steps),
        in_specs=[tok] * 6 + [st],
        out_specs=[out_tok, st],
        out_shape=[
            jax.ShapeDtypeStruct((n_seq * n_chunk * C, RWKV_DIM), F32),
            jax.ShapeDtypeStruct((n_seq, n_grp, GW, GW), F32),
        ],
        scratch_shapes=[pltpu.VMEM((n_grp, GW, GW), F32)],
        compiler_params=_cparams(("parallel", "arbitrary")),
        name="wkv_c%d" % C,
    )(*arrs, h0)


def _to_blockdiag(h):
    B = h.shape[0]
    n_grp = RWKV_HEADS // WKV_GROUP
    hg = h.reshape(B, n_grp, WKV_GROUP, RWKV_N, RWKV_N)
    bd = jnp.einsum('bghkv,hi->bghkiv', hg, jnp.eye(WKV_GROUP, dtype=h.dtype))
    return bd.reshape(B, n_grp, WKV_GROUP * RWKV_N, WKV_GROUP * RWKV_N)


def _from_blockdiag(bd):
    B, n_grp = bd.shape[:2]
    x = bd.reshape(B, n_grp, WKV_GROUP, RWKV_N, WKV_GROUP, RWKV_N)
    return jnp.einsum('bghkhv->bghkv', x).reshape(B, RWKV_HEADS, RWKV_N, RWKV_N)


def _mix_kernel(x_ref, attn_ref, y_ref, bonus_ref, g_ref, seg_ref, lng_ref, lnb_ref, woa_ref, wob_ref,
                g1_ref, b1_ref, wr_ref, br_ref, h_ref, hb_ref, route_ref):
    seg = seg_ref[...]
    y = y_ref[...]
    inv_n = 1.0 / RWKV_N
    yc = y - _dot_exact_rhs(y, seg) * inv_n
    var = _dot_exact_rhs(yc * yc, seg) * inv_n
    yn = yc * lax.rsqrt(var + GN_EPS) * lng_ref[...] + lnb_ref[...]
    rw = ((yn + bonus_ref[...]) * g_ref[...]).astype(BF16)
    m = _dot(attn_ref[...], woa_ref[...]) + _dot(rw, wob_ref[...])
    h = _layer_norm(DN_ALPHA * x_ref[...] + m, g1_ref[...], b1_ref[...])
    h_ref[...] = h
    hb_ref[...] = h.astype(BF16)

    h_hi = h.astype(BF16)
    h_lo = (h - h_hi.astype(F32)).astype(BF16)
    logits = _dot(h_hi, wr_ref[0]) + _dot(h_lo, wr_ref[0]) + _dot(h_hi, wr_ref[1]) + br_ref[...]
    lane = lax.broadcasted_iota(jnp.int32, logits.shape, 1)
    big = jnp.int32(LANE)
    gl = jnp.where(lane < N_GROUPS, logits, NEG_INF)
    gmax = jnp.max(gl, axis=1, keepdims=True)
    grp = jnp.min(jnp.where(gl == gmax, lane, big), axis=1, keepdims=True)
    p_grp = 1.0 / jnp.sum(jnp.exp(gl - gmax), axis=1, keepdims=True)
    e_idx = lane - N_GROUPS
    in_grp = (lane >= N_GROUPS) & (lane < N_GROUPS + N_EXPERTS) & ((e_idx // EXPERTS_PER_GROUP) == grp)
    el = jnp.where(in_grp, logits, NEG_INF)
    m1 = jnp.max(el, axis=1, keepdims=True)
    i1 = jnp.min(jnp.where(el == m1, lane, big), axis=1, keepdims=True)
    el2 = jnp.where(lane == i1, NEG_INF, el)
    m2 = jnp.max(el2, axis=1, keepdims=True)
    i2 = jnp.min(jnp.where(el2 == m2, lane, big), axis=1, keepdims=True)
    t = jnp.exp(m2 - m1)
    g1 = p_grp / (1.0 + t)
    g2 = g1 * t
    e1 = (i1 - N_GROUPS).astype(F32)
    e2 = (i2 - N_GROUPS).astype(F32)
    route_ref[...] = jnp.where(lane == 0, e1, jnp.where(lane == 1, e2, jnp.where(lane == 2, g1,
                               jnp.where(lane == 3, g2, 0.0))))


def _mix_call(x, attn, y, bonus, g, seg, lnx_g, lnx_b, woa, wob, ln1_g, ln1_b, wr, br):
    T = x.shape[0]
    tm = TOKEN_TILE
    row = lambda i: (i, 0)
    full = lambda i: (0, 0)
    half = pl.BlockSpec((tm, RWKV_DIM), row)
    vec5 = pl.BlockSpec((1, RWKV_DIM), full)
    vec10 = pl.BlockSpec((1, D_MODEL), full)
    return pl.pallas_call(
        _mix_kernel,
        grid=(T // tm,),
        in_specs=[
            pl.BlockSpec((tm, D_MODEL), row), half, half, half, half,
            pl.BlockSpec((RWKV_DIM, RWKV_DIM), full), vec5, vec5,
            pl.BlockSpec((RWKV_DIM, D_MODEL), full), pl.BlockSpec((RWKV_DIM, D_MODEL), full),
            vec10, vec10,
            pl.BlockSpec((2, D_MODEL, LANE), lambda i: (0, 0, 0)), pl.BlockSpec((1, LANE), full),
        ],
        out_specs=[pl.BlockSpec((tm, D_MODEL), row), pl.BlockSpec((tm, D_MODEL), row),
                   pl.BlockSpec((tm, LANE), row)],
        out_shape=[jax.ShapeDtypeStruct((T, D_MODEL), F32), jax.ShapeDtypeStruct((T, D_MODEL), BF16),
                   jax.ShapeDtypeStruct((T, LANE), F32)],
        compiler_params=_cparams(("parallel",)),
        name="mix",
    )(x, attn, y, bonus, g, seg, lnx_g, lnx_b, woa, wob, ln1_g, ln1_b, wr, br)


def _expert_kernel(be_ref, nu_ref, xs_ref, wgu_ref, wd_ref, ys_ref):
    i = pl.program_id(0)

    @pl.when(i < nu_ref[0])
    def _():
        gu = _dot(xs_ref[...], wgu_ref[0])
        gate, up = gu[:, :D_EXPERT], gu[:, D_EXPERT:]
        act = (gate * _sigmoid(gate) * up).astype(BF16)
        ys_ref[...] = _dot(act, wd_ref[0])

    @pl.when(i >= nu_ref[0])
    def _():
        ys_ref[...] = jnp.zeros(ys_ref.shape, ys_ref.dtype)


def _expert_call(block_e, n_used, xs, wgu, wd):
    n_blk = xs.shape[0] // MOE_BLK
    grid_spec = pltpu.PrefetchScalarGridSpec(
        num_scalar_prefetch=2,
        grid=(n_blk,),
        in_specs=[
            pl.BlockSpec((MOE_BLK, D_MODEL), lambda i, be, nu: (i, 0)),
            pl.BlockSpec((1, D_MODEL, 2 * D_EXPERT), lambda i, be, nu: (be[i], 0, 0)),
            pl.BlockSpec((1, D_EXPERT, D_MODEL), lambda i, be, nu: (be[i], 0, 0)),
        ],
        out_specs=pl.BlockSpec((MOE_BLK, D_MODEL), lambda i, be, nu: (i, 0)),
    )
    return pl.pallas_call(
        _expert_kernel,
        grid_spec=grid_spec,
        out_shape=jax.ShapeDtypeStruct((n_blk * MOE_BLK, D_MODEL), F32),
        compiler_params=_cparams(("arbitrary",)),
        name="experts",
    )(block_e, n_used, xs, wgu, wd)


def _combine_kernel(h_ref, ya_ref, yb_ref, route_ref, g2_ref, b2_ref, o_ref):
    route = route_ref[...]
    f = ya_ref[...] * route[:, 2:3] + yb_ref[...] * route[:, 3:4]
    o_ref[...] = _layer_norm(DN_ALPHA * h_ref[...] + f, g2_ref[...], b2_ref[...])


def _combine_call(h, ya, yb, route, ln2_g, ln2_b):
    T = h.shape[0]
    tm = TOKEN_TILE
    row = lambda i: (i, 0)
    full = lambda i: (0, 0)
    big = pl.BlockSpec((tm, D_MODEL), row)
    return pl.pallas_call(
        _combine_kernel,
        grid=(T // tm,),
        in_specs=[big, big, big, pl.BlockSpec((tm, LANE), row),
                  pl.BlockSpec((1, D_MODEL), full), pl.BlockSpec((1, D_MODEL), full)],
        out_specs=big,
        out_shape=jax.ShapeDtypeStruct((T, D_MODEL), F32),
        compiler_params=_cparams(("parallel",)),
        name="combine",
    )(h, ya, yb, route, ln2_g, ln2_b)


def _prep_weights(w_in, w_uq, w_ukv):
    half = MLA_ROPE // 2
    kpe_w = w_in[:, Q_LORA + KV_LORA:MLA_IN]
    kpe_b = jnp.concatenate([-kpe_w[:, half:], kpe_w[:, :half]], axis=1)
    w1 = jnp.concatenate([w_in[:, :Q_LORA + KV_LORA], kpe_w, kpe_b,
                          jnp.zeros((D_MODEL, 64), F32), w_in[:, MLA_IN:]], axis=1).astype(BF16)
    pad_q = jnp.zeros((Q_LORA, MLA_HEADS, HEAD_PAD - MLA_NOPE - MLA_ROPE), F32)
    wqa = jnp.concatenate([w_uq, pad_q], axis=2).reshape(Q_LORA, -1).astype(BF16)
    rot = jnp.concatenate([jnp.zeros((Q_LORA, MLA_HEADS, MLA_NOPE), F32),
                           -w_uq[:, :, MLA_NOPE + half:], w_uq[:, :, MLA_NOPE:MLA_NOPE + half], pad_q], axis=2)
    wqb = rot.reshape(Q_LORA, -1).astype(BF16)
    w_uk, w_uv = w_ukv[:, :, :MLA_NOPE], w_ukv[:, :, MLA_NOPE:]
    wk = jnp.concatenate([w_uk, jnp.zeros((KV_LORA, MLA_HEADS, HEAD_PAD - MLA_NOPE), F32)], axis=2)
    wk = wk.reshape(KV_LORA, -1).astype(BF16)
    pk_np = np.zeros((MLA_ROPE, MLA_HEADS * HEAD_PAD), np.float32)
    for h in range(MLA_HEADS):
        for i in range(MLA_ROPE):
            pk_np[i, h * HEAD_PAD + MLA_NOPE + i] = 1.0
    pk = jnp.asarray(pk_np).astype(BF16)
    zv = jnp.zeros((KV_LORA, MLA_HEADS // 2, MLA_V), F32)
    wv4 = w_uv.reshape(KV_LORA, MLA_HEADS // 2, 2, MLA_V)
    wv = jnp.stack([jnp.concatenate([wv4[:, :, 0], zv], axis=2),
                    jnp.concatenate([zv, wv4[:, :, 1]], axis=2)], axis=2)
    wv = wv.reshape(KV_LORA, -1).astype(BF16)
    wuk = jnp.transpose(w_uk, (1, 2, 0)).astype(BF16)
    wuv_np = np.zeros((MLA_HEADS, MLA_HEADS * MLA_V), np.float32)
    for h in range(MLA_HEADS):
        wuv_np[h, h * MLA_V:(h + 1) * MLA_V] = 1.0
    wuv = jnp.transpose(w_uv, (1, 0, 2))
    wuv = (jnp.tile(wuv, (1, 1, MLA_HEADS)) * jnp.asarray(wuv_np)[:, None, :]).astype(BF16)
    return w1, wqa, wqb, wk, pk, wv, wuk, wuv


def _rope_table(pos):
    inv = ROPE_BASE ** (-jnp.arange(0, MLA_ROPE, 2, dtype=F32) / MLA_ROPE)
    ang = pos.astype(F32)[:, None] * inv[None, :]
    cos, sin = jnp.cos(ang), jnp.sin(ang)
    n = pos.shape[0]
    one = jnp.ones((n, MLA_NOPE), F32)
    z32 = jnp.zeros((n, HEAD_PAD - MLA_NOPE - MLA_ROPE), F32)
    z64 = jnp.zeros((n, MLA_NOPE), F32)
    cq = jnp.concatenate([one, cos, cos, z32], axis=1) * (MLA_SCALE * LOG2E)
    sq = jnp.concatenate([z64, sin, sin, z32], axis=1) * (MLA_SCALE * LOG2E)
    kt = jnp.concatenate([cos, cos, sin, sin, z64], axis=1)
    return jnp.concatenate([cq, sq, kt], axis=1)


def _seg_ones():
    idx = np.arange(RWKV_DIM) // RWKV_N
    return jnp.asarray((idx[:, None] == idx[None, :]).astype(np.float32)).astype(BF16)


def _dispatch(route, t_total):
    A = t_total * TOP_K
    flat_e = route[:, :TOP_K].astype(jnp.int32).reshape(A)
    onehot = (flat_e[:, None] == jnp.arange(N_EXPERTS, dtype=jnp.int32)[None, :]).astype(jnp.int32)
    csum = jnp.cumsum(onehot, axis=0)
    counts = csum[-1]
    rank = jnp.take_along_axis(csum, flat_e[:, None], axis=1)[:, 0] - 1
    blocks_per_e = (counts + MOE_BLK - 1) // MOE_BLK
    blk_end = jnp.cumsum(blocks_per_e)
    blk_start = blk_end - blocks_per_e
    dest = blk_start[flat_e] * MOE_BLK + rank
    n_blk = -(-A // MOE_BLK) + N_EXPERTS
    block_e = jnp.minimum(jnp.searchsorted(blk_end, jnp.arange(n_blk, dtype=jnp.int32), side='right'),
                          N_EXPERTS - 1).astype(jnp.int32)
    n_used = blk_end[-1:].astype(jnp.int32)
    slot_tok = jnp.zeros((n_blk * MOE_BLK,), jnp.int32).at[dest].set(
        jnp.arange(A, dtype=jnp.int32) // TOP_K)
    return dest.reshape(t_total, TOP_K), block_e, n_used, slot_tok


def kernel(x_prompt, x_sample, cache_ckv, cache_kpe, state_wkv, state_shift, w_in, q_norm_g, kv_norm_g, w_uq,
           w_ukv, mu_shift, w0, w2, a0, a2, g2, k_k, k_a, r_k, lnx_g, lnx_b, w_o, ln1_g, ln1_b, w_gr, b_gr,
           w_er, b_er, w_eg, w_eu, w_ed, ln2_g, ln2_b):
    B, S, D = x_prompt.shape
    DB, DS, _ = x_sample.shape
    past = cache_ckv.shape[2]
    Tp, Ts = B * S, DB * DS
    T = Tp + Ts
    assert D == D_MODEL and DS == SHIFT_GROUP and S % ATTN_TQ == 0 and S % (CHUNK * WKV_SUB) == 0
    assert Tp % TOKEN_TILE == 0 and T % TOKEN_TILE == 0 and w_in.shape[0] == DEPTH

    l = 0
    x = jnp.concatenate([x_prompt.reshape(Tp, D), x_sample.reshape(Ts, D)], axis=0)
    w1, wqa, wqb, wk, pk, wv, wuk, wuv = _prep_weights(w_in[l], w_uq[l], w_ukv[l])
    pos = jnp.concatenate([jnp.tile(jnp.arange(S, dtype=jnp.int32), B),
                           jnp.tile(past + jnp.arange(DS, dtype=jnp.int32), DB)])
    rope = _rope_table(pos)

    q, kcat, vcat, ckv, kpe, pr = _proj_call(x, rope, w1, q_norm_g[l][None], kv_norm_g[l][None],
                                             wqa, wqb, wk, pk, wv)

    attn_p = _attn_call(q, kcat, vcat, B, S)
    attn_s = _mla_sample_call(q, ckv, kpe, cache_ckv[l], cache_kpe[l], wuk, wuv, Tp, DB, DS)
    attn = jnp.concatenate([attn_p, attn_s], axis=0)

    last_rows = pr[SHIFT_GROUP - 1::SHIFT_GROUP]
    bnd = jnp.concatenate([jnp.zeros((1, RWKV_IN), F32), last_rows[:-1]], axis=0)
    gidx = jnp.arange(T // SHIFT_GROUP)
    seq_start = (gidx < Tp // SHIFT_GROUP) & (gidx % (S // SHIFT_GROUP) == 0)
    bnd = jnp.where(seq_start[:, None], 0.0, bnd)
    bnd = jnp.concatenate([bnd[:Tp // SHIFT_GROUP], state_shift[l]], axis=0)[:, None, :]

    seg = _seg_ones()
    vec = lambda a: a.reshape(1, -1)
    r, lw, kh, v, na, b, bonus, g = _prep_call(
        pr, bnd, vec(mu_shift[l]), vec(w0[l]), vec(a0[l]), vec(k_k[l]), vec(k_a[l]), vec(r_k[l]),
        w2[l].astype(BF16), a2[l].astype(BF16), g2[l].astype(BF16), seg)

    scan_in = (r, lw, kh, v, na, b)
    h0_p = jnp.zeros((B, RWKV_HEADS // WKV_GROUP, WKV_GROUP * RWKV_N, WKV_GROUP * RWKV_N), F32)
    y_p, hT_p = _wkv_call(scan_in, h0_p, 0, B, S // CHUNK, CHUNK, WKV_SUB)
    h0_s = _to_blockdiag(jnp.swapaxes(state_wkv[l], -1, -2))
    y_s, hT_s = _wkv_call(scan_in, h0_s, Tp, DB, 1, DS, 1)
    y = jnp.concatenate([y_p, y_s], axis=0)

    wo_b = w_o[l].astype(BF16)
    wr = jnp.concatenate([w_gr[l], w_er[l], jnp.zeros((D, LANE - N_GROUPS - N_EXPERTS), F32)], axis=1)
    wr_hi = wr.astype(BF16)
    wr_lo = (wr - wr_hi.astype(F32)).astype(BF16)
    br = jnp.concatenate([b_gr[l], b_er[l], jnp.zeros((LANE - N_GROUPS - N_EXPERTS,), F32)])[None]
    h, hb, route = _mix_call(x, attn, y, bonus, g, seg, vec(lnx_g[l]), vec(lnx_b[l]),
                             wo_b[:MLA_HEADS * MLA_V], wo_b[MLA_HEADS * MLA_V:], vec(ln1_g[l]), vec(ln1_b[l]),
                             jnp.stack([wr_hi, wr_lo]), br)

    dest, block_e, n_used, slot_tok = _dispatch(route, T)
    xs = hb[slot_tok]
    wgu = jnp.concatenate([w_eg[l], w_eu[l]], axis=2).astype(BF16)
    ys = _expert_call(block_e, n_used, xs, wgu, w_ed[l].astype(BF16))
    out = _combine_call(h, ys[dest[:, 0]], ys[dest[:, 1]], route, vec(ln2_g[l]), vec(ln2_b[l]))

    y_prompt = out[:Tp].reshape(B, S, D)
    y_sample = out[Tp:].reshape(DB, DS, D)
    p_ckv = ckv[:Tp].reshape(1, B, S, KV_LORA)
    p_kpe = kpe[:Tp].reshape(1, B, S, MLA_ROPE)
    s_ckv = ckv[Tp:].reshape(1, DB, DS, KV_LORA)
    s_kpe = kpe[Tp:].reshape(1, DB, DS, MLA_ROPE)
    p_wkv = jnp.swapaxes(_from_blockdiag(hT_p), -1, -2)[None]
    s_wkv = jnp.swapaxes(_from_blockdiag(hT_s), -1, -2)[None]
    p_sh = pr[:Tp].reshape(B, S, RWKV_IN)[:, -1][None]
    s_sh = pr[Tp:].reshape(DB, DS, RWKV_IN)[:, -1][None]
    return (y_prompt, y_sample, p_ckv, p_kpe, p_wkv, p_sh, s_ckv, s_kpe, s_wkv, s_sh)
```

```python
import functools
import math

import numpy as np
import jax
import jax.numpy as jnp
from jax import lax
from jax.experimental import pallas as pl
from jax.experimental.pallas import tpu as pltpu
from jax.experimental.pallas import tpu_sc as plsc

F32 = jnp.float32
BF16 = jnp.bfloat16

D_MODEL = 1024
CHUNK = 64
MLA_HEADS = 8
MLA_NOPE = 64
MLA_ROPE = 32
MLA_V = 64
Q_LORA = 384
KV_LORA = 256
ROPE_BASE = 10000.0
MLA_IN = Q_LORA + KV_LORA + MLA_ROPE
MLA_SCALE = (MLA_NOPE + MLA_ROPE) ** -0.5
RWKV_HEADS = 8
RWKV_N = 64
RWKV_DIM = RWKV_HEADS * RWKV_N
DECAY_LORA = 64
AAA_LORA = 64
GATE_LORA = 128
RWKV_IN = 3 * RWKV_DIM + DECAY_LORA + AAA_LORA + GATE_LORA
N_GROUPS = 4
EXPERTS_PER_GROUP = 8
N_EXPERTS = N_GROUPS * EXPERTS_PER_GROUP
TOP_K = 2
D_EXPERT = 256
MOE_BLK = 256
LN_EPS = 1e-5
RMS_EPS = 1e-6
GN_EPS = 64e-5
NEG_INF = -1e30
DEPTH = 1
DN_ALPHA = (2 * DEPTH) ** 0.25

LANE = 128
HEAD_PAD = 128
PROJ_W = 768 + RWKV_IN
SHIFT_GROUP = 32
TOKEN_TILE = 512
ATTN_TQ = 1024
ATTN_TK = 512
V_ONE_LANE = (MLA_V, 0)
LOG2E = math.log2(math.e)
VMEM_LIMIT = 48 * 1024 * 1024
SC_CORES = 2
SC_SUBCORES = 16
SC_WINDOW = 32
WKV_GROUP = 4
WKV_SUB = 4


def _cparams(sem):
    return pltpu.CompilerParams(dimension_semantics=sem, vmem_limit_bytes=VMEM_LIMIT)


def _split3(x):
    hi = x.astype(BF16)
    r1 = x - hi.astype(F32)
    mid = r1.astype(BF16)
    lo = (r1 - mid.astype(F32)).astype(BF16)
    return hi, mid, lo


def _dot(a, b):
    return jnp.dot(a, b, preferred_element_type=F32)


def _dot_nt(a, b):
    return lax.dot_general(a, b, (((1,), (1,)), ((), ())), preferred_element_type=F32)


def _dot_exact_rhs(x, w):
    hi, mid, lo = _split3(x)
    return _dot(hi, w) + _dot(mid, w) + _dot(lo, w)


def _dot_exact_lhs(w, x):
    hi, mid, lo = _split3(x)
    return _dot(w, hi) + _dot(w, mid) + _dot(w, lo)


def _sigmoid(x):
    return 1.0 / (1.0 + jnp.exp(-x))


def _layer_norm(x, g, b):
    xc = x - jnp.mean(x, -1, keepdims=True)
    var = jnp.mean(xc * xc, -1, keepdims=True)
    return xc * lax.rsqrt(var + LN_EPS) * g + b


def _proj_kernel(xp_ref, xs_ref, rope_ref, w1_ref, gq_ref, gkv_ref, wqa_ref, wqb_ref, wk_ref, pk_ref, wv_ref,
                 q_ref, k_ref, v_ref, ckvp_ref, ckvs_ref, kpep_ref, kpes_ref, pr_ref, last_ref, *, n_p, tm):
    i = pl.program_id(0)
    is_p = i < n_p
    x = jnp.where(is_p, xp_ref[...], xs_ref[...]).astype(BF16)
    proj = _dot(x, w1_ref[...])
    c_q = proj[:, :Q_LORA]
    c_kv = proj[:, Q_LORA:Q_LORA + KV_LORA]
    kp = proj[:, 640:768]
    pr = proj[:, 768:]
    pr_ref[...] = pr
    last_ref[...] = pr.reshape(tm // SHIFT_GROUP, SHIFT_GROUP, RWKV_IN)[:, SHIFT_GROUP - 1, :]

    cqn = c_q * lax.rsqrt(jnp.mean(c_q * c_q, -1, keepdims=True) + RMS_EPS) * gq_ref[...]
    ckv = c_kv * lax.rsqrt(jnp.mean(c_kv * c_kv, -1, keepdims=True) + RMS_EPS) * gkv_ref[...]

    rope = rope_ref[...]
    cq = rope[:, :LANE]
    sq = rope[:, LANE:2 * LANE]
    kt = rope[:, 2 * LANE:]
    prod = kp * kt
    kpe = prod[:, :MLA_ROPE] + prod[:, MLA_ROPE:2 * MLA_ROPE]

    @pl.when(is_p)
    def _():
        ckvp_ref[...] = ckv
        kpep_ref[...] = kpe

    @pl.when(jnp.logical_not(is_p))
    def _():
        ckvs_ref[...] = ckv
        kpes_ref[...] = kpe

    cqb = cqn.astype(BF16)
    qa = _dot(cqb, wqa_ref[...])
    qb = _dot(cqb, wqb_ref[...])
    for h in range(MLA_HEADS):
        sl = slice(h * HEAD_PAD, (h + 1) * HEAD_PAD)
        q_ref[:, sl] = (qa[:, sl] * cq + qb[:, sl] * sq).astype(BF16)

    ckv_b = ckv.astype(BF16)
    k = _dot(ckv_b, wk_ref[...]) + _dot(kpe.astype(BF16), pk_ref[...])
    k_ref[...] = k.astype(BF16)
    lane = lax.broadcasted_iota(jnp.int32, (1, MLA_HEADS * HEAD_PAD), 1)
    odd = (lane // HEAD_PAD) % 2
    one_lane = jnp.where(odd == 1, V_ONE_LANE[1], V_ONE_LANE[0])
    v_one = jnp.where(lane % HEAD_PAD == one_lane, 1.0, 0.0)
    v_ref[...] = (_dot(ckv_b, wv_ref[...]) + v_one).astype(BF16)


def _split_rows(n_p):
    return (lambda i: (jnp.minimum(i, n_p - 1), 0)), (lambda i: (jnp.maximum(i - n_p, 0), 0))


def _proj_call(xp, xs, rope, rope_tiles, w1, gq, gkv, wqa, wqb, wk, pk, wv):
    Tp, Ts = xp.shape[0], xs.shape[0]
    T = Tp + Ts
    tm = TOKEN_TILE
    n_p = Tp // tm
    row = lambda i: (i, 0)
    full = lambda i: (0, 0)
    row_p, row_s = _split_rows(n_p)
    wide = MLA_HEADS * HEAD_PAD
    ng = tm // SHIFT_GROUP
    return pl.pallas_call(
        functools.partial(_proj_kernel, n_p=n_p, tm=tm),
        grid=(T // tm,),
        in_specs=[
            pl.BlockSpec((tm, D_MODEL), row_p),
            pl.BlockSpec((tm, D_MODEL), row_s),
            pl.BlockSpec((tm, 3 * LANE), lambda i: (jnp.where(i < n_p, i % rope_tiles, rope_tiles), 0)),
            pl.BlockSpec((D_MODEL, PROJ_W), full),
            pl.BlockSpec((1, Q_LORA), full),
            pl.BlockSpec((1, KV_LORA), full),
            pl.BlockSpec((Q_LORA, wide), full),
            pl.BlockSpec((Q_LORA, wide), full),
            pl.BlockSpec((KV_LORA, wide), full),
            pl.BlockSpec((MLA_ROPE, wide), full),
            pl.BlockSpec((KV_LORA, wide), full),
        ],
        out_specs=[
            pl.BlockSpec((tm, wide), row),
            pl.BlockSpec((tm, wide), row),
            pl.BlockSpec((tm, wide), row),
            pl.BlockSpec((tm, KV_LORA), row_p),
            pl.BlockSpec((tm, KV_LORA), row_s),
            pl.BlockSpec((tm, MLA_ROPE), row_p),
            pl.BlockSpec((tm, MLA_ROPE), row_s),
            pl.BlockSpec((tm, RWKV_IN), row),
            pl.BlockSpec((ng, RWKV_IN), row),
        ],
        out_shape=[
            jax.ShapeDtypeStruct((T, wide), BF16),
            jax.ShapeDtypeStruct((T, wide), BF16),
            jax.ShapeDtypeStruct((T, wide), BF16),
            jax.ShapeDtypeStruct((Tp, KV_LORA), F32),
            jax.ShapeDtypeStruct((Ts, KV_LORA), F32),
            jax.ShapeDtypeStruct((Tp, MLA_ROPE), F32),
            jax.ShapeDtypeStruct((Ts, MLA_ROPE), F32),
            jax.ShapeDtypeStruct((T, RWKV_IN), F32),
            jax.ShapeDtypeStruct((T // SHIFT_GROUP, RWKV_IN), F32),
        ],
        compiler_params=_cparams(("arbitrary",)),
        name="proj",
    )(xp, xs, rope, w1, gq, gkv, wqa, wqb, wk, pk, wv)


def _attn_kernel(q_ref, k_ref, v_ref, o_ref, m_scr, acc_scr, *, tq, tk):
    qi = pl.program_id(2)
    m_scr[...] = jnp.full(m_scr.shape, NEG_INF, F32)
    acc_scr[...] = jnp.zeros(acc_scr.shape, F32)
    reps = tk // LANE
    n_diag = tq // tk

    def kv_block(kj, masked):
        k0 = pl.multiple_of(kj * tk, tk)
        if masked:
            r = (qi * tq + lax.broadcasted_iota(jnp.int32, (tq, tk), 0)) // CHUNK
            c = (k0 + lax.broadcasted_iota(jnp.int32, (tq, tk), 1)) // CHUNK
            visible = c <= r
        H = range(2)
        sls = [slice(h * HEAD_PAD, (h + 1) * HEAD_PAD) for h in H]
        s = [_dot_nt(q_ref[:, sl], k_ref[pl.ds(k0, tk), sl]) for sl in sls]
        if masked:
            s = [jnp.where(visible, s[h], NEG_INF) for h in H]
        m_prev = [m_scr[h] for h in H]
        m_new = [jnp.maximum(m_prev[h], jnp.max(s[h], axis=1, keepdims=True)) for h in H]
        pexp = [jnp.exp2(s[h] - jnp.tile(m_new[h], (1, reps))).astype(BF16) for h in H]
        pv = [_dot(pexp[h], v_ref[pl.ds(k0, tk), sls[h]]) for h in H]
        for h in H:
            acc_scr[h] = jnp.exp2(m_prev[h] - m_new[h]) * acc_scr[h] + pv[h]
            m_scr[h] = m_new[h]

    def body(kj, carry):
        kv_block(kj, False)
        return carry

    lax.fori_loop(0, qi * n_diag, body, 0)
    for d in range(n_diag):
        kv_block(qi * n_diag + d, True)
    acc0, acc1 = acc_scr[0], acc_scr[1]
    lane = lax.broadcasted_iota(jnp.int32, acc0.shape, 1)
    l0 = acc0[:, V_ONE_LANE[0]:V_ONE_LANE[0] + 1]
    l1 = acc1[:, V_ONE_LANE[1]:V_ONE_LANE[1] + 1]
    o_ref[...] = jnp.where(lane < MLA_V, acc0 / l0, acc1 / l1).astype(o_ref.dtype)


def _attn_call(q, k, v, n_batch, seq):
    tq, tk = ATTN_TQ, ATTN_TK
    nq = seq // tq
    hp = MLA_HEADS // 2
    resident = pl.BlockSpec((seq, 2 * HEAD_PAD), lambda b, h, i: (b, h), pipeline_mode=pl.Buffered(1))
    return pl.pallas_call(
        functools.partial(_attn_kernel, tq=tq, tk=tk),
        grid=(n_batch, hp, nq),
        in_specs=[
            pl.BlockSpec((tq, 2 * HEAD_PAD), lambda b, h, i: (b * nq + i, h)),
            resident,
            resident,
        ],
        out_specs=pl.BlockSpec((tq, LANE), lambda b, h, i: (b * nq + i, h)),
        out_shape=jax.ShapeDtypeStruct((n_batch * seq, hp * LANE), BF16),
        scratch_shapes=[
            pltpu.VMEM((2, tq, LANE), F32),
            pltpu.VMEM((2, tq, LANE), F32),
        ],
        compiler_params=_cparams(("parallel", "parallel", "arbitrary")),
        name="attn",
    )(q, k, v)


def _mla_sample_kernel(q_ref, cn_ref, kn_ref, cp_ref, kp_ref, wuk_ref, wuv_ref, o_ref, *, past, dec):
    cp = cp_ref[0].astype(BF16)
    kp = kp_ref[0].astype(BF16)
    cn = cn_ref[...].astype(BF16)
    kn = kn_ref[...].astype(BF16)
    qpos = (past + lax.broadcasted_iota(jnp.int32, (dec, past), 0)) // CHUNK
    vis_p = (lax.broadcasted_iota(jnp.int32, (dec, past), 1) // CHUNK) <= qpos
    qpos_n = (past + lax.broadcasted_iota(jnp.int32, (dec, dec), 0)) // CHUNK
    vis_n = ((past + lax.broadcasted_iota(jnp.int32, (dec, dec), 1)) // CHUNK) <= qpos_n
    out = jnp.zeros((dec, MLA_HEADS * MLA_V), F32)
    for h in range(MLA_HEADS):
        qn = q_ref[:, h * HEAD_PAD:h * HEAD_PAD + MLA_NOPE]
        qp = q_ref[:, h * HEAD_PAD + MLA_NOPE:h * HEAD_PAD + MLA_NOPE + MLA_ROPE]
        q_lat = _dot(qn, wuk_ref[h]).astype(BF16)
        s_p = jnp.where(vis_p, _dot_nt(q_lat, cp) + _dot_nt(qp, kp), NEG_INF)
        s_n = jnp.where(vis_n, _dot_nt(q_lat, cn) + _dot_nt(qp, kn), NEG_INF)
        m = jnp.maximum(jnp.max(s_p, axis=1, keepdims=True), jnp.max(s_n, axis=1, keepdims=True))
        e_p = jnp.exp2(s_p - m)
        e_n = jnp.exp2(s_n - m)
        l = jnp.sum(e_p, axis=1, keepdims=True) + jnp.sum(e_n, axis=1, keepdims=True)
        o_lat = (_dot(e_p.astype(BF16), cp) + _dot(e_n.astype(BF16), cn)) / l
        out = out + _dot(o_lat.astype(BF16), wuv_ref[h])
    o_ref[...] = out.astype(o_ref.dtype)


def _mla_sample_call(q, ckv, kpe, cache_ckv, cache_kpe, wuk, wuv, row0, n_seq, dec):
    past = cache_ckv.shape[1]
    blk0 = row0 // dec
    wide = MLA_HEADS * HEAD_PAD
    return pl.pallas_call(
        functools.partial(_mla_sample_kernel, past=past, dec=dec),
        grid=(n_seq,),
        in_specs=[
            pl.BlockSpec((dec, wide), lambda b: (blk0 + b, 0)),
            pl.BlockSpec((dec, KV_LORA), lambda b: (b, 0)),
            pl.BlockSpec((dec, MLA_ROPE), lambda b: (b, 0)),
            pl.BlockSpec((1, past, KV_LORA), lambda b: (b, 0, 0)),
            pl.BlockSpec((1, past, MLA_ROPE), lambda b: (b, 0, 0)),
            pl.BlockSpec((MLA_HEADS, MLA_NOPE, KV_LORA), lambda b: (0, 0, 0)),
            pl.BlockSpec((MLA_HEADS, KV_LORA, MLA_HEADS * MLA_V), lambda b: (0, 0, 0)),
        ],
        out_specs=pl.BlockSpec((dec, MLA_HEADS * MLA_V), lambda b: (b, 0)),
        out_shape=jax.ShapeDtypeStruct((n_seq * dec, MLA_HEADS * MLA_V), BF16),
        compiler_params=_cparams(("parallel",)),
        name="mla_sample",
    )(q, ckv, kpe, cache_ckv, cache_kpe, wuk, wuv)


def _prep_kernel(pr_ref, bnd_ref, mu_ref, w0_ref, a0_ref, kk_ref, ka_ref, rk_ref, w2_ref, a2_ref, g2_ref,
                 seg_ref, r_ref, lw_ref, kh_ref, v_ref, na_ref, b_ref, bonus_ref, g_ref, *, tm):
    ng = tm // SHIFT_GROUP
    pr = pr_ref[...]
    pr3 = pr.reshape(ng, SHIFT_GROUP, RWKV_IN)
    rolled = pltpu.roll(pr3, 1, 1)
    first = lax.broadcasted_iota(jnp.int32, pr3.shape, 1) == 0
    prev = jnp.where(first, bnd_ref[...], rolled).reshape(tm, RWKV_IN)
    u = pr + mu_ref[...] * (prev - pr)
    o1, o2, o3 = RWKV_DIM, 2 * RWKV_DIM, 3 * RWKV_DIM
    o4, o5 = o3 + DECAY_LORA, o3 + DECAY_LORA + AAA_LORA
    r, k, v = u[:, :o1], u[:, o1:o2], u[:, o2:o3]
    w_lo, a_lo, g_lo = u[:, o3:o4], u[:, o4:o5], u[:, o5:]
    wl = w0_ref[...] + _dot(jnp.tanh(w_lo).astype(BF16), w2_ref[...])
    lw_ref[...] = -math.exp(-0.5) * _sigmoid(wl)
    a = _sigmoid(a0_ref[...] + _dot(a_lo.astype(BF16), a2_ref[...]))
    g_ref[...] = _dot(_sigmoid(g_lo).astype(BF16), g2_ref[...])
    seg = seg_ref[...]
    kk = k * kk_ref[...]
    kk = kk / jnp.maximum(jnp.sqrt(_dot_exact_rhs(kk * kk, seg)), 1e-12)
    kh = k * (1.0 + (a - 1.0) * ka_ref[...])
    r_ref[...] = r
    kh_ref[...] = kh
    v_ref[...] = v
    na_ref[...] = -kk
    b_ref[...] = kk * a
    bonus_ref[...] = _dot_exact_rhs(r * kh * rk_ref[...], seg) * v


def _prep_call(pr, bnd, mu, w0, a0, k_k, k_a, r_k, w2, a2, g2, seg):
    T = pr.shape[0]
    tm = TOKEN_TILE
    row = lambda i: (i, 0)
    full = lambda i: (0, 0)
    vec = pl.BlockSpec((1, RWKV_DIM), full)
    out = pl.BlockSpec((tm, RWKV_DIM), row)
    return pl.pallas_call(
        functools.partial(_prep_kernel, tm=tm),
        grid=(T // tm,),
        in_specs=[
            pl.BlockSpec((tm, RWKV_IN), row),
            pl.BlockSpec((tm // SHIFT_GROUP, 1, RWKV_IN), lambda i: (i, 0, 0)),
            pl.BlockSpec((1, RWKV_IN), full),
            vec, vec, vec, vec, vec,
            pl.BlockSpec((DECAY_LORA, RWKV_DIM), full),
            pl.BlockSpec((AAA_LORA, RWKV_DIM), full),
            pl.BlockSpec((GATE_LORA, RWKV_DIM), full),
            pl.BlockSpec((RWKV_DIM, RWKV_DIM), full),
        ],
        out_specs=[out] * 8,
        out_shape=[jax.ShapeDtypeStruct((T, RWKV_DIM), F32)] * 8,
        compiler_params=_cparams(("parallel",)),
        name="rwkv_prep",
    )(pr, bnd, mu, w0, a0, k_k, k_a, r_k, w2, a2, g2, seg)


def _wkv_kernel(r_ref, lw_ref, k_ref, v_ref, a_ref, b_ref, h0_ref, y_ref, hT_ref, h_scr, *, C, n_sub):
    GW = WKV_GROUP * RWKV_N
    R = WKV_GROUP * C
    n_grp = RWKV_HEADS // WKV_GROUP
    n_lev = int(round(math.log2(C))) - 1
    c = pl.program_id(1)

    @pl.when(c == 0)
    def _():
        h_scr[...] = h0_ref[0]

    row = lax.broadcasted_iota(jnp.int32, (C, C), 0)
    col = lax.broadcasted_iota(jnp.int32, (C, C), 1)
    tri = jnp.where(col <= row, 1.0, 0.0).astype(BF16)
    rr = lax.broadcasted_iota(jnp.int32, (R, R), 0)
    cc = lax.broadcasted_iota(jnp.int32, (R, R), 1)
    same = (rr // C) == (cc // C)
    strict = same & (cc < rr)
    lower = same & (cc <= rr)
    eye_r = jnp.where(rr == cc, 1.0, 0.0)
    keep = (lax.broadcasted_iota(jnp.int32, (R, GW), 0) // C
            == lax.broadcasted_iota(jnp.int32, (R, GW), 1) // RWKV_N)
    eye_g = (lax.broadcasted_iota(jnp.int32, (GW, GW), 0) == lax.broadcasted_iota(jnp.int32, (GW, GW), 1))
    eye_g_bf = jnp.where(eye_g, 1.0, 0.0).astype(BF16)

    def stack(x4):
        return jnp.where(keep, jnp.concatenate([x4] * WKV_GROUP, axis=0), jnp.zeros((), x4.dtype))

    sls = [slice(g * GW, (g + 1) * GW) for g in range(n_grp)]
    J = [(ci, g) for ci in range(n_sub) for g in range(n_grp)]
    ops, p_end = {}, {}
    for ci in range(n_sub):
        rows = slice(ci * C, (ci + 1) * C)
        lw = lw_ref[rows, :]
        cum = _dot_exact_lhs(tri, lw)
        cum_end = cum[C - 1:C, :]
        e_neg = jnp.exp(-cum)
        e_end = jnp.exp(cum_end - cum)
        b_in = b_ref[rows, :]
        k_in = k_ref[rows, :]
        full = ((a_ref[rows, :] * jnp.exp(cum - lw)).astype(BF16),
                (r_ref[rows, :] * jnp.exp(cum)).astype(BF16),
                (b_in * e_neg).astype(BF16), (k_in * e_neg).astype(BF16),
                (b_in * e_end).astype(BF16), (k_in * e_end).astype(BF16),
                v_ref[rows, :].astype(BF16))
        p_end[ci] = jnp.exp(cum_end)
        for g in range(n_grp):
            ops[ci, g] = [stack(t[:, sls[g]]) for t in full]
    a_s, r_s, b_s, k_s, be_s, ke_s, v_s = [{j: ops[j][i] for j in J} for i in range(7)]
    m = {j: _dot_nt(jnp.concatenate([a_s[j], r_s[j]], axis=0), jnp.concatenate([b_s[j], k_s[j]], axis=0))
         for j in J}
    l_ab = {j: jnp.where(strict, m[j][:R, :R], 0.0) for j in J}
    a_ak = {j: jnp.where(strict, m[j][:R, R:], 0.0).astype(BF16) for j in J}
    a_rb = {j: jnp.where(lower, m[j][R:, :R], 0.0).astype(BF16) for j in J}
    a_rk = {j: jnp.where(lower, m[j][R:, R:], 0.0).astype(BF16) for j in J}
    t_inv = {j: eye_r + l_ab[j] for j in J}
    l_pow = {j: l_ab[j].astype(BF16) for j in J}
    for _ in range(n_lev):
        l_pow = {j: _dot(l_pow[j], l_pow[j]).astype(BF16) for j in J}
        t_inv = {j: t_inv[j] + _dot(t_inv[j].astype(BF16), l_pow[j]) for j in J}
    t_b = {j: t_inv[j].astype(BF16) for j in J}
    bke_t = {j: _dot_nt(eye_g_bf, jnp.concatenate([be_s[j], ke_s[j]], axis=0)).astype(BF16) for j in J}

    G = range(n_grp)
    h_cur = [h_scr[g] for g in G]
    for ci in range(n_sub):
        rows = slice(ci * C, (ci + 1) * C)
        h0_b = [h_cur[g].astype(BF16) for g in G]
        x = [_dot(a_s[ci, g], h0_b[g]) + _dot(a_ak[ci, g], v_s[ci, g]) for g in G]
        u = [_dot(t_b[ci, g], x[g].astype(BF16)).astype(BF16) for g in G]
        y_bd = [_dot(r_s[ci, g], h0_b[g]) + _dot(a_rb[ci, g], u[g]) + _dot(a_rk[ci, g], v_s[ci, g]) for g in G]
        h_add = [_dot(bke_t[ci, g], jnp.concatenate([u[g], v_s[ci, g]], axis=0)) for g in G]
        for g in G:
            y4 = y_bd[g][:C]
            for hh in range(1, WKV_GROUP):
                y4 = y4 + y_bd[g][hh * C:(hh + 1) * C]
            y_ref[rows, sls[g]] = y4
            p_col = jnp.sum(jnp.where(eye_g, p_end[ci][:, sls[g]], 0.0), axis=1, keepdims=True)
            h_cur[g] = p_col * h_cur[g] + h_add[g]

    for g in G:
        h_scr[g] = h_cur[g]

    @pl.when(c == pl.num_programs(1) - 1)
    def _():
        hT_ref[0] = h_scr[...]


def _wkv_call(arrs, h0, row0, n_seq, n_chunk, C, n_sub):
    rows = C * n_sub
    steps = n_chunk // n_sub
    blk0 = row0 // rows
    GW = WKV_GROUP * RWKV_N
    n_grp = RWKV_HEADS // WKV_GROUP
    tok = pl.BlockSpec((rows, RWKV_DIM), lambda b, c: (blk0 + b * steps + c, 0))
    out_tok = pl.BlockSpec((rows, RWKV_DIM), lambda b, c: (b * steps + c, 0))
    st = pl.BlockSpec((1, n_grp, GW, GW), lambda b, c: (b, 0, 0, 0))
    return pl.pallas_call(
        functools.partial(_wkv_kernel, C=C, n_sub=n_sub),
        grid=(n_seq, steps),
        in_specs=[tok] * 6 + [st],
        out_specs=[out_tok, st],
        out_shape=[
            jax.ShapeDtypeStruct((n_seq * n_chunk * C, RWKV_DIM), F32),
            jax.ShapeDtypeStruct((n_seq, n_grp, GW, GW), F32),
        ],
        scratch_shapes=[pltpu.VMEM((n_grp, GW, GW), F32)],
        compiler_params=_cparams(("parallel", "arbitrary")),
        name="wkv_c%d" % C,
    )(*arrs, h0)


def _to_blockdiag(h):
    B = h.shape[0]
    n_grp = RWKV_HEADS // WKV_GROUP
    hg = h.reshape(B, n_grp, WKV_GROUP, RWKV_N, RWKV_N)
    bd = jnp.einsum('bghkv,hi->bghkiv', hg, jnp.eye(WKV_GROUP, dtype=h.dtype))
    return bd.reshape(B, n_grp, WKV_GROUP * RWKV_N, WKV_GROUP * RWKV_N)


def _from_blockdiag(bd):
    B, n_grp = bd.shape[:2]
    x = bd.reshape(B, n_grp, WKV_GROUP, RWKV_N, WKV_GROUP, RWKV_N)
    return jnp.einsum('bghkhv->bghkv', x).reshape(B, RWKV_HEADS, RWKV_N, RWKV_N)


def _mix_kernel(xp_ref, xs_ref, attnp_ref, attns_ref, yp_ref, ys_ref, bonus_ref, g_ref, seg_ref, lng_ref,
                lnb_ref, woa_ref, wob_ref, g1_ref, b1_ref, wr_ref, br_ref, h_ref, route_ref, *, n_p):
    is_p = pl.program_id(0) < n_p
    seg = seg_ref[...]
    y = jnp.where(is_p, yp_ref[...], ys_ref[...])
    attn = jnp.where(is_p, attnp_ref[...], attns_ref[...])
    x = jnp.where(is_p, xp_ref[...], xs_ref[...])
    inv_n = 1.0 / RWKV_N
    yc = y - _dot_exact_rhs(y, seg) * inv_n
    var = _dot_exact_rhs(yc * yc, seg) * inv_n
    yn = yc * lax.rsqrt(var + GN_EPS) * lng_ref[...] + lnb_ref[...]
    rw = ((yn + bonus_ref[...]) * g_ref[...]).astype(BF16)
    m = _dot(attn, woa_ref[...]) + _dot(rw, wob_ref[...])
    h = _layer_norm(DN_ALPHA * x + m, g1_ref[...], b1_ref[...])
    h_ref[...] = h

    h_hi = h.astype(BF16)
    h_lo = (h - h_hi.astype(F32)).astype(BF16)
    logits = _dot(h_hi, wr_ref[0]) + _dot(h_lo, wr_ref[0]) + _dot(h_hi, wr_ref[1]) + br_ref[...]
    lane = lax.broadcasted_iota(jnp.int32, logits.shape, 1)
    big = jnp.int32(LANE)
    gl = jnp.where(lane < N_GROUPS, logits, NEG_INF)
    gmax = jnp.max(gl, axis=1, keepdims=True)
    grp = jnp.min(jnp.where(gl == gmax, lane, big), axis=1, keepdims=True)
    p_grp = 1.0 / jnp.sum(jnp.exp(gl - gmax), axis=1, keepdims=True)
    e_idx = lane - N_GROUPS
    in_grp = (lane >= N_GROUPS) & (lane < N_GROUPS + N_EXPERTS) & ((e_idx // EXPERTS_PER_GROUP) == grp)
    el = jnp.where(in_grp, logits, NEG_INF)
    m1 = jnp.max(el, axis=1, keepdims=True)
    i1 = jnp.min(jnp.where(el == m1, lane, big), axis=1, keepdims=True)
    el2 = jnp.where(lane == i1, NEG_INF, el)
    m2 = jnp.max(el2, axis=1, keepdims=True)
    i2 = jnp.min(jnp.where(el2 == m2, lane, big), axis=1, keepdims=True)
    t = jnp.exp(m2 - m1)
    g1 = p_grp / (1.0 + t)
    g2 = g1 * t
    e1 = (i1 - N_GROUPS).astype(F32)
    e2 = (i2 - N_GROUPS).astype(F32)
    route_ref[...] = jnp.where(lane == 0, e1, jnp.where(lane == 1, e2, jnp.where(lane == 2, g1,
                               jnp.where(lane == 3, g2, 0.0))))


def _mix_call(xp, xs, attn_p, attn_s, y_p, y_s, bonus, g, seg, lnx_g, lnx_b, woa, wob, ln1_g, ln1_b, wr, br):
    T = xp.shape[0] + xs.shape[0]
    tm = TOKEN_TILE
    n_p = xp.shape[0] // tm
    row = lambda i: (i, 0)
    full = lambda i: (0, 0)
    row_p, row_s = _split_rows(n_p)
    half = pl.BlockSpec((tm, RWKV_DIM), row)
    vec5 = pl.BlockSpec((1, RWKV_DIM), full)
    vec10 = pl.BlockSpec((1, D_MODEL), full)
    return pl.pallas_call(
        functools.partial(_mix_kernel, n_p=n_p),
        grid=(T // tm,),
        in_specs=[
            pl.BlockSpec((tm, D_MODEL), row_p), pl.BlockSpec((tm, D_MODEL), row_s),
            pl.BlockSpec((tm, RWKV_DIM), row_p), pl.BlockSpec((tm, RWKV_DIM), row_s),
            pl.BlockSpec((tm, RWKV_DIM), row_p), pl.BlockSpec((tm, RWKV_DIM), row_s),
            half, half,
            pl.BlockSpec((RWKV_DIM, RWKV_DIM), full), vec5, vec5,
            pl.BlockSpec((RWKV_DIM, D_MODEL), full), pl.BlockSpec((RWKV_DIM, D_MODEL), full),
            vec10, vec10,
            pl.BlockSpec((2, D_MODEL, LANE), lambda i: (0, 0, 0)), pl.BlockSpec((1, LANE), full),
        ],
        out_specs=[pl.BlockSpec((tm, D_MODEL), row), pl.BlockSpec((tm, LANE), row)],
        out_shape=[jax.ShapeDtypeStruct((T, D_MODEL), F32), jax.ShapeDtypeStruct((T, LANE), F32)],
        compiler_params=_cparams(("parallel",)),
        name="mix",
    )(xp, xs, attn_p, attn_s, y_p, y_s, bonus, g, seg, lnx_g, lnx_b, woa, wob, ln1_g, ln1_b, wr, br)


def _expert_kernel(be_ref, nu_ref, xs_ref, wg_ref, wu_ref, wd_ref, ys_ref):
    i = pl.program_id(0)

    @pl.when(i < nu_ref[0])
    def _():
        xb = xs_ref[...].astype(BF16)
        gate = _dot(xb, wg_ref[0].astype(BF16))
        up = _dot(xb, wu_ref[0].astype(BF16))
        act = (gate * _sigmoid(gate) * up).astype(BF16)
        ys_ref[...] = _dot(act, wd_ref[0].astype(BF16))

    @pl.when(i >= nu_ref[0])
    def _():
        ys_ref[...] = jnp.zeros(ys_ref.shape, ys_ref.dtype)


def _expert_call(block_e, n_used, xs, wg, wu, wd):
    n_blk = xs.shape[0] // MOE_BLK
    grid_spec = pltpu.PrefetchScalarGridSpec(
        num_scalar_prefetch=2,
        grid=(n_blk,),
        in_specs=[
            pl.BlockSpec((MOE_BLK, D_MODEL), lambda i, be, nu: (i, 0)),
            pl.BlockSpec((1, D_MODEL, D_EXPERT), lambda i, be, nu: (be[i], 0, 0)),
            pl.BlockSpec((1, D_MODEL, D_EXPERT), lambda i, be, nu: (be[i], 0, 0)),
            pl.BlockSpec((1, D_EXPERT, D_MODEL), lambda i, be, nu: (be[i], 0, 0)),
        ],
        out_specs=pl.BlockSpec((MOE_BLK, D_MODEL), lambda i, be, nu: (i, 0)),
    )
    return pl.pallas_call(
        _expert_kernel,
        grid_spec=grid_spec,
        out_shape=jax.ShapeDtypeStruct((n_blk * MOE_BLK, D_MODEL), F32),
        compiler_params=_cparams(("arbitrary",)),
        name="experts",
    )(block_e, n_used, xs, wg, wu, wd)


def _combine_kernel(h_ref, ya_ref, yb_ref, route_ref, g2_ref, b2_ref, op_ref, os_ref, *, n_p):
    i = pl.program_id(0)
    route = route_ref[...]
    f = ya_ref[...] * route[:, 2:3] + yb_ref[...] * route[:, 3:4]
    out = _layer_norm(DN_ALPHA * h_ref[...] + f, g2_ref[...], b2_ref[...])

    @pl.when(i < n_p)
    def _():
        op_ref[...] = out

    @pl.when(i >= n_p)
    def _():
        os_ref[...] = out


def _combine_call(h, yab, route, ln2_g, ln2_b, t_prompt):
    T = h.shape[0]
    tm = TOKEN_TILE
    n_t, n_p = T // tm, t_prompt // tm
    row = lambda i: (i, 0)
    full = lambda i: (0, 0)
    row_p, row_s = _split_rows(n_p)
    big = pl.BlockSpec((tm, D_MODEL), row)
    return pl.pallas_call(
        functools.partial(_combine_kernel, n_p=n_p),
        grid=(n_t,),
        in_specs=[big, big, pl.BlockSpec((tm, D_MODEL), lambda i: (i + n_t, 0)), pl.BlockSpec((tm, LANE), row),
                  pl.BlockSpec((1, D_MODEL), full), pl.BlockSpec((1, D_MODEL), full)],
        out_specs=[pl.BlockSpec((tm, D_MODEL), row_p), pl.BlockSpec((tm, D_MODEL), row_s)],
        out_shape=[jax.ShapeDtypeStruct((t_prompt, D_MODEL), F32),
                   jax.ShapeDtypeStruct((T - t_prompt, D_MODEL), F32)],
        compiler_params=_cparams(("arbitrary",)),
        name="combine",
    )(h, yab, yab, route, ln2_g, ln2_b)


def _prep_weights(w_in, w_uq, w_ukv):
    half = MLA_ROPE // 2
    kpe_w = w_in[:, Q_LORA + KV_LORA:MLA_IN]
    kpe_b = jnp.concatenate([-kpe_w[:, half:], kpe_w[:, :half]], axis=1)
    w1 = jnp.concatenate([w_in[:, :Q_LORA + KV_LORA], kpe_w, kpe_b,
                          jnp.zeros((D_MODEL, 64), F32), w_in[:, MLA_IN:]], axis=1).astype(BF16)
    pad_q = jnp.zeros((Q_LORA, MLA_HEADS, HEAD_PAD - MLA_NOPE - MLA_ROPE), F32)
    wqa = jnp.concatenate([w_uq, pad_q], axis=2).reshape(Q_LORA, -1).astype(BF16)
    rot = jnp.concatenate([jnp.zeros((Q_LORA, MLA_HEADS, MLA_NOPE), F32),
                           -w_uq[:, :, MLA_NOPE + half:], w_uq[:, :, MLA_NOPE:MLA_NOPE + half], pad_q], axis=2)
    wqb = rot.reshape(Q_LORA, -1).astype(BF16)
    w_uk, w_uv = w_ukv[:, :, :MLA_NOPE], w_ukv[:, :, MLA_NOPE:]
    wk = jnp.concatenate([w_uk, jnp.zeros((KV_LORA, MLA_HEADS, HEAD_PAD - MLA_NOPE), F32)], axis=2)
    wk = wk.reshape(KV_LORA, -1).astype(BF16)
    pk_np = np.zeros((MLA_ROPE, MLA_HEADS * HEAD_PAD), np.float32)
    for h in range(MLA_HEADS):
        for i in range(MLA_ROPE):
            pk_np[i, h * HEAD_PAD + MLA_NOPE + i] = 1.0
    pk = jnp.asarray(pk_np).astype(BF16)
    zv = jnp.zeros((KV_LORA, MLA_HEADS // 2, MLA_V), F32)
    wv4 = w_uv.reshape(KV_LORA, MLA_HEADS // 2, 2, MLA_V)
    wv = jnp.stack([jnp.concatenate([wv4[:, :, 0], zv], axis=2),
                    jnp.concatenate([zv, wv4[:, :, 1]], axis=2)], axis=2)
    wv = wv.reshape(KV_LORA, -1).astype(BF16)
    wuk = jnp.transpose(w_uk, (1, 2, 0)).astype(BF16)
    wuv_np = np.zeros((MLA_HEADS, MLA_HEADS * MLA_V), np.float32)
    for h in range(MLA_HEADS):
        wuv_np[h, h * MLA_V:(h + 1) * MLA_V] = 1.0
    wuv = jnp.transpose(w_uv, (1, 0, 2))
    wuv = (jnp.tile(wuv, (1, 1, MLA_HEADS)) * jnp.asarray(wuv_np)[:, None, :]).astype(BF16)
    return w1, wqa, wqb, wk, pk, wv, wuk, wuv


def _rope_table(pos):
    inv = ROPE_BASE ** (-jnp.arange(0, MLA_ROPE, 2, dtype=F32) / MLA_ROPE)
    ang = pos.astype(F32)[:, None] * inv[None, :]
    cos, sin = jnp.cos(ang), jnp.sin(ang)
    n = pos.shape[0]
    one = jnp.ones((n, MLA_NOPE), F32)
    z32 = jnp.zeros((n, HEAD_PAD - MLA_NOPE - MLA_ROPE), F32)
    z64 = jnp.zeros((n, MLA_NOPE), F32)
    cq = jnp.concatenate([one, cos, cos, z32], axis=1) * (MLA_SCALE * LOG2E)
    sq = jnp.concatenate([z64, sin, sin, z32], axis=1) * (MLA_SCALE * LOG2E)
    kt = jnp.concatenate([cos, cos, sin, sin, z64], axis=1)
    return jnp.concatenate([cq, sq, kt], axis=1)


def _seg_ones():
    idx = np.arange(RWKV_DIM) // RWKV_N
    return jnp.asarray((idx[:, None] == idx[None, :]).astype(np.float32)).astype(BF16)


def _dispatch(route, t_total):
    A = t_total * TOP_K
    flat_e = route[:, :TOP_K].astype(jnp.int32).reshape(A)
    onehot = (flat_e[:, None] == jnp.arange(N_EXPERTS, dtype=jnp.int32)[None, :]).astype(jnp.int32)
    csum = jnp.cumsum(onehot, axis=0)
    counts = csum[-1]
    rank = jnp.take_along_axis(csum, flat_e[:, None], axis=1)[:, 0] - 1
    blocks_per_e = (counts + MOE_BLK - 1) // MOE_BLK
    blk_end = jnp.cumsum(blocks_per_e)
    blk_start = blk_end - blocks_per_e
    dest = blk_start[flat_e] * MOE_BLK + rank
    n_blk = -(-A // MOE_BLK) + N_EXPERTS
    block_e = jnp.minimum(jnp.searchsorted(blk_end, jnp.arange(n_blk, dtype=jnp.int32), side='right'),
                          N_EXPERTS - 1).astype(jnp.int32)
    n_used = blk_end[-1:].astype(jnp.int32)
    slot_tok = jnp.zeros((n_blk * MOE_BLK,), jnp.int32).at[dest].set(
        jnp.arange(A, dtype=jnp.int32) // TOP_K)
    return dest.reshape(t_total, TOP_K), block_e, n_used, slot_tok


def _sc_gather_rows(table, idx):
    n_rows, width = idx.shape[0], table.shape[1]
    n_workers = SC_CORES * SC_SUBCORES
    per_worker = n_rows // n_workers
    assert n_rows % n_workers == 0 and per_worker % SC_WINDOW == 0
    mesh = plsc.VectorSubcoreMesh(core_axis_name="c", subcore_axis_name="s")

    @functools.partial(
        pl.kernel, mesh=mesh,
        out_type=jax.ShapeDtypeStruct((n_rows, width), table.dtype),
        scratch_types=[
            pltpu.VMEM((SC_WINDOW,), jnp.int32),
            pltpu.VMEM((SC_WINDOW, width), table.dtype),
            pltpu.SemaphoreType.DMA,
        ],
    )
    def gather(table_hbm, idx_hbm, out_hbm, idx_v, rows_v, sem):
        wid = lax.axis_index("s") * SC_CORES + lax.axis_index("c")
        base = wid * per_worker

        @pl.loop(0, per_worker // SC_WINDOW)
        def _(w):
            off = pl.multiple_of(base + w * SC_WINDOW, SC_WINDOW)
            pltpu.sync_copy(idx_hbm.at[pl.ds(off, SC_WINDOW)], idx_v)
            pltpu.async_copy(table_hbm.at[idx_v], rows_v, sem).wait()
            pltpu.sync_copy(rows_v, out_hbm.at[pl.ds(off, SC_WINDOW)])

    return gather(table, idx)


def kernel(x_prompt, x_sample, cache_ckv, cache_kpe, state_wkv, state_shift, w_in, q_norm_g, kv_norm_g, w_uq,
           w_ukv, mu_shift, w0, w2, a0, a2, g2, k_k, k_a, r_k, lnx_g, lnx_b, w_o, ln1_g, ln1_b, w_gr, b_gr,
           w_er, b_er, w_eg, w_eu, w_ed, ln2_g, ln2_b):
    B, S, D = x_prompt.shape
    DB, DS, _ = x_sample.shape
    past = cache_ckv.shape[2]
    Tp, Ts = B * S, DB * DS
    T = Tp + Ts
    assert D == D_MODEL and DS == SHIFT_GROUP and S % ATTN_TQ == 0 and S % (CHUNK * WKV_SUB) == 0
    assert Tp % TOKEN_TILE == 0 and T % TOKEN_TILE == 0 and w_in.shape[0] == DEPTH

    l = 0
    xp, xs_in = x_prompt.reshape(Tp, D), x_sample.reshape(Ts, D)
    w1, wqa, wqb, wk, pk, wv, wuk, wuv = _prep_weights(w_in[l], w_uq[l], w_ukv[l])
    pos = jnp.concatenate([jnp.arange(S, dtype=jnp.int32),
                           jnp.tile(past + jnp.arange(DS, dtype=jnp.int32), TOKEN_TILE // DS)])
    rope = _rope_table(pos)

    q, kcat, vcat, ckv_p, ckv_s, kpe_p, kpe_s, pr, last_rows = _proj_call(
        xp, xs_in, rope, S // TOKEN_TILE, w1, q_norm_g[l][None], kv_norm_g[l][None], wqa, wqb, wk, pk, wv)

    attn_p = _attn_call(q, kcat, vcat, B, S)
    attn_s = _mla_sample_call(q, ckv_s, kpe_s, cache_ckv[l], cache_kpe[l], wuk, wuv, Tp, DB, DS)

    bnd = jnp.concatenate([jnp.zeros((1, RWKV_IN), F32), last_rows[:-1]], axis=0)
    gidx = jnp.arange(T // SHIFT_GROUP)
    seq_start = (gidx < Tp // SHIFT_GROUP) & (gidx % (S // SHIFT_GROUP) == 0)
    bnd = jnp.where(seq_start[:, None], 0.0, bnd)
    bnd = jnp.concatenate([bnd[:Tp // SHIFT_GROUP], state_shift[l]], axis=0)[:, None, :]

    seg = _seg_ones()
    vec = lambda a: a.reshape(1, -1)
    r, lw, kh, v, na, b, bonus, g = _prep_call(
        pr, bnd, vec(mu_shift[l]), vec(w0[l]), vec(a0[l]), vec(k_k[l]), vec(k_a[l]), vec(r_k[l]),
        w2[l].astype(BF16), a2[l].astype(BF16), g2[l].astype(BF16), seg)

    scan_in = (r, lw, kh, v, na, b)
    h0_p = jnp.zeros((B, RWKV_HEADS // WKV_GROUP, WKV_GROUP * RWKV_N, WKV_GROUP * RWKV_N), F32)
    y_p, hT_p = _wkv_call(scan_in, h0_p, 0, B, S // CHUNK, CHUNK, WKV_SUB)
    h0_s = _to_blockdiag(jnp.swapaxes(state_wkv[l], -1, -2))
    y_s, hT_s = _wkv_call(scan_in, h0_s, Tp, DB, 1, DS, 1)

    wo_b = w_o[l].astype(BF16)
    wr = jnp.concatenate([w_gr[l], w_er[l], jnp.zeros((D, LANE - N_GROUPS - N_EXPERTS), F32)], axis=1)
    wr_hi = wr.astype(BF16)
    wr_lo = (wr - wr_hi.astype(F32)).astype(BF16)
    br = jnp.concatenate([b_gr[l], b_er[l], jnp.zeros((LANE - N_GROUPS - N_EXPERTS,), F32)])[None]
    h, route = _mix_call(xp, xs_in, attn_p, attn_s, y_p, y_s, bonus, g, seg, vec(lnx_g[l]), vec(lnx_b[l]),
                         wo_b[:MLA_HEADS * MLA_V], wo_b[MLA_HEADS * MLA_V:], vec(ln1_g[l]), vec(ln1_b[l]),
                         jnp.stack([wr_hi, wr_lo]), br)

    dest, block_e, n_used, slot_tok = _dispatch(route, T)
    xs = _sc_gather_rows(h, slot_tok)
    ys = _expert_call(block_e, n_used, xs, w_eg[l], w_eu[l], w_ed[l])
    yab = _sc_gather_rows(ys, jnp.concatenate([dest[:, 0], dest[:, 1]]))
    out_p, out_s = _combine_call(h, yab, route, vec(ln2_g[l]), vec(ln2_b[l]), Tp)

    y_prompt = out_p.reshape(B, S, D)
    y_sample = out_s.reshape(DB, DS, D)
    p_ckv = ckv_p.reshape(1, B, S, KV_LORA)
    p_kpe = kpe_p.reshape(1, B, S, MLA_ROPE)
    s_ckv = ckv_s.reshape(1, DB, DS, KV_LORA)
    s_kpe = kpe_s.reshape(1, DB, DS, MLA_ROPE)
    p_wkv = jnp.swapaxes(_from_blockdiag(hT_p), -1, -2)[None]
    s_wkv = jnp.swapaxes(_from_blockdiag(hT_s), -1, -2)[None]
    gp = S // SHIFT_GROUP
    p_sh = last_rows[gp - 1:B * gp:gp][None]
    s_sh = last_rows[B * gp:][None]
    return (y_prompt, y_sample, p_ckv, p_kpe, p_wkv, p_sh, s_ckv, s_kpe, s_wkv, s_sh)
```

```python
import functools
import math

import numpy as np
import jax
import jax.numpy as jnp
from jax import lax
from jax.experimental import pallas as pl
from jax.experimental.pallas import tpu as pltpu
from jax.experimental.pallas import tpu_sc as plsc

F32 = jnp.float32
BF16 = jnp.bfloat16

D_MODEL = 1024
CHUNK = 64
MLA_HEADS = 8
MLA_NOPE = 64
MLA_ROPE = 32
MLA_V = 64
Q_LORA = 384
KV_LORA = 256
ROPE_BASE = 10000.0
MLA_IN = Q_LORA + KV_LORA + MLA_ROPE
MLA_SCALE = (MLA_NOPE + MLA_ROPE) ** -0.5
RWKV_HEADS = 8
RWKV_N = 64
RWKV_DIM = RWKV_HEADS * RWKV_N
DECAY_LORA = 64
AAA_LORA = 64
GATE_LORA = 128
RWKV_IN = 3 * RWKV_DIM + DECAY_LORA + AAA_LORA + GATE_LORA
N_GROUPS = 4
EXPERTS_PER_GROUP = 8
N_EXPERTS = N_GROUPS * EXPERTS_PER_GROUP
TOP_K = 2
D_EXPERT = 256
MOE_BLK = 256
LN_EPS = 1e-5
RMS_EPS = 1e-6
GN_EPS = 64e-5
NEG_INF = -1e30
DEPTH = 1
DN_ALPHA = (2 * DEPTH) ** 0.25

LANE = 128
HEAD_PAD = 128
PROJ_W = 768 + RWKV_IN
SHIFT_GROUP = 32
TOKEN_TILE = 512
ATTN_TQ = 1024
ATTN_TK = 512
V_ONE_LANE = (MLA_V, 0)
LOG2E = math.log2(math.e)
VMEM_LIMIT = 48 * 1024 * 1024
SC_CORES = 2
SC_SUBCORES = 16
SC_WINDOW = 32
WKV_GROUP = 4
WKV_SUB = 4


def _cparams(sem):
    return pltpu.CompilerParams(dimension_semantics=sem, vmem_limit_bytes=VMEM_LIMIT)


def _split3(x):
    hi = x.astype(BF16)
    r1 = x - hi.astype(F32)
    mid = r1.astype(BF16)
    lo = (r1 - mid.astype(F32)).astype(BF16)
    return hi, mid, lo


def _dot(a, b):
    return jnp.dot(a, b, preferred_element_type=F32)


def _dot_nt(a, b):
    return lax.dot_general(a, b, (((1,), (1,)), ((), ())), preferred_element_type=F32)


def _dot_exact_rhs(x, w):
    hi, mid, lo = _split3(x)
    return _dot(hi, w) + _dot(mid, w) + _dot(lo, w)


def _dot_exact_lhs(w, x):
    hi, mid, lo = _split3(x)
    return _dot(w, hi) + _dot(w, mid) + _dot(w, lo)


def _sigmoid(x):
    return 1.0 / (1.0 + jnp.exp(-x))


def _layer_norm(x, g, b):
    xc = x - jnp.mean(x, -1, keepdims=True)
    var = jnp.mean(xc * xc, -1, keepdims=True)
    return xc * lax.rsqrt(var + LN_EPS) * g + b


def _proj_kernel(xp_ref, xs_ref, rope_ref, w1_ref, gq_ref, gkv_ref, wqa_ref, wqb_ref, wk_ref, pk_ref, wv_ref,
                 q_ref, k_ref, v_ref, ckvp_ref, ckvs_ref, kpep_ref, kpes_ref, pr_ref, last_ref, *, n_p, tm):
    i = pl.program_id(0)
    is_p = i < n_p
    x = jnp.where(is_p, xp_ref[...], xs_ref[...]).astype(BF16)
    proj = _dot(x, w1_ref[...])
    c_q = proj[:, :Q_LORA]
    c_kv = proj[:, Q_LORA:Q_LORA + KV_LORA]
    kp = proj[:, 640:768]
    pr = proj[:, 768:]
    pr_ref[...] = pr
    last_ref[...] = pr.reshape(tm // SHIFT_GROUP, SHIFT_GROUP, RWKV_IN)[:, SHIFT_GROUP - 1, :]

    cqn = c_q * lax.rsqrt(jnp.mean(c_q * c_q, -1, keepdims=True) + RMS_EPS) * gq_ref[...]
    ckv = c_kv * lax.rsqrt(jnp.mean(c_kv * c_kv, -1, keepdims=True) + RMS_EPS) * gkv_ref[...]

    rope = rope_ref[...]
    cq = rope[:, :LANE]
    sq = rope[:, LANE:2 * LANE]
    kt = rope[:, 2 * LANE:]
    prod = kp * kt
    kpe = prod[:, :MLA_ROPE] + prod[:, MLA_ROPE:2 * MLA_ROPE]

    @pl.when(is_p)
    def _():
        ckvp_ref[...] = ckv
        kpep_ref[...] = kpe

    @pl.when(jnp.logical_not(is_p))
    def _():
        ckvs_ref[...] = ckv
        kpes_ref[...] = kpe

    cqb = cqn.astype(BF16)
    qa = _dot(cqb, wqa_ref[...])
    qb = _dot(cqb, wqb_ref[...])
    for h in range(MLA_HEADS):
        sl = slice(h * HEAD_PAD, (h + 1) * HEAD_PAD)
        q_ref[:, sl] = (qa[:, sl] * cq + qb[:, sl] * sq).astype(BF16)

    ckv_b = ckv.astype(BF16)
    k = _dot(ckv_b, wk_ref[...]) + _dot(kpe.astype(BF16), pk_ref[...])
    k_ref[...] = k.astype(BF16)
    lane = lax.broadcasted_iota(jnp.int32, (1, MLA_HEADS * HEAD_PAD), 1)
    odd = (lane // HEAD_PAD) % 2
    one_lane = jnp.where(odd == 1, V_ONE_LANE[1], V_ONE_LANE[0])
    v_one = jnp.where(lane % HEAD_PAD == one_lane, 1.0, 0.0)
    v_ref[...] = (_dot(ckv_b, wv_ref[...]) + v_one).astype(BF16)


def _split_rows(n_p):
    return (lambda i: (jnp.minimum(i, n_p - 1), 0)), (lambda i: (jnp.maximum(i - n_p, 0), 0))


def _proj_call(xp, xs, rope, rope_tiles, w1, gq, gkv, wqa, wqb, wk, pk, wv):
    Tp, Ts = xp.shape[0], xs.shape[0]
    T = Tp + Ts
    tm = TOKEN_TILE
    n_p = Tp // tm
    row = lambda i: (i, 0)
    full = lambda i: (0, 0)
    row_p, row_s = _split_rows(n_p)
    wide = MLA_HEADS * HEAD_PAD
    ng = tm // SHIFT_GROUP
    return pl.pallas_call(
        functools.partial(_proj_kernel, n_p=n_p, tm=tm),
        grid=(T // tm,),
        in_specs=[
            pl.BlockSpec((tm, D_MODEL), row_p),
            pl.BlockSpec((tm, D_MODEL), row_s),
            pl.BlockSpec((tm, 3 * LANE), lambda i: (jnp.where(i < n_p, i % rope_tiles, rope_tiles), 0)),
            pl.BlockSpec((D_MODEL, PROJ_W), full),
            pl.BlockSpec((1, Q_LORA), full),
            pl.BlockSpec((1, KV_LORA), full),
            pl.BlockSpec((Q_LORA, wide), full),
            pl.BlockSpec((Q_LORA, wide), full),
            pl.BlockSpec((KV_LORA, wide), full),
            pl.BlockSpec((MLA_ROPE, wide), full),
            pl.BlockSpec((KV_LORA, wide), full),
        ],
        out_specs=[
            pl.BlockSpec((tm, wide), row),
            pl.BlockSpec((tm, wide), row),
            pl.BlockSpec((tm, wide), row),
            pl.BlockSpec((tm, KV_LORA), row_p),
            pl.BlockSpec((tm, KV_LORA), row_s),
            pl.BlockSpec((tm, MLA_ROPE), row_p),
            pl.BlockSpec((tm, MLA_ROPE), row_s),
            pl.BlockSpec((tm, RWKV_IN), row),
            pl.BlockSpec((ng, RWKV_IN), row),
        ],
        out_shape=[
            jax.ShapeDtypeStruct((T, wide), BF16),
            jax.ShapeDtypeStruct((T, wide), BF16),
            jax.ShapeDtypeStruct((T, wide), BF16),
            jax.ShapeDtypeStruct((Tp, KV_LORA), F32),
            jax.ShapeDtypeStruct((Ts, KV_LORA), F32),
            jax.ShapeDtypeStruct((Tp, MLA_ROPE), F32),
            jax.ShapeDtypeStruct((Ts, MLA_ROPE), F32),
            jax.ShapeDtypeStruct((T, RWKV_IN), F32),
            jax.ShapeDtypeStruct((T // SHIFT_GROUP, RWKV_IN), F32),
        ],
        compiler_params=_cparams(("arbitrary",)),
        name="proj",
    )(xp, xs, rope, w1, gq, gkv, wqa, wqb, wk, pk, wv)


def _attn_kernel(q_ref, k_ref, v_ref, o_ref, m_scr, acc_scr, *, tq, tk):
    qi = pl.program_id(2)
    m_scr[...] = jnp.full(m_scr.shape, NEG_INF, F32)
    acc_scr[...] = jnp.zeros(acc_scr.shape, F32)
    reps = tk // LANE
    n_diag = tq // tk

    def kv_block(kj, rows, masked):
        k0 = pl.multiple_of(kj * tk, tk)
        n_rows = rows.stop - rows.start
        if masked:
            r = lax.broadcasted_iota(jnp.int32, (n_rows, tk), 0) // CHUNK
            c = lax.broadcasted_iota(jnp.int32, (n_rows, tk), 1) // CHUNK
            visible = c <= r
        H = range(2)
        sls = [slice(h * HEAD_PAD, (h + 1) * HEAD_PAD) for h in H]
        s = [_dot_nt(q_ref[rows, sl], k_ref[pl.ds(k0, tk), sl]) for sl in sls]
        if masked:
            s = [jnp.where(visible, s[h], NEG_INF) for h in H]
        m_prev = [m_scr[h, rows, :] for h in H]
        m_new = [jnp.maximum(m_prev[h], jnp.max(s[h], axis=1, keepdims=True)) for h in H]
        pexp = [jnp.exp2(s[h] - jnp.tile(m_new[h], (1, reps))).astype(BF16) for h in H]
        pv = [_dot(pexp[h], v_ref[pl.ds(k0, tk), sls[h]]) for h in H]
        for h in H:
            acc_scr[h, rows, :] = jnp.exp2(m_prev[h] - m_new[h]) * acc_scr[h, rows, :] + pv[h]
            m_scr[h, rows, :] = m_new[h]

    def body(kj, carry):
        kv_block(kj, slice(0, tq), False)
        return carry

    lax.fori_loop(0, qi * n_diag, body, 0)
    for d in range(n_diag):
        kj = qi * n_diag + d
        kv_block(kj, slice(d * tk, (d + 1) * tk), True)
        if (d + 1) * tk < tq:
            kv_block(kj, slice((d + 1) * tk, tq), False)
    acc0, acc1 = acc_scr[0], acc_scr[1]
    lane = lax.broadcasted_iota(jnp.int32, acc0.shape, 1)
    l0 = acc0[:, V_ONE_LANE[0]:V_ONE_LANE[0] + 1]
    l1 = acc1[:, V_ONE_LANE[1]:V_ONE_LANE[1] + 1]
    o_ref[...] = jnp.where(lane < MLA_V, acc0 / l0, acc1 / l1).astype(o_ref.dtype)


def _attn_call(q, k, v, n_batch, seq):
    tq, tk = ATTN_TQ, ATTN_TK
    nq = seq // tq
    hp = MLA_HEADS // 2
    resident = pl.BlockSpec((seq, 2 * HEAD_PAD), lambda b, h, i: (b, h), pipeline_mode=pl.Buffered(1))
    return pl.pallas_call(
        functools.partial(_attn_kernel, tq=tq, tk=tk),
        grid=(n_batch, hp, nq),
        in_specs=[
            pl.BlockSpec((tq, 2 * HEAD_PAD), lambda b, h, i: (b * nq + i, h)),
            resident,
            resident,
        ],
        out_specs=pl.BlockSpec((tq, LANE), lambda b, h, i: (b * nq + i, h)),
        out_shape=jax.ShapeDtypeStruct((n_batch * seq, hp * LANE), BF16),
        scratch_shapes=[
            pltpu.VMEM((2, tq, LANE), F32),
            pltpu.VMEM((2, tq, LANE), F32),
        ],
        compiler_params=_cparams(("parallel", "parallel", "arbitrary")),
        name="attn",
    )(q, k, v)


def _mla_sample_kernel(q_ref, cn_ref, kn_ref, cp_ref, kp_ref, wuk_ref, wuv_ref, o_ref, *, past, dec):
    cp = cp_ref[0].astype(BF16)
    kp = kp_ref[0].astype(BF16)
    cn = cn_ref[...].astype(BF16)
    kn = kn_ref[...].astype(BF16)
    qpos = (past + lax.broadcasted_iota(jnp.int32, (dec, past), 0)) // CHUNK
    vis_p = (lax.broadcasted_iota(jnp.int32, (dec, past), 1) // CHUNK) <= qpos
    qpos_n = (past + lax.broadcasted_iota(jnp.int32, (dec, dec), 0)) // CHUNK
    vis_n = ((past + lax.broadcasted_iota(jnp.int32, (dec, dec), 1)) // CHUNK) <= qpos_n
    out = jnp.zeros((dec, MLA_HEADS * MLA_V), F32)
    for h in range(MLA_HEADS):
        qn = q_ref[:, h * HEAD_PAD:h * HEAD_PAD + MLA_NOPE]
        qp = q_ref[:, h * HEAD_PAD + MLA_NOPE:h * HEAD_PAD + MLA_NOPE + MLA_ROPE]
        q_lat = _dot(qn, wuk_ref[h]).astype(BF16)
        s_p = jnp.where(vis_p, _dot_nt(q_lat, cp) + _dot_nt(qp, kp), NEG_INF)
        s_n = jnp.where(vis_n, _dot_nt(q_lat, cn) + _dot_nt(qp, kn), NEG_INF)
        m = jnp.maximum(jnp.max(s_p, axis=1, keepdims=True), jnp.max(s_n, axis=1, keepdims=True))
        e_p = jnp.exp2(s_p - m)
        e_n = jnp.exp2(s_n - m)
        l = jnp.sum(e_p, axis=1, keepdims=True) + jnp.sum(e_n, axis=1, keepdims=True)
        o_lat = (_dot(e_p.astype(BF16), cp) + _dot(e_n.astype(BF16), cn)) / l
        out = out + _dot(o_lat.astype(BF16), wuv_ref[h])
    o_ref[...] = out.astype(o_ref.dtype)


def _mla_sample_call(q, ckv, kpe, cache_ckv, cache_kpe, wuk, wuv, row0, n_seq, dec):
    past = cache_ckv.shape[1]
    blk0 = row0 // dec
    wide = MLA_HEADS * HEAD_PAD
    return pl.pallas_call(
        functools.partial(_mla_sample_kernel, past=past, dec=dec),
        grid=(n_seq,),
        in_specs=[
            pl.BlockSpec((dec, wide), lambda b: (blk0 + b, 0)),
            pl.BlockSpec((dec, KV_LORA), lambda b: (b, 0)),
            pl.BlockSpec((dec, MLA_ROPE), lambda b: (b, 0)),
            pl.BlockSpec((1, past, KV_LORA), lambda b: (b, 0, 0)),
            pl.BlockSpec((1, past, MLA_ROPE), lambda b: (b, 0, 0)),
            pl.BlockSpec((MLA_HEADS, MLA_NOPE, KV_LORA), lambda b: (0, 0, 0)),
            pl.BlockSpec((MLA_HEADS, KV_LORA, MLA_HEADS * MLA_V), lambda b: (0, 0, 0)),
        ],
        out_specs=pl.BlockSpec((dec, MLA_HEADS * MLA_V), lambda b: (b, 0)),
        out_shape=jax.ShapeDtypeStruct((n_seq * dec, MLA_HEADS * MLA_V), BF16),
        compiler_params=_cparams(("parallel",)),
        name="mla_sample",
    )(q, ckv, kpe, cache_ckv, cache_kpe, wuk, wuv)


def _prep_kernel(pr_ref, bnd_ref, mu_ref, w0_ref, a0_ref, kk_ref, ka_ref, rk_ref, w2_ref, a2_ref, g2_ref,
                 seg_ref, r_ref, lw_ref, kh_ref, v_ref, na_ref, b_ref, bonus_ref, g_ref, *, tm):
    ng = tm // SHIFT_GROUP
    pr = pr_ref[...]
    pr3 = pr.reshape(ng, SHIFT_GROUP, RWKV_IN)
    rolled = pltpu.roll(pr3, 1, 1)
    first = lax.broadcasted_iota(jnp.int32, pr3.shape, 1) == 0
    prev = jnp.where(first, bnd_ref[...], rolled).reshape(tm, RWKV_IN)
    u = pr + mu_ref[...] * (prev - pr)
    o1, o2, o3 = RWKV_DIM, 2 * RWKV_DIM, 3 * RWKV_DIM
    o4, o5 = o3 + DECAY_LORA, o3 + DECAY_LORA + AAA_LORA
    r, k, v = u[:, :o1], u[:, o1:o2], u[:, o2:o3]
    w_lo, a_lo, g_lo = u[:, o3:o4], u[:, o4:o5], u[:, o5:]
    wl = w0_ref[...] + _dot(jnp.tanh(w_lo).astype(BF16), w2_ref[...])
    lw_ref[...] = -math.exp(-0.5) * _sigmoid(wl)
    a = _sigmoid(a0_ref[...] + _dot(a_lo.astype(BF16), a2_ref[...]))
    g_ref[...] = _dot(_sigmoid(g_lo).astype(BF16), g2_ref[...])
    seg = seg_ref[...]
    kk = k * kk_ref[...]
    kk = kk / jnp.maximum(jnp.sqrt(_dot_exact_rhs(kk * kk, seg)), 1e-12)
    kh = k * (1.0 + (a - 1.0) * ka_ref[...])
    r_ref[...] = r
    kh_ref[...] = kh
    v_ref[...] = v
    na_ref[...] = -kk
    b_ref[...] = kk * a
    bonus_ref[...] = _dot_exact_rhs(r * kh * rk_ref[...], seg) * v


def _prep_call(pr, bnd, mu, w0, a0, k_k, k_a, r_k, w2, a2, g2, seg):
    T = pr.shape[0]
    tm = TOKEN_TILE
    row = lambda i: (i, 0)
    full = lambda i: (0, 0)
    vec = pl.BlockSpec((1, RWKV_DIM), full)
    out = pl.BlockSpec((tm, RWKV_DIM), row)
    return pl.pallas_call(
        functools.partial(_prep_kernel, tm=tm),
        grid=(T // tm,),
        in_specs=[
            pl.BlockSpec((tm, RWKV_IN), row),
            pl.BlockSpec((tm // SHIFT_GROUP, 1, RWKV_IN), lambda i: (i, 0, 0)),
            pl.BlockSpec((1, RWKV_IN), full),
            vec, vec, vec, vec, vec,
            pl.BlockSpec((DECAY_LORA, RWKV_DIM), full),
            pl.BlockSpec((AAA_LORA, RWKV_DIM), full),
            pl.BlockSpec((GATE_LORA, RWKV_DIM), full),
            pl.BlockSpec((RWKV_DIM, RWKV_DIM), full),
        ],
        out_specs=[out] * 8,
        out_shape=[jax.ShapeDtypeStruct((T, RWKV_DIM), F32)] * 8,
        compiler_params=_cparams(("parallel",)),
        name="rwkv_prep",
    )(pr, bnd, mu, w0, a0, k_k, k_a, r_k, w2, a2, g2, seg)


def _wkv_kernel(r_ref, lw_ref, k_ref, v_ref, a_ref, b_ref, h0_ref, y_ref, hT_ref, h_scr, *, C, n_sub):
    GW = WKV_GROUP * RWKV_N
    R = WKV_GROUP * C
    n_grp = RWKV_HEADS // WKV_GROUP
    n_lev = int(round(math.log2(C))) - 1
    c = pl.program_id(1)

    @pl.when(c == 0)
    def _():
        h_scr[...] = h0_ref[0]

    row = lax.broadcasted_iota(jnp.int32, (C, C), 0)
    col = lax.broadcasted_iota(jnp.int32, (C, C), 1)
    tri = jnp.where(col <= row, 1.0, 0.0).astype(BF16)
    rr = lax.broadcasted_iota(jnp.int32, (R, R), 0)
    cc = lax.broadcasted_iota(jnp.int32, (R, R), 1)
    same = (rr // C) == (cc // C)
    strict = same & (cc < rr)
    lower = same & (cc <= rr)
    eye_r = jnp.where(rr == cc, 1.0, 0.0)
    keep = (lax.broadcasted_iota(jnp.int32, (R, GW), 0) // C
            == lax.broadcasted_iota(jnp.int32, (R, GW), 1) // RWKV_N)
    eye_g = (lax.broadcasted_iota(jnp.int32, (GW, GW), 0) == lax.broadcasted_iota(jnp.int32, (GW, GW), 1))
    eye_g_bf = jnp.where(eye_g, 1.0, 0.0).astype(BF16)

    def stack(x4):
        return jnp.where(keep, jnp.concatenate([x4] * WKV_GROUP, axis=0), jnp.zeros((), x4.dtype))

    sls = [slice(g * GW, (g + 1) * GW) for g in range(n_grp)]
    J = [(ci, g) for ci in range(n_sub) for g in range(n_grp)]
    ops, p_end = {}, {}
    for ci in range(n_sub):
        rows = slice(ci * C, (ci + 1) * C)
        lw = lw_ref[rows, :]
        cum = _dot_exact_lhs(tri, lw)
        cum_end = cum[C - 1:C, :]
        e_neg = jnp.exp(-cum)
        e_end = jnp.exp(cum_end - cum)
        b_in = b_ref[rows, :]
        k_in = k_ref[rows, :]
        full = ((a_ref[rows, :] * jnp.exp(cum - lw)).astype(BF16),
                (r_ref[rows, :] * jnp.exp(cum)).astype(BF16),
                (b_in * e_neg).astype(BF16), (k_in * e_neg).astype(BF16),
                (b_in * e_end).astype(BF16), (k_in * e_end).astype(BF16),
                v_ref[rows, :].astype(BF16))
        p_end[ci] = jnp.exp(cum_end)
        for g in range(n_grp):
            ops[ci, g] = [stack(t[:, sls[g]]) for t in full]
    a_s, r_s, b_s, k_s, be_s, ke_s, v_s = [{j: ops[j][i] for j in J} for i in range(7)]
    m = {j: _dot_nt(jnp.concatenate([a_s[j], r_s[j]], axis=0), jnp.concatenate([b_s[j], k_s[j]], axis=0))
         for j in J}
    l_ab = {j: jnp.where(strict, m[j][:R, :R], 0.0) for j in J}
    a_ak = {j: jnp.where(strict, m[j][:R, R:], 0.0).astype(BF16) for j in J}
    a_rb = {j: jnp.where(lower, m[j][R:, :R], 0.0).astype(BF16) for j in J}
    a_rk = {j: jnp.where(lower, m[j][R:, R:], 0.0).astype(BF16) for j in J}
    t_inv = {j: eye_r + l_ab[j] for j in J}
    l_pow = {j: l_ab[j].astype(BF16) for j in J}
    for _ in range(n_lev):
        l_pow = {j: _dot(l_pow[j], l_pow[j]).astype(BF16) for j in J}
        t_inv = {j: t_inv[j] + _dot(t_inv[j].astype(BF16), l_pow[j]) for j in J}
    t_b = {j: t_inv[j].astype(BF16) for j in J}
    bke_t = {j: _dot_nt(eye_g_bf, jnp.concatenate([be_s[j], ke_s[j]], axis=0)).astype(BF16) for j in J}

    G = range(n_grp)
    h_cur = [h_scr[g] for g in G]
    for ci in range(n_sub):
        rows = slice(ci * C, (ci + 1) * C)
        h0_b = [h_cur[g].astype(BF16) for g in G]
        x = [_dot(a_s[ci, g], h0_b[g]) + _dot(a_ak[ci, g], v_s[ci, g]) for g in G]
        u = [_dot(t_b[ci, g], x[g].astype(BF16)).astype(BF16) for g in G]
        y_bd = [_dot(r_s[ci, g], h0_b[g]) + _dot(a_rb[ci, g], u[g]) + _dot(a_rk[ci, g], v_s[ci, g]) for g in G]
        h_add = [_dot(bke_t[ci, g], jnp.concatenate([u[g], v_s[ci, g]], axis=0)) for g in G]
        for g in G:
            y4 = y_bd[g][:C]
            for hh in range(1, WKV_GROUP):
                y4 = y4 + y_bd[g][hh * C:(hh + 1) * C]
            y_ref[rows, sls[g]] = y4
            p_col = jnp.sum(jnp.where(eye_g, p_end[ci][:, sls[g]], 0.0), axis=1, keepdims=True)
            h_cur[g] = p_col * h_cur[g] + h_add[g]

    for g in G:
        h_scr[g] = h_cur[g]

    @pl.when(c == pl.num_programs(1) - 1)
    def _():
        hT_ref[0] = h_scr[...]


def _wkv_call(arrs, h0, row0, n_seq, n_chunk, C, n_sub):
    rows = C * n_sub
    steps = n_chunk // n_sub
    blk0 = row0 // rows
    GW = WKV_GROUP * RWKV_N
    n_grp = RWKV_HEADS // WKV_GROUP
    tok = pl.BlockSpec((rows, RWKV_DIM), lambda b, c: (blk0 + b * steps + c, 0))
    out_tok = pl.BlockSpec((rows, RWKV_DIM), lambda b, c: (b * steps + c, 0))
    st = pl.BlockSpec((1, n_grp, GW, GW), lambda b, c: (b, 0, 0, 0))
    return pl.pallas_call(
        functools.partial(_wkv_kernel, C=C, n_sub=n_sub),
        grid=(n_seq, steps),
        in_specs=[tok] * 6 + [st],
        out_specs=[out_tok, st],
        out_shape=[
            jax.ShapeDtypeStruct((n_seq * n_chunk * C, RWKV_DIM), F32),
            jax.ShapeDtypeStruct((n_seq, n_grp, GW, GW), F32),
        ],
        scratch_shapes=[pltpu.VMEM((n_grp, GW, GW), F32)],
        compiler_params=_cparams(("parallel", "arbitrary")),
        name="wkv_c%d" % C,
    )(*arrs, h0)


def _to_blockdiag(h):
    B = h.shape[0]
    n_grp = RWKV_HEADS // WKV_GROUP
    hg = h.reshape(B, n_grp, WKV_GROUP, RWKV_N, RWKV_N)
    bd = jnp.einsum('bghkv,hi->bghkiv', hg, jnp.eye(WKV_GROUP, dtype=h.dtype))
    return bd.reshape(B, n_grp, WKV_GROUP * RWKV_N, WKV_GROUP * RWKV_N)


def _from_blockdiag(bd):
    B, n_grp = bd.shape[:2]
    x = bd.reshape(B, n_grp, WKV_GROUP, RWKV_N, WKV_GROUP, RWKV_N)
    return jnp.einsum('bghkhv->bghkv', x).reshape(B, RWKV_HEADS, RWKV_N, RWKV_N)


def _mix_kernel(xp_ref, xs_ref, attnp_ref, attns_ref, yp_ref, ys_ref, bonus_ref, g_ref, seg_ref, lng_ref,
                lnb_ref, woa_ref, wob_ref, g1_ref, b1_ref, wr_ref, br_ref, h_ref, route_ref, count_ref, *, n_p):
    is_p = pl.program_id(0) < n_p
    seg = seg_ref[...]
    y = jnp.where(is_p, yp_ref[...], ys_ref[...])
    attn = jnp.where(is_p, attnp_ref[...], attns_ref[...])
    x = jnp.where(is_p, xp_ref[...], xs_ref[...])
    inv_n = 1.0 / RWKV_N
    yc = y - _dot_exact_rhs(y, seg) * inv_n
    var = _dot_exact_rhs(yc * yc, seg) * inv_n
    yn = yc * lax.rsqrt(var + GN_EPS) * lng_ref[...] + lnb_ref[...]
    rw = ((yn + bonus_ref[...]) * g_ref[...]).astype(BF16)
    m = _dot(attn, woa_ref[...]) + _dot(rw, wob_ref[...])
    h = _layer_norm(DN_ALPHA * x + m, g1_ref[...], b1_ref[...])
    h_ref[...] = h

    h_hi = h.astype(BF16)
    h_lo = (h - h_hi.astype(F32)).astype(BF16)
    logits = _dot(h_hi, wr_ref[0]) + _dot(h_lo, wr_ref[0]) + _dot(h_hi, wr_ref[1]) + br_ref[...]
    lane = lax.broadcasted_iota(jnp.int32, logits.shape, 1)
    big = jnp.int32(LANE)
    gl = jnp.where(lane < N_GROUPS, logits, NEG_INF)
    gmax = jnp.max(gl, axis=1, keepdims=True)
    grp = jnp.min(jnp.where(gl == gmax, lane, big), axis=1, keepdims=True)
    p_grp = 1.0 / jnp.sum(jnp.exp(gl - gmax), axis=1, keepdims=True)
    e_idx = lane - N_GROUPS
    in_grp = (lane >= N_GROUPS) & (lane < N_GROUPS + N_EXPERTS) & ((e_idx // EXPERTS_PER_GROUP) == grp)
    el = jnp.where(in_grp, logits, NEG_INF)
    m1 = jnp.max(el, axis=1, keepdims=True)
    i1 = jnp.min(jnp.where(el == m1, lane, big), axis=1, keepdims=True)
    el2 = jnp.where(lane == i1, NEG_INF, el)
    m2 = jnp.max(el2, axis=1, keepdims=True)
    i2 = jnp.min(jnp.where(el2 == m2, lane, big), axis=1, keepdims=True)
    t = jnp.exp(m2 - m1)
    g1 = p_grp / (1.0 + t)
    g2 = g1 * t
    e1 = (i1 - N_GROUPS).astype(F32)
    e2 = (i2 - N_GROUPS).astype(F32)
    tm = logits.shape[0]
    chosen = jnp.where(lane == i1, 1.0, jnp.where(lane == i2, 1.0, 0.0))
    earlier = (lax.broadcasted_iota(jnp.int32, (tm, tm), 1) < lax.broadcasted_iota(jnp.int32, (tm, tm), 0))
    before = _dot(jnp.where(earlier, 1.0, 0.0).astype(BF16), chosen.astype(BF16))
    r1 = jnp.sum(jnp.where(lane == i1, before, 0.0), axis=1, keepdims=True)
    r2 = jnp.sum(jnp.where(lane == i2, before, 0.0), axis=1, keepdims=True)
    count_ref[0] = jnp.sum(chosen, axis=0, keepdims=True)
    cols = (e1, e2, g1, g2, r1, r2)
    route = jnp.zeros(logits.shape, F32)
    for j, col in enumerate(cols):
        route = jnp.where(lane == j, col, route)
    route_ref[...] = route


def _mix_call(xp, xs, attn_p, attn_s, y_p, y_s, bonus, g, seg, lnx_g, lnx_b, woa, wob, ln1_g, ln1_b, wr, br):
    T = xp.shape[0] + xs.shape[0]
    tm = TOKEN_TILE
    n_p = xp.shape[0] // tm
    row = lambda i: (i, 0)
    full = lambda i: (0, 0)
    row_p, row_s = _split_rows(n_p)
    half = pl.BlockSpec((tm, RWKV_DIM), row)
    vec5 = pl.BlockSpec((1, RWKV_DIM), full)
    vec10 = pl.BlockSpec((1, D_MODEL), full)
    return pl.pallas_call(
        functools.partial(_mix_kernel, n_p=n_p),
        grid=(T // tm,),
        in_specs=[
            pl.BlockSpec((tm, D_MODEL), row_p), pl.BlockSpec((tm, D_MODEL), row_s),
            pl.BlockSpec((tm, RWKV_DIM), row_p), pl.BlockSpec((tm, RWKV_DIM), row_s),
            pl.BlockSpec((tm, RWKV_DIM), row_p), pl.BlockSpec((tm, RWKV_DIM), row_s),
            half, half,
            pl.BlockSpec((RWKV_DIM, RWKV_DIM), full), vec5, vec5,
            pl.BlockSpec((RWKV_DIM, D_MODEL), full), pl.BlockSpec((RWKV_DIM, D_MODEL), full),
            vec10, vec10,
            pl.BlockSpec((2, D_MODEL, LANE), lambda i: (0, 0, 0)), pl.BlockSpec((1, LANE), full),
        ],
        out_specs=[pl.BlockSpec((tm, D_MODEL), row), pl.BlockSpec((tm, LANE), row),
                   pl.BlockSpec((1, 1, LANE), lambda i: (i, 0, 0))],
        out_shape=[jax.ShapeDtypeStruct((T, D_MODEL), F32), jax.ShapeDtypeStruct((T, LANE), F32),
                   jax.ShapeDtypeStruct((T // tm, 1, LANE), F32)],
        compiler_params=_cparams(("parallel",)),
        name="mix",
    )(xp, xs, attn_p, attn_s, y_p, y_s, bonus, g, seg, lnx_g, lnx_b, woa, wob, ln1_g, ln1_b, wr, br)


def _expert_kernel(be_ref, nu_ref, xs_ref, wg_ref, wu_ref, wd_ref, ys_ref):
    i = pl.program_id(0)

    @pl.when(i < nu_ref[0])
    def _():
        xb = xs_ref[...].astype(BF16)
        gate = _dot(xb, wg_ref[0].astype(BF16))
        up = _dot(xb, wu_ref[0].astype(BF16))
        act = (gate * _sigmoid(gate) * up).astype(BF16)
        ys_ref[...] = _dot(act, wd_ref[0].astype(BF16))

    @pl.when(i >= nu_ref[0])
    def _():
        ys_ref[...] = jnp.zeros(ys_ref.shape, ys_ref.dtype)


def _expert_call(block_e, n_used, xs, wg, wu, wd):
    n_blk = xs.shape[0] // MOE_BLK
    grid_spec = pltpu.PrefetchScalarGridSpec(
        num_scalar_prefetch=2,
        grid=(n_blk,),
        in_specs=[
            pl.BlockSpec((MOE_BLK, D_MODEL), lambda i, be, nu: (i, 0)),
            pl.BlockSpec((1, D_MODEL, D_EXPERT), lambda i, be, nu: (be[i], 0, 0)),
            pl.BlockSpec((1, D_MODEL, D_EXPERT), lambda i, be, nu: (be[i], 0, 0)),
            pl.BlockSpec((1, D_EXPERT, D_MODEL), lambda i, be, nu: (be[i], 0, 0)),
        ],
        out_specs=pl.BlockSpec((MOE_BLK, D_MODEL), lambda i, be, nu: (i, 0)),
    )
    return pl.pallas_call(
        _expert_kernel,
        grid_spec=grid_spec,
        out_shape=jax.ShapeDtypeStruct((n_blk * MOE_BLK, D_MODEL), F32),
        compiler_params=_cparams(("arbitrary",)),
        name="experts",
    )(block_e, n_used, xs, wg, wu, wd)


def _combine_kernel(h_ref, ya_ref, yb_ref, route_ref, g2_ref, b2_ref, op_ref, os_ref, *, n_p):
    i = pl.program_id(0)
    route = route_ref[...]
    f = ya_ref[...] * route[:, 2:3] + yb_ref[...] * route[:, 3:4]
    out = _layer_norm(DN_ALPHA * h_ref[...] + f, g2_ref[...], b2_ref[...])

    @pl.when(i < n_p)
    def _():
        op_ref[...] = out

    @pl.when(i >= n_p)
    def _():
        os_ref[...] = out


def _combine_call(h, yab, route, ln2_g, ln2_b, t_prompt):
    T = h.shape[0]
    tm = TOKEN_TILE
    n_t, n_p = T // tm, t_prompt // tm
    row = lambda i: (i, 0)
    full = lambda i: (0, 0)
    row_p, row_s = _split_rows(n_p)
    big = pl.BlockSpec((tm, D_MODEL), row)
    return pl.pallas_call(
        functools.partial(_combine_kernel, n_p=n_p),
        grid=(n_t,),
        in_specs=[big, big, pl.BlockSpec((tm, D_MODEL), lambda i: (i + n_t, 0)), pl.BlockSpec((tm, LANE), row),
                  pl.BlockSpec((1, D_MODEL), full), pl.BlockSpec((1, D_MODEL), full)],
        out_specs=[pl.BlockSpec((tm, D_MODEL), row_p), pl.BlockSpec((tm, D_MODEL), row_s)],
        out_shape=[jax.ShapeDtypeStruct((t_prompt, D_MODEL), F32),
                   jax.ShapeDtypeStruct((T - t_prompt, D_MODEL), F32)],
        compiler_params=_cparams(("arbitrary",)),
        name="combine",
    )(h, yab, yab, route, ln2_g, ln2_b)


def _prep_weights(w_in, w_uq, w_ukv):
    half = MLA_ROPE // 2
    kpe_w = w_in[:, Q_LORA + KV_LORA:MLA_IN]
    kpe_b = jnp.concatenate([-kpe_w[:, half:], kpe_w[:, :half]], axis=1)
    w1 = jnp.concatenate([w_in[:, :Q_LORA + KV_LORA], kpe_w, kpe_b,
                          jnp.zeros((D_MODEL, 64), F32), w_in[:, MLA_IN:]], axis=1).astype(BF16)
    pad_q = jnp.zeros((Q_LORA, MLA_HEADS, HEAD_PAD - MLA_NOPE - MLA_ROPE), F32)
    wqa = jnp.concatenate([w_uq, pad_q], axis=2).reshape(Q_LORA, -1).astype(BF16)
    rot = jnp.concatenate([jnp.zeros((Q_LORA, MLA_HEADS, MLA_NOPE), F32),
                           -w_uq[:, :, MLA_NOPE + half:], w_uq[:, :, MLA_NOPE:MLA_NOPE + half], pad_q], axis=2)
    wqb = rot.reshape(Q_LORA, -1).astype(BF16)
    w_uk, w_uv = w_ukv[:, :, :MLA_NOPE], w_ukv[:, :, MLA_NOPE:]
    wk = jnp.concatenate([w_uk, jnp.zeros((KV_LORA, MLA_HEADS, HEAD_PAD - MLA_NOPE), F32)], axis=2)
    wk = wk.reshape(KV_LORA, -1).astype(BF16)
    pk_np = np.zeros((MLA_ROPE, MLA_HEADS * HEAD_PAD), np.float32)
    for h in range(MLA_HEADS):
        for i in range(MLA_ROPE):
            pk_np[i, h * HEAD_PAD + MLA_NOPE + i] = 1.0
    pk = jnp.asarray(pk_np).astype(BF16)
    zv = jnp.zeros((KV_LORA, MLA_HEADS // 2, MLA_V), F32)
    wv4 = w_uv.reshape(KV_LORA, MLA_HEADS // 2, 2, MLA_V)
    wv = jnp.stack([jnp.concatenate([wv4[:, :, 0], zv], axis=2),
                    jnp.concatenate([zv, wv4[:, :, 1]], axis=2)], axis=2)
    wv = wv.reshape(KV_LORA, -1).astype(BF16)
    wuk = jnp.transpose(w_uk, (1, 2, 0)).astype(BF16)
    wuv_np = np.zeros((MLA_HEADS, MLA_HEADS * MLA_V), np.float32)
    for h in range(MLA_HEADS):
        wuv_np[h, h * MLA_V:(h + 1) * MLA_V] = 1.0
    wuv = jnp.transpose(w_uv, (1, 0, 2))
    wuv = (jnp.tile(wuv, (1, 1, MLA_HEADS)) * jnp.asarray(wuv_np)[:, None, :]).astype(BF16)
    return w1, wqa, wqb, wk, pk, wv, wuk, wuv


def _rope_table(pos):
    inv = ROPE_BASE ** (-jnp.arange(0, MLA_ROPE, 2, dtype=F32) / MLA_ROPE)
    ang = pos.astype(F32)[:, None] * inv[None, :]
    cos, sin = jnp.cos(ang), jnp.sin(ang)
    n = pos.shape[0]
    one = jnp.ones((n, MLA_NOPE), F32)
    z32 = jnp.zeros((n, HEAD_PAD - MLA_NOPE - MLA_ROPE), F32)
    z64 = jnp.zeros((n, MLA_NOPE), F32)
    cq = jnp.concatenate([one, cos, cos, z32], axis=1) * (MLA_SCALE * LOG2E)
    sq = jnp.concatenate([z64, sin, sin, z32], axis=1) * (MLA_SCALE * LOG2E)
    kt = jnp.concatenate([cos, cos, sin, sin, z64], axis=1)
    return jnp.concatenate([cq, sq, kt], axis=1)


def _seg_ones():
    idx = np.arange(RWKV_DIM) // RWKV_N
    return jnp.asarray((idx[:, None] == idx[None, :]).astype(np.float32)).astype(BF16)


def _dispatch(route, tile_counts, t_total):
    A = t_total * TOP_K
    n_tiles = tile_counts.shape[0]
    counts_te = tile_counts[:, 0, N_GROUPS:N_GROUPS + N_EXPERTS].astype(jnp.int32)
    counts = jnp.sum(counts_te, axis=0)
    blocks_per_e = (counts + MOE_BLK - 1) // MOE_BLK
    blk_end = jnp.cumsum(blocks_per_e)
    blk_start = blk_end - blocks_per_e
    tile_off = jnp.cumsum(counts_te, axis=0) - counts_te
    base = (blk_start[None, :] * MOE_BLK + tile_off).reshape(-1)
    e = route[:, :TOP_K].astype(jnp.int32)
    rank = route[:, 4:4 + TOP_K].astype(jnp.int32)
    tile = (jnp.arange(t_total, dtype=jnp.int32) // (t_total // n_tiles))[:, None]
    dest = base[tile * N_EXPERTS + e] + rank
    n_blk = -(-A // MOE_BLK) + N_EXPERTS
    blk = jnp.arange(n_blk, dtype=jnp.int32)
    block_e = jnp.minimum(jnp.sum((blk[:, None] >= blk_end[None, :]).astype(jnp.int32), axis=1),
                          N_EXPERTS - 1).astype(jnp.int32)
    n_used = blk_end[-1:].astype(jnp.int32)
    return dest, block_e, n_used


def _sc_gather_rows(table, idx):
    n_rows, width = idx.shape[0], table.shape[1]
    n_workers = SC_CORES * SC_SUBCORES
    per_worker = n_rows // n_workers
    assert n_rows % n_workers == 0 and per_worker % SC_WINDOW == 0
    mesh = plsc.VectorSubcoreMesh(core_axis_name="c", subcore_axis_name="s")

    @functools.partial(
        pl.kernel, mesh=mesh,
        out_type=jax.ShapeDtypeStruct((n_rows, width), table.dtype),
        scratch_types=[
            pltpu.VMEM((SC_WINDOW,), jnp.int32),
            pltpu.VMEM((SC_WINDOW, width), table.dtype),
            pltpu.SemaphoreType.DMA,
        ],
    )
    def gather(table_hbm, idx_hbm, out_hbm, idx_v, rows_v, sem):
        wid = lax.axis_index("s") * SC_CORES + lax.axis_index("c")
        base = wid * per_worker

        @pl.loop(0, per_worker // SC_WINDOW)
        def _(w):
            off = pl.multiple_of(base + w * SC_WINDOW, SC_WINDOW)
            pltpu.sync_copy(idx_hbm.at[pl.ds(off, SC_WINDOW)], idx_v)
            pltpu.async_copy(table_hbm.at[idx_v], rows_v, sem).wait()
            pltpu.sync_copy(rows_v, out_hbm.at[pl.ds(off, SC_WINDOW)])

    return gather(table, idx)


def _sc_scatter_rows(src, idx_a, idx_b, n_slots):
    n_rows, width = src.shape
    n_workers = SC_CORES * SC_SUBCORES
    per_worker = n_rows // n_workers
    assert n_rows % n_workers == 0 and per_worker % SC_WINDOW == 0
    n_win = per_worker // SC_WINDOW
    mesh = plsc.VectorSubcoreMesh(core_axis_name="c", subcore_axis_name="s")

    @functools.partial(
        pl.kernel, mesh=mesh,
        out_type=jax.ShapeDtypeStruct((n_slots, width), src.dtype),
        scratch_types=[
            pltpu.VMEM((1, SC_WINDOW), jnp.int32),
            pltpu.VMEM((1, SC_WINDOW), jnp.int32),
            pltpu.VMEM((SC_WINDOW, width), src.dtype),
        ],
    )
    def scatter(src_hbm, ia_hbm, ib_hbm, out_hbm, ia_v, ib_v, rows_v):
        wid = lax.axis_index("s") * SC_CORES + lax.axis_index("c")

        @pl.loop(0, n_win)
        def _(w):
            win = wid * n_win + w
            off = pl.multiple_of(win * SC_WINDOW, SC_WINDOW)
            pltpu.sync_copy(src_hbm.at[pl.ds(off, SC_WINDOW)], rows_v)
            pltpu.sync_copy(ia_hbm.at[pl.ds(win, 1)], ia_v)
            pltpu.sync_copy(ib_hbm.at[pl.ds(win, 1)], ib_v)
            pltpu.sync_copy(rows_v, out_hbm.at[ia_v.at[0]])
            pltpu.sync_copy(rows_v, out_hbm.at[ib_v.at[0]])

    return scatter(src, idx_a, idx_b)


def kernel(x_prompt, x_sample, cache_ckv, cache_kpe, state_wkv, state_shift, w_in, q_norm_g, kv_norm_g, w_uq,
           w_ukv, mu_shift, w0, w2, a0, a2, g2, k_k, k_a, r_k, lnx_g, lnx_b, w_o, ln1_g, ln1_b, w_gr, b_gr,
           w_er, b_er, w_eg, w_eu, w_ed, ln2_g, ln2_b):
    B, S, D = x_prompt.shape
    DB, DS, _ = x_sample.shape
    past = cache_ckv.shape[2]
    Tp, Ts = B * S, DB * DS
    T = Tp + Ts
    assert D == D_MODEL and DS == SHIFT_GROUP and S % ATTN_TQ == 0 and S % (CHUNK * WKV_SUB) == 0
    assert Tp % TOKEN_TILE == 0 and T % TOKEN_TILE == 0 and w_in.shape[0] == DEPTH

    l = 0
    xp, xs_in = x_prompt.reshape(Tp, D), x_sample.reshape(Ts, D)
    w1, wqa, wqb, wk, pk, wv, wuk, wuv = _prep_weights(w_in[l], w_uq[l], w_ukv[l])
    pos = jnp.concatenate([jnp.arange(S, dtype=jnp.int32),
                           jnp.tile(past + jnp.arange(DS, dtype=jnp.int32), TOKEN_TILE // DS)])
    rope = _rope_table(pos)

    q, kcat, vcat, ckv_p, ckv_s, kpe_p, kpe_s, pr, last_rows = _proj_call(
        xp, xs_in, rope, S // TOKEN_TILE, w1, q_norm_g[l][None], kv_norm_g[l][None], wqa, wqb, wk, pk, wv)

    attn_p = _attn_call(q, kcat, vcat, B, S)
    attn_s = _mla_sample_call(q, ckv_s, kpe_s, cache_ckv[l], cache_kpe[l], wuk, wuv, Tp, DB, DS)

    bnd = jnp.concatenate([jnp.zeros((1, RWKV_IN), F32), last_rows[:-1]], axis=0)
    gidx = jnp.arange(T // SHIFT_GROUP)
    seq_start = (gidx < Tp // SHIFT_GROUP) & (gidx % (S // SHIFT_GROUP) == 0)
    bnd = jnp.where(seq_start[:, None], 0.0, bnd)
    bnd = jnp.concatenate([bnd[:Tp // SHIFT_GROUP], state_shift[l]], axis=0)[:, None, :]

    seg = _seg_ones()
    vec = lambda a: a.reshape(1, -1)
    r, lw, kh, v, na, b, bonus, g = _prep_call(
        pr, bnd, vec(mu_shift[l]), vec(w0[l]), vec(a0[l]), vec(k_k[l]), vec(k_a[l]), vec(r_k[l]),
        w2[l].astype(BF16), a2[l].astype(BF16), g2[l].astype(BF16), seg)

    scan_in = (r, lw, kh, v, na, b)
    h0_p = jnp.zeros((B, RWKV_HEADS // WKV_GROUP, WKV_GROUP * RWKV_N, WKV_GROUP * RWKV_N), F32)
    y_p, hT_p = _wkv_call(scan_in, h0_p, 0, B, S // CHUNK, CHUNK, WKV_SUB)
    h0_s = _to_blockdiag(jnp.swapaxes(state_wkv[l], -1, -2))
    y_s, hT_s = _wkv_call(scan_in, h0_s, Tp, DB, 1, DS, 1)

    wo_b = w_o[l].astype(BF16)
    wr = jnp.concatenate([w_gr[l], w_er[l], jnp.zeros((D, LANE - N_GROUPS - N_EXPERTS), F32)], axis=1)
    wr_hi = wr.astype(BF16)
    wr_lo = (wr - wr_hi.astype(F32)).astype(BF16)
    br = jnp.concatenate([b_gr[l], b_er[l], jnp.zeros((LANE - N_GROUPS - N_EXPERTS,), F32)])[None]
    h, route, tile_counts = _mix_call(
        xp, xs_in, attn_p, attn_s, y_p, y_s, bonus, g, seg, vec(lnx_g[l]), vec(lnx_b[l]),
        wo_b[:MLA_HEADS * MLA_V], wo_b[MLA_HEADS * MLA_V:], vec(ln1_g[l]), vec(ln1_b[l]),
        jnp.stack([wr_hi, wr_lo]), br)

    dest, block_e, n_used = _dispatch(route, tile_counts, T)
    win = lambda a: a.reshape(T // SC_WINDOW, SC_WINDOW)
    xs = _sc_scatter_rows(h, win(dest[:, 0]), win(dest[:, 1]), block_e.shape[0] * MOE_BLK)
    ys = _expert_call(block_e, n_used, xs, w_eg[l], w_eu[l], w_ed[l])
    yab = _sc_gather_rows(ys, jnp.concatenate([dest[:, 0], dest[:, 1]]))
    out_p, out_s = _combine_call(h, yab, route, vec(ln2_g[l]), vec(ln2_b[l]), Tp)

    y_prompt = out_p.reshape(B, S, D)
    y_sample = out_s.reshape(DB, DS, D)
    p_ckv = ckv_p.reshape(1, B, S, KV_LORA)
    p_kpe = kpe_p.reshape(1, B, S, MLA_ROPE)
    s_ckv = ckv_s.reshape(1, DB, DS, KV_LORA)
    s_kpe = kpe_s.reshape(1, DB, DS, MLA_ROPE)
    p_wkv = jnp.swapaxes(_from_blockdiag(hT_p), -1, -2)[None]
    s_wkv = jnp.swapaxes(_from_blockdiag(hT_s), -1, -2)[None]
    gp = S // SHIFT_GROUP
    p_sh = last_rows[gp - 1:B * gp:gp][None]
    s_sh = last_rows[B * gp:][None]
    return (y_prompt, y_sample, p_ckv, p_kpe, p_wkv, p_sh, s_ckv, s_kpe, s_wkv, s_sh)
```

```python
import functools
import math

import numpy as np
import jax
import jax.numpy as jnp
from jax import lax
from jax.experimental import pallas as pl
from jax.experimental.pallas import tpu as pltpu
from jax.experimental.pallas import tpu_sc as plsc

F32 = jnp.float32
BF16 = jnp.bfloat16

D_MODEL = 1024
CHUNK = 64
MLA_HEADS = 8
MLA_NOPE = 64
MLA_ROPE = 32
MLA_V = 64
Q_LORA = 384
KV_LORA = 256
ROPE_BASE = 10000.0
MLA_IN = Q_LORA + KV_LORA + MLA_ROPE
MLA_SCALE = (MLA_NOPE + MLA_ROPE) ** -0.5
RWKV_HEADS = 8
RWKV_N = 64
RWKV_DIM = RWKV_HEADS * RWKV_N
DECAY_LORA = 64
AAA_LORA = 64
GATE_LORA = 128
RWKV_IN = 3 * RWKV_DIM + DECAY_LORA + AAA_LORA + GATE_LORA
N_GROUPS = 4
EXPERTS_PER_GROUP = 8
N_EXPERTS = N_GROUPS * EXPERTS_PER_GROUP
TOP_K = 2
D_EXPERT = 256
MOE_BLK = 256
LN_EPS = 1e-5
RMS_EPS = 1e-6
GN_EPS = 64e-5
NEG_INF = -1e30
DEPTH = 1
DN_ALPHA = (2 * DEPTH) ** 0.25

LANE = 128
HEAD_PAD = 128
PROJ_W = 768 + RWKV_IN
SHIFT_GROUP = 32
TOKEN_TILE = 512
ATTN_TQ = 1024
ATTN_TK = 512
V_ONE_LANE = (MLA_V, 0)
LOG2E = math.log2(math.e)
VMEM_LIMIT = 48 * 1024 * 1024
SC_CORES = 2
SC_SUBCORES = 16
SC_WINDOW = 32
WKV_GROUP = 4
WKV_SUB = 4


def _cparams(sem):
    return pltpu.CompilerParams(dimension_semantics=sem, vmem_limit_bytes=VMEM_LIMIT)


def _split3(x):
    hi = x.astype(BF16)
    r1 = x - hi.astype(F32)
    mid = r1.astype(BF16)
    lo = (r1 - mid.astype(F32)).astype(BF16)
    return hi, mid, lo


def _dot(a, b):
    return jnp.dot(a, b, preferred_element_type=F32)


def _dot_nt(a, b):
    return lax.dot_general(a, b, (((1,), (1,)), ((), ())), preferred_element_type=F32)


def _dot_exact_rhs(x, w):
    hi, mid, lo = _split3(x)
    return _dot(hi, w) + _dot(mid, w) + _dot(lo, w)


def _dot_exact_lhs(w, x):
    hi, mid, lo = _split3(x)
    return _dot(w, hi) + _dot(w, mid) + _dot(w, lo)


def _pack_bf16_pairs(x):
    n = x.shape[1] // 2
    bits = pltpu.bitcast(x.astype(BF16).astype(F32), jnp.int32)
    return (bits[:, :n] & jnp.int32(-65536)) | lax.shift_right_logical(bits[:, n:], jnp.int32(16))


def _unpack_bf16_pairs(p):
    hi = pltpu.bitcast(p & jnp.int32(-65536), F32)
    lo = pltpu.bitcast(lax.shift_left(p, jnp.int32(16)), F32)
    return jnp.concatenate([hi, lo], axis=1)


def _sigmoid(x):
    return 1.0 / (1.0 + jnp.exp(-x))


def _layer_norm(x, g, b):
    xc = x - jnp.mean(x, -1, keepdims=True)
    var = jnp.mean(xc * xc, -1, keepdims=True)
    return xc * lax.rsqrt(var + LN_EPS) * g + b


def _proj_kernel(xp_ref, xs_ref, rope_ref, w1_ref, gq_ref, gkv_ref, wqa_ref, wqb_ref, wk_ref, pk_ref, wv_ref,
                 q_ref, k_ref, v_ref, ckvp_ref, ckvs_ref, kpep_ref, kpes_ref, pr_ref, last_ref, *, n_p, tm):
    i = pl.program_id(0)
    is_p = i < n_p
    x = jnp.where(is_p, xp_ref[...], xs_ref[...]).astype(BF16)
    proj = _dot(x, w1_ref[...])
    c_q = proj[:, :Q_LORA]
    c_kv = proj[:, Q_LORA:Q_LORA + KV_LORA]
    kp = proj[:, 640:768]
    pr = proj[:, 768:]
    pr_ref[...] = pr
    last_ref[...] = pr.reshape(tm // SHIFT_GROUP, SHIFT_GROUP, RWKV_IN)[:, SHIFT_GROUP - 1, :]

    cqn = c_q * lax.rsqrt(jnp.mean(c_q * c_q, -1, keepdims=True) + RMS_EPS) * gq_ref[...]
    ckv = c_kv * lax.rsqrt(jnp.mean(c_kv * c_kv, -1, keepdims=True) + RMS_EPS) * gkv_ref[...]

    rope = rope_ref[...]
    cq = rope[:, :LANE]
    sq = rope[:, LANE:2 * LANE]
    kt = rope[:, 2 * LANE:]
    prod = kp * kt
    kpe = prod[:, :MLA_ROPE] + prod[:, MLA_ROPE:2 * MLA_ROPE]

    @pl.when(is_p)
    def _():
        ckvp_ref[...] = ckv
        kpep_ref[...] = kpe

    @pl.when(jnp.logical_not(is_p))
    def _():
        ckvs_ref[...] = ckv
        kpes_ref[...] = kpe

    cqb = cqn.astype(BF16)
    qa = _dot(cqb, wqa_ref[...])
    qb = _dot(cqb, wqb_ref[...])
    for h in range(MLA_HEADS):
        sl = slice(h * HEAD_PAD, (h + 1) * HEAD_PAD)
        q_ref[:, sl] = (qa[:, sl] * cq + qb[:, sl] * sq).astype(BF16)

    ckv_b = ckv.astype(BF16)
    k = _dot(ckv_b, wk_ref[...]) + _dot(kpe.astype(BF16), pk_ref[...])
    k_ref[...] = k.astype(BF16)
    lane = lax.broadcasted_iota(jnp.int32, (1, MLA_HEADS * HEAD_PAD), 1)
    odd = (lane // HEAD_PAD) % 2
    one_lane = jnp.where(odd == 1, V_ONE_LANE[1], V_ONE_LANE[0])
    v_one = jnp.where(lane % HEAD_PAD == one_lane, 1.0, 0.0)
    v_ref[...] = (_dot(ckv_b, wv_ref[...]) + v_one).astype(BF16)


def _split_rows(n_p):
    return (lambda i: (jnp.minimum(i, n_p - 1), 0)), (lambda i: (jnp.maximum(i - n_p, 0), 0))


def _proj_call(xp, xs, rope, rope_tiles, w1, gq, gkv, wqa, wqb, wk, pk, wv):
    Tp, Ts = xp.shape[0], xs.shape[0]
    T = Tp + Ts
    tm = TOKEN_TILE
    n_p = Tp // tm
    row = lambda i: (i, 0)
    full = lambda i: (0, 0)
    row_p, row_s = _split_rows(n_p)
    wide = MLA_HEADS * HEAD_PAD
    ng = tm // SHIFT_GROUP
    return pl.pallas_call(
        functools.partial(_proj_kernel, n_p=n_p, tm=tm),
        grid=(T // tm,),
        in_specs=[
            pl.BlockSpec((tm, D_MODEL), row_p),
            pl.BlockSpec((tm, D_MODEL), row_s),
            pl.BlockSpec((tm, 3 * LANE), lambda i: (jnp.where(i < n_p, i % rope_tiles, rope_tiles), 0)),
            pl.BlockSpec((D_MODEL, PROJ_W), full),
            pl.BlockSpec((1, Q_LORA), full),
            pl.BlockSpec((1, KV_LORA), full),
            pl.BlockSpec((Q_LORA, wide), full),
            pl.BlockSpec((Q_LORA, wide), full),
            pl.BlockSpec((KV_LORA, wide), full),
            pl.BlockSpec((MLA_ROPE, wide), full),
            pl.BlockSpec((KV_LORA, wide), full),
        ],
        out_specs=[
            pl.BlockSpec((tm, wide), row),
            pl.BlockSpec((tm, wide), row),
            pl.BlockSpec((tm, wide), row),
            pl.BlockSpec((tm, KV_LORA), row_p),
            pl.BlockSpec((tm, KV_LORA), row_s),
            pl.BlockSpec((tm, MLA_ROPE), row_p),
            pl.BlockSpec((tm, MLA_ROPE), row_s),
            pl.BlockSpec((tm, RWKV_IN), row),
            pl.BlockSpec((ng, RWKV_IN), row),
        ],
        out_shape=[
            jax.ShapeDtypeStruct((T, wide), BF16),
            jax.ShapeDtypeStruct((T, wide), BF16),
            jax.ShapeDtypeStruct((T, wide), BF16),
            jax.ShapeDtypeStruct((Tp, KV_LORA), F32),
            jax.ShapeDtypeStruct((Ts, KV_LORA), F32),
            jax.ShapeDtypeStruct((Tp, MLA_ROPE), F32),
            jax.ShapeDtypeStruct((Ts, MLA_ROPE), F32),
            jax.ShapeDtypeStruct((T, RWKV_IN), F32),
            jax.ShapeDtypeStruct((T // SHIFT_GROUP, RWKV_IN), F32),
        ],
        compiler_params=_cparams(("arbitrary",)),
        name="proj",
    )(xp, xs, rope, w1, gq, gkv, wqa, wqb, wk, pk, wv)


def _attn_kernel(q_ref, k_ref, v_ref, o_ref, m_scr, acc_scr, *, tq, tk):
    qi = pl.program_id(2)
    m_scr[...] = jnp.full(m_scr.shape, NEG_INF, F32)
    acc_scr[...] = jnp.zeros(acc_scr.shape, F32)
    reps = tk // LANE
    n_diag = tq // tk

    def kv_block(kj, rows, masked):
        k0 = pl.multiple_of(kj * tk, tk)
        n_rows = rows.stop - rows.start
        if masked:
            r = lax.broadcasted_iota(jnp.int32, (n_rows, tk), 0) // CHUNK
            c = lax.broadcasted_iota(jnp.int32, (n_rows, tk), 1) // CHUNK
            visible = c <= r
        H = range(2)
        sls = [slice(h * HEAD_PAD, (h + 1) * HEAD_PAD) for h in H]
        s = [_dot_nt(q_ref[rows, sl], k_ref[pl.ds(k0, tk), sl]) for sl in sls]
        if masked:
            s = [jnp.where(visible, s[h], NEG_INF) for h in H]
        m_prev = [m_scr[h, rows, :] for h in H]
        m_new = [jnp.maximum(m_prev[h], jnp.max(s[h], axis=1, keepdims=True)) for h in H]
        pexp = [jnp.exp2(s[h] - jnp.tile(m_new[h], (1, reps))).astype(BF16) for h in H]
        pv = [_dot(pexp[h], v_ref[pl.ds(k0, tk), sls[h]]) for h in H]
        for h in H:
            acc_scr[h, rows, :] = jnp.exp2(m_prev[h] - m_new[h]) * acc_scr[h, rows, :] + pv[h]
            m_scr[h, rows, :] = m_new[h]

    def body(kj, carry):
        kv_block(kj, slice(0, tq), False)
        return carry

    lax.fori_loop(0, qi * n_diag, body, 0)
    for d in range(n_diag):
        kj = qi * n_diag + d
        kv_block(kj, slice(d * tk, (d + 1) * tk), True)
        if (d + 1) * tk < tq:
            kv_block(kj, slice((d + 1) * tk, tq), False)
    acc0, acc1 = acc_scr[0], acc_scr[1]
    lane = lax.broadcasted_iota(jnp.int32, acc0.shape, 1)
    l0 = acc0[:, V_ONE_LANE[0]:V_ONE_LANE[0] + 1]
    l1 = acc1[:, V_ONE_LANE[1]:V_ONE_LANE[1] + 1]
    o_ref[...] = jnp.where(lane < MLA_V, acc0 / l0, acc1 / l1).astype(o_ref.dtype)


def _attn_call(q, k, v, n_batch, seq):
    tq, tk = ATTN_TQ, ATTN_TK
    nq = seq // tq
    hp = MLA_HEADS // 2
    resident = pl.BlockSpec((seq, 2 * HEAD_PAD), lambda b, h, i: (b, h), pipeline_mode=pl.Buffered(1))
    return pl.pallas_call(
        functools.partial(_attn_kernel, tq=tq, tk=tk),
        grid=(n_batch, hp, nq),
        in_specs=[
            pl.BlockSpec((tq, 2 * HEAD_PAD), lambda b, h, i: (b * nq + i, h)),
            resident,
            resident,
        ],
        out_specs=pl.BlockSpec((tq, LANE), lambda b, h, i: (b * nq + i, h)),
        out_shape=jax.ShapeDtypeStruct((n_batch * seq, hp * LANE), BF16),
        scratch_shapes=[
            pltpu.VMEM((2, tq, LANE), F32),
            pltpu.VMEM((2, tq, LANE), F32),
        ],
        compiler_params=_cparams(("parallel", "parallel", "arbitrary")),
        name="attn",
    )(q, k, v)


def _mla_sample_kernel(q_ref, cn_ref, kn_ref, cp_ref, kp_ref, wuk_ref, wuv_ref, o_ref, *, past, dec):
    cp = cp_ref[0].astype(BF16)
    kp = kp_ref[0].astype(BF16)
    cn = cn_ref[...].astype(BF16)
    kn = kn_ref[...].astype(BF16)
    R = MLA_HEADS * dec
    qrow = (past + lax.broadcasted_iota(jnp.int32, (R, past), 0) % dec) // CHUNK
    vis_p = (lax.broadcasted_iota(jnp.int32, (R, past), 1) // CHUNK) <= qrow
    qrow_n = (past + lax.broadcasted_iota(jnp.int32, (R, dec), 0) % dec) // CHUNK
    vis_n = ((past + lax.broadcasted_iota(jnp.int32, (R, dec), 1)) // CHUNK) <= qrow_n
    q_lat, qp = [], []
    for h in range(MLA_HEADS):
        qn = q_ref[:, h * HEAD_PAD:h * HEAD_PAD + MLA_NOPE]
        qp.append(q_ref[:, h * HEAD_PAD + MLA_NOPE:h * HEAD_PAD + MLA_NOPE + MLA_ROPE])
        q_lat.append(_dot(qn, wuk_ref[h]).astype(BF16))
    q_lat = jnp.concatenate(q_lat, axis=0)
    qp = jnp.concatenate(qp, axis=0)
    s_p = jnp.where(vis_p, _dot_nt(q_lat, cp) + _dot_nt(qp, kp), NEG_INF)
    s_n = jnp.where(vis_n, _dot_nt(q_lat, cn) + _dot_nt(qp, kn), NEG_INF)
    m = jnp.maximum(jnp.max(s_p, axis=1, keepdims=True), jnp.max(s_n, axis=1, keepdims=True))
    e_p = jnp.exp2(s_p - m)
    e_n = jnp.exp2(s_n - m)
    l = jnp.sum(e_p, axis=1, keepdims=True) + jnp.sum(e_n, axis=1, keepdims=True)
    o_lat = ((_dot(e_p.astype(BF16), cp) + _dot(e_n.astype(BF16), cn)) / l).astype(BF16)
    out = jnp.zeros((dec, MLA_HEADS * MLA_V), F32)
    for h in range(MLA_HEADS):
        out = out + _dot(o_lat[h * dec:(h + 1) * dec], wuv_ref[h])
    o_ref[...] = out.astype(o_ref.dtype)


def _mla_sample_call(q, ckv, kpe, cache_ckv, cache_kpe, wuk, wuv, row0, n_seq, dec):
    past = cache_ckv.shape[1]
    blk0 = row0 // dec
    wide = MLA_HEADS * HEAD_PAD
    return pl.pallas_call(
        functools.partial(_mla_sample_kernel, past=past, dec=dec),
        grid=(n_seq,),
        in_specs=[
            pl.BlockSpec((dec, wide), lambda b: (blk0 + b, 0)),
            pl.BlockSpec((dec, KV_LORA), lambda b: (b, 0)),
            pl.BlockSpec((dec, MLA_ROPE), lambda b: (b, 0)),
            pl.BlockSpec((1, past, KV_LORA), lambda b: (b, 0, 0)),
            pl.BlockSpec((1, past, MLA_ROPE), lambda b: (b, 0, 0)),
            pl.BlockSpec((MLA_HEADS, MLA_NOPE, KV_LORA), lambda b: (0, 0, 0)),
            pl.BlockSpec((MLA_HEADS, KV_LORA, MLA_HEADS * MLA_V), lambda b: (0, 0, 0)),
        ],
        out_specs=pl.BlockSpec((dec, MLA_HEADS * MLA_V), lambda b: (b, 0)),
        out_shape=jax.ShapeDtypeStruct((n_seq * dec, MLA_HEADS * MLA_V), BF16),
        compiler_params=_cparams(("parallel",)),
        name="mla_sample",
    )(q, ckv, kpe, cache_ckv, cache_kpe, wuk, wuv)


def _prep_kernel(pr_ref, bnd_ref, mu_ref, w0_ref, a0_ref, kk_ref, ka_ref, rk_ref, w2_ref, a2_ref, g2_ref,
                 seg_ref, r_ref, lw_ref, kh_ref, v_ref, na_ref, b_ref, bonus_ref, g_ref, *, tm):
    ng = tm // SHIFT_GROUP
    pr = pr_ref[...]
    pr3 = pr.reshape(ng, SHIFT_GROUP, RWKV_IN)
    rolled = pltpu.roll(pr3, 1, 1)
    first = lax.broadcasted_iota(jnp.int32, pr3.shape, 1) == 0
    prev = jnp.where(first, bnd_ref[...], rolled).reshape(tm, RWKV_IN)
    u = pr + mu_ref[...] * (prev - pr)
    o1, o2, o3 = RWKV_DIM, 2 * RWKV_DIM, 3 * RWKV_DIM
    o4, o5 = o3 + DECAY_LORA, o3 + DECAY_LORA + AAA_LORA
    r, k, v = u[:, :o1], u[:, o1:o2], u[:, o2:o3]
    w_lo, a_lo, g_lo = u[:, o3:o4], u[:, o4:o5], u[:, o5:]
    wl = w0_ref[...] + _dot(jnp.tanh(w_lo).astype(BF16), w2_ref[...])
    lw_ref[...] = -math.exp(-0.5) * _sigmoid(wl)
    a = _sigmoid(a0_ref[...] + _dot(a_lo.astype(BF16), a2_ref[...]))
    g_ref[...] = _dot(_sigmoid(g_lo).astype(BF16), g2_ref[...])
    seg = seg_ref[...]
    kk = k * kk_ref[...]
    kk = kk / jnp.maximum(jnp.sqrt(_dot_exact_rhs(kk * kk, seg)), 1e-12)
    kh = k * (1.0 + (a - 1.0) * ka_ref[...])
    r_ref[...] = r
    kh_ref[...] = kh
    v_ref[...] = v
    na_ref[...] = -kk
    b_ref[...] = kk * a
    bonus_ref[...] = _dot_exact_rhs(r * kh * rk_ref[...], seg) * v


def _prep_call(pr, bnd, mu, w0, a0, k_k, k_a, r_k, w2, a2, g2, seg):
    T = pr.shape[0]
    tm = TOKEN_TILE
    row = lambda i: (i, 0)
    full = lambda i: (0, 0)
    vec = pl.BlockSpec((1, RWKV_DIM), full)
    out = pl.BlockSpec((tm, RWKV_DIM), row)
    return pl.pallas_call(
        functools.partial(_prep_kernel, tm=tm),
        grid=(T // tm,),
        in_specs=[
            pl.BlockSpec((tm, RWKV_IN), row),
            pl.BlockSpec((tm // SHIFT_GROUP, 1, RWKV_IN), lambda i: (i, 0, 0)),
            pl.BlockSpec((1, RWKV_IN), full),
            vec, vec, vec, vec, vec,
            pl.BlockSpec((DECAY_LORA, RWKV_DIM), full),
            pl.BlockSpec((AAA_LORA, RWKV_DIM), full),
            pl.BlockSpec((GATE_LORA, RWKV_DIM), full),
            pl.BlockSpec((RWKV_DIM, RWKV_DIM), full),
        ],
        out_specs=[out] * 8,
        out_shape=[jax.ShapeDtypeStruct((T, RWKV_DIM), F32)] * 8,
        compiler_params=_cparams(("parallel",)),
        name="rwkv_prep",
    )(pr, bnd, mu, w0, a0, k_k, k_a, r_k, w2, a2, g2, seg)


def _wkv_kernel(r_ref, lw_ref, k_ref, v_ref, a_ref, b_ref, h0_ref, y_ref, hT_ref, h_scr, *, C, n_sub):
    GW = WKV_GROUP * RWKV_N
    R = WKV_GROUP * C
    n_grp = RWKV_HEADS // WKV_GROUP
    n_lev = int(round(math.log2(C))) - 1
    c = pl.program_id(1)

    def head_block(hh):
        return slice(hh * RWKV_N, (hh + 1) * RWKV_N)

    @pl.when(c == 0)
    def _():
        h_scr[...] = jnp.zeros(h_scr.shape, F32)
        for hd in range(RWKV_HEADS):
            g, hh = divmod(hd, WKV_GROUP)
            h_scr[g, head_block(hh), head_block(hh)] = h0_ref[0, hd]

    row = lax.broadcasted_iota(jnp.int32, (C, C), 0)
    col = lax.broadcasted_iota(jnp.int32, (C, C), 1)
    tri = jnp.where(col <= row, 1.0, 0.0).astype(BF16)
    rr = lax.broadcasted_iota(jnp.int32, (R, R), 0)
    cc = lax.broadcasted_iota(jnp.int32, (R, R), 1)
    same = (rr // C) == (cc // C)
    strict = same & (cc < rr)
    lower = same & (cc <= rr)
    eye_r = jnp.where(rr == cc, 1.0, 0.0)
    keep = (lax.broadcasted_iota(jnp.int32, (R, GW), 0) // C
            == lax.broadcasted_iota(jnp.int32, (R, GW), 1) // RWKV_N)
    eye_g = (lax.broadcasted_iota(jnp.int32, (GW, GW), 0) == lax.broadcasted_iota(jnp.int32, (GW, GW), 1))
    eye_g_bf = jnp.where(eye_g, 1.0, 0.0).astype(BF16)

    def stack(x4):
        return jnp.where(keep, jnp.concatenate([x4] * WKV_GROUP, axis=0), jnp.zeros((), x4.dtype))

    sls = [slice(g * GW, (g + 1) * GW) for g in range(n_grp)]
    J = [(ci, g) for ci in range(n_sub) for g in range(n_grp)]
    ops, p_end = {}, {}
    for ci in range(n_sub):
        rows = slice(ci * C, (ci + 1) * C)
        lw = lw_ref[rows, :]
        cum = _dot_exact_lhs(tri, lw)
        cum_end = cum[C - 1:C, :]
        e_neg = jnp.exp(-cum)
        e_end = jnp.exp(cum_end - cum)
        b_in = b_ref[rows, :]
        k_in = k_ref[rows, :]
        full = ((a_ref[rows, :] * jnp.exp(cum - lw)).astype(BF16),
                (r_ref[rows, :] * jnp.exp(cum)).astype(BF16),
                (b_in * e_neg).astype(BF16), (k_in * e_neg).astype(BF16),
                (b_in * e_end).astype(BF16), (k_in * e_end).astype(BF16),
                v_ref[rows, :].astype(BF16))
        p_end[ci] = jnp.exp(cum_end)
        for g in range(n_grp):
            ops[ci, g] = [stack(t[:, sls[g]]) for t in full]
    a_s, r_s, b_s, k_s, be_s, ke_s, v_s = [{j: ops[j][i] for j in J} for i in range(7)]
    m = {j: _dot_nt(jnp.concatenate([a_s[j], r_s[j]], axis=0), jnp.concatenate([b_s[j], k_s[j]], axis=0))
         for j in J}
    l_ab = {j: jnp.where(strict, m[j][:R, :R], 0.0) for j in J}
    a_ak = {j: jnp.where(strict, m[j][:R, R:], 0.0).astype(BF16) for j in J}
    a_rb = {j: jnp.where(lower, m[j][R:, :R], 0.0).astype(BF16) for j in J}
    a_rk = {j: jnp.where(lower, m[j][R:, R:], 0.0).astype(BF16) for j in J}
    t_inv = {j: eye_r + l_ab[j] for j in J}
    l_pow = {j: l_ab[j].astype(BF16) for j in J}
    for _ in range(n_lev):
        l_pow = {j: _dot(l_pow[j], l_pow[j]).astype(BF16) for j in J}
        t_inv = {j: t_inv[j] + _dot(t_inv[j].astype(BF16), l_pow[j]) for j in J}
    t_b = {j: t_inv[j].astype(BF16) for j in J}
    bke_t = {j: _dot_nt(eye_g_bf, jnp.concatenate([be_s[j], ke_s[j]], axis=0)).astype(BF16) for j in J}

    G = range(n_grp)
    h_cur = [h_scr[g] for g in G]
    for ci in range(n_sub):
        rows = slice(ci * C, (ci + 1) * C)
        h0_b = [h_cur[g].astype(BF16) for g in G]
        x = [_dot(a_s[ci, g], h0_b[g]) + _dot(a_ak[ci, g], v_s[ci, g]) for g in G]
        u = [_dot(t_b[ci, g], x[g].astype(BF16)).astype(BF16) for g in G]
        y_bd = [_dot(r_s[ci, g], h0_b[g]) + _dot(a_rb[ci, g], u[g]) + _dot(a_rk[ci, g], v_s[ci, g]) for g in G]
        h_add = [_dot(bke_t[ci, g], jnp.concatenate([u[g], v_s[ci, g]], axis=0)) for g in G]
        for g in G:
            y4 = y_bd[g][:C]
            for hh in range(1, WKV_GROUP):
                y4 = y4 + y_bd[g][hh * C:(hh + 1) * C]
            y_ref[rows, sls[g]] = y4
            p_col = jnp.sum(jnp.where(eye_g, p_end[ci][:, sls[g]], 0.0), axis=1, keepdims=True)
            h_cur[g] = p_col * h_cur[g] + h_add[g]

    for g in G:
        h_scr[g] = h_cur[g]

    @pl.when(c == pl.num_programs(1) - 1)
    def _():
        for hd in range(RWKV_HEADS):
            g, hh = divmod(hd, WKV_GROUP)
            hT_ref[0, hd] = h_scr[g, head_block(hh), head_block(hh)]


def _wkv_call(arrs, h0, row0, n_seq, n_chunk, C, n_sub):
    rows = C * n_sub
    steps = n_chunk // n_sub
    blk0 = row0 // rows
    GW = WKV_GROUP * RWKV_N
    n_grp = RWKV_HEADS // WKV_GROUP
    tok = pl.BlockSpec((rows, RWKV_DIM), lambda b, c: (blk0 + b * steps + c, 0))
    out_tok = pl.BlockSpec((rows, RWKV_DIM), lambda b, c: (b * steps + c, 0))
    st = pl.BlockSpec((1, RWKV_HEADS, RWKV_N, RWKV_N), lambda b, c: (b, 0, 0, 0))
    return pl.pallas_call(
        functools.partial(_wkv_kernel, C=C, n_sub=n_sub),
        grid=(n_seq, steps),
        in_specs=[tok] * 6 + [st],
        out_specs=[out_tok, st],
        out_shape=[
            jax.ShapeDtypeStruct((n_seq * n_chunk * C, RWKV_DIM), F32),
            jax.ShapeDtypeStruct((n_seq, RWKV_HEADS, RWKV_N, RWKV_N), F32),
        ],
        scratch_shapes=[pltpu.VMEM((n_grp, GW, GW), F32)],
        compiler_params=_cparams(("parallel", "arbitrary")),
        name="wkv_c%d" % C,
    )(*arrs, h0)


def _mix_kernel(xp_ref, xs_ref, attnp_ref, attns_ref, yp_ref, ys_ref, bonus_ref, g_ref, seg_ref, lng_ref,
                lnb_ref, woa_ref, wob_ref, g1_ref, b1_ref, wr_ref, br_ref, h_ref, hpk_ref, route_ref, count_ref, *, n_p):
    is_p = pl.program_id(0) < n_p
    seg = seg_ref[...]
    y = jnp.where(is_p, yp_ref[...], ys_ref[...])
    attn = jnp.where(is_p, attnp_ref[...], attns_ref[...])
    x = jnp.where(is_p, xp_ref[...], xs_ref[...])
    inv_n = 1.0 / RWKV_N
    yc = y - _dot_exact_rhs(y, seg) * inv_n
    var = _dot_exact_rhs(yc * yc, seg) * inv_n
    yn = yc * lax.rsqrt(var + GN_EPS) * lng_ref[...] + lnb_ref[...]
    rw = ((yn + bonus_ref[...]) * g_ref[...]).astype(BF16)
    m = _dot(attn, woa_ref[...]) + _dot(rw, wob_ref[...])
    h = _layer_norm(DN_ALPHA * x + m, g1_ref[...], b1_ref[...])
    h_ref[...] = h
    hpk_ref[...] = _pack_bf16_pairs(h)

    h_hi = h.astype(BF16)
    h_lo = (h - h_hi.astype(F32)).astype(BF16)
    logits = _dot(h_hi, wr_ref[0]) + _dot(h_lo, wr_ref[0]) + _dot(h_hi, wr_ref[1]) + br_ref[...]
    lane = lax.broadcasted_iota(jnp.int32, logits.shape, 1)
    big = jnp.int32(LANE)
    gl = jnp.where(lane < N_GROUPS, logits, NEG_INF)
    gmax = jnp.max(gl, axis=1, keepdims=True)
    grp = jnp.min(jnp.where(gl == gmax, lane, big), axis=1, keepdims=True)
    p_grp = 1.0 / jnp.sum(jnp.exp(gl - gmax), axis=1, keepdims=True)
    e_idx = lane - N_GROUPS
    in_grp = (lane >= N_GROUPS) & (lane < N_GROUPS + N_EXPERTS) & ((e_idx // EXPERTS_PER_GROUP) == grp)
    el = jnp.where(in_grp, logits, NEG_INF)
    m1 = jnp.max(el, axis=1, keepdims=True)
    i1 = jnp.min(jnp.where(el == m1, lane, big), axis=1, keepdims=True)
    el2 = jnp.where(lane == i1, NEG_INF, el)
    m2 = jnp.max(el2, axis=1, keepdims=True)
    i2 = jnp.min(jnp.where(el2 == m2, lane, big), axis=1, keepdims=True)
    t = jnp.exp(m2 - m1)
    g1 = p_grp / (1.0 + t)
    g2 = g1 * t
    e1 = (i1 - N_GROUPS).astype(F32)
    e2 = (i2 - N_GROUPS).astype(F32)
    tm = logits.shape[0]
    chosen = jnp.where(lane == i1, 1.0, jnp.where(lane == i2, 1.0, 0.0))
    earlier = (lax.broadcasted_iota(jnp.int32, (tm, tm), 1) < lax.broadcasted_iota(jnp.int32, (tm, tm), 0))
    before = _dot(jnp.where(earlier, 1.0, 0.0).astype(BF16), chosen.astype(BF16))
    r1 = jnp.sum(jnp.where(lane == i1, before, 0.0), axis=1, keepdims=True)
    r2 = jnp.sum(jnp.where(lane == i2, before, 0.0), axis=1, keepdims=True)
    count_ref[0] = jnp.sum(chosen, axis=0, keepdims=True)
    cols = (e1, e2, g1, g2, r1, r2)
    route = jnp.zeros(logits.shape, F32)
    for j, col in enumerate(cols):
        route = jnp.where(lane == j, col, route)
    route_ref[...] = route


def _mix_call(xp, xs, attn_p, attn_s, y_p, y_s, bonus, g, seg, lnx_g, lnx_b, woa, wob, ln1_g, ln1_b, wr, br):
    T = xp.shape[0] + xs.shape[0]
    tm = TOKEN_TILE
    n_p = xp.shape[0] // tm
    row = lambda i: (i, 0)
    full = lambda i: (0, 0)
    row_p, row_s = _split_rows(n_p)
    half = pl.BlockSpec((tm, RWKV_DIM), row)
    vec5 = pl.BlockSpec((1, RWKV_DIM), full)
    vec10 = pl.BlockSpec((1, D_MODEL), full)
    return pl.pallas_call(
        functools.partial(_mix_kernel, n_p=n_p),
        grid=(T // tm,),
        in_specs=[
            pl.BlockSpec((tm, D_MODEL), row_p), pl.BlockSpec((tm, D_MODEL), row_s),
            pl.BlockSpec((tm, RWKV_DIM), row_p), pl.BlockSpec((tm, RWKV_DIM), row_s),
            pl.BlockSpec((tm, RWKV_DIM), row_p), pl.BlockSpec((tm, RWKV_DIM), row_s),
            half, half,
            pl.BlockSpec((RWKV_DIM, RWKV_DIM), full), vec5, vec5,
            pl.BlockSpec((RWKV_DIM, D_MODEL), full), pl.BlockSpec((RWKV_DIM, D_MODEL), full),
            vec10, vec10,
            pl.BlockSpec((2, D_MODEL, LANE), lambda i: (0, 0, 0)), pl.BlockSpec((1, LANE), full),
        ],
        out_specs=[pl.BlockSpec((tm, D_MODEL), row), pl.BlockSpec((tm, D_MODEL // 2), row),
                   pl.BlockSpec((tm, LANE), row), pl.BlockSpec((1, 1, LANE), lambda i: (i, 0, 0))],
        out_shape=[jax.ShapeDtypeStruct((T, D_MODEL), F32), jax.ShapeDtypeStruct((T, D_MODEL // 2), jnp.int32),
                   jax.ShapeDtypeStruct((T, LANE), F32), jax.ShapeDtypeStruct((T // tm, 1, LANE), F32)],
        compiler_params=_cparams(("parallel",)),
        name="mix",
    )(xp, xs, attn_p, attn_s, y_p, y_s, bonus, g, seg, lnx_g, lnx_b, woa, wob, ln1_g, ln1_b, wr, br)


def _expert_kernel(be_ref, nu_ref, xs_ref, wg_ref, wu_ref, wd_ref, ys_ref):
    i = pl.program_id(0)

    @pl.when(i < nu_ref[0])
    def _():
        xb = _unpack_bf16_pairs(xs_ref[...]).astype(BF16)
        gate = _dot(xb, wg_ref[0].astype(BF16))
        up = _dot(xb, wu_ref[0].astype(BF16))
        act = (gate * _sigmoid(gate) * up).astype(BF16)
        ys_ref[...] = _pack_bf16_pairs(_dot(act, wd_ref[0].astype(BF16)))

    @pl.when(i >= nu_ref[0])
    def _():
        ys_ref[...] = jnp.zeros(ys_ref.shape, ys_ref.dtype)


def _expert_call(block_e, n_used, xs, wg, wu, wd):
    n_blk = xs.shape[0] // MOE_BLK
    grid_spec = pltpu.PrefetchScalarGridSpec(
        num_scalar_prefetch=2,
        grid=(n_blk,),
        in_specs=[
            pl.BlockSpec((MOE_BLK, D_MODEL // 2), lambda i, be, nu: (i, 0)),
            pl.BlockSpec((1, D_MODEL, D_EXPERT), lambda i, be, nu: (be[i], 0, 0)),
            pl.BlockSpec((1, D_MODEL, D_EXPERT), lambda i, be, nu: (be[i], 0, 0)),
            pl.BlockSpec((1, D_EXPERT, D_MODEL), lambda i, be, nu: (be[i], 0, 0)),
        ],
        out_specs=pl.BlockSpec((MOE_BLK, D_MODEL // 2), lambda i, be, nu: (i, 0)),
    )
    return pl.pallas_call(
        _expert_kernel,
        grid_spec=grid_spec,
        out_shape=jax.ShapeDtypeStruct((n_blk * MOE_BLK, D_MODEL // 2), jnp.int32),
        compiler_params=_cparams(("arbitrary",)),
        name="experts",
    )(block_e, n_used, xs, wg, wu, wd)


def _combine_kernel(h_ref, ya_ref, yb_ref, route_ref, g2_ref, b2_ref, op_ref, os_ref, *, n_p):
    i = pl.program_id(0)
    route = route_ref[...]
    f = _unpack_bf16_pairs(ya_ref[...]) * route[:, 2:3] + _unpack_bf16_pairs(yb_ref[...]) * route[:, 3:4]
    out = _layer_norm(DN_ALPHA * h_ref[...] + f, g2_ref[...], b2_ref[...])

    @pl.when(i < n_p)
    def _():
        op_ref[...] = out

    @pl.when(i >= n_p)
    def _():
        os_ref[...] = out


def _combine_call(h, yab, route, ln2_g, ln2_b, t_prompt):
    T = h.shape[0]
    tm = TOKEN_TILE
    n_t, n_p = T // tm, t_prompt // tm
    row = lambda i: (i, 0)
    full = lambda i: (0, 0)
    row_p, row_s = _split_rows(n_p)
    big = pl.BlockSpec((tm, D_MODEL), row)
    return pl.pallas_call(
        functools.partial(_combine_kernel, n_p=n_p),
        grid=(n_t,),
        in_specs=[big, pl.BlockSpec((tm, D_MODEL // 2), row),
                  pl.BlockSpec((tm, D_MODEL // 2), lambda i: (i + n_t, 0)), pl.BlockSpec((tm, LANE), row),
                  pl.BlockSpec((1, D_MODEL), full), pl.BlockSpec((1, D_MODEL), full)],
        out_specs=[pl.BlockSpec((tm, D_MODEL), row_p), pl.BlockSpec((tm, D_MODEL), row_s)],
        out_shape=[jax.ShapeDtypeStruct((t_prompt, D_MODEL), F32),
                   jax.ShapeDtypeStruct((T - t_prompt, D_MODEL), F32)],
        compiler_params=_cparams(("arbitrary",)),
        name="combine",
    )(h, yab, yab, route, ln2_g, ln2_b)


def _prep_weights(w_in, w_uq, w_ukv):
    half = MLA_ROPE // 2
    kpe_w = w_in[:, Q_LORA + KV_LORA:MLA_IN]
    kpe_b = jnp.concatenate([-kpe_w[:, half:], kpe_w[:, :half]], axis=1)
    w1 = jnp.concatenate([w_in[:, :Q_LORA + KV_LORA], kpe_w, kpe_b,
                          jnp.zeros((D_MODEL, 64), F32), w_in[:, MLA_IN:]], axis=1).astype(BF16)
    pad_q = jnp.zeros((Q_LORA, MLA_HEADS, HEAD_PAD - MLA_NOPE - MLA_ROPE), F32)
    wqa = jnp.concatenate([w_uq, pad_q], axis=2).reshape(Q_LORA, -1).astype(BF16)
    rot = jnp.concatenate([jnp.zeros((Q_LORA, MLA_HEADS, MLA_NOPE), F32),
                           -w_uq[:, :, MLA_NOPE + half:], w_uq[:, :, MLA_NOPE:MLA_NOPE + half], pad_q], axis=2)
    wqb = rot.reshape(Q_LORA, -1).astype(BF16)
    w_uk, w_uv = w_ukv[:, :, :MLA_NOPE], w_ukv[:, :, MLA_NOPE:]
    wk = jnp.concatenate([w_uk, jnp.zeros((KV_LORA, MLA_HEADS, HEAD_PAD - MLA_NOPE), F32)], axis=2)
    wk = wk.reshape(KV_LORA, -1).astype(BF16)
    pk_np = np.zeros((MLA_ROPE, MLA_HEADS * HEAD_PAD), np.float32)
    for h in range(MLA_HEADS):
        for i in range(MLA_ROPE):
            pk_np[i, h * HEAD_PAD + MLA_NOPE + i] = 1.0
    pk = jnp.asarray(pk_np).astype(BF16)
    zv = jnp.zeros((KV_LORA, MLA_HEADS // 2, MLA_V), F32)
    wv4 = w_uv.reshape(KV_LORA, MLA_HEADS // 2, 2, MLA_V)
    wv = jnp.stack([jnp.concatenate([wv4[:, :, 0], zv], axis=2),
                    jnp.concatenate([zv, wv4[:, :, 1]], axis=2)], axis=2)
    wv = wv.reshape(KV_LORA, -1).astype(BF16)
    wuk = jnp.transpose(w_uk, (1, 2, 0)).astype(BF16)
    wuv_np = np.zeros((MLA_HEADS, MLA_HEADS * MLA_V), np.float32)
    for h in range(MLA_HEADS):
        wuv_np[h, h * MLA_V:(h + 1) * MLA_V] = 1.0
    wuv = jnp.transpose(w_uv, (1, 0, 2))
    wuv = (jnp.tile(wuv, (1, 1, MLA_HEADS)) * jnp.asarray(wuv_np)[:, None, :]).astype(BF16)
    return w1, wqa, wqb, wk, pk, wv, wuk, wuv


def _rope_table(pos):
    inv = ROPE_BASE ** (-jnp.arange(0, MLA_ROPE, 2, dtype=F32) / MLA_ROPE)
    ang = pos.astype(F32)[:, None] * inv[None, :]
    cos, sin = jnp.cos(ang), jnp.sin(ang)
    n = pos.shape[0]
    one = jnp.ones((n, MLA_NOPE), F32)
    z32 = jnp.zeros((n, HEAD_PAD - MLA_NOPE - MLA_ROPE), F32)
    z64 = jnp.zeros((n, MLA_NOPE), F32)
    cq = jnp.concatenate([one, cos, cos, z32], axis=1) * (MLA_SCALE * LOG2E)
    sq = jnp.concatenate([z64, sin, sin, z32], axis=1) * (MLA_SCALE * LOG2E)
    kt = jnp.concatenate([cos, cos, sin, sin, z64], axis=1)
    return jnp.concatenate([cq, sq, kt], axis=1)


def _seg_ones():
    idx = np.arange(RWKV_DIM) // RWKV_N
    return jnp.asarray((idx[:, None] == idx[None, :]).astype(np.float32)).astype(BF16)


def _dispatch(route, tile_counts, t_total):
    A = t_total * TOP_K
    n_tiles = tile_counts.shape[0]
    counts_te = tile_counts[:, 0, N_GROUPS:N_GROUPS + N_EXPERTS].astype(jnp.int32)
    counts = jnp.sum(counts_te, axis=0)
    blocks_per_e = (counts + MOE_BLK - 1) // MOE_BLK
    blk_end = jnp.cumsum(blocks_per_e)
    blk_start = blk_end - blocks_per_e
    tile_off = jnp.cumsum(counts_te, axis=0) - counts_te
    base = blk_start[None, :] * MOE_BLK + tile_off
    e = route[:, :TOP_K].astype(jnp.int32)
    rank = route[:, 4:4 + TOP_K].astype(jnp.int32)
    base_tok = jnp.repeat(base, t_total // n_tiles, axis=0)
    pick = e[:, :, None] == jnp.arange(N_EXPERTS, dtype=jnp.int32)[None, None, :]
    dest = jnp.sum(jnp.where(pick, base_tok[:, None, :], 0), axis=-1) + rank
    n_blk = -(-A // MOE_BLK) + N_EXPERTS
    blk = jnp.arange(n_blk, dtype=jnp.int32)
    block_e = jnp.minimum(jnp.sum((blk[:, None] >= blk_end[None, :]).astype(jnp.int32), axis=1),
                          N_EXPERTS - 1).astype(jnp.int32)
    n_used = blk_end[-1:].astype(jnp.int32)
    return dest, block_e, n_used


def _sc_gather_rows(table, idx):
    n_rows, width = idx.shape[0], table.shape[1]
    n_workers = SC_CORES * SC_SUBCORES
    per_worker = n_rows // n_workers
    assert n_rows % n_workers == 0 and per_worker % SC_WINDOW == 0
    mesh = plsc.VectorSubcoreMesh(core_axis_name="c", subcore_axis_name="s")

    @functools.partial(
        pl.kernel, mesh=mesh,
        out_type=jax.ShapeDtypeStruct((n_rows, width), table.dtype),
        scratch_types=[
            pltpu.VMEM((SC_WINDOW,), jnp.int32),
            pltpu.VMEM((SC_WINDOW, width), table.dtype),
            pltpu.SemaphoreType.DMA,
        ],
    )
    def gather(table_hbm, idx_hbm, out_hbm, idx_v, rows_v, sem):
        wid = lax.axis_index("s") * SC_CORES + lax.axis_index("c")
        base = wid * per_worker

        @pl.loop(0, per_worker // SC_WINDOW)
        def _(w):
            off = pl.multiple_of(base + w * SC_WINDOW, SC_WINDOW)
            pltpu.sync_copy(idx_hbm.at[pl.ds(off, SC_WINDOW)], idx_v)
            pltpu.async_copy(table_hbm.at[idx_v], rows_v, sem).wait()
            pltpu.sync_copy(rows_v, out_hbm.at[pl.ds(off, SC_WINDOW)])

    return gather(table, idx)


def _sc_scatter_rows(src, idx_a, idx_b, n_slots):
    n_rows, width = src.shape
    n_workers = SC_CORES * SC_SUBCORES
    per_worker = n_rows // n_workers
    assert n_rows % n_workers == 0 and per_worker % SC_WINDOW == 0
    n_win = per_worker // SC_WINDOW
    mesh = plsc.VectorSubcoreMesh(core_axis_name="c", subcore_axis_name="s")

    @functools.partial(
        pl.kernel, mesh=mesh,
        out_type=jax.ShapeDtypeStruct((n_slots, width), src.dtype),
        scratch_types=[
            pltpu.VMEM((1, SC_WINDOW), jnp.int32),
            pltpu.VMEM((1, SC_WINDOW), jnp.int32),
            pltpu.VMEM((SC_WINDOW, width), src.dtype),
        ],
    )
    def scatter(src_hbm, ia_hbm, ib_hbm, out_hbm, ia_v, ib_v, rows_v):
        wid = lax.axis_index("s") * SC_CORES + lax.axis_index("c")

        @pl.loop(0, n_win)
        def _(w):
            win = wid * n_win + w
            off = pl.multiple_of(win * SC_WINDOW, SC_WINDOW)
            pltpu.sync_copy(src_hbm.at[pl.ds(off, SC_WINDOW)], rows_v)
            pltpu.sync_copy(ia_hbm.at[pl.ds(win, 1)], ia_v)
            pltpu.sync_copy(ib_hbm.at[pl.ds(win, 1)], ib_v)
            pltpu.sync_copy(rows_v, out_hbm.at[ia_v.at[0]])
            pltpu.sync_copy(rows_v, out_hbm.at[ib_v.at[0]])

    return scatter(src, idx_a, idx_b)


def kernel(x_prompt, x_sample, cache_ckv, cache_kpe, state_wkv, state_shift, w_in, q_norm_g, kv_norm_g, w_uq,
           w_ukv, mu_shift, w0, w2, a0, a2, g2, k_k, k_a, r_k, lnx_g, lnx_b, w_o, ln1_g, ln1_b, w_gr, b_gr,
           w_er, b_er, w_eg, w_eu, w_ed, ln2_g, ln2_b):
    B, S, D = x_prompt.shape
    DB, DS, _ = x_sample.shape
    past = cache_ckv.shape[2]
    Tp, Ts = B * S, DB * DS
    T = Tp + Ts
    assert D == D_MODEL and DS == SHIFT_GROUP and S % ATTN_TQ == 0 and S % (CHUNK * WKV_SUB) == 0
    assert Tp % TOKEN_TILE == 0 and T % TOKEN_TILE == 0 and w_in.shape[0] == DEPTH

    l = 0
    xp, xs_in = x_prompt.reshape(Tp, D), x_sample.reshape(Ts, D)
    w1, wqa, wqb, wk, pk, wv, wuk, wuv = _prep_weights(w_in[l], w_uq[l], w_ukv[l])
    pos = jnp.concatenate([jnp.arange(S, dtype=jnp.int32),
                           jnp.tile(past + jnp.arange(DS, dtype=jnp.int32), TOKEN_TILE // DS)])
    rope = _rope_table(pos)

    q, kcat, vcat, ckv_p, ckv_s, kpe_p, kpe_s, pr, last_rows = _proj_call(
        xp, xs_in, rope, S // TOKEN_TILE, w1, q_norm_g[l][None], kv_norm_g[l][None], wqa, wqb, wk, pk, wv)

    attn_p = _attn_call(q, kcat, vcat, B, S)
    attn_s = _mla_sample_call(q, ckv_s, kpe_s, cache_ckv[l], cache_kpe[l], wuk, wuv, Tp, DB, DS)

    bnd = jnp.concatenate([jnp.zeros((1, RWKV_IN), F32), last_rows[:-1]], axis=0)
    gidx = jnp.arange(T // SHIFT_GROUP)
    seq_start = (gidx < Tp // SHIFT_GROUP) & (gidx % (S // SHIFT_GROUP) == 0)
    bnd = jnp.where(seq_start[:, None], 0.0, bnd)
    bnd = jnp.concatenate([bnd[:Tp // SHIFT_GROUP], state_shift[l]], axis=0)[:, None, :]

    seg = _seg_ones()
    vec = lambda a: a.reshape(1, -1)
    r, lw, kh, v, na, b, bonus, g = _prep_call(
        pr, bnd, vec(mu_shift[l]), vec(w0[l]), vec(a0[l]), vec(k_k[l]), vec(k_a[l]), vec(r_k[l]),
        w2[l].astype(BF16), a2[l].astype(BF16), g2[l].astype(BF16), seg)

    scan_in = (r, lw, kh, v, na, b)
    h0_p = jnp.zeros((B, RWKV_HEADS, RWKV_N, RWKV_N), F32)
    y_p, hT_p = _wkv_call(scan_in, h0_p, 0, B, S // CHUNK, CHUNK, WKV_SUB)
    h0_s = jnp.swapaxes(state_wkv[l], -1, -2)
    y_s, hT_s = _wkv_call(scan_in, h0_s, Tp, DB, 1, DS, 1)

    wo_b = w_o[l].astype(BF16)
    wr = jnp.concatenate([w_gr[l], w_er[l], jnp.zeros((D, LANE - N_GROUPS - N_EXPERTS), F32)], axis=1)
    wr_hi = wr.astype(BF16)
    wr_lo = (wr - wr_hi.astype(F32)).astype(BF16)
    br = jnp.concatenate([b_gr[l], b_er[l], jnp.zeros((LANE - N_GROUPS - N_EXPERTS,), F32)])[None]
    h, hpk, route, tile_counts = _mix_call(
        xp, xs_in, attn_p, attn_s, y_p, y_s, bonus, g, seg, vec(lnx_g[l]), vec(lnx_b[l]),
        wo_b[:MLA_HEADS * MLA_V], wo_b[MLA_HEADS * MLA_V:], vec(ln1_g[l]), vec(ln1_b[l]),
        jnp.stack([wr_hi, wr_lo]), br)

    dest, block_e, n_used = _dispatch(route, tile_counts, T)
    win = lambda a: a.reshape(T // SC_WINDOW, SC_WINDOW)
    xs = _sc_scatter_rows(hpk, win(dest[:, 0]), win(dest[:, 1]), block_e.shape[0] * MOE_BLK)
    ys = _expert_call(block_e, n_used, xs, w_eg[l], w_eu[l], w_ed[l])
    yab = _sc_gather_rows(ys, jnp.concatenate([dest[:, 0], dest[:, 1]]))
    out_p, out_s = _combine_call(h, yab, route, vec(ln2_g[l]), vec(ln2_b[l]), Tp)

    y_prompt = out_p.reshape(B, S, D)
    y_sample = out_s.reshape(DB, DS, D)
    p_ckv = ckv_p.reshape(1, B, S, KV_LORA)
    p_kpe = kpe_p.reshape(1, B, S, MLA_ROPE)
    s_ckv = ckv_s.reshape(1, DB, DS, KV_LORA)
    s_kpe = kpe_s.reshape(1, DB, DS, MLA_ROPE)
    p_wkv = jnp.swapaxes(hT_p, -1, -2)[None]
    s_wkv = jnp.swapaxes(hT_s, -1, -2)[None]
    gp = S // SHIFT_GROUP
    p_sh = last_rows[gp - 1:B * gp:gp][None]
    s_sh = last_rows[B * gp:][None]
    return (y_prompt, y_sample, p_ckv, p_kpe, p_wkv, p_sh, s_ckv, s_kpe, s_wkv, s_sh)
```

```python
import functools
import math

import numpy as np
import jax
import jax.numpy as jnp
from jax import lax
from jax.experimental import pallas as pl
from jax.experimental.pallas import tpu as pltpu
from jax.experimental.pallas import tpu_sc as plsc

F32 = jnp.float32
BF16 = jnp.bfloat16

D_MODEL = 1024
CHUNK = 64
MLA_HEADS = 8
MLA_NOPE = 64
MLA_ROPE = 32
MLA_V = 64
Q_LORA = 384
KV_LORA = 256
ROPE_BASE = 10000.0
MLA_IN = Q_LORA + KV_LORA + MLA_ROPE
MLA_SCALE = (MLA_NOPE + MLA_ROPE) ** -0.5
RWKV_HEADS = 8
RWKV_N = 64
RWKV_DIM = RWKV_HEADS * RWKV_N
DECAY_LORA = 64
AAA_LORA = 64
GATE_LORA = 128
RWKV_IN = 3 * RWKV_DIM + DECAY_LORA + AAA_LORA + GATE_LORA
N_GROUPS = 4
EXPERTS_PER_GROUP = 8
N_EXPERTS = N_GROUPS * EXPERTS_PER_GROUP
TOP_K = 2
D_EXPERT = 256
MOE_BLK = 256
LN_EPS = 1e-5
RMS_EPS = 1e-6
GN_EPS = 64e-5
NEG_INF = -1e30
DEPTH = 1
DN_ALPHA = (2 * DEPTH) ** 0.25

LANE = 128
HEAD_PAD = 128
PROJ_W = 768 + RWKV_IN
SHIFT_GROUP = 32
TOKEN_TILE = 512
ATTN_TQ = 1024
ATTN_TK = 512
V_ONE_LANE = (MLA_V, 0)
LOG2E = math.log2(math.e)
VMEM_LIMIT = 48 * 1024 * 1024
SC_CORES = 2
SC_SUBCORES = 16
SC_WINDOW = 32
WKV_GROUP = 4
WKV_SUB = 4


def _cparams(sem):
    return pltpu.CompilerParams(dimension_semantics=sem, vmem_limit_bytes=VMEM_LIMIT)


def _split3(x):
    hi = x.astype(BF16)
    r1 = x - hi.astype(F32)
    mid = r1.astype(BF16)
    lo = (r1 - mid.astype(F32)).astype(BF16)
    return hi, mid, lo


def _dot(a, b):
    return jnp.dot(a, b, preferred_element_type=F32)


def _dot_nt(a, b):
    return lax.dot_general(a, b, (((1,), (1,)), ((), ())), preferred_element_type=F32)


def _dot_exact_rhs(x, w):
    hi = x.astype(BF16)
    lo = (x - hi.astype(F32)).astype(BF16)
    return _dot(hi, w) + _dot(lo, w)


def _dot_exact_lhs(w, x):
    hi, mid, lo = _split3(x)
    return _dot(w, hi) + _dot(w, mid) + _dot(w, lo)


def _pack_bf16_pairs(x):
    n = x.shape[1] // 2
    bits = pltpu.bitcast(x.astype(BF16).astype(F32), jnp.int32)
    return (bits[:, :n] & jnp.int32(-65536)) | lax.shift_right_logical(bits[:, n:], jnp.int32(16))


def _unpack_bf16_pairs(p):
    hi = pltpu.bitcast(p & jnp.int32(-65536), F32)
    lo = pltpu.bitcast(lax.shift_left(p, jnp.int32(16)), F32)
    return jnp.concatenate([hi, lo], axis=1)


def _sigmoid(x):
    return 1.0 / (1.0 + jnp.exp(-x))


def _layer_norm(x, g, b):
    xc = x - jnp.mean(x, -1, keepdims=True)
    var = jnp.mean(xc * xc, -1, keepdims=True)
    return xc * lax.rsqrt(var + LN_EPS) * g + b


def _proj_kernel(xp_ref, xs_ref, rope_ref, w1_ref, gq_ref, gkv_ref, wqa_ref, wqb_ref, wk_ref, pk_ref, wv_ref,
                 q_ref, k_ref, v_ref, ckvp_ref, ckvs_ref, kpep_ref, kpes_ref, pr_ref, last_ref, *, n_p, tm):
    i = pl.program_id(0)
    is_p = i < n_p
    x = jnp.where(is_p, xp_ref[...], xs_ref[...]).astype(BF16)
    proj = _dot(x, w1_ref[...])
    c_q = proj[:, :Q_LORA]
    c_kv = proj[:, Q_LORA:Q_LORA + KV_LORA]
    kp = proj[:, 640:768]
    pr = proj[:, 768:]
    pr_ref[...] = pr
    last_ref[...] = pr.reshape(tm // SHIFT_GROUP, SHIFT_GROUP, RWKV_IN)[:, SHIFT_GROUP - 1, :]

    cqn = c_q * lax.rsqrt(jnp.mean(c_q * c_q, -1, keepdims=True) + RMS_EPS) * gq_ref[...]
    ckv = c_kv * lax.rsqrt(jnp.mean(c_kv * c_kv, -1, keepdims=True) + RMS_EPS) * gkv_ref[...]

    rope = rope_ref[...]
    cq = rope[:, :LANE]
    sq = rope[:, LANE:2 * LANE]
    kt = rope[:, 2 * LANE:]
    prod = kp * kt
    kpe = prod[:, :MLA_ROPE] + prod[:, MLA_ROPE:2 * MLA_ROPE]

    @pl.when(is_p)
    def _():
        ckvp_ref[...] = ckv
        kpep_ref[...] = kpe

    @pl.when(jnp.logical_not(is_p))
    def _():
        ckvs_ref[...] = ckv
        kpes_ref[...] = kpe

    cqb = cqn.astype(BF16)
    qa = _dot(cqb, wqa_ref[...])
    qb = _dot(cqb, wqb_ref[...])
    for h in range(MLA_HEADS):
        sl = slice(h * HEAD_PAD, (h + 1) * HEAD_PAD)
        q_ref[:, sl] = (qa[:, sl] * cq + qb[:, sl] * sq).astype(BF16)

    ckv_b = ckv.astype(BF16)
    k = _dot(ckv_b, wk_ref[...]) + _dot(kpe.astype(BF16), pk_ref[...])
    k_ref[...] = k.astype(BF16)
    lane = lax.broadcasted_iota(jnp.int32, (1, MLA_HEADS * HEAD_PAD), 1)
    odd = (lane // HEAD_PAD) % 2
    one_lane = jnp.where(odd == 1, V_ONE_LANE[1], V_ONE_LANE[0])
    v_one = jnp.where(lane % HEAD_PAD == one_lane, 1.0, 0.0)
    v_ref[...] = (_dot(ckv_b, wv_ref[...]) + v_one).astype(BF16)


def _split_rows(n_p):
    return (lambda i: (jnp.minimum(i, n_p - 1), 0)), (lambda i: (jnp.maximum(i - n_p, 0), 0))


def _proj_call(xp, xs, rope, rope_tiles, w1, gq, gkv, wqa, wqb, wk, pk, wv):
    Tp, Ts = xp.shape[0], xs.shape[0]
    T = Tp + Ts
    tm = TOKEN_TILE
    n_p = Tp // tm
    row = lambda i: (i, 0)
    full = lambda i: (0, 0)
    row_p, row_s = _split_rows(n_p)
    wide = MLA_HEADS * HEAD_PAD
    ng = tm // SHIFT_GROUP
    return pl.pallas_call(
        functools.partial(_proj_kernel, n_p=n_p, tm=tm),
        grid=(T // tm,),
        in_specs=[
            pl.BlockSpec((tm, D_MODEL), row_p),
            pl.BlockSpec((tm, D_MODEL), row_s),
            pl.BlockSpec((tm, 3 * LANE), lambda i: (jnp.where(i < n_p, i % rope_tiles, rope_tiles), 0)),
            pl.BlockSpec((D_MODEL, PROJ_W), full),
            pl.BlockSpec((1, Q_LORA), full),
            pl.BlockSpec((1, KV_LORA), full),
            pl.BlockSpec((Q_LORA, wide), full),
            pl.BlockSpec((Q_LORA, wide), full),
            pl.BlockSpec((KV_LORA, wide), full),
            pl.BlockSpec((MLA_ROPE, wide), full),
            pl.BlockSpec((KV_LORA, wide), full),
        ],
        out_specs=[
            pl.BlockSpec((tm, wide), row),
            pl.BlockSpec((tm, wide), row),
            pl.BlockSpec((tm, wide), row),
            pl.BlockSpec((tm, KV_LORA), row_p),
            pl.BlockSpec((tm, KV_LORA), row_s),
            pl.BlockSpec((tm, MLA_ROPE), row_p),
            pl.BlockSpec((tm, MLA_ROPE), row_s),
            pl.BlockSpec((tm, RWKV_IN), row),
            pl.BlockSpec((ng, RWKV_IN), row),
        ],
        out_shape=[
            jax.ShapeDtypeStruct((T, wide), BF16),
            jax.ShapeDtypeStruct((T, wide), BF16),
            jax.ShapeDtypeStruct((T, wide), BF16),
            jax.ShapeDtypeStruct((Tp, KV_LORA), F32),
            jax.ShapeDtypeStruct((Ts, KV_LORA), F32),
            jax.ShapeDtypeStruct((Tp, MLA_ROPE), F32),
            jax.ShapeDtypeStruct((Ts, MLA_ROPE), F32),
            jax.ShapeDtypeStruct((T, RWKV_IN), F32),
            jax.ShapeDtypeStruct((T // SHIFT_GROUP, RWKV_IN), F32),
        ],
        compiler_params=_cparams(("arbitrary",)),
        name="proj",
    )(xp, xs, rope, w1, gq, gkv, wqa, wqb, wk, pk, wv)


def _attn_kernel(q_ref, k_ref, v_ref, o_ref, m_scr, acc_scr, *, tq, tk):
    qi = pl.program_id(2)
    m_scr[...] = jnp.full(m_scr.shape, NEG_INF, F32)
    acc_scr[...] = jnp.zeros(acc_scr.shape, F32)
    n_diag = tq // tk

    def kv_block(k0, width, rows, masked):
        n_rows = rows.stop - rows.start
        if masked:
            r = lax.broadcasted_iota(jnp.int32, (n_rows, width), 0) // CHUNK
            c = lax.broadcasted_iota(jnp.int32, (n_rows, width), 1) // CHUNK
            visible = c <= r
        H = range(2)
        sls = [slice(h * HEAD_PAD, (h + 1) * HEAD_PAD) for h in H]
        s = [_dot_nt(q_ref[rows, sl], k_ref[pl.ds(k0, width), sl]) for sl in sls]
        if masked:
            s = [jnp.where(visible, s[h], NEG_INF) for h in H]
        m_prev = [m_scr[h, rows, :] for h in H]
        m_new = [jnp.maximum(m_prev[h], jnp.max(s[h], axis=1, keepdims=True)) for h in H]
        pexp = [jnp.exp2(s[h] - jnp.tile(m_new[h], (1, width // LANE))).astype(BF16) for h in H]
        pv = [_dot(pexp[h], v_ref[pl.ds(k0, width), sls[h]]) for h in H]
        for h in H:
            acc_scr[h, rows, :] = jnp.exp2(m_prev[h] - m_new[h]) * acc_scr[h, rows, :] + pv[h]
            m_scr[h, rows, :] = m_new[h]

    def body(kj, carry):
        kv_block(pl.multiple_of(kj * tq, tq), tq, slice(0, tq), False)
        return carry

    lax.fori_loop(0, qi, body, 0)
    for d in range(n_diag):
        k0 = pl.multiple_of(qi * tq + d * tk, tk)
        kv_block(k0, tk, slice(d * tk, (d + 1) * tk), True)
        if (d + 1) * tk < tq:
            kv_block(k0, tk, slice((d + 1) * tk, tq), False)
    acc0, acc1 = acc_scr[0], acc_scr[1]
    lane = lax.broadcasted_iota(jnp.int32, acc0.shape, 1)
    l0 = acc0[:, V_ONE_LANE[0]:V_ONE_LANE[0] + 1]
    l1 = acc1[:, V_ONE_LANE[1]:V_ONE_LANE[1] + 1]
    o_ref[...] = jnp.where(lane < MLA_V, acc0 / l0, acc1 / l1).astype(o_ref.dtype)


def _attn_call(q, k, v, n_batch, seq):
    tq, tk = ATTN_TQ, ATTN_TK
    nq = seq // tq
    hp = MLA_HEADS // 2
    resident = pl.BlockSpec((seq, 2 * HEAD_PAD), lambda b, h, i: (b, h), pipeline_mode=pl.Buffered(1))
    return pl.pallas_call(
        functools.partial(_attn_kernel, tq=tq, tk=tk),
        grid=(n_batch, hp, nq),
        in_specs=[
            pl.BlockSpec((tq, 2 * HEAD_PAD), lambda b, h, i: (b * nq + i, h)),
            resident,
            resident,
        ],
        out_specs=pl.BlockSpec((tq, LANE), lambda b, h, i: (b * nq + i, h)),
        out_shape=jax.ShapeDtypeStruct((n_batch * seq, hp * LANE), BF16),
        scratch_shapes=[
            pltpu.VMEM((2, tq, LANE), F32),
            pltpu.VMEM((2, tq, LANE), F32),
        ],
        compiler_params=_cparams(("parallel", "parallel", "arbitrary")),
        name="attn",
    )(q, k, v)


def _mla_sample_kernel(q_ref, cn_ref, kn_ref, cp_ref, kp_ref, wuk_ref, wuv_ref, o_ref, *, past, dec):
    cp = cp_ref[0].astype(BF16)
    kp = kp_ref[0].astype(BF16)
    cn = cn_ref[...].astype(BF16)
    kn = kn_ref[...].astype(BF16)
    R = MLA_HEADS * dec
    qrow = (past + lax.broadcasted_iota(jnp.int32, (R, past), 0) % dec) // CHUNK
    vis_p = (lax.broadcasted_iota(jnp.int32, (R, past), 1) // CHUNK) <= qrow
    qrow_n = (past + lax.broadcasted_iota(jnp.int32, (R, dec), 0) % dec) // CHUNK
    vis_n = ((past + lax.broadcasted_iota(jnp.int32, (R, dec), 1)) // CHUNK) <= qrow_n
    q_lat, qp = [], []
    for h in range(MLA_HEADS):
        qn = q_ref[:, h * HEAD_PAD:h * HEAD_PAD + MLA_NOPE]
        qp.append(q_ref[:, h * HEAD_PAD + MLA_NOPE:h * HEAD_PAD + MLA_NOPE + MLA_ROPE])
        q_lat.append(_dot(qn, wuk_ref[h]).astype(BF16))
    q_lat = jnp.concatenate(q_lat, axis=0)
    qp = jnp.concatenate(qp, axis=0)
    s_p = jnp.where(vis_p, _dot_nt(q_lat, cp) + _dot_nt(qp, kp), NEG_INF)
    s_n = jnp.where(vis_n, _dot_nt(q_lat, cn) + _dot_nt(qp, kn), NEG_INF)
    m = jnp.maximum(jnp.max(s_p, axis=1, keepdims=True), jnp.max(s_n, axis=1, keepdims=True))
    e_p = jnp.exp2(s_p - m)
    e_n = jnp.exp2(s_n - m)
    l = jnp.sum(e_p, axis=1, keepdims=True) + jnp.sum(e_n, axis=1, keepdims=True)
    o_lat = ((_dot(e_p.astype(BF16), cp) + _dot(e_n.astype(BF16), cn)) / l).astype(BF16)
    out = jnp.zeros((dec, MLA_HEADS * MLA_V), F32)
    for h in range(MLA_HEADS):
        out = out + _dot(o_lat[h * dec:(h + 1) * dec], wuv_ref[h])
    o_ref[...] = out.astype(o_ref.dtype)


def _mla_sample_call(q, ckv, kpe, cache_ckv, cache_kpe, wuk, wuv, row0, n_seq, dec):
    past = cache_ckv.shape[1]
    blk0 = row0 // dec
    wide = MLA_HEADS * HEAD_PAD
    return pl.pallas_call(
        functools.partial(_mla_sample_kernel, past=past, dec=dec),
        grid=(n_seq,),
        in_specs=[
            pl.BlockSpec((dec, wide), lambda b: (blk0 + b, 0)),
            pl.BlockSpec((dec, KV_LORA), lambda b: (b, 0)),
            pl.BlockSpec((dec, MLA_ROPE), lambda b: (b, 0)),
            pl.BlockSpec((1, past, KV_LORA), lambda b: (b, 0, 0)),
            pl.BlockSpec((1, past, MLA_ROPE), lambda b: (b, 0, 0)),
            pl.BlockSpec((MLA_HEADS, MLA_NOPE, KV_LORA), lambda b: (0, 0, 0)),
            pl.BlockSpec((MLA_HEADS, KV_LORA, MLA_HEADS * MLA_V), lambda b: (0, 0, 0)),
        ],
        out_specs=pl.BlockSpec((dec, MLA_HEADS * MLA_V), lambda b: (b, 0)),
        out_shape=jax.ShapeDtypeStruct((n_seq * dec, MLA_HEADS * MLA_V), BF16),
        compiler_params=_cparams(("parallel",)),
        name="mla_sample",
    )(q, ckv, kpe, cache_ckv, cache_kpe, wuk, wuv)


def _prep_kernel(pr_ref, bnd_ref, mu_ref, w0_ref, a0_ref, kk_ref, ka_ref, rk_ref, w2_ref, a2_ref, g2_ref,
                 seg_ref, r_ref, lw_ref, kh_ref, v_ref, na_ref, b_ref, bonus_ref, g_ref, *, tm):
    ng = tm // SHIFT_GROUP
    pr = pr_ref[...]
    pr3 = pr.reshape(ng, SHIFT_GROUP, RWKV_IN)
    rolled = pltpu.roll(pr3, 1, 1)
    first = lax.broadcasted_iota(jnp.int32, pr3.shape, 1) == 0
    prev = jnp.where(first, bnd_ref[...], rolled).reshape(tm, RWKV_IN)
    u = pr + mu_ref[...] * (prev - pr)
    o1, o2, o3 = RWKV_DIM, 2 * RWKV_DIM, 3 * RWKV_DIM
    o4, o5 = o3 + DECAY_LORA, o3 + DECAY_LORA + AAA_LORA
    r, k, v = u[:, :o1], u[:, o1:o2], u[:, o2:o3]
    w_lo, a_lo, g_lo = u[:, o3:o4], u[:, o4:o5], u[:, o5:]
    wl = w0_ref[...] + _dot(jnp.tanh(w_lo).astype(BF16), w2_ref[...])
    lw_ref[...] = -math.exp(-0.5) * _sigmoid(wl)
    a = _sigmoid(a0_ref[...] + _dot(a_lo.astype(BF16), a2_ref[...]))
    g_ref[...] = _dot(_sigmoid(g_lo).astype(BF16), g2_ref[...])
    seg = seg_ref[...]
    kk = k * kk_ref[...]
    kk = kk / jnp.maximum(jnp.sqrt(_dot_exact_rhs(kk * kk, seg)), 1e-12)
    kh = k * (1.0 + (a - 1.0) * ka_ref[...])
    r_ref[...] = r
    kh_ref[...] = kh
    v_ref[...] = v
    na_ref[...] = -kk
    b_ref[...] = kk * a
    bonus_ref[...] = _dot_exact_rhs(r * kh * rk_ref[...], seg) * v


def _prep_call(pr, bnd, mu, w0, a0, k_k, k_a, r_k, w2, a2, g2, seg):
    T = pr.shape[0]
    tm = TOKEN_TILE
    row = lambda i: (i, 0)
    full = lambda i: (0, 0)
    vec = pl.BlockSpec((1, RWKV_DIM), full)
    out = pl.BlockSpec((tm, RWKV_DIM), row)
    return pl.pallas_call(
        functools.partial(_prep_kernel, tm=tm),
        grid=(T // tm,),
        in_specs=[
            pl.BlockSpec((tm, RWKV_IN), row),
            pl.BlockSpec((tm // SHIFT_GROUP, 1, RWKV_IN), lambda i: (i, 0, 0)),
            pl.BlockSpec((1, RWKV_IN), full),
            vec, vec, vec, vec, vec,
            pl.BlockSpec((DECAY_LORA, RWKV_DIM), full),
            pl.BlockSpec((AAA_LORA, RWKV_DIM), full),
            pl.BlockSpec((GATE_LORA, RWKV_DIM), full),
            pl.BlockSpec((RWKV_DIM, RWKV_DIM), full),
        ],
        out_specs=[out] * 8,
        out_shape=[jax.ShapeDtypeStruct((T, RWKV_DIM), F32)] * 8,
        compiler_params=_cparams(("parallel",)),
        name="rwkv_prep",
    )(pr, bnd, mu, w0, a0, k_k, k_a, r_k, w2, a2, g2, seg)


def _wkv_kernel(r_ref, lw_ref, k_ref, v_ref, a_ref, b_ref, h0_ref, y_ref, hT_ref, h_scr, *, C, n_sub):
    GW = WKV_GROUP * RWKV_N
    R = WKV_GROUP * C
    n_grp = RWKV_HEADS // WKV_GROUP
    n_lev = int(round(math.log2(C))) - 1
    c = pl.program_id(1)

    def head_block(hh):
        return slice(hh * RWKV_N, (hh + 1) * RWKV_N)

    @pl.when(c == 0)
    def _():
        h_scr[...] = jnp.zeros(h_scr.shape, F32)
        for hd in range(RWKV_HEADS):
            g, hh = divmod(hd, WKV_GROUP)
            h_scr[g, head_block(hh), head_block(hh)] = h0_ref[0, hd]

    row = lax.broadcasted_iota(jnp.int32, (C, C), 0)
    col = lax.broadcasted_iota(jnp.int32, (C, C), 1)
    tri = jnp.where(col <= row, 1.0, 0.0).astype(BF16)
    rr = lax.broadcasted_iota(jnp.int32, (R, R), 0)
    cc = lax.broadcasted_iota(jnp.int32, (R, R), 1)
    same = (rr // C) == (cc // C)
    ti = lax.broadcasted_iota(jnp.int32, (C, R), 0)
    si = lax.broadcasted_iota(jnp.int32, (C, R), 1) % C
    strict4 = si < ti
    lower4 = si <= ti
    eye4 = jnp.where(si == ti, 1.0, 0.0)
    keep = (lax.broadcasted_iota(jnp.int32, (R, GW), 0) // C
            == lax.broadcasted_iota(jnp.int32, (R, GW), 1) // RWKV_N)
    gr = lax.broadcasted_iota(jnp.int32, (GW, GW), 0)
    gc = lax.broadcasted_iota(jnp.int32, (GW, GW), 1)
    eye_g = gr == gc
    same_head = (gr // RWKV_N) == (gc // RWKV_N)
    eye_g_bf = jnp.where(eye_g, 1.0, 0.0).astype(BF16)

    def rows4(x4):
        return jnp.concatenate([x4] * WKV_GROUP, axis=0)

    def stack(x4):
        return jnp.where(keep, rows4(x4), jnp.zeros((), x4.dtype))


    sls = [slice(g * GW, (g + 1) * GW) for g in range(n_grp)]
    J = [(ci, g) for ci in range(n_sub) for g in range(n_grp)]
    ops, p_end = {}, {}
    for ci in range(n_sub):
        rows = slice(ci * C, (ci + 1) * C)
        lw = lw_ref[rows, :]
        cum = _dot_exact_lhs(tri, lw)
        cum_end = cum[C - 1:C, :]
        e_neg = jnp.exp(-cum)
        e_end = jnp.exp(cum_end - cum)
        b_in = b_ref[rows, :]
        k_in = k_ref[rows, :]
        full = ((a_ref[rows, :] * jnp.exp(cum - lw)).astype(BF16),
                (r_ref[rows, :] * jnp.exp(cum)).astype(BF16),
                (b_in * e_neg).astype(BF16), (k_in * e_neg).astype(BF16),
                (b_in * e_end).astype(BF16), (k_in * e_end).astype(BF16),
                v_ref[rows, :].astype(BF16))
        p_end[ci] = jnp.exp(cum_end)
        for g in range(n_grp):
            ops[ci, g] = [t[:, sls[g]] for t in full]
    a4, r4, b4, k4, be4, ke4, v4 = [{j: ops[j][i] for j in J} for i in range(7)]
    v_s = {j: stack(v4[j]) for j in J}
    m = {j: _dot_nt(jnp.concatenate([a4[j], r4[j]], axis=0),
                    jnp.concatenate([stack(b4[j]), stack(k4[j])], axis=0)) for j in J}
    l4 = {j: jnp.where(strict4, m[j][:C, :R], 0.0) for j in J}
    a_ak = {j: jnp.where(strict4, m[j][:C, R:], 0.0).astype(BF16) for j in J}
    a_rb = {j: jnp.where(lower4, m[j][C:, :R], 0.0).astype(BF16) for j in J}
    a_rk = {j: jnp.where(lower4, m[j][C:, R:], 0.0).astype(BF16) for j in J}
    def block_diag(x4):
        return jnp.where(same, rows4(x4.astype(BF16)), jnp.zeros((), BF16))

    t4 = {j: eye4 + l4[j] for j in J}
    l_bd = {j: block_diag(l4[j]) for j in J}
    for _ in range(n_lev):
        l4 = {j: _dot(l4[j].astype(BF16), l_bd[j]) for j in J}
        l_bd = {j: block_diag(l4[j]) for j in J}
        t4 = {j: t4[j] + _dot(t4[j].astype(BF16), l_bd[j]) for j in J}
    t_b = {j: t4[j].astype(BF16) for j in J}
    bke_t = {j: _dot_nt(eye_g_bf, jnp.concatenate([be4[j], ke4[j]], axis=0)).astype(BF16) for j in J}

    G = range(n_grp)
    h_cur = [h_scr[g] for g in G]
    for ci in range(n_sub):
        rows = slice(ci * C, (ci + 1) * C)
        h0_b = [h_cur[g].astype(BF16) for g in G]
        x4 = [_dot(a4[ci, g], h0_b[g]) + _dot(a_ak[ci, g], v_s[ci, g]) for g in G]
        u4 = [_dot(t_b[ci, g], stack(x4[g].astype(BF16))).astype(BF16) for g in G]
        y4 = [_dot(r4[ci, g], h0_b[g]) + _dot(a_rb[ci, g], stack(u4[g])) + _dot(a_rk[ci, g], v_s[ci, g])
              for g in G]
        uv4 = [jnp.concatenate([u4[g], v4[ci, g]], axis=0) for g in G]
        h_add = [jnp.where(same_head, _dot(bke_t[ci, g], uv4[g]), 0.0) for g in G]
        for g in G:
            y_ref[rows, sls[g]] = y4[g]
            p_col = jnp.sum(jnp.where(eye_g, p_end[ci][:, sls[g]], 0.0), axis=1, keepdims=True)
            h_cur[g] = p_col * h_cur[g] + h_add[g]

    for g in G:
        h_scr[g] = h_cur[g]

    @pl.when(c == pl.num_programs(1) - 1)
    def _():
        for hd in range(RWKV_HEADS):
            g, hh = divmod(hd, WKV_GROUP)
            hT_ref[0, hd] = h_scr[g, head_block(hh), head_block(hh)]


def _wkv_call(arrs, h0, row0, n_seq, n_chunk, C, n_sub):
    rows = C * n_sub
    steps = n_chunk // n_sub
    blk0 = row0 // rows
    GW = WKV_GROUP * RWKV_N
    n_grp = RWKV_HEADS // WKV_GROUP
    tok = pl.BlockSpec((rows, RWKV_DIM), lambda b, c: (blk0 + b * steps + c, 0))
    out_tok = pl.BlockSpec((rows, RWKV_DIM), lambda b, c: (b * steps + c, 0))
    st = pl.BlockSpec((1, RWKV_HEADS, RWKV_N, RWKV_N), lambda b, c: (b, 0, 0, 0))
    return pl.pallas_call(
        functools.partial(_wkv_kernel, C=C, n_sub=n_sub),
        grid=(n_seq, steps),
        in_specs=[tok] * 6 + [st],
        out_specs=[out_tok, st],
        out_shape=[
            jax.ShapeDtypeStruct((n_seq * n_chunk * C, RWKV_DIM), F32),
            jax.ShapeDtypeStruct((n_seq, RWKV_HEADS, RWKV_N, RWKV_N), F32),
        ],
        scratch_shapes=[pltpu.VMEM((n_grp, GW, GW), F32)],
        compiler_params=_cparams(("parallel", "arbitrary")),
        name="wkv_c%d" % C,
    )(*arrs, h0)


def _mix_kernel(xp_ref, xs_ref, attnp_ref, attns_ref, yp_ref, ys_ref, bonus_ref, g_ref, seg_ref, lng_ref,
                lnb_ref, woa_ref, wob_ref, g1_ref, b1_ref, wr_ref, br_ref, h_ref, hpk_ref, route_ref, count_ref, *, n_p):
    is_p = pl.program_id(0) < n_p
    seg = seg_ref[...]
    y = jnp.where(is_p, yp_ref[...], ys_ref[...])
    attn = jnp.where(is_p, attnp_ref[...], attns_ref[...])
    x = jnp.where(is_p, xp_ref[...], xs_ref[...])
    inv_n = 1.0 / RWKV_N
    yc = y - _dot_exact_rhs(y, seg) * inv_n
    var = _dot_exact_rhs(yc * yc, seg) * inv_n
    yn = yc * lax.rsqrt(var + GN_EPS) * lng_ref[...] + lnb_ref[...]
    rw = ((yn + bonus_ref[...]) * g_ref[...]).astype(BF16)
    m = _dot(attn, woa_ref[...]) + _dot(rw, wob_ref[...])
    h = _layer_norm(DN_ALPHA * x + m, g1_ref[...], b1_ref[...])
    h_ref[...] = h
    hpk_ref[...] = _pack_bf16_pairs(h)

    h_hi = h.astype(BF16)
    h_lo = (h - h_hi.astype(F32)).astype(BF16)
    logits = _dot(h_hi, wr_ref[0]) + _dot(h_lo, wr_ref[0]) + _dot(h_hi, wr_ref[1]) + br_ref[...]
    lane = lax.broadcasted_iota(jnp.int32, logits.shape, 1)
    big = jnp.int32(LANE)
    gl = jnp.where(lane < N_GROUPS, logits, NEG_INF)
    gmax = jnp.max(gl, axis=1, keepdims=True)
    grp = jnp.min(jnp.where(gl == gmax, lane, big), axis=1, keepdims=True)
    p_grp = 1.0 / jnp.sum(jnp.exp(gl - gmax), axis=1, keepdims=True)
    e_idx = lane - N_GROUPS
    in_grp = (lane >= N_GROUPS) & (lane < N_GROUPS + N_EXPERTS) & ((e_idx // EXPERTS_PER_GROUP) == grp)
    el = jnp.where(in_grp, logits, NEG_INF)
    m1 = jnp.max(el, axis=1, keepdims=True)
    i1 = jnp.min(jnp.where(el == m1, lane, big), axis=1, keepdims=True)
    el2 = jnp.where(lane == i1, NEG_INF, el)
    m2 = jnp.max(el2, axis=1, keepdims=True)
    i2 = jnp.min(jnp.where(el2 == m2, lane, big), axis=1, keepdims=True)
    t = jnp.exp(m2 - m1)
    g1 = p_grp / (1.0 + t)
    g2 = g1 * t
    e1 = (i1 - N_GROUPS).astype(F32)
    e2 = (i2 - N_GROUPS).astype(F32)
    tm = logits.shape[0]
    chosen = jnp.where(lane == i1, 1.0, jnp.where(lane == i2, 1.0, 0.0))
    earlier = (lax.broadcasted_iota(jnp.int32, (tm, tm), 1) < lax.broadcasted_iota(jnp.int32, (tm, tm), 0))
    before = _dot(jnp.where(earlier, 1.0, 0.0).astype(BF16), chosen.astype(BF16))
    r1 = jnp.sum(jnp.where(lane == i1, before, 0.0), axis=1, keepdims=True)
    r2 = jnp.sum(jnp.where(lane == i2, before, 0.0), axis=1, keepdims=True)
    count_ref[0] = jnp.sum(chosen, axis=0, keepdims=True)
    cols = (e1, e2, g1, g2, r1, r2)
    route = jnp.zeros(logits.shape, F32)
    for j, col in enumerate(cols):
        route = jnp.where(lane == j, col, route)
    route_ref[...] = route


def _mix_call(xp, xs, attn_p, attn_s, y_p, y_s, bonus, g, seg, lnx_g, lnx_b, woa, wob, ln1_g, ln1_b, wr, br):
    T = xp.shape[0] + xs.shape[0]
    tm = TOKEN_TILE
    n_p = xp.shape[0] // tm
    row = lambda i: (i, 0)
    full = lambda i: (0, 0)
    row_p, row_s = _split_rows(n_p)
    half = pl.BlockSpec((tm, RWKV_DIM), row)
    vec5 = pl.BlockSpec((1, RWKV_DIM), full)
    vec10 = pl.BlockSpec((1, D_MODEL), full)
    return pl.pallas_call(
        functools.partial(_mix_kernel, n_p=n_p),
        grid=(T // tm,),
        in_specs=[
            pl.BlockSpec((tm, D_MODEL), row_p), pl.BlockSpec((tm, D_MODEL), row_s),
            pl.BlockSpec((tm, RWKV_DIM), row_p), pl.BlockSpec((tm, RWKV_DIM), row_s),
            pl.BlockSpec((tm, RWKV_DIM), row_p), pl.BlockSpec((tm, RWKV_DIM), row_s),
            half, half,
            pl.BlockSpec((RWKV_DIM, RWKV_DIM), full), vec5, vec5,
            pl.BlockSpec((RWKV_DIM, D_MODEL), full), pl.BlockSpec((RWKV_DIM, D_MODEL), full),
            vec10, vec10,
            pl.BlockSpec((2, D_MODEL, LANE), lambda i: (0, 0, 0)), pl.BlockSpec((1, LANE), full),
        ],
        out_specs=[pl.BlockSpec((tm, D_MODEL), row), pl.BlockSpec((tm, D_MODEL // 2), row),
                   pl.BlockSpec((tm, LANE), row), pl.BlockSpec((1, 1, LANE), lambda i: (i, 0, 0))],
        out_shape=[jax.ShapeDtypeStruct((T, D_MODEL), F32), jax.ShapeDtypeStruct((T, D_MODEL // 2), jnp.int32),
                   jax.ShapeDtypeStruct((T, LANE), F32), jax.ShapeDtypeStruct((T // tm, 1, LANE), F32)],
        compiler_params=_cparams(("parallel",)),
        name="mix",
    )(xp, xs, attn_p, attn_s, y_p, y_s, bonus, g, seg, lnx_g, lnx_b, woa, wob, ln1_g, ln1_b, wr, br)


def _expert_kernel(be_ref, nu_ref, xs_ref, wg_ref, wu_ref, wd_ref, ys_ref):
    i = pl.program_id(0)

    @pl.when(i < nu_ref[0])
    def _():
        xb = _unpack_bf16_pairs(xs_ref[...]).astype(BF16)
        gate = _dot(xb, wg_ref[0].astype(BF16))
        up = _dot(xb, wu_ref[0].astype(BF16))
        act = (gate * _sigmoid(gate) * up).astype(BF16)
        ys_ref[...] = _pack_bf16_pairs(_dot(act, wd_ref[0].astype(BF16)))

    @pl.when(i >= nu_ref[0])
    def _():
        ys_ref[...] = jnp.zeros(ys_ref.shape, ys_ref.dtype)


def _expert_call(block_e, n_used, xs, wg, wu, wd):
    n_blk = xs.shape[0] // MOE_BLK
    grid_spec = pltpu.PrefetchScalarGridSpec(
        num_scalar_prefetch=2,
        grid=(n_blk,),
        in_specs=[
            pl.BlockSpec((MOE_BLK, D_MODEL // 2), lambda i, be, nu: (i, 0)),
            pl.BlockSpec((1, D_MODEL, D_EXPERT), lambda i, be, nu: (be[i], 0, 0)),
            pl.BlockSpec((1, D_MODEL, D_EXPERT), lambda i, be, nu: (be[i], 0, 0)),
            pl.BlockSpec((1, D_EXPERT, D_MODEL), lambda i, be, nu: (be[i], 0, 0)),
        ],
        out_specs=pl.BlockSpec((MOE_BLK, D_MODEL // 2), lambda i, be, nu: (i, 0)),
    )
    return pl.pallas_call(
        _expert_kernel,
        grid_spec=grid_spec,
        out_shape=jax.ShapeDtypeStruct((n_blk * MOE_BLK, D_MODEL // 2), jnp.int32),
        compiler_params=_cparams(("arbitrary",)),
        name="experts",
    )(block_e, n_used, xs, wg, wu, wd)


def _combine_kernel(h_ref, ya_ref, yb_ref, route_ref, g2_ref, b2_ref, op_ref, os_ref, *, n_p):
    i = pl.program_id(0)
    route = route_ref[...]
    f = _unpack_bf16_pairs(ya_ref[...]) * route[:, 2:3] + _unpack_bf16_pairs(yb_ref[...]) * route[:, 3:4]
    out = _layer_norm(DN_ALPHA * h_ref[...] + f, g2_ref[...], b2_ref[...])

    @pl.when(i < n_p)
    def _():
        op_ref[...] = out

    @pl.when(i >= n_p)
    def _():
        os_ref[...] = out


def _combine_call(h, yab, route, ln2_g, ln2_b, t_prompt):
    T = h.shape[0]
    tm = TOKEN_TILE
    n_t, n_p = T // tm, t_prompt // tm
    row = lambda i: (i, 0)
    full = lambda i: (0, 0)
    row_p, row_s = _split_rows(n_p)
    big = pl.BlockSpec((tm, D_MODEL), row)
    return pl.pallas_call(
        functools.partial(_combine_kernel, n_p=n_p),
        grid=(n_t,),
        in_specs=[big, pl.BlockSpec((tm, D_MODEL // 2), row),
                  pl.BlockSpec((tm, D_MODEL // 2), lambda i: (i + n_t, 0)), pl.BlockSpec((tm, LANE), row),
                  pl.BlockSpec((1, D_MODEL), full), pl.BlockSpec((1, D_MODEL), full)],
        out_specs=[pl.BlockSpec((tm, D_MODEL), row_p), pl.BlockSpec((tm, D_MODEL), row_s)],
        out_shape=[jax.ShapeDtypeStruct((t_prompt, D_MODEL), F32),
                   jax.ShapeDtypeStruct((T - t_prompt, D_MODEL), F32)],
        compiler_params=_cparams(("arbitrary",)),
        name="combine",
    )(h, yab, yab, route, ln2_g, ln2_b)


def _prep_weights(w_in, w_uq, w_ukv):
    half = MLA_ROPE // 2
    kpe_w = w_in[:, Q_LORA + KV_LORA:MLA_IN]
    kpe_b = jnp.concatenate([-kpe_w[:, half:], kpe_w[:, :half]], axis=1)
    w1 = jnp.concatenate([w_in[:, :Q_LORA + KV_LORA], kpe_w, kpe_b,
                          jnp.zeros((D_MODEL, 64), F32), w_in[:, MLA_IN:]], axis=1).astype(BF16)
    pad_q = jnp.zeros((Q_LORA, MLA_HEADS, HEAD_PAD - MLA_NOPE - MLA_ROPE), F32)
    wqa = jnp.concatenate([w_uq, pad_q], axis=2).reshape(Q_LORA, -1).astype(BF16)
    rot = jnp.concatenate([jnp.zeros((Q_LORA, MLA_HEADS, MLA_NOPE), F32),
                           -w_uq[:, :, MLA_NOPE + half:], w_uq[:, :, MLA_NOPE:MLA_NOPE + half], pad_q], axis=2)
    wqb = rot.reshape(Q_LORA, -1).astype(BF16)
    w_uk, w_uv = w_ukv[:, :, :MLA_NOPE], w_ukv[:, :, MLA_NOPE:]
    wk = jnp.concatenate([w_uk, jnp.zeros((KV_LORA, MLA_HEADS, HEAD_PAD - MLA_NOPE), F32)], axis=2)
    wk = wk.reshape(KV_LORA, -1).astype(BF16)
    pk_np = np.zeros((MLA_ROPE, MLA_HEADS * HEAD_PAD), np.float32)
    for h in range(MLA_HEADS):
        for i in range(MLA_ROPE):
            pk_np[i, h * HEAD_PAD + MLA_NOPE + i] = 1.0
    pk = jnp.asarray(pk_np).astype(BF16)
    zv = jnp.zeros((KV_LORA, MLA_HEADS // 2, MLA_V), F32)
    wv4 = w_uv.reshape(KV_LORA, MLA_HEADS // 2, 2, MLA_V)
    wv = jnp.stack([jnp.concatenate([wv4[:, :, 0], zv], axis=2),
                    jnp.concatenate([zv, wv4[:, :, 1]], axis=2)], axis=2)
    wv = wv.reshape(KV_LORA, -1).astype(BF16)
    wuk = jnp.transpose(w_uk, (1, 2, 0)).astype(BF16)
    wuv_np = np.zeros((MLA_HEADS, MLA_HEADS * MLA_V), np.float32)
    for h in range(MLA_HEADS):
        wuv_np[h, h * MLA_V:(h + 1) * MLA_V] = 1.0
    wuv = jnp.transpose(w_uv, (1, 0, 2))
    wuv = (jnp.tile(wuv, (1, 1, MLA_HEADS)) * jnp.asarray(wuv_np)[:, None, :]).astype(BF16)
    return w1, wqa, wqb, wk, pk, wv, wuk, wuv


def _rope_table(pos):
    inv = ROPE_BASE ** (-jnp.arange(0, MLA_ROPE, 2, dtype=F32) / MLA_ROPE)
    ang = pos.astype(F32)[:, None] * inv[None, :]
    cos, sin = jnp.cos(ang), jnp.sin(ang)
    n = pos.shape[0]
    one = jnp.ones((n, MLA_NOPE), F32)
    z32 = jnp.zeros((n, HEAD_PAD - MLA_NOPE - MLA_ROPE), F32)
    z64 = jnp.zeros((n, MLA_NOPE), F32)
    cq = jnp.concatenate([one, cos, cos, z32], axis=1) * (MLA_SCALE * LOG2E)
    sq = jnp.concatenate([z64, sin, sin, z32], axis=1) * (MLA_SCALE * LOG2E)
    kt = jnp.concatenate([cos, cos, sin, sin, z64], axis=1)
    return jnp.concatenate([cq, sq, kt], axis=1)


def _seg_ones():
    idx = np.arange(RWKV_DIM) // RWKV_N
    return jnp.asarray((idx[:, None] == idx[None, :]).astype(np.float32)).astype(BF16)


def _dispatch(route, tile_counts, t_total):
    A = t_total * TOP_K
    n_tiles = tile_counts.shape[0]
    counts_te = tile_counts[:, 0, N_GROUPS:N_GROUPS + N_EXPERTS].astype(jnp.int32)
    counts = jnp.sum(counts_te, axis=0)
    blocks_per_e = (counts + MOE_BLK - 1) // MOE_BLK
    blk_end = jnp.cumsum(blocks_per_e)
    blk_start = blk_end - blocks_per_e
    tile_off = jnp.cumsum(counts_te, axis=0) - counts_te
    base = blk_start[None, :] * MOE_BLK + tile_off
    e = route[:, :TOP_K].astype(jnp.int32)
    rank = route[:, 4:4 + TOP_K].astype(jnp.int32)
    base_tok = jnp.repeat(base, t_total // n_tiles, axis=0)
    pick = e[:, :, None] == jnp.arange(N_EXPERTS, dtype=jnp.int32)[None, None, :]
    dest = jnp.sum(jnp.where(pick, base_tok[:, None, :], 0), axis=-1) + rank
    n_blk = -(-A // MOE_BLK) + N_EXPERTS
    blk = jnp.arange(n_blk, dtype=jnp.int32)
    block_e = jnp.minimum(jnp.sum((blk[:, None] >= blk_end[None, :]).astype(jnp.int32), axis=1),
                          N_EXPERTS - 1).astype(jnp.int32)
    n_used = blk_end[-1:].astype(jnp.int32)
    return dest, block_e, n_used


def _sc_gather_rows(table, idx):
    n_rows, width = idx.shape[0], table.shape[1]
    n_workers = SC_CORES * SC_SUBCORES
    per_worker = n_rows // n_workers
    assert n_rows % n_workers == 0 and per_worker % SC_WINDOW == 0
    mesh = plsc.VectorSubcoreMesh(core_axis_name="c", subcore_axis_name="s")

    @functools.partial(
        pl.kernel, mesh=mesh,
        out_type=jax.ShapeDtypeStruct((n_rows, width), table.dtype),
        scratch_types=[
            pltpu.VMEM((SC_WINDOW,), jnp.int32),
            pltpu.VMEM((SC_WINDOW, width), table.dtype),
            pltpu.SemaphoreType.DMA,
        ],
    )
    def gather(table_hbm, idx_hbm, out_hbm, idx_v, rows_v, sem):
        wid = lax.axis_index("s") * SC_CORES + lax.axis_index("c")
        base = wid * per_worker

        @pl.loop(0, per_worker // SC_WINDOW)
        def _(w):
            off = pl.multiple_of(base + w * SC_WINDOW, SC_WINDOW)
            pltpu.sync_copy(idx_hbm.at[pl.ds(off, SC_WINDOW)], idx_v)
            pltpu.async_copy(table_hbm.at[idx_v], rows_v, sem).wait()
            pltpu.sync_copy(rows_v, out_hbm.at[pl.ds(off, SC_WINDOW)])

    return gather(table, idx)


def _sc_scatter_rows(src, idx_a, idx_b, n_slots):
    n_rows, width = src.shape
    n_workers = SC_CORES * SC_SUBCORES
    per_worker = n_rows // n_workers
    assert n_rows % n_workers == 0 and per_worker % SC_WINDOW == 0
    n_win = per_worker // SC_WINDOW
    mesh = plsc.VectorSubcoreMesh(core_axis_name="c", subcore_axis_name="s")

    @functools.partial(
        pl.kernel, mesh=mesh,
        out_type=jax.ShapeDtypeStruct((n_slots, width), src.dtype),
        scratch_types=[
            pltpu.VMEM((1, SC_WINDOW), jnp.int32),
            pltpu.VMEM((1, SC_WINDOW), jnp.int32),
            pltpu.VMEM((SC_WINDOW, width), src.dtype),
        ],
    )
    def scatter(src_hbm, ia_hbm, ib_hbm, out_hbm, ia_v, ib_v, rows_v):
        wid = lax.axis_index("s") * SC_CORES + lax.axis_index("c")

        @pl.loop(0, n_win)
        def _(w):
            win = wid * n_win + w
            off = pl.multiple_of(win * SC_WINDOW, SC_WINDOW)
            pltpu.sync_copy(src_hbm.at[pl.ds(off, SC_WINDOW)], rows_v)
            pltpu.sync_copy(ia_hbm.at[pl.ds(win, 1)], ia_v)
            pltpu.sync_copy(ib_hbm.at[pl.ds(win, 1)], ib_v)
            pltpu.sync_copy(rows_v, out_hbm.at[ia_v.at[0]])
            pltpu.sync_copy(rows_v, out_hbm.at[ib_v.at[0]])

    return scatter(src, idx_a, idx_b)


def kernel(x_prompt, x_sample, cache_ckv, cache_kpe, state_wkv, state_shift, w_in, q_norm_g, kv_norm_g, w_uq,
           w_ukv, mu_shift, w0, w2, a0, a2, g2, k_k, k_a, r_k, lnx_g, lnx_b, w_o, ln1_g, ln1_b, w_gr, b_gr,
           w_er, b_er, w_eg, w_eu, w_ed, ln2_g, ln2_b):
    B, S, D = x_prompt.shape
    DB, DS, _ = x_sample.shape
    past = cache_ckv.shape[2]
    Tp, Ts = B * S, DB * DS
    T = Tp + Ts
    assert D == D_MODEL and DS == SHIFT_GROUP and S % ATTN_TQ == 0 and S % (CHUNK * WKV_SUB) == 0
    assert Tp % TOKEN_TILE == 0 and T % TOKEN_TILE == 0 and w_in.shape[0] == DEPTH

    l = 0
    xp, xs_in = x_prompt.reshape(Tp, D), x_sample.reshape(Ts, D)
    w1, wqa, wqb, wk, pk, wv, wuk, wuv = _prep_weights(w_in[l], w_uq[l], w_ukv[l])
    pos = jnp.concatenate([jnp.arange(S, dtype=jnp.int32),
                           jnp.tile(past + jnp.arange(DS, dtype=jnp.int32), TOKEN_TILE // DS)])
    rope = _rope_table(pos)

    q, kcat, vcat, ckv_p, ckv_s, kpe_p, kpe_s, pr, last_rows = _proj_call(
        xp, xs_in, rope, S // TOKEN_TILE, w1, q_norm_g[l][None], kv_norm_g[l][None], wqa, wqb, wk, pk, wv)

    attn_p = _attn_call(q, kcat, vcat, B, S)
    attn_s = _mla_sample_call(q, ckv_s, kpe_s, cache_ckv[l], cache_kpe[l], wuk, wuv, Tp, DB, DS)

    bnd = jnp.concatenate([jnp.zeros((1, RWKV_IN), F32), last_rows[:-1]], axis=0)
    gidx = jnp.arange(T // SHIFT_GROUP)
    seq_start = (gidx < Tp // SHIFT_GROUP) & (gidx % (S // SHIFT_GROUP) == 0)
    bnd = jnp.where(seq_start[:, None], 0.0, bnd)
    bnd = jnp.concatenate([bnd[:Tp // SHIFT_GROUP], state_shift[l]], axis=0)[:, None, :]

    seg = _seg_ones()
    vec = lambda a: a.reshape(1, -1)
    r, lw, kh, v, na, b, bonus, g = _prep_call(
        pr, bnd, vec(mu_shift[l]), vec(w0[l]), vec(a0[l]), vec(k_k[l]), vec(k_a[l]), vec(r_k[l]),
        w2[l].astype(BF16), a2[l].astype(BF16), g2[l].astype(BF16), seg)

    scan_in = (r, lw, kh, v, na, b)
    h0_p = jnp.zeros((B, RWKV_HEADS, RWKV_N, RWKV_N), F32)
    y_p, hT_p = _wkv_call(scan_in, h0_p, 0, B, S // CHUNK, CHUNK, WKV_SUB)
    h0_s = jnp.swapaxes(state_wkv[l], -1, -2)
    y_s, hT_s = _wkv_call(scan_in, h0_s, Tp, DB, 1, DS, 1)

    wo_b = w_o[l].astype(BF16)
    wr = jnp.concatenate([w_gr[l], w_er[l], jnp.zeros((D, LANE - N_GROUPS - N_EXPERTS), F32)], axis=1)
    wr_hi = wr.astype(BF16)
    wr_lo = (wr - wr_hi.astype(F32)).astype(BF16)
    br = jnp.concatenate([b_gr[l], b_er[l], jnp.zeros((LANE - N_GROUPS - N_EXPERTS,), F32)])[None]
    h, hpk, route, tile_counts = _mix_call(
        xp, xs_in, attn_p, attn_s, y_p, y_s, bonus, g, seg, vec(lnx_g[l]), vec(lnx_b[l]),
        wo_b[:MLA_HEADS * MLA_V], wo_b[MLA_HEADS * MLA_V:], vec(ln1_g[l]), vec(ln1_b[l]),
        jnp.stack([wr_hi, wr_lo]), br)

    dest, block_e, n_used = _dispatch(route, tile_counts, T)
    win = lambda a: a.reshape(T // SC_WINDOW, SC_WINDOW)
    xs = _sc_scatter_rows(hpk, win(dest[:, 0]), win(dest[:, 1]), block_e.shape[0] * MOE_BLK)
    ys = _expert_call(block_e, n_used, xs, w_eg[l], w_eu[l], w_ed[l])
    yab = _sc_gather_rows(ys, jnp.concatenate([dest[:, 0], dest[:, 1]]))
    out_p, out_s = _combine_call(h, yab, route, vec(ln2_g[l]), vec(ln2_b[l]), Tp)

    y_prompt = out_p.reshape(B, S, D)
    y_sample = out_s.reshape(DB, DS, D)
    p_ckv = ckv_p.reshape(1, B, S, KV_LORA)
    p_kpe = kpe_p.reshape(1, B, S, MLA_ROPE)
    s_ckv = ckv_s.reshape(1, DB, DS, KV_LORA)
    s_kpe = kpe_s.reshape(1, DB, DS, MLA_ROPE)
    p_wkv = jnp.swapaxes(hT_p, -1, -2)[None]
    s_wkv = jnp.swapaxes(hT_s, -1, -2)[None]
    gp = S // SHIFT_GROUP
    p_sh = last_rows[gp - 1:B * gp:gp][None]
    s_sh = last_rows[B * gp:][None]
    return (y_prompt, y_sample, p_ckv, p_kpe, p_wkv, p_sh, s_ckv, s_kpe, s_wkv, s_sh)
```

```python
import functools
import math

import numpy as np
import jax
import jax.numpy as jnp
from jax import lax
from jax.experimental import pallas as pl
from jax.experimental.pallas import tpu as pltpu
from jax.experimental.pallas import tpu_sc as plsc

F32 = jnp.float32
BF16 = jnp.bfloat16

D_MODEL = 1024
CHUNK = 64
MLA_HEADS = 8
MLA_NOPE = 64
MLA_ROPE = 32
MLA_V = 64
Q_LORA = 384
KV_LORA = 256
ROPE_BASE = 10000.0
MLA_IN = Q_LORA + KV_LORA + MLA_ROPE
MLA_SCALE = (MLA_NOPE + MLA_ROPE) ** -0.5
RWKV_HEADS = 8
RWKV_N = 64
RWKV_DIM = RWKV_HEADS * RWKV_N
DECAY_LORA = 64
AAA_LORA = 64
GATE_LORA = 128
RWKV_IN = 3 * RWKV_DIM + DECAY_LORA + AAA_LORA + GATE_LORA
N_GROUPS = 4
EXPERTS_PER_GROUP = 8
N_EXPERTS = N_GROUPS * EXPERTS_PER_GROUP
TOP_K = 2
D_EXPERT = 256
MOE_BLK = 512
LN_EPS = 1e-5
RMS_EPS = 1e-6
GN_EPS = 64e-5
NEG_INF = -1e30
DEPTH = 1
DN_ALPHA = (2 * DEPTH) ** 0.25

LANE = 128
HEAD_PAD = 128
PROJ_W = 768 + RWKV_IN
SHIFT_GROUP = 32
TOKEN_TILE = 512
ATTN_TQ = 1024
ATTN_TK = 512
V_ONE_LANE = (MLA_V, 0)
LOG2E = math.log2(math.e)
VMEM_LIMIT = 48 * 1024 * 1024
SC_CORES = 2
SC_SUBCORES = 16
SC_WINDOW = 32
WKV_GROUP = 4
WKV_SUB = 4
MIX_PARTS = 2


def _cparams(sem):
    return pltpu.CompilerParams(dimension_semantics=sem, vmem_limit_bytes=VMEM_LIMIT)


def _split3(x):
    hi = x.astype(BF16)
    r1 = x - hi.astype(F32)
    mid = r1.astype(BF16)
    lo = (r1 - mid.astype(F32)).astype(BF16)
    return hi, mid, lo


def _dot(a, b):
    return jnp.dot(a, b, preferred_element_type=F32)


def _dot_nt(a, b):
    return lax.dot_general(a, b, (((1,), (1,)), ((), ())), preferred_element_type=F32)


def _dot_exact_rhs(x, w):
    hi = x.astype(BF16)
    lo = (x - hi.astype(F32)).astype(BF16)
    return _dot(hi, w) + _dot(lo, w)


def _dot_exact_lhs(w, x):
    hi, mid, lo = _split3(x)
    return _dot(w, hi) + _dot(w, mid) + _dot(w, lo)


def _pack_bf16_pairs(x):
    n = x.shape[1] // 2
    bits = pltpu.bitcast(x.astype(BF16).astype(F32), jnp.int32)
    return (bits[:, :n] & jnp.int32(-65536)) | lax.shift_right_logical(bits[:, n:], jnp.int32(16))


def _unpack_bf16_pairs(p):
    hi = pltpu.bitcast(p & jnp.int32(-65536), F32)
    lo = pltpu.bitcast(lax.shift_left(p, jnp.int32(16)), F32)
    return jnp.concatenate([hi, lo], axis=1)


def _sigmoid(x):
    return 1.0 / (1.0 + jnp.exp(-x))


def _layer_norm(x, g, b):
    xc = x - jnp.mean(x, -1, keepdims=True)
    var = jnp.mean(xc * xc, -1, keepdims=True)
    return xc * lax.rsqrt(var + LN_EPS) * g + b


def _proj_kernel(xp_ref, xs_ref, rope_ref, w1_ref, gq_ref, gkv_ref, wqa_ref, wqb_ref, wk_ref, pk_ref, wv_ref,
                 q_ref, k_ref, v_ref, ckvp_ref, ckvs_ref, kpep_ref, kpes_ref, pr_ref, last_ref, *, n_p, tm):
    i = pl.program_id(0)
    is_p = i < n_p
    x = jnp.where(is_p, xp_ref[...], xs_ref[...]).astype(BF16)
    proj = _dot(x, w1_ref[...])
    c_q = proj[:, :Q_LORA]
    c_kv = proj[:, Q_LORA:Q_LORA + KV_LORA]
    kp = proj[:, 640:768]
    pr = proj[:, 768:]
    pr_ref[...] = pr
    last_ref[...] = pr.reshape(tm // SHIFT_GROUP, SHIFT_GROUP, RWKV_IN)[:, SHIFT_GROUP - 1, :]

    cqn = c_q * lax.rsqrt(jnp.mean(c_q * c_q, -1, keepdims=True) + RMS_EPS) * gq_ref[...]
    ckv = c_kv * lax.rsqrt(jnp.mean(c_kv * c_kv, -1, keepdims=True) + RMS_EPS) * gkv_ref[...]

    rope = rope_ref[...]
    cq = rope[:, :LANE]
    sq = rope[:, LANE:2 * LANE]
    kt = rope[:, 2 * LANE:]
    prod = kp * kt
    kpe = prod[:, :MLA_ROPE] + prod[:, MLA_ROPE:2 * MLA_ROPE]

    @pl.when(is_p)
    def _():
        ckvp_ref[...] = ckv
        kpep_ref[...] = kpe

    @pl.when(jnp.logical_not(is_p))
    def _():
        ckvs_ref[...] = ckv
        kpes_ref[...] = kpe

    cqb = cqn.astype(BF16)
    qa = _dot(cqb, wqa_ref[...])
    qb = _dot(cqb, wqb_ref[...])
    for h in range(MLA_HEADS):
        sl = slice(h * HEAD_PAD, (h + 1) * HEAD_PAD)
        q_ref[:, sl] = (qa[:, sl] * cq + qb[:, sl] * sq).astype(BF16)

    ckv_b = ckv.astype(BF16)
    k = _dot(ckv_b, wk_ref[...]) + _dot(kpe.astype(BF16), pk_ref[...])
    k_ref[...] = k.astype(BF16)
    lane = lax.broadcasted_iota(jnp.int32, (1, MLA_HEADS * HEAD_PAD), 1)
    odd = (lane // HEAD_PAD) % 2
    one_lane = jnp.where(odd == 1, V_ONE_LANE[1], V_ONE_LANE[0])
    v_one = jnp.where(lane % HEAD_PAD == one_lane, 1.0, 0.0)
    v_ref[...] = (_dot(ckv_b, wv_ref[...]) + v_one).astype(BF16)


def _split_rows(n_p):
    return (lambda i: (jnp.minimum(i, n_p - 1), 0)), (lambda i: (jnp.maximum(i - n_p, 0), 0))


def _proj_call(xp, xs, rope, rope_tiles, w1, gq, gkv, wqa, wqb, wk, pk, wv):
    Tp, Ts = xp.shape[0], xs.shape[0]
    T = Tp + Ts
    tm = TOKEN_TILE
    n_p = Tp // tm
    row = lambda i: (i, 0)
    full = lambda i: (0, 0)
    row_p, row_s = _split_rows(n_p)
    wide = MLA_HEADS * HEAD_PAD
    ng = tm // SHIFT_GROUP
    return pl.pallas_call(
        functools.partial(_proj_kernel, n_p=n_p, tm=tm),
        grid=(T // tm,),
        in_specs=[
            pl.BlockSpec((tm, D_MODEL), row_p),
            pl.BlockSpec((tm, D_MODEL), row_s),
            pl.BlockSpec((tm, 3 * LANE), lambda i: (jnp.where(i < n_p, i % rope_tiles, rope_tiles), 0)),
            pl.BlockSpec((D_MODEL, PROJ_W), full),
            pl.BlockSpec((1, Q_LORA), full),
            pl.BlockSpec((1, KV_LORA), full),
            pl.BlockSpec((Q_LORA, wide), full),
            pl.BlockSpec((Q_LORA, wide), full),
            pl.BlockSpec((KV_LORA, wide), full),
            pl.BlockSpec((MLA_ROPE, wide), full),
            pl.BlockSpec((KV_LORA, wide), full),
        ],
        out_specs=[
            pl.BlockSpec((tm, wide), row),
            pl.BlockSpec((tm, wide), row),
            pl.BlockSpec((tm, wide), row),
            pl.BlockSpec((tm, KV_LORA), row_p),
            pl.BlockSpec((tm, KV_LORA), row_s),
            pl.BlockSpec((tm, MLA_ROPE), row_p),
            pl.BlockSpec((tm, MLA_ROPE), row_s),
            pl.BlockSpec((tm, RWKV_IN), row),
            pl.BlockSpec((ng, RWKV_IN), row),
        ],
        out_shape=[
            jax.ShapeDtypeStruct((T, wide), BF16),
            jax.ShapeDtypeStruct((T, wide), BF16),
            jax.ShapeDtypeStruct((T, wide), BF16),
            jax.ShapeDtypeStruct((Tp, KV_LORA), F32),
            jax.ShapeDtypeStruct((Ts, KV_LORA), F32),
            jax.ShapeDtypeStruct((Tp, MLA_ROPE), F32),
            jax.ShapeDtypeStruct((Ts, MLA_ROPE), F32),
            jax.ShapeDtypeStruct((T, RWKV_IN), F32),
            jax.ShapeDtypeStruct((T // SHIFT_GROUP, RWKV_IN), F32),
        ],
        compiler_params=_cparams(("arbitrary",)),
        name="proj",
    )(xp, xs, rope, w1, gq, gkv, wqa, wqb, wk, pk, wv)


def _attn_kernel(q_ref, k_ref, v_ref, o_ref, m_scr, acc_scr, *, tq, tk):
    qi = pl.program_id(2)
    m_scr[...] = jnp.full(m_scr.shape, NEG_INF, F32)
    acc_scr[...] = jnp.zeros(acc_scr.shape, F32)
    n_diag = tq // tk

    def kv_block(k0, width, rows, masked):
        n_rows = rows.stop - rows.start
        if masked:
            r = lax.broadcasted_iota(jnp.int32, (n_rows, width), 0) // CHUNK
            c = lax.broadcasted_iota(jnp.int32, (n_rows, width), 1) // CHUNK
            visible = c <= r
        H = range(2)
        sls = [slice(h * HEAD_PAD, (h + 1) * HEAD_PAD) for h in H]
        s = [_dot_nt(q_ref[rows, sl], k_ref[pl.ds(k0, width), sl]) for sl in sls]
        if masked:
            s = [jnp.where(visible, s[h], NEG_INF) for h in H]
        m_prev = [m_scr[h, rows, :] for h in H]
        m_new = [jnp.maximum(m_prev[h], jnp.max(s[h], axis=1, keepdims=True)) for h in H]
        pexp = [jnp.exp2(s[h] - jnp.tile(m_new[h], (1, width // LANE))).astype(BF16) for h in H]
        pv = [_dot(pexp[h], v_ref[pl.ds(k0, width), sls[h]]) for h in H]
        for h in H:
            acc_scr[h, rows, :] = jnp.exp2(m_prev[h] - m_new[h]) * acc_scr[h, rows, :] + pv[h]
            m_scr[h, rows, :] = m_new[h]

    def body(kj, carry):
        kv_block(pl.multiple_of(kj * tq, tq), tq, slice(0, tq), False)
        return carry

    lax.fori_loop(0, qi, body, 0)
    for d in range(n_diag):
        k0 = pl.multiple_of(qi * tq + d * tk, tk)
        kv_block(k0, tk, slice(d * tk, (d + 1) * tk), True)
        if (d + 1) * tk < tq:
            kv_block(k0, tk, slice((d + 1) * tk, tq), False)
    acc0, acc1 = acc_scr[0], acc_scr[1]
    lane = lax.broadcasted_iota(jnp.int32, acc0.shape, 1)
    l0 = acc0[:, V_ONE_LANE[0]:V_ONE_LANE[0] + 1]
    l1 = acc1[:, V_ONE_LANE[1]:V_ONE_LANE[1] + 1]
    o_ref[...] = jnp.where(lane < MLA_V, acc0 / l0, acc1 / l1).astype(o_ref.dtype)


def _attn_call(q, k, v, n_batch, seq):
    tq, tk = ATTN_TQ, ATTN_TK
    nq = seq // tq
    hp = MLA_HEADS // 2
    resident = pl.BlockSpec((seq, 2 * HEAD_PAD), lambda b, h, i: (b, h), pipeline_mode=pl.Buffered(1))
    return pl.pallas_call(
        functools.partial(_attn_kernel, tq=tq, tk=tk),
        grid=(n_batch, hp, nq),
        in_specs=[
            pl.BlockSpec((tq, 2 * HEAD_PAD), lambda b, h, i: (b * nq + i, h)),
            resident,
            resident,
        ],
        out_specs=pl.BlockSpec((tq, LANE), lambda b, h, i: (b * nq + i, h)),
        out_shape=jax.ShapeDtypeStruct((n_batch * seq, hp * LANE), BF16),
        scratch_shapes=[
            pltpu.VMEM((2, tq, LANE), F32),
            pltpu.VMEM((2, tq, LANE), F32),
        ],
        compiler_params=_cparams(("parallel", "parallel", "arbitrary")),
        name="attn",
    )(q, k, v)


def _mla_sample_kernel(q_ref, cn_ref, kn_ref, cp_ref, kp_ref, wuk_ref, wuv_ref, o_ref, *, past, dec):
    cp = cp_ref[0].astype(BF16)
    kp = kp_ref[0].astype(BF16)
    cn = cn_ref[...].astype(BF16)
    kn = kn_ref[...].astype(BF16)
    R = MLA_HEADS * dec
    qrow = (past + lax.broadcasted_iota(jnp.int32, (R, past), 0) % dec) // CHUNK
    vis_p = (lax.broadcasted_iota(jnp.int32, (R, past), 1) // CHUNK) <= qrow
    qrow_n = (past + lax.broadcasted_iota(jnp.int32, (R, dec), 0) % dec) // CHUNK
    vis_n = ((past + lax.broadcasted_iota(jnp.int32, (R, dec), 1)) // CHUNK) <= qrow_n
    q_lat, qp = [], []
    for h in range(MLA_HEADS):
        qn = q_ref[:, h * HEAD_PAD:h * HEAD_PAD + MLA_NOPE]
        qp.append(q_ref[:, h * HEAD_PAD + MLA_NOPE:h * HEAD_PAD + MLA_NOPE + MLA_ROPE])
        q_lat.append(_dot(qn, wuk_ref[h]).astype(BF16))
    q_lat = jnp.concatenate(q_lat, axis=0)
    qp = jnp.concatenate(qp, axis=0)
    s_p = jnp.where(vis_p, _dot_nt(q_lat, cp) + _dot_nt(qp, kp), NEG_INF)
    s_n = jnp.where(vis_n, _dot_nt(q_lat, cn) + _dot_nt(qp, kn), NEG_INF)
    m = jnp.maximum(jnp.max(s_p, axis=1, keepdims=True), jnp.max(s_n, axis=1, keepdims=True))
    e_p = jnp.exp2(s_p - m)
    e_n = jnp.exp2(s_n - m)
    l = jnp.sum(e_p, axis=1, keepdims=True) + jnp.sum(e_n, axis=1, keepdims=True)
    o_lat = ((_dot(e_p.astype(BF16), cp) + _dot(e_n.astype(BF16), cn)) / l).astype(BF16)
    out = jnp.zeros((dec, MLA_HEADS * MLA_V), F32)
    for h in range(MLA_HEADS):
        out = out + _dot(o_lat[h * dec:(h + 1) * dec], wuv_ref[h])
    o_ref[...] = out.astype(o_ref.dtype)


def _mla_sample_call(q, ckv, kpe, cache_ckv, cache_kpe, wuk, wuv, row0, n_seq, dec):
    past = cache_ckv.shape[1]
    blk0 = row0 // dec
    wide = MLA_HEADS * HEAD_PAD
    return pl.pallas_call(
        functools.partial(_mla_sample_kernel, past=past, dec=dec),
        grid=(n_seq,),
        in_specs=[
            pl.BlockSpec((dec, wide), lambda b: (blk0 + b, 0)),
            pl.BlockSpec((dec, KV_LORA), lambda b: (b, 0)),
            pl.BlockSpec((dec, MLA_ROPE), lambda b: (b, 0)),
            pl.BlockSpec((1, past, KV_LORA), lambda b: (b, 0, 0)),
            pl.BlockSpec((1, past, MLA_ROPE), lambda b: (b, 0, 0)),
            pl.BlockSpec((MLA_HEADS, MLA_NOPE, KV_LORA), lambda b: (0, 0, 0)),
            pl.BlockSpec((MLA_HEADS, KV_LORA, MLA_HEADS * MLA_V), lambda b: (0, 0, 0)),
        ],
        out_specs=pl.BlockSpec((dec, MLA_HEADS * MLA_V), lambda b: (b, 0)),
        out_shape=jax.ShapeDtypeStruct((n_seq * dec, MLA_HEADS * MLA_V), BF16),
        compiler_params=_cparams(("parallel",)),
        name="mla_sample",
    )(q, ckv, kpe, cache_ckv, cache_kpe, wuk, wuv)


def _prep_kernel(pr_ref, bnd_ref, mu_ref, w0_ref, a0_ref, kk_ref, ka_ref, rk_ref, w2_ref, a2_ref, g2_ref,
                 seg_ref, r_ref, lw_ref, kh_ref, v_ref, na_ref, b_ref, bonus_ref, g_ref, *, tm):
    ng = tm // SHIFT_GROUP
    pr = pr_ref[...]
    pr3 = pr.reshape(ng, SHIFT_GROUP, RWKV_IN)
    rolled = pltpu.roll(pr3, 1, 1)
    first = lax.broadcasted_iota(jnp.int32, pr3.shape, 1) == 0
    prev = jnp.where(first, bnd_ref[...], rolled).reshape(tm, RWKV_IN)
    u = pr + mu_ref[...] * (prev - pr)
    o1, o2, o3 = RWKV_DIM, 2 * RWKV_DIM, 3 * RWKV_DIM
    o4, o5 = o3 + DECAY_LORA, o3 + DECAY_LORA + AAA_LORA
    r, k, v = u[:, :o1], u[:, o1:o2], u[:, o2:o3]
    w_lo, a_lo, g_lo = u[:, o3:o4], u[:, o4:o5], u[:, o5:]
    wl = w0_ref[...] + _dot(jnp.tanh(w_lo).astype(BF16), w2_ref[...])
    lw_ref[...] = -math.exp(-0.5) * _sigmoid(wl)
    a = _sigmoid(a0_ref[...] + _dot(a_lo.astype(BF16), a2_ref[...]))
    g_ref[...] = _dot(_sigmoid(g_lo).astype(BF16), g2_ref[...])
    seg = seg_ref[...]
    kk = k * kk_ref[...]
    kk = kk / jnp.maximum(jnp.sqrt(_dot_exact_rhs(kk * kk, seg)), 1e-12)
    kh = k * (1.0 + (a - 1.0) * ka_ref[...])
    r_ref[...] = r
    kh_ref[...] = kh
    v_ref[...] = v
    na_ref[...] = -kk
    b_ref[...] = kk * a
    bonus_ref[...] = _dot_exact_rhs(r * kh * rk_ref[...], seg) * v


def _prep_call(pr, bnd, mu, w0, a0, k_k, k_a, r_k, w2, a2, g2, seg):
    T = pr.shape[0]
    tm = TOKEN_TILE
    row = lambda i: (i, 0)
    full = lambda i: (0, 0)
    vec = pl.BlockSpec((1, RWKV_DIM), full)
    out = pl.BlockSpec((tm, RWKV_DIM), row)
    return pl.pallas_call(
        functools.partial(_prep_kernel, tm=tm),
        grid=(T // tm,),
        in_specs=[
            pl.BlockSpec((tm, RWKV_IN), row),
            pl.BlockSpec((tm // SHIFT_GROUP, 1, RWKV_IN), lambda i: (i, 0, 0)),
            pl.BlockSpec((1, RWKV_IN), full),
            vec, vec, vec, vec, vec,
            pl.BlockSpec((DECAY_LORA, RWKV_DIM), full),
            pl.BlockSpec((AAA_LORA, RWKV_DIM), full),
            pl.BlockSpec((GATE_LORA, RWKV_DIM), full),
            pl.BlockSpec((RWKV_DIM, RWKV_DIM), full),
        ],
        out_specs=[out] * 8,
        out_shape=[jax.ShapeDtypeStruct((T, RWKV_DIM), F32)] * 8,
        compiler_params=_cparams(("parallel",)),
        name="rwkv_prep",
    )(pr, bnd, mu, w0, a0, k_k, k_a, r_k, w2, a2, g2, seg)


def _wkv_kernel(r_ref, lw_ref, k_ref, v_ref, a_ref, b_ref, h0_ref, y_ref, hT_ref, h_scr, *, C, n_sub):
    GW = WKV_GROUP * RWKV_N
    R = WKV_GROUP * C
    n_grp = RWKV_HEADS // WKV_GROUP
    n_lev = int(round(math.log2(C))) - 1
    c = pl.program_id(1)

    def head_block(hh):
        return slice(hh * RWKV_N, (hh + 1) * RWKV_N)

    @pl.when(c == 0)
    def _():
        h_scr[...] = jnp.zeros(h_scr.shape, F32)
        for hd in range(RWKV_HEADS):
            g, hh = divmod(hd, WKV_GROUP)
            h_scr[g, head_block(hh), head_block(hh)] = h0_ref[0, hd]

    row = lax.broadcasted_iota(jnp.int32, (C, C), 0)
    col = lax.broadcasted_iota(jnp.int32, (C, C), 1)
    tri = jnp.where(col <= row, 1.0, 0.0).astype(BF16)
    rr = lax.broadcasted_iota(jnp.int32, (R, R), 0)
    cc = lax.broadcasted_iota(jnp.int32, (R, R), 1)
    same = (rr // C) == (cc // C)
    ti = lax.broadcasted_iota(jnp.int32, (C, R), 0)
    si = lax.broadcasted_iota(jnp.int32, (C, R), 1) % C
    strict4 = si < ti
    lower4 = si <= ti
    eye4 = jnp.where(si == ti, 1.0, 0.0)
    keep = (lax.broadcasted_iota(jnp.int32, (R, GW), 0) // C
            == lax.broadcasted_iota(jnp.int32, (R, GW), 1) // RWKV_N)
    gr = lax.broadcasted_iota(jnp.int32, (GW, GW), 0)
    gc = lax.broadcasted_iota(jnp.int32, (GW, GW), 1)
    eye_g = gr == gc
    same_head = (gr // RWKV_N) == (gc // RWKV_N)
    eye_g_bf = jnp.where(eye_g, 1.0, 0.0).astype(BF16)

    def rows4(x4):
        return jnp.concatenate([x4] * WKV_GROUP, axis=0)

    def stack(x4):
        return jnp.where(keep, rows4(x4), jnp.zeros((), x4.dtype))


    sls = [slice(g * GW, (g + 1) * GW) for g in range(n_grp)]
    J = [(ci, g) for ci in range(n_sub) for g in range(n_grp)]
    ops, p_end = {}, {}
    for ci in range(n_sub):
        rows = slice(ci * C, (ci + 1) * C)
        lw = lw_ref[rows, :]
        cum = _dot_exact_lhs(tri, lw)
        cum_end = cum[C - 1:C, :]
        e_neg = jnp.exp(-cum)
        e_end = jnp.exp(cum_end - cum)
        b_in = b_ref[rows, :]
        k_in = k_ref[rows, :]
        full = ((a_ref[rows, :] * jnp.exp(cum - lw)).astype(BF16),
                (r_ref[rows, :] * jnp.exp(cum)).astype(BF16),
                (b_in * e_neg).astype(BF16), (k_in * e_neg).astype(BF16),
                (b_in * e_end).astype(BF16), (k_in * e_end).astype(BF16),
                v_ref[rows, :].astype(BF16))
        p_end[ci] = jnp.exp(cum_end)
        for g in range(n_grp):
            ops[ci, g] = [t[:, sls[g]] for t in full]
    a4, r4, b4, k4, be4, ke4, v4 = [{j: ops[j][i] for j in J} for i in range(7)]
    v_s = {j: stack(v4[j]) for j in J}
    m = {j: _dot_nt(jnp.concatenate([a4[j], r4[j]], axis=0),
                    jnp.concatenate([stack(b4[j]), stack(k4[j])], axis=0)) for j in J}
    l4 = {j: jnp.where(strict4, m[j][:C, :R], 0.0) for j in J}
    a_ak = {j: jnp.where(strict4, m[j][:C, R:], 0.0).astype(BF16) for j in J}
    a_rb = {j: jnp.where(lower4, m[j][C:, :R], 0.0).astype(BF16) for j in J}
    a_rk = {j: jnp.where(lower4, m[j][C:, R:], 0.0).astype(BF16) for j in J}
    def block_diag(x4):
        return jnp.where(same, rows4(x4.astype(BF16)), jnp.zeros((), BF16))

    t4 = {j: eye4 + l4[j] for j in J}
    l_bd = {j: block_diag(l4[j]) for j in J}
    for _ in range(n_lev):
        l4 = {j: _dot(l4[j].astype(BF16), l_bd[j]) for j in J}
        l_bd = {j: block_diag(l4[j]) for j in J}
        t4 = {j: t4[j] + _dot(t4[j].astype(BF16), l_bd[j]) for j in J}
    t_b = {j: t4[j].astype(BF16) for j in J}
    bke_t = {j: _dot_nt(eye_g_bf, jnp.concatenate([be4[j], ke4[j]], axis=0)).astype(BF16) for j in J}

    G = range(n_grp)
    h_cur = [h_scr[g] for g in G]
    for ci in range(n_sub):
        rows = slice(ci * C, (ci + 1) * C)
        h0_b = [h_cur[g].astype(BF16) for g in G]
        x4 = [_dot(a4[ci, g], h0_b[g]) + _dot(a_ak[ci, g], v_s[ci, g]) for g in G]
        u4 = [_dot(t_b[ci, g], stack(x4[g].astype(BF16))).astype(BF16) for g in G]
        y4 = [_dot(r4[ci, g], h0_b[g]) + _dot(a_rb[ci, g], stack(u4[g])) + _dot(a_rk[ci, g], v_s[ci, g])
              for g in G]
        uv4 = [jnp.concatenate([u4[g], v4[ci, g]], axis=0) for g in G]
        h_add = [jnp.where(same_head, _dot(bke_t[ci, g], uv4[g]), 0.0) for g in G]
        for g in G:
            y_ref[rows, sls[g]] = y4[g]
            p_col = jnp.sum(jnp.where(eye_g, p_end[ci][:, sls[g]], 0.0), axis=1, keepdims=True)
            h_cur[g] = p_col * h_cur[g] + h_add[g]

    for g in G:
        h_scr[g] = h_cur[g]

    @pl.when(c == pl.num_programs(1) - 1)
    def _():
        for hd in range(RWKV_HEADS):
            g, hh = divmod(hd, WKV_GROUP)
            hT_ref[0, hd] = h_scr[g, head_block(hh), head_block(hh)]


def _wkv_call(arrs, h0, row0, n_seq, n_chunk, C, n_sub):
    rows = C * n_sub
    steps = n_chunk // n_sub
    blk0 = row0 // rows
    GW = WKV_GROUP * RWKV_N
    n_grp = RWKV_HEADS // WKV_GROUP
    tok = pl.BlockSpec((rows, RWKV_DIM), lambda b, c: (blk0 + b * steps + c, 0))
    out_tok = pl.BlockSpec((rows, RWKV_DIM), lambda b, c: (b * steps + c, 0))
    st = pl.BlockSpec((1, RWKV_HEADS, RWKV_N, RWKV_N), lambda b, c: (b, 0, 0, 0))
    return pl.pallas_call(
        functools.partial(_wkv_kernel, C=C, n_sub=n_sub),
        grid=(n_seq, steps),
        in_specs=[tok] * 6 + [st],
        out_specs=[out_tok, st],
        out_shape=[
            jax.ShapeDtypeStruct((n_seq * n_chunk * C, RWKV_DIM), F32),
            jax.ShapeDtypeStruct((n_seq, RWKV_HEADS, RWKV_N, RWKV_N), F32),
        ],
        scratch_shapes=[pltpu.VMEM((n_grp, GW, GW), F32)],
        compiler_params=_cparams(("parallel", "arbitrary")),
        name="wkv_c%d" % C,
    )(*arrs, h0)


def _mix_kernel(xp_ref, xs_ref, attnp_ref, attns_ref, yp_ref, ys_ref, bonus_ref, g_ref, seg_ref, lng_ref,
                lnb_ref, woa_ref, wob_ref, g1_ref, b1_ref, wr_ref, br_ref, tri_ref, h_ref, hpk_ref, route_ref,
                count_ref, *, n_p, tm):
    is_p = pl.program_id(0) < n_p
    P = range(MIX_PARTS)
    bands = [slice(k * tm // MIX_PARTS, (k + 1) * tm // MIX_PARTS) for k in P]
    seg = seg_ref[...]
    inv_n = 1.0 / RWKV_N
    y = [jnp.where(is_p, yp_ref[r, :], ys_ref[r, :]) for r in bands]
    yc = [y[k] - _dot_exact_rhs(y[k], seg) * inv_n for k in P]
    var = [_dot_exact_rhs(yc[k] * yc[k], seg) * inv_n for k in P]
    yn = [yc[k] * lax.rsqrt(var[k] + GN_EPS) * lng_ref[...] + lnb_ref[...] for k in P]
    rw = [((yn[k] + bonus_ref[bands[k], :]) * g_ref[bands[k], :]).astype(BF16) for k in P]
    attn = [jnp.where(is_p, attnp_ref[r, :], attns_ref[r, :]) for r in bands]
    m = [_dot(attn[k], woa_ref[...]) + _dot(rw[k], wob_ref[...]) for k in P]
    x = [jnp.where(is_p, xp_ref[r, :], xs_ref[r, :]) for r in bands]
    h = [_layer_norm(DN_ALPHA * x[k] + m[k], g1_ref[...], b1_ref[...]) for k in P]
    for k in P:
        h_ref[bands[k], :] = h[k]
        hpk_ref[bands[k], :] = _pack_bf16_pairs(h[k])

    h_hi = [h[k].astype(BF16) for k in P]
    h_lo = [(h[k] - h_hi[k].astype(F32)).astype(BF16) for k in P]
    logits = [_dot(h_hi[k], wr_ref[0]) + _dot(h_lo[k], wr_ref[0]) + _dot(h_hi[k], wr_ref[1]) + br_ref[...]
              for k in P]
    lane = lax.broadcasted_iota(jnp.int32, logits[0].shape, 1)
    big = jnp.int32(LANE)

    def route_band(lg):
        gl = jnp.where(lane < N_GROUPS, lg, NEG_INF)
        gmax = jnp.max(gl, axis=1, keepdims=True)
        grp = jnp.min(jnp.where(gl == gmax, lane, big), axis=1, keepdims=True)
        p_grp = 1.0 / jnp.sum(jnp.exp(gl - gmax), axis=1, keepdims=True)
        e_idx = lane - N_GROUPS
        in_grp = (lane >= N_GROUPS) & (lane < N_GROUPS + N_EXPERTS) & ((e_idx // EXPERTS_PER_GROUP) == grp)
        el = jnp.where(in_grp, lg, NEG_INF)
        m1 = jnp.max(el, axis=1, keepdims=True)
        i1 = jnp.min(jnp.where(el == m1, lane, big), axis=1, keepdims=True)
        el2 = jnp.where(lane == i1, NEG_INF, el)
        m2 = jnp.max(el2, axis=1, keepdims=True)
        i2 = jnp.min(jnp.where(el2 == m2, lane, big), axis=1, keepdims=True)
        t = jnp.exp(m2 - m1)
        g1 = p_grp / (1.0 + t)
        return i1, i2, g1, g1 * t

    routed = [route_band(logits[k]) for k in P]
    chosen = jnp.concatenate(
        [jnp.where(lane == routed[k][0], 1.0, jnp.where(lane == routed[k][1], 1.0, 0.0)) for k in P], axis=0)
    before = _dot(tri_ref[...], chosen.astype(BF16))
    count_ref[0] = jnp.sum(chosen, axis=0, keepdims=True)
    for k in P:
        i1, i2, g1, g2 = routed[k]
        bef = before[bands[k]]
        r1 = jnp.sum(jnp.where(lane == i1, bef, 0.0), axis=1, keepdims=True)
        r2 = jnp.sum(jnp.where(lane == i2, bef, 0.0), axis=1, keepdims=True)
        cols = ((i1 - N_GROUPS).astype(F32), (i2 - N_GROUPS).astype(F32), g1, g2, r1, r2)
        route = jnp.zeros(lane.shape, F32)
        for j, col in enumerate(cols):
            route = jnp.where(lane == j, col, route)
        route_ref[bands[k], :] = route


def _mix_call(xp, xs, attn_p, attn_s, y_p, y_s, bonus, g, seg, lnx_g, lnx_b, woa, wob, ln1_g, ln1_b, wr, br):
    T = xp.shape[0] + xs.shape[0]
    tm = TOKEN_TILE
    n_p = xp.shape[0] // tm
    row = lambda i: (i, 0)
    full = lambda i: (0, 0)
    row_p, row_s = _split_rows(n_p)
    half = pl.BlockSpec((tm, RWKV_DIM), row)
    vec5 = pl.BlockSpec((1, RWKV_DIM), full)
    vec10 = pl.BlockSpec((1, D_MODEL), full)
    idx = np.arange(tm)
    tri = jnp.asarray((idx[None, :] < idx[:, None]).astype(np.float32)).astype(BF16)
    return pl.pallas_call(
        functools.partial(_mix_kernel, n_p=n_p, tm=tm),
        grid=(T // tm,),
        in_specs=[
            pl.BlockSpec((tm, D_MODEL), row_p), pl.BlockSpec((tm, D_MODEL), row_s),
            pl.BlockSpec((tm, RWKV_DIM), row_p), pl.BlockSpec((tm, RWKV_DIM), row_s),
            pl.BlockSpec((tm, RWKV_DIM), row_p), pl.BlockSpec((tm, RWKV_DIM), row_s),
            half, half,
            pl.BlockSpec((RWKV_DIM, RWKV_DIM), full), vec5, vec5,
            pl.BlockSpec((RWKV_DIM, D_MODEL), full), pl.BlockSpec((RWKV_DIM, D_MODEL), full),
            vec10, vec10,
            pl.BlockSpec((2, D_MODEL, LANE), lambda i: (0, 0, 0)), pl.BlockSpec((1, LANE), full),
            pl.BlockSpec((tm, tm), full),
        ],
        out_specs=[pl.BlockSpec((tm, D_MODEL), row), pl.BlockSpec((tm, D_MODEL // 2), row),
                   pl.BlockSpec((tm, LANE), row), pl.BlockSpec((1, 1, LANE), lambda i: (i, 0, 0))],
        out_shape=[jax.ShapeDtypeStruct((T, D_MODEL), F32), jax.ShapeDtypeStruct((T, D_MODEL // 2), jnp.int32),
                   jax.ShapeDtypeStruct((T, LANE), F32), jax.ShapeDtypeStruct((T // tm, 1, LANE), F32)],
        compiler_params=_cparams(("parallel",)),
        name="mix",
    )(xp, xs, attn_p, attn_s, y_p, y_s, bonus, g, seg, lnx_g, lnx_b, woa, wob, ln1_g, ln1_b, wr, br, tri)


def _expert_kernel(be_ref, nu_ref, xs_ref, wg_ref, wu_ref, wd_ref, ys_ref, wgu_b, wd_b):
    i = pl.program_id(0)

    @pl.when((i == 0) | (be_ref[i] != be_ref[jnp.maximum(i - 1, 0)]))
    def _():
        wgu_b[:, :D_EXPERT] = wg_ref[0].astype(BF16)
        wgu_b[:, D_EXPERT:] = wu_ref[0].astype(BF16)
        wd_b[...] = wd_ref[0].astype(BF16)

    @pl.when(i < nu_ref[0])
    def _():
        P = range(2)
        bands = [slice(k * MOE_BLK // 2, (k + 1) * MOE_BLK // 2) for k in P]
        xb = [_unpack_bf16_pairs(xs_ref[r, :]).astype(BF16) for r in bands]
        gu = [_dot(xb[k], wgu_b[...]) for k in P]
        act = [(gu[k][:, :D_EXPERT] * _sigmoid(gu[k][:, :D_EXPERT]) * gu[k][:, D_EXPERT:]).astype(BF16)
               for k in P]
        out = [_dot(act[k], wd_b[...]) for k in P]
        for k in P:
            ys_ref[bands[k], :] = _pack_bf16_pairs(out[k])

    @pl.when(i >= nu_ref[0])
    def _():
        ys_ref[...] = jnp.zeros(ys_ref.shape, ys_ref.dtype)


def _expert_call(block_e, n_used, xs, wg, wu, wd):
    n_blk = xs.shape[0] // MOE_BLK
    grid_spec = pltpu.PrefetchScalarGridSpec(
        num_scalar_prefetch=2,
        grid=(n_blk,),
        in_specs=[
            pl.BlockSpec((MOE_BLK, D_MODEL // 2), lambda i, be, nu: (i, 0)),
            pl.BlockSpec((1, D_MODEL, D_EXPERT), lambda i, be, nu: (be[i], 0, 0)),
            pl.BlockSpec((1, D_MODEL, D_EXPERT), lambda i, be, nu: (be[i], 0, 0)),
            pl.BlockSpec((1, D_EXPERT, D_MODEL), lambda i, be, nu: (be[i], 0, 0)),
        ],
        out_specs=pl.BlockSpec((MOE_BLK, D_MODEL // 2), lambda i, be, nu: (i, 0)),
        scratch_shapes=[pltpu.VMEM((D_MODEL, 2 * D_EXPERT), BF16), pltpu.VMEM((D_EXPERT, D_MODEL), BF16)],
    )
    return pl.pallas_call(
        _expert_kernel,
        grid_spec=grid_spec,
        out_shape=jax.ShapeDtypeStruct((n_blk * MOE_BLK, D_MODEL // 2), jnp.int32),
        compiler_params=_cparams(("arbitrary",)),
        name="experts",
    )(block_e, n_used, xs, wg, wu, wd)


def _combine_kernel(h_ref, ya_ref, yb_ref, route_ref, g2_ref, b2_ref, op_ref, os_ref, *, n_p):
    i = pl.program_id(0)
    route = route_ref[...]
    f = _unpack_bf16_pairs(ya_ref[...]) * route[:, 2:3] + _unpack_bf16_pairs(yb_ref[...]) * route[:, 3:4]
    out = _layer_norm(DN_ALPHA * h_ref[...] + f, g2_ref[...], b2_ref[...])

    @pl.when(i < n_p)
    def _():
        op_ref[...] = out

    @pl.when(i >= n_p)
    def _():
        os_ref[...] = out


def _combine_call(h, yab, route, ln2_g, ln2_b, t_prompt):
    T = h.shape[0]
    tm = TOKEN_TILE
    n_t, n_p = T // tm, t_prompt // tm
    row = lambda i: (i, 0)
    full = lambda i: (0, 0)
    row_p, row_s = _split_rows(n_p)
    big = pl.BlockSpec((tm, D_MODEL), row)
    return pl.pallas_call(
        functools.partial(_combine_kernel, n_p=n_p),
        grid=(n_t,),
        in_specs=[big, pl.BlockSpec((tm, D_MODEL // 2), row),
                  pl.BlockSpec((tm, D_MODEL // 2), lambda i: (i + n_t, 0)), pl.BlockSpec((tm, LANE), row),
                  pl.BlockSpec((1, D_MODEL), full), pl.BlockSpec((1, D_MODEL), full)],
        out_specs=[pl.BlockSpec((tm, D_MODEL), row_p), pl.BlockSpec((tm, D_MODEL), row_s)],
        out_shape=[jax.ShapeDtypeStruct((t_prompt, D_MODEL), F32),
                   jax.ShapeDtypeStruct((T - t_prompt, D_MODEL), F32)],
        compiler_params=_cparams(("arbitrary",)),
        name="combine",
    )(h, yab, yab, route, ln2_g, ln2_b)


def _prep_weights(w_in, w_uq, w_ukv):
    half = MLA_ROPE // 2
    kpe_w = w_in[:, Q_LORA + KV_LORA:MLA_IN]
    kpe_b = jnp.concatenate([-kpe_w[:, half:], kpe_w[:, :half]], axis=1)
    w1 = jnp.concatenate([w_in[:, :Q_LORA + KV_LORA], kpe_w, kpe_b,
                          jnp.zeros((D_MODEL, 64), F32), w_in[:, MLA_IN:]], axis=1).astype(BF16)
    pad_q = jnp.zeros((Q_LORA, MLA_HEADS, HEAD_PAD - MLA_NOPE - MLA_ROPE), F32)
    wqa = jnp.concatenate([w_uq, pad_q], axis=2).reshape(Q_LORA, -1).astype(BF16)
    rot = jnp.concatenate([jnp.zeros((Q_LORA, MLA_HEADS, MLA_NOPE), F32),
                           -w_uq[:, :, MLA_NOPE + half:], w_uq[:, :, MLA_NOPE:MLA_NOPE + half], pad_q], axis=2)
    wqb = rot.reshape(Q_LORA, -1).astype(BF16)
    w_uk, w_uv = w_ukv[:, :, :MLA_NOPE], w_ukv[:, :, MLA_NOPE:]
    wk = jnp.concatenate([w_uk, jnp.zeros((KV_LORA, MLA_HEADS, HEAD_PAD - MLA_NOPE), F32)], axis=2)
    wk = wk.reshape(KV_LORA, -1).astype(BF16)
    pk_np = np.zeros((MLA_ROPE, MLA_HEADS * HEAD_PAD), np.float32)
    for h in range(MLA_HEADS):
        for i in range(MLA_ROPE):
            pk_np[i, h * HEAD_PAD + MLA_NOPE + i] = 1.0
    pk = jnp.asarray(pk_np).astype(BF16)
    zv = jnp.zeros((KV_LORA, MLA_HEADS // 2, MLA_V), F32)
    wv4 = w_uv.reshape(KV_LORA, MLA_HEADS // 2, 2, MLA_V)
    wv = jnp.stack([jnp.concatenate([wv4[:, :, 0], zv], axis=2),
                    jnp.concatenate([zv, wv4[:, :, 1]], axis=2)], axis=2)
    wv = wv.reshape(KV_LORA, -1).astype(BF16)
    wuk = jnp.transpose(w_uk, (1, 2, 0)).astype(BF16)
    wuv_np = np.zeros((MLA_HEADS, MLA_HEADS * MLA_V), np.float32)
    for h in range(MLA_HEADS):
        wuv_np[h, h * MLA_V:(h + 1) * MLA_V] = 1.0
    wuv = jnp.transpose(w_uv, (1, 0, 2))
    wuv = (jnp.tile(wuv, (1, 1, MLA_HEADS)) * jnp.asarray(wuv_np)[:, None, :]).astype(BF16)
    return w1, wqa, wqb, wk, pk, wv, wuk, wuv


def _rope_table(pos):
    inv = ROPE_BASE ** (-jnp.arange(0, MLA_ROPE, 2, dtype=F32) / MLA_ROPE)
    ang = pos.astype(F32)[:, None] * inv[None, :]
    cos, sin = jnp.cos(ang), jnp.sin(ang)
    n = pos.shape[0]
    one = jnp.ones((n, MLA_NOPE), F32)
    z32 = jnp.zeros((n, HEAD_PAD - MLA_NOPE - MLA_ROPE), F32)
    z64 = jnp.zeros((n, MLA_NOPE), F32)
    cq = jnp.concatenate([one, cos, cos, z32], axis=1) * (MLA_SCALE * LOG2E)
    sq = jnp.concatenate([z64, sin, sin, z32], axis=1) * (MLA_SCALE * LOG2E)
    kt = jnp.concatenate([cos, cos, sin, sin, z64], axis=1)
    return jnp.concatenate([cq, sq, kt], axis=1)


def _seg_ones():
    idx = np.arange(RWKV_DIM) // RWKV_N
    return jnp.asarray((idx[:, None] == idx[None, :]).astype(np.float32)).astype(BF16)


def _dispatch(route, tile_counts, t_total):
    A = t_total * TOP_K
    n_tiles = tile_counts.shape[0]
    counts_te = tile_counts[:, 0, N_GROUPS:N_GROUPS + N_EXPERTS].astype(jnp.int32)
    counts = jnp.sum(counts_te, axis=0)
    blocks_per_e = (counts + MOE_BLK - 1) // MOE_BLK
    blk_end = jnp.cumsum(blocks_per_e)
    blk_start = blk_end - blocks_per_e
    tile_off = jnp.cumsum(counts_te, axis=0) - counts_te
    base = blk_start[None, :] * MOE_BLK + tile_off
    e = route[:, :TOP_K].astype(jnp.int32)
    rank = route[:, 4:4 + TOP_K].astype(jnp.int32)
    base_tok = jnp.repeat(base, t_total // n_tiles, axis=0)
    pick = e[:, :, None] == jnp.arange(N_EXPERTS, dtype=jnp.int32)[None, None, :]
    dest = jnp.sum(jnp.where(pick, base_tok[:, None, :], 0), axis=-1) + rank
    n_blk = -(-A // MOE_BLK) + N_EXPERTS
    blk = jnp.arange(n_blk, dtype=jnp.int32)
    block_e = jnp.minimum(jnp.sum((blk[:, None] >= blk_end[None, :]).astype(jnp.int32), axis=1),
                          N_EXPERTS - 1).astype(jnp.int32)
    n_used = blk_end[-1:].astype(jnp.int32)
    return dest, block_e, n_used


def _sc_gather_rows(table, idx):
    n_rows, width = idx.shape[0], table.shape[1]
    n_workers = SC_CORES * SC_SUBCORES
    per_worker = n_rows // n_workers
    assert n_rows % n_workers == 0 and per_worker % SC_WINDOW == 0
    mesh = plsc.VectorSubcoreMesh(core_axis_name="c", subcore_axis_name="s")

    @functools.partial(
        pl.kernel, mesh=mesh,
        out_type=jax.ShapeDtypeStruct((n_rows, width), table.dtype),
        scratch_types=[
            pltpu.VMEM((SC_WINDOW,), jnp.int32),
            pltpu.VMEM((SC_WINDOW, width), table.dtype),
            pltpu.SemaphoreType.DMA,
        ],
    )
    def gather(table_hbm, idx_hbm, out_hbm, idx_v, rows_v, sem):
        wid = lax.axis_index("s") * SC_CORES + lax.axis_index("c")
        base = wid * per_worker

        @pl.loop(0, per_worker // SC_WINDOW)
        def _(w):
            off = pl.multiple_of(base + w * SC_WINDOW, SC_WINDOW)
            pltpu.sync_copy(idx_hbm.at[pl.ds(off, SC_WINDOW)], idx_v)
            pltpu.async_copy(table_hbm.at[idx_v], rows_v, sem).wait()
            pltpu.sync_copy(rows_v, out_hbm.at[pl.ds(off, SC_WINDOW)])

    return gather(table, idx)


def _sc_scatter_rows(src, idx_a, idx_b, n_slots):
    n_rows, width = src.shape
    n_workers = SC_CORES * SC_SUBCORES
    per_worker = n_rows // n_workers
    assert n_rows % n_workers == 0 and per_worker % SC_WINDOW == 0
    n_win = per_worker // SC_WINDOW
    mesh = plsc.VectorSubcoreMesh(core_axis_name="c", subcore_axis_name="s")

    @functools.partial(
        pl.kernel, mesh=mesh,
        out_type=jax.ShapeDtypeStruct((n_slots, width), src.dtype),
        scratch_types=[
            pltpu.VMEM((1, SC_WINDOW), jnp.int32),
            pltpu.VMEM((1, SC_WINDOW), jnp.int32),
            pltpu.VMEM((SC_WINDOW, width), src.dtype),
        ],
    )
    def scatter(src_hbm, ia_hbm, ib_hbm, out_hbm, ia_v, ib_v, rows_v):
        wid = lax.axis_index("s") * SC_CORES + lax.axis_index("c")

        @pl.loop(0, n_win)
        def _(w):
            win = wid * n_win + w
            off = pl.multiple_of(win * SC_WINDOW, SC_WINDOW)
            pltpu.sync_copy(src_hbm.at[pl.ds(off, SC_WINDOW)], rows_v)
            pltpu.sync_copy(ia_hbm.at[pl.ds(win, 1)], ia_v)
            pltpu.sync_copy(ib_hbm.at[pl.ds(win, 1)], ib_v)
            pltpu.sync_copy(rows_v, out_hbm.at[ia_v.at[0]])
            pltpu.sync_copy(rows_v, out_hbm.at[ib_v.at[0]])

    return scatter(src, idx_a, idx_b)


def kernel(x_prompt, x_sample, cache_ckv, cache_kpe, state_wkv, state_shift, w_in, q_norm_g, kv_norm_g, w_uq,
           w_ukv, mu_shift, w0, w2, a0, a2, g2, k_k, k_a, r_k, lnx_g, lnx_b, w_o, ln1_g, ln1_b, w_gr, b_gr,
           w_er, b_er, w_eg, w_eu, w_ed, ln2_g, ln2_b):
    B, S, D = x_prompt.shape
    DB, DS, _ = x_sample.shape
    past = cache_ckv.shape[2]
    Tp, Ts = B * S, DB * DS
    T = Tp + Ts
    assert D == D_MODEL and DS == SHIFT_GROUP and S % ATTN_TQ == 0 and S % (CHUNK * WKV_SUB) == 0
    assert Tp % TOKEN_TILE == 0 and T % TOKEN_TILE == 0 and w_in.shape[0] == DEPTH

    l = 0
    xp, xs_in = x_prompt.reshape(Tp, D), x_sample.reshape(Ts, D)
    w1, wqa, wqb, wk, pk, wv, wuk, wuv = _prep_weights(w_in[l], w_uq[l], w_ukv[l])
    pos = jnp.concatenate([jnp.arange(S, dtype=jnp.int32),
                           jnp.tile(past + jnp.arange(DS, dtype=jnp.int32), TOKEN_TILE // DS)])
    rope = _rope_table(pos)

    q, kcat, vcat, ckv_p, ckv_s, kpe_p, kpe_s, pr, last_rows = _proj_call(
        xp, xs_in, rope, S // TOKEN_TILE, w1, q_norm_g[l][None], kv_norm_g[l][None], wqa, wqb, wk, pk, wv)

    attn_p = _attn_call(q, kcat, vcat, B, S)
    attn_s = _mla_sample_call(q, ckv_s, kpe_s, cache_ckv[l], cache_kpe[l], wuk, wuv, Tp, DB, DS)

    bnd = jnp.concatenate([jnp.zeros((1, RWKV_IN), F32), last_rows[:-1]], axis=0)
    gidx = jnp.arange(T // SHIFT_GROUP)
    seq_start = (gidx < Tp // SHIFT_GROUP) & (gidx % (S // SHIFT_GROUP) == 0)
    bnd = jnp.where(seq_start[:, None], 0.0, bnd)
    bnd = jnp.concatenate([bnd[:Tp // SHIFT_GROUP], state_shift[l]], axis=0)[:, None, :]

    seg = _seg_ones()
    vec = lambda a: a.reshape(1, -1)
    r, lw, kh, v, na, b, bonus, g = _prep_call(
        pr, bnd, vec(mu_shift[l]), vec(w0[l]), vec(a0[l]), vec(k_k[l]), vec(k_a[l]), vec(r_k[l]),
        w2[l].astype(BF16), a2[l].astype(BF16), g2[l].astype(BF16), seg)

    scan_in = (r, lw, kh, v, na, b)
    h0_p = jnp.zeros((B, RWKV_HEADS, RWKV_N, RWKV_N), F32)
    y_p, hT_p = _wkv_call(scan_in, h0_p, 0, B, S // CHUNK, CHUNK, WKV_SUB)
    h0_s = jnp.swapaxes(state_wkv[l], -1, -2)
    y_s, hT_s = _wkv_call(scan_in, h0_s, Tp, DB, 1, DS, 1)

    wo_b = w_o[l].astype(BF16)
    wr = jnp.concatenate([w_gr[l], w_er[l], jnp.zeros((D, LANE - N_GROUPS - N_EXPERTS), F32)], axis=1)
    wr_hi = wr.astype(BF16)
    wr_lo = (wr - wr_hi.astype(F32)).astype(BF16)
    br = jnp.concatenate([b_gr[l], b_er[l], jnp.zeros((LANE - N_GROUPS - N_EXPERTS,), F32)])[None]
    h, hpk, route, tile_counts = _mix_call(
        xp, xs_in, attn_p, attn_s, y_p, y_s, bonus, g, seg, vec(lnx_g[l]), vec(lnx_b[l]),
        wo_b[:MLA_HEADS * MLA_V], wo_b[MLA_HEADS * MLA_V:], vec(ln1_g[l]), vec(ln1_b[l]),
        jnp.stack([wr_hi, wr_lo]), br)

    dest, block_e, n_used = _dispatch(route, tile_counts, T)
    win = lambda a: a.reshape(T // SC_WINDOW, SC_WINDOW)
    xs = _sc_scatter_rows(hpk, win(dest[:, 0]), win(dest[:, 1]), block_e.shape[0] * MOE_BLK)
    ys = _expert_call(block_e, n_used, xs, w_eg[l], w_eu[l], w_ed[l])
    yab = _sc_gather_rows(ys, jnp.concatenate([dest[:, 0], dest[:, 1]]))
    out_p, out_s = _combine_call(h, yab, route, vec(ln2_g[l]), vec(ln2_b[l]), Tp)

    y_prompt = out_p.reshape(B, S, D)
    y_sample = out_s.reshape(DB, DS, D)
    p_ckv = ckv_p.reshape(1, B, S, KV_LORA)
    p_kpe = kpe_p.reshape(1, B, S, MLA_ROPE)
    s_ckv = ckv_s.reshape(1, DB, DS, KV_LORA)
    s_kpe = kpe_s.reshape(1, DB, DS, MLA_ROPE)
    p_wkv = jnp.swapaxes(hT_p, -1, -2)[None]
    s_wkv = jnp.swapaxes(hT_s, -1, -2)[None]
    gp = S // SHIFT_GROUP
    p_sh = last_rows[gp - 1:B * gp:gp][None]
    s_sh = last_rows[B * gp:][None]
    return (y_prompt, y_sample, p_ckv, p_kpe, p_wkv, p_sh, s_ckv, s_kpe, s_wkv, s_sh)
```

```python
import functools
import math

import numpy as np
import jax
import jax.numpy as jnp
from jax import lax
from jax.experimental import pallas as pl
from jax.experimental.pallas import tpu as pltpu
from jax.experimental.pallas import tpu_sc as plsc

F32 = jnp.float32
BF16 = jnp.bfloat16

D_MODEL = 1024
CHUNK = 64
MLA_HEADS = 8
MLA_NOPE = 64
MLA_ROPE = 32
MLA_V = 64
Q_LORA = 384
KV_LORA = 256
ROPE_BASE = 10000.0
MLA_IN = Q_LORA + KV_LORA + MLA_ROPE
MLA_SCALE = (MLA_NOPE + MLA_ROPE) ** -0.5
RWKV_HEADS = 8
RWKV_N = 64
RWKV_DIM = RWKV_HEADS * RWKV_N
DECAY_LORA = 64
AAA_LORA = 64
GATE_LORA = 128
RWKV_IN = 3 * RWKV_DIM + DECAY_LORA + AAA_LORA + GATE_LORA
N_GROUPS = 4
EXPERTS_PER_GROUP = 8
N_EXPERTS = N_GROUPS * EXPERTS_PER_GROUP
TOP_K = 2
D_EXPERT = 256
MOE_BLK = 512
LN_EPS = 1e-5
RMS_EPS = 1e-6
GN_EPS = 64e-5
NEG_INF = -1e30
DEPTH = 1
DN_ALPHA = (2 * DEPTH) ** 0.25

LANE = 128
HEAD_PAD = 128
PROJ_W = 768 + RWKV_IN
SHIFT_GROUP = 32
TOKEN_TILE = 512
ATTN_TQ = 1024
ATTN_TK = 512
V_ONE_LANE = (MLA_V, 0)
LOG2E = math.log2(math.e)
VMEM_LIMIT = 48 * 1024 * 1024
SC_CORES = 2
SC_SUBCORES = 16
SC_WINDOW = 32
WKV_GROUP = 4
WKV_SUB = 2
WKV_PAR = 2
MIX_PARTS = 2


def _cparams(sem):
    return pltpu.CompilerParams(dimension_semantics=sem, vmem_limit_bytes=VMEM_LIMIT)


def _split3(x):
    hi = x.astype(BF16)
    r1 = x - hi.astype(F32)
    mid = r1.astype(BF16)
    lo = (r1 - mid.astype(F32)).astype(BF16)
    return hi, mid, lo


def _dot(a, b):
    return jnp.dot(a, b, preferred_element_type=F32)


def _dot_nt(a, b):
    return lax.dot_general(a, b, (((1,), (1,)), ((), ())), preferred_element_type=F32)


def _dot_exact_rhs(x, w):
    hi = x.astype(BF16)
    lo = (x - hi.astype(F32)).astype(BF16)
    return _dot(hi, w) + _dot(lo, w)


def _dot_exact_lhs(w, x):
    hi, mid, lo = _split3(x)
    return _dot(w, hi) + _dot(w, mid) + _dot(w, lo)


def _pack_bf16_pairs(x):
    n = x.shape[1] // 2
    bits = pltpu.bitcast(x.astype(BF16).astype(F32), jnp.int32)
    return (bits[:, :n] & jnp.int32(-65536)) | lax.shift_right_logical(bits[:, n:], jnp.int32(16))


def _unpack_bf16_pairs(p):
    hi = pltpu.bitcast(p & jnp.int32(-65536), F32)
    lo = pltpu.bitcast(lax.shift_left(p, jnp.int32(16)), F32)
    return jnp.concatenate([hi, lo], axis=1)


def _sigmoid(x):
    return 1.0 / (1.0 + jnp.exp(-x))


def _layer_norm(x, g, b):
    xc = x - jnp.mean(x, -1, keepdims=True)
    var = jnp.mean(xc * xc, -1, keepdims=True)
    return xc * lax.rsqrt(var + LN_EPS) * g + b


def _proj_kernel(xp_ref, xs_ref, rope_ref, w1_ref, gq_ref, gkv_ref, wqa_ref, wqb_ref, wk_ref, pk_ref, wv_ref,
                 q_ref, k_ref, v_ref, ckvp_ref, ckvs_ref, kpep_ref, kpes_ref, pr_ref, last_ref, *, n_p, tm):
    i = pl.program_id(0)
    is_p = i < n_p
    x = jnp.where(is_p, xp_ref[...], xs_ref[...]).astype(BF16)
    proj = _dot(x, w1_ref[...])
    c_q = proj[:, :Q_LORA]
    c_kv = proj[:, Q_LORA:Q_LORA + KV_LORA]
    kp = proj[:, 640:768]
    pr = proj[:, 768:]
    pr_ref[...] = pr
    last_ref[...] = pr.reshape(tm // SHIFT_GROUP, SHIFT_GROUP, RWKV_IN)[:, SHIFT_GROUP - 1, :]

    cqn = c_q * lax.rsqrt(jnp.mean(c_q * c_q, -1, keepdims=True) + RMS_EPS) * gq_ref[...]
    ckv = c_kv * lax.rsqrt(jnp.mean(c_kv * c_kv, -1, keepdims=True) + RMS_EPS) * gkv_ref[...]

    rope = rope_ref[...]
    cq = rope[:, :LANE]
    sq = rope[:, LANE:2 * LANE]
    kt = rope[:, 2 * LANE:]
    prod = kp * kt
    kpe = prod[:, :MLA_ROPE] + prod[:, MLA_ROPE:2 * MLA_ROPE]

    @pl.when(is_p)
    def _():
        ckvp_ref[...] = ckv
        kpep_ref[...] = kpe

    @pl.when(jnp.logical_not(is_p))
    def _():
        ckvs_ref[...] = ckv
        kpes_ref[...] = kpe

    cqb = cqn.astype(BF16)
    qa = _dot(cqb, wqa_ref[...])
    qb = _dot(cqb, wqb_ref[...])
    for h in range(MLA_HEADS):
        sl = slice(h * HEAD_PAD, (h + 1) * HEAD_PAD)
        q_ref[:, sl] = (qa[:, sl] * cq + qb[:, sl] * sq).astype(BF16)

    ckv_b = ckv.astype(BF16)
    k = _dot(ckv_b, wk_ref[...]) + _dot(kpe.astype(BF16), pk_ref[...])
    k_ref[...] = k.astype(BF16)
    lane = lax.broadcasted_iota(jnp.int32, (1, MLA_HEADS * HEAD_PAD), 1)
    odd = (lane // HEAD_PAD) % 2
    one_lane = jnp.where(odd == 1, V_ONE_LANE[1], V_ONE_LANE[0])
    v_one = jnp.where(lane % HEAD_PAD == one_lane, 1.0, 0.0)
    v_ref[...] = (_dot(ckv_b, wv_ref[...]) + v_one).astype(BF16)


def _split_rows(n_p):
    return (lambda i: (jnp.minimum(i, n_p - 1), 0)), (lambda i: (jnp.maximum(i - n_p, 0), 0))


def _proj_call(xp, xs, rope, rope_tiles, w1, gq, gkv, wqa, wqb, wk, pk, wv):
    Tp, Ts = xp.shape[0], xs.shape[0]
    T = Tp + Ts
    tm = TOKEN_TILE
    n_p = Tp // tm
    row = lambda i: (i, 0)
    full = lambda i: (0, 0)
    row_p, row_s = _split_rows(n_p)
    wide = MLA_HEADS * HEAD_PAD
    ng = tm // SHIFT_GROUP
    return pl.pallas_call(
        functools.partial(_proj_kernel, n_p=n_p, tm=tm),
        grid=(T // tm,),
        in_specs=[
            pl.BlockSpec((tm, D_MODEL), row_p),
            pl.BlockSpec((tm, D_MODEL), row_s),
            pl.BlockSpec((tm, 3 * LANE), lambda i: (jnp.where(i < n_p, i % rope_tiles, rope_tiles), 0)),
            pl.BlockSpec((D_MODEL, PROJ_W), full),
            pl.BlockSpec((1, Q_LORA), full),
            pl.BlockSpec((1, KV_LORA), full),
            pl.BlockSpec((Q_LORA, wide), full),
            pl.BlockSpec((Q_LORA, wide), full),
            pl.BlockSpec((KV_LORA, wide), full),
            pl.BlockSpec((MLA_ROPE, wide), full),
            pl.BlockSpec((KV_LORA, wide), full),
        ],
        out_specs=[
            pl.BlockSpec((tm, wide), row),
            pl.BlockSpec((tm, wide), row),
            pl.BlockSpec((tm, wide), row),
            pl.BlockSpec((tm, KV_LORA), row_p),
            pl.BlockSpec((tm, KV_LORA), row_s),
            pl.BlockSpec((tm, MLA_ROPE), row_p),
            pl.BlockSpec((tm, MLA_ROPE), row_s),
            pl.BlockSpec((tm, RWKV_IN), row),
            pl.BlockSpec((ng, RWKV_IN), row),
        ],
        out_shape=[
            jax.ShapeDtypeStruct((T, wide), BF16),
            jax.ShapeDtypeStruct((T, wide), BF16),
            jax.ShapeDtypeStruct((T, wide), BF16),
            jax.ShapeDtypeStruct((Tp, KV_LORA), F32),
            jax.ShapeDtypeStruct((Ts, KV_LORA), F32),
            jax.ShapeDtypeStruct((Tp, MLA_ROPE), F32),
            jax.ShapeDtypeStruct((Ts, MLA_ROPE), F32),
            jax.ShapeDtypeStruct((T, RWKV_IN), F32),
            jax.ShapeDtypeStruct((T // SHIFT_GROUP, RWKV_IN), F32),
        ],
        compiler_params=_cparams(("arbitrary",)),
        name="proj",
    )(xp, xs, rope, w1, gq, gkv, wqa, wqb, wk, pk, wv)


def _attn_kernel(q_ref, k_ref, v_ref, o_ref, m_scr, acc_scr, *, tq, tk):
    qi = pl.program_id(2)
    m_scr[...] = jnp.full(m_scr.shape, NEG_INF, F32)
    acc_scr[...] = jnp.zeros(acc_scr.shape, F32)
    n_diag = tq // tk

    H = range(2)
    sls = [slice(h * HEAD_PAD, (h + 1) * HEAD_PAD) for h in H]

    def scores(k0, width, rows):
        return [_dot_nt(q_ref[rows, sl], k_ref[pl.ds(k0, width), sl]) for sl in sls]

    def accumulate(s, k0, width, rows):
        m_prev = [m_scr[h, rows, :] for h in H]
        m_new = [jnp.maximum(m_prev[h], jnp.max(s[h], axis=1, keepdims=True)) for h in H]
        pexp = [jnp.exp2(s[h] - jnp.tile(m_new[h], (1, width // LANE))).astype(BF16) for h in H]
        pv = [_dot(pexp[h], v_ref[pl.ds(k0, width), sls[h]]) for h in H]
        for h in H:
            acc_scr[h, rows, :] = jnp.exp2(m_prev[h] - m_new[h]) * acc_scr[h, rows, :] + pv[h]
            m_scr[h, rows, :] = m_new[h]

    def kv_block(k0, width, rows, masked):
        s = scores(k0, width, rows)
        if masked:
            n_rows = rows.stop - rows.start
            r = lax.broadcasted_iota(jnp.int32, (n_rows, width), 0) // CHUNK
            c = lax.broadcasted_iota(jnp.int32, (n_rows, width), 1) // CHUNK
            s = [jnp.where(c <= r, s[h], NEG_INF) for h in H]
        accumulate(s, k0, width, rows)

    all_rows = slice(0, tq)

    def pair(t, carry):
        k0a = pl.multiple_of(2 * t * tq, tq)
        k0b = pl.multiple_of(k0a + tq, tq)
        s_a = scores(k0a, tq, all_rows)
        s_b = scores(k0b, tq, all_rows)
        accumulate(s_a, k0a, tq, all_rows)
        accumulate(s_b, k0b, tq, all_rows)
        return carry

    lax.fori_loop(0, qi // 2, pair, 0)

    @pl.when(qi % 2 == 1)
    def _():
        kv_block(pl.multiple_of((qi - 1) * tq, tq), tq, all_rows, False)

    for d in range(n_diag):
        k0 = pl.multiple_of(qi * tq + d * tk, tk)
        kv_block(k0, tk, slice(d * tk, (d + 1) * tk), True)
        if (d + 1) * tk < tq:
            kv_block(k0, tk, slice((d + 1) * tk, tq), False)
    acc0, acc1 = acc_scr[0], acc_scr[1]
    lane = lax.broadcasted_iota(jnp.int32, acc0.shape, 1)
    l0 = acc0[:, V_ONE_LANE[0]:V_ONE_LANE[0] + 1]
    l1 = acc1[:, V_ONE_LANE[1]:V_ONE_LANE[1] + 1]
    o_ref[...] = jnp.where(lane < MLA_V, acc0 / l0, acc1 / l1).astype(o_ref.dtype)


def _attn_call(q, k, v, n_batch, seq):
    tq, tk = ATTN_TQ, ATTN_TK
    nq = seq // tq
    hp = MLA_HEADS // 2
    resident = pl.BlockSpec((seq, 2 * HEAD_PAD), lambda b, h, i: (b, h), pipeline_mode=pl.Buffered(1))
    return pl.pallas_call(
        functools.partial(_attn_kernel, tq=tq, tk=tk),
        grid=(n_batch, hp, nq),
        in_specs=[
            pl.BlockSpec((tq, 2 * HEAD_PAD), lambda b, h, i: (b * nq + i, h)),
            resident,
            resident,
        ],
        out_specs=pl.BlockSpec((tq, LANE), lambda b, h, i: (b * nq + i, h)),
        out_shape=jax.ShapeDtypeStruct((n_batch * seq, hp * LANE), BF16),
        scratch_shapes=[
            pltpu.VMEM((2, tq, LANE), F32),
            pltpu.VMEM((2, tq, LANE), F32),
        ],
        compiler_params=_cparams(("parallel", "parallel", "arbitrary")),
        name="attn",
    )(q, k, v)


def _mla_sample_kernel(q_ref, cn_ref, kn_ref, cp_ref, kp_ref, wuk_ref, wuv_ref, o_ref, *, past, dec):
    cp = cp_ref[0].astype(BF16)
    kp = kp_ref[0].astype(BF16)
    cn = cn_ref[...].astype(BF16)
    kn = kn_ref[...].astype(BF16)
    R = MLA_HEADS * dec
    qrow = (past + lax.broadcasted_iota(jnp.int32, (R, past), 0) % dec) // CHUNK
    vis_p = (lax.broadcasted_iota(jnp.int32, (R, past), 1) // CHUNK) <= qrow
    qrow_n = (past + lax.broadcasted_iota(jnp.int32, (R, dec), 0) % dec) // CHUNK
    vis_n = ((past + lax.broadcasted_iota(jnp.int32, (R, dec), 1)) // CHUNK) <= qrow_n
    q_lat, qp = [], []
    for h in range(MLA_HEADS):
        qn = q_ref[:, h * HEAD_PAD:h * HEAD_PAD + MLA_NOPE]
        qp.append(q_ref[:, h * HEAD_PAD + MLA_NOPE:h * HEAD_PAD + MLA_NOPE + MLA_ROPE])
        q_lat.append(_dot(qn, wuk_ref[h]).astype(BF16))
    q_lat = jnp.concatenate(q_lat, axis=0)
    qp = jnp.concatenate(qp, axis=0)
    s_p = jnp.where(vis_p, _dot_nt(q_lat, cp) + _dot_nt(qp, kp), NEG_INF)
    s_n = jnp.where(vis_n, _dot_nt(q_lat, cn) + _dot_nt(qp, kn), NEG_INF)
    m = jnp.maximum(jnp.max(s_p, axis=1, keepdims=True), jnp.max(s_n, axis=1, keepdims=True))
    e_p = jnp.exp2(s_p - m)
    e_n = jnp.exp2(s_n - m)
    l = jnp.sum(e_p, axis=1, keepdims=True) + jnp.sum(e_n, axis=1, keepdims=True)
    o_lat = ((_dot(e_p.astype(BF16), cp) + _dot(e_n.astype(BF16), cn)) / l).astype(BF16)
    out = jnp.zeros((dec, MLA_HEADS * MLA_V), F32)
    for h in range(MLA_HEADS):
        out = out + _dot(o_lat[h * dec:(h + 1) * dec], wuv_ref[h])
    o_ref[...] = out.astype(o_ref.dtype)


def _mla_sample_call(q, ckv, kpe, cache_ckv, cache_kpe, wuk, wuv, row0, n_seq, dec):
    past = cache_ckv.shape[1]
    blk0 = row0 // dec
    wide = MLA_HEADS * HEAD_PAD
    return pl.pallas_call(
        functools.partial(_mla_sample_kernel, past=past, dec=dec),
        grid=(n_seq,),
        in_specs=[
            pl.BlockSpec((dec, wide), lambda b: (blk0 + b, 0)),
            pl.BlockSpec((dec, KV_LORA), lambda b: (b, 0)),
            pl.BlockSpec((dec, MLA_ROPE), lambda b: (b, 0)),
            pl.BlockSpec((1, past, KV_LORA), lambda b: (b, 0, 0)),
            pl.BlockSpec((1, past, MLA_ROPE), lambda b: (b, 0, 0)),
            pl.BlockSpec((MLA_HEADS, MLA_NOPE, KV_LORA), lambda b: (0, 0, 0)),
            pl.BlockSpec((MLA_HEADS, KV_LORA, MLA_HEADS * MLA_V), lambda b: (0, 0, 0)),
        ],
        out_specs=pl.BlockSpec((dec, MLA_HEADS * MLA_V), lambda b: (b, 0)),
        out_shape=jax.ShapeDtypeStruct((n_seq * dec, MLA_HEADS * MLA_V), BF16),
        compiler_params=_cparams(("parallel",)),
        name="mla_sample",
    )(q, ckv, kpe, cache_ckv, cache_kpe, wuk, wuv)


def _prep_kernel(pr_ref, bnd_ref, mu_ref, w0_ref, a0_ref, kk_ref, ka_ref, rk_ref, w2_ref, a2_ref, g2_ref,
                 seg_ref, r_ref, lw_ref, kh_ref, v_ref, na_ref, b_ref, bonus_ref, g_ref, *, tm):
    ng = tm // SHIFT_GROUP
    pr = pr_ref[...]
    pr3 = pr.reshape(ng, SHIFT_GROUP, RWKV_IN)
    rolled = pltpu.roll(pr3, 1, 1)
    first = lax.broadcasted_iota(jnp.int32, pr3.shape, 1) == 0
    prev = jnp.where(first, bnd_ref[...], rolled).reshape(tm, RWKV_IN)
    u = pr + mu_ref[...] * (prev - pr)
    o1, o2, o3 = RWKV_DIM, 2 * RWKV_DIM, 3 * RWKV_DIM
    o4, o5 = o3 + DECAY_LORA, o3 + DECAY_LORA + AAA_LORA
    r, k, v = u[:, :o1], u[:, o1:o2], u[:, o2:o3]
    w_lo, a_lo, g_lo = u[:, o3:o4], u[:, o4:o5], u[:, o5:]
    wl = w0_ref[...] + _dot(jnp.tanh(w_lo).astype(BF16), w2_ref[...])
    lw_ref[...] = -math.exp(-0.5) * _sigmoid(wl)
    a = _sigmoid(a0_ref[...] + _dot(a_lo.astype(BF16), a2_ref[...]))
    g_ref[...] = _dot(_sigmoid(g_lo).astype(BF16), g2_ref[...])
    seg = seg_ref[...]
    kk = k * kk_ref[...]
    kk = kk / jnp.maximum(jnp.sqrt(_dot_exact_rhs(kk * kk, seg)), 1e-12)
    kh = k * (1.0 + (a - 1.0) * ka_ref[...])
    r_ref[...] = r
    kh_ref[...] = kh
    v_ref[...] = v
    na_ref[...] = -kk
    b_ref[...] = kk * a
    bonus_ref[...] = _dot_exact_rhs(r * kh * rk_ref[...], seg) * v


def _prep_call(pr, bnd, mu, w0, a0, k_k, k_a, r_k, w2, a2, g2, seg):
    T = pr.shape[0]
    tm = TOKEN_TILE
    row = lambda i: (i, 0)
    full = lambda i: (0, 0)
    vec = pl.BlockSpec((1, RWKV_DIM), full)
    out = pl.BlockSpec((tm, RWKV_DIM), row)
    return pl.pallas_call(
        functools.partial(_prep_kernel, tm=tm),
        grid=(T // tm,),
        in_specs=[
            pl.BlockSpec((tm, RWKV_IN), row),
            pl.BlockSpec((tm // SHIFT_GROUP, 1, RWKV_IN), lambda i: (i, 0, 0)),
            pl.BlockSpec((1, RWKV_IN), full),
            vec, vec, vec, vec, vec,
            pl.BlockSpec((DECAY_LORA, RWKV_DIM), full),
            pl.BlockSpec((AAA_LORA, RWKV_DIM), full),
            pl.BlockSpec((GATE_LORA, RWKV_DIM), full),
            pl.BlockSpec((RWKV_DIM, RWKV_DIM), full),
        ],
        out_specs=[out] * 8,
        out_shape=[jax.ShapeDtypeStruct((T, RWKV_DIM), F32)] * 8,
        compiler_params=_cparams(("parallel",)),
        name="rwkv_prep",
    )(pr, bnd, mu, w0, a0, k_k, k_a, r_k, w2, a2, g2, seg)


def _wkv_kernel(*refs, C, n_sub, n_par):
    GW = WKV_GROUP * RWKV_N
    R = WKV_GROUP * C
    n_grp = RWKV_HEADS // WKV_GROUP
    n_lev = int(round(math.log2(C))) - 1
    c = pl.program_id(1)
    tok = [refs[6 * p:6 * p + 6] for p in range(n_par)]
    h0_ref, y_ref, hT_ref, h_scr = refs[6 * n_par:]
    n_state = n_par * n_grp

    def head_block(hh):
        return slice(hh * RWKV_N, (hh + 1) * RWKV_N)

    @pl.when(c == 0)
    def _():
        h_scr[...] = jnp.zeros(h_scr.shape, F32)
        for p in range(n_par):
            for hd in range(RWKV_HEADS):
                g, hh = divmod(hd, WKV_GROUP)
                h_scr[p * n_grp + g, head_block(hh), head_block(hh)] = h0_ref[p, hd]

    row = lax.broadcasted_iota(jnp.int32, (C, C), 0)
    col = lax.broadcasted_iota(jnp.int32, (C, C), 1)
    tri = jnp.where(col <= row, 1.0, 0.0).astype(BF16)
    rr = lax.broadcasted_iota(jnp.int32, (R, R), 0)
    cc = lax.broadcasted_iota(jnp.int32, (R, R), 1)
    same = (rr // C) == (cc // C)
    ti = lax.broadcasted_iota(jnp.int32, (C, R), 0)
    si = lax.broadcasted_iota(jnp.int32, (C, R), 1) % C
    strict4 = si < ti
    lower4 = si <= ti
    eye4 = jnp.where(si == ti, 1.0, 0.0)
    keep = (lax.broadcasted_iota(jnp.int32, (R, GW), 0) // C
            == lax.broadcasted_iota(jnp.int32, (R, GW), 1) // RWKV_N)
    gr = lax.broadcasted_iota(jnp.int32, (GW, GW), 0)
    gc = lax.broadcasted_iota(jnp.int32, (GW, GW), 1)
    eye_g = gr == gc
    same_head = (gr // RWKV_N) == (gc // RWKV_N)
    eye_g_bf = jnp.where(eye_g, 1.0, 0.0).astype(BF16)

    def rows4(x4):
        return jnp.concatenate([x4] * WKV_GROUP, axis=0)

    def stack(x4):
        return jnp.where(keep, rows4(x4), jnp.zeros((), x4.dtype))


    sls = [slice(g * GW, (g + 1) * GW) for g in range(n_grp)]
    J = [(ci, q) for ci in range(n_sub) for q in range(n_state)]
    ops, p_end = {}, {}
    for p, ci in [(p, ci) for p in range(n_par) for ci in range(n_sub)]:
        r_ref, lw_ref, k_ref, v_ref, a_ref, b_ref = tok[p]
        rows = slice(ci * C, (ci + 1) * C)
        lw = lw_ref[rows, :]
        cum = _dot_exact_lhs(tri, lw)
        cum_end = cum[C - 1:C, :]
        e_neg = jnp.exp(-cum)
        e_end = jnp.exp(cum_end - cum)
        b_in = b_ref[rows, :]
        k_in = k_ref[rows, :]
        full = ((a_ref[rows, :] * jnp.exp(cum - lw)).astype(BF16),
                (r_ref[rows, :] * jnp.exp(cum)).astype(BF16),
                (b_in * e_neg).astype(BF16), (k_in * e_neg).astype(BF16),
                (b_in * e_end).astype(BF16), (k_in * e_end).astype(BF16),
                v_ref[rows, :].astype(BF16))
        p_end[p, ci] = jnp.exp(cum_end)
        for g in range(n_grp):
            ops[ci, p * n_grp + g] = [t[:, sls[g]] for t in full]
    a4, r4, b4, k4, be4, ke4, v4 = [{j: ops[j][i] for j in J} for i in range(7)]
    v_s = {j: stack(v4[j]) for j in J}
    m = {j: _dot_nt(jnp.concatenate([a4[j], r4[j]], axis=0),
                    jnp.concatenate([stack(b4[j]), stack(k4[j])], axis=0)) for j in J}
    l4 = {j: jnp.where(strict4, m[j][:C, :R], 0.0) for j in J}
    a_ak = {j: jnp.where(strict4, m[j][:C, R:], 0.0).astype(BF16) for j in J}
    a_rb = {j: jnp.where(lower4, m[j][C:, :R], 0.0).astype(BF16) for j in J}
    a_rk = {j: jnp.where(lower4, m[j][C:, R:], 0.0).astype(BF16) for j in J}
    def block_diag(x4):
        return jnp.where(same, rows4(x4.astype(BF16)), jnp.zeros((), BF16))

    t4 = {j: eye4 + l4[j] for j in J}
    l_bd = {j: block_diag(l4[j]) for j in J}
    for _ in range(n_lev):
        l4 = {j: _dot(l4[j].astype(BF16), l_bd[j]) for j in J}
        l_bd = {j: block_diag(l4[j]) for j in J}
        t4 = {j: t4[j] + _dot(t4[j].astype(BF16), l_bd[j]) for j in J}
    t_b = {j: t4[j].astype(BF16) for j in J}
    bke_t = {j: _dot_nt(eye_g_bf, jnp.concatenate([be4[j], ke4[j]], axis=0)).astype(BF16) for j in J}

    G = range(n_state)
    h_cur = [h_scr[q] for q in G]
    for ci in range(n_sub):
        rows = slice(ci * C, (ci + 1) * C)
        h0_b = [h_cur[g].astype(BF16) for g in G]
        x4 = [_dot(a4[ci, g], h0_b[g]) + _dot(a_ak[ci, g], v_s[ci, g]) for g in G]
        u4 = [_dot(t_b[ci, g], stack(x4[g].astype(BF16))).astype(BF16) for g in G]
        y4 = [_dot(r4[ci, g], h0_b[g]) + _dot(a_rb[ci, g], stack(u4[g])) + _dot(a_rk[ci, g], v_s[ci, g])
              for g in G]
        uv4 = [jnp.concatenate([u4[g], v4[ci, g]], axis=0) for g in G]
        h_add = [jnp.where(same_head, _dot(bke_t[ci, g], uv4[g]), 0.0) for g in G]
        for q in G:
            p, g = divmod(q, n_grp)
            y_ref[p, rows, sls[g]] = y4[q]
            p_col = jnp.sum(jnp.where(eye_g, p_end[p, ci][:, sls[g]], 0.0), axis=1, keepdims=True)
            h_cur[q] = p_col * h_cur[q] + h_add[q]

    for q in G:
        h_scr[q] = h_cur[q]

    @pl.when(c == pl.num_programs(1) - 1)
    def _():
        for p in range(n_par):
            for hd in range(RWKV_HEADS):
                g, hh = divmod(hd, WKV_GROUP)
                hT_ref[p, hd] = h_scr[p * n_grp + g, head_block(hh), head_block(hh)]


def _wkv_call(arrs, h0, row0, n_seq, n_chunk, C, n_sub, n_par):
    rows = C * n_sub
    steps = n_chunk // n_sub
    blk0 = row0 // rows
    GW = WKV_GROUP * RWKV_N
    n_grp = RWKV_HEADS // WKV_GROUP
    assert n_seq % n_par == 0 and n_chunk % n_sub == 0
    tok = [pl.BlockSpec((rows, RWKV_DIM), lambda b, c, p=p: (blk0 + (b * n_par + p) * steps + c, 0))
           for p in range(n_par)]
    st = pl.BlockSpec((n_par, RWKV_HEADS, RWKV_N, RWKV_N), lambda b, c: (b, 0, 0, 0))
    y, h_fin = pl.pallas_call(
        functools.partial(_wkv_kernel, C=C, n_sub=n_sub, n_par=n_par),
        grid=(n_seq // n_par, steps),
        in_specs=[tok[p] for p in range(n_par) for _ in range(6)] + [st],
        out_specs=[pl.BlockSpec((n_par, rows, RWKV_DIM), lambda b, c: (b, c, 0)), st],
        out_shape=[
            jax.ShapeDtypeStruct((n_seq, n_chunk * C, RWKV_DIM), F32),
            jax.ShapeDtypeStruct((n_seq, RWKV_HEADS, RWKV_N, RWKV_N), F32),
        ],
        scratch_shapes=[pltpu.VMEM((n_par * n_grp, GW, GW), F32)],
        compiler_params=_cparams(("parallel", "arbitrary")),
        name="wkv_c%d" % C,
    )(*(list(arrs) * n_par), h0)
    return y.reshape(n_seq * n_chunk * C, RWKV_DIM), h_fin


def _mix_kernel(xp_ref, xs_ref, attnp_ref, attns_ref, yp_ref, ys_ref, bonus_ref, g_ref, seg_ref, lng_ref,
                lnb_ref, woa_ref, wob_ref, g1_ref, b1_ref, wr_ref, br_ref, tri_ref, h_ref, hpk_ref, route_ref,
                count_ref, *, n_p, tm):
    is_p = pl.program_id(0) < n_p
    P = range(MIX_PARTS)
    bands = [slice(k * tm // MIX_PARTS, (k + 1) * tm // MIX_PARTS) for k in P]
    seg = seg_ref[...]
    inv_n = 1.0 / RWKV_N
    y = [jnp.where(is_p, yp_ref[r, :], ys_ref[r, :]) for r in bands]
    yc = [y[k] - _dot_exact_rhs(y[k], seg) * inv_n for k in P]
    var = [_dot_exact_rhs(yc[k] * yc[k], seg) * inv_n for k in P]
    yn = [yc[k] * lax.rsqrt(var[k] + GN_EPS) * lng_ref[...] + lnb_ref[...] for k in P]
    rw = [((yn[k] + bonus_ref[bands[k], :]) * g_ref[bands[k], :]).astype(BF16) for k in P]
    attn = [jnp.where(is_p, attnp_ref[r, :], attns_ref[r, :]) for r in bands]
    m = [_dot(attn[k], woa_ref[...]) + _dot(rw[k], wob_ref[...]) for k in P]
    x = [jnp.where(is_p, xp_ref[r, :], xs_ref[r, :]) for r in bands]
    h = [_layer_norm(DN_ALPHA * x[k] + m[k], g1_ref[...], b1_ref[...]) for k in P]
    for k in P:
        h_ref[bands[k], :] = h[k]
        hpk_ref[bands[k], :] = _pack_bf16_pairs(h[k])

    h_hi = [h[k].astype(BF16) for k in P]
    h_lo = [(h[k] - h_hi[k].astype(F32)).astype(BF16) for k in P]
    logits = [_dot(h_hi[k], wr_ref[0]) + _dot(h_lo[k], wr_ref[0]) + _dot(h_hi[k], wr_ref[1]) + br_ref[...]
              for k in P]
    lane = lax.broadcasted_iota(jnp.int32, logits[0].shape, 1)
    big = jnp.int32(LANE)

    def route_band(lg):
        gl = jnp.where(lane < N_GROUPS, lg, NEG_INF)
        gmax = jnp.max(gl, axis=1, keepdims=True)
        grp = jnp.min(jnp.where(gl == gmax, lane, big), axis=1, keepdims=True)
        p_grp = 1.0 / jnp.sum(jnp.exp(gl - gmax), axis=1, keepdims=True)
        e_idx = lane - N_GROUPS
        in_grp = (lane >= N_GROUPS) & (lane < N_GROUPS + N_EXPERTS) & ((e_idx // EXPERTS_PER_GROUP) == grp)
        el = jnp.where(in_grp, lg, NEG_INF)
        m1 = jnp.max(el, axis=1, keepdims=True)
        i1 = jnp.min(jnp.where(el == m1, lane, big), axis=1, keepdims=True)
        el2 = jnp.where(lane == i1, NEG_INF, el)
        m2 = jnp.max(el2, axis=1, keepdims=True)
        i2 = jnp.min(jnp.where(el2 == m2, lane, big), axis=1, keepdims=True)
        t = jnp.exp(m2 - m1)
        g1 = p_grp / (1.0 + t)
        return i1, i2, g1, g1 * t

    routed = [route_band(logits[k]) for k in P]
    chosen = jnp.concatenate(
        [jnp.where(lane == routed[k][0], 1.0, jnp.where(lane == routed[k][1], 1.0, 0.0)) for k in P], axis=0)
    before = _dot(tri_ref[...], chosen.astype(BF16))
    count_ref[0] = jnp.sum(chosen, axis=0, keepdims=True)
    for k in P:
        i1, i2, g1, g2 = routed[k]
        bef = before[bands[k]]
        r1 = jnp.sum(jnp.where(lane == i1, bef, 0.0), axis=1, keepdims=True)
        r2 = jnp.sum(jnp.where(lane == i2, bef, 0.0), axis=1, keepdims=True)
        cols = ((i1 - N_GROUPS).astype(F32), (i2 - N_GROUPS).astype(F32), g1, g2, r1, r2)
        route = jnp.zeros(lane.shape, F32)
        for j, col in enumerate(cols):
            route = jnp.where(lane == j, col, route)
        route_ref[bands[k], :] = route


def _mix_call(xp, xs, attn_p, attn_s, y_p, y_s, bonus, g, seg, lnx_g, lnx_b, woa, wob, ln1_g, ln1_b, wr, br):
    T = xp.shape[0] + xs.shape[0]
    tm = TOKEN_TILE
    n_p = xp.shape[0] // tm
    row = lambda i: (i, 0)
    full = lambda i: (0, 0)
    row_p, row_s = _split_rows(n_p)
    half = pl.BlockSpec((tm, RWKV_DIM), row)
    vec5 = pl.BlockSpec((1, RWKV_DIM), full)
    vec10 = pl.BlockSpec((1, D_MODEL), full)
    idx = np.arange(tm)
    tri = jnp.asarray((idx[None, :] < idx[:, None]).astype(np.float32)).astype(BF16)
    return pl.pallas_call(
        functools.partial(_mix_kernel, n_p=n_p, tm=tm),
        grid=(T // tm,),
        in_specs=[
            pl.BlockSpec((tm, D_MODEL), row_p), pl.BlockSpec((tm, D_MODEL), row_s),
            pl.BlockSpec((tm, RWKV_DIM), row_p), pl.BlockSpec((tm, RWKV_DIM), row_s),
            pl.BlockSpec((tm, RWKV_DIM), row_p), pl.BlockSpec((tm, RWKV_DIM), row_s),
            half, half,
            pl.BlockSpec((RWKV_DIM, RWKV_DIM), full), vec5, vec5,
            pl.BlockSpec((RWKV_DIM, D_MODEL), full), pl.BlockSpec((RWKV_DIM, D_MODEL), full),
            vec10, vec10,
            pl.BlockSpec((2, D_MODEL, LANE), lambda i: (0, 0, 0)), pl.BlockSpec((1, LANE), full),
            pl.BlockSpec((tm, tm), full),
        ],
        out_specs=[pl.BlockSpec((tm, D_MODEL), row), pl.BlockSpec((tm, D_MODEL // 2), row),
                   pl.BlockSpec((tm, LANE), row), pl.BlockSpec((1, 1, LANE), lambda i: (i, 0, 0))],
        out_shape=[jax.ShapeDtypeStruct((T, D_MODEL), F32), jax.ShapeDtypeStruct((T, D_MODEL // 2), jnp.int32),
                   jax.ShapeDtypeStruct((T, LANE), F32), jax.ShapeDtypeStruct((T // tm, 1, LANE), F32)],
        compiler_params=_cparams(("parallel",)),
        name="mix",
    )(xp, xs, attn_p, attn_s, y_p, y_s, bonus, g, seg, lnx_g, lnx_b, woa, wob, ln1_g, ln1_b, wr, br, tri)


def _expert_kernel(be_ref, nu_ref, xs_ref, wg_ref, wu_ref, wd_ref, ys_ref, wgu_b, wd_b):
    i = pl.program_id(0)

    @pl.when((i == 0) | (be_ref[i] != be_ref[jnp.maximum(i - 1, 0)]))
    def _():
        wgu_b[:, :D_EXPERT] = wg_ref[0].astype(BF16)
        wgu_b[:, D_EXPERT:] = wu_ref[0].astype(BF16)
        wd_b[...] = wd_ref[0].astype(BF16)

    @pl.when(i < nu_ref[0])
    def _():
        P = range(2)
        bands = [slice(k * MOE_BLK // 2, (k + 1) * MOE_BLK // 2) for k in P]
        xb = [_unpack_bf16_pairs(xs_ref[r, :]).astype(BF16) for r in bands]
        gu = [_dot(xb[k], wgu_b[...]) for k in P]
        act = [(gu[k][:, :D_EXPERT] * _sigmoid(gu[k][:, :D_EXPERT]) * gu[k][:, D_EXPERT:]).astype(BF16)
               for k in P]
        out = [_dot(act[k], wd_b[...]) for k in P]
        for k in P:
            ys_ref[bands[k], :] = _pack_bf16_pairs(out[k])

    @pl.when(i >= nu_ref[0])
    def _():
        ys_ref[...] = jnp.zeros(ys_ref.shape, ys_ref.dtype)


def _expert_call(block_e, n_used, xs, wg, wu, wd):
    n_blk = xs.shape[0] // MOE_BLK
    grid_spec = pltpu.PrefetchScalarGridSpec(
        num_scalar_prefetch=2,
        grid=(n_blk,),
        in_specs=[
            pl.BlockSpec((MOE_BLK, D_MODEL // 2), lambda i, be, nu: (i, 0)),
            pl.BlockSpec((1, D_MODEL, D_EXPERT), lambda i, be, nu: (be[i], 0, 0)),
            pl.BlockSpec((1, D_MODEL, D_EXPERT), lambda i, be, nu: (be[i], 0, 0)),
            pl.BlockSpec((1, D_EXPERT, D_MODEL), lambda i, be, nu: (be[i], 0, 0)),
        ],
        out_specs=pl.BlockSpec((MOE_BLK, D_MODEL // 2), lambda i, be, nu: (i, 0)),
        scratch_shapes=[pltpu.VMEM((D_MODEL, 2 * D_EXPERT), BF16), pltpu.VMEM((D_EXPERT, D_MODEL), BF16)],
    )
    return pl.pallas_call(
        _expert_kernel,
        grid_spec=grid_spec,
        out_shape=jax.ShapeDtypeStruct((n_blk * MOE_BLK, D_MODEL // 2), jnp.int32),
        compiler_params=_cparams(("arbitrary",)),
        name="experts",
    )(block_e, n_used, xs, wg, wu, wd)


def _combine_kernel(h_ref, ya_ref, yb_ref, route_ref, g2_ref, b2_ref, op_ref, os_ref, *, n_p):
    i = pl.program_id(0)
    route = route_ref[...]
    f = _unpack_bf16_pairs(ya_ref[...]) * route[:, 2:3] + _unpack_bf16_pairs(yb_ref[...]) * route[:, 3:4]
    out = _layer_norm(DN_ALPHA * h_ref[...] + f, g2_ref[...], b2_ref[...])

    @pl.when(i < n_p)
    def _():
        op_ref[...] = out

    @pl.when(i >= n_p)
    def _():
        os_ref[...] = out


def _combine_call(h, yab, route, ln2_g, ln2_b, t_prompt):
    T = h.shape[0]
    tm = TOKEN_TILE
    n_t, n_p = T // tm, t_prompt // tm
    row = lambda i: (i, 0)
    full = lambda i: (0, 0)
    row_p, row_s = _split_rows(n_p)
    big = pl.BlockSpec((tm, D_MODEL), row)
    return pl.pallas_call(
        functools.partial(_combine_kernel, n_p=n_p),
        grid=(n_t,),
        in_specs=[big, pl.BlockSpec((tm, D_MODEL // 2), row),
                  pl.BlockSpec((tm, D_MODEL // 2), lambda i: (i + n_t, 0)), pl.BlockSpec((tm, LANE), row),
                  pl.BlockSpec((1, D_MODEL), full), pl.BlockSpec((1, D_MODEL), full)],
        out_specs=[pl.BlockSpec((tm, D_MODEL), row_p), pl.BlockSpec((tm, D_MODEL), row_s)],
        out_shape=[jax.ShapeDtypeStruct((t_prompt, D_MODEL), F32),
                   jax.ShapeDtypeStruct((T - t_prompt, D_MODEL), F32)],
        compiler_params=_cparams(("arbitrary",)),
        name="combine",
    )(h, yab, yab, route, ln2_g, ln2_b)


def _prep_weights(w_in, w_uq, w_ukv):
    half = MLA_ROPE // 2
    kpe_w = w_in[:, Q_LORA + KV_LORA:MLA_IN]
    kpe_b = jnp.concatenate([-kpe_w[:, half:], kpe_w[:, :half]], axis=1)
    w1 = jnp.concatenate([w_in[:, :Q_LORA + KV_LORA], kpe_w, kpe_b,
                          jnp.zeros((D_MODEL, 64), F32), w_in[:, MLA_IN:]], axis=1).astype(BF16)
    pad_q = jnp.zeros((Q_LORA, MLA_HEADS, HEAD_PAD - MLA_NOPE - MLA_ROPE), F32)
    wqa = jnp.concatenate([w_uq, pad_q], axis=2).reshape(Q_LORA, -1).astype(BF16)
    rot = jnp.concatenate([jnp.zeros((Q_LORA, MLA_HEADS, MLA_NOPE), F32),
                           -w_uq[:, :, MLA_NOPE + half:], w_uq[:, :, MLA_NOPE:MLA_NOPE + half], pad_q], axis=2)
    wqb = rot.reshape(Q_LORA, -1).astype(BF16)
    w_uk, w_uv = w_ukv[:, :, :MLA_NOPE], w_ukv[:, :, MLA_NOPE:]
    wk = jnp.concatenate([w_uk, jnp.zeros((KV_LORA, MLA_HEADS, HEAD_PAD - MLA_NOPE), F32)], axis=2)
    wk = wk.reshape(KV_LORA, -1).astype(BF16)
    pk_np = np.zeros((MLA_ROPE, MLA_HEADS * HEAD_PAD), np.float32)
    for h in range(MLA_HEADS):
        for i in range(MLA_ROPE):
            pk_np[i, h * HEAD_PAD + MLA_NOPE + i] = 1.0
    pk = jnp.asarray(pk_np).astype(BF16)
    zv = jnp.zeros((KV_LORA, MLA_HEADS // 2, MLA_V), F32)
    wv4 = w_uv.reshape(KV_LORA, MLA_HEADS // 2, 2, MLA_V)
    wv = jnp.stack([jnp.concatenate([wv4[:, :, 0], zv], axis=2),
                    jnp.concatenate([zv, wv4[:, :, 1]], axis=2)], axis=2)
    wv = wv.reshape(KV_LORA, -1).astype(BF16)
    wuk = jnp.transpose(w_uk, (1, 2, 0)).astype(BF16)
    wuv_np = np.zeros((MLA_HEADS, MLA_HEADS * MLA_V), np.float32)
    for h in range(MLA_HEADS):
        wuv_np[h, h * MLA_V:(h + 1) * MLA_V] = 1.0
    wuv = jnp.transpose(w_uv, (1, 0, 2))
    wuv = (jnp.tile(wuv, (1, 1, MLA_HEADS)) * jnp.asarray(wuv_np)[:, None, :]).astype(BF16)
    return w1, wqa, wqb, wk, pk, wv, wuk, wuv


def _rope_table(pos):
    inv = ROPE_BASE ** (-jnp.arange(0, MLA_ROPE, 2, dtype=F32) / MLA_ROPE)
    ang = pos.astype(F32)[:, None] * inv[None, :]
    cos, sin = lax.optimization_barrier((jnp.cos(ang), jnp.sin(ang)))
    n = pos.shape[0]
    one = jnp.ones((n, MLA_NOPE), F32)
    z32 = jnp.zeros((n, HEAD_PAD - MLA_NOPE - MLA_ROPE), F32)
    z64 = jnp.zeros((n, MLA_NOPE), F32)
    cq = jnp.concatenate([one, cos, cos, z32], axis=1) * (MLA_SCALE * LOG2E)
    sq = jnp.concatenate([z64, sin, sin, z32], axis=1) * (MLA_SCALE * LOG2E)
    kt = jnp.concatenate([cos, cos, sin, sin, z64], axis=1)
    return jnp.concatenate([cq, sq, kt], axis=1)


def _seg_ones():
    idx = np.arange(RWKV_DIM) // RWKV_N
    return jnp.asarray((idx[:, None] == idx[None, :]).astype(np.float32)).astype(BF16)


def _dispatch(route, tile_counts, t_total):
    A = t_total * TOP_K
    n_tiles = tile_counts.shape[0]
    counts_te = tile_counts[:, 0, N_GROUPS:N_GROUPS + N_EXPERTS].astype(jnp.int32)
    counts = jnp.sum(counts_te, axis=0)
    blocks_per_e = (counts + MOE_BLK - 1) // MOE_BLK
    blk_end = jnp.cumsum(blocks_per_e)
    blk_start = blk_end - blocks_per_e
    tile_off = jnp.cumsum(counts_te, axis=0) - counts_te
    base = blk_start[None, :] * MOE_BLK + tile_off
    e = route[:, :TOP_K].astype(jnp.int32)
    rank = route[:, 4:4 + TOP_K].astype(jnp.int32)
    base_tok = jnp.repeat(base, t_total // n_tiles, axis=0)
    pick = e[:, :, None] == jnp.arange(N_EXPERTS, dtype=jnp.int32)[None, None, :]
    dest = jnp.sum(jnp.where(pick, base_tok[:, None, :], 0), axis=-1) + rank
    n_blk = -(-A // MOE_BLK) + N_EXPERTS
    blk = jnp.arange(n_blk, dtype=jnp.int32)
    block_e = jnp.minimum(jnp.sum((blk[:, None] >= blk_end[None, :]).astype(jnp.int32), axis=1),
                          N_EXPERTS - 1).astype(jnp.int32)
    n_used = blk_end[-1:].astype(jnp.int32)
    return dest, block_e, n_used


def _sc_gather_rows(table, idx):
    n_rows, width = idx.shape[0], table.shape[1]
    n_workers = SC_CORES * SC_SUBCORES
    per_worker = n_rows // n_workers
    assert n_rows % n_workers == 0 and per_worker % SC_WINDOW == 0
    mesh = plsc.VectorSubcoreMesh(core_axis_name="c", subcore_axis_name="s")

    @functools.partial(
        pl.kernel, mesh=mesh,
        out_type=jax.ShapeDtypeStruct((n_rows, width), table.dtype),
        scratch_types=[
            pltpu.VMEM((SC_WINDOW,), jnp.int32),
            pltpu.VMEM((SC_WINDOW, width), table.dtype),
            pltpu.SemaphoreType.DMA,
        ],
    )
    def gather(table_hbm, idx_hbm, out_hbm, idx_v, rows_v, sem):
        wid = lax.axis_index("s") * SC_CORES + lax.axis_index("c")
        base = wid * per_worker

        @pl.loop(0, per_worker // SC_WINDOW)
        def _(w):
            off = pl.multiple_of(base + w * SC_WINDOW, SC_WINDOW)
            pltpu.sync_copy(idx_hbm.at[pl.ds(off, SC_WINDOW)], idx_v)
            pltpu.async_copy(table_hbm.at[idx_v], rows_v, sem).wait()
            pltpu.sync_copy(rows_v, out_hbm.at[pl.ds(off, SC_WINDOW)])

    return gather(table, idx)


def _sc_scatter_rows(src, idx_a, idx_b, n_slots):
    n_rows, width = src.shape
    n_workers = SC_CORES * SC_SUBCORES
    per_worker = n_rows // n_workers
    assert n_rows % n_workers == 0 and per_worker % SC_WINDOW == 0
    n_win = per_worker // SC_WINDOW
    mesh = plsc.VectorSubcoreMesh(core_axis_name="c", subcore_axis_name="s")

    @functools.partial(
        pl.kernel, mesh=mesh,
        out_type=jax.ShapeDtypeStruct((n_slots, width), src.dtype),
        scratch_types=[
            pltpu.VMEM((1, SC_WINDOW), jnp.int32),
            pltpu.VMEM((1, SC_WINDOW), jnp.int32),
            pltpu.VMEM((SC_WINDOW, width), src.dtype),
        ],
    )
    def scatter(src_hbm, ia_hbm, ib_hbm, out_hbm, ia_v, ib_v, rows_v):
        wid = lax.axis_index("s") * SC_CORES + lax.axis_index("c")

        @pl.loop(0, n_win)
        def _(w):
            win = wid * n_win + w
            off = pl.multiple_of(win * SC_WINDOW, SC_WINDOW)
            pltpu.sync_copy(src_hbm.at[pl.ds(off, SC_WINDOW)], rows_v)
            pltpu.sync_copy(ia_hbm.at[pl.ds(win, 1)], ia_v)
            pltpu.sync_copy(ib_hbm.at[pl.ds(win, 1)], ib_v)
            pltpu.sync_copy(rows_v, out_hbm.at[ia_v.at[0]])
            pltpu.sync_copy(rows_v, out_hbm.at[ib_v.at[0]])

    return scatter(src, idx_a, idx_b)


def kernel(x_prompt, x_sample, cache_ckv, cache_kpe, state_wkv, state_shift, w_in, q_norm_g, kv_norm_g, w_uq,
           w_ukv, mu_shift, w0, w2, a0, a2, g2, k_k, k_a, r_k, lnx_g, lnx_b, w_o, ln1_g, ln1_b, w_gr, b_gr,
           w_er, b_er, w_eg, w_eu, w_ed, ln2_g, ln2_b):
    B, S, D = x_prompt.shape
    DB, DS, _ = x_sample.shape
    past = cache_ckv.shape[2]
    Tp, Ts = B * S, DB * DS
    T = Tp + Ts
    assert D == D_MODEL and DS == SHIFT_GROUP and S % ATTN_TQ == 0 and S % (CHUNK * WKV_SUB) == 0
    assert Tp % TOKEN_TILE == 0 and T % TOKEN_TILE == 0 and w_in.shape[0] == DEPTH

    l = 0
    xp, xs_in = x_prompt.reshape(Tp, D), x_sample.reshape(Ts, D)
    w1, wqa, wqb, wk, pk, wv, wuk, wuv = _prep_weights(w_in[l], w_uq[l], w_ukv[l])
    pos = jnp.concatenate([jnp.arange(S, dtype=jnp.int32),
                           jnp.tile(past + jnp.arange(DS, dtype=jnp.int32), TOKEN_TILE // DS)])
    rope = _rope_table(pos)

    q, kcat, vcat, ckv_p, ckv_s, kpe_p, kpe_s, pr, last_rows = _proj_call(
        xp, xs_in, rope, S // TOKEN_TILE, w1, q_norm_g[l][None], kv_norm_g[l][None], wqa, wqb, wk, pk, wv)

    attn_p = _attn_call(q, kcat, vcat, B, S)
    attn_s = _mla_sample_call(q, ckv_s, kpe_s, cache_ckv[l], cache_kpe[l], wuk, wuv, Tp, DB, DS)

    bnd = jnp.concatenate([jnp.zeros((1, RWKV_IN), F32), last_rows[:-1]], axis=0)
    gidx = jnp.arange(T // SHIFT_GROUP)
    seq_start = (gidx < Tp // SHIFT_GROUP) & (gidx % (S // SHIFT_GROUP) == 0)
    bnd = jnp.where(seq_start[:, None], 0.0, bnd)
    bnd = jnp.concatenate([bnd[:Tp // SHIFT_GROUP], state_shift[l]], axis=0)[:, None, :]

    seg = _seg_ones()
    vec = lambda a: a.reshape(1, -1)
    r, lw, kh, v, na, b, bonus, g = _prep_call(
        pr, bnd, vec(mu_shift[l]), vec(w0[l]), vec(a0[l]), vec(k_k[l]), vec(k_a[l]), vec(r_k[l]),
        w2[l].astype(BF16), a2[l].astype(BF16), g2[l].astype(BF16), seg)

    scan_in = (r, lw, kh, v, na, b)
    h0_p = jnp.zeros((B, RWKV_HEADS, RWKV_N, RWKV_N), F32)
    y_p, hT_p = _wkv_call(scan_in, h0_p, 0, B, S // CHUNK, CHUNK, WKV_SUB, math.gcd(B, WKV_PAR))
    h0_s = jnp.swapaxes(state_wkv[l], -1, -2)
    y_s, hT_s = _wkv_call(scan_in, h0_s, Tp, DB, 1, DS, 1, math.gcd(DB, WKV_SUB * WKV_PAR))

    wo_b = w_o[l].astype(BF16)
    wr = jnp.concatenate([w_gr[l], w_er[l], jnp.zeros((D, LANE - N_GROUPS - N_EXPERTS), F32)], axis=1)
    wr_hi = wr.astype(BF16)
    wr_lo = (wr - wr_hi.astype(F32)).astype(BF16)
    br = jnp.concatenate([b_gr[l], b_er[l], jnp.zeros((LANE - N_GROUPS - N_EXPERTS,), F32)])[None]
    h, hpk, route, tile_counts = _mix_call(
        xp, xs_in, attn_p, attn_s, y_p, y_s, bonus, g, seg, vec(lnx_g[l]), vec(lnx_b[l]),
        wo_b[:MLA_HEADS * MLA_V], wo_b[MLA_HEADS * MLA_V:], vec(ln1_g[l]), vec(ln1_b[l]),
        jnp.stack([wr_hi, wr_lo]), br)

    dest, block_e, n_used = _dispatch(route, tile_counts, T)
    win = lambda a: a.reshape(T // SC_WINDOW, SC_WINDOW)
    xs = _sc_scatter_rows(hpk, win(dest[:, 0]), win(dest[:, 1]), block_e.shape[0] * MOE_BLK)
    ys = _expert_call(block_e, n_used, xs, w_eg[l], w_eu[l], w_ed[l])
    yab = _sc_gather_rows(ys, jnp.concatenate([dest[:, 0], dest[:, 1]]))
    out_p, out_s = _combine_call(h, yab, route, vec(ln2_g[l]), vec(ln2_b[l]), Tp)

    y_prompt = out_p.reshape(B, S, D)
    y_sample = out_s.reshape(DB, DS, D)
    p_ckv = ckv_p.reshape(1, B, S, KV_LORA)
    p_kpe = kpe_p.reshape(1, B, S, MLA_ROPE)
    s_ckv = ckv_s.reshape(1, DB, DS, KV_LORA)
    s_kpe = kpe_s.reshape(1, DB, DS, MLA_ROPE)
    p_wkv = jnp.swapaxes(hT_p, -1, -2)[None]
    s_wkv = jnp.swapaxes(hT_s, -1, -2)[None]
    gp = S // SHIFT_GROUP
    p_sh = last_rows[gp - 1:B * gp:gp][None]
    s_sh = last_rows[B * gp:][None]
    return (y_prompt, y_sample, p_ckv, p_kpe, p_wkv, p_sh, s_ckv, s_kpe, s_wkv, s_sh)
```

```python
import functools
import math

import numpy as np
import jax
import jax.numpy as jnp
from jax import lax
from jax.experimental import pallas as pl
from jax.experimental.pallas import tpu as pltpu
from jax.experimental.pallas import tpu_sc as plsc

F32 = jnp.float32
BF16 = jnp.bfloat16

D_MODEL = 1024
CHUNK = 64
MLA_HEADS = 8
MLA_NOPE = 64
MLA_ROPE = 32
MLA_V = 64
Q_LORA = 384
KV_LORA = 256
ROPE_BASE = 10000.0
MLA_IN = Q_LORA + KV_LORA + MLA_ROPE
MLA_SCALE = (MLA_NOPE + MLA_ROPE) ** -0.5
RWKV_HEADS = 8
RWKV_N = 64
RWKV_DIM = RWKV_HEADS * RWKV_N
DECAY_LORA = 64
AAA_LORA = 64
GATE_LORA = 128
RWKV_IN = 3 * RWKV_DIM + DECAY_LORA + AAA_LORA + GATE_LORA
N_GROUPS = 4
EXPERTS_PER_GROUP = 8
N_EXPERTS = N_GROUPS * EXPERTS_PER_GROUP
TOP_K = 2
D_EXPERT = 256
MOE_BLK = 512
LN_EPS = 1e-5
RMS_EPS = 1e-6
GN_EPS = 64e-5
NEG_INF = -1e30
DEPTH = 1
DN_ALPHA = (2 * DEPTH) ** 0.25

LANE = 128
HEAD_PAD = 128
PROJ_W = 768 + RWKV_IN
SHIFT_GROUP = 32
TOKEN_TILE = 512
ATTN_TQ = 1024
ATTN_TK = 512
V_ONE_LANE = (MLA_V, 0)
LOG2E = math.log2(math.e)
VMEM_LIMIT = 48 * 1024 * 1024
SC_CORES = 2
SC_SUBCORES = 16
SC_WINDOW = 32
WKV_GROUP = 4
WKV_SUB = 2
WKV_PAR = 2
MIX_PARTS = 2


def _cparams(sem):
    return pltpu.CompilerParams(dimension_semantics=sem, vmem_limit_bytes=VMEM_LIMIT)


def _split3(x):
    hi = x.astype(BF16)
    r1 = x - hi.astype(F32)
    mid = r1.astype(BF16)
    lo = (r1 - mid.astype(F32)).astype(BF16)
    return hi, mid, lo


def _dot(a, b):
    return jnp.dot(a, b, preferred_element_type=F32)


def _dot_nt(a, b):
    return lax.dot_general(a, b, (((1,), (1,)), ((), ())), preferred_element_type=F32)


def _dot_exact_rhs(x, w):
    hi = x.astype(BF16)
    lo = (x - hi.astype(F32)).astype(BF16)
    return _dot(hi, w) + _dot(lo, w)


def _dot_exact_lhs(w, x):
    hi, mid, lo = _split3(x)
    return _dot(w, hi) + _dot(w, mid) + _dot(w, lo)


def _pack_bf16_pairs(x):
    n = x.shape[1] // 2
    bits = pltpu.bitcast(x.astype(BF16).astype(F32), jnp.int32)
    return (bits[:, :n] & jnp.int32(-65536)) | lax.shift_right_logical(bits[:, n:], jnp.int32(16))


def _unpack_bf16_pairs(p):
    hi = pltpu.bitcast(p & jnp.int32(-65536), F32)
    lo = pltpu.bitcast(lax.shift_left(p, jnp.int32(16)), F32)
    return jnp.concatenate([hi, lo], axis=1)


def _sigmoid(x):
    return 1.0 / (1.0 + jnp.exp(-x))


def _layer_norm(x, g, b):
    xc = x - jnp.mean(x, -1, keepdims=True)
    var = jnp.mean(xc * xc, -1, keepdims=True)
    return xc * lax.rsqrt(var + LN_EPS) * g + b


def _proj_kernel(xp_ref, xs_ref, rope_ref, rp_ref, rc_ref, w1_ref, gq_ref, gkv_ref, wqa_ref, wk_ref, pk_ref, wv_ref,
                 q_ref, k_ref, v_ref, ckvp_ref, ckvs_ref, kpep_ref, kpes_ref, pr_ref, last_ref, *, n_p, tm):
    i = pl.program_id(0)
    is_p = i < n_p
    x = jnp.where(is_p, xp_ref[...], xs_ref[...]).astype(BF16)
    proj = _dot(x, w1_ref[...])
    c_q = proj[:, :Q_LORA]
    c_kv = proj[:, Q_LORA:Q_LORA + KV_LORA]
    kp = proj[:, 640:768]
    pr = proj[:, 768:]
    pr_ref[...] = pr
    last_ref[...] = pr.reshape(tm // SHIFT_GROUP, SHIFT_GROUP, RWKV_IN)[:, SHIFT_GROUP - 1, :]

    cqn = c_q * lax.rsqrt(jnp.mean(c_q * c_q, -1, keepdims=True) + RMS_EPS) * gq_ref[...]
    ckv = c_kv * lax.rsqrt(jnp.mean(c_kv * c_kv, -1, keepdims=True) + RMS_EPS) * gkv_ref[...]

    rope = (_dot_exact_rhs(rope_ref[...], rp_ref[...]) + rc_ref[0:1, :]) * rc_ref[1:2, :]
    cq = rope[:, :LANE]
    sq_up = rope[:, LANE:2 * LANE]
    sq_dn = rope[:, 2 * LANE:3 * LANE]
    kt = rope[:, 3 * LANE:]
    prod = kp * kt
    kpe = prod[:, :MLA_ROPE] + prod[:, MLA_ROPE:2 * MLA_ROPE]

    @pl.when(is_p)
    def _():
        ckvp_ref[...] = ckv
        kpep_ref[...] = kpe

    @pl.when(jnp.logical_not(is_p))
    def _():
        ckvs_ref[...] = ckv
        kpes_ref[...] = kpe

    qa = _dot(cqn.astype(BF16), wqa_ref[...])
    half = MLA_ROPE // 2
    for h in range(MLA_HEADS):
        qh = qa[:, h * HEAD_PAD:(h + 1) * HEAD_PAD]
        rot = pltpu.roll(qh, HEAD_PAD - half, 1) * sq_up + pltpu.roll(qh, half, 1) * sq_dn
        q_ref[:, h * HEAD_PAD:(h + 1) * HEAD_PAD] = (qh * cq + rot).astype(BF16)

    ckv_b = ckv.astype(BF16)
    k = _dot(ckv_b, wk_ref[...]) + _dot(kpe.astype(BF16), pk_ref[...])
    k_ref[...] = k.astype(BF16)
    lane = lax.broadcasted_iota(jnp.int32, (1, MLA_HEADS * HEAD_PAD), 1)
    odd = (lane // HEAD_PAD) % 2
    one_lane = jnp.where(odd == 1, V_ONE_LANE[1], V_ONE_LANE[0])
    v_one = jnp.where(lane % HEAD_PAD == one_lane, 1.0, 0.0)
    v_ref[...] = (_dot(ckv_b, wv_ref[...]) + v_one).astype(BF16)


def _split_rows(n_p):
    return (lambda i: (jnp.minimum(i, n_p - 1), 0)), (lambda i: (jnp.maximum(i - n_p, 0), 0))


def _proj_call(xp, xs, rope, rope_tiles, rope_place, rope_rows, w1, gq, gkv, wqa, wk, pk, wv):
    Tp, Ts = xp.shape[0], xs.shape[0]
    T = Tp + Ts
    tm = TOKEN_TILE
    n_p = Tp // tm
    row = lambda i: (i, 0)
    full = lambda i: (0, 0)
    row_p, row_s = _split_rows(n_p)
    wide = MLA_HEADS * HEAD_PAD
    ng = tm // SHIFT_GROUP
    return pl.pallas_call(
        functools.partial(_proj_kernel, n_p=n_p, tm=tm),
        grid=(T // tm,),
        in_specs=[
            pl.BlockSpec((tm, D_MODEL), row_p),
            pl.BlockSpec((tm, D_MODEL), row_s),
            pl.BlockSpec((tm, MLA_ROPE), lambda i: (jnp.where(i < n_p, i % rope_tiles, rope_tiles), 0)),
            pl.BlockSpec((MLA_ROPE, 4 * LANE), full),
            pl.BlockSpec((2, 4 * LANE), full),
            pl.BlockSpec((D_MODEL, PROJ_W), full),
            pl.BlockSpec((1, Q_LORA), full),
            pl.BlockSpec((1, KV_LORA), full),
            pl.BlockSpec((Q_LORA, wide), full),
            pl.BlockSpec((KV_LORA, wide), full),
            pl.BlockSpec((MLA_ROPE, wide), full),
            pl.BlockSpec((KV_LORA, wide), full),
        ],
        out_specs=[
            pl.BlockSpec((tm, wide), row),
            pl.BlockSpec((tm, wide), row),
            pl.BlockSpec((tm, wide), row),
            pl.BlockSpec((tm, KV_LORA), row_p),
            pl.BlockSpec((tm, KV_LORA), row_s),
            pl.BlockSpec((tm, MLA_ROPE), row_p),
            pl.BlockSpec((tm, MLA_ROPE), row_s),
            pl.BlockSpec((tm, RWKV_IN), row),
            pl.BlockSpec((ng, RWKV_IN), row),
        ],
        out_shape=[
            jax.ShapeDtypeStruct((T, wide), BF16),
            jax.ShapeDtypeStruct((T, wide), BF16),
            jax.ShapeDtypeStruct((T, wide), BF16),
            jax.ShapeDtypeStruct((Tp, KV_LORA), F32),
            jax.ShapeDtypeStruct((Ts, KV_LORA), F32),
            jax.ShapeDtypeStruct((Tp, MLA_ROPE), F32),
            jax.ShapeDtypeStruct((Ts, MLA_ROPE), F32),
            jax.ShapeDtypeStruct((T, RWKV_IN), F32),
            jax.ShapeDtypeStruct((T // SHIFT_GROUP, RWKV_IN), F32),
        ],
        compiler_params=_cparams(("arbitrary",)),
        name="proj",
    )(xp, xs, rope, rope_place, rope_rows, w1, gq, gkv, wqa, wk, pk, wv)


def _attn_kernel(q_ref, k_ref, v_ref, o_ref, m_scr, acc_scr, *, tq, tk):
    qi = pl.program_id(2)
    m_scr[...] = jnp.full(m_scr.shape, NEG_INF, F32)
    acc_scr[...] = jnp.zeros(acc_scr.shape, F32)
    n_diag = tq // tk

    H = range(2)
    sls = [slice(h * HEAD_PAD, (h + 1) * HEAD_PAD) for h in H]

    def scores(k0, width, rows):
        return [_dot_nt(q_ref[rows, sl], k_ref[pl.ds(k0, width), sl]) for sl in sls]

    def accumulate(s, k0, width, rows):
        m_prev = [m_scr[h, rows, :] for h in H]
        m_new = [jnp.maximum(m_prev[h], jnp.max(s[h], axis=1, keepdims=True)) for h in H]
        pexp = [jnp.exp2(s[h] - jnp.tile(m_new[h], (1, width // LANE))).astype(BF16) for h in H]
        pv = [_dot(pexp[h], v_ref[pl.ds(k0, width), sls[h]]) for h in H]
        for h in H:
            acc_scr[h, rows, :] = jnp.exp2(m_prev[h] - m_new[h]) * acc_scr[h, rows, :] + pv[h]
            m_scr[h, rows, :] = m_new[h]

    def kv_block(k0, width, rows, masked):
        s = scores(k0, width, rows)
        if masked:
            n_rows = rows.stop - rows.start
            r = lax.broadcasted_iota(jnp.int32, (n_rows, width), 0) // CHUNK
            c = lax.broadcasted_iota(jnp.int32, (n_rows, width), 1) // CHUNK
            s = [jnp.where(c <= r, s[h], NEG_INF) for h in H]
        accumulate(s, k0, width, rows)

    all_rows = slice(0, tq)

    def pair(t, carry):
        k0a = pl.multiple_of(2 * t * tq, tq)
        k0b = pl.multiple_of(k0a + tq, tq)
        s_a = scores(k0a, tq, all_rows)
        s_b = scores(k0b, tq, all_rows)
        accumulate(s_a, k0a, tq, all_rows)
        accumulate(s_b, k0b, tq, all_rows)
        return carry

    lax.fori_loop(0, qi // 2, pair, 0)

    @pl.when(qi % 2 == 1)
    def _():
        kv_block(pl.multiple_of((qi - 1) * tq, tq), tq, all_rows, False)

    for d in range(n_diag):
        k0 = pl.multiple_of(qi * tq + d * tk, tk)
        kv_block(k0, tk, slice(d * tk, (d + 1) * tk), True)
        if (d + 1) * tk < tq:
            kv_block(k0, tk, slice((d + 1) * tk, tq), False)
    acc0, acc1 = acc_scr[0], acc_scr[1]
    lane = lax.broadcasted_iota(jnp.int32, acc0.shape, 1)
    l0 = acc0[:, V_ONE_LANE[0]:V_ONE_LANE[0] + 1]
    l1 = acc1[:, V_ONE_LANE[1]:V_ONE_LANE[1] + 1]
    o_ref[...] = jnp.where(lane < MLA_V, acc0 / l0, acc1 / l1).astype(o_ref.dtype)


def _attn_call(q, k, v, n_batch, seq):
    tq, tk = ATTN_TQ, ATTN_TK
    nq = seq // tq
    hp = MLA_HEADS // 2
    resident = pl.BlockSpec((seq, 2 * HEAD_PAD), lambda b, h, i: (b, h), pipeline_mode=pl.Buffered(1))
    return pl.pallas_call(
        functools.partial(_attn_kernel, tq=tq, tk=tk),
        grid=(n_batch, hp, nq),
        in_specs=[
            pl.BlockSpec((tq, 2 * HEAD_PAD), lambda b, h, i: (b * nq + i, h)),
            resident,
            resident,
        ],
        out_specs=pl.BlockSpec((tq, LANE), lambda b, h, i: (b * nq + i, h)),
        out_shape=jax.ShapeDtypeStruct((n_batch * seq, hp * LANE), BF16),
        scratch_shapes=[
            pltpu.VMEM((2, tq, LANE), F32),
            pltpu.VMEM((2, tq, LANE), F32),
        ],
        compiler_params=_cparams(("parallel", "parallel", "arbitrary")),
        name="attn",
    )(q, k, v)


def _mla_sample_kernel(q_ref, cn_ref, kn_ref, cp_ref, kp_ref, wuk_ref, wuv_ref, o_ref, *, past, dec):
    cp = cp_ref[0].astype(BF16)
    kp = kp_ref[0].astype(BF16)
    cn = cn_ref[...].astype(BF16)
    kn = kn_ref[...].astype(BF16)
    R = MLA_HEADS * dec
    qrow = (past + lax.broadcasted_iota(jnp.int32, (R, past), 0) % dec) // CHUNK
    vis_p = (lax.broadcasted_iota(jnp.int32, (R, past), 1) // CHUNK) <= qrow
    qrow_n = (past + lax.broadcasted_iota(jnp.int32, (R, dec), 0) % dec) // CHUNK
    vis_n = ((past + lax.broadcasted_iota(jnp.int32, (R, dec), 1)) // CHUNK) <= qrow_n
    q_lat, qp = [], []
    for h in range(MLA_HEADS):
        qn = q_ref[:, h * HEAD_PAD:h * HEAD_PAD + MLA_NOPE]
        qp.append(q_ref[:, h * HEAD_PAD + MLA_NOPE:h * HEAD_PAD + MLA_NOPE + MLA_ROPE])
        q_lat.append(_dot(qn, wuk_ref[h]).astype(BF16))
    q_lat = jnp.concatenate(q_lat, axis=0)
    qp = jnp.concatenate(qp, axis=0)
    s_p = jnp.where(vis_p, _dot_nt(q_lat, cp) + _dot_nt(qp, kp), NEG_INF)
    s_n = jnp.where(vis_n, _dot_nt(q_lat, cn) + _dot_nt(qp, kn), NEG_INF)
    m = jnp.maximum(jnp.max(s_p, axis=1, keepdims=True), jnp.max(s_n, axis=1, keepdims=True))
    e_p = jnp.exp2(s_p - m)
    e_n = jnp.exp2(s_n - m)
    l = jnp.sum(e_p, axis=1, keepdims=True) + jnp.sum(e_n, axis=1, keepdims=True)
    o_lat = ((_dot(e_p.astype(BF16), cp) + _dot(e_n.astype(BF16), cn)) / l).astype(BF16)
    out = jnp.zeros((dec, MLA_HEADS * MLA_V), F32)
    for h in range(MLA_HEADS):
        out = out + _dot(o_lat[h * dec:(h + 1) * dec], wuv_ref[h])
    o_ref[...] = out.astype(o_ref.dtype)


def _mla_sample_call(q, ckv, kpe, cache_ckv, cache_kpe, wuk, wuv, row0, n_seq, dec):
    past = cache_ckv.shape[1]
    blk0 = row0 // dec
    wide = MLA_HEADS * HEAD_PAD
    return pl.pallas_call(
        functools.partial(_mla_sample_kernel, past=past, dec=dec),
        grid=(n_seq,),
        in_specs=[
            pl.BlockSpec((dec, wide), lambda b: (blk0 + b, 0)),
            pl.BlockSpec((dec, KV_LORA), lambda b: (b, 0)),
            pl.BlockSpec((dec, MLA_ROPE), lambda b: (b, 0)),
            pl.BlockSpec((1, past, KV_LORA), lambda b: (b, 0, 0)),
            pl.BlockSpec((1, past, MLA_ROPE), lambda b: (b, 0, 0)),
            pl.BlockSpec((MLA_HEADS, MLA_NOPE, KV_LORA), lambda b: (0, 0, 0)),
            pl.BlockSpec((MLA_HEADS, KV_LORA, MLA_HEADS * MLA_V), lambda b: (0, 0, 0)),
        ],
        out_specs=pl.BlockSpec((dec, MLA_HEADS * MLA_V), lambda b: (b, 0)),
        out_shape=jax.ShapeDtypeStruct((n_seq * dec, MLA_HEADS * MLA_V), BF16),
        compiler_params=_cparams(("parallel",)),
        name="mla_sample",
    )(q, ckv, kpe, cache_ckv, cache_kpe, wuk, wuv)


def _prep_kernel(pr_ref, bnd_ref, mu_ref, w0_ref, a0_ref, kk_ref, ka_ref, rk_ref, w2_ref, a2_ref, g2_ref,
                 seg_ref, r_ref, lw_ref, kh_ref, v_ref, na_ref, b_ref, bonus_ref, g_ref, *, tm):
    ng = tm // SHIFT_GROUP
    pr = pr_ref[...]
    pr3 = pr.reshape(ng, SHIFT_GROUP, RWKV_IN)
    rolled = pltpu.roll(pr3, 1, 1)
    first = lax.broadcasted_iota(jnp.int32, pr3.shape, 1) == 0
    prev = jnp.where(first, bnd_ref[...], rolled).reshape(tm, RWKV_IN)
    u = pr + mu_ref[...] * (prev - pr)
    o1, o2, o3 = RWKV_DIM, 2 * RWKV_DIM, 3 * RWKV_DIM
    o4, o5 = o3 + DECAY_LORA, o3 + DECAY_LORA + AAA_LORA
    r, k, v = u[:, :o1], u[:, o1:o2], u[:, o2:o3]
    w_lo, a_lo, g_lo = u[:, o3:o4], u[:, o4:o5], u[:, o5:]
    wl = w0_ref[...] + _dot(jnp.tanh(w_lo).astype(BF16), w2_ref[...])
    lw_ref[...] = -math.exp(-0.5) * _sigmoid(wl)
    a = _sigmoid(a0_ref[...] + _dot(a_lo.astype(BF16), a2_ref[...]))
    g_ref[...] = _dot(_sigmoid(g_lo).astype(BF16), g2_ref[...])
    seg = seg_ref[...]
    kk = k * kk_ref[...]
    kk = kk / jnp.maximum(jnp.sqrt(_dot_exact_rhs(kk * kk, seg)), 1e-12)
    kh = k * (1.0 + (a - 1.0) * ka_ref[...])
    r_ref[...] = r
    kh_ref[...] = kh
    v_ref[...] = v
    na_ref[...] = -kk
    b_ref[...] = kk * a
    bonus_ref[...] = _dot_exact_rhs(r * kh * rk_ref[...], seg) * v


def _prep_call(pr, bnd, mu, w0, a0, k_k, k_a, r_k, w2, a2, g2, seg):
    T = pr.shape[0]
    tm = TOKEN_TILE
    row = lambda i: (i, 0)
    full = lambda i: (0, 0)
    vec = pl.BlockSpec((1, RWKV_DIM), full)
    out = pl.BlockSpec((tm, RWKV_DIM), row)
    return pl.pallas_call(
        functools.partial(_prep_kernel, tm=tm),
        grid=(T // tm,),
        in_specs=[
            pl.BlockSpec((tm, RWKV_IN), row),
            pl.BlockSpec((tm // SHIFT_GROUP, 1, RWKV_IN), lambda i: (i, 0, 0)),
            pl.BlockSpec((1, RWKV_IN), full),
            vec, vec, vec, vec, vec,
            pl.BlockSpec((DECAY_LORA, RWKV_DIM), full),
            pl.BlockSpec((AAA_LORA, RWKV_DIM), full),
            pl.BlockSpec((GATE_LORA, RWKV_DIM), full),
            pl.BlockSpec((RWKV_DIM, RWKV_DIM), full),
        ],
        out_specs=[out] * 8,
        out_shape=[jax.ShapeDtypeStruct((T, RWKV_DIM), F32)] * 8,
        compiler_params=_cparams(("parallel",)),
        name="rwkv_prep",
    )(pr, bnd, mu, w0, a0, k_k, k_a, r_k, w2, a2, g2, seg)


def _wkv_kernel(*refs, C, n_sub, n_par):
    GW = WKV_GROUP * RWKV_N
    R = WKV_GROUP * C
    n_grp = RWKV_HEADS // WKV_GROUP
    n_lev = int(round(math.log2(C))) - 1
    c = pl.program_id(1)
    tok = [refs[6 * p:6 * p + 6] for p in range(n_par)]
    h0_ref, y_ref, hT_ref, h_scr = refs[6 * n_par:]
    n_state = n_par * n_grp

    def head_block(hh):
        return slice(hh * RWKV_N, (hh + 1) * RWKV_N)

    @pl.when(c == 0)
    def _():
        h_scr[...] = jnp.zeros(h_scr.shape, F32)
        for p in range(n_par):
            for hd in range(RWKV_HEADS):
                g, hh = divmod(hd, WKV_GROUP)
                h_scr[p * n_grp + g, head_block(hh), head_block(hh)] = h0_ref[p, hd]

    row = lax.broadcasted_iota(jnp.int32, (C, C), 0)
    col = lax.broadcasted_iota(jnp.int32, (C, C), 1)
    tri = jnp.where(col <= row, 1.0, 0.0).astype(BF16)
    rr = lax.broadcasted_iota(jnp.int32, (R, R), 0)
    cc = lax.broadcasted_iota(jnp.int32, (R, R), 1)
    same = (rr // C) == (cc // C)
    ti = lax.broadcasted_iota(jnp.int32, (C, R), 0)
    si = lax.broadcasted_iota(jnp.int32, (C, R), 1) % C
    strict4 = si < ti
    lower4 = si <= ti
    eye4 = jnp.where(si == ti, 1.0, 0.0)
    keep = (lax.broadcasted_iota(jnp.int32, (R, GW), 0) // C
            == lax.broadcasted_iota(jnp.int32, (R, GW), 1) // RWKV_N)
    gr = lax.broadcasted_iota(jnp.int32, (GW, GW), 0)
    gc = lax.broadcasted_iota(jnp.int32, (GW, GW), 1)
    eye_g = gr == gc
    same_head = (gr // RWKV_N) == (gc // RWKV_N)
    eye_g_bf = jnp.where(eye_g, 1.0, 0.0).astype(BF16)

    def rows4(x4):
        return jnp.concatenate([x4] * WKV_GROUP, axis=0)

    def stack(x4):
        return jnp.where(keep, rows4(x4), jnp.zeros((), x4.dtype))


    sls = [slice(g * GW, (g + 1) * GW) for g in range(n_grp)]
    J = [(ci, q) for ci in range(n_sub) for q in range(n_state)]
    ops, p_end = {}, {}
    for p, ci in [(p, ci) for p in range(n_par) for ci in range(n_sub)]:
        r_ref, lw_ref, k_ref, v_ref, a_ref, b_ref = tok[p]
        rows = slice(ci * C, (ci + 1) * C)
        lw = lw_ref[rows, :]
        cum = _dot_exact_lhs(tri, lw)
        cum_end = cum[C - 1:C, :]
        e_neg = jnp.exp(-cum)
        e_end = jnp.exp(cum_end - cum)
        b_in = b_ref[rows, :]
        k_in = k_ref[rows, :]
        full = ((a_ref[rows, :] * jnp.exp(cum - lw)).astype(BF16),
                (r_ref[rows, :] * jnp.exp(cum)).astype(BF16),
                (b_in * e_neg).astype(BF16), (k_in * e_neg).astype(BF16),
                (b_in * e_end).astype(BF16), (k_in * e_end).astype(BF16),
                v_ref[rows, :].astype(BF16))
        p_end[p, ci] = jnp.exp(cum_end)
        for g in range(n_grp):
            ops[ci, p * n_grp + g] = [t[:, sls[g]] for t in full]
    a4, r4, b4, k4, be4, ke4, v4 = [{j: ops[j][i] for j in J} for i in range(7)]
    v_s = {j: stack(v4[j]) for j in J}
    m = {j: _dot_nt(jnp.concatenate([a4[j], r4[j]], axis=0),
                    jnp.concatenate([stack(b4[j]), stack(k4[j])], axis=0)) for j in J}
    l4 = {j: jnp.where(strict4, m[j][:C, :R], 0.0) for j in J}
    a_ak = {j: jnp.where(strict4, m[j][:C, R:], 0.0).astype(BF16) for j in J}
    a_rb = {j: jnp.where(lower4, m[j][C:, :R], 0.0).astype(BF16) for j in J}
    a_rk = {j: jnp.where(lower4, m[j][C:, R:], 0.0).astype(BF16) for j in J}
    def block_diag(x4):
        return jnp.where(same, rows4(x4.astype(BF16)), jnp.zeros((), BF16))

    t4 = {j: eye4 + l4[j] for j in J}
    l_bd = {j: block_diag(l4[j]) for j in J}
    for _ in range(n_lev):
        l4 = {j: _dot(l4[j].astype(BF16), l_bd[j]) for j in J}
        l_bd = {j: block_diag(l4[j]) for j in J}
        t4 = {j: t4[j] + _dot(t4[j].astype(BF16), l_bd[j]) for j in J}
    t_b = {j: t4[j].astype(BF16) for j in J}
    bke_t = {j: _dot_nt(eye_g_bf, jnp.concatenate([be4[j], ke4[j]], axis=0)).astype(BF16) for j in J}

    G = range(n_state)
    h_cur = [h_scr[q] for q in G]
    for ci in range(n_sub):
        rows = slice(ci * C, (ci + 1) * C)
        h0_b = [h_cur[g].astype(BF16) for g in G]
        x4 = [_dot(a4[ci, g], h0_b[g]) + _dot(a_ak[ci, g], v_s[ci, g]) for g in G]
        u4 = [_dot(t_b[ci, g], stack(x4[g].astype(BF16))).astype(BF16) for g in G]
        y4 = [_dot(r4[ci, g], h0_b[g]) + _dot(a_rb[ci, g], stack(u4[g])) + _dot(a_rk[ci, g], v_s[ci, g])
              for g in G]
        uv4 = [jnp.concatenate([u4[g], v4[ci, g]], axis=0) for g in G]
        h_add = [jnp.where(same_head, _dot(bke_t[ci, g], uv4[g]), 0.0) for g in G]
        for q in G:
            p, g = divmod(q, n_grp)
            y_ref[p, rows, sls[g]] = y4[q]
            p_col = jnp.sum(jnp.where(eye_g, p_end[p, ci][:, sls[g]], 0.0), axis=1, keepdims=True)
            h_cur[q] = p_col * h_cur[q] + h_add[q]

    for q in G:
        h_scr[q] = h_cur[q]

    @pl.when(c == pl.num_programs(1) - 1)
    def _():
        for p in range(n_par):
            for hd in range(RWKV_HEADS):
                g, hh = divmod(hd, WKV_GROUP)
                hT_ref[p, hd] = h_scr[p * n_grp + g, head_block(hh), head_block(hh)]


def _wkv_call(arrs, h0, row0, n_seq, n_chunk, C, n_sub, n_par):
    rows = C * n_sub
    steps = n_chunk // n_sub
    blk0 = row0 // rows
    GW = WKV_GROUP * RWKV_N
    n_grp = RWKV_HEADS // WKV_GROUP
    assert n_seq % n_par == 0 and n_chunk % n_sub == 0
    tok = [pl.BlockSpec((rows, RWKV_DIM), lambda b, c, p=p: (blk0 + (b * n_par + p) * steps + c, 0))
           for p in range(n_par)]
    st = pl.BlockSpec((n_par, RWKV_HEADS, RWKV_N, RWKV_N), lambda b, c: (b, 0, 0, 0))
    y, h_fin = pl.pallas_call(
        functools.partial(_wkv_kernel, C=C, n_sub=n_sub, n_par=n_par),
        grid=(n_seq // n_par, steps),
        in_specs=[tok[p] for p in range(n_par) for _ in range(6)] + [st],
        out_specs=[pl.BlockSpec((n_par, rows, RWKV_DIM), lambda b, c: (b, c, 0)), st],
        out_shape=[
            jax.ShapeDtypeStruct((n_seq, n_chunk * C, RWKV_DIM), F32),
            jax.ShapeDtypeStruct((n_seq, RWKV_HEADS, RWKV_N, RWKV_N), F32),
        ],
        scratch_shapes=[pltpu.VMEM((n_par * n_grp, GW, GW), F32)],
        compiler_params=_cparams(("parallel", "arbitrary")),
        name="wkv_c%d" % C,
    )(*(list(arrs) * n_par), h0)
    return y.reshape(n_seq * n_chunk * C, RWKV_DIM), h_fin


def _mix_kernel(xp_ref, xs_ref, attnp_ref, attns_ref, yp_ref, ys_ref, bonus_ref, g_ref, seg_ref, lng_ref,
                lnb_ref, woa_ref, wob_ref, g1_ref, b1_ref, wr_ref, br_ref, tri_ref, h_ref, hpk_ref, route_ref,
                count_ref, *, n_p, tm):
    is_p = pl.program_id(0) < n_p
    P = range(MIX_PARTS)
    bands = [slice(k * tm // MIX_PARTS, (k + 1) * tm // MIX_PARTS) for k in P]
    seg = seg_ref[...]
    inv_n = 1.0 / RWKV_N
    y = [jnp.where(is_p, yp_ref[r, :], ys_ref[r, :]) for r in bands]
    yc = [y[k] - _dot_exact_rhs(y[k], seg) * inv_n for k in P]
    var = [_dot_exact_rhs(yc[k] * yc[k], seg) * inv_n for k in P]
    yn = [yc[k] * lax.rsqrt(var[k] + GN_EPS) * lng_ref[...] + lnb_ref[...] for k in P]
    rw = [((yn[k] + bonus_ref[bands[k], :]) * g_ref[bands[k], :]).astype(BF16) for k in P]
    attn = [jnp.where(is_p, attnp_ref[r, :], attns_ref[r, :]) for r in bands]
    m = [_dot(attn[k], woa_ref[...]) + _dot(rw[k], wob_ref[...]) for k in P]
    x = [jnp.where(is_p, xp_ref[r, :], xs_ref[r, :]) for r in bands]
    h = [_layer_norm(DN_ALPHA * x[k] + m[k], g1_ref[...], b1_ref[...]) for k in P]
    for k in P:
        h_ref[bands[k], :] = h[k]
        hpk_ref[bands[k], :] = _pack_bf16_pairs(h[k])

    h_hi = [h[k].astype(BF16) for k in P]
    h_lo = [(h[k] - h_hi[k].astype(F32)).astype(BF16) for k in P]
    logits = [_dot(h_hi[k], wr_ref[0]) + _dot(h_lo[k], wr_ref[0]) + _dot(h_hi[k], wr_ref[1]) + br_ref[...]
              for k in P]
    lane = lax.broadcasted_iota(jnp.int32, logits[0].shape, 1)
    big = jnp.int32(LANE)

    def route_band(lg):
        gl = jnp.where(lane < N_GROUPS, lg, NEG_INF)
        gmax = jnp.max(gl, axis=1, keepdims=True)
        grp = jnp.min(jnp.where(gl == gmax, lane, big), axis=1, keepdims=True)
        p_grp = 1.0 / jnp.sum(jnp.exp(gl - gmax), axis=1, keepdims=True)
        e_idx = lane - N_GROUPS
        in_grp = (lane >= N_GROUPS) & (lane < N_GROUPS + N_EXPERTS) & ((e_idx // EXPERTS_PER_GROUP) == grp)
        el = jnp.where(in_grp, lg, NEG_INF)
        m1 = jnp.max(el, axis=1, keepdims=True)
        i1 = jnp.min(jnp.where(el == m1, lane, big), axis=1, keepdims=True)
        el2 = jnp.where(lane == i1, NEG_INF, el)
        m2 = jnp.max(el2, axis=1, keepdims=True)
        i2 = jnp.min(jnp.where(el2 == m2, lane, big), axis=1, keepdims=True)
        t = jnp.exp(m2 - m1)
        g1 = p_grp / (1.0 + t)
        return i1, i2, g1, g1 * t

    routed = [route_band(logits[k]) for k in P]
    chosen = jnp.concatenate(
        [jnp.where(lane == routed[k][0], 1.0, jnp.where(lane == routed[k][1], 1.0, 0.0)) for k in P], axis=0)
    before = _dot(tri_ref[...], chosen.astype(BF16))
    count_ref[0] = jnp.sum(chosen, axis=0, keepdims=True)
    for k in P:
        i1, i2, g1, g2 = routed[k]
        bef = before[bands[k]]
        r1 = jnp.sum(jnp.where(lane == i1, bef, 0.0), axis=1, keepdims=True)
        r2 = jnp.sum(jnp.where(lane == i2, bef, 0.0), axis=1, keepdims=True)
        cols = ((i1 - N_GROUPS).astype(F32), (i2 - N_GROUPS).astype(F32), g1, g2, r1, r2)
        route = jnp.zeros(lane.shape, F32)
        for j, col in enumerate(cols):
            route = jnp.where(lane == j, col, route)
        route_ref[bands[k], :] = route


def _mix_call(xp, xs, attn_p, attn_s, y_p, y_s, bonus, g, seg, lnx_g, lnx_b, woa, wob, ln1_g, ln1_b, wr, br):
    T = xp.shape[0] + xs.shape[0]
    tm = TOKEN_TILE
    n_p = xp.shape[0] // tm
    row = lambda i: (i, 0)
    full = lambda i: (0, 0)
    row_p, row_s = _split_rows(n_p)
    half = pl.BlockSpec((tm, RWKV_DIM), row)
    vec5 = pl.BlockSpec((1, RWKV_DIM), full)
    vec10 = pl.BlockSpec((1, D_MODEL), full)
    idx = np.arange(tm)
    tri = jnp.asarray((idx[None, :] < idx[:, None]).astype(np.float32)).astype(BF16)
    return pl.pallas_call(
        functools.partial(_mix_kernel, n_p=n_p, tm=tm),
        grid=(T // tm,),
        in_specs=[
            pl.BlockSpec((tm, D_MODEL), row_p), pl.BlockSpec((tm, D_MODEL), row_s),
            pl.BlockSpec((tm, RWKV_DIM), row_p), pl.BlockSpec((tm, RWKV_DIM), row_s),
            pl.BlockSpec((tm, RWKV_DIM), row_p), pl.BlockSpec((tm, RWKV_DIM), row_s),
            half, half,
            pl.BlockSpec((RWKV_DIM, RWKV_DIM), full), vec5, vec5,
            pl.BlockSpec((RWKV_DIM, D_MODEL), full), pl.BlockSpec((RWKV_DIM, D_MODEL), full),
            vec10, vec10,
            pl.BlockSpec((2, D_MODEL, LANE), lambda i: (0, 0, 0)), pl.BlockSpec((1, LANE), full),
            pl.BlockSpec((tm, tm), full),
        ],
        out_specs=[pl.BlockSpec((tm, D_MODEL), row), pl.BlockSpec((tm, D_MODEL // 2), row),
                   pl.BlockSpec((tm, LANE), row), pl.BlockSpec((1, 1, LANE), lambda i: (i, 0, 0))],
        out_shape=[jax.ShapeDtypeStruct((T, D_MODEL), F32), jax.ShapeDtypeStruct((T, D_MODEL // 2), jnp.int32),
                   jax.ShapeDtypeStruct((T, LANE), F32), jax.ShapeDtypeStruct((T // tm, 1, LANE), F32)],
        compiler_params=_cparams(("parallel",)),
        name="mix",
    )(xp, xs, attn_p, attn_s, y_p, y_s, bonus, g, seg, lnx_g, lnx_b, woa, wob, ln1_g, ln1_b, wr, br, tri)


def _expert_kernel(be_ref, nu_ref, xs_ref, wg_ref, wu_ref, wd_ref, ys_ref, wgu_b, wd_b):
    i = pl.program_id(0)

    @pl.when((i == 0) | (be_ref[i] != be_ref[jnp.maximum(i - 1, 0)]))
    def _():
        wgu_b[:, :D_EXPERT] = wg_ref[0].astype(BF16)
        wgu_b[:, D_EXPERT:] = wu_ref[0].astype(BF16)
        wd_b[...] = wd_ref[0].astype(BF16)

    @pl.when(i < nu_ref[0])
    def _():
        P = range(2)
        bands = [slice(k * MOE_BLK // 2, (k + 1) * MOE_BLK // 2) for k in P]
        xb = [_unpack_bf16_pairs(xs_ref[r, :]).astype(BF16) for r in bands]
        gu = [_dot(xb[k], wgu_b[...]) for k in P]
        act = [(gu[k][:, :D_EXPERT] * _sigmoid(gu[k][:, :D_EXPERT]) * gu[k][:, D_EXPERT:]).astype(BF16)
               for k in P]
        out = [_dot(act[k], wd_b[...]) for k in P]
        for k in P:
            ys_ref[bands[k], :] = _pack_bf16_pairs(out[k])

    @pl.when(i >= nu_ref[0])
    def _():
        ys_ref[...] = jnp.zeros(ys_ref.shape, ys_ref.dtype)


def _expert_call(block_e, n_used, xs, wg, wu, wd):
    n_blk = xs.shape[0] // MOE_BLK
    grid_spec = pltpu.PrefetchScalarGridSpec(
        num_scalar_prefetch=2,
        grid=(n_blk,),
        in_specs=[
            pl.BlockSpec((MOE_BLK, D_MODEL // 2), lambda i, be, nu: (i, 0)),
            pl.BlockSpec((1, D_MODEL, D_EXPERT), lambda i, be, nu: (be[i], 0, 0)),
            pl.BlockSpec((1, D_MODEL, D_EXPERT), lambda i, be, nu: (be[i], 0, 0)),
            pl.BlockSpec((1, D_EXPERT, D_MODEL), lambda i, be, nu: (be[i], 0, 0)),
        ],
        out_specs=pl.BlockSpec((MOE_BLK, D_MODEL // 2), lambda i, be, nu: (i, 0)),
        scratch_shapes=[pltpu.VMEM((D_MODEL, 2 * D_EXPERT), BF16), pltpu.VMEM((D_EXPERT, D_MODEL), BF16)],
    )
    return pl.pallas_call(
        _expert_kernel,
        grid_spec=grid_spec,
        out_shape=jax.ShapeDtypeStruct((n_blk * MOE_BLK, D_MODEL // 2), jnp.int32),
        compiler_params=_cparams(("arbitrary",)),
        name="experts",
    )(block_e, n_used, xs, wg, wu, wd)


def _combine_kernel(h_ref, ya_ref, yb_ref, route_ref, g2_ref, b2_ref, op_ref, os_ref, *, n_p):
    i = pl.program_id(0)
    route = route_ref[...]
    f = _unpack_bf16_pairs(ya_ref[...]) * route[:, 2:3] + _unpack_bf16_pairs(yb_ref[...]) * route[:, 3:4]
    out = _layer_norm(DN_ALPHA * h_ref[...] + f, g2_ref[...], b2_ref[...])

    @pl.when(i < n_p)
    def _():
        op_ref[...] = out

    @pl.when(i >= n_p)
    def _():
        os_ref[...] = out


def _combine_call(h, yab, route, ln2_g, ln2_b, t_prompt):
    T = h.shape[0]
    tm = TOKEN_TILE
    n_t, n_p = T // tm, t_prompt // tm
    row = lambda i: (i, 0)
    full = lambda i: (0, 0)
    row_p, row_s = _split_rows(n_p)
    big = pl.BlockSpec((tm, D_MODEL), row)
    return pl.pallas_call(
        functools.partial(_combine_kernel, n_p=n_p),
        grid=(n_t,),
        in_specs=[big, pl.BlockSpec((tm, D_MODEL // 2), row),
                  pl.BlockSpec((tm, D_MODEL // 2), lambda i: (i + n_t, 0)), pl.BlockSpec((tm, LANE), row),
                  pl.BlockSpec((1, D_MODEL), full), pl.BlockSpec((1, D_MODEL), full)],
        out_specs=[pl.BlockSpec((tm, D_MODEL), row_p), pl.BlockSpec((tm, D_MODEL), row_s)],
        out_shape=[jax.ShapeDtypeStruct((t_prompt, D_MODEL), F32),
                   jax.ShapeDtypeStruct((T - t_prompt, D_MODEL), F32)],
        compiler_params=_cparams(("arbitrary",)),
        name="combine",
    )(h, yab, yab, route, ln2_g, ln2_b)


def _prep_weights(w_in, w_uq, w_ukv):
    half = MLA_ROPE // 2
    kpe_w = w_in[:, Q_LORA + KV_LORA:MLA_IN]
    kpe_b = jnp.concatenate([-kpe_w[:, half:], kpe_w[:, :half]], axis=1)
    w1 = jnp.concatenate([w_in[:, :Q_LORA + KV_LORA], kpe_w, kpe_b,
                          jnp.zeros((D_MODEL, 64), F32), w_in[:, MLA_IN:]], axis=1).astype(BF16)
    pad_q = jnp.zeros((Q_LORA, MLA_HEADS, HEAD_PAD - MLA_NOPE - MLA_ROPE), F32)
    wqa = jnp.concatenate([w_uq, pad_q], axis=2).reshape(Q_LORA, -1).astype(BF16)
    w_uk, w_uv = w_ukv[:, :, :MLA_NOPE], w_ukv[:, :, MLA_NOPE:]
    wk = jnp.concatenate([w_uk, jnp.zeros((KV_LORA, MLA_HEADS, HEAD_PAD - MLA_NOPE), F32)], axis=2)
    wk = wk.reshape(KV_LORA, -1).astype(BF16)
    pk_np = np.zeros((MLA_ROPE, MLA_HEADS * HEAD_PAD), np.float32)
    for h in range(MLA_HEADS):
        for i in range(MLA_ROPE):
            pk_np[i, h * HEAD_PAD + MLA_NOPE + i] = 1.0
    pk = jnp.asarray(pk_np).astype(BF16)
    zv = jnp.zeros((KV_LORA, MLA_HEADS // 2, MLA_V), F32)
    wv4 = w_uv.reshape(KV_LORA, MLA_HEADS // 2, 2, MLA_V)
    wv = jnp.stack([jnp.concatenate([wv4[:, :, 0], zv], axis=2),
                    jnp.concatenate([zv, wv4[:, :, 1]], axis=2)], axis=2)
    wv = wv.reshape(KV_LORA, -1).astype(BF16)
    wuk = jnp.transpose(w_uk, (1, 2, 0)).astype(BF16)
    wuv_np = np.zeros((MLA_HEADS, MLA_HEADS * MLA_V), np.float32)
    for h in range(MLA_HEADS):
        wuv_np[h, h * MLA_V:(h + 1) * MLA_V] = 1.0
    wuv = jnp.transpose(w_uv, (1, 0, 2))
    wuv = (jnp.tile(wuv, (1, 1, MLA_HEADS)) * jnp.asarray(wuv_np)[:, None, :]).astype(BF16)
    return w1, wqa, wk, pk, wv, wuk, wuv


def _rope_table(pos):
    inv = ROPE_BASE ** (-jnp.arange(0, MLA_ROPE, 2, dtype=F32) / MLA_ROPE)
    ang = pos.astype(F32)[:, None] * inv[None, :]
    return jnp.concatenate([jnp.cos(ang), jnp.sin(ang)], axis=1)


def _rope_placement():
    half = MLA_ROPE // 2
    place = np.zeros((MLA_ROPE, 4 * LANE), np.float32)
    rows = np.zeros((2, 4 * LANE), np.float32)
    for i in range(half):
        c, s = i, half + i
        place[c, MLA_NOPE + i] = place[c, MLA_NOPE + half + i] = 1.0
        place[s, LANE + MLA_NOPE + i] = -1.0
        place[s, 2 * LANE + MLA_NOPE + half + i] = 1.0
        place[c, 3 * LANE + i] = place[c, 3 * LANE + half + i] = 1.0
        place[s, 3 * LANE + 2 * half + i] = place[s, 3 * LANE + 3 * half + i] = 1.0
    rows[0, :MLA_NOPE] = 1.0
    rows[1, :3 * LANE] = MLA_SCALE * LOG2E
    rows[1, 3 * LANE:] = 1.0
    return jnp.asarray(place).astype(BF16), jnp.asarray(rows)


def _seg_ones():
    idx = np.arange(RWKV_DIM) // RWKV_N
    return jnp.asarray((idx[:, None] == idx[None, :]).astype(np.float32)).astype(BF16)


def _dispatch(route, tile_counts, t_total):
    A = t_total * TOP_K
    n_tiles = tile_counts.shape[0]
    counts_te = tile_counts[:, 0, N_GROUPS:N_GROUPS + N_EXPERTS].astype(jnp.int32)
    counts = jnp.sum(counts_te, axis=0)
    blocks_per_e = (counts + MOE_BLK - 1) // MOE_BLK
    blk_end = jnp.cumsum(blocks_per_e)
    blk_start = blk_end - blocks_per_e
    tile_off = jnp.cumsum(counts_te, axis=0) - counts_te
    base = blk_start[None, :] * MOE_BLK + tile_off
    e = route[:, :TOP_K].astype(jnp.int32)
    rank = route[:, 4:4 + TOP_K].astype(jnp.int32)
    base_tok = jnp.repeat(base, t_total // n_tiles, axis=0)
    pick = e[:, :, None] == jnp.arange(N_EXPERTS, dtype=jnp.int32)[None, None, :]
    dest = jnp.sum(jnp.where(pick, base_tok[:, None, :], 0), axis=-1) + rank
    n_blk = -(-A // MOE_BLK) + N_EXPERTS
    blk = jnp.arange(n_blk, dtype=jnp.int32)
    block_e = jnp.minimum(jnp.sum((blk[:, None] >= blk_end[None, :]).astype(jnp.int32), axis=1),
                          N_EXPERTS - 1).astype(jnp.int32)
    n_used = blk_end[-1:].astype(jnp.int32)
    return dest, block_e, n_used


def _sc_gather_rows(table, idx):
    n_rows, width = idx.shape[0], table.shape[1]
    n_workers = SC_CORES * SC_SUBCORES
    per_worker = n_rows // n_workers
    assert n_rows % n_workers == 0 and per_worker % SC_WINDOW == 0
    mesh = plsc.VectorSubcoreMesh(core_axis_name="c", subcore_axis_name="s")

    @functools.partial(
        pl.kernel, mesh=mesh,
        out_type=jax.ShapeDtypeStruct((n_rows, width), table.dtype),
        scratch_types=[
            pltpu.VMEM((SC_WINDOW,), jnp.int32),
            pltpu.VMEM((SC_WINDOW, width), table.dtype),
            pltpu.SemaphoreType.DMA,
        ],
    )
    def gather(table_hbm, idx_hbm, out_hbm, idx_v, rows_v, sem):
        wid = lax.axis_index("s") * SC_CORES + lax.axis_index("c")
        base = wid * per_worker

        @pl.loop(0, per_worker // SC_WINDOW)
        def _(w):
            off = pl.multiple_of(base + w * SC_WINDOW, SC_WINDOW)
            pltpu.sync_copy(idx_hbm.at[pl.ds(off, SC_WINDOW)], idx_v)
            pltpu.async_copy(table_hbm.at[idx_v], rows_v, sem).wait()
            pltpu.sync_copy(rows_v, out_hbm.at[pl.ds(off, SC_WINDOW)])

    return gather(table, idx)


def _sc_scatter_rows(src, idx_a, idx_b, n_slots):
    n_rows, width = src.shape
    n_workers = SC_CORES * SC_SUBCORES
    per_worker = n_rows // n_workers
    assert n_rows % n_workers == 0 and per_worker % SC_WINDOW == 0
    n_win = per_worker // SC_WINDOW
    mesh = plsc.VectorSubcoreMesh(core_axis_name="c", subcore_axis_name="s")

    @functools.partial(
        pl.kernel, mesh=mesh,
        out_type=jax.ShapeDtypeStruct((n_slots, width), src.dtype),
        scratch_types=[
            pltpu.VMEM((1, SC_WINDOW), jnp.int32),
            pltpu.VMEM((1, SC_WINDOW), jnp.int32),
            pltpu.VMEM((SC_WINDOW, width), src.dtype),
        ],
    )
    def scatter(src_hbm, ia_hbm, ib_hbm, out_hbm, ia_v, ib_v, rows_v):
        wid = lax.axis_index("s") * SC_CORES + lax.axis_index("c")

        @pl.loop(0, n_win)
        def _(w):
            win = wid * n_win + w
            off = pl.multiple_of(win * SC_WINDOW, SC_WINDOW)
            pltpu.sync_copy(src_hbm.at[pl.ds(off, SC_WINDOW)], rows_v)
            pltpu.sync_copy(ia_hbm.at[pl.ds(win, 1)], ia_v)
            pltpu.sync_copy(ib_hbm.at[pl.ds(win, 1)], ib_v)
            pltpu.sync_copy(rows_v, out_hbm.at[ia_v.at[0]])
            pltpu.sync_copy(rows_v, out_hbm.at[ib_v.at[0]])

    return scatter(src, idx_a, idx_b)


def kernel(x_prompt, x_sample, cache_ckv, cache_kpe, state_wkv, state_shift, w_in, q_norm_g, kv_norm_g, w_uq,
           w_ukv, mu_shift, w0, w2, a0, a2, g2, k_k, k_a, r_k, lnx_g, lnx_b, w_o, ln1_g, ln1_b, w_gr, b_gr,
           w_er, b_er, w_eg, w_eu, w_ed, ln2_g, ln2_b):
    B, S, D = x_prompt.shape
    DB, DS, _ = x_sample.shape
    past = cache_ckv.shape[2]
    Tp, Ts = B * S, DB * DS
    T = Tp + Ts
    assert D == D_MODEL and DS == SHIFT_GROUP and S % ATTN_TQ == 0 and S % (CHUNK * WKV_SUB) == 0
    assert Tp % TOKEN_TILE == 0 and T % TOKEN_TILE == 0 and w_in.shape[0] == DEPTH

    l = 0
    xp, xs_in = x_prompt.reshape(Tp, D), x_sample.reshape(Ts, D)
    w1, wqa, wk, pk, wv, wuk, wuv = _prep_weights(w_in[l], w_uq[l], w_ukv[l])
    pos = jnp.concatenate([jnp.arange(S, dtype=jnp.int32),
                           jnp.tile(past + jnp.arange(DS, dtype=jnp.int32), TOKEN_TILE // DS)])
    rope = _rope_table(pos)

    q, kcat, vcat, ckv_p, ckv_s, kpe_p, kpe_s, pr, last_rows = _proj_call(
        xp, xs_in, rope, S // TOKEN_TILE, *_rope_placement(), w1, q_norm_g[l][None], kv_norm_g[l][None],
        wqa, wk, pk, wv)

    attn_p = _attn_call(q, kcat, vcat, B, S)
    attn_s = _mla_sample_call(q, ckv_s, kpe_s, cache_ckv[l], cache_kpe[l], wuk, wuv, Tp, DB, DS)

    bnd = jnp.concatenate([jnp.zeros((1, RWKV_IN), F32), last_rows[:-1]], axis=0)
    gidx = jnp.arange(T // SHIFT_GROUP)
    seq_start = (gidx < Tp // SHIFT_GROUP) & (gidx % (S // SHIFT_GROUP) == 0)
    bnd = jnp.where(seq_start[:, None], 0.0, bnd)
    bnd = jnp.concatenate([bnd[:Tp // SHIFT_GROUP], state_shift[l]], axis=0)[:, None, :]

    seg = _seg_ones()
    vec = lambda a: a.reshape(1, -1)
    r, lw, kh, v, na, b, bonus, g = _prep_call(
        pr, bnd, vec(mu_shift[l]), vec(w0[l]), vec(a0[l]), vec(k_k[l]), vec(k_a[l]), vec(r_k[l]),
        w2[l].astype(BF16), a2[l].astype(BF16), g2[l].astype(BF16), seg)

    scan_in = (r, lw, kh, v, na, b)
    h0_p = jnp.zeros((B, RWKV_HEADS, RWKV_N, RWKV_N), F32)
    y_p, hT_p = _wkv_call(scan_in, h0_p, 0, B, S // CHUNK, CHUNK, WKV_SUB, math.gcd(B, WKV_PAR))
    h0_s = jnp.swapaxes(state_wkv[l], -1, -2)
    y_s, hT_s = _wkv_call(scan_in, h0_s, Tp, DB, 1, DS, 1, math.gcd(DB, WKV_SUB * WKV_PAR))

    wo_b = w_o[l].astype(BF16)
    wr = jnp.concatenate([w_gr[l], w_er[l], jnp.zeros((D, LANE - N_GROUPS - N_EXPERTS), F32)], axis=1)
    wr_hi = wr.astype(BF16)
    wr_lo = (wr - wr_hi.astype(F32)).astype(BF16)
    br = jnp.concatenate([b_gr[l], b_er[l], jnp.zeros((LANE - N_GROUPS - N_EXPERTS,), F32)])[None]
    h, hpk, route, tile_counts = _mix_call(
        xp, xs_in, attn_p, attn_s, y_p, y_s, bonus, g, seg, vec(lnx_g[l]), vec(lnx_b[l]),
        wo_b[:MLA_HEADS * MLA_V], wo_b[MLA_HEADS * MLA_V:], vec(ln1_g[l]), vec(ln1_b[l]),
        jnp.stack([wr_hi, wr_lo]), br)

    dest, block_e, n_used = _dispatch(route, tile_counts, T)
    win = lambda a: a.reshape(T // SC_WINDOW, SC_WINDOW)
    xs = _sc_scatter_rows(hpk, win(dest[:, 0]), win(dest[:, 1]), block_e.shape[0] * MOE_BLK)
    ys = _expert_call(block_e, n_used, xs, w_eg[l], w_eu[l], w_ed[l])
    yab = _sc_gather_rows(ys, jnp.concatenate([dest[:, 0], dest[:, 1]]))
    out_p, out_s = _combine_call(h, yab, route, vec(ln2_g[l]), vec(ln2_b[l]), Tp)

    y_prompt = out_p.reshape(B, S, D)
    y_sample = out_s.reshape(DB, DS, D)
    p_ckv = ckv_p.reshape(1, B, S, KV_LORA)
    p_kpe = kpe_p.reshape(1, B, S, MLA_ROPE)
    s_ckv = ckv_s.reshape(1, DB, DS, KV_LORA)
    s_kpe = kpe_s.reshape(1, DB, DS, MLA_ROPE)
    p_wkv = jnp.swapaxes(hT_p, -1, -2)[None]
    s_wkv = jnp.swapaxes(hT_s, -1, -2)[None]
    gp = S // SHIFT_GROUP
    p_sh = last_rows[gp - 1:B * gp:gp][None]
    s_sh = last_rows[B * gp:][None]
    return (y_prompt, y_sample, p_ckv, p_kpe, p_wkv, p_sh, s_ckv, s_kpe, s_wkv, s_sh)
```

```python
import functools
import math

import numpy as np
import jax
import jax.numpy as jnp
from jax import lax
from jax.experimental import pallas as pl
from jax.experimental.pallas import tpu as pltpu
from jax.experimental.pallas import tpu_sc as plsc

F32 = jnp.float32
BF16 = jnp.bfloat16

D_MODEL = 1024
CHUNK = 64
MLA_HEADS = 8
MLA_NOPE = 64
MLA_ROPE = 32
MLA_V = 64
Q_LORA = 384
KV_LORA = 256
ROPE_BASE = 10000.0
MLA_IN = Q_LORA + KV_LORA + MLA_ROPE
MLA_SCALE = (MLA_NOPE + MLA_ROPE) ** -0.5
RWKV_HEADS = 8
RWKV_N = 64
RWKV_DIM = RWKV_HEADS * RWKV_N
DECAY_LORA = 64
AAA_LORA = 64
GATE_LORA = 128
RWKV_IN = 3 * RWKV_DIM + DECAY_LORA + AAA_LORA + GATE_LORA
N_GROUPS = 4
EXPERTS_PER_GROUP = 8
N_EXPERTS = N_GROUPS * EXPERTS_PER_GROUP
TOP_K = 2
D_EXPERT = 256
MOE_BLK = 512
LN_EPS = 1e-5
RMS_EPS = 1e-6
GN_EPS = 64e-5
NEG_INF = -1e30
DEPTH = 1
DN_ALPHA = (2 * DEPTH) ** 0.25

LANE = 128
HEAD_PAD = 128
PROJ_W = 768 + RWKV_IN
SHIFT_GROUP = 32
TOKEN_TILE = 512
ATTN_TQ = 1024
ATTN_TK = 512
V_ONE_LANE = (MLA_V, 0)
LOG2E = math.log2(math.e)
VMEM_LIMIT = 48 * 1024 * 1024
PROJ_VMEM_LIMIT = 56 * 1024 * 1024
SC_CORES = 2
SC_SUBCORES = 16
SC_WINDOW = 32
WKV_GROUP = 4
WKV_SUB = 2
WKV_PAR = 2
MIX_PARTS = 2


def _cparams(sem):
    return pltpu.CompilerParams(dimension_semantics=sem, vmem_limit_bytes=VMEM_LIMIT)


def _split3(x):
    hi = x.astype(BF16)
    r1 = x - hi.astype(F32)
    mid = r1.astype(BF16)
    lo = (r1 - mid.astype(F32)).astype(BF16)
    return hi, mid, lo


def _dot(a, b):
    return jnp.dot(a, b, preferred_element_type=F32)


def _dot_nt(a, b):
    return lax.dot_general(a, b, (((1,), (1,)), ((), ())), preferred_element_type=F32)


def _dot_exact_rhs(x, w):
    hi = x.astype(BF16)
    lo = (x - hi.astype(F32)).astype(BF16)
    return _dot(hi, w) + _dot(lo, w)


def _dot_exact_lhs(w, x):
    hi, mid, lo = _split3(x)
    return _dot(w, hi) + _dot(w, mid) + _dot(w, lo)


def _pack_bf16_pairs(x):
    n = x.shape[1] // 2
    bits = pltpu.bitcast(x.astype(BF16).astype(F32), jnp.int32)
    return (bits[:, :n] & jnp.int32(-65536)) | lax.shift_right_logical(bits[:, n:], jnp.int32(16))


def _unpack_bf16_pairs(p):
    hi = pltpu.bitcast(p & jnp.int32(-65536), F32)
    lo = pltpu.bitcast(lax.shift_left(p, jnp.int32(16)), F32)
    return jnp.concatenate([hi, lo], axis=1)


def _sigmoid(x):
    return 1.0 / (1.0 + jnp.exp(-x))


def _layer_norm(x, g, b):
    xc = x - jnp.mean(x, -1, keepdims=True)
    var = jnp.mean(xc * xc, -1, keepdims=True)
    return xc * lax.rsqrt(var + LN_EPS) * g + b


def _proj_kernel(xp_ref, xs_ref, rope_ref, rp_ref, rc_ref, w1_ref, gq_ref, gkv_ref, wqa_ref, wk_ref, pk_ref, wv_ref,
                 shift_ref, *rest, n_p, tm, seq_tiles):
    rwkv_w = rest[:10]
    q_ref, k_ref, v_ref, ckvp_ref, ckvs_ref, kpep_ref, kpes_ref, last_ref = rest[10:18]
    rwkv_out = rest[18:26]
    carry_scr = rest[26]
    i = pl.program_id(0)
    is_p = i < n_p
    x = jnp.where(is_p, xp_ref[...], xs_ref[...]).astype(BF16)
    proj = _dot(x, w1_ref[...])
    c_q = proj[:, :Q_LORA]
    c_kv = proj[:, Q_LORA:Q_LORA + KV_LORA]
    kp = proj[:, 640:768]
    pr = proj[:, 768:]

    ng = tm // SHIFT_GROUP
    pr3 = pr.reshape(ng, SHIFT_GROUP, RWKV_IN)
    last_ref[...] = pr3[:, SHIFT_GROUP - 1, :]
    rolled = pltpu.roll(pr, 1, 0).reshape(ng, SHIFT_GROUP, RWKV_IN)
    first_row = jnp.where(i % seq_tiles == 0, 0.0, carry_scr[...])
    bound = jnp.where(is_p, first_row[None], shift_ref[...])
    row_in_grp = lax.broadcasted_iota(jnp.int32, pr3.shape, 1)
    grp = lax.broadcasted_iota(jnp.int32, pr3.shape, 0)
    use_bound = (row_in_grp == 0) & (grp <= jnp.where(is_p, 0, ng))
    prev = jnp.where(use_bound, bound, rolled).reshape(tm, RWKV_IN)
    carry_scr[...] = pr[tm - 1:tm, :]

    cqn = c_q * lax.rsqrt(jnp.mean(c_q * c_q, -1, keepdims=True) + RMS_EPS) * gq_ref[...]
    ckv = c_kv * lax.rsqrt(jnp.mean(c_kv * c_kv, -1, keepdims=True) + RMS_EPS) * gkv_ref[...]

    rope = (_dot_exact_rhs(rope_ref[...], rp_ref[...]) + rc_ref[0:1, :]) * rc_ref[1:2, :]
    cq = rope[:, :LANE]
    sq_up = rope[:, LANE:2 * LANE]
    sq_dn = rope[:, 2 * LANE:3 * LANE]
    kt = rope[:, 3 * LANE:]
    prod = kp * kt
    kpe = prod[:, :MLA_ROPE] + prod[:, MLA_ROPE:2 * MLA_ROPE]

    @pl.when(is_p)
    def _():
        ckvp_ref[...] = ckv
        kpep_ref[...] = kpe

    @pl.when(jnp.logical_not(is_p))
    def _():
        ckvs_ref[...] = ckv
        kpes_ref[...] = kpe

    qa = _dot(cqn.astype(BF16), wqa_ref[...])
    half = MLA_ROPE // 2
    for h in range(MLA_HEADS):
        qh = qa[:, h * HEAD_PAD:(h + 1) * HEAD_PAD]
        rot = pltpu.roll(qh, HEAD_PAD - half, 1) * sq_up + pltpu.roll(qh, half, 1) * sq_dn
        q_ref[:, h * HEAD_PAD:(h + 1) * HEAD_PAD] = (qh * cq + rot).astype(BF16)

    ckv_b = ckv.astype(BF16)
    k = _dot(ckv_b, wk_ref[...]) + _dot(kpe.astype(BF16), pk_ref[...])
    k_ref[...] = k.astype(BF16)
    lane = lax.broadcasted_iota(jnp.int32, (1, MLA_HEADS * HEAD_PAD), 1)
    odd = (lane // HEAD_PAD) % 2
    one_lane = jnp.where(odd == 1, V_ONE_LANE[1], V_ONE_LANE[0])
    v_one = jnp.where(lane % HEAD_PAD == one_lane, 1.0, 0.0)
    v_ref[...] = (_dot(ckv_b, wv_ref[...]) + v_one).astype(BF16)
    _rwkv_heads(pr, prev, *rwkv_w, *rwkv_out)


def _split_rows(n_p):
    return (lambda i: (jnp.minimum(i, n_p - 1), 0)), (lambda i: (jnp.maximum(i - n_p, 0), 0))


def _proj_call(xp, xs, rope, seq_tiles, rope_place, rope_rows, w1, gq, gkv, wqa, wk, pk, wv, shift, rwkv_w):
    Tp, Ts = xp.shape[0], xs.shape[0]
    T = Tp + Ts
    tm = TOKEN_TILE
    n_p = Tp // tm
    rope_tiles = seq_tiles
    row = lambda i: (i, 0)
    full = lambda i: (0, 0)
    row_p, row_s = _split_rows(n_p)
    wide = MLA_HEADS * HEAD_PAD
    ng = tm // SHIFT_GROUP
    once = pl.Buffered(1)
    vec = pl.BlockSpec((1, RWKV_DIM), full, pipeline_mode=once)
    rwkv_specs = [pl.BlockSpec((1, RWKV_IN), full, pipeline_mode=once), vec, vec, vec, vec, vec,
                  pl.BlockSpec((DECAY_LORA, RWKV_DIM), full, pipeline_mode=once),
                  pl.BlockSpec((AAA_LORA, RWKV_DIM), full, pipeline_mode=once),
                  pl.BlockSpec((GATE_LORA, RWKV_DIM), full, pipeline_mode=once),
                  pl.BlockSpec((RWKV_DIM, RWKV_DIM), full, pipeline_mode=once)]
    tok = pl.BlockSpec((tm, RWKV_DIM), row)
    return pl.pallas_call(
        functools.partial(_proj_kernel, n_p=n_p, tm=tm, seq_tiles=seq_tiles),
        grid=(T // tm,),
        in_specs=[
            pl.BlockSpec((tm, D_MODEL), row_p),
            pl.BlockSpec((tm, D_MODEL), row_s),
            pl.BlockSpec((tm, MLA_ROPE), lambda i: (jnp.where(i < n_p, i % rope_tiles, rope_tiles), 0)),
            pl.BlockSpec((MLA_ROPE, 4 * LANE), full, pipeline_mode=once),
            pl.BlockSpec((2, 4 * LANE), full, pipeline_mode=once),
            pl.BlockSpec((D_MODEL, PROJ_W), full, pipeline_mode=once),
            pl.BlockSpec((1, Q_LORA), full, pipeline_mode=once),
            pl.BlockSpec((1, KV_LORA), full, pipeline_mode=once),
            pl.BlockSpec((Q_LORA, wide), full, pipeline_mode=once),
            pl.BlockSpec((KV_LORA, wide), full, pipeline_mode=once),
            pl.BlockSpec((MLA_ROPE, wide), full, pipeline_mode=once),
            pl.BlockSpec((KV_LORA, wide), full, pipeline_mode=once),
            pl.BlockSpec((ng, 1, RWKV_IN), lambda i: (jnp.maximum(i - n_p, 0), 0, 0)),
        ] + rwkv_specs,
        out_specs=[
            pl.BlockSpec((tm, wide), row),
            pl.BlockSpec((tm, wide), row),
            pl.BlockSpec((tm, wide), row),
            pl.BlockSpec((tm, KV_LORA), row_p),
            pl.BlockSpec((tm, KV_LORA), row_s),
            pl.BlockSpec((tm, MLA_ROPE), row_p),
            pl.BlockSpec((tm, MLA_ROPE), row_s),
            pl.BlockSpec((ng, RWKV_IN), row),
        ] + [tok] * 8,
        out_shape=[
            jax.ShapeDtypeStruct((T, wide), BF16),
            jax.ShapeDtypeStruct((T, wide), BF16),
            jax.ShapeDtypeStruct((T, wide), BF16),
            jax.ShapeDtypeStruct((Tp, KV_LORA), F32),
            jax.ShapeDtypeStruct((Ts, KV_LORA), F32),
            jax.ShapeDtypeStruct((Tp, MLA_ROPE), F32),
            jax.ShapeDtypeStruct((Ts, MLA_ROPE), F32),
            jax.ShapeDtypeStruct((T // SHIFT_GROUP, RWKV_IN), F32),
        ] + [jax.ShapeDtypeStruct((T, RWKV_DIM), F32)] * 8,
        scratch_shapes=[pltpu.VMEM((1, RWKV_IN), F32)],
        compiler_params=pltpu.CompilerParams(dimension_semantics=("arbitrary",), vmem_limit_bytes=PROJ_VMEM_LIMIT),
        name="proj",
    )(xp, xs, rope, rope_place, rope_rows, w1, gq, gkv, wqa, wk, pk, wv, shift, *rwkv_w)


def _attn_kernel(q_ref, k_ref, v_ref, o_ref, m_scr, acc_scr, *, tq, tk):
    qi = pl.program_id(2)
    m_scr[...] = jnp.full(m_scr.shape, NEG_INF, F32)
    acc_scr[...] = jnp.zeros(acc_scr.shape, F32)
    n_diag = tq // tk

    H = range(2)
    sls = [slice(h * HEAD_PAD, (h + 1) * HEAD_PAD) for h in H]

    def scores(k0, width, rows):
        return [_dot_nt(q_ref[rows, sl], k_ref[pl.ds(k0, width), sl]) for sl in sls]

    def accumulate(s, k0, width, rows):
        m_prev = [m_scr[h, rows, :] for h in H]
        m_new = [jnp.maximum(m_prev[h], jnp.max(s[h], axis=1, keepdims=True)) for h in H]
        pexp = [jnp.exp2(s[h] - jnp.tile(m_new[h], (1, width // LANE))).astype(BF16) for h in H]
        pv = [_dot(pexp[h], v_ref[pl.ds(k0, width), sls[h]]) for h in H]
        for h in H:
            acc_scr[h, rows, :] = jnp.exp2(m_prev[h] - m_new[h]) * acc_scr[h, rows, :] + pv[h]
            m_scr[h, rows, :] = m_new[h]

    def kv_block(k0, width, rows, masked):
        s = scores(k0, width, rows)
        if masked:
            n_rows = rows.stop - rows.start
            r = lax.broadcasted_iota(jnp.int32, (n_rows, width), 0) // CHUNK
            c = lax.broadcasted_iota(jnp.int32, (n_rows, width), 1) // CHUNK
            s = [jnp.where(c <= r, s[h], NEG_INF) for h in H]
        accumulate(s, k0, width, rows)

    all_rows = slice(0, tq)

    def pair(t, carry):
        k0a = pl.multiple_of(2 * t * tq, tq)
        k0b = pl.multiple_of(k0a + tq, tq)
        s_a = scores(k0a, tq, all_rows)
        s_b = scores(k0b, tq, all_rows)
        accumulate(s_a, k0a, tq, all_rows)
        accumulate(s_b, k0b, tq, all_rows)
        return carry

    lax.fori_loop(0, qi // 2, pair, 0)

    @pl.when(qi % 2 == 1)
    def _():
        kv_block(pl.multiple_of((qi - 1) * tq, tq), tq, all_rows, False)

    for d in range(n_diag):
        k0 = pl.multiple_of(qi * tq + d * tk, tk)
        kv_block(k0, tk, slice(d * tk, (d + 1) * tk), True)
        if (d + 1) * tk < tq:
            kv_block(k0, tk, slice((d + 1) * tk, tq), False)
    acc0, acc1 = acc_scr[0], acc_scr[1]
    lane = lax.broadcasted_iota(jnp.int32, acc0.shape, 1)
    l0 = acc0[:, V_ONE_LANE[0]:V_ONE_LANE[0] + 1]
    l1 = acc1[:, V_ONE_LANE[1]:V_ONE_LANE[1] + 1]
    o_ref[...] = jnp.where(lane < MLA_V, acc0 / l0, acc1 / l1).astype(o_ref.dtype)


def _attn_call(q, k, v, n_batch, seq):
    tq, tk = ATTN_TQ, ATTN_TK
    nq = seq // tq
    hp = MLA_HEADS // 2
    resident = pl.BlockSpec((seq, 2 * HEAD_PAD), lambda b, h, i: (b, h), pipeline_mode=pl.Buffered(1))
    return pl.pallas_call(
        functools.partial(_attn_kernel, tq=tq, tk=tk),
        grid=(n_batch, hp, nq),
        in_specs=[
            pl.BlockSpec((tq, 2 * HEAD_PAD), lambda b, h, i: (b * nq + i, h)),
            resident,
            resident,
        ],
        out_specs=pl.BlockSpec((tq, LANE), lambda b, h, i: (b * nq + i, h)),
        out_shape=jax.ShapeDtypeStruct((n_batch * seq, hp * LANE), BF16),
        scratch_shapes=[
            pltpu.VMEM((2, tq, LANE), F32),
            pltpu.VMEM((2, tq, LANE), F32),
        ],
        compiler_params=_cparams(("parallel", "parallel", "arbitrary")),
        name="attn",
    )(q, k, v)


def _mla_sample_kernel(q_ref, cn_ref, kn_ref, cp_ref, kp_ref, wuk_ref, wuv_ref, o_ref, *, past, dec):
    cp = cp_ref[0].astype(BF16)
    kp = kp_ref[0].astype(BF16)
    cn = cn_ref[...].astype(BF16)
    kn = kn_ref[...].astype(BF16)
    R = MLA_HEADS * dec
    qrow = (past + lax.broadcasted_iota(jnp.int32, (R, past), 0) % dec) // CHUNK
    vis_p = (lax.broadcasted_iota(jnp.int32, (R, past), 1) // CHUNK) <= qrow
    qrow_n = (past + lax.broadcasted_iota(jnp.int32, (R, dec), 0) % dec) // CHUNK
    vis_n = ((past + lax.broadcasted_iota(jnp.int32, (R, dec), 1)) // CHUNK) <= qrow_n
    q_lat, qp = [], []
    for h in range(MLA_HEADS):
        qn = q_ref[:, h * HEAD_PAD:h * HEAD_PAD + MLA_NOPE]
        qp.append(q_ref[:, h * HEAD_PAD + MLA_NOPE:h * HEAD_PAD + MLA_NOPE + MLA_ROPE])
        q_lat.append(_dot(qn, wuk_ref[h]).astype(BF16))
    q_lat = jnp.concatenate(q_lat, axis=0)
    qp = jnp.concatenate(qp, axis=0)
    s_p = jnp.where(vis_p, _dot_nt(q_lat, cp) + _dot_nt(qp, kp), NEG_INF)
    s_n = jnp.where(vis_n, _dot_nt(q_lat, cn) + _dot_nt(qp, kn), NEG_INF)
    m = jnp.maximum(jnp.max(s_p, axis=1, keepdims=True), jnp.max(s_n, axis=1, keepdims=True))
    e_p = jnp.exp2(s_p - m)
    e_n = jnp.exp2(s_n - m)
    l = jnp.sum(e_p, axis=1, keepdims=True) + jnp.sum(e_n, axis=1, keepdims=True)
    o_lat = ((_dot(e_p.astype(BF16), cp) + _dot(e_n.astype(BF16), cn)) / l).astype(BF16)
    out = jnp.zeros((dec, MLA_HEADS * MLA_V), F32)
    for h in range(MLA_HEADS):
        out = out + _dot(o_lat[h * dec:(h + 1) * dec], wuv_ref[h])
    o_ref[...] = out.astype(o_ref.dtype)


def _mla_sample_call(q, ckv, kpe, cache_ckv, cache_kpe, wuk, wuv, row0, n_seq, dec):
    past = cache_ckv.shape[1]
    blk0 = row0 // dec
    wide = MLA_HEADS * HEAD_PAD
    return pl.pallas_call(
        functools.partial(_mla_sample_kernel, past=past, dec=dec),
        grid=(n_seq,),
        in_specs=[
            pl.BlockSpec((dec, wide), lambda b: (blk0 + b, 0)),
            pl.BlockSpec((dec, KV_LORA), lambda b: (b, 0)),
            pl.BlockSpec((dec, MLA_ROPE), lambda b: (b, 0)),
            pl.BlockSpec((1, past, KV_LORA), lambda b: (b, 0, 0)),
            pl.BlockSpec((1, past, MLA_ROPE), lambda b: (b, 0, 0)),
            pl.BlockSpec((MLA_HEADS, MLA_NOPE, KV_LORA), lambda b: (0, 0, 0)),
            pl.BlockSpec((MLA_HEADS, KV_LORA, MLA_HEADS * MLA_V), lambda b: (0, 0, 0)),
        ],
        out_specs=pl.BlockSpec((dec, MLA_HEADS * MLA_V), lambda b: (b, 0)),
        out_shape=jax.ShapeDtypeStruct((n_seq * dec, MLA_HEADS * MLA_V), BF16),
        compiler_params=_cparams(("parallel",)),
        name="mla_sample",
    )(q, ckv, kpe, cache_ckv, cache_kpe, wuk, wuv)


def _rwkv_heads(pr, prev, mu_ref, w0_ref, a0_ref, kk_ref, ka_ref, rk_ref, w2_ref, a2_ref, g2_ref,
                seg_ref, r_ref, lw_ref, kh_ref, v_ref, na_ref, b_ref, bonus_ref, g_ref):
    u = pr + mu_ref[...] * (prev - pr)
    o1, o2, o3 = RWKV_DIM, 2 * RWKV_DIM, 3 * RWKV_DIM
    o4, o5 = o3 + DECAY_LORA, o3 + DECAY_LORA + AAA_LORA
    r, k, v = u[:, :o1], u[:, o1:o2], u[:, o2:o3]
    w_lo, a_lo, g_lo = u[:, o3:o4], u[:, o4:o5], u[:, o5:]
    wl = w0_ref[...] + _dot(jnp.tanh(w_lo).astype(BF16), w2_ref[...])
    lw_ref[...] = -math.exp(-0.5) * _sigmoid(wl)
    a = _sigmoid(a0_ref[...] + _dot(a_lo.astype(BF16), a2_ref[...]))
    g_ref[...] = _dot(_sigmoid(g_lo).astype(BF16), g2_ref[...])
    seg = seg_ref[...]
    kk = k * kk_ref[...]
    kk = kk / jnp.maximum(jnp.sqrt(_dot_exact_rhs(kk * kk, seg)), 1e-12)
    kh = k * (1.0 + (a - 1.0) * ka_ref[...])
    r_ref[...] = r
    kh_ref[...] = kh
    v_ref[...] = v
    na_ref[...] = -kk
    b_ref[...] = kk * a
    bonus_ref[...] = _dot_exact_rhs(r * kh * rk_ref[...], seg) * v


def _wkv_kernel(*refs, C, n_sub, n_par):
    GW = WKV_GROUP * RWKV_N
    R = WKV_GROUP * C
    n_grp = RWKV_HEADS // WKV_GROUP
    n_lev = int(round(math.log2(C))) - 1
    c = pl.program_id(1)
    tok = [refs[6 * p:6 * p + 6] for p in range(n_par)]
    h0_ref, y_ref, hT_ref, h_scr = refs[6 * n_par:]
    n_state = n_par * n_grp

    def head_block(hh):
        return slice(hh * RWKV_N, (hh + 1) * RWKV_N)

    @pl.when(c == 0)
    def _():
        h_scr[...] = jnp.zeros(h_scr.shape, F32)
        for p in range(n_par):
            for hd in range(RWKV_HEADS):
                g, hh = divmod(hd, WKV_GROUP)
                h_scr[p * n_grp + g, head_block(hh), head_block(hh)] = h0_ref[p, hd]

    row = lax.broadcasted_iota(jnp.int32, (C, C), 0)
    col = lax.broadcasted_iota(jnp.int32, (C, C), 1)
    tri = jnp.where(col <= row, 1.0, 0.0).astype(BF16)
    rr = lax.broadcasted_iota(jnp.int32, (R, R), 0)
    cc = lax.broadcasted_iota(jnp.int32, (R, R), 1)
    same = (rr // C) == (cc // C)
    ti = lax.broadcasted_iota(jnp.int32, (C, R), 0)
    si = lax.broadcasted_iota(jnp.int32, (C, R), 1) % C
    strict4 = si < ti
    lower4 = si <= ti
    eye4 = jnp.where(si == ti, 1.0, 0.0)
    keep = (lax.broadcasted_iota(jnp.int32, (R, GW), 0) // C
            == lax.broadcasted_iota(jnp.int32, (R, GW), 1) // RWKV_N)
    gr = lax.broadcasted_iota(jnp.int32, (GW, GW), 0)
    gc = lax.broadcasted_iota(jnp.int32, (GW, GW), 1)
    eye_g = gr == gc
    same_head = (gr // RWKV_N) == (gc // RWKV_N)
    eye_g_bf = jnp.where(eye_g, 1.0, 0.0).astype(BF16)

    def rows4(x4):
        return jnp.concatenate([x4] * WKV_GROUP, axis=0)

    def stack(x4):
        return jnp.where(keep, rows4(x4), jnp.zeros((), x4.dtype))


    sls = [slice(g * GW, (g + 1) * GW) for g in range(n_grp)]
    J = [(ci, q) for ci in range(n_sub) for q in range(n_state)]
    ops, p_end = {}, {}
    for p, ci in [(p, ci) for p in range(n_par) for ci in range(n_sub)]:
        r_ref, lw_ref, k_ref, v_ref, a_ref, b_ref = tok[p]
        rows = slice(ci * C, (ci + 1) * C)
        lw = lw_ref[rows, :]
        cum = _dot_exact_lhs(tri, lw)
        cum_end = cum[C - 1:C, :]
        e_neg = jnp.exp(-cum)
        e_end = jnp.exp(cum_end - cum)
        b_in = b_ref[rows, :]
        k_in = k_ref[rows, :]
        full = ((a_ref[rows, :] * jnp.exp(cum - lw)).astype(BF16),
                (r_ref[rows, :] * jnp.exp(cum)).astype(BF16),
                (b_in * e_neg).astype(BF16), (k_in * e_neg).astype(BF16),
                (b_in * e_end).astype(BF16), (k_in * e_end).astype(BF16),
                v_ref[rows, :].astype(BF16))
        p_end[p, ci] = jnp.exp(cum_end)
        for g in range(n_grp):
            ops[ci, p * n_grp + g] = [t[:, sls[g]] for t in full]
    a4, r4, b4, k4, be4, ke4, v4 = [{j: ops[j][i] for j in J} for i in range(7)]
    v_s = {j: stack(v4[j]) for j in J}
    m = {j: _dot_nt(jnp.concatenate([a4[j], r4[j]], axis=0),
                    jnp.concatenate([stack(b4[j]), stack(k4[j])], axis=0)) for j in J}
    l4 = {j: jnp.where(strict4, m[j][:C, :R], 0.0) for j in J}
    a_ak = {j: jnp.where(strict4, m[j][:C, R:], 0.0).astype(BF16) for j in J}
    a_rb = {j: jnp.where(lower4, m[j][C:, :R], 0.0).astype(BF16) for j in J}
    a_rk = {j: jnp.where(lower4, m[j][C:, R:], 0.0).astype(BF16) for j in J}
    def block_diag(x4):
        return jnp.where(same, rows4(x4.astype(BF16)), jnp.zeros((), BF16))

    t4 = {j: eye4 + l4[j] for j in J}
    l_bd = {j: block_diag(l4[j]) for j in J}
    for _ in range(n_lev):
        l4 = {j: _dot(l4[j].astype(BF16), l_bd[j]) for j in J}
        l_bd = {j: block_diag(l4[j]) for j in J}
        t4 = {j: t4[j] + _dot(t4[j].astype(BF16), l_bd[j]) for j in J}
    t_b = {j: t4[j].astype(BF16) for j in J}
    bke_t = {j: _dot_nt(eye_g_bf, jnp.concatenate([be4[j], ke4[j]], axis=0)).astype(BF16) for j in J}

    G = range(n_state)
    h_cur = [h_scr[q] for q in G]
    for ci in range(n_sub):
        rows = slice(ci * C, (ci + 1) * C)
        h0_b = [h_cur[g].astype(BF16) for g in G]
        x4 = [_dot(a4[ci, g], h0_b[g]) + _dot(a_ak[ci, g], v_s[ci, g]) for g in G]
        u4 = [_dot(t_b[ci, g], stack(x4[g].astype(BF16))).astype(BF16) for g in G]
        y4 = [_dot(r4[ci, g], h0_b[g]) + _dot(a_rb[ci, g], stack(u4[g])) + _dot(a_rk[ci, g], v_s[ci, g])
              for g in G]
        uv4 = [jnp.concatenate([u4[g], v4[ci, g]], axis=0) for g in G]
        h_add = [jnp.where(same_head, _dot(bke_t[ci, g], uv4[g]), 0.0) for g in G]
        for q in G:
            p, g = divmod(q, n_grp)
            y_ref[p, rows, sls[g]] = y4[q]
            p_col = jnp.sum(jnp.where(eye_g, p_end[p, ci][:, sls[g]], 0.0), axis=1, keepdims=True)
            h_cur[q] = p_col * h_cur[q] + h_add[q]

    for q in G:
        h_scr[q] = h_cur[q]

    @pl.when(c == pl.num_programs(1) - 1)
    def _():
        for p in range(n_par):
            for hd in range(RWKV_HEADS):
                g, hh = divmod(hd, WKV_GROUP)
                hT_ref[p, hd] = h_scr[p * n_grp + g, head_block(hh), head_block(hh)]


def _wkv_call(arrs, h0, row0, n_seq, n_chunk, C, n_sub, n_par):
    rows = C * n_sub
    steps = n_chunk // n_sub
    blk0 = row0 // rows
    GW = WKV_GROUP * RWKV_N
    n_grp = RWKV_HEADS // WKV_GROUP
    assert n_seq % n_par == 0 and n_chunk % n_sub == 0
    tok = [pl.BlockSpec((rows, RWKV_DIM), lambda b, c, p=p: (blk0 + (b * n_par + p) * steps + c, 0))
           for p in range(n_par)]
    st = pl.BlockSpec((n_par, RWKV_HEADS, RWKV_N, RWKV_N), lambda b, c: (b, 0, 0, 0))
    y, h_fin = pl.pallas_call(
        functools.partial(_wkv_kernel, C=C, n_sub=n_sub, n_par=n_par),
        grid=(n_seq // n_par, steps),
        in_specs=[tok[p] for p in range(n_par) for _ in range(6)] + [st],
        out_specs=[pl.BlockSpec((n_par, rows, RWKV_DIM), lambda b, c: (b, c, 0)), st],
        out_shape=[
            jax.ShapeDtypeStruct((n_seq, n_chunk * C, RWKV_DIM), F32),
            jax.ShapeDtypeStruct((n_seq, RWKV_HEADS, RWKV_N, RWKV_N), F32),
        ],
        scratch_shapes=[pltpu.VMEM((n_par * n_grp, GW, GW), F32)],
        compiler_params=_cparams(("parallel", "arbitrary")),
        name="wkv_c%d" % C,
    )(*(list(arrs) * n_par), h0)
    return y.reshape(n_seq * n_chunk * C, RWKV_DIM), h_fin


def _mix_kernel(xp_ref, xs_ref, attnp_ref, attns_ref, yp_ref, ys_ref, bonus_ref, g_ref, seg_ref, lng_ref,
                lnb_ref, woa_ref, wob_ref, g1_ref, b1_ref, wr_ref, br_ref, tri_ref, h_ref, hpk_ref, route_ref,
                count_ref, *, n_p, tm):
    is_p = pl.program_id(0) < n_p
    P = range(MIX_PARTS)
    bands = [slice(k * tm // MIX_PARTS, (k + 1) * tm // MIX_PARTS) for k in P]
    seg = seg_ref[...]
    inv_n = 1.0 / RWKV_N
    y = [jnp.where(is_p, yp_ref[r, :], ys_ref[r, :]) for r in bands]
    yc = [y[k] - _dot_exact_rhs(y[k], seg) * inv_n for k in P]
    var = [_dot_exact_rhs(yc[k] * yc[k], seg) * inv_n for k in P]
    yn = [yc[k] * lax.rsqrt(var[k] + GN_EPS) * lng_ref[...] + lnb_ref[...] for k in P]
    rw = [((yn[k] + bonus_ref[bands[k], :]) * g_ref[bands[k], :]).astype(BF16) for k in P]
    attn = [jnp.where(is_p, attnp_ref[r, :], attns_ref[r, :]) for r in bands]
    m = [_dot(attn[k], woa_ref[...]) + _dot(rw[k], wob_ref[...]) for k in P]
    x = [jnp.where(is_p, xp_ref[r, :], xs_ref[r, :]) for r in bands]
    h = [_layer_norm(DN_ALPHA * x[k] + m[k], g1_ref[...], b1_ref[...]) for k in P]
    for k in P:
        h_ref[bands[k], :] = h[k]
        hpk_ref[bands[k], :] = _pack_bf16_pairs(h[k])

    h_hi = [h[k].astype(BF16) for k in P]
    h_lo = [(h[k] - h_hi[k].astype(F32)).astype(BF16) for k in P]
    logits = [_dot(h_hi[k], wr_ref[0]) + _dot(h_lo[k], wr_ref[0]) + _dot(h_hi[k], wr_ref[1]) + br_ref[...]
              for k in P]
    lane = lax.broadcasted_iota(jnp.int32, logits[0].shape, 1)
    big = jnp.int32(LANE)

    def route_band(lg):
        gl = jnp.where(lane < N_GROUPS, lg, NEG_INF)
        gmax = jnp.max(gl, axis=1, keepdims=True)
        grp = jnp.min(jnp.where(gl == gmax, lane, big), axis=1, keepdims=True)
        p_grp = 1.0 / jnp.sum(jnp.exp(gl - gmax), axis=1, keepdims=True)
        e_idx = lane - N_GROUPS
        in_grp = (lane >= N_GROUPS) & (lane < N_GROUPS + N_EXPERTS) & ((e_idx // EXPERTS_PER_GROUP) == grp)
        el = jnp.where(in_grp, lg, NEG_INF)
        m1 = jnp.max(el, axis=1, keepdims=True)
        i1 = jnp.min(jnp.where(el == m1, lane, big), axis=1, keepdims=True)
        el2 = jnp.where(lane == i1, NEG_INF, el)
        m2 = jnp.max(el2, axis=1, keepdims=True)
        i2 = jnp.min(jnp.where(el2 == m2, lane, big), axis=1, keepdims=True)
        t = jnp.exp(m2 - m1)
        g1 = p_grp / (1.0 + t)
        return i1, i2, g1, g1 * t

    routed = [route_band(logits[k]) for k in P]
    chosen = jnp.concatenate(
        [jnp.where(lane == routed[k][0], 1.0, jnp.where(lane == routed[k][1], 1.0, 0.0)) for k in P], axis=0)
    before = _dot(tri_ref[...], chosen.astype(BF16))
    count_ref[0] = jnp.sum(chosen, axis=0, keepdims=True)
    for k in P:
        i1, i2, g1, g2 = routed[k]
        bef = before[bands[k]]
        r1 = jnp.sum(jnp.where(lane == i1, bef, 0.0), axis=1, keepdims=True)
        r2 = jnp.sum(jnp.where(lane == i2, bef, 0.0), axis=1, keepdims=True)
        cols = ((i1 - N_GROUPS).astype(F32), (i2 - N_GROUPS).astype(F32), g1, g2, r1, r2)
        route = jnp.zeros(lane.shape, F32)
        for j, col in enumerate(cols):
            route = jnp.where(lane == j, col, route)
        route_ref[bands[k], :] = route


def _mix_call(xp, xs, attn_p, attn_s, y_p, y_s, bonus, g, seg, lnx_g, lnx_b, woa, wob, ln1_g, ln1_b, wr, br):
    T = xp.shape[0] + xs.shape[0]
    tm = TOKEN_TILE
    n_p = xp.shape[0] // tm
    row = lambda i: (i, 0)
    full = lambda i: (0, 0)
    row_p, row_s = _split_rows(n_p)
    half = pl.BlockSpec((tm, RWKV_DIM), row)
    vec5 = pl.BlockSpec((1, RWKV_DIM), full)
    vec10 = pl.BlockSpec((1, D_MODEL), full)
    idx = np.arange(tm)
    tri = jnp.asarray((idx[None, :] < idx[:, None]).astype(np.float32)).astype(BF16)
    return pl.pallas_call(
        functools.partial(_mix_kernel, n_p=n_p, tm=tm),
        grid=(T // tm,),
        in_specs=[
            pl.BlockSpec((tm, D_MODEL), row_p), pl.BlockSpec((tm, D_MODEL), row_s),
            pl.BlockSpec((tm, RWKV_DIM), row_p), pl.BlockSpec((tm, RWKV_DIM), row_s),
            pl.BlockSpec((tm, RWKV_DIM), row_p), pl.BlockSpec((tm, RWKV_DIM), row_s),
            half, half,
            pl.BlockSpec((RWKV_DIM, RWKV_DIM), full), vec5, vec5,
            pl.BlockSpec((RWKV_DIM, D_MODEL), full), pl.BlockSpec((RWKV_DIM, D_MODEL), full),
            vec10, vec10,
            pl.BlockSpec((2, D_MODEL, LANE), lambda i: (0, 0, 0)), pl.BlockSpec((1, LANE), full),
            pl.BlockSpec((tm, tm), full),
        ],
        out_specs=[pl.BlockSpec((tm, D_MODEL), row), pl.BlockSpec((tm, D_MODEL // 2), row),
                   pl.BlockSpec((tm, LANE), row), pl.BlockSpec((1, 1, LANE), lambda i: (i, 0, 0))],
        out_shape=[jax.ShapeDtypeStruct((T, D_MODEL), F32), jax.ShapeDtypeStruct((T, D_MODEL // 2), jnp.int32),
                   jax.ShapeDtypeStruct((T, LANE), F32), jax.ShapeDtypeStruct((T // tm, 1, LANE), F32)],
        compiler_params=_cparams(("parallel",)),
        name="mix",
    )(xp, xs, attn_p, attn_s, y_p, y_s, bonus, g, seg, lnx_g, lnx_b, woa, wob, ln1_g, ln1_b, wr, br, tri)


def _expert_kernel(be_ref, nu_ref, xs_ref, wg_ref, wu_ref, wd_ref, ys_ref, wgu_b, wd_b):
    i = pl.program_id(0)

    @pl.when((i == 0) | (be_ref[i] != be_ref[jnp.maximum(i - 1, 0)]))
    def _():
        wgu_b[:, :D_EXPERT] = wg_ref[0].astype(BF16)
        wgu_b[:, D_EXPERT:] = wu_ref[0].astype(BF16)
        wd_b[...] = wd_ref[0].astype(BF16)

    @pl.when(i < nu_ref[0])
    def _():
        P = range(2)
        bands = [slice(k * MOE_BLK // 2, (k + 1) * MOE_BLK // 2) for k in P]
        xb = [_unpack_bf16_pairs(xs_ref[r, :]).astype(BF16) for r in bands]
        gu = [_dot(xb[k], wgu_b[...]) for k in P]
        act = [(gu[k][:, :D_EXPERT] * _sigmoid(gu[k][:, :D_EXPERT]) * gu[k][:, D_EXPERT:]).astype(BF16)
               for k in P]
        out = [_dot(act[k], wd_b[...]) for k in P]
        for k in P:
            ys_ref[bands[k], :] = _pack_bf16_pairs(out[k])

    @pl.when(i >= nu_ref[0])
    def _():
        ys_ref[...] = jnp.zeros(ys_ref.shape, ys_ref.dtype)


def _expert_call(block_e, n_used, xs, wg, wu, wd):
    n_blk = xs.shape[0] // MOE_BLK
    grid_spec = pltpu.PrefetchScalarGridSpec(
        num_scalar_prefetch=2,
        grid=(n_blk,),
        in_specs=[
            pl.BlockSpec((MOE_BLK, D_MODEL // 2), lambda i, be, nu: (i, 0)),
            pl.BlockSpec((1, D_MODEL, D_EXPERT), lambda i, be, nu: (be[i], 0, 0)),
            pl.BlockSpec((1, D_MODEL, D_EXPERT), lambda i, be, nu: (be[i], 0, 0)),
            pl.BlockSpec((1, D_EXPERT, D_MODEL), lambda i, be, nu: (be[i], 0, 0)),
        ],
        out_specs=pl.BlockSpec((MOE_BLK, D_MODEL // 2), lambda i, be, nu: (i, 0)),
        scratch_shapes=[pltpu.VMEM((D_MODEL, 2 * D_EXPERT), BF16), pltpu.VMEM((D_EXPERT, D_MODEL), BF16)],
    )
    return pl.pallas_call(
        _expert_kernel,
        grid_spec=grid_spec,
        out_shape=jax.ShapeDtypeStruct((n_blk * MOE_BLK, D_MODEL // 2), jnp.int32),
        compiler_params=_cparams(("arbitrary",)),
        name="experts",
    )(block_e, n_used, xs, wg, wu, wd)


def _combine_kernel(h_ref, ya_ref, yb_ref, route_ref, g2_ref, b2_ref, op_ref, os_ref, *, n_p):
    i = pl.program_id(0)
    route = route_ref[...]
    f = _unpack_bf16_pairs(ya_ref[...]) * route[:, 2:3] + _unpack_bf16_pairs(yb_ref[...]) * route[:, 3:4]
    out = _layer_norm(DN_ALPHA * h_ref[...] + f, g2_ref[...], b2_ref[...])

    @pl.when(i < n_p)
    def _():
        op_ref[...] = out

    @pl.when(i >= n_p)
    def _():
        os_ref[...] = out


def _combine_call(h, yab, route, ln2_g, ln2_b, t_prompt):
    T = h.shape[0]
    tm = TOKEN_TILE
    n_t, n_p = T // tm, t_prompt // tm
    row = lambda i: (i, 0)
    full = lambda i: (0, 0)
    row_p, row_s = _split_rows(n_p)
    big = pl.BlockSpec((tm, D_MODEL), row)
    return pl.pallas_call(
        functools.partial(_combine_kernel, n_p=n_p),
        grid=(n_t,),
        in_specs=[big, pl.BlockSpec((tm, D_MODEL // 2), row),
                  pl.BlockSpec((tm, D_MODEL // 2), lambda i: (i + n_t, 0)), pl.BlockSpec((tm, LANE), row),
                  pl.BlockSpec((1, D_MODEL), full), pl.BlockSpec((1, D_MODEL), full)],
        out_specs=[pl.BlockSpec((tm, D_MODEL), row_p), pl.BlockSpec((tm, D_MODEL), row_s)],
        out_shape=[jax.ShapeDtypeStruct((t_prompt, D_MODEL), F32),
                   jax.ShapeDtypeStruct((T - t_prompt, D_MODEL), F32)],
        compiler_params=_cparams(("arbitrary",)),
        name="combine",
    )(h, yab, yab, route, ln2_g, ln2_b)


def _prep_weights(w_in, w_uq, w_ukv):
    half = MLA_ROPE // 2
    kpe_w = w_in[:, Q_LORA + KV_LORA:MLA_IN]
    kpe_b = jnp.concatenate([-kpe_w[:, half:], kpe_w[:, :half]], axis=1)
    w1 = jnp.concatenate([w_in[:, :Q_LORA + KV_LORA], kpe_w, kpe_b,
                          jnp.zeros((D_MODEL, 64), F32), w_in[:, MLA_IN:]], axis=1).astype(BF16)
    pad_q = jnp.zeros((Q_LORA, MLA_HEADS, HEAD_PAD - MLA_NOPE - MLA_ROPE), F32)
    wqa = jnp.concatenate([w_uq, pad_q], axis=2).reshape(Q_LORA, -1).astype(BF16)
    w_uk, w_uv = w_ukv[:, :, :MLA_NOPE], w_ukv[:, :, MLA_NOPE:]
    wk = jnp.concatenate([w_uk, jnp.zeros((KV_LORA, MLA_HEADS, HEAD_PAD - MLA_NOPE), F32)], axis=2)
    wk = wk.reshape(KV_LORA, -1).astype(BF16)
    pk_np = np.zeros((MLA_ROPE, MLA_HEADS * HEAD_PAD), np.float32)
    for h in range(MLA_HEADS):
        for i in range(MLA_ROPE):
            pk_np[i, h * HEAD_PAD + MLA_NOPE + i] = 1.0
    pk = jnp.asarray(pk_np).astype(BF16)
    zv = jnp.zeros((KV_LORA, MLA_HEADS // 2, MLA_V), F32)
    wv4 = w_uv.reshape(KV_LORA, MLA_HEADS // 2, 2, MLA_V)
    wv = jnp.stack([jnp.concatenate([wv4[:, :, 0], zv], axis=2),
                    jnp.concatenate([zv, wv4[:, :, 1]], axis=2)], axis=2)
    wv = wv.reshape(KV_LORA, -1).astype(BF16)
    wuk = jnp.transpose(w_uk, (1, 2, 0)).astype(BF16)
    wuv_np = np.zeros((MLA_HEADS, MLA_HEADS * MLA_V), np.float32)
    for h in range(MLA_HEADS):
        wuv_np[h, h * MLA_V:(h + 1) * MLA_V] = 1.0
    wuv = jnp.transpose(w_uv, (1, 0, 2))
    wuv = (jnp.tile(wuv, (1, 1, MLA_HEADS)) * jnp.asarray(wuv_np)[:, None, :]).astype(BF16)
    return w1, wqa, wk, pk, wv, wuk, wuv


def _rope_table(pos):
    inv = ROPE_BASE ** (-jnp.arange(0, MLA_ROPE, 2, dtype=F32) / MLA_ROPE)
    ang = pos.astype(F32)[:, None] * inv[None, :]
    return jnp.concatenate([jnp.cos(ang), jnp.sin(ang)], axis=1)


def _rope_placement():
    half = MLA_ROPE // 2
    place = np.zeros((MLA_ROPE, 4 * LANE), np.float32)
    rows = np.zeros((2, 4 * LANE), np.float32)
    for i in range(half):
        c, s = i, half + i
        place[c, MLA_NOPE + i] = place[c, MLA_NOPE + half + i] = 1.0
        place[s, LANE + MLA_NOPE + i] = -1.0
        place[s, 2 * LANE + MLA_NOPE + half + i] = 1.0
        place[c, 3 * LANE + i] = place[c, 3 * LANE + half + i] = 1.0
        place[s, 3 * LANE + 2 * half + i] = place[s, 3 * LANE + 3 * half + i] = 1.0
    rows[0, :MLA_NOPE] = 1.0
    rows[1, :3 * LANE] = MLA_SCALE * LOG2E
    rows[1, 3 * LANE:] = 1.0
    return jnp.asarray(place).astype(BF16), jnp.asarray(rows)


def _seg_ones():
    idx = np.arange(RWKV_DIM) // RWKV_N
    return jnp.asarray((idx[:, None] == idx[None, :]).astype(np.float32)).astype(BF16)


def _dispatch(route, tile_counts, t_total):
    A = t_total * TOP_K
    n_tiles = tile_counts.shape[0]
    counts_te = tile_counts[:, 0, N_GROUPS:N_GROUPS + N_EXPERTS].astype(jnp.int32)
    counts = jnp.sum(counts_te, axis=0)
    blocks_per_e = (counts + MOE_BLK - 1) // MOE_BLK
    blk_end = jnp.cumsum(blocks_per_e)
    blk_start = blk_end - blocks_per_e
    tile_off = jnp.cumsum(counts_te, axis=0) - counts_te
    base = blk_start[None, :] * MOE_BLK + tile_off
    e = route[:, :TOP_K].astype(jnp.int32)
    rank = route[:, 4:4 + TOP_K].astype(jnp.int32)
    base_tok = jnp.repeat(base, t_total // n_tiles, axis=0)
    pick = e[:, :, None] == jnp.arange(N_EXPERTS, dtype=jnp.int32)[None, None, :]
    dest = jnp.sum(jnp.where(pick, base_tok[:, None, :], 0), axis=-1) + rank
    n_blk = -(-A // MOE_BLK) + N_EXPERTS
    blk = jnp.arange(n_blk, dtype=jnp.int32)
    block_e = jnp.minimum(jnp.sum((blk[:, None] >= blk_end[None, :]).astype(jnp.int32), axis=1),
                          N_EXPERTS - 1).astype(jnp.int32)
    n_used = blk_end[-1:].astype(jnp.int32)
    return dest, block_e, n_used


def _sc_gather_rows(table, idx):
    n_rows, width = idx.shape[0], table.shape[1]
    n_workers = SC_CORES * SC_SUBCORES
    per_worker = n_rows // n_workers
    assert n_rows % n_workers == 0 and per_worker % SC_WINDOW == 0
    mesh = plsc.VectorSubcoreMesh(core_axis_name="c", subcore_axis_name="s")

    @functools.partial(
        pl.kernel, mesh=mesh,
        out_type=jax.ShapeDtypeStruct((n_rows, width), table.dtype),
        scratch_types=[
            pltpu.VMEM((SC_WINDOW,), jnp.int32),
            pltpu.VMEM((SC_WINDOW, width), table.dtype),
            pltpu.SemaphoreType.DMA,
        ],
    )
    def gather(table_hbm, idx_hbm, out_hbm, idx_v, rows_v, sem):
        wid = lax.axis_index("s") * SC_CORES + lax.axis_index("c")
        base = wid * per_worker

        @pl.loop(0, per_worker // SC_WINDOW)
        def _(w):
            off = pl.multiple_of(base + w * SC_WINDOW, SC_WINDOW)
            pltpu.sync_copy(idx_hbm.at[pl.ds(off, SC_WINDOW)], idx_v)
            pltpu.async_copy(table_hbm.at[idx_v], rows_v, sem).wait()
            pltpu.sync_copy(rows_v, out_hbm.at[pl.ds(off, SC_WINDOW)])

    return gather(table, idx)


def _sc_scatter_rows(src, idx_a, idx_b, n_slots):
    n_rows, width = src.shape
    n_workers = SC_CORES * SC_SUBCORES
    per_worker = n_rows // n_workers
    assert n_rows % n_workers == 0 and per_worker % SC_WINDOW == 0
    n_win = per_worker // SC_WINDOW
    mesh = plsc.VectorSubcoreMesh(core_axis_name="c", subcore_axis_name="s")

    @functools.partial(
        pl.kernel, mesh=mesh,
        out_type=jax.ShapeDtypeStruct((n_slots, width), src.dtype),
        scratch_types=[
            pltpu.VMEM((1, SC_WINDOW), jnp.int32),
            pltpu.VMEM((1, SC_WINDOW), jnp.int32),
            pltpu.VMEM((SC_WINDOW, width), src.dtype),
        ],
    )
    def scatter(src_hbm, ia_hbm, ib_hbm, out_hbm, ia_v, ib_v, rows_v):
        wid = lax.axis_index("s") * SC_CORES + lax.axis_index("c")

        @pl.loop(0, n_win)
        def _(w):
            win = wid * n_win + w
            off = pl.multiple_of(win * SC_WINDOW, SC_WINDOW)
            pltpu.sync_copy(src_hbm.at[pl.ds(off, SC_WINDOW)], rows_v)
            pltpu.sync_copy(ia_hbm.at[pl.ds(win, 1)], ia_v)
            pltpu.sync_copy(ib_hbm.at[pl.ds(win, 1)], ib_v)
            pltpu.sync_copy(rows_v, out_hbm.at[ia_v.at[0]])
            pltpu.sync_copy(rows_v, out_hbm.at[ib_v.at[0]])

    return scatter(src, idx_a, idx_b)


def kernel(x_prompt, x_sample, cache_ckv, cache_kpe, state_wkv, state_shift, w_in, q_norm_g, kv_norm_g, w_uq,
           w_ukv, mu_shift, w0, w2, a0, a2, g2, k_k, k_a, r_k, lnx_g, lnx_b, w_o, ln1_g, ln1_b, w_gr, b_gr,
           w_er, b_er, w_eg, w_eu, w_ed, ln2_g, ln2_b):
    B, S, D = x_prompt.shape
    DB, DS, _ = x_sample.shape
    past = cache_ckv.shape[2]
    Tp, Ts = B * S, DB * DS
    T = Tp + Ts
    assert D == D_MODEL and DS == SHIFT_GROUP and S % ATTN_TQ == 0 and S % (CHUNK * WKV_SUB) == 0
    assert Tp % TOKEN_TILE == 0 and T % TOKEN_TILE == 0 and w_in.shape[0] == DEPTH

    l = 0
    xp, xs_in = x_prompt.reshape(Tp, D), x_sample.reshape(Ts, D)
    w1, wqa, wk, pk, wv, wuk, wuv = _prep_weights(w_in[l], w_uq[l], w_ukv[l])
    pos = jnp.concatenate([jnp.arange(S, dtype=jnp.int32),
                           jnp.tile(past + jnp.arange(DS, dtype=jnp.int32), TOKEN_TILE // DS)])
    rope = _rope_table(pos)

    seg = _seg_ones()
    vec = lambda a: a.reshape(1, -1)
    rwkv_w = (vec(mu_shift[l]), vec(w0[l]), vec(a0[l]), vec(k_k[l]), vec(k_a[l]), vec(r_k[l]),
              w2[l].astype(BF16), a2[l].astype(BF16), g2[l].astype(BF16), seg)
    (q, kcat, vcat, ckv_p, ckv_s, kpe_p, kpe_s, last_rows,
     r, lw, kh, v, na, b, bonus, g) = _proj_call(
        xp, xs_in, rope, S // TOKEN_TILE, *_rope_placement(), w1, q_norm_g[l][None], kv_norm_g[l][None],
        wqa, wk, pk, wv, state_shift[l][:, None, :], rwkv_w)

    attn_p = _attn_call(q, kcat, vcat, B, S)
    attn_s = _mla_sample_call(q, ckv_s, kpe_s, cache_ckv[l], cache_kpe[l], wuk, wuv, Tp, DB, DS)

    scan_in = (r, lw, kh, v, na, b)
    h0_p = jnp.zeros((B, RWKV_HEADS, RWKV_N, RWKV_N), F32)
    y_p, hT_p = _wkv_call(scan_in, h0_p, 0, B, S // CHUNK, CHUNK, WKV_SUB, math.gcd(B, WKV_PAR))
    h0_s = jnp.swapaxes(state_wkv[l], -1, -2)
    y_s, hT_s = _wkv_call(scan_in, h0_s, Tp, DB, 1, DS, 1, math.gcd(DB, WKV_SUB * WKV_PAR))

    wo_b = w_o[l].astype(BF16)
    wr = jnp.concatenate([w_gr[l], w_er[l], jnp.zeros((D, LANE - N_GROUPS - N_EXPERTS), F32)], axis=1)
    wr_hi = wr.astype(BF16)
    wr_lo = (wr - wr_hi.astype(F32)).astype(BF16)
    br = jnp.concatenate([b_gr[l], b_er[l], jnp.zeros((LANE - N_GROUPS - N_EXPERTS,), F32)])[None]
    h, hpk, route, tile_counts = _mix_call(
        xp, xs_in, attn_p, attn_s, y_p, y_s, bonus, g, seg, vec(lnx_g[l]), vec(lnx_b[l]),
        wo_b[:MLA_HEADS * MLA_V], wo_b[MLA_HEADS * MLA_V:], vec(ln1_g[l]), vec(ln1_b[l]),
        jnp.stack([wr_hi, wr_lo]), br)

    dest, block_e, n_used = _dispatch(route, tile_counts, T)
    win = lambda a: a.reshape(T // SC_WINDOW, SC_WINDOW)
    xs = _sc_scatter_rows(hpk, win(dest[:, 0]), win(dest[:, 1]), block_e.shape[0] * MOE_BLK)
    ys = _expert_call(block_e, n_used, xs, w_eg[l], w_eu[l], w_ed[l])
    yab = _sc_gather_rows(ys, jnp.concatenate([dest[:, 0], dest[:, 1]]))
    out_p, out_s = _combine_call(h, yab, route, vec(ln2_g[l]), vec(ln2_b[l]), Tp)

    y_prompt = out_p.reshape(B, S, D)
    y_sample = out_s.reshape(DB, DS, D)
    p_ckv = ckv_p.reshape(1, B, S, KV_LORA)
    p_kpe = kpe_p.reshape(1, B, S, MLA_ROPE)
    s_ckv = ckv_s.reshape(1, DB, DS, KV_LORA)
    s_kpe = kpe_s.reshape(1, DB, DS, MLA_ROPE)
    p_wkv = jnp.swapaxes(hT_p, -1, -2)[None]
    s_wkv = jnp.swapaxes(hT_s, -1, -2)[None]
    gp = S // SHIFT_GROUP
    p_sh = last_rows[gp - 1:B * gp:gp][None]
    s_sh = last_rows[B * gp:][None]
    return (y_prompt, y_sample, p_ckv, p_kpe, p_wkv, p_sh, s_ckv, s_kpe, s_wkv, s_sh)
```

```python
import functools
import math

import numpy as np
import jax
import jax.numpy as jnp
from jax import lax
from jax.experimental import pallas as pl
from jax.experimental.pallas import tpu as pltpu
from jax.experimental.pallas import tpu_sc as plsc

F32 = jnp.float32
BF16 = jnp.bfloat16

D_MODEL = 1024
CHUNK = 64
MLA_HEADS = 8
MLA_NOPE = 64
MLA_ROPE = 32
MLA_V = 64
Q_LORA = 384
KV_LORA = 256
ROPE_BASE = 10000.0
MLA_IN = Q_LORA + KV_LORA + MLA_ROPE
MLA_SCALE = (MLA_NOPE + MLA_ROPE) ** -0.5
RWKV_HEADS = 8
RWKV_N = 64
RWKV_DIM = RWKV_HEADS * RWKV_N
DECAY_LORA = 64
AAA_LORA = 64
GATE_LORA = 128
RWKV_IN = 3 * RWKV_DIM + DECAY_LORA + AAA_LORA + GATE_LORA
N_GROUPS = 4
EXPERTS_PER_GROUP = 8
N_EXPERTS = N_GROUPS * EXPERTS_PER_GROUP
TOP_K = 2
D_EXPERT = 256
MOE_BLK = 512
LN_EPS = 1e-5
RMS_EPS = 1e-6
GN_EPS = 64e-5
NEG_INF = -1e30
DEPTH = 1
DN_ALPHA = (2 * DEPTH) ** 0.25

LANE = 128
HEAD_PAD = 128
PROJ_W = 768 + RWKV_IN
SHIFT_GROUP = 32
TOKEN_TILE = 512
ATTN_TQ = 1024
ATTN_TK = 512
V_ONE_LANE = (MLA_V, 0)
LOG2E = math.log2(math.e)
VMEM_LIMIT = 48 * 1024 * 1024
PROJ_VMEM_LIMIT = 56 * 1024 * 1024
SC_CORES = 2
SC_SUBCORES = 16
SC_WINDOW = 32
WKV_GROUP = 4
WKV_SUB = 2
WKV_PAR = 2
MIX_PARTS = 2
MOE_WAVE_TILES = 34


def _cparams(sem):
    return pltpu.CompilerParams(dimension_semantics=sem, vmem_limit_bytes=VMEM_LIMIT)


def _split3(x):
    hi = x.astype(BF16)
    r1 = x - hi.astype(F32)
    mid = r1.astype(BF16)
    lo = (r1 - mid.astype(F32)).astype(BF16)
    return hi, mid, lo


def _dot(a, b):
    return jnp.dot(a, b, preferred_element_type=F32)


def _dot_nt(a, b):
    return lax.dot_general(a, b, (((1,), (1,)), ((), ())), preferred_element_type=F32)


def _dot_exact_rhs(x, w):
    hi = x.astype(BF16)
    lo = (x - hi.astype(F32)).astype(BF16)
    return _dot(hi, w) + _dot(lo, w)


def _dot_exact_lhs(w, x):
    hi, mid, lo = _split3(x)
    return _dot(w, hi) + _dot(w, mid) + _dot(w, lo)


def _pack_bf16_pairs(x):
    n = x.shape[1] // 2
    bits = pltpu.bitcast(x.astype(BF16).astype(F32), jnp.int32)
    return (bits[:, :n] & jnp.int32(-65536)) | lax.shift_right_logical(bits[:, n:], jnp.int32(16))


def _unpack_bf16_pairs(p):
    hi = pltpu.bitcast(p & jnp.int32(-65536), F32)
    lo = pltpu.bitcast(lax.shift_left(p, jnp.int32(16)), F32)
    return jnp.concatenate([hi, lo], axis=1)


def _sigmoid(x):
    return 1.0 / (1.0 + jnp.exp(-x))


def _layer_norm(x, g, b):
    xc = x - jnp.mean(x, -1, keepdims=True)
    var = jnp.mean(xc * xc, -1, keepdims=True)
    return xc * lax.rsqrt(var + LN_EPS) * g + b


def _proj_kernel(xp_ref, xs_ref, rope_ref, rp_ref, rc_ref, w1_ref, gq_ref, gkv_ref, wqa_ref, wk_ref, pk_ref, wv_ref,
                 shift_ref, *rest, n_p, tm, seq_tiles):
    rwkv_w = rest[:10]
    q_ref, k_ref, v_ref, ckvp_ref, ckvs_ref, kpep_ref, kpes_ref, last_ref = rest[10:18]
    rwkv_out = rest[18:26]
    carry_scr = rest[26]
    i = pl.program_id(0)
    is_p = i < n_p
    x = jnp.where(is_p, xp_ref[...], xs_ref[...]).astype(BF16)
    proj = _dot(x, w1_ref[...])
    c_q = proj[:, :Q_LORA]
    c_kv = proj[:, Q_LORA:Q_LORA + KV_LORA]
    kp = proj[:, 640:768]
    pr = proj[:, 768:]

    ng = tm // SHIFT_GROUP
    pr3 = pr.reshape(ng, SHIFT_GROUP, RWKV_IN)
    last_ref[...] = pr3[:, SHIFT_GROUP - 1, :]
    rolled = pltpu.roll(pr, 1, 0).reshape(ng, SHIFT_GROUP, RWKV_IN)
    first_row = jnp.where(i % seq_tiles == 0, 0.0, carry_scr[...])
    bound = jnp.where(is_p, first_row[None], shift_ref[...])
    row_in_grp = lax.broadcasted_iota(jnp.int32, pr3.shape, 1)
    grp = lax.broadcasted_iota(jnp.int32, pr3.shape, 0)
    use_bound = (row_in_grp == 0) & (grp <= jnp.where(is_p, 0, ng))
    prev = jnp.where(use_bound, bound, rolled).reshape(tm, RWKV_IN)
    carry_scr[...] = pr[tm - 1:tm, :]

    cqn = c_q * lax.rsqrt(jnp.mean(c_q * c_q, -1, keepdims=True) + RMS_EPS) * gq_ref[...]
    ckv = c_kv * lax.rsqrt(jnp.mean(c_kv * c_kv, -1, keepdims=True) + RMS_EPS) * gkv_ref[...]

    rope = (_dot_exact_rhs(rope_ref[...], rp_ref[...]) + rc_ref[0:1, :]) * rc_ref[1:2, :]
    cq = rope[:, :LANE]
    sq_up = rope[:, LANE:2 * LANE]
    sq_dn = rope[:, 2 * LANE:3 * LANE]
    kt = rope[:, 3 * LANE:]
    prod = kp * kt
    kpe = prod[:, :MLA_ROPE] + prod[:, MLA_ROPE:2 * MLA_ROPE]

    @pl.when(is_p)
    def _():
        ckvp_ref[...] = ckv
        kpep_ref[...] = kpe

    @pl.when(jnp.logical_not(is_p))
    def _():
        ckvs_ref[...] = ckv
        kpes_ref[...] = kpe

    qa = _dot(cqn.astype(BF16), wqa_ref[...])
    half = MLA_ROPE // 2
    for h in range(MLA_HEADS):
        qh = qa[:, h * HEAD_PAD:(h + 1) * HEAD_PAD]
        rot = pltpu.roll(qh, HEAD_PAD - half, 1) * sq_up + pltpu.roll(qh, half, 1) * sq_dn
        q_ref[:, h * HEAD_PAD:(h + 1) * HEAD_PAD] = (qh * cq + rot).astype(BF16)

    ckv_b = ckv.astype(BF16)
    k = _dot(ckv_b, wk_ref[...]) + _dot(kpe.astype(BF16), pk_ref[...])
    k_ref[...] = k.astype(BF16)
    lane = lax.broadcasted_iota(jnp.int32, (1, MLA_HEADS * HEAD_PAD), 1)
    odd = (lane // HEAD_PAD) % 2
    one_lane = jnp.where(odd == 1, V_ONE_LANE[1], V_ONE_LANE[0])
    v_one = jnp.where(lane % HEAD_PAD == one_lane, 1.0, 0.0)
    v_ref[...] = (_dot(ckv_b, wv_ref[...]) + v_one).astype(BF16)
    _rwkv_heads(pr, prev, *rwkv_w, *rwkv_out)


def _split_rows(n_p, tile0=0):
    return ((lambda i: (jnp.minimum(i + tile0, n_p - 1), 0)),
            (lambda i: (jnp.maximum(i + tile0 - n_p, 0), 0)))


def _proj_call(xp, xs, rope, seq_tiles, rope_place, rope_rows, w1, gq, gkv, wqa, wk, pk, wv, shift, rwkv_w):
    Tp, Ts = xp.shape[0], xs.shape[0]
    T = Tp + Ts
    tm = TOKEN_TILE
    n_p = Tp // tm
    rope_tiles = seq_tiles
    row = lambda i: (i, 0)
    full = lambda i: (0, 0)
    row_p, row_s = _split_rows(n_p)
    wide = MLA_HEADS * HEAD_PAD
    ng = tm // SHIFT_GROUP
    once = pl.Buffered(1)
    vec = pl.BlockSpec((1, RWKV_DIM), full, pipeline_mode=once)
    rwkv_specs = [pl.BlockSpec((1, RWKV_IN), full, pipeline_mode=once), vec, vec, vec, vec, vec,
                  pl.BlockSpec((DECAY_LORA, RWKV_DIM), full, pipeline_mode=once),
                  pl.BlockSpec((AAA_LORA, RWKV_DIM), full, pipeline_mode=once),
                  pl.BlockSpec((GATE_LORA, RWKV_DIM), full, pipeline_mode=once),
                  pl.BlockSpec((RWKV_DIM, RWKV_DIM), full, pipeline_mode=once)]
    tok = pl.BlockSpec((tm, RWKV_DIM), row)
    return pl.pallas_call(
        functools.partial(_proj_kernel, n_p=n_p, tm=tm, seq_tiles=seq_tiles),
        grid=(T // tm,),
        in_specs=[
            pl.BlockSpec((tm, D_MODEL), row_p),
            pl.BlockSpec((tm, D_MODEL), row_s),
            pl.BlockSpec((tm, MLA_ROPE), lambda i: (jnp.where(i < n_p, i % rope_tiles, rope_tiles), 0)),
            pl.BlockSpec((MLA_ROPE, 4 * LANE), full, pipeline_mode=once),
            pl.BlockSpec((2, 4 * LANE), full, pipeline_mode=once),
            pl.BlockSpec((D_MODEL, PROJ_W), full, pipeline_mode=once),
            pl.BlockSpec((1, Q_LORA), full, pipeline_mode=once),
            pl.BlockSpec((1, KV_LORA), full, pipeline_mode=once),
            pl.BlockSpec((Q_LORA, wide), full, pipeline_mode=once),
            pl.BlockSpec((KV_LORA, wide), full, pipeline_mode=once),
            pl.BlockSpec((MLA_ROPE, wide), full, pipeline_mode=once),
            pl.BlockSpec((KV_LORA, wide), full, pipeline_mode=once),
            pl.BlockSpec((ng, 1, RWKV_IN), lambda i: (jnp.maximum(i - n_p, 0), 0, 0)),
        ] + rwkv_specs,
        out_specs=[
            pl.BlockSpec((tm, wide), row),
            pl.BlockSpec((tm, wide), row),
            pl.BlockSpec((tm, wide), row),
            pl.BlockSpec((tm, KV_LORA), row_p),
            pl.BlockSpec((tm, KV_LORA), row_s),
            pl.BlockSpec((tm, MLA_ROPE), row_p),
            pl.BlockSpec((tm, MLA_ROPE), row_s),
            pl.BlockSpec((ng, RWKV_IN), row),
        ] + [tok] * 8,
        out_shape=[
            jax.ShapeDtypeStruct((T, wide), BF16),
            jax.ShapeDtypeStruct((T, wide), BF16),
            jax.ShapeDtypeStruct((T, wide), BF16),
            jax.ShapeDtypeStruct((Tp, KV_LORA), F32),
            jax.ShapeDtypeStruct((Ts, KV_LORA), F32),
            jax.ShapeDtypeStruct((Tp, MLA_ROPE), F32),
            jax.ShapeDtypeStruct((Ts, MLA_ROPE), F32),
            jax.ShapeDtypeStruct((T // SHIFT_GROUP, RWKV_IN), F32),
        ] + [jax.ShapeDtypeStruct((T, RWKV_DIM), F32)] * 8,
        scratch_shapes=[pltpu.VMEM((1, RWKV_IN), F32)],
        compiler_params=pltpu.CompilerParams(dimension_semantics=("arbitrary",), vmem_limit_bytes=PROJ_VMEM_LIMIT),
        name="proj",
    )(xp, xs, rope, rope_place, rope_rows, w1, gq, gkv, wqa, wk, pk, wv, shift, *rwkv_w)


def _attn_kernel(q_ref, k_ref, v_ref, o_ref, m_scr, acc_scr, *, tq, tk):
    qi = pl.program_id(2)
    m_scr[...] = jnp.full(m_scr.shape, NEG_INF, F32)
    acc_scr[...] = jnp.zeros(acc_scr.shape, F32)
    n_diag = tq // tk

    H = range(2)
    sls = [slice(h * HEAD_PAD, (h + 1) * HEAD_PAD) for h in H]

    def scores(k0, width, rows):
        return [_dot_nt(q_ref[rows, sl], k_ref[pl.ds(k0, width), sl]) for sl in sls]

    def accumulate(s, k0, width, rows):
        m_prev = [m_scr[h, rows, :] for h in H]
        m_new = [jnp.maximum(m_prev[h], jnp.max(s[h], axis=1, keepdims=True)) for h in H]
        pexp = [jnp.exp2(s[h] - jnp.tile(m_new[h], (1, width // LANE))).astype(BF16) for h in H]
        pv = [_dot(pexp[h], v_ref[pl.ds(k0, width), sls[h]]) for h in H]
        for h in H:
            acc_scr[h, rows, :] = jnp.exp2(m_prev[h] - m_new[h]) * acc_scr[h, rows, :] + pv[h]
            m_scr[h, rows, :] = m_new[h]

    def kv_block(k0, width, rows, masked):
        s = scores(k0, width, rows)
        if masked:
            n_rows = rows.stop - rows.start
            r = lax.broadcasted_iota(jnp.int32, (n_rows, width), 0) // CHUNK
            c = lax.broadcasted_iota(jnp.int32, (n_rows, width), 1) // CHUNK
            s = [jnp.where(c <= r, s[h], NEG_INF) for h in H]
        accumulate(s, k0, width, rows)

    all_rows = slice(0, tq)

    def pair(t, carry):
        k0a = pl.multiple_of(2 * t * tq, tq)
        k0b = pl.multiple_of(k0a + tq, tq)
        s_a = scores(k0a, tq, all_rows)
        s_b = scores(k0b, tq, all_rows)
        accumulate(s_a, k0a, tq, all_rows)
        accumulate(s_b, k0b, tq, all_rows)
        return carry

    lax.fori_loop(0, qi // 2, pair, 0)

    @pl.when(qi % 2 == 1)
    def _():
        kv_block(pl.multiple_of((qi - 1) * tq, tq), tq, all_rows, False)

    for d in range(n_diag):
        k0 = pl.multiple_of(qi * tq + d * tk, tk)
        kv_block(k0, tk, slice(d * tk, (d + 1) * tk), True)
        if (d + 1) * tk < tq:
            kv_block(k0, tk, slice((d + 1) * tk, tq), False)
    acc0, acc1 = acc_scr[0], acc_scr[1]
    lane = lax.broadcasted_iota(jnp.int32, acc0.shape, 1)
    l0 = acc0[:, V_ONE_LANE[0]:V_ONE_LANE[0] + 1]
    l1 = acc1[:, V_ONE_LANE[1]:V_ONE_LANE[1] + 1]
    o_ref[...] = jnp.where(lane < MLA_V, acc0 / l0, acc1 / l1).astype(o_ref.dtype)


def _attn_call(q, k, v, n_batch, seq):
    tq, tk = ATTN_TQ, ATTN_TK
    nq = seq // tq
    hp = MLA_HEADS // 2
    resident = pl.BlockSpec((seq, 2 * HEAD_PAD), lambda b, h, i: (b, h), pipeline_mode=pl.Buffered(1))
    return pl.pallas_call(
        functools.partial(_attn_kernel, tq=tq, tk=tk),
        grid=(n_batch, hp, nq),
        in_specs=[
            pl.BlockSpec((tq, 2 * HEAD_PAD), lambda b, h, i: (b * nq + i, h)),
            resident,
            resident,
        ],
        out_specs=pl.BlockSpec((tq, LANE), lambda b, h, i: (b * nq + i, h)),
        out_shape=jax.ShapeDtypeStruct((n_batch * seq, hp * LANE), BF16),
        scratch_shapes=[
            pltpu.VMEM((2, tq, LANE), F32),
            pltpu.VMEM((2, tq, LANE), F32),
        ],
        compiler_params=_cparams(("parallel", "parallel", "arbitrary")),
        name="attn",
    )(q, k, v)


def _mla_sample_kernel(q_ref, cn_ref, kn_ref, cp_ref, kp_ref, wuk_ref, wuv_ref, o_ref, *, past, dec):
    cp = cp_ref[0].astype(BF16)
    kp = kp_ref[0].astype(BF16)
    cn = cn_ref[...].astype(BF16)
    kn = kn_ref[...].astype(BF16)
    R = MLA_HEADS * dec
    qrow = (past + lax.broadcasted_iota(jnp.int32, (R, past), 0) % dec) // CHUNK
    vis_p = (lax.broadcasted_iota(jnp.int32, (R, past), 1) // CHUNK) <= qrow
    qrow_n = (past + lax.broadcasted_iota(jnp.int32, (R, dec), 0) % dec) // CHUNK
    vis_n = ((past + lax.broadcasted_iota(jnp.int32, (R, dec), 1)) // CHUNK) <= qrow_n
    q_lat, qp = [], []
    for h in range(MLA_HEADS):
        qn = q_ref[:, h * HEAD_PAD:h * HEAD_PAD + MLA_NOPE]
        qp.append(q_ref[:, h * HEAD_PAD + MLA_NOPE:h * HEAD_PAD + MLA_NOPE + MLA_ROPE])
        q_lat.append(_dot(qn, wuk_ref[h]).astype(BF16))
    q_lat = jnp.concatenate(q_lat, axis=0)
    qp = jnp.concatenate(qp, axis=0)
    s_p = jnp.where(vis_p, _dot_nt(q_lat, cp) + _dot_nt(qp, kp), NEG_INF)
    s_n = jnp.where(vis_n, _dot_nt(q_lat, cn) + _dot_nt(qp, kn), NEG_INF)
    m = jnp.maximum(jnp.max(s_p, axis=1, keepdims=True), jnp.max(s_n, axis=1, keepdims=True))
    e_p = jnp.exp2(s_p - m)
    e_n = jnp.exp2(s_n - m)
    l = jnp.sum(e_p, axis=1, keepdims=True) + jnp.sum(e_n, axis=1, keepdims=True)
    o_lat = ((_dot(e_p.astype(BF16), cp) + _dot(e_n.astype(BF16), cn)) / l).astype(BF16)
    out = jnp.zeros((dec, MLA_HEADS * MLA_V), F32)
    for h in range(MLA_HEADS):
        out = out + _dot(o_lat[h * dec:(h + 1) * dec], wuv_ref[h])
    o_ref[...] = out.astype(o_ref.dtype)


def _mla_sample_call(q, ckv, kpe, cache_ckv, cache_kpe, wuk, wuv, row0, n_seq, dec):
    past = cache_ckv.shape[1]
    blk0 = row0 // dec
    wide = MLA_HEADS * HEAD_PAD
    return pl.pallas_call(
        functools.partial(_mla_sample_kernel, past=past, dec=dec),
        grid=(n_seq,),
        in_specs=[
            pl.BlockSpec((dec, wide), lambda b: (blk0 + b, 0)),
            pl.BlockSpec((dec, KV_LORA), lambda b: (b, 0)),
            pl.BlockSpec((dec, MLA_ROPE), lambda b: (b, 0)),
            pl.BlockSpec((1, past, KV_LORA), lambda b: (b, 0, 0)),
            pl.BlockSpec((1, past, MLA_ROPE), lambda b: (b, 0, 0)),
            pl.BlockSpec((MLA_HEADS, MLA_NOPE, KV_LORA), lambda b: (0, 0, 0)),
            pl.BlockSpec((MLA_HEADS, KV_LORA, MLA_HEADS * MLA_V), lambda b: (0, 0, 0)),
        ],
        out_specs=pl.BlockSpec((dec, MLA_HEADS * MLA_V), lambda b: (b, 0)),
        out_shape=jax.ShapeDtypeStruct((n_seq * dec, MLA_HEADS * MLA_V), BF16),
        compiler_params=_cparams(("parallel",)),
        name="mla_sample",
    )(q, ckv, kpe, cache_ckv, cache_kpe, wuk, wuv)


def _rwkv_heads(pr, prev, mu_ref, w0_ref, a0_ref, kk_ref, ka_ref, rk_ref, w2_ref, a2_ref, g2_ref,
                seg_ref, r_ref, lw_ref, kh_ref, v_ref, na_ref, b_ref, bonus_ref, g_ref):
    u = pr + mu_ref[...] * (prev - pr)
    o1, o2, o3 = RWKV_DIM, 2 * RWKV_DIM, 3 * RWKV_DIM
    o4, o5 = o3 + DECAY_LORA, o3 + DECAY_LORA + AAA_LORA
    r, k, v = u[:, :o1], u[:, o1:o2], u[:, o2:o3]
    w_lo, a_lo, g_lo = u[:, o3:o4], u[:, o4:o5], u[:, o5:]
    wl = w0_ref[...] + _dot(jnp.tanh(w_lo).astype(BF16), w2_ref[...])
    lw_ref[...] = -math.exp(-0.5) * _sigmoid(wl)
    a = _sigmoid(a0_ref[...] + _dot(a_lo.astype(BF16), a2_ref[...]))
    g_ref[...] = _dot(_sigmoid(g_lo).astype(BF16), g2_ref[...])
    seg = seg_ref[...]
    kk = k * kk_ref[...]
    kk = kk / jnp.maximum(jnp.sqrt(_dot_exact_rhs(kk * kk, seg)), 1e-12)
    kh = k * (1.0 + (a - 1.0) * ka_ref[...])
    r_ref[...] = r
    kh_ref[...] = kh
    v_ref[...] = v
    na_ref[...] = -kk
    b_ref[...] = kk * a
    bonus_ref[...] = _dot_exact_rhs(r * kh * rk_ref[...], seg) * v


def _wkv_kernel(*refs, C, n_sub, n_par):
    GW = WKV_GROUP * RWKV_N
    R = WKV_GROUP * C
    n_grp = RWKV_HEADS // WKV_GROUP
    n_lev = int(round(math.log2(C))) - 1
    c = pl.program_id(1)
    tok = [refs[6 * p:6 * p + 6] for p in range(n_par)]
    h0_ref, y_ref, hT_ref, h_scr = refs[6 * n_par:]
    n_state = n_par * n_grp

    def head_block(hh):
        return slice(hh * RWKV_N, (hh + 1) * RWKV_N)

    @pl.when(c == 0)
    def _():
        h_scr[...] = jnp.zeros(h_scr.shape, F32)
        for p in range(n_par):
            for hd in range(RWKV_HEADS):
                g, hh = divmod(hd, WKV_GROUP)
                h_scr[p * n_grp + g, head_block(hh), head_block(hh)] = h0_ref[p, hd]

    row = lax.broadcasted_iota(jnp.int32, (C, C), 0)
    col = lax.broadcasted_iota(jnp.int32, (C, C), 1)
    tri = jnp.where(col <= row, 1.0, 0.0).astype(BF16)
    rr = lax.broadcasted_iota(jnp.int32, (R, R), 0)
    cc = lax.broadcasted_iota(jnp.int32, (R, R), 1)
    same = (rr // C) == (cc // C)
    ti = lax.broadcasted_iota(jnp.int32, (C, R), 0)
    si = lax.broadcasted_iota(jnp.int32, (C, R), 1) % C
    strict4 = si < ti
    lower4 = si <= ti
    eye4 = jnp.where(si == ti, 1.0, 0.0)
    keep = (lax.broadcasted_iota(jnp.int32, (R, GW), 0) // C
            == lax.broadcasted_iota(jnp.int32, (R, GW), 1) // RWKV_N)
    gr = lax.broadcasted_iota(jnp.int32, (GW, GW), 0)
    gc = lax.broadcasted_iota(jnp.int32, (GW, GW), 1)
    eye_g = gr == gc
    same_head = (gr // RWKV_N) == (gc // RWKV_N)
    eye_g_bf = jnp.where(eye_g, 1.0, 0.0).astype(BF16)

    def rows4(x4):
        return jnp.concatenate([x4] * WKV_GROUP, axis=0)

    def stack(x4):
        return jnp.where(keep, rows4(x4), jnp.zeros((), x4.dtype))


    sls = [slice(g * GW, (g + 1) * GW) for g in range(n_grp)]
    J = [(ci, q) for ci in range(n_sub) for q in range(n_state)]
    ops, p_end = {}, {}
    for p, ci in [(p, ci) for p in range(n_par) for ci in range(n_sub)]:
        r_ref, lw_ref, k_ref, v_ref, a_ref, b_ref = tok[p]
        rows = slice(ci * C, (ci + 1) * C)
        lw = lw_ref[rows, :]
        cum = _dot_exact_lhs(tri, lw)
        cum_end = cum[C - 1:C, :]
        e_neg = jnp.exp(-cum)
        e_end = jnp.exp(cum_end - cum)
        b_in = b_ref[rows, :]
        k_in = k_ref[rows, :]
        full = ((a_ref[rows, :] * jnp.exp(cum - lw)).astype(BF16),
                (r_ref[rows, :] * jnp.exp(cum)).astype(BF16),
                (b_in * e_neg).astype(BF16), (k_in * e_neg).astype(BF16),
                (b_in * e_end).astype(BF16), (k_in * e_end).astype(BF16),
                v_ref[rows, :].astype(BF16))
        p_end[p, ci] = jnp.exp(cum_end)
        for g in range(n_grp):
            ops[ci, p * n_grp + g] = [t[:, sls[g]] for t in full]
    a4, r4, b4, k4, be4, ke4, v4 = [{j: ops[j][i] for j in J} for i in range(7)]
    v_s = {j: stack(v4[j]) for j in J}
    m = {j: _dot_nt(jnp.concatenate([a4[j], r4[j]], axis=0),
                    jnp.concatenate([stack(b4[j]), stack(k4[j])], axis=0)) for j in J}
    l4 = {j: jnp.where(strict4, m[j][:C, :R], 0.0) for j in J}
    a_ak = {j: jnp.where(strict4, m[j][:C, R:], 0.0).astype(BF16) for j in J}
    a_rb = {j: jnp.where(lower4, m[j][C:, :R], 0.0).astype(BF16) for j in J}
    a_rk = {j: jnp.where(lower4, m[j][C:, R:], 0.0).astype(BF16) for j in J}
    def block_diag(x4):
        return jnp.where(same, rows4(x4.astype(BF16)), jnp.zeros((), BF16))

    t4 = {j: eye4 + l4[j] for j in J}
    l_bd = {j: block_diag(l4[j]) for j in J}
    for _ in range(n_lev):
        l4 = {j: _dot(l4[j].astype(BF16), l_bd[j]) for j in J}
        l_bd = {j: block_diag(l4[j]) for j in J}
        t4 = {j: t4[j] + _dot(t4[j].astype(BF16), l_bd[j]) for j in J}
    t_b = {j: t4[j].astype(BF16) for j in J}
    bke_t = {j: _dot_nt(eye_g_bf, jnp.concatenate([be4[j], ke4[j]], axis=0)).astype(BF16) for j in J}

    G = range(n_state)
    h_cur = [h_scr[q] for q in G]
    for ci in range(n_sub):
        rows = slice(ci * C, (ci + 1) * C)
        h0_b = [h_cur[g].astype(BF16) for g in G]
        x4 = [_dot(a4[ci, g], h0_b[g]) + _dot(a_ak[ci, g], v_s[ci, g]) for g in G]
        u4 = [_dot(t_b[ci, g], stack(x4[g].astype(BF16))).astype(BF16) for g in G]
        y4 = [_dot(r4[ci, g], h0_b[g]) + _dot(a_rb[ci, g], stack(u4[g])) + _dot(a_rk[ci, g], v_s[ci, g])
              for g in G]
        uv4 = [jnp.concatenate([u4[g], v4[ci, g]], axis=0) for g in G]
        h_add = [jnp.where(same_head, _dot(bke_t[ci, g], uv4[g]), 0.0) for g in G]
        for q in G:
            p, g = divmod(q, n_grp)
            y_ref[p, rows, sls[g]] = y4[q]
            p_col = jnp.sum(jnp.where(eye_g, p_end[p, ci][:, sls[g]], 0.0), axis=1, keepdims=True)
            h_cur[q] = p_col * h_cur[q] + h_add[q]

    for q in G:
        h_scr[q] = h_cur[q]

    @pl.when(c == pl.num_programs(1) - 1)
    def _():
        for p in range(n_par):
            for hd in range(RWKV_HEADS):
                g, hh = divmod(hd, WKV_GROUP)
                hT_ref[p, hd] = h_scr[p * n_grp + g, head_block(hh), head_block(hh)]


def _wkv_call(arrs, h0, row0, n_seq, n_chunk, C, n_sub, n_par):
    rows = C * n_sub
    steps = n_chunk // n_sub
    blk0 = row0 // rows
    GW = WKV_GROUP * RWKV_N
    n_grp = RWKV_HEADS // WKV_GROUP
    assert n_seq % n_par == 0 and n_chunk % n_sub == 0
    tok = [pl.BlockSpec((rows, RWKV_DIM), lambda b, c, p=p: (blk0 + (b * n_par + p) * steps + c, 0))
           for p in range(n_par)]
    st = pl.BlockSpec((n_par, RWKV_HEADS, RWKV_N, RWKV_N), lambda b, c: (b, 0, 0, 0))
    y, h_fin = pl.pallas_call(
        functools.partial(_wkv_kernel, C=C, n_sub=n_sub, n_par=n_par),
        grid=(n_seq // n_par, steps),
        in_specs=[tok[p] for p in range(n_par) for _ in range(6)] + [st],
        out_specs=[pl.BlockSpec((n_par, rows, RWKV_DIM), lambda b, c: (b, c, 0)), st],
        out_shape=[
            jax.ShapeDtypeStruct((n_seq, n_chunk * C, RWKV_DIM), F32),
            jax.ShapeDtypeStruct((n_seq, RWKV_HEADS, RWKV_N, RWKV_N), F32),
        ],
        scratch_shapes=[pltpu.VMEM((n_par * n_grp, GW, GW), F32)],
        compiler_params=_cparams(("parallel", "arbitrary")),
        name="wkv_c%d" % C,
    )(*(list(arrs) * n_par), h0)
    return y.reshape(n_seq * n_chunk * C, RWKV_DIM), h_fin


def _mix_kernel(xp_ref, xs_ref, attnp_ref, attns_ref, yp_ref, ys_ref, bonus_ref, g_ref, seg_ref, lng_ref,
                lnb_ref, woa_ref, wob_ref, g1_ref, b1_ref, wr_ref, br_ref, tri_ref, h_ref, hpk_ref, route_ref,
                count_ref, *, n_p, tm):
    is_p = pl.program_id(0) < n_p
    P = range(MIX_PARTS)
    bands = [slice(k * tm // MIX_PARTS, (k + 1) * tm // MIX_PARTS) for k in P]
    seg = seg_ref[...]
    inv_n = 1.0 / RWKV_N
    y = [jnp.where(is_p, yp_ref[r, :], ys_ref[r, :]) for r in bands]
    yc = [y[k] - _dot_exact_rhs(y[k], seg) * inv_n for k in P]
    var = [_dot_exact_rhs(yc[k] * yc[k], seg) * inv_n for k in P]
    yn = [yc[k] * lax.rsqrt(var[k] + GN_EPS) * lng_ref[...] + lnb_ref[...] for k in P]
    rw = [((yn[k] + bonus_ref[bands[k], :]) * g_ref[bands[k], :]).astype(BF16) for k in P]
    attn = [jnp.where(is_p, attnp_ref[r, :], attns_ref[r, :]) for r in bands]
    m = [_dot(attn[k], woa_ref[...]) + _dot(rw[k], wob_ref[...]) for k in P]
    x = [jnp.where(is_p, xp_ref[r, :], xs_ref[r, :]) for r in bands]
    h = [_layer_norm(DN_ALPHA * x[k] + m[k], g1_ref[...], b1_ref[...]) for k in P]
    for k in P:
        h_ref[bands[k], :] = h[k]
        hpk_ref[bands[k], :] = _pack_bf16_pairs(h[k])

    h_hi = [h[k].astype(BF16) for k in P]
    h_lo = [(h[k] - h_hi[k].astype(F32)).astype(BF16) for k in P]
    logits = [_dot(h_hi[k], wr_ref[0]) + _dot(h_lo[k], wr_ref[0]) + _dot(h_hi[k], wr_ref[1]) + br_ref[...]
              for k in P]
    lane = lax.broadcasted_iota(jnp.int32, logits[0].shape, 1)
    big = jnp.int32(LANE)

    def route_band(lg):
        gl = jnp.where(lane < N_GROUPS, lg, NEG_INF)
        gmax = jnp.max(gl, axis=1, keepdims=True)
        grp = jnp.min(jnp.where(gl == gmax, lane, big), axis=1, keepdims=True)
        p_grp = 1.0 / jnp.sum(jnp.exp(gl - gmax), axis=1, keepdims=True)
        e_idx = lane - N_GROUPS
        in_grp = (lane >= N_GROUPS) & (lane < N_GROUPS + N_EXPERTS) & ((e_idx // EXPERTS_PER_GROUP) == grp)
        el = jnp.where(in_grp, lg, NEG_INF)
        m1 = jnp.max(el, axis=1, keepdims=True)
        i1 = jnp.min(jnp.where(el == m1, lane, big), axis=1, keepdims=True)
        el2 = jnp.where(lane == i1, NEG_INF, el)
        m2 = jnp.max(el2, axis=1, keepdims=True)
        i2 = jnp.min(jnp.where(el2 == m2, lane, big), axis=1, keepdims=True)
        t = jnp.exp(m2 - m1)
        g1 = p_grp / (1.0 + t)
        return i1, i2, g1, g1 * t

    routed = [route_band(logits[k]) for k in P]
    chosen = jnp.concatenate(
        [jnp.where(lane == routed[k][0], 1.0, jnp.where(lane == routed[k][1], 1.0, 0.0)) for k in P], axis=0)
    before = _dot(tri_ref[...], chosen.astype(BF16))
    count_ref[0] = jnp.sum(chosen, axis=0, keepdims=True)
    for k in P:
        i1, i2, g1, g2 = routed[k]
        bef = before[bands[k]]
        r1 = jnp.sum(jnp.where(lane == i1, bef, 0.0), axis=1, keepdims=True)
        r2 = jnp.sum(jnp.where(lane == i2, bef, 0.0), axis=1, keepdims=True)
        cols = ((i1 - N_GROUPS).astype(F32), (i2 - N_GROUPS).astype(F32), g1, g2, r1, r2)
        route = jnp.zeros(lane.shape, F32)
        for j, col in enumerate(cols):
            route = jnp.where(lane == j, col, route)
        route_ref[bands[k], :] = route


def _mix_call(xp, xs, attn_p, attn_s, y_p, y_s, bonus, g, seg, lnx_g, lnx_b, woa, wob, ln1_g, ln1_b, wr, br,
              tile0, n_tiles):
    tm = TOKEN_TILE
    T = n_tiles * tm
    n_p = xp.shape[0] // tm
    row = lambda i: (i, 0)
    full = lambda i: (0, 0)
    row_p, row_s = _split_rows(n_p, tile0)
    half = pl.BlockSpec((tm, RWKV_DIM), lambda i: (i + tile0, 0))
    vec5 = pl.BlockSpec((1, RWKV_DIM), full)
    vec10 = pl.BlockSpec((1, D_MODEL), full)
    idx = np.arange(tm)
    tri = jnp.asarray((idx[None, :] < idx[:, None]).astype(np.float32)).astype(BF16)
    return pl.pallas_call(
        functools.partial(_mix_kernel, n_p=n_p - tile0, tm=tm),
        grid=(n_tiles,),
        in_specs=[
            pl.BlockSpec((tm, D_MODEL), row_p), pl.BlockSpec((tm, D_MODEL), row_s),
            pl.BlockSpec((tm, RWKV_DIM), row_p), pl.BlockSpec((tm, RWKV_DIM), row_s),
            pl.BlockSpec((tm, RWKV_DIM), row_p), pl.BlockSpec((tm, RWKV_DIM), row_s),
            half, half,
            pl.BlockSpec((RWKV_DIM, RWKV_DIM), full), vec5, vec5,
            pl.BlockSpec((RWKV_DIM, D_MODEL), full), pl.BlockSpec((RWKV_DIM, D_MODEL), full),
            vec10, vec10,
            pl.BlockSpec((2, D_MODEL, LANE), lambda i: (0, 0, 0)), pl.BlockSpec((1, LANE), full),
            pl.BlockSpec((tm, tm), full),
        ],
        out_specs=[pl.BlockSpec((tm, D_MODEL), row), pl.BlockSpec((tm, D_MODEL // 2), row),
                   pl.BlockSpec((tm, LANE), row), pl.BlockSpec((1, 1, LANE), lambda i: (i, 0, 0))],
        out_shape=[jax.ShapeDtypeStruct((T, D_MODEL), F32), jax.ShapeDtypeStruct((T, D_MODEL // 2), jnp.int32),
                   jax.ShapeDtypeStruct((T, LANE), F32), jax.ShapeDtypeStruct((T // tm, 1, LANE), F32)],
        compiler_params=_cparams(("parallel",)),
        name="mix",
    )(xp, xs, attn_p, attn_s, y_p, y_s, bonus, g, seg, lnx_g, lnx_b, woa, wob, ln1_g, ln1_b, wr, br, tri)


def _expert_kernel(be_ref, nu_ref, xs_ref, wg_ref, wu_ref, wd_ref, ys_ref, wgu_b, wd_b):
    i = pl.program_id(0)

    @pl.when((i == 0) | (be_ref[i] != be_ref[jnp.maximum(i - 1, 0)]))
    def _():
        wgu_b[:, :D_EXPERT] = wg_ref[0].astype(BF16)
        wgu_b[:, D_EXPERT:] = wu_ref[0].astype(BF16)
        wd_b[...] = wd_ref[0].astype(BF16)

    @pl.when(i < nu_ref[0])
    def _():
        P = range(2)
        bands = [slice(k * MOE_BLK // 2, (k + 1) * MOE_BLK // 2) for k in P]
        xb = [_unpack_bf16_pairs(xs_ref[r, :]).astype(BF16) for r in bands]
        gu = [_dot(xb[k], wgu_b[...]) for k in P]
        act = [(gu[k][:, :D_EXPERT] * _sigmoid(gu[k][:, :D_EXPERT]) * gu[k][:, D_EXPERT:]).astype(BF16)
               for k in P]
        out = [_dot(act[k], wd_b[...]) for k in P]
        for k in P:
            ys_ref[bands[k], :] = _pack_bf16_pairs(out[k])

    @pl.when(i >= nu_ref[0])
    def _():
        ys_ref[...] = jnp.zeros(ys_ref.shape, ys_ref.dtype)


def _expert_call(block_e, n_used, xs, wg, wu, wd):
    n_blk = xs.shape[0] // MOE_BLK
    grid_spec = pltpu.PrefetchScalarGridSpec(
        num_scalar_prefetch=2,
        grid=(n_blk,),
        in_specs=[
            pl.BlockSpec((MOE_BLK, D_MODEL // 2), lambda i, be, nu: (i, 0)),
            pl.BlockSpec((1, D_MODEL, D_EXPERT), lambda i, be, nu: (be[i], 0, 0)),
            pl.BlockSpec((1, D_MODEL, D_EXPERT), lambda i, be, nu: (be[i], 0, 0)),
            pl.BlockSpec((1, D_EXPERT, D_MODEL), lambda i, be, nu: (be[i], 0, 0)),
        ],
        out_specs=pl.BlockSpec((MOE_BLK, D_MODEL // 2), lambda i, be, nu: (i, 0)),
        scratch_shapes=[pltpu.VMEM((D_MODEL, 2 * D_EXPERT), BF16), pltpu.VMEM((D_EXPERT, D_MODEL), BF16)],
    )
    return pl.pallas_call(
        _expert_kernel,
        grid_spec=grid_spec,
        out_shape=jax.ShapeDtypeStruct((n_blk * MOE_BLK, D_MODEL // 2), jnp.int32),
        compiler_params=_cparams(("arbitrary",)),
        name="experts",
    )(block_e, n_used, xs, wg, wu, wd)


def _combine_kernel(h_ref, ya_ref, yb_ref, route_ref, g2_ref, b2_ref, *rest, n_p, has_prev, has_prompt,
                    has_sample):
    outs = list(rest[1:] if has_prev else rest)
    i = pl.program_id(0)
    route = route_ref[...]
    f = _unpack_bf16_pairs(ya_ref[...]) * route[:, 2:3] + _unpack_bf16_pairs(yb_ref[...]) * route[:, 3:4]
    out = _layer_norm(DN_ALPHA * h_ref[...] + f, g2_ref[...], b2_ref[...])

    if has_prompt:
        op_ref = outs.pop(0)

        @pl.when(i < n_p)
        def _():
            op_ref[...] = out

    if has_sample:
        os_ref = outs.pop(0)

        @pl.when(i >= n_p)
        def _():
            os_ref[...] = out


def _combine_call(h, yab, route, ln2_g, ln2_b, t_prompt, t_sample, tile0, out_p_prev):
    tm = TOKEN_TILE
    n_t, n_p = h.shape[0] // tm, t_prompt // tm
    has_prompt = tile0 < n_p
    has_sample = tile0 + n_t > n_p
    has_prev = has_prompt and out_p_prev is not None
    row = lambda i: (i, 0)
    full = lambda i: (0, 0)
    row_p, row_s = _split_rows(n_p, tile0)
    big = pl.BlockSpec((tm, D_MODEL), row)
    in_specs = [big, pl.BlockSpec((tm, D_MODEL // 2), row),
                pl.BlockSpec((tm, D_MODEL // 2), lambda i: (i + n_t, 0)), pl.BlockSpec((tm, LANE), row),
                pl.BlockSpec((1, D_MODEL), full), pl.BlockSpec((1, D_MODEL), full)]
    args = [h, yab, yab, route, ln2_g, ln2_b]
    out_specs, out_shape, aliases = [], [], {}
    if has_prompt:
        out_specs.append(pl.BlockSpec((tm, D_MODEL), row_p))
        out_shape.append(jax.ShapeDtypeStruct((t_prompt, D_MODEL), F32))
    if has_prev:
        in_specs.append(pl.BlockSpec(memory_space=pl.ANY))
        args.append(out_p_prev)
        aliases = {len(args) - 1: 0}
    if has_sample:
        out_specs.append(pl.BlockSpec((tm, D_MODEL), row_s))
        out_shape.append(jax.ShapeDtypeStruct((t_sample, D_MODEL), F32))
    outs = pl.pallas_call(
        functools.partial(_combine_kernel, n_p=n_p - tile0, has_prev=has_prev, has_prompt=has_prompt,
                          has_sample=has_sample),
        grid=(n_t,),
        in_specs=in_specs,
        out_specs=out_specs,
        out_shape=out_shape,
        input_output_aliases=aliases,
        compiler_params=_cparams(("arbitrary",)),
        name="combine",
    )(*args)
    out_p = outs[0] if has_prompt else out_p_prev
    out_s = outs[-1] if has_sample else None
    return out_p, out_s


def _prep_weights(w_in, w_uq, w_ukv):
    half = MLA_ROPE // 2
    kpe_w = w_in[:, Q_LORA + KV_LORA:MLA_IN]
    kpe_b = jnp.concatenate([-kpe_w[:, half:], kpe_w[:, :half]], axis=1)
    w1 = jnp.concatenate([w_in[:, :Q_LORA + KV_LORA], kpe_w, kpe_b,
                          jnp.zeros((D_MODEL, 64), F32), w_in[:, MLA_IN:]], axis=1).astype(BF16)
    pad_q = jnp.zeros((Q_LORA, MLA_HEADS, HEAD_PAD - MLA_NOPE - MLA_ROPE), F32)
    wqa = jnp.concatenate([w_uq, pad_q], axis=2).reshape(Q_LORA, -1).astype(BF16)
    w_uk, w_uv = w_ukv[:, :, :MLA_NOPE], w_ukv[:, :, MLA_NOPE:]
    wk = jnp.concatenate([w_uk, jnp.zeros((KV_LORA, MLA_HEADS, HEAD_PAD - MLA_NOPE), F32)], axis=2)
    wk = wk.reshape(KV_LORA, -1).astype(BF16)
    pk_np = np.zeros((MLA_ROPE, MLA_HEADS * HEAD_PAD), np.float32)
    for h in range(MLA_HEADS):
        for i in range(MLA_ROPE):
            pk_np[i, h * HEAD_PAD + MLA_NOPE + i] = 1.0
    pk = jnp.asarray(pk_np).astype(BF16)
    zv = jnp.zeros((KV_LORA, MLA_HEADS // 2, MLA_V), F32)
    wv4 = w_uv.reshape(KV_LORA, MLA_HEADS // 2, 2, MLA_V)
    wv = jnp.stack([jnp.concatenate([wv4[:, :, 0], zv], axis=2),
                    jnp.concatenate([zv, wv4[:, :, 1]], axis=2)], axis=2)
    wv = wv.reshape(KV_LORA, -1).astype(BF16)
    wuk = jnp.transpose(w_uk, (1, 2, 0)).astype(BF16)
    wuv_np = np.zeros((MLA_HEADS, MLA_HEADS * MLA_V), np.float32)
    for h in range(MLA_HEADS):
        wuv_np[h, h * MLA_V:(h + 1) * MLA_V] = 1.0
    wuv = jnp.transpose(w_uv, (1, 0, 2))
    wuv = (jnp.tile(wuv, (1, 1, MLA_HEADS)) * jnp.asarray(wuv_np)[:, None, :]).astype(BF16)
    return w1, wqa, wk, pk, wv, wuk, wuv


def _rope_table(pos):
    inv = ROPE_BASE ** (-jnp.arange(0, MLA_ROPE, 2, dtype=F32) / MLA_ROPE)
    ang = pos.astype(F32)[:, None] * inv[None, :]
    return jnp.concatenate([jnp.cos(ang), jnp.sin(ang)], axis=1)


def _rope_placement():
    half = MLA_ROPE // 2
    place = np.zeros((MLA_ROPE, 4 * LANE), np.float32)
    rows = np.zeros((2, 4 * LANE), np.float32)
    for i in range(half):
        c, s = i, half + i
        place[c, MLA_NOPE + i] = place[c, MLA_NOPE + half + i] = 1.0
        place[s, LANE + MLA_NOPE + i] = -1.0
        place[s, 2 * LANE + MLA_NOPE + half + i] = 1.0
        place[c, 3 * LANE + i] = place[c, 3 * LANE + half + i] = 1.0
        place[s, 3 * LANE + 2 * half + i] = place[s, 3 * LANE + 3 * half + i] = 1.0
    rows[0, :MLA_NOPE] = 1.0
    rows[1, :3 * LANE] = MLA_SCALE * LOG2E
    rows[1, 3 * LANE:] = 1.0
    return jnp.asarray(place).astype(BF16), jnp.asarray(rows)


def _seg_ones():
    idx = np.arange(RWKV_DIM) // RWKV_N
    return jnp.asarray((idx[:, None] == idx[None, :]).astype(np.float32)).astype(BF16)


def _dispatch(route, tile_counts, t_total):
    A = t_total * TOP_K
    n_tiles = tile_counts.shape[0]
    counts_te = tile_counts[:, 0, N_GROUPS:N_GROUPS + N_EXPERTS].astype(jnp.int32)
    counts = jnp.sum(counts_te, axis=0)
    blocks_per_e = (counts + MOE_BLK - 1) // MOE_BLK
    blk_end = jnp.cumsum(blocks_per_e)
    blk_start = blk_end - blocks_per_e
    tile_off = jnp.cumsum(counts_te, axis=0) - counts_te
    base = blk_start[None, :] * MOE_BLK + tile_off
    e = route[:, :TOP_K].astype(jnp.int32)
    rank = route[:, 4:4 + TOP_K].astype(jnp.int32)
    base_tok = jnp.repeat(base, t_total // n_tiles, axis=0)
    pick = e[:, :, None] == jnp.arange(N_EXPERTS, dtype=jnp.int32)[None, None, :]
    dest = jnp.sum(jnp.where(pick, base_tok[:, None, :], 0), axis=-1) + rank
    n_blk = -(-A // MOE_BLK) + N_EXPERTS
    blk = jnp.arange(n_blk, dtype=jnp.int32)
    block_e = jnp.minimum(jnp.sum((blk[:, None] >= blk_end[None, :]).astype(jnp.int32), axis=1),
                          N_EXPERTS - 1).astype(jnp.int32)
    n_used = blk_end[-1:].astype(jnp.int32)
    return dest, block_e, n_used


def _sc_gather_rows(table, idx):
    n_rows, width = idx.shape[0], table.shape[1]
    n_workers = SC_CORES * SC_SUBCORES
    per_worker = n_rows // n_workers
    assert n_rows % n_workers == 0 and per_worker % SC_WINDOW == 0
    mesh = plsc.VectorSubcoreMesh(core_axis_name="c", subcore_axis_name="s")

    @functools.partial(
        pl.kernel, mesh=mesh,
        out_type=jax.ShapeDtypeStruct((n_rows, width), table.dtype),
        scratch_types=[
            pltpu.VMEM((SC_WINDOW,), jnp.int32),
            pltpu.VMEM((SC_WINDOW, width), table.dtype),
            pltpu.SemaphoreType.DMA,
        ],
    )
    def gather(table_hbm, idx_hbm, out_hbm, idx_v, rows_v, sem):
        wid = lax.axis_index("s") * SC_CORES + lax.axis_index("c")
        base = wid * per_worker

        @pl.loop(0, per_worker // SC_WINDOW)
        def _(w):
            off = pl.multiple_of(base + w * SC_WINDOW, SC_WINDOW)
            pltpu.sync_copy(idx_hbm.at[pl.ds(off, SC_WINDOW)], idx_v)
            pltpu.async_copy(table_hbm.at[idx_v], rows_v, sem).wait()
            pltpu.sync_copy(rows_v, out_hbm.at[pl.ds(off, SC_WINDOW)])

    return gather(table, idx)


def _sc_scatter_rows(src, idx_a, idx_b, n_slots):
    n_rows, width = src.shape
    n_workers = SC_CORES * SC_SUBCORES
    per_worker = n_rows // n_workers
    assert n_rows % n_workers == 0 and per_worker % SC_WINDOW == 0
    n_win = per_worker // SC_WINDOW
    mesh = plsc.VectorSubcoreMesh(core_axis_name="c", subcore_axis_name="s")

    @functools.partial(
        pl.kernel, mesh=mesh,
        out_type=jax.ShapeDtypeStruct((n_slots, width), src.dtype),
        scratch_types=[
            pltpu.VMEM((1, SC_WINDOW), jnp.int32),
            pltpu.VMEM((1, SC_WINDOW), jnp.int32),
            pltpu.VMEM((SC_WINDOW, width), src.dtype),
        ],
    )
    def scatter(src_hbm, ia_hbm, ib_hbm, out_hbm, ia_v, ib_v, rows_v):
        wid = lax.axis_index("s") * SC_CORES + lax.axis_index("c")

        @pl.loop(0, n_win)
        def _(w):
            win = wid * n_win + w
            off = pl.multiple_of(win * SC_WINDOW, SC_WINDOW)
            pltpu.sync_copy(src_hbm.at[pl.ds(off, SC_WINDOW)], rows_v)
            pltpu.sync_copy(ia_hbm.at[pl.ds(win, 1)], ia_v)
            pltpu.sync_copy(ib_hbm.at[pl.ds(win, 1)], ib_v)
            pltpu.sync_copy(rows_v, out_hbm.at[ia_v.at[0]])
            pltpu.sync_copy(rows_v, out_hbm.at[ib_v.at[0]])

    return scatter(src, idx_a, idx_b)


def kernel(x_prompt, x_sample, cache_ckv, cache_kpe, state_wkv, state_shift, w_in, q_norm_g, kv_norm_g, w_uq,
           w_ukv, mu_shift, w0, w2, a0, a2, g2, k_k, k_a, r_k, lnx_g, lnx_b, w_o, ln1_g, ln1_b, w_gr, b_gr,
           w_er, b_er, w_eg, w_eu, w_ed, ln2_g, ln2_b):
    B, S, D = x_prompt.shape
    DB, DS, _ = x_sample.shape
    past = cache_ckv.shape[2]
    Tp, Ts = B * S, DB * DS
    T = Tp + Ts
    assert D == D_MODEL and DS == SHIFT_GROUP and S % ATTN_TQ == 0 and S % (CHUNK * WKV_SUB) == 0
    assert Tp % TOKEN_TILE == 0 and T % TOKEN_TILE == 0 and w_in.shape[0] == DEPTH

    l = 0
    xp, xs_in = x_prompt.reshape(Tp, D), x_sample.reshape(Ts, D)
    w1, wqa, wk, pk, wv, wuk, wuv = _prep_weights(w_in[l], w_uq[l], w_ukv[l])
    pos = jnp.concatenate([jnp.arange(S, dtype=jnp.int32),
                           jnp.tile(past + jnp.arange(DS, dtype=jnp.int32), TOKEN_TILE // DS)])
    rope = _rope_table(pos)

    seg = _seg_ones()
    vec = lambda a: a.reshape(1, -1)
    rwkv_w = (vec(mu_shift[l]), vec(w0[l]), vec(a0[l]), vec(k_k[l]), vec(k_a[l]), vec(r_k[l]),
              w2[l].astype(BF16), a2[l].astype(BF16), g2[l].astype(BF16), seg)
    (q, kcat, vcat, ckv_p, ckv_s, kpe_p, kpe_s, last_rows,
     r, lw, kh, v, na, b, bonus, g) = _proj_call(
        xp, xs_in, rope, S // TOKEN_TILE, *_rope_placement(), w1, q_norm_g[l][None], kv_norm_g[l][None],
        wqa, wk, pk, wv, state_shift[l][:, None, :], rwkv_w)

    attn_p = _attn_call(q, kcat, vcat, B, S)
    attn_s = _mla_sample_call(q, ckv_s, kpe_s, cache_ckv[l], cache_kpe[l], wuk, wuv, Tp, DB, DS)

    scan_in = (r, lw, kh, v, na, b)
    h0_p = jnp.zeros((B, RWKV_HEADS, RWKV_N, RWKV_N), F32)
    y_p, hT_p = _wkv_call(scan_in, h0_p, 0, B, S // CHUNK, CHUNK, WKV_SUB, math.gcd(B, WKV_PAR))
    h0_s = jnp.swapaxes(state_wkv[l], -1, -2)
    y_s, hT_s = _wkv_call(scan_in, h0_s, Tp, DB, 1, DS, 1, math.gcd(DB, WKV_SUB * WKV_PAR))

    wo_b = w_o[l].astype(BF16)
    wr = jnp.concatenate([w_gr[l], w_er[l], jnp.zeros((D, LANE - N_GROUPS - N_EXPERTS), F32)], axis=1)
    wr_hi = wr.astype(BF16)
    wr_lo = (wr - wr_hi.astype(F32)).astype(BF16)
    br = jnp.concatenate([b_gr[l], b_er[l], jnp.zeros((LANE - N_GROUPS - N_EXPERTS,), F32)])[None]
    n_tiles = T // TOKEN_TILE
    wave_tiles = [(t0, min(t0 + MOE_WAVE_TILES, n_tiles) - t0) for t0 in range(0, n_tiles, MOE_WAVE_TILES)]
    out_p, out_s = None, None
    for tile0, nt in wave_tiles:
        t_w = nt * TOKEN_TILE
        h, hpk, route, tile_counts = _mix_call(
            xp, xs_in, attn_p, attn_s, y_p, y_s, bonus, g, seg, vec(lnx_g[l]), vec(lnx_b[l]),
            wo_b[:MLA_HEADS * MLA_V], wo_b[MLA_HEADS * MLA_V:], vec(ln1_g[l]), vec(ln1_b[l]),
            jnp.stack([wr_hi, wr_lo]), br, tile0, nt)
        dest, block_e, n_used = _dispatch(route, tile_counts, t_w)
        win = lambda a: a.reshape(t_w // SC_WINDOW, SC_WINDOW)
        xs = _sc_scatter_rows(hpk, win(dest[:, 0]), win(dest[:, 1]), block_e.shape[0] * MOE_BLK)
        ys = _expert_call(block_e, n_used, xs, w_eg[l], w_eu[l], w_ed[l])
        yab = _sc_gather_rows(ys, jnp.concatenate([dest[:, 0], dest[:, 1]]))
        out_p, wave_s = _combine_call(h, yab, route, vec(ln2_g[l]), vec(ln2_b[l]), Tp, Ts, tile0, out_p)
        assert wave_s is None or out_s is None, "the sample tiles must fall inside one wave"
        out_s = wave_s if wave_s is not None else out_s

    y_prompt = out_p.reshape(B, S, D)
    y_sample = out_s.reshape(DB, DS, D)
    p_ckv = ckv_p.reshape(1, B, S, KV_LORA)
    p_kpe = kpe_p.reshape(1, B, S, MLA_ROPE)
    s_ckv = ckv_s.reshape(1, DB, DS, KV_LORA)
    s_kpe = kpe_s.reshape(1, DB, DS, MLA_ROPE)
    p_wkv = jnp.swapaxes(hT_p, -1, -2)[None]
    s_wkv = jnp.swapaxes(hT_s, -1, -2)[None]
    gp = S // SHIFT_GROUP
    p_sh = last_rows[gp - 1:B * gp:gp][None]
    s_sh = last_rows[B * gp:][None]
    return (y_prompt, y_sample, p_ckv, p_kpe, p_wkv, p_sh, s_ckv, s_kpe, s_wkv, s_sh)
```

```python
import functools
import math

import numpy as np
import jax
import jax.numpy as jnp
from jax import lax
from jax.experimental import pallas as pl
from jax.experimental.pallas import tpu as pltpu
from jax.experimental.pallas import tpu_sc as plsc

F32 = jnp.float32
BF16 = jnp.bfloat16

D_MODEL = 1024
CHUNK = 64
MLA_HEADS = 8
MLA_NOPE = 64
MLA_ROPE = 32
MLA_V = 64
Q_LORA = 384
KV_LORA = 256
ROPE_BASE = 10000.0
MLA_IN = Q_LORA + KV_LORA + MLA_ROPE
MLA_SCALE = (MLA_NOPE + MLA_ROPE) ** -0.5
RWKV_HEADS = 8
RWKV_N = 64
RWKV_DIM = RWKV_HEADS * RWKV_N
DECAY_LORA = 64
AAA_LORA = 64
GATE_LORA = 128
RWKV_IN = 3 * RWKV_DIM + DECAY_LORA + AAA_LORA + GATE_LORA
N_GROUPS = 4
EXPERTS_PER_GROUP = 8
N_EXPERTS = N_GROUPS * EXPERTS_PER_GROUP
TOP_K = 2
D_EXPERT = 256
MOE_BLK = 512
LN_EPS = 1e-5
RMS_EPS = 1e-6
GN_EPS = 64e-5
NEG_INF = -1e30
DEPTH = 1
DN_ALPHA = (2 * DEPTH) ** 0.25

LANE = 128
HEAD_PAD = 128
PROJ_KPE = Q_LORA + KV_LORA
PROJ_PR = PROJ_KPE + LANE
PROJ_W = PROJ_PR + RWKV_IN
SHIFT_GROUP = 32
TOKEN_TILE = 512
ATTN_TQ = 1024
ATTN_TK = 512
V_ONE_LANE = (MLA_V, 0)
LOG2E = math.log2(math.e)
VMEM_LIMIT = 48 * 1024 * 1024
PROJ_VMEM_LIMIT = 56 * 1024 * 1024
SC_CORES = 2
SC_SUBCORES = 16
SC_WINDOW = 32
WKV_GROUP = 4
WKV_SUB = 2
WKV_PAR = 2
MIX_PARTS = 2
MOE_WAVE_TILES = 34


def _cparams(sem):
    return pltpu.CompilerParams(dimension_semantics=sem, vmem_limit_bytes=VMEM_LIMIT)


def _split3(x):
    hi = x.astype(BF16)
    r1 = x - hi.astype(F32)
    mid = r1.astype(BF16)
    lo = (r1 - mid.astype(F32)).astype(BF16)
    return hi, mid, lo


def _dot(a, b):
    return jnp.dot(a, b, preferred_element_type=F32)


def _dot_nt(a, b):
    return lax.dot_general(a, b, (((1,), (1,)), ((), ())), preferred_element_type=F32)


def _dot_exact_rhs(x, w):
    hi = x.astype(BF16)
    lo = (x - hi.astype(F32)).astype(BF16)
    return _dot(hi, w) + _dot(lo, w)


def _dot_exact_lhs(w, x):
    hi, mid, lo = _split3(x)
    return _dot(w, hi) + _dot(w, mid) + _dot(w, lo)


def _pack_bf16_pairs(x):
    n = x.shape[1] // 2
    bits = pltpu.bitcast(x.astype(BF16).astype(F32), jnp.int32)
    return (bits[:, :n] & jnp.int32(-65536)) | lax.shift_right_logical(bits[:, n:], jnp.int32(16))


def _unpack_bf16_pairs(p):
    hi = pltpu.bitcast(p & jnp.int32(-65536), F32)
    lo = pltpu.bitcast(lax.shift_left(p, jnp.int32(16)), F32)
    return jnp.concatenate([hi, lo], axis=1)


def _sigmoid(x):
    return 1.0 / (1.0 + jnp.exp(-x))


def _layer_norm(x, g, b):
    xc = x - jnp.mean(x, -1, keepdims=True)
    var = jnp.mean(xc * xc, -1, keepdims=True)
    return xc * lax.rsqrt(var + LN_EPS) * g + b


def _proj_kernel(xp_ref, xs_ref, rope_ref, rp_ref, rc_ref, w1_ref, gq_ref, gkv_ref, wqa_ref, wk_ref, pk_ref, wv_ref,
                 shift_ref, *rest, n_p, tm, seq_tiles):
    rwkv_w = rest[:10]
    q_ref, k_ref, v_ref, ckvp_ref, ckvs_ref, kpep_ref, kpes_ref, last_ref = rest[10:18]
    rwkv_out = rest[18:26]
    carry_scr = rest[26]
    i = pl.program_id(0)
    is_p = i < n_p
    x = jnp.where(is_p, xp_ref[...], xs_ref[...]).astype(BF16)
    proj = _dot(x, w1_ref[...])
    c_q = proj[:, :Q_LORA]
    c_kv = proj[:, Q_LORA:Q_LORA + KV_LORA]
    kp = proj[:, PROJ_KPE:PROJ_PR]
    pr = proj[:, PROJ_PR:]

    ng = tm // SHIFT_GROUP
    pr3 = pr.reshape(ng, SHIFT_GROUP, RWKV_IN)
    last_ref[...] = pr3[:, SHIFT_GROUP - 1, :]
    rolled = pltpu.roll(pr, 1, 0).reshape(ng, SHIFT_GROUP, RWKV_IN)
    first_row = jnp.where(i % seq_tiles == 0, 0.0, carry_scr[...])
    bound = jnp.where(is_p, first_row[None], shift_ref[...])
    row_in_grp = lax.broadcasted_iota(jnp.int32, pr3.shape, 1)
    grp = lax.broadcasted_iota(jnp.int32, pr3.shape, 0)
    use_bound = (row_in_grp == 0) & (grp <= jnp.where(is_p, 0, ng))
    prev = jnp.where(use_bound, bound, rolled).reshape(tm, RWKV_IN)
    carry_scr[...] = pr[tm - 1:tm, :]

    cqn = c_q * lax.rsqrt(jnp.mean(c_q * c_q, -1, keepdims=True) + RMS_EPS) * gq_ref[...]
    ckv = c_kv * lax.rsqrt(jnp.mean(c_kv * c_kv, -1, keepdims=True) + RMS_EPS) * gkv_ref[...]

    rope = (_dot_exact_rhs(rope_ref[...], rp_ref[...]) + rc_ref[0:1, :]) * rc_ref[1:2, :]
    cq = rope[:, :LANE]
    sq_up = rope[:, LANE:2 * LANE]
    sq_dn = rope[:, 2 * LANE:3 * LANE]
    kt = rope[:, 3 * LANE:]
    prod = kp * kt
    kpe = prod[:, :MLA_ROPE] + prod[:, MLA_ROPE:2 * MLA_ROPE]

    qa = _dot(cqn.astype(BF16), wqa_ref[...])
    half = MLA_ROPE // 2
    for h in range(MLA_HEADS):
        qh = qa[:, h * HEAD_PAD:(h + 1) * HEAD_PAD]
        rot = pltpu.roll(qh, HEAD_PAD - half, 1) * sq_up + pltpu.roll(qh, half, 1) * sq_dn
        q_ref[:, h * HEAD_PAD:(h + 1) * HEAD_PAD] = (qh * cq + rot).astype(BF16)

    ckv_b = ckv.astype(BF16)
    k = _dot(ckv_b, wk_ref[...]) + _dot(kpe.astype(BF16), pk_ref[...])
    k_ref[...] = k.astype(BF16)
    lane = lax.broadcasted_iota(jnp.int32, (1, MLA_HEADS * HEAD_PAD), 1)
    odd = (lane // HEAD_PAD) % 2
    one_lane = jnp.where(odd == 1, V_ONE_LANE[1], V_ONE_LANE[0])
    v_one = jnp.where(lane % HEAD_PAD == one_lane, 1.0, 0.0)
    v_ref[...] = (_dot(ckv_b, wv_ref[...]) + v_one).astype(BF16)
    _rwkv_heads(pr, prev, *rwkv_w, *rwkv_out)

    @pl.when(is_p)
    def _():
        ckvp_ref[...] = ckv
        kpep_ref[...] = kpe

    @pl.when(jnp.logical_not(is_p))
    def _():
        ckvs_ref[...] = ckv
        kpes_ref[...] = kpe


def _split_rows(n_p, tile0=0):
    return ((lambda i: (jnp.minimum(i + tile0, n_p - 1), 0)),
            (lambda i: (jnp.maximum(i + tile0 - n_p, 0), 0)))


def _proj_call(xp, xs, rope, seq_tiles, rope_place, rope_rows, w1, gq, gkv, wqa, wk, pk, wv, shift, rwkv_w):
    Tp, Ts = xp.shape[0], xs.shape[0]
    T = Tp + Ts
    tm = TOKEN_TILE
    n_p = Tp // tm
    rope_tiles = seq_tiles
    row = lambda i: (i, 0)
    full = lambda i: (0, 0)
    row_p, row_s = _split_rows(n_p)
    wide = MLA_HEADS * HEAD_PAD
    ng = tm // SHIFT_GROUP
    once = pl.Buffered(1)
    vec = pl.BlockSpec((1, RWKV_DIM), full, pipeline_mode=once)
    rwkv_specs = [pl.BlockSpec((1, RWKV_IN), full, pipeline_mode=once), vec, vec, vec, vec, vec,
                  pl.BlockSpec((DECAY_LORA, RWKV_DIM), full, pipeline_mode=once),
                  pl.BlockSpec((AAA_LORA, RWKV_DIM), full, pipeline_mode=once),
                  pl.BlockSpec((GATE_LORA, RWKV_DIM), full, pipeline_mode=once),
                  pl.BlockSpec((RWKV_DIM, RWKV_DIM), full, pipeline_mode=once)]
    tok = pl.BlockSpec((tm, RWKV_DIM), row)
    return pl.pallas_call(
        functools.partial(_proj_kernel, n_p=n_p, tm=tm, seq_tiles=seq_tiles),
        grid=(T // tm,),
        in_specs=[
            pl.BlockSpec((tm, D_MODEL), row_p),
            pl.BlockSpec((tm, D_MODEL), row_s),
            pl.BlockSpec((tm, MLA_ROPE), lambda i: (jnp.where(i < n_p, i % rope_tiles, rope_tiles), 0)),
            pl.BlockSpec((MLA_ROPE, 4 * LANE), full, pipeline_mode=once),
            pl.BlockSpec((2, 4 * LANE), full, pipeline_mode=once),
            pl.BlockSpec((D_MODEL, PROJ_W), full, pipeline_mode=once),
            pl.BlockSpec((1, Q_LORA), full, pipeline_mode=once),
            pl.BlockSpec((1, KV_LORA), full, pipeline_mode=once),
            pl.BlockSpec((Q_LORA, wide), full, pipeline_mode=once),
            pl.BlockSpec((KV_LORA, wide), full, pipeline_mode=once),
            pl.BlockSpec((MLA_ROPE, wide), full, pipeline_mode=once),
            pl.BlockSpec((KV_LORA, wide), full, pipeline_mode=once),
            pl.BlockSpec((ng, 1, RWKV_IN), lambda i: (jnp.maximum(i - n_p, 0), 0, 0)),
        ] + rwkv_specs,
        out_specs=[
            pl.BlockSpec((tm, wide), row),
            pl.BlockSpec((tm, wide), row),
            pl.BlockSpec((tm, wide), row),
            pl.BlockSpec((tm, KV_LORA), row_p),
            pl.BlockSpec((tm, KV_LORA), row_s),
            pl.BlockSpec((tm, MLA_ROPE), row_p),
            pl.BlockSpec((tm, MLA_ROPE), row_s),
            pl.BlockSpec((ng, RWKV_IN), row),
        ] + [tok] * 8,
        out_shape=[
            jax.ShapeDtypeStruct((T, wide), BF16),
            jax.ShapeDtypeStruct((T, wide), BF16),
            jax.ShapeDtypeStruct((T, wide), BF16),
            jax.ShapeDtypeStruct((Tp, KV_LORA), F32),
            jax.ShapeDtypeStruct((Ts, KV_LORA), F32),
            jax.ShapeDtypeStruct((Tp, MLA_ROPE), F32),
            jax.ShapeDtypeStruct((Ts, MLA_ROPE), F32),
            jax.ShapeDtypeStruct((T // SHIFT_GROUP, RWKV_IN), F32),
        ] + [jax.ShapeDtypeStruct((T, RWKV_DIM), F32)] * 8,
        scratch_shapes=[pltpu.VMEM((1, RWKV_IN), F32)],
        compiler_params=pltpu.CompilerParams(dimension_semantics=("arbitrary",), vmem_limit_bytes=PROJ_VMEM_LIMIT),
        name="proj",
    )(xp, xs, rope, rope_place, rope_rows, w1, gq, gkv, wqa, wk, pk, wv, shift, *rwkv_w)


def _attn_kernel(q_ref, k_ref, v_ref, o_ref, m_scr, acc_scr, *, tq, tk):
    qi = pl.program_id(2)
    m_scr[...] = jnp.full(m_scr.shape, NEG_INF, F32)
    acc_scr[...] = jnp.zeros(acc_scr.shape, F32)
    n_diag = tq // tk

    H = range(2)
    sls = [slice(h * HEAD_PAD, (h + 1) * HEAD_PAD) for h in H]

    def scores(k0, width, rows):
        return [_dot_nt(q_ref[rows, sl], k_ref[pl.ds(k0, width), sl]) for sl in sls]

    def accumulate(s, k0, width, rows):
        m_prev = [m_scr[h, rows, :] for h in H]
        m_new = [jnp.maximum(m_prev[h], jnp.max(s[h], axis=1, keepdims=True)) for h in H]
        pexp = [jnp.exp2(s[h] - jnp.tile(m_new[h], (1, width // LANE))).astype(BF16) for h in H]
        pv = [_dot(pexp[h], v_ref[pl.ds(k0, width), sls[h]]) for h in H]
        for h in H:
            acc_scr[h, rows, :] = jnp.exp2(m_prev[h] - m_new[h]) * acc_scr[h, rows, :] + pv[h]
            m_scr[h, rows, :] = m_new[h]

    def kv_block(k0, width, rows, masked):
        s = scores(k0, width, rows)
        if masked:
            n_rows = rows.stop - rows.start
            r = lax.broadcasted_iota(jnp.int32, (n_rows, width), 0) // CHUNK
            c = lax.broadcasted_iota(jnp.int32, (n_rows, width), 1) // CHUNK
            s = [jnp.where(c <= r, s[h], NEG_INF) for h in H]
        accumulate(s, k0, width, rows)

    all_rows = slice(0, tq)

    def pair(t, carry):
        k0a = pl.multiple_of(2 * t * tq, tq)
        k0b = pl.multiple_of(k0a + tq, tq)
        s_a = scores(k0a, tq, all_rows)
        s_b = scores(k0b, tq, all_rows)
        accumulate(s_a, k0a, tq, all_rows)
        accumulate(s_b, k0b, tq, all_rows)
        return carry

    lax.fori_loop(0, qi // 2, pair, 0)

    @pl.when(qi % 2 == 1)
    def _():
        kv_block(pl.multiple_of((qi - 1) * tq, tq), tq, all_rows, False)

    for d in range(n_diag):
        k0 = pl.multiple_of(qi * tq + d * tk, tk)
        kv_block(k0, tk, slice(d * tk, (d + 1) * tk), True)
        if (d + 1) * tk < tq:
            kv_block(k0, tk, slice((d + 1) * tk, tq), False)
    acc0, acc1 = acc_scr[0], acc_scr[1]
    lane = lax.broadcasted_iota(jnp.int32, acc0.shape, 1)
    l0 = acc0[:, V_ONE_LANE[0]:V_ONE_LANE[0] + 1]
    l1 = acc1[:, V_ONE_LANE[1]:V_ONE_LANE[1] + 1]
    o_ref[...] = jnp.where(lane < MLA_V, acc0 / l0, acc1 / l1).astype(o_ref.dtype)


def _attn_call(q, k, v, n_batch, seq):
    tq, tk = ATTN_TQ, ATTN_TK
    nq = seq // tq
    hp = MLA_HEADS // 2
    resident = pl.BlockSpec((seq, 2 * HEAD_PAD), lambda b, h, i: (b, h), pipeline_mode=pl.Buffered(1))
    return pl.pallas_call(
        functools.partial(_attn_kernel, tq=tq, tk=tk),
        grid=(n_batch, hp, nq),
        in_specs=[
            pl.BlockSpec((tq, 2 * HEAD_PAD), lambda b, h, i: (b * nq + i, h)),
            resident,
            resident,
        ],
        out_specs=pl.BlockSpec((tq, LANE), lambda b, h, i: (b * nq + i, h)),
        out_shape=jax.ShapeDtypeStruct((n_batch * seq, hp * LANE), BF16),
        scratch_shapes=[
            pltpu.VMEM((2, tq, LANE), F32),
            pltpu.VMEM((2, tq, LANE), F32),
        ],
        compiler_params=_cparams(("parallel", "parallel", "arbitrary")),
        name="attn",
    )(q, k, v)


def _mla_sample_kernel(q_ref, cn_ref, kn_ref, cp_ref, kp_ref, wuk_ref, wuv_ref, o_ref, *, past, dec):
    cp = cp_ref[0].astype(BF16)
    kp = kp_ref[0].astype(BF16)
    cn = cn_ref[...].astype(BF16)
    kn = kn_ref[...].astype(BF16)
    R = MLA_HEADS * dec
    qrow = (past + lax.broadcasted_iota(jnp.int32, (R, past), 0) % dec) // CHUNK
    vis_p = (lax.broadcasted_iota(jnp.int32, (R, past), 1) // CHUNK) <= qrow
    qrow_n = (past + lax.broadcasted_iota(jnp.int32, (R, dec), 0) % dec) // CHUNK
    vis_n = ((past + lax.broadcasted_iota(jnp.int32, (R, dec), 1)) // CHUNK) <= qrow_n
    q_lat, qp = [], []
    for h in range(MLA_HEADS):
        qn = q_ref[:, h * HEAD_PAD:h * HEAD_PAD + MLA_NOPE]
        qp.append(q_ref[:, h * HEAD_PAD + MLA_NOPE:h * HEAD_PAD + MLA_NOPE + MLA_ROPE])
        q_lat.append(_dot(qn, wuk_ref[h]).astype(BF16))
    q_lat = jnp.concatenate(q_lat, axis=0)
    qp = jnp.concatenate(qp, axis=0)
    s_p = jnp.where(vis_p, _dot_nt(q_lat, cp) + _dot_nt(qp, kp), NEG_INF)
    s_n = jnp.where(vis_n, _dot_nt(q_lat, cn) + _dot_nt(qp, kn), NEG_INF)
    m = jnp.maximum(jnp.max(s_p, axis=1, keepdims=True), jnp.max(s_n, axis=1, keepdims=True))
    e_p = jnp.exp2(s_p - m)
    e_n = jnp.exp2(s_n - m)
    l = jnp.sum(e_p, axis=1, keepdims=True) + jnp.sum(e_n, axis=1, keepdims=True)
    o_lat = ((_dot(e_p.astype(BF16), cp) + _dot(e_n.astype(BF16), cn)) / l).astype(BF16)
    out = jnp.zeros((dec, MLA_HEADS * MLA_V), F32)
    for h in range(MLA_HEADS):
        out = out + _dot(o_lat[h * dec:(h + 1) * dec], wuv_ref[h])
    o_ref[...] = out.astype(o_ref.dtype)


def _mla_sample_call(q, ckv, kpe, cache_ckv, cache_kpe, wuk, wuv, row0, n_seq, dec):
    past = cache_ckv.shape[1]
    blk0 = row0 // dec
    wide = MLA_HEADS * HEAD_PAD
    return pl.pallas_call(
        functools.partial(_mla_sample_kernel, past=past, dec=dec),
        grid=(n_seq,),
        in_specs=[
            pl.BlockSpec((dec, wide), lambda b: (blk0 + b, 0)),
            pl.BlockSpec((dec, KV_LORA), lambda b: (b, 0)),
            pl.BlockSpec((dec, MLA_ROPE), lambda b: (b, 0)),
            pl.BlockSpec((1, past, KV_LORA), lambda b: (b, 0, 0)),
            pl.BlockSpec((1, past, MLA_ROPE), lambda b: (b, 0, 0)),
            pl.BlockSpec((MLA_HEADS, MLA_NOPE, KV_LORA), lambda b: (0, 0, 0)),
            pl.BlockSpec((MLA_HEADS, KV_LORA, MLA_HEADS * MLA_V), lambda b: (0, 0, 0)),
        ],
        out_specs=pl.BlockSpec((dec, MLA_HEADS * MLA_V), lambda b: (b, 0)),
        out_shape=jax.ShapeDtypeStruct((n_seq * dec, MLA_HEADS * MLA_V), BF16),
        compiler_params=_cparams(("parallel",)),
        name="mla_sample",
    )(q, ckv, kpe, cache_ckv, cache_kpe, wuk, wuv)


def _rwkv_heads(pr, prev, mu_ref, w0_ref, a0_ref, kk_ref, ka_ref, rk_ref, w2_ref, a2_ref, g2_ref,
                seg_ref, r_ref, lw_ref, kh_ref, v_ref, na_ref, b_ref, bonus_ref, g_ref):
    u = pr + mu_ref[...] * (prev - pr)
    o1, o2, o3 = RWKV_DIM, 2 * RWKV_DIM, 3 * RWKV_DIM
    o4, o5 = o3 + DECAY_LORA, o3 + DECAY_LORA + AAA_LORA
    r, k, v = u[:, :o1], u[:, o1:o2], u[:, o2:o3]
    w_lo, a_lo, g_lo = u[:, o3:o4], u[:, o4:o5], u[:, o5:]
    wl = w0_ref[...] + _dot(jnp.tanh(w_lo).astype(BF16), w2_ref[...])
    lw_ref[...] = -math.exp(-0.5) * _sigmoid(wl)
    a = _sigmoid(a0_ref[...] + _dot(a_lo.astype(BF16), a2_ref[...]))
    g_ref[...] = _dot(_sigmoid(g_lo).astype(BF16), g2_ref[...])
    seg = seg_ref[...]
    kk = k * kk_ref[...]
    kk = kk / jnp.maximum(jnp.sqrt(_dot_exact_rhs(kk * kk, seg)), 1e-12)
    kh = k * (1.0 + (a - 1.0) * ka_ref[...])
    r_ref[...] = r
    kh_ref[...] = kh
    v_ref[...] = v
    na_ref[...] = -kk
    b_ref[...] = kk * a
    bonus_ref[...] = _dot_exact_rhs(r * kh * rk_ref[...], seg) * v


def _wkv_kernel(*refs, C, n_sub, n_par):
    GW = WKV_GROUP * RWKV_N
    R = WKV_GROUP * C
    n_grp = RWKV_HEADS // WKV_GROUP
    n_lev = int(round(math.log2(C))) - 1
    c = pl.program_id(1)
    tok = [refs[6 * p:6 * p + 6] for p in range(n_par)]
    h0_ref, y_ref, hT_ref, h_scr = refs[6 * n_par:]
    n_state = n_par * n_grp

    def head_block(hh):
        return slice(hh * RWKV_N, (hh + 1) * RWKV_N)

    @pl.when(c == 0)
    def _():
        h_scr[...] = jnp.zeros(h_scr.shape, F32)
        for p in range(n_par):
            for hd in range(RWKV_HEADS):
                g, hh = divmod(hd, WKV_GROUP)
                h_scr[p * n_grp + g, head_block(hh), head_block(hh)] = h0_ref[p, hd]

    row = lax.broadcasted_iota(jnp.int32, (C, C), 0)
    col = lax.broadcasted_iota(jnp.int32, (C, C), 1)
    tri = jnp.where(col <= row, 1.0, 0.0).astype(BF16)
    rr = lax.broadcasted_iota(jnp.int32, (R, R), 0)
    cc = lax.broadcasted_iota(jnp.int32, (R, R), 1)
    same = (rr // C) == (cc // C)
    ti = lax.broadcasted_iota(jnp.int32, (C, R), 0)
    si = lax.broadcasted_iota(jnp.int32, (C, R), 1) % C
    strict4 = si < ti
    lower4 = si <= ti
    eye4 = jnp.where(si == ti, 1.0, 0.0)
    keep = (lax.broadcasted_iota(jnp.int32, (R, GW), 0) // C
            == lax.broadcasted_iota(jnp.int32, (R, GW), 1) // RWKV_N)
    gr = lax.broadcasted_iota(jnp.int32, (GW, GW), 0)
    gc = lax.broadcasted_iota(jnp.int32, (GW, GW), 1)
    eye_g = gr == gc
    same_head = (gr // RWKV_N) == (gc // RWKV_N)
    eye_g_bf = jnp.where(eye_g, 1.0, 0.0).astype(BF16)

    def rows4(x4):
        return jnp.concatenate([x4] * WKV_GROUP, axis=0)

    def stack(x4):
        return jnp.where(keep, rows4(x4), jnp.zeros((), x4.dtype))


    sls = [slice(g * GW, (g + 1) * GW) for g in range(n_grp)]
    J = [(ci, q) for ci in range(n_sub) for q in range(n_state)]
    ops, p_end = {}, {}
    for p, ci in [(p, ci) for p in range(n_par) for ci in range(n_sub)]:
        r_ref, lw_ref, k_ref, v_ref, a_ref, b_ref = tok[p]
        rows = slice(ci * C, (ci + 1) * C)
        lw = lw_ref[rows, :]
        cum = _dot_exact_lhs(tri, lw)
        cum_end = cum[C - 1:C, :]
        e_neg = jnp.exp(-cum)
        e_end = jnp.exp(cum_end - cum)
        b_in = b_ref[rows, :]
        k_in = k_ref[rows, :]
        full = ((a_ref[rows, :] * jnp.exp(cum - lw)).astype(BF16),
                (r_ref[rows, :] * jnp.exp(cum)).astype(BF16),
                (b_in * e_neg).astype(BF16), (k_in * e_neg).astype(BF16),
                (b_in * e_end).astype(BF16), (k_in * e_end).astype(BF16),
                v_ref[rows, :].astype(BF16))
        p_end[p, ci] = jnp.exp(cum_end)
        for g in range(n_grp):
            ops[ci, p * n_grp + g] = [t[:, sls[g]] for t in full]
    a4, r4, b4, k4, be4, ke4, v4 = [{j: ops[j][i] for j in J} for i in range(7)]
    v_s = {j: stack(v4[j]) for j in J}
    m = {j: _dot_nt(jnp.concatenate([a4[j], r4[j]], axis=0),
                    jnp.concatenate([stack(b4[j]), stack(k4[j])], axis=0)) for j in J}
    l4 = {j: jnp.where(strict4, m[j][:C, :R], 0.0) for j in J}
    a_ak = {j: jnp.where(strict4, m[j][:C, R:], 0.0).astype(BF16) for j in J}
    a_rb = {j: jnp.where(lower4, m[j][C:, :R], 0.0).astype(BF16) for j in J}
    a_rk = {j: jnp.where(lower4, m[j][C:, R:], 0.0).astype(BF16) for j in J}
    def block_diag(x4):
        return jnp.where(same, rows4(x4.astype(BF16)), jnp.zeros((), BF16))

    t4 = {j: eye4 + l4[j] for j in J}
    l_bd = {j: block_diag(l4[j]) for j in J}
    for _ in range(n_lev):
        l4 = {j: _dot(l4[j].astype(BF16), l_bd[j]) for j in J}
        l_bd = {j: block_diag(l4[j]) for j in J}
        t4 = {j: t4[j] + _dot(t4[j].astype(BF16), l_bd[j]) for j in J}
    t_b = {j: t4[j].astype(BF16) for j in J}
    bke_t = {j: _dot_nt(eye_g_bf, jnp.concatenate([be4[j], ke4[j]], axis=0)).astype(BF16) for j in J}

    G = range(n_state)
    h_cur = [h_scr[q] for q in G]
    for ci in range(n_sub):
        rows = slice(ci * C, (ci + 1) * C)
        h0_b = [h_cur[g].astype(BF16) for g in G]
        x4 = [_dot(a4[ci, g], h0_b[g]) + _dot(a_ak[ci, g], v_s[ci, g]) for g in G]
        u4 = [_dot(t_b[ci, g], stack(x4[g].astype(BF16))).astype(BF16) for g in G]
        y4 = [_dot(r4[ci, g], h0_b[g]) + _dot(a_rb[ci, g], stack(u4[g])) + _dot(a_rk[ci, g], v_s[ci, g])
              for g in G]
        uv4 = [jnp.concatenate([u4[g], v4[ci, g]], axis=0) for g in G]
        h_add = [jnp.where(same_head, _dot(bke_t[ci, g], uv4[g]), 0.0) for g in G]
        for q in G:
            p, g = divmod(q, n_grp)
            y_ref[p, rows, sls[g]] = y4[q]
            p_col = jnp.sum(jnp.where(eye_g, p_end[p, ci][:, sls[g]], 0.0), axis=1, keepdims=True)
            h_cur[q] = p_col * h_cur[q] + h_add[q]

    for q in G:
        h_scr[q] = h_cur[q]

    @pl.when(c == pl.num_programs(1) - 1)
    def _():
        for p in range(n_par):
            for hd in range(RWKV_HEADS):
                g, hh = divmod(hd, WKV_GROUP)
                hT_ref[p, hd] = h_scr[p * n_grp + g, head_block(hh), head_block(hh)]


def _wkv_call(arrs, h0, row0, n_seq, n_chunk, C, n_sub, n_par):
    rows = C * n_sub
    steps = n_chunk // n_sub
    blk0 = row0 // rows
    GW = WKV_GROUP * RWKV_N
    n_grp = RWKV_HEADS // WKV_GROUP
    assert n_seq % n_par == 0 and n_chunk % n_sub == 0
    tok = [pl.BlockSpec((rows, RWKV_DIM), lambda b, c, p=p: (blk0 + (b * n_par + p) * steps + c, 0))
           for p in range(n_par)]
    st = pl.BlockSpec((n_par, RWKV_HEADS, RWKV_N, RWKV_N), lambda b, c: (b, 0, 0, 0))
    y, h_fin = pl.pallas_call(
        functools.partial(_wkv_kernel, C=C, n_sub=n_sub, n_par=n_par),
        grid=(n_seq // n_par, steps),
        in_specs=[tok[p] for p in range(n_par) for _ in range(6)] + [st],
        out_specs=[pl.BlockSpec((n_par, rows, RWKV_DIM), lambda b, c: (b, c, 0)), st],
        out_shape=[
            jax.ShapeDtypeStruct((n_seq, n_chunk * C, RWKV_DIM), F32),
            jax.ShapeDtypeStruct((n_seq, RWKV_HEADS, RWKV_N, RWKV_N), F32),
        ],
        scratch_shapes=[pltpu.VMEM((n_par * n_grp, GW, GW), F32)],
        compiler_params=_cparams(("parallel", "arbitrary")),
        name="wkv_c%d" % C,
    )(*(list(arrs) * n_par), h0)
    return y.reshape(n_seq * n_chunk * C, RWKV_DIM), h_fin


def _mix_kernel(xp_ref, xs_ref, attnp_ref, attns_ref, yp_ref, ys_ref, bonus_ref, g_ref, seg_ref, lng_ref,
                lnb_ref, woa_ref, wob_ref, g1_ref, b1_ref, wr_ref, br_ref, tri_ref, h_ref, hpk_ref, route_ref,
                count_ref, *, n_p, tm):
    is_p = pl.program_id(0) < n_p
    P = range(MIX_PARTS)
    bands = [slice(k * tm // MIX_PARTS, (k + 1) * tm // MIX_PARTS) for k in P]
    seg = seg_ref[...]
    inv_n = 1.0 / RWKV_N
    y = [jnp.where(is_p, yp_ref[r, :], ys_ref[r, :]) for r in bands]
    yc = [y[k] - _dot_exact_rhs(y[k], seg) * inv_n for k in P]
    var = [_dot_exact_rhs(yc[k] * yc[k], seg) * inv_n for k in P]
    yn = [yc[k] * lax.rsqrt(var[k] + GN_EPS) * lng_ref[...] + lnb_ref[...] for k in P]
    rw = [((yn[k] + bonus_ref[bands[k], :]) * g_ref[bands[k], :]).astype(BF16) for k in P]
    attn = [jnp.where(is_p, attnp_ref[r, :], attns_ref[r, :]) for r in bands]
    m = [_dot(attn[k], woa_ref[...]) + _dot(rw[k], wob_ref[...]) for k in P]
    x = [jnp.where(is_p, xp_ref[r, :], xs_ref[r, :]) for r in bands]
    h = [_layer_norm(DN_ALPHA * x[k] + m[k], g1_ref[...], b1_ref[...]) for k in P]
    for k in P:
        h_ref[bands[k], :] = h[k]
        hpk_ref[bands[k], :] = _pack_bf16_pairs(h[k])

    h_hi = [h[k].astype(BF16) for k in P]
    h_lo = [(h[k] - h_hi[k].astype(F32)).astype(BF16) for k in P]
    logits = [_dot(h_hi[k], wr_ref[0]) + _dot(h_lo[k], wr_ref[0]) + _dot(h_hi[k], wr_ref[1]) + br_ref[...]
              for k in P]
    lane = lax.broadcasted_iota(jnp.int32, logits[0].shape, 1)
    big = jnp.int32(LANE)

    def route_band(lg):
        gl = jnp.where(lane < N_GROUPS, lg, NEG_INF)
        gmax = jnp.max(gl, axis=1, keepdims=True)
        grp = jnp.min(jnp.where(gl == gmax, lane, big), axis=1, keepdims=True)
        p_grp = 1.0 / jnp.sum(jnp.exp(gl - gmax), axis=1, keepdims=True)
        e_idx = lane - N_GROUPS
        in_grp = (lane >= N_GROUPS) & (lane < N_GROUPS + N_EXPERTS) & ((e_idx // EXPERTS_PER_GROUP) == grp)
        el = jnp.where(in_grp, lg, NEG_INF)
        m1 = jnp.max(el, axis=1, keepdims=True)
        i1 = jnp.min(jnp.where(el == m1, lane, big), axis=1, keepdims=True)
        el2 = jnp.where(lane == i1, NEG_INF, el)
        m2 = jnp.max(el2, axis=1, keepdims=True)
        i2 = jnp.min(jnp.where(el2 == m2, lane, big), axis=1, keepdims=True)
        t = jnp.exp(m2 - m1)
        g1 = p_grp / (1.0 + t)
        return i1, i2, g1, g1 * t

    routed = [route_band(logits[k]) for k in P]
    chosen = jnp.concatenate(
        [jnp.where(lane == routed[k][0], 1.0, jnp.where(lane == routed[k][1], 1.0, 0.0)) for k in P], axis=0)
    before = _dot(tri_ref[...], chosen.astype(BF16))
    count_ref[0] = jnp.sum(chosen, axis=0, keepdims=True)
    for k in P:
        i1, i2, g1, g2 = routed[k]
        bef = before[bands[k]]
        r1 = jnp.sum(jnp.where(lane == i1, bef, 0.0), axis=1, keepdims=True)
        r2 = jnp.sum(jnp.where(lane == i2, bef, 0.0), axis=1, keepdims=True)
        cols = ((i1 - N_GROUPS).astype(F32), (i2 - N_GROUPS).astype(F32), g1, g2, r1, r2)
        route = jnp.zeros(lane.shape, F32)
        for j, col in enumerate(cols):
            route = jnp.where(lane == j, col, route)
        route_ref[bands[k], :] = route


def _mix_call(xp, xs, attn_p, attn_s, y_p, y_s, bonus, g, seg, lnx_g, lnx_b, woa, wob, ln1_g, ln1_b, wr, br,
              tile0, n_tiles):
    tm = TOKEN_TILE
    T = n_tiles * tm
    n_p = xp.shape[0] // tm
    row = lambda i: (i, 0)
    full = lambda i: (0, 0)
    row_p, row_s = _split_rows(n_p, tile0)
    half = pl.BlockSpec((tm, RWKV_DIM), lambda i: (i + tile0, 0))
    vec5 = pl.BlockSpec((1, RWKV_DIM), full)
    vec10 = pl.BlockSpec((1, D_MODEL), full)
    idx = np.arange(tm)
    tri = jnp.asarray((idx[None, :] < idx[:, None]).astype(np.float32)).astype(BF16)
    return pl.pallas_call(
        functools.partial(_mix_kernel, n_p=n_p - tile0, tm=tm),
        grid=(n_tiles,),
        in_specs=[
            pl.BlockSpec((tm, D_MODEL), row_p), pl.BlockSpec((tm, D_MODEL), row_s),
            pl.BlockSpec((tm, RWKV_DIM), row_p), pl.BlockSpec((tm, RWKV_DIM), row_s),
            pl.BlockSpec((tm, RWKV_DIM), row_p), pl.BlockSpec((tm, RWKV_DIM), row_s),
            half, half,
            pl.BlockSpec((RWKV_DIM, RWKV_DIM), full), vec5, vec5,
            pl.BlockSpec((RWKV_DIM, D_MODEL), full), pl.BlockSpec((RWKV_DIM, D_MODEL), full),
            vec10, vec10,
            pl.BlockSpec((2, D_MODEL, LANE), lambda i: (0, 0, 0)), pl.BlockSpec((1, LANE), full),
            pl.BlockSpec((tm, tm), full),
        ],
        out_specs=[pl.BlockSpec((tm, D_MODEL), row), pl.BlockSpec((tm, D_MODEL // 2), row),
                   pl.BlockSpec((tm, LANE), row), pl.BlockSpec((1, 1, LANE), lambda i: (i, 0, 0))],
        out_shape=[jax.ShapeDtypeStruct((T, D_MODEL), F32), jax.ShapeDtypeStruct((T, D_MODEL // 2), jnp.int32),
                   jax.ShapeDtypeStruct((T, LANE), F32), jax.ShapeDtypeStruct((T // tm, 1, LANE), F32)],
        compiler_params=_cparams(("parallel",)),
        name="mix",
    )(xp, xs, attn_p, attn_s, y_p, y_s, bonus, g, seg, lnx_g, lnx_b, woa, wob, ln1_g, ln1_b, wr, br, tri)


def _expert_kernel(be_ref, nu_ref, xs_ref, wg_ref, wu_ref, wd_ref, ys_ref, wgu_b, wd_b):
    i = pl.program_id(0)

    @pl.when((i == 0) | (be_ref[i] != be_ref[jnp.maximum(i - 1, 0)]))
    def _():
        wgu_b[:, :D_EXPERT] = wg_ref[0].astype(BF16)
        wgu_b[:, D_EXPERT:] = wu_ref[0].astype(BF16)
        wd_b[...] = wd_ref[0].astype(BF16)

    @pl.when(i < nu_ref[0])
    def _():
        P = range(2)
        bands = [slice(k * MOE_BLK // 2, (k + 1) * MOE_BLK // 2) for k in P]
        xb = [_unpack_bf16_pairs(xs_ref[r, :]).astype(BF16) for r in bands]
        gu = [_dot(xb[k], wgu_b[...]) for k in P]
        act = [(gu[k][:, :D_EXPERT] * _sigmoid(gu[k][:, :D_EXPERT]) * gu[k][:, D_EXPERT:]).astype(BF16)
               for k in P]
        out = [_dot(act[k], wd_b[...]) for k in P]
        for k in P:
            ys_ref[bands[k], :] = _pack_bf16_pairs(out[k])

    @pl.when(i >= nu_ref[0])
    def _():
        ys_ref[...] = jnp.zeros(ys_ref.shape, ys_ref.dtype)


def _expert_call(block_e, n_used, xs, wg, wu, wd):
    n_blk = xs.shape[0] // MOE_BLK
    grid_spec = pltpu.PrefetchScalarGridSpec(
        num_scalar_prefetch=2,
        grid=(n_blk,),
        in_specs=[
            pl.BlockSpec((MOE_BLK, D_MODEL // 2), lambda i, be, nu: (i, 0)),
            pl.BlockSpec((1, D_MODEL, D_EXPERT), lambda i, be, nu: (be[i], 0, 0)),
            pl.BlockSpec((1, D_MODEL, D_EXPERT), lambda i, be, nu: (be[i], 0, 0)),
            pl.BlockSpec((1, D_EXPERT, D_MODEL), lambda i, be, nu: (be[i], 0, 0)),
        ],
        out_specs=pl.BlockSpec((MOE_BLK, D_MODEL // 2), lambda i, be, nu: (i, 0)),
        scratch_shapes=[pltpu.VMEM((D_MODEL, 2 * D_EXPERT), BF16), pltpu.VMEM((D_EXPERT, D_MODEL), BF16)],
    )
    return pl.pallas_call(
        _expert_kernel,
        grid_spec=grid_spec,
        out_shape=jax.ShapeDtypeStruct((n_blk * MOE_BLK, D_MODEL // 2), jnp.int32),
        compiler_params=_cparams(("arbitrary",)),
        name="experts",
    )(block_e, n_used, xs, wg, wu, wd)


def _combine_kernel(h_ref, ya_ref, yb_ref, route_ref, g2_ref, b2_ref, *rest, n_p, has_prev, has_prompt,
                    has_sample):
    outs = list(rest[1:] if has_prev else rest)
    i = pl.program_id(0)
    route = route_ref[...]
    f = _unpack_bf16_pairs(ya_ref[...]) * route[:, 2:3] + _unpack_bf16_pairs(yb_ref[...]) * route[:, 3:4]
    out = _layer_norm(DN_ALPHA * h_ref[...] + f, g2_ref[...], b2_ref[...])

    if has_prompt:
        op_ref = outs.pop(0)

        @pl.when(i < n_p)
        def _():
            op_ref[...] = out

    if has_sample:
        os_ref = outs.pop(0)

        @pl.when(i >= n_p)
        def _():
            os_ref[...] = out


def _combine_call(h, yab, route, ln2_g, ln2_b, t_prompt, t_sample, tile0, out_p_prev):
    tm = TOKEN_TILE
    n_t, n_p = h.shape[0] // tm, t_prompt // tm
    has_prompt = tile0 < n_p
    has_sample = tile0 + n_t > n_p
    has_prev = has_prompt and out_p_prev is not None
    row = lambda i: (i, 0)
    full = lambda i: (0, 0)
    row_p, row_s = _split_rows(n_p, tile0)
    big = pl.BlockSpec((tm, D_MODEL), row)
    in_specs = [big, pl.BlockSpec((tm, D_MODEL // 2), row),
                pl.BlockSpec((tm, D_MODEL // 2), lambda i: (i + n_t, 0)), pl.BlockSpec((tm, LANE), row),
                pl.BlockSpec((1, D_MODEL), full), pl.BlockSpec((1, D_MODEL), full)]
    args = [h, yab, yab, route, ln2_g, ln2_b]
    out_specs, out_shape, aliases = [], [], {}
    if has_prompt:
        out_specs.append(pl.BlockSpec((tm, D_MODEL), row_p))
        out_shape.append(jax.ShapeDtypeStruct((t_prompt, D_MODEL), F32))
    if has_prev:
        in_specs.append(pl.BlockSpec(memory_space=pl.ANY))
        args.append(out_p_prev)
        aliases = {len(args) - 1: 0}
    if has_sample:
        out_specs.append(pl.BlockSpec((tm, D_MODEL), row_s))
        out_shape.append(jax.ShapeDtypeStruct((t_sample, D_MODEL), F32))
    outs = pl.pallas_call(
        functools.partial(_combine_kernel, n_p=n_p - tile0, has_prev=has_prev, has_prompt=has_prompt,
                          has_sample=has_sample),
        grid=(n_t,),
        in_specs=in_specs,
        out_specs=out_specs,
        out_shape=out_shape,
        input_output_aliases=aliases,
        compiler_params=_cparams(("arbitrary",)),
        name="combine",
    )(*args)
    out_p = outs[0] if has_prompt else out_p_prev
    out_s = outs[-1] if has_sample else None
    return out_p, out_s


def _prep_weights(w_in, w_uq, w_ukv):
    half = MLA_ROPE // 2
    kpe_w = w_in[:, Q_LORA + KV_LORA:MLA_IN]
    kpe_b = jnp.concatenate([-kpe_w[:, half:], kpe_w[:, :half]], axis=1)
    w1 = jnp.concatenate([w_in[:, :Q_LORA + KV_LORA], kpe_w, kpe_b,
                          jnp.zeros((D_MODEL, LANE - 2 * MLA_ROPE), F32), w_in[:, MLA_IN:]], axis=1).astype(BF16)
    pad_q = jnp.zeros((Q_LORA, MLA_HEADS, HEAD_PAD - MLA_NOPE - MLA_ROPE), F32)
    wqa = jnp.concatenate([w_uq, pad_q], axis=2).reshape(Q_LORA, -1).astype(BF16)
    w_uk, w_uv = w_ukv[:, :, :MLA_NOPE], w_ukv[:, :, MLA_NOPE:]
    wk = jnp.concatenate([w_uk, jnp.zeros((KV_LORA, MLA_HEADS, HEAD_PAD - MLA_NOPE), F32)], axis=2)
    wk = wk.reshape(KV_LORA, -1).astype(BF16)
    pk_np = np.zeros((MLA_ROPE, MLA_HEADS * HEAD_PAD), np.float32)
    for h in range(MLA_HEADS):
        for i in range(MLA_ROPE):
            pk_np[i, h * HEAD_PAD + MLA_NOPE + i] = 1.0
    pk = jnp.asarray(pk_np).astype(BF16)
    zv = jnp.zeros((KV_LORA, MLA_HEADS // 2, MLA_V), F32)
    wv4 = w_uv.reshape(KV_LORA, MLA_HEADS // 2, 2, MLA_V)
    wv = jnp.stack([jnp.concatenate([wv4[:, :, 0], zv], axis=2),
                    jnp.concatenate([zv, wv4[:, :, 1]], axis=2)], axis=2)
    wv = wv.reshape(KV_LORA, -1).astype(BF16)
    wuk = jnp.transpose(w_uk, (1, 2, 0)).astype(BF16)
    wuv_np = np.zeros((MLA_HEADS, MLA_HEADS * MLA_V), np.float32)
    for h in range(MLA_HEADS):
        wuv_np[h, h * MLA_V:(h + 1) * MLA_V] = 1.0
    wuv = jnp.transpose(w_uv, (1, 0, 2))
    wuv = (jnp.tile(wuv, (1, 1, MLA_HEADS)) * jnp.asarray(wuv_np)[:, None, :]).astype(BF16)
    return w1, wqa, wk, pk, wv, wuk, wuv


def _rope_table(pos):
    inv = ROPE_BASE ** (-jnp.arange(0, MLA_ROPE, 2, dtype=F32) / MLA_ROPE)
    ang = pos.astype(F32)[:, None] * inv[None, :]
    return jnp.concatenate([jnp.cos(ang), jnp.sin(ang)], axis=1)


def _rope_placement():
    half = MLA_ROPE // 2
    place = np.zeros((MLA_ROPE, 4 * LANE), np.float32)
    rows = np.zeros((2, 4 * LANE), np.float32)
    for i in range(half):
        c, s = i, half + i
        place[c, MLA_NOPE + i] = place[c, MLA_NOPE + half + i] = 1.0
        place[s, LANE + MLA_NOPE + i] = -1.0
        place[s, 2 * LANE + MLA_NOPE + half + i] = 1.0
        place[c, 3 * LANE + i] = place[c, 3 * LANE + half + i] = 1.0
        place[s, 3 * LANE + 2 * half + i] = place[s, 3 * LANE + 3 * half + i] = 1.0
    rows[0, :MLA_NOPE] = 1.0
    rows[1, :3 * LANE] = MLA_SCALE * LOG2E
    rows[1, 3 * LANE:] = 1.0
    return jnp.asarray(place).astype(BF16), jnp.asarray(rows)


def _seg_ones():
    idx = np.arange(RWKV_DIM) // RWKV_N
    return jnp.asarray((idx[:, None] == idx[None, :]).astype(np.float32)).astype(BF16)


def _dispatch(route, tile_counts, t_total):
    A = t_total * TOP_K
    n_tiles = tile_counts.shape[0]
    counts_te = tile_counts[:, 0, N_GROUPS:N_GROUPS + N_EXPERTS].astype(jnp.int32)
    counts = jnp.sum(counts_te, axis=0)
    blocks_per_e = (counts + MOE_BLK - 1) // MOE_BLK
    blk_end = jnp.cumsum(blocks_per_e)
    blk_start = blk_end - blocks_per_e
    tile_off = jnp.cumsum(counts_te, axis=0) - counts_te
    base = blk_start[None, :] * MOE_BLK + tile_off
    e = route[:, :TOP_K].astype(jnp.int32)
    rank = route[:, 4:4 + TOP_K].astype(jnp.int32)
    base_tok = jnp.repeat(base, t_total // n_tiles, axis=0)
    pick = e[:, :, None] == jnp.arange(N_EXPERTS, dtype=jnp.int32)[None, None, :]
    dest = jnp.sum(jnp.where(pick, base_tok[:, None, :], 0), axis=-1) + rank
    n_blk = -(-A // MOE_BLK) + N_EXPERTS
    blk = jnp.arange(n_blk, dtype=jnp.int32)
    block_e = jnp.minimum(jnp.sum((blk[:, None] >= blk_end[None, :]).astype(jnp.int32), axis=1),
                          N_EXPERTS - 1).astype(jnp.int32)
    n_used = blk_end[-1:].astype(jnp.int32)
    return dest, block_e, n_used


def _sc_gather_rows(table, idx):
    n_rows, width = idx.shape[0], table.shape[1]
    n_workers = SC_CORES * SC_SUBCORES
    per_worker = n_rows // n_workers
    assert n_rows % n_workers == 0 and per_worker % SC_WINDOW == 0
    mesh = plsc.VectorSubcoreMesh(core_axis_name="c", subcore_axis_name="s")

    @functools.partial(
        pl.kernel, mesh=mesh,
        out_type=jax.ShapeDtypeStruct((n_rows, width), table.dtype),
        scratch_types=[
            pltpu.VMEM((SC_WINDOW,), jnp.int32),
            pltpu.VMEM((SC_WINDOW, width), table.dtype),
            pltpu.SemaphoreType.DMA,
        ],
    )
    def gather(table_hbm, idx_hbm, out_hbm, idx_v, rows_v, sem):
        wid = lax.axis_index("s") * SC_CORES + lax.axis_index("c")
        base = wid * per_worker

        @pl.loop(0, per_worker // SC_WINDOW)
        def _(w):
            off = pl.multiple_of(base + w * SC_WINDOW, SC_WINDOW)
            pltpu.sync_copy(idx_hbm.at[pl.ds(off, SC_WINDOW)], idx_v)
            pltpu.async_copy(table_hbm.at[idx_v], rows_v, sem).wait()
            pltpu.sync_copy(rows_v, out_hbm.at[pl.ds(off, SC_WINDOW)])

    return gather(table, idx)


def _sc_scatter_rows(src, idx_a, idx_b, n_slots):
    n_rows, width = src.shape
    n_workers = SC_CORES * SC_SUBCORES
    per_worker = n_rows // n_workers
    assert n_rows % n_workers == 0 and per_worker % SC_WINDOW == 0
    n_win = per_worker // SC_WINDOW
    mesh = plsc.VectorSubcoreMesh(core_axis_name="c", subcore_axis_name="s")

    @functools.partial(
        pl.kernel, mesh=mesh,
        out_type=jax.ShapeDtypeStruct((n_slots, width), src.dtype),
        scratch_types=[
            pltpu.VMEM((1, SC_WINDOW), jnp.int32),
            pltpu.VMEM((1, SC_WINDOW), jnp.int32),
            pltpu.VMEM((SC_WINDOW, width), src.dtype),
        ],
    )
    def scatter(src_hbm, ia_hbm, ib_hbm, out_hbm, ia_v, ib_v, rows_v):
        wid = lax.axis_index("s") * SC_CORES + lax.axis_index("c")

        @pl.loop(0, n_win)
        def _(w):
            win = wid * n_win + w
            off = pl.multiple_of(win * SC_WINDOW, SC_WINDOW)
            pltpu.sync_copy(src_hbm.at[pl.ds(off, SC_WINDOW)], rows_v)
            pltpu.sync_copy(ia_hbm.at[pl.ds(win, 1)], ia_v)
            pltpu.sync_copy(ib_hbm.at[pl.ds(win, 1)], ib_v)
            pltpu.sync_copy(rows_v, out_hbm.at[ia_v.at[0]])
            pltpu.sync_copy(rows_v, out_hbm.at[ib_v.at[0]])

    return scatter(src, idx_a, idx_b)


def kernel(x_prompt, x_sample, cache_ckv, cache_kpe, state_wkv, state_shift, w_in, q_norm_g, kv_norm_g, w_uq,
           w_ukv, mu_shift, w0, w2, a0, a2, g2, k_k, k_a, r_k, lnx_g, lnx_b, w_o, ln1_g, ln1_b, w_gr, b_gr,
           w_er, b_er, w_eg, w_eu, w_ed, ln2_g, ln2_b):
    B, S, D = x_prompt.shape
    DB, DS, _ = x_sample.shape
    past = cache_ckv.shape[2]
    Tp, Ts = B * S, DB * DS
    T = Tp + Ts
    assert D == D_MODEL and DS == SHIFT_GROUP and S % ATTN_TQ == 0 and S % (CHUNK * WKV_SUB) == 0
    assert Tp % TOKEN_TILE == 0 and T % TOKEN_TILE == 0 and w_in.shape[0] == DEPTH

    l = 0
    xp, xs_in = x_prompt.reshape(Tp, D), x_sample.reshape(Ts, D)
    w1, wqa, wk, pk, wv, wuk, wuv = _prep_weights(w_in[l], w_uq[l], w_ukv[l])
    pos = jnp.concatenate([jnp.arange(S, dtype=jnp.int32),
                           jnp.tile(past + jnp.arange(DS, dtype=jnp.int32), TOKEN_TILE // DS)])
    rope = _rope_table(pos)

    seg = _seg_ones()
    vec = lambda a: a.reshape(1, -1)
    rwkv_w = (vec(mu_shift[l]), vec(w0[l]), vec(a0[l]), vec(k_k[l]), vec(k_a[l]), vec(r_k[l]),
              w2[l].astype(BF16), a2[l].astype(BF16), g2[l].astype(BF16), seg)
    (q, kcat, vcat, ckv_p, ckv_s, kpe_p, kpe_s, last_rows,
     r, lw, kh, v, na, b, bonus, g) = _proj_call(
        xp, xs_in, rope, S // TOKEN_TILE, *_rope_placement(), w1, q_norm_g[l][None], kv_norm_g[l][None],
        wqa, wk, pk, wv, state_shift[l][:, None, :], rwkv_w)

    attn_p = _attn_call(q, kcat, vcat, B, S)
    attn_s = _mla_sample_call(q, ckv_s, kpe_s, cache_ckv[l], cache_kpe[l], wuk, wuv, Tp, DB, DS)

    scan_in = (r, lw, kh, v, na, b)
    h0_p = jnp.zeros((B, RWKV_HEADS, RWKV_N, RWKV_N), F32)
    y_p, hT_p = _wkv_call(scan_in, h0_p, 0, B, S // CHUNK, CHUNK, WKV_SUB, math.gcd(B, WKV_PAR))
    h0_s = jnp.swapaxes(state_wkv[l], -1, -2)
    y_s, hT_s = _wkv_call(scan_in, h0_s, Tp, DB, 1, DS, 1, math.gcd(DB, WKV_SUB * WKV_PAR))

    wo_b = w_o[l].astype(BF16)
    wr = jnp.concatenate([w_gr[l], w_er[l], jnp.zeros((D, LANE - N_GROUPS - N_EXPERTS), F32)], axis=1)
    wr_hi = wr.astype(BF16)
    wr_lo = (wr - wr_hi.astype(F32)).astype(BF16)
    br = jnp.concatenate([b_gr[l], b_er[l], jnp.zeros((LANE - N_GROUPS - N_EXPERTS,), F32)])[None]
    n_tiles = T // TOKEN_TILE
    wave_tiles = [(t0, min(t0 + MOE_WAVE_TILES, n_tiles) - t0) for t0 in range(0, n_tiles, MOE_WAVE_TILES)]
    out_p, out_s = None, None
    for tile0, nt in wave_tiles:
        t_w = nt * TOKEN_TILE
        h, hpk, route, tile_counts = _mix_call(
            xp, xs_in, attn_p, attn_s, y_p, y_s, bonus, g, seg, vec(lnx_g[l]), vec(lnx_b[l]),
            wo_b[:MLA_HEADS * MLA_V], wo_b[MLA_HEADS * MLA_V:], vec(ln1_g[l]), vec(ln1_b[l]),
            jnp.stack([wr_hi, wr_lo]), br, tile0, nt)
        dest, block_e, n_used = _dispatch(route, tile_counts, t_w)
        win = lambda a: a.reshape(t_w // SC_WINDOW, SC_WINDOW)
        xs = _sc_scatter_rows(hpk, win(dest[:, 0]), win(dest[:, 1]), block_e.shape[0] * MOE_BLK)
        ys = _expert_call(block_e, n_used, xs, w_eg[l], w_eu[l], w_ed[l])
        yab = _sc_gather_rows(ys, jnp.concatenate([dest[:, 0], dest[:, 1]]))
        out_p, wave_s = _combine_call(h, yab, route, vec(ln2_g[l]), vec(ln2_b[l]), Tp, Ts, tile0, out_p)
        assert wave_s is None or out_s is None, "the sample tiles must fall inside one wave"
        out_s = wave_s if wave_s is not None else out_s

    y_prompt = out_p.reshape(B, S, D)
    y_sample = out_s.reshape(DB, DS, D)
    p_ckv = ckv_p.reshape(1, B, S, KV_LORA)
    p_kpe = kpe_p.reshape(1, B, S, MLA_ROPE)
    s_ckv = ckv_s.reshape(1, DB, DS, KV_LORA)
    s_kpe = kpe_s.reshape(1, DB, DS, MLA_ROPE)
    p_wkv = jnp.swapaxes(hT_p, -1, -2)[None]
    s_wkv = jnp.swapaxes(hT_s, -1, -2)[None]
    gp = S // SHIFT_GROUP
    p_sh = last_rows[gp - 1:B * gp:gp][None]
    s_sh = last_rows[B * gp:][None]
    return (y_prompt, y_sample, p_ckv, p_kpe, p_wkv, p_sh, s_ckv, s_kpe, s_wkv, s_sh)
```

```python
import functools
import math

import numpy as np
import jax
import jax.numpy as jnp
from jax import lax
from jax.experimental import pallas as pl
from jax.experimental.pallas import tpu as pltpu
from jax.experimental.pallas import tpu_sc as plsc

F32 = jnp.float32
BF16 = jnp.bfloat16

D_MODEL = 1024
CHUNK = 64
MLA_HEADS = 8
MLA_NOPE = 64
MLA_ROPE = 32
MLA_V = 64
Q_LORA = 384
KV_LORA = 256
ROPE_BASE = 10000.0
MLA_IN = Q_LORA + KV_LORA + MLA_ROPE
MLA_SCALE = (MLA_NOPE + MLA_ROPE) ** -0.5
RWKV_HEADS = 8
RWKV_N = 64
RWKV_DIM = RWKV_HEADS * RWKV_N
DECAY_LORA = 64
AAA_LORA = 64
GATE_LORA = 128
RWKV_IN = 3 * RWKV_DIM + DECAY_LORA + AAA_LORA + GATE_LORA
N_GROUPS = 4
EXPERTS_PER_GROUP = 8
N_EXPERTS = N_GROUPS * EXPERTS_PER_GROUP
TOP_K = 2
D_EXPERT = 256
MOE_BLK = 512
LN_EPS = 1e-5
RMS_EPS = 1e-6
GN_EPS = 64e-5
NEG_INF = -1e30
DEPTH = 1
DN_ALPHA = (2 * DEPTH) ** 0.25

LANE = 128
HEAD_PAD = 128
PROJ_KPE = Q_LORA + KV_LORA
PROJ_PR = PROJ_KPE + LANE
PROJ_W = PROJ_PR + RWKV_IN
SHIFT_GROUP = 32
TOKEN_TILE = 512
ATTN_TQ = 1024
ATTN_TK = 512
V_ONE_LANE = (MLA_V, 0)
LOG2E = math.log2(math.e)
VMEM_LIMIT = 48 * 1024 * 1024
PROJ_VMEM_LIMIT = 56 * 1024 * 1024
SC_CORES = 2
SC_SUBCORES = 16
SC_WINDOW = 32
WKV_GROUP = 4
WKV_SUB = 4
WKV_PAR = 2
MIX_PARTS = 2
MOE_WAVE_TILES = 34


def _cparams(sem):
    return pltpu.CompilerParams(dimension_semantics=sem, vmem_limit_bytes=VMEM_LIMIT)


def _split3(x):
    hi = x.astype(BF16)
    r1 = x - hi.astype(F32)
    mid = r1.astype(BF16)
    lo = (r1 - mid.astype(F32)).astype(BF16)
    return hi, mid, lo


def _dot(a, b):
    return jnp.dot(a, b, preferred_element_type=F32)


def _dot_nt(a, b):
    return lax.dot_general(a, b, (((1,), (1,)), ((), ())), preferred_element_type=F32)


def _dot_exact_rhs(x, w):
    hi = x.astype(BF16)
    lo = (x - hi.astype(F32)).astype(BF16)
    return _dot(hi, w) + _dot(lo, w)


def _dot_exact_lhs(w, x):
    hi, mid, lo = _split3(x)
    return _dot(w, hi) + _dot(w, mid) + _dot(w, lo)


def _pack_bf16_pairs(x):
    n = x.shape[1] // 2
    bits = pltpu.bitcast(x.astype(BF16).astype(F32), jnp.int32)
    return (bits[:, :n] & jnp.int32(-65536)) | lax.shift_right_logical(bits[:, n:], jnp.int32(16))


def _unpack_bf16_pairs(p):
    hi = pltpu.bitcast(p & jnp.int32(-65536), F32)
    lo = pltpu.bitcast(lax.shift_left(p, jnp.int32(16)), F32)
    return jnp.concatenate([hi, lo], axis=1)


def _sigmoid(x):
    return 1.0 / (1.0 + jnp.exp(-x))


def _layer_norm(x, g, b):
    xc = x - jnp.mean(x, -1, keepdims=True)
    var = jnp.mean(xc * xc, -1, keepdims=True)
    return xc * lax.rsqrt(var + LN_EPS) * g + b


def _proj_kernel(xp_ref, xs_ref, rope_ref, rp_ref, rc_ref, w1_ref, gq_ref, gkv_ref, wqa_ref, wk_ref, pk_ref, wv_ref,
                 shift_ref, *rest, n_p, tm, seq_tiles):
    rwkv_w = rest[:10]
    q_ref, k_ref, v_ref, ckvp_ref, ckvs_ref, kpep_ref, kpes_ref, last_ref = rest[10:18]
    rwkv_out = rest[18:26]
    carry_scr = rest[26]
    i = pl.program_id(0)
    is_p = i < n_p
    x = jnp.where(is_p, xp_ref[...], xs_ref[...]).astype(BF16)
    proj = _dot(x, w1_ref[...])
    c_q = proj[:, :Q_LORA]
    c_kv = proj[:, Q_LORA:Q_LORA + KV_LORA]
    kp = proj[:, PROJ_KPE:PROJ_PR]
    pr = proj[:, PROJ_PR:]

    ng = tm // SHIFT_GROUP
    pr3 = pr.reshape(ng, SHIFT_GROUP, RWKV_IN)
    last_ref[...] = pr3[:, SHIFT_GROUP - 1, :]
    rolled = pltpu.roll(pr, 1, 0).reshape(ng, SHIFT_GROUP, RWKV_IN)
    first_row = jnp.where(i % seq_tiles == 0, 0.0, carry_scr[...])
    bound = jnp.where(is_p, first_row[None], shift_ref[...])
    row_in_grp = lax.broadcasted_iota(jnp.int32, pr3.shape, 1)
    grp = lax.broadcasted_iota(jnp.int32, pr3.shape, 0)
    use_bound = (row_in_grp == 0) & (grp <= jnp.where(is_p, 0, ng))
    prev = jnp.where(use_bound, bound, rolled).reshape(tm, RWKV_IN)
    carry_scr[...] = pr[tm - 1:tm, :]

    cqn = c_q * lax.rsqrt(jnp.mean(c_q * c_q, -1, keepdims=True) + RMS_EPS) * gq_ref[...]
    ckv = c_kv * lax.rsqrt(jnp.mean(c_kv * c_kv, -1, keepdims=True) + RMS_EPS) * gkv_ref[...]

    rope = (_dot_exact_rhs(rope_ref[...], rp_ref[...]) + rc_ref[0:1, :]) * rc_ref[1:2, :]
    cq = rope[:, :LANE]
    sq_up = rope[:, LANE:2 * LANE]
    sq_dn = rope[:, 2 * LANE:3 * LANE]
    kt = rope[:, 3 * LANE:]
    prod = kp * kt
    kpe = prod[:, :MLA_ROPE] + prod[:, MLA_ROPE:2 * MLA_ROPE]

    qa = _dot(cqn.astype(BF16), wqa_ref[...])
    half = MLA_ROPE // 2
    for h in range(MLA_HEADS):
        qh = qa[:, h * HEAD_PAD:(h + 1) * HEAD_PAD]
        rot = pltpu.roll(qh, HEAD_PAD - half, 1) * sq_up + pltpu.roll(qh, half, 1) * sq_dn
        q_ref[:, h * HEAD_PAD:(h + 1) * HEAD_PAD] = (qh * cq + rot).astype(BF16)

    ckv_b = ckv.astype(BF16)
    k = _dot(ckv_b, wk_ref[...]) + _dot(kpe.astype(BF16), pk_ref[...])
    k_ref[...] = k.astype(BF16)
    lane = lax.broadcasted_iota(jnp.int32, (1, MLA_HEADS * HEAD_PAD), 1)
    odd = (lane // HEAD_PAD) % 2
    one_lane = jnp.where(odd == 1, V_ONE_LANE[1], V_ONE_LANE[0])
    v_one = jnp.where(lane % HEAD_PAD == one_lane, 1.0, 0.0)
    v_ref[...] = (_dot(ckv_b, wv_ref[...]) + v_one).astype(BF16)
    _rwkv_heads(pr, prev, *rwkv_w, *rwkv_out)

    @pl.when(is_p)
    def _():
        ckvp_ref[...] = ckv
        kpep_ref[...] = kpe

    @pl.when(jnp.logical_not(is_p))
    def _():
        ckvs_ref[...] = ckv
        kpes_ref[...] = kpe


def _split_rows(n_p, tile0=0):
    return ((lambda i: (jnp.minimum(i + tile0, n_p - 1), 0)),
            (lambda i: (jnp.maximum(i + tile0 - n_p, 0), 0)))


def _proj_call(xp, xs, rope, seq_tiles, rope_place, rope_rows, w1, gq, gkv, wqa, wk, pk, wv, shift, rwkv_w):
    Tp, Ts = xp.shape[0], xs.shape[0]
    T = Tp + Ts
    tm = TOKEN_TILE
    n_p = Tp // tm
    rope_tiles = seq_tiles
    row = lambda i: (i, 0)
    full = lambda i: (0, 0)
    row_p, row_s = _split_rows(n_p)
    wide = MLA_HEADS * HEAD_PAD
    ng = tm // SHIFT_GROUP
    once = pl.Buffered(1)
    vec = pl.BlockSpec((1, RWKV_DIM), full, pipeline_mode=once)
    rwkv_specs = [pl.BlockSpec((1, RWKV_IN), full, pipeline_mode=once), vec, vec, vec, vec, vec,
                  pl.BlockSpec((DECAY_LORA, RWKV_DIM), full, pipeline_mode=once),
                  pl.BlockSpec((AAA_LORA, RWKV_DIM), full, pipeline_mode=once),
                  pl.BlockSpec((GATE_LORA, RWKV_DIM), full, pipeline_mode=once),
                  pl.BlockSpec((RWKV_DIM, RWKV_DIM), full, pipeline_mode=once)]
    tok = pl.BlockSpec((tm, RWKV_DIM), row)
    return pl.pallas_call(
        functools.partial(_proj_kernel, n_p=n_p, tm=tm, seq_tiles=seq_tiles),
        grid=(T // tm,),
        in_specs=[
            pl.BlockSpec((tm, D_MODEL), row_p),
            pl.BlockSpec((tm, D_MODEL), row_s),
            pl.BlockSpec((tm, MLA_ROPE), lambda i: (jnp.where(i < n_p, i % rope_tiles, rope_tiles), 0)),
            pl.BlockSpec((MLA_ROPE, 4 * LANE), full, pipeline_mode=once),
            pl.BlockSpec((2, 4 * LANE), full, pipeline_mode=once),
            pl.BlockSpec((D_MODEL, PROJ_W), full, pipeline_mode=once),
            pl.BlockSpec((1, Q_LORA), full, pipeline_mode=once),
            pl.BlockSpec((1, KV_LORA), full, pipeline_mode=once),
            pl.BlockSpec((Q_LORA, wide), full, pipeline_mode=once),
            pl.BlockSpec((KV_LORA, wide), full, pipeline_mode=once),
            pl.BlockSpec((MLA_ROPE, wide), full, pipeline_mode=once),
            pl.BlockSpec((KV_LORA, wide), full, pipeline_mode=once),
            pl.BlockSpec((ng, 1, RWKV_IN), lambda i: (jnp.maximum(i - n_p, 0), 0, 0)),
        ] + rwkv_specs,
        out_specs=[
            pl.BlockSpec((tm, wide), row),
            pl.BlockSpec((tm, wide), row),
            pl.BlockSpec((tm, wide), row),
            pl.BlockSpec((tm, KV_LORA), row_p),
            pl.BlockSpec((tm, KV_LORA), row_s),
            pl.BlockSpec((tm, MLA_ROPE), row_p),
            pl.BlockSpec((tm, MLA_ROPE), row_s),
            pl.BlockSpec((ng, RWKV_IN), row),
        ] + [tok] * 8,
        out_shape=[
            jax.ShapeDtypeStruct((T, wide), BF16),
            jax.ShapeDtypeStruct((T, wide), BF16),
            jax.ShapeDtypeStruct((T, wide), BF16),
            jax.ShapeDtypeStruct((Tp, KV_LORA), F32),
            jax.ShapeDtypeStruct((Ts, KV_LORA), F32),
            jax.ShapeDtypeStruct((Tp, MLA_ROPE), F32),
            jax.ShapeDtypeStruct((Ts, MLA_ROPE), F32),
            jax.ShapeDtypeStruct((T // SHIFT_GROUP, RWKV_IN), F32),
        ] + [jax.ShapeDtypeStruct((T, RWKV_DIM), F32)] * 8,
        scratch_shapes=[pltpu.VMEM((1, RWKV_IN), F32)],
        compiler_params=pltpu.CompilerParams(dimension_semantics=("arbitrary",), vmem_limit_bytes=PROJ_VMEM_LIMIT),
        name="proj",
    )(xp, xs, rope, rope_place, rope_rows, w1, gq, gkv, wqa, wk, pk, wv, shift, *rwkv_w)


def _attn_kernel(q_ref, k_ref, v_ref, o_ref, m_scr, acc_scr, *, tq, tk):
    qi = pl.program_id(2)
    m_scr[...] = jnp.full(m_scr.shape, NEG_INF, F32)
    acc_scr[...] = jnp.zeros(acc_scr.shape, F32)
    n_diag = tq // tk

    H = range(2)
    sls = [slice(h * HEAD_PAD, (h + 1) * HEAD_PAD) for h in H]

    def scores(k0, width, rows):
        return [_dot_nt(q_ref[rows, sl], k_ref[pl.ds(k0, width), sl]) for sl in sls]

    def accumulate(s, k0, width, rows):
        m_prev = [m_scr[h, rows, :] for h in H]
        m_new = [jnp.maximum(m_prev[h], jnp.max(s[h], axis=1, keepdims=True)) for h in H]
        pexp = [jnp.exp2(s[h] - jnp.tile(m_new[h], (1, width // LANE))).astype(BF16) for h in H]
        pv = [_dot(pexp[h], v_ref[pl.ds(k0, width), sls[h]]) for h in H]
        for h in H:
            acc_scr[h, rows, :] = jnp.exp2(m_prev[h] - m_new[h]) * acc_scr[h, rows, :] + pv[h]
            m_scr[h, rows, :] = m_new[h]

    def kv_block(k0, width, rows, masked):
        s = scores(k0, width, rows)
        if masked:
            n_rows = rows.stop - rows.start
            r = lax.broadcasted_iota(jnp.int32, (n_rows, width), 0) // CHUNK
            c = lax.broadcasted_iota(jnp.int32, (n_rows, width), 1) // CHUNK
            s = [jnp.where(c <= r, s[h], NEG_INF) for h in H]
        accumulate(s, k0, width, rows)

    all_rows = slice(0, tq)

    def pair(t, carry):
        k0a = pl.multiple_of(2 * t * tq, tq)
        k0b = pl.multiple_of(k0a + tq, tq)
        s_a = scores(k0a, tq, all_rows)
        s_b = scores(k0b, tq, all_rows)
        accumulate(s_a, k0a, tq, all_rows)
        accumulate(s_b, k0b, tq, all_rows)
        return carry

    lax.fori_loop(0, qi // 2, pair, 0)

    @pl.when(qi % 2 == 1)
    def _():
        kv_block(pl.multiple_of((qi - 1) * tq, tq), tq, all_rows, False)

    for d in range(n_diag):
        k0 = pl.multiple_of(qi * tq + d * tk, tk)
        kv_block(k0, tk, slice(d * tk, (d + 1) * tk), True)
        if (d + 1) * tk < tq:
            kv_block(k0, tk, slice((d + 1) * tk, tq), False)
    acc0, acc1 = acc_scr[0], acc_scr[1]
    lane = lax.broadcasted_iota(jnp.int32, acc0.shape, 1)
    l0 = acc0[:, V_ONE_LANE[0]:V_ONE_LANE[0] + 1]
    l1 = acc1[:, V_ONE_LANE[1]:V_ONE_LANE[1] + 1]
    o_ref[...] = jnp.where(lane < MLA_V, acc0 / l0, acc1 / l1).astype(o_ref.dtype)


def _attn_call(q, k, v, n_batch, seq):
    tq, tk = ATTN_TQ, ATTN_TK
    nq = seq // tq
    hp = MLA_HEADS // 2
    resident = pl.BlockSpec((seq, 2 * HEAD_PAD), lambda b, h, i: (b, h), pipeline_mode=pl.Buffered(1))
    return pl.pallas_call(
        functools.partial(_attn_kernel, tq=tq, tk=tk),
        grid=(n_batch, hp, nq),
        in_specs=[
            pl.BlockSpec((tq, 2 * HEAD_PAD), lambda b, h, i: (b * nq + i, h)),
            resident,
            resident,
        ],
        out_specs=pl.BlockSpec((tq, LANE), lambda b, h, i: (b * nq + i, h)),
        out_shape=jax.ShapeDtypeStruct((n_batch * seq, hp * LANE), BF16),
        scratch_shapes=[
            pltpu.VMEM((2, tq, LANE), F32),
            pltpu.VMEM((2, tq, LANE), F32),
        ],
        compiler_params=_cparams(("parallel", "parallel", "arbitrary")),
        name="attn",
    )(q, k, v)


def _mla_sample_kernel(q_ref, cn_ref, kn_ref, cp_ref, kp_ref, wuk_ref, wuv_ref, o_ref, *, past, dec):
    cp = cp_ref[0].astype(BF16)
    kp = kp_ref[0].astype(BF16)
    cn = cn_ref[...].astype(BF16)
    kn = kn_ref[...].astype(BF16)
    R = MLA_HEADS * dec
    qrow = (past + lax.broadcasted_iota(jnp.int32, (R, past), 0) % dec) // CHUNK
    vis_p = (lax.broadcasted_iota(jnp.int32, (R, past), 1) // CHUNK) <= qrow
    qrow_n = (past + lax.broadcasted_iota(jnp.int32, (R, dec), 0) % dec) // CHUNK
    vis_n = ((past + lax.broadcasted_iota(jnp.int32, (R, dec), 1)) // CHUNK) <= qrow_n
    q_lat, qp = [], []
    for h in range(MLA_HEADS):
        qn = q_ref[:, h * HEAD_PAD:h * HEAD_PAD + MLA_NOPE]
        qp.append(q_ref[:, h * HEAD_PAD + MLA_NOPE:h * HEAD_PAD + MLA_NOPE + MLA_ROPE])
        q_lat.append(_dot(qn, wuk_ref[h]).astype(BF16))
    q_lat = jnp.concatenate(q_lat, axis=0)
    qp = jnp.concatenate(qp, axis=0)
    s_p = jnp.where(vis_p, _dot_nt(q_lat, cp) + _dot_nt(qp, kp), NEG_INF)
    s_n = jnp.where(vis_n, _dot_nt(q_lat, cn) + _dot_nt(qp, kn), NEG_INF)
    m = jnp.maximum(jnp.max(s_p, axis=1, keepdims=True), jnp.max(s_n, axis=1, keepdims=True))
    e_p = jnp.exp2(s_p - m)
    e_n = jnp.exp2(s_n - m)
    l = jnp.sum(e_p, axis=1, keepdims=True) + jnp.sum(e_n, axis=1, keepdims=True)
    o_lat = ((_dot(e_p.astype(BF16), cp) + _dot(e_n.astype(BF16), cn)) / l).astype(BF16)
    out = jnp.zeros((dec, MLA_HEADS * MLA_V), F32)
    for h in range(MLA_HEADS):
        out = out + _dot(o_lat[h * dec:(h + 1) * dec], wuv_ref[h])
    o_ref[...] = out.astype(o_ref.dtype)


def _mla_sample_call(q, ckv, kpe, cache_ckv, cache_kpe, wuk, wuv, row0, n_seq, dec):
    past = cache_ckv.shape[1]
    blk0 = row0 // dec
    wide = MLA_HEADS * HEAD_PAD
    return pl.pallas_call(
        functools.partial(_mla_sample_kernel, past=past, dec=dec),
        grid=(n_seq,),
        in_specs=[
            pl.BlockSpec((dec, wide), lambda b: (blk0 + b, 0)),
            pl.BlockSpec((dec, KV_LORA), lambda b: (b, 0)),
            pl.BlockSpec((dec, MLA_ROPE), lambda b: (b, 0)),
            pl.BlockSpec((1, past, KV_LORA), lambda b: (b, 0, 0)),
            pl.BlockSpec((1, past, MLA_ROPE), lambda b: (b, 0, 0)),
            pl.BlockSpec((MLA_HEADS, MLA_NOPE, KV_LORA), lambda b: (0, 0, 0)),
            pl.BlockSpec((MLA_HEADS, KV_LORA, MLA_HEADS * MLA_V), lambda b: (0, 0, 0)),
        ],
        out_specs=pl.BlockSpec((dec, MLA_HEADS * MLA_V), lambda b: (b, 0)),
        out_shape=jax.ShapeDtypeStruct((n_seq * dec, MLA_HEADS * MLA_V), BF16),
        compiler_params=_cparams(("parallel",)),
        name="mla_sample",
    )(q, ckv, kpe, cache_ckv, cache_kpe, wuk, wuv)


def _rwkv_heads(pr, prev, mu_ref, w0_ref, a0_ref, kk_ref, ka_ref, rk_ref, w2_ref, a2_ref, g2_ref,
                seg_ref, r_ref, lw_ref, kh_ref, v_ref, na_ref, b_ref, bonus_ref, g_ref):
    u = pr + mu_ref[...] * (prev - pr)
    o1, o2, o3 = RWKV_DIM, 2 * RWKV_DIM, 3 * RWKV_DIM
    o4, o5 = o3 + DECAY_LORA, o3 + DECAY_LORA + AAA_LORA
    r, k, v = u[:, :o1], u[:, o1:o2], u[:, o2:o3]
    w_lo, a_lo, g_lo = u[:, o3:o4], u[:, o4:o5], u[:, o5:]
    wl = w0_ref[...] + _dot(jnp.tanh(w_lo).astype(BF16), w2_ref[...])
    lw_ref[...] = -math.exp(-0.5) * _sigmoid(wl)
    a = _sigmoid(a0_ref[...] + _dot(a_lo.astype(BF16), a2_ref[...]))
    g_ref[...] = _dot(_sigmoid(g_lo).astype(BF16), g2_ref[...])
    seg = seg_ref[...]
    kk = k * kk_ref[...]
    kk = kk / jnp.maximum(jnp.sqrt(_dot_exact_rhs(kk * kk, seg)), 1e-12)
    kh = k * (1.0 + (a - 1.0) * ka_ref[...])
    r_ref[...] = r
    kh_ref[...] = kh
    v_ref[...] = v
    na_ref[...] = -kk
    b_ref[...] = kk * a
    bonus_ref[...] = _dot_exact_rhs(r * kh * rk_ref[...], seg) * v


def _wkv_kernel(*refs, C, n_sub, n_par):
    GW = WKV_GROUP * RWKV_N
    R = WKV_GROUP * C
    n_grp = RWKV_HEADS // WKV_GROUP
    n_lev = int(round(math.log2(C))) - 1
    c = pl.program_id(1)
    tok = [refs[6 * p:6 * p + 6] for p in range(n_par)]
    h0_ref, y_ref, hT_ref, h_scr = refs[6 * n_par:]
    n_state = n_par * n_grp

    def head_block(hh):
        return slice(hh * RWKV_N, (hh + 1) * RWKV_N)

    @pl.when(c == 0)
    def _():
        h_scr[...] = jnp.zeros(h_scr.shape, F32)
        for p in range(n_par):
            for hd in range(RWKV_HEADS):
                g, hh = divmod(hd, WKV_GROUP)
                h_scr[p * n_grp + g, head_block(hh), head_block(hh)] = h0_ref[p, hd]

    row = lax.broadcasted_iota(jnp.int32, (C, C), 0)
    col = lax.broadcasted_iota(jnp.int32, (C, C), 1)
    tri = jnp.where(col <= row, 1.0, 0.0).astype(BF16)
    rr = lax.broadcasted_iota(jnp.int32, (R, R), 0)
    cc = lax.broadcasted_iota(jnp.int32, (R, R), 1)
    same = (rr // C) == (cc // C)
    ti = lax.broadcasted_iota(jnp.int32, (C, R), 0)
    si = lax.broadcasted_iota(jnp.int32, (C, R), 1) % C
    strict4 = si < ti
    lower4 = si <= ti
    eye4 = jnp.where(si == ti, 1.0, 0.0)
    keep = (lax.broadcasted_iota(jnp.int32, (R, GW), 0) // C
            == lax.broadcasted_iota(jnp.int32, (R, GW), 1) // RWKV_N)
    gr = lax.broadcasted_iota(jnp.int32, (GW, GW), 0)
    gc = lax.broadcasted_iota(jnp.int32, (GW, GW), 1)
    eye_g = gr == gc
    same_head = (gr // RWKV_N) == (gc // RWKV_N)
    eye_g_bf = jnp.where(eye_g, 1.0, 0.0).astype(BF16)

    def rows4(x4):
        return jnp.concatenate([x4] * WKV_GROUP, axis=0)

    def stack(x4):
        return jnp.where(keep, rows4(x4), jnp.zeros((), x4.dtype))


    sls = [slice(g * GW, (g + 1) * GW) for g in range(n_grp)]
    J = [(ci, q) for ci in range(n_sub) for q in range(n_state)]
    ops, p_end = {}, {}
    for p, ci in [(p, ci) for p in range(n_par) for ci in range(n_sub)]:
        r_ref, lw_ref, k_ref, v_ref, a_ref, b_ref = tok[p]
        rows = slice(ci * C, (ci + 1) * C)
        lw = lw_ref[rows, :]
        cum = _dot_exact_lhs(tri, lw)
        cum_end = cum[C - 1:C, :]
        e_neg = jnp.exp(-cum)
        e_end = jnp.exp(cum_end - cum)
        b_in = b_ref[rows, :]
        k_in = k_ref[rows, :]
        full = ((a_ref[rows, :] * jnp.exp(cum - lw)).astype(BF16),
                (r_ref[rows, :] * jnp.exp(cum)).astype(BF16),
                (b_in * e_neg).astype(BF16), (k_in * e_neg).astype(BF16),
                (b_in * e_end).astype(BF16), (k_in * e_end).astype(BF16),
                v_ref[rows, :].astype(BF16))
        p_end[p, ci] = jnp.exp(cum_end)
        for g in range(n_grp):
            ops[ci, p * n_grp + g] = [t[:, sls[g]] for t in full]
    a4, r4, b4, k4, be4, ke4, v4 = [{j: ops[j][i] for j in J} for i in range(7)]
    v_s = {j: stack(v4[j]) for j in J}
    m = {j: _dot_nt(jnp.concatenate([a4[j], r4[j]], axis=0),
                    jnp.concatenate([stack(b4[j]), stack(k4[j])], axis=0)) for j in J}
    l4 = {j: jnp.where(strict4, m[j][:C, :R], 0.0) for j in J}
    a_ak = {j: jnp.where(strict4, m[j][:C, R:], 0.0).astype(BF16) for j in J}
    a_rb = {j: jnp.where(lower4, m[j][C:, :R], 0.0).astype(BF16) for j in J}
    a_rk = {j: jnp.where(lower4, m[j][C:, R:], 0.0).astype(BF16) for j in J}
    def block_diag(x4):
        return jnp.where(same, rows4(x4.astype(BF16)), jnp.zeros((), BF16))

    t4 = {j: eye4 + l4[j] for j in J}
    l_bd = {j: block_diag(l4[j]) for j in J}
    for _ in range(n_lev):
        l4 = {j: _dot(l4[j].astype(BF16), l_bd[j]) for j in J}
        l_bd = {j: block_diag(l4[j]) for j in J}
        t4 = {j: t4[j] + _dot(t4[j].astype(BF16), l_bd[j]) for j in J}
    t_b = {j: t4[j].astype(BF16) for j in J}
    bke_t = {j: _dot_nt(eye_g_bf, jnp.concatenate([be4[j], ke4[j]], axis=0)).astype(BF16) for j in J}

    G = range(n_state)
    h_cur = [h_scr[q] for q in G]
    for ci in range(n_sub):
        rows = slice(ci * C, (ci + 1) * C)
        h0_b = [h_cur[g].astype(BF16) for g in G]
        x4 = [_dot(a4[ci, g], h0_b[g]) + _dot(a_ak[ci, g], v_s[ci, g]) for g in G]
        u4 = [_dot(t_b[ci, g], stack(x4[g].astype(BF16))).astype(BF16) for g in G]
        y4 = [_dot(r4[ci, g], h0_b[g]) + _dot(a_rb[ci, g], stack(u4[g])) + _dot(a_rk[ci, g], v_s[ci, g])
              for g in G]
        uv4 = [jnp.concatenate([u4[g], v4[ci, g]], axis=0) for g in G]
        h_add = [jnp.where(same_head, _dot(bke_t[ci, g], uv4[g]), 0.0) for g in G]
        for q in G:
            p, g = divmod(q, n_grp)
            y_ref[p, rows, sls[g]] = y4[q]
            p_col = jnp.sum(jnp.where(eye_g, p_end[p, ci][:, sls[g]], 0.0), axis=1, keepdims=True)
            h_cur[q] = p_col * h_cur[q] + h_add[q]

    for q in G:
        h_scr[q] = h_cur[q]

    @pl.when(c == pl.num_programs(1) - 1)
    def _():
        for p in range(n_par):
            for hd in range(RWKV_HEADS):
                g, hh = divmod(hd, WKV_GROUP)
                hT_ref[p, hd] = h_scr[p * n_grp + g, head_block(hh), head_block(hh)]


def _wkv_call(arrs, h0, row0, n_seq, n_chunk, C, n_sub, n_par):
    rows = C * n_sub
    steps = n_chunk // n_sub
    blk0 = row0 // rows
    GW = WKV_GROUP * RWKV_N
    n_grp = RWKV_HEADS // WKV_GROUP
    assert n_seq % n_par == 0 and n_chunk % n_sub == 0
    tok = [pl.BlockSpec((rows, RWKV_DIM), lambda b, c, p=p: (blk0 + (b * n_par + p) * steps + c, 0))
           for p in range(n_par)]
    st = pl.BlockSpec((n_par, RWKV_HEADS, RWKV_N, RWKV_N), lambda b, c: (b, 0, 0, 0))
    y, h_fin = pl.pallas_call(
        functools.partial(_wkv_kernel, C=C, n_sub=n_sub, n_par=n_par),
        grid=(n_seq // n_par, steps),
        in_specs=[tok[p] for p in range(n_par) for _ in range(6)] + [st],
        out_specs=[pl.BlockSpec((n_par, rows, RWKV_DIM), lambda b, c: (b, c, 0)), st],
        out_shape=[
            jax.ShapeDtypeStruct((n_seq, n_chunk * C, RWKV_DIM), F32),
            jax.ShapeDtypeStruct((n_seq, RWKV_HEADS, RWKV_N, RWKV_N), F32),
        ],
        scratch_shapes=[pltpu.VMEM((n_par * n_grp, GW, GW), F32)],
        compiler_params=_cparams(("parallel", "arbitrary")),
        name="wkv_c%d" % C,
    )(*(list(arrs) * n_par), h0)
    return y.reshape(n_seq * n_chunk * C, RWKV_DIM), h_fin


def _mix_kernel(xp_ref, xs_ref, attnp_ref, attns_ref, yp_ref, ys_ref, bonus_ref, g_ref, seg_ref, lng_ref,
                lnb_ref, woa_ref, wob_ref, g1_ref, b1_ref, wr_ref, br_ref, tri_ref, h_ref, hpk_ref, route_ref,
                count_ref, *, n_p, tm):
    is_p = pl.program_id(0) < n_p
    P = range(MIX_PARTS)
    bands = [slice(k * tm // MIX_PARTS, (k + 1) * tm // MIX_PARTS) for k in P]
    seg = seg_ref[...]
    inv_n = 1.0 / RWKV_N
    y = [jnp.where(is_p, yp_ref[r, :], ys_ref[r, :]) for r in bands]
    yc = [y[k] - _dot_exact_rhs(y[k], seg) * inv_n for k in P]
    var = [_dot_exact_rhs(yc[k] * yc[k], seg) * inv_n for k in P]
    yn = [yc[k] * lax.rsqrt(var[k] + GN_EPS) * lng_ref[...] + lnb_ref[...] for k in P]
    rw = [((yn[k] + bonus_ref[bands[k], :]) * g_ref[bands[k], :]).astype(BF16) for k in P]
    attn = [jnp.where(is_p, attnp_ref[r, :], attns_ref[r, :]) for r in bands]
    m = [_dot(attn[k], woa_ref[...]) + _dot(rw[k], wob_ref[...]) for k in P]
    x = [jnp.where(is_p, xp_ref[r, :], xs_ref[r, :]) for r in bands]
    h = [_layer_norm(DN_ALPHA * x[k] + m[k], g1_ref[...], b1_ref[...]) for k in P]
    for k in P:
        h_ref[bands[k], :] = h[k]
        hpk_ref[bands[k], :] = _pack_bf16_pairs(h[k])

    h_hi = [h[k].astype(BF16) for k in P]
    h_lo = [(h[k] - h_hi[k].astype(F32)).astype(BF16) for k in P]
    logits = [_dot(h_hi[k], wr_ref[0]) + _dot(h_lo[k], wr_ref[0]) + _dot(h_hi[k], wr_ref[1]) + br_ref[...]
              for k in P]
    lane = lax.broadcasted_iota(jnp.int32, logits[0].shape, 1)
    big = jnp.int32(LANE)

    def route_band(lg):
        gl = jnp.where(lane < N_GROUPS, lg, NEG_INF)
        gmax = jnp.max(gl, axis=1, keepdims=True)
        grp = jnp.min(jnp.where(gl == gmax, lane, big), axis=1, keepdims=True)
        p_grp = 1.0 / jnp.sum(jnp.exp(gl - gmax), axis=1, keepdims=True)
        e_idx = lane - N_GROUPS
        in_grp = (lane >= N_GROUPS) & (lane < N_GROUPS + N_EXPERTS) & ((e_idx // EXPERTS_PER_GROUP) == grp)
        el = jnp.where(in_grp, lg, NEG_INF)
        m1 = jnp.max(el, axis=1, keepdims=True)
        i1 = jnp.min(jnp.where(el == m1, lane, big), axis=1, keepdims=True)
        el2 = jnp.where(lane == i1, NEG_INF, el)
        m2 = jnp.max(el2, axis=1, keepdims=True)
        i2 = jnp.min(jnp.where(el2 == m2, lane, big), axis=1, keepdims=True)
        t = jnp.exp(m2 - m1)
        g1 = p_grp / (1.0 + t)
        return i1, i2, g1, g1 * t

    routed = [route_band(logits[k]) for k in P]
    chosen = jnp.concatenate(
        [jnp.where(lane == routed[k][0], 1.0, jnp.where(lane == routed[k][1], 1.0, 0.0)) for k in P], axis=0)
    before = _dot(tri_ref[...], chosen.astype(BF16))
    count_ref[0] = jnp.sum(chosen, axis=0, keepdims=True)
    for k in P:
        i1, i2, g1, g2 = routed[k]
        bef = before[bands[k]]
        r1 = jnp.sum(jnp.where(lane == i1, bef, 0.0), axis=1, keepdims=True)
        r2 = jnp.sum(jnp.where(lane == i2, bef, 0.0), axis=1, keepdims=True)
        cols = ((i1 - N_GROUPS).astype(F32), (i2 - N_GROUPS).astype(F32), g1, g2, r1, r2)
        route = jnp.zeros(lane.shape, F32)
        for j, col in enumerate(cols):
            route = jnp.where(lane == j, col, route)
        route_ref[bands[k], :] = route


def _mix_call(xp, xs, attn_p, attn_s, y_p, y_s, bonus, g, seg, lnx_g, lnx_b, woa, wob, ln1_g, ln1_b, wr, br,
              tile0, n_tiles):
    tm = TOKEN_TILE
    T = n_tiles * tm
    n_p = xp.shape[0] // tm
    row = lambda i: (i, 0)
    full = lambda i: (0, 0)
    row_p, row_s = _split_rows(n_p, tile0)
    half = pl.BlockSpec((tm, RWKV_DIM), lambda i: (i + tile0, 0))
    vec5 = pl.BlockSpec((1, RWKV_DIM), full)
    vec10 = pl.BlockSpec((1, D_MODEL), full)
    idx = np.arange(tm)
    tri = jnp.asarray((idx[None, :] < idx[:, None]).astype(np.float32)).astype(BF16)
    return pl.pallas_call(
        functools.partial(_mix_kernel, n_p=n_p - tile0, tm=tm),
        grid=(n_tiles,),
        in_specs=[
            pl.BlockSpec((tm, D_MODEL), row_p), pl.BlockSpec((tm, D_MODEL), row_s),
            pl.BlockSpec((tm, RWKV_DIM), row_p), pl.BlockSpec((tm, RWKV_DIM), row_s),
            pl.BlockSpec((tm, RWKV_DIM), row_p), pl.BlockSpec((tm, RWKV_DIM), row_s),
            half, half,
            pl.BlockSpec((RWKV_DIM, RWKV_DIM), full), vec5, vec5,
            pl.BlockSpec((RWKV_DIM, D_MODEL), full), pl.BlockSpec((RWKV_DIM, D_MODEL), full),
            vec10, vec10,
            pl.BlockSpec((2, D_MODEL, LANE), lambda i: (0, 0, 0)), pl.BlockSpec((1, LANE), full),
            pl.BlockSpec((tm, tm), full),
        ],
        out_specs=[pl.BlockSpec((tm, D_MODEL), row), pl.BlockSpec((tm, D_MODEL // 2), row),
                   pl.BlockSpec((tm, LANE), row), pl.BlockSpec((1, 1, LANE), lambda i: (i, 0, 0))],
        out_shape=[jax.ShapeDtypeStruct((T, D_MODEL), F32), jax.ShapeDtypeStruct((T, D_MODEL // 2), jnp.int32),
                   jax.ShapeDtypeStruct((T, LANE), F32), jax.ShapeDtypeStruct((T // tm, 1, LANE), F32)],
        compiler_params=_cparams(("parallel",)),
        name="mix",
    )(xp, xs, attn_p, attn_s, y_p, y_s, bonus, g, seg, lnx_g, lnx_b, woa, wob, ln1_g, ln1_b, wr, br, tri)


def _expert_kernel(be_ref, nu_ref, xs_ref, wg_ref, wu_ref, wd_ref, ys_ref, wgu_b, wd_b):
    i = pl.program_id(0)

    @pl.when((i == 0) | (be_ref[i] != be_ref[jnp.maximum(i - 1, 0)]))
    def _():
        wgu_b[:, :D_EXPERT] = wg_ref[0].astype(BF16)
        wgu_b[:, D_EXPERT:] = wu_ref[0].astype(BF16)
        wd_b[...] = wd_ref[0].astype(BF16)

    @pl.when(i < nu_ref[0])
    def _():
        P = range(2)
        bands = [slice(k * MOE_BLK // 2, (k + 1) * MOE_BLK // 2) for k in P]
        xb = [_unpack_bf16_pairs(xs_ref[r, :]).astype(BF16) for r in bands]
        gu = [_dot(xb[k], wgu_b[...]) for k in P]
        act = [(gu[k][:, :D_EXPERT] * _sigmoid(gu[k][:, :D_EXPERT]) * gu[k][:, D_EXPERT:]).astype(BF16)
               for k in P]
        out = [_dot(act[k], wd_b[...]) for k in P]
        for k in P:
            ys_ref[bands[k], :] = _pack_bf16_pairs(out[k])

    @pl.when(i >= nu_ref[0])
    def _():
        ys_ref[...] = jnp.zeros(ys_ref.shape, ys_ref.dtype)


def _expert_call(block_e, n_used, xs, wg, wu, wd):
    n_blk = xs.shape[0] // MOE_BLK
    grid_spec = pltpu.PrefetchScalarGridSpec(
        num_scalar_prefetch=2,
        grid=(n_blk,),
        in_specs=[
            pl.BlockSpec((MOE_BLK, D_MODEL // 2), lambda i, be, nu: (i, 0)),
            pl.BlockSpec((1, D_MODEL, D_EXPERT), lambda i, be, nu: (be[i], 0, 0)),
            pl.BlockSpec((1, D_MODEL, D_EXPERT), lambda i, be, nu: (be[i], 0, 0)),
            pl.BlockSpec((1, D_EXPERT, D_MODEL), lambda i, be, nu: (be[i], 0, 0)),
        ],
        out_specs=pl.BlockSpec((MOE_BLK, D_MODEL // 2), lambda i, be, nu: (i, 0)),
        scratch_shapes=[pltpu.VMEM((D_MODEL, 2 * D_EXPERT), BF16), pltpu.VMEM((D_EXPERT, D_MODEL), BF16)],
    )
    return pl.pallas_call(
        _expert_kernel,
        grid_spec=grid_spec,
        out_shape=jax.ShapeDtypeStruct((n_blk * MOE_BLK, D_MODEL // 2), jnp.int32),
        compiler_params=_cparams(("arbitrary",)),
        name="experts",
    )(block_e, n_used, xs, wg, wu, wd)


def _combine_kernel(h_ref, ya_ref, yb_ref, route_ref, g2_ref, b2_ref, *rest, n_p, has_prev, has_prompt,
                    has_sample):
    outs = list(rest[1:] if has_prev else rest)
    i = pl.program_id(0)
    route = route_ref[...]
    f = _unpack_bf16_pairs(ya_ref[...]) * route[:, 2:3] + _unpack_bf16_pairs(yb_ref[...]) * route[:, 3:4]
    out = _layer_norm(DN_ALPHA * h_ref[...] + f, g2_ref[...], b2_ref[...])

    if has_prompt:
        op_ref = outs.pop(0)

        @pl.when(i < n_p)
        def _():
            op_ref[...] = out

    if has_sample:
        os_ref = outs.pop(0)

        @pl.when(i >= n_p)
        def _():
            os_ref[...] = out


def _combine_call(h, yab, route, ln2_g, ln2_b, t_prompt, t_sample, tile0, out_p_prev):
    tm = TOKEN_TILE
    n_t, n_p = h.shape[0] // tm, t_prompt // tm
    has_prompt = tile0 < n_p
    has_sample = tile0 + n_t > n_p
    has_prev = has_prompt and out_p_prev is not None
    row = lambda i: (i, 0)
    full = lambda i: (0, 0)
    row_p, row_s = _split_rows(n_p, tile0)
    big = pl.BlockSpec((tm, D_MODEL), row)
    in_specs = [big, pl.BlockSpec((tm, D_MODEL // 2), row),
                pl.BlockSpec((tm, D_MODEL // 2), lambda i: (i + n_t, 0)), pl.BlockSpec((tm, LANE), row),
                pl.BlockSpec((1, D_MODEL), full), pl.BlockSpec((1, D_MODEL), full)]
    args = [h, yab, yab, route, ln2_g, ln2_b]
    out_specs, out_shape, aliases = [], [], {}
    if has_prompt:
        out_specs.append(pl.BlockSpec((tm, D_MODEL), row_p))
        out_shape.append(jax.ShapeDtypeStruct((t_prompt, D_MODEL), F32))
    if has_prev:
        in_specs.append(pl.BlockSpec(memory_space=pl.ANY))
        args.append(out_p_prev)
        aliases = {len(args) - 1: 0}
    if has_sample:
        out_specs.append(pl.BlockSpec((tm, D_MODEL), row_s))
        out_shape.append(jax.ShapeDtypeStruct((t_sample, D_MODEL), F32))
    outs = pl.pallas_call(
        functools.partial(_combine_kernel, n_p=n_p - tile0, has_prev=has_prev, has_prompt=has_prompt,
                          has_sample=has_sample),
        grid=(n_t,),
        in_specs=in_specs,
        out_specs=out_specs,
        out_shape=out_shape,
        input_output_aliases=aliases,
        compiler_params=_cparams(("arbitrary",)),
        name="combine",
    )(*args)
    out_p = outs[0] if has_prompt else out_p_prev
    out_s = outs[-1] if has_sample else None
    return out_p, out_s


def _prep_weights(w_in, w_uq, w_ukv):
    half = MLA_ROPE // 2
    kpe_w = w_in[:, Q_LORA + KV_LORA:MLA_IN]
    kpe_b = jnp.concatenate([-kpe_w[:, half:], kpe_w[:, :half]], axis=1)
    w1 = jnp.concatenate([w_in[:, :Q_LORA + KV_LORA], kpe_w, kpe_b,
                          jnp.zeros((D_MODEL, LANE - 2 * MLA_ROPE), F32), w_in[:, MLA_IN:]], axis=1).astype(BF16)
    pad_q = jnp.zeros((Q_LORA, MLA_HEADS, HEAD_PAD - MLA_NOPE - MLA_ROPE), F32)
    wqa = jnp.concatenate([w_uq, pad_q], axis=2).reshape(Q_LORA, -1).astype(BF16)
    w_uk, w_uv = w_ukv[:, :, :MLA_NOPE], w_ukv[:, :, MLA_NOPE:]
    wk = jnp.concatenate([w_uk, jnp.zeros((KV_LORA, MLA_HEADS, HEAD_PAD - MLA_NOPE), F32)], axis=2)
    wk = wk.reshape(KV_LORA, -1).astype(BF16)
    pk_np = np.zeros((MLA_ROPE, MLA_HEADS * HEAD_PAD), np.float32)
    for h in range(MLA_HEADS):
        for i in range(MLA_ROPE):
            pk_np[i, h * HEAD_PAD + MLA_NOPE + i] = 1.0
    pk = jnp.asarray(pk_np).astype(BF16)
    zv = jnp.zeros((KV_LORA, MLA_HEADS // 2, MLA_V), F32)
    wv4 = w_uv.reshape(KV_LORA, MLA_HEADS // 2, 2, MLA_V)
    wv = jnp.stack([jnp.concatenate([wv4[:, :, 0], zv], axis=2),
                    jnp.concatenate([zv, wv4[:, :, 1]], axis=2)], axis=2)
    wv = wv.reshape(KV_LORA, -1).astype(BF16)
    wuk = jnp.transpose(w_uk, (1, 2, 0)).astype(BF16)
    wuv_np = np.zeros((MLA_HEADS, MLA_HEADS * MLA_V), np.float32)
    for h in range(MLA_HEADS):
        wuv_np[h, h * MLA_V:(h + 1) * MLA_V] = 1.0
    wuv = jnp.transpose(w_uv, (1, 0, 2))
    wuv = (jnp.tile(wuv, (1, 1, MLA_HEADS)) * jnp.asarray(wuv_np)[:, None, :]).astype(BF16)
    return w1, wqa, wk, pk, wv, wuk, wuv


def _rope_table(pos):
    inv = ROPE_BASE ** (-jnp.arange(0, MLA_ROPE, 2, dtype=F32) / MLA_ROPE)
    ang = pos.astype(F32)[:, None] * inv[None, :]
    return jnp.concatenate([jnp.cos(ang), jnp.sin(ang)], axis=1)


def _rope_placement():
    half = MLA_ROPE // 2
    place = np.zeros((MLA_ROPE, 4 * LANE), np.float32)
    rows = np.zeros((2, 4 * LANE), np.float32)
    for i in range(half):
        c, s = i, half + i
        place[c, MLA_NOPE + i] = place[c, MLA_NOPE + half + i] = 1.0
        place[s, LANE + MLA_NOPE + i] = -1.0
        place[s, 2 * LANE + MLA_NOPE + half + i] = 1.0
        place[c, 3 * LANE + i] = place[c, 3 * LANE + half + i] = 1.0
        place[s, 3 * LANE + 2 * half + i] = place[s, 3 * LANE + 3 * half + i] = 1.0
    rows[0, :MLA_NOPE] = 1.0
    rows[1, :3 * LANE] = MLA_SCALE * LOG2E
    rows[1, 3 * LANE:] = 1.0
    return jnp.asarray(place).astype(BF16), jnp.asarray(rows)


def _seg_ones():
    idx = np.arange(RWKV_DIM) // RWKV_N
    return jnp.asarray((idx[:, None] == idx[None, :]).astype(np.float32)).astype(BF16)


def _dispatch(route, tile_counts, t_total):
    A = t_total * TOP_K
    n_tiles = tile_counts.shape[0]
    counts_te = tile_counts[:, 0, N_GROUPS:N_GROUPS + N_EXPERTS].astype(jnp.int32)
    counts = jnp.sum(counts_te, axis=0)
    blocks_per_e = (counts + MOE_BLK - 1) // MOE_BLK
    blk_end = jnp.cumsum(blocks_per_e)
    blk_start = blk_end - blocks_per_e
    tile_off = jnp.cumsum(counts_te, axis=0) - counts_te
    base = blk_start[None, :] * MOE_BLK + tile_off
    e = route[:, :TOP_K].astype(jnp.int32)
    rank = route[:, 4:4 + TOP_K].astype(jnp.int32)
    base_tok = jnp.repeat(base, t_total // n_tiles, axis=0)
    pick = e[:, :, None] == jnp.arange(N_EXPERTS, dtype=jnp.int32)[None, None, :]
    dest = jnp.sum(jnp.where(pick, base_tok[:, None, :], 0), axis=-1) + rank
    n_blk = -(-A // MOE_BLK) + N_EXPERTS
    blk = jnp.arange(n_blk, dtype=jnp.int32)
    block_e = jnp.minimum(jnp.sum((blk[:, None] >= blk_end[None, :]).astype(jnp.int32), axis=1),
                          N_EXPERTS - 1).astype(jnp.int32)
    n_used = blk_end[-1:].astype(jnp.int32)
    return dest, block_e, n_used


def _sc_gather_rows(table, idx):
    n_rows, width = idx.shape[0], table.shape[1]
    n_workers = SC_CORES * SC_SUBCORES
    per_worker = n_rows // n_workers
    assert n_rows % n_workers == 0 and per_worker % SC_WINDOW == 0
    mesh = plsc.VectorSubcoreMesh(core_axis_name="c", subcore_axis_name="s")

    @functools.partial(
        pl.kernel, mesh=mesh,
        out_type=jax.ShapeDtypeStruct((n_rows, width), table.dtype),
        scratch_types=[
            pltpu.VMEM((SC_WINDOW,), jnp.int32),
            pltpu.VMEM((SC_WINDOW, width), table.dtype),
            pltpu.SemaphoreType.DMA,
        ],
    )
    def gather(table_hbm, idx_hbm, out_hbm, idx_v, rows_v, sem):
        wid = lax.axis_index("s") * SC_CORES + lax.axis_index("c")
        base = wid * per_worker

        @pl.loop(0, per_worker // SC_WINDOW)
        def _(w):
            off = pl.multiple_of(base + w * SC_WINDOW, SC_WINDOW)
            pltpu.sync_copy(idx_hbm.at[pl.ds(off, SC_WINDOW)], idx_v)
            pltpu.async_copy(table_hbm.at[idx_v], rows_v, sem).wait()
            pltpu.sync_copy(rows_v, out_hbm.at[pl.ds(off, SC_WINDOW)])

    return gather(table, idx)


def _sc_scatter_rows(src, idx_a, idx_b, n_slots):
    n_rows, width = src.shape
    n_workers = SC_CORES * SC_SUBCORES
    per_worker = n_rows // n_workers
    assert n_rows % n_workers == 0 and per_worker % SC_WINDOW == 0
    n_win = per_worker // SC_WINDOW
    mesh = plsc.VectorSubcoreMesh(core_axis_name="c", subcore_axis_name="s")

    @functools.partial(
        pl.kernel, mesh=mesh,
        out_type=jax.ShapeDtypeStruct((n_slots, width), src.dtype),
        scratch_types=[
            pltpu.VMEM((1, SC_WINDOW), jnp.int32),
            pltpu.VMEM((1, SC_WINDOW), jnp.int32),
            pltpu.VMEM((SC_WINDOW, width), src.dtype),
        ],
    )
    def scatter(src_hbm, ia_hbm, ib_hbm, out_hbm, ia_v, ib_v, rows_v):
        wid = lax.axis_index("s") * SC_CORES + lax.axis_index("c")

        @pl.loop(0, n_win)
        def _(w):
            win = wid * n_win + w
            off = pl.multiple_of(win * SC_WINDOW, SC_WINDOW)
            pltpu.sync_copy(src_hbm.at[pl.ds(off, SC_WINDOW)], rows_v)
            pltpu.sync_copy(ia_hbm.at[pl.ds(win, 1)], ia_v)
            pltpu.sync_copy(ib_hbm.at[pl.ds(win, 1)], ib_v)
            pltpu.sync_copy(rows_v, out_hbm.at[ia_v.at[0]])
            pltpu.sync_copy(rows_v, out_hbm.at[ib_v.at[0]])

    return scatter(src, idx_a, idx_b)


def kernel(x_prompt, x_sample, cache_ckv, cache_kpe, state_wkv, state_shift, w_in, q_norm_g, kv_norm_g, w_uq,
           w_ukv, mu_shift, w0, w2, a0, a2, g2, k_k, k_a, r_k, lnx_g, lnx_b, w_o, ln1_g, ln1_b, w_gr, b_gr,
           w_er, b_er, w_eg, w_eu, w_ed, ln2_g, ln2_b):
    B, S, D = x_prompt.shape
    DB, DS, _ = x_sample.shape
    past = cache_ckv.shape[2]
    Tp, Ts = B * S, DB * DS
    T = Tp + Ts
    assert D == D_MODEL and DS == SHIFT_GROUP and S % ATTN_TQ == 0 and S % (CHUNK * WKV_SUB) == 0
    assert Tp % TOKEN_TILE == 0 and T % TOKEN_TILE == 0 and w_in.shape[0] == DEPTH

    l = 0
    xp, xs_in = x_prompt.reshape(Tp, D), x_sample.reshape(Ts, D)
    w1, wqa, wk, pk, wv, wuk, wuv = _prep_weights(w_in[l], w_uq[l], w_ukv[l])
    pos = jnp.concatenate([jnp.arange(S, dtype=jnp.int32),
                           jnp.tile(past + jnp.arange(DS, dtype=jnp.int32), TOKEN_TILE // DS)])
    rope = _rope_table(pos)

    seg = _seg_ones()
    vec = lambda a: a.reshape(1, -1)
    rwkv_w = (vec(mu_shift[l]), vec(w0[l]), vec(a0[l]), vec(k_k[l]), vec(k_a[l]), vec(r_k[l]),
              w2[l].astype(BF16), a2[l].astype(BF16), g2[l].astype(BF16), seg)
    (q, kcat, vcat, ckv_p, ckv_s, kpe_p, kpe_s, last_rows,
     r, lw, kh, v, na, b, bonus, g) = _proj_call(
        xp, xs_in, rope, S // TOKEN_TILE, *_rope_placement(), w1, q_norm_g[l][None], kv_norm_g[l][None],
        wqa, wk, pk, wv, state_shift[l][:, None, :], rwkv_w)

    attn_p = _attn_call(q, kcat, vcat, B, S)
    attn_s = _mla_sample_call(q, ckv_s, kpe_s, cache_ckv[l], cache_kpe[l], wuk, wuv, Tp, DB, DS)

    scan_in = (r, lw, kh, v, na, b)
    h0_p = jnp.zeros((B, RWKV_HEADS, RWKV_N, RWKV_N), F32)
    y_p, hT_p = _wkv_call(scan_in, h0_p, 0, B, S // CHUNK, CHUNK, WKV_SUB, math.gcd(B, WKV_PAR))
    h0_s = jnp.swapaxes(state_wkv[l], -1, -2)
    y_s, hT_s = _wkv_call(scan_in, h0_s, Tp, DB, 1, DS, 1, math.gcd(DB, WKV_SUB * WKV_PAR))

    wo_b = w_o[l].astype(BF16)
    wr = jnp.concatenate([w_gr[l], w_er[l], jnp.zeros((D, LANE - N_GROUPS - N_EXPERTS), F32)], axis=1)
    wr_hi = wr.astype(BF16)
    wr_lo = (wr - wr_hi.astype(F32)).astype(BF16)
    br = jnp.concatenate([b_gr[l], b_er[l], jnp.zeros((LANE - N_GROUPS - N_EXPERTS,), F32)])[None]
    n_tiles = T // TOKEN_TILE
    wave_tiles = [(t0, min(t0 + MOE_WAVE_TILES, n_tiles) - t0) for t0 in range(0, n_tiles, MOE_WAVE_TILES)]
    out_p, out_s = None, None
    for tile0, nt in wave_tiles:
        t_w = nt * TOKEN_TILE
        h, hpk, route, tile_counts = _mix_call(
            xp, xs_in, attn_p, attn_s, y_p, y_s, bonus, g, seg, vec(lnx_g[l]), vec(lnx_b[l]),
            wo_b[:MLA_HEADS * MLA_V], wo_b[MLA_HEADS * MLA_V:], vec(ln1_g[l]), vec(ln1_b[l]),
            jnp.stack([wr_hi, wr_lo]), br, tile0, nt)
        dest, block_e, n_used = _dispatch(route, tile_counts, t_w)
        win = lambda a: a.reshape(t_w // SC_WINDOW, SC_WINDOW)
        xs = _sc_scatter_rows(hpk, win(dest[:, 0]), win(dest[:, 1]), block_e.shape[0] * MOE_BLK)
        ys = _expert_call(block_e, n_used, xs, w_eg[l], w_eu[l], w_ed[l])
        yab = _sc_gather_rows(ys, jnp.concatenate([dest[:, 0], dest[:, 1]]))
        out_p, wave_s = _combine_call(h, yab, route, vec(ln2_g[l]), vec(ln2_b[l]), Tp, Ts, tile0, out_p)
        assert wave_s is None or out_s is None, "the sample tiles must fall inside one wave"
        out_s = wave_s if wave_s is not None else out_s

    y_prompt = out_p.reshape(B, S, D)
    y_sample = out_s.reshape(DB, DS, D)
    p_ckv = ckv_p.reshape(1, B, S, KV_LORA)
    p_kpe = kpe_p.reshape(1, B, S, MLA_ROPE)
    s_ckv = ckv_s.reshape(1, DB, DS, KV_LORA)
    s_kpe = kpe_s.reshape(1, DB, DS, MLA_ROPE)
    p_wkv = jnp.swapaxes(hT_p, -1, -2)[None]
    s_wkv = jnp.swapaxes(hT_s, -1, -2)[None]
    gp = S // SHIFT_GROUP
    p_sh = last_rows[gp - 1:B * gp:gp][None]
    s_sh = last_rows[B * gp:][None]
    return (y_prompt, y_sample, p_ckv, p_kpe, p_wkv, p_sh, s_ckv, s_kpe, s_wkv, s_sh)
```

```python
import functools
import math

import numpy as np
import jax
import jax.numpy as jnp
from jax import lax
from jax.experimental import pallas as pl
from jax.experimental.pallas import tpu as pltpu
from jax.experimental.pallas import tpu_sc as plsc

F32 = jnp.float32
BF16 = jnp.bfloat16

D_MODEL = 1024
CHUNK = 64
MLA_HEADS = 8
MLA_NOPE = 64
MLA_ROPE = 32
MLA_V = 64
Q_LORA = 384
KV_LORA = 256
ROPE_BASE = 10000.0
MLA_IN = Q_LORA + KV_LORA + MLA_ROPE
MLA_SCALE = (MLA_NOPE + MLA_ROPE) ** -0.5
RWKV_HEADS = 8
RWKV_N = 64
RWKV_DIM = RWKV_HEADS * RWKV_N
DECAY_LORA = 64
AAA_LORA = 64
GATE_LORA = 128
RWKV_IN = 3 * RWKV_DIM + DECAY_LORA + AAA_LORA + GATE_LORA
N_GROUPS = 4
EXPERTS_PER_GROUP = 8
N_EXPERTS = N_GROUPS * EXPERTS_PER_GROUP
TOP_K = 2
D_EXPERT = 256
MOE_BLK = 512
LN_EPS = 1e-5
RMS_EPS = 1e-6
GN_EPS = 64e-5
NEG_INF = -1e30
DEPTH = 1
DN_ALPHA = (2 * DEPTH) ** 0.25

LANE = 128
HEAD_PAD = 128
PROJ_KPE = Q_LORA + KV_LORA
PROJ_PR = PROJ_KPE + LANE
PROJ_W = PROJ_PR + RWKV_IN
SHIFT_GROUP = 32
TOKEN_TILE = 512
ATTN_TQ = 1024
ATTN_TK = 512
V_ONE_LANE = (MLA_V, 0)
LOG2E = math.log2(math.e)
VMEM_LIMIT = 48 * 1024 * 1024
PROJ_VMEM_LIMIT = 56 * 1024 * 1024
SC_CORES = 2
SC_SUBCORES = 16
SC_WINDOW = 32
WKV_GROUP = 4
WKV_SUB = 2
WKV_PAR = 2
MIX_PARTS = 2
MOE_WAVE_TILES = 34


def _cparams(sem):
    return pltpu.CompilerParams(dimension_semantics=sem, vmem_limit_bytes=VMEM_LIMIT)


def _split3(x):
    hi = x.astype(BF16)
    r1 = x - hi.astype(F32)
    mid = r1.astype(BF16)
    lo = (r1 - mid.astype(F32)).astype(BF16)
    return hi, mid, lo


def _dot(a, b):
    return jnp.dot(a, b, preferred_element_type=F32)


def _dot_nt(a, b):
    return lax.dot_general(a, b, (((1,), (1,)), ((), ())), preferred_element_type=F32)


def _dot_exact_rhs(x, w):
    hi = x.astype(BF16)
    lo = (x - hi.astype(F32)).astype(BF16)
    return _dot(hi, w) + _dot(lo, w)


def _dot_exact_lhs(w, x):
    hi, mid, lo = _split3(x)
    return _dot(w, hi) + _dot(w, mid) + _dot(w, lo)


def _pack_bf16_pairs(x):
    n = x.shape[1] // 2
    bits = pltpu.bitcast(x.astype(BF16).astype(F32), jnp.int32)
    return (bits[:, :n] & jnp.int32(-65536)) | lax.shift_right_logical(bits[:, n:], jnp.int32(16))


def _unpack_bf16_pairs(p):
    hi = pltpu.bitcast(p & jnp.int32(-65536), F32)
    lo = pltpu.bitcast(lax.shift_left(p, jnp.int32(16)), F32)
    return jnp.concatenate([hi, lo], axis=1)


def _sigmoid(x):
    return 1.0 / (1.0 + jnp.exp(-x))


def _layer_norm(x, g, b):
    xc = x - jnp.mean(x, -1, keepdims=True)
    var = jnp.mean(xc * xc, -1, keepdims=True)
    return xc * lax.rsqrt(var + LN_EPS) * g + b


def _proj_kernel(xp_ref, xs_ref, rope_ref, rp_ref, rc_ref, w1_ref, gq_ref, gkv_ref, wqa_ref, wk_ref, pk_ref, wv_ref,
                 shift_ref, *rest, n_p, tm, seq_tiles):
    rwkv_w = rest[:10]
    q_ref, k_ref, v_ref, ckvp_ref, ckvs_ref, kpep_ref, kpes_ref, last_ref = rest[10:18]
    rwkv_out = rest[18:26]
    carry_scr = rest[26]
    i = pl.program_id(0)
    is_p = i < n_p
    x = jnp.where(is_p, xp_ref[...], xs_ref[...]).astype(BF16)
    proj = _dot(x, w1_ref[...])
    c_q = proj[:, :Q_LORA]
    c_kv = proj[:, Q_LORA:Q_LORA + KV_LORA]
    kp = proj[:, PROJ_KPE:PROJ_PR]
    pr = proj[:, PROJ_PR:]

    ng = tm // SHIFT_GROUP
    pr3 = pr.reshape(ng, SHIFT_GROUP, RWKV_IN)
    last_ref[...] = pr3[:, SHIFT_GROUP - 1, :]
    rolled = pltpu.roll(pr, 1, 0).reshape(ng, SHIFT_GROUP, RWKV_IN)
    first_row = jnp.where(i % seq_tiles == 0, 0.0, carry_scr[...])
    bound = jnp.where(is_p, first_row[None], shift_ref[...])
    row_in_grp = lax.broadcasted_iota(jnp.int32, pr3.shape, 1)
    grp = lax.broadcasted_iota(jnp.int32, pr3.shape, 0)
    use_bound = (row_in_grp == 0) & (grp <= jnp.where(is_p, 0, ng))
    prev = jnp.where(use_bound, bound, rolled).reshape(tm, RWKV_IN)
    carry_scr[...] = pr[tm - 1:tm, :]

    cqn = c_q * lax.rsqrt(jnp.mean(c_q * c_q, -1, keepdims=True) + RMS_EPS) * gq_ref[...]
    ckv = c_kv * lax.rsqrt(jnp.mean(c_kv * c_kv, -1, keepdims=True) + RMS_EPS) * gkv_ref[...]

    rope = (_dot_exact_rhs(rope_ref[...], rp_ref[...]) + rc_ref[0:1, :]) * rc_ref[1:2, :]
    cq = rope[:, :LANE]
    sq_up = rope[:, LANE:2 * LANE]
    sq_dn = rope[:, 2 * LANE:3 * LANE]
    kt = rope[:, 3 * LANE:]
    prod = kp * kt
    kpe = prod[:, :MLA_ROPE] + prod[:, MLA_ROPE:2 * MLA_ROPE]

    qa = _dot(cqn.astype(BF16), wqa_ref[...])
    half = MLA_ROPE // 2
    for h in range(MLA_HEADS):
        qh = qa[:, h * HEAD_PAD:(h + 1) * HEAD_PAD]
        rot = pltpu.roll(qh, HEAD_PAD - half, 1) * sq_up + pltpu.roll(qh, half, 1) * sq_dn
        q_ref[:, h * HEAD_PAD:(h + 1) * HEAD_PAD] = (qh * cq + rot).astype(BF16)

    ckv_b = ckv.astype(BF16)
    k = _dot(ckv_b, wk_ref[...]) + _dot(kpe.astype(BF16), pk_ref[...])
    k_ref[...] = k.astype(BF16)
    lane = lax.broadcasted_iota(jnp.int32, (1, MLA_HEADS * HEAD_PAD), 1)
    odd = (lane // HEAD_PAD) % 2
    one_lane = jnp.where(odd == 1, V_ONE_LANE[1], V_ONE_LANE[0])
    v_one = jnp.where(lane % HEAD_PAD == one_lane, 1.0, 0.0)
    v_ref[...] = (_dot(ckv_b, wv_ref[...]) + v_one).astype(BF16)
    _rwkv_heads(pr, prev, *rwkv_w, *rwkv_out)

    @pl.when(is_p)
    def _():
        ckvp_ref[...] = ckv
        eye = jnp.where(lax.broadcasted_iota(jnp.int32, (MLA_ROPE, MLA_ROPE), 0)
                        == lax.broadcasted_iota(jnp.int32, (MLA_ROPE, MLA_ROPE), 1), 1.0, 0.0).astype(BF16)
        kpep_ref[...] = sum(_dot_nt(eye, part) for part in _split3(kpe))

    @pl.when(jnp.logical_not(is_p))
    def _():
        ckvs_ref[...] = ckv
        kpes_ref[...] = kpe


def _split_rows(n_p, tile0=0):
    return ((lambda i: (jnp.minimum(i + tile0, n_p - 1), 0)),
            (lambda i: (jnp.maximum(i + tile0 - n_p, 0), 0)))


def _proj_call(xp, xs, rope, seq_tiles, rope_place, rope_rows, w1, gq, gkv, wqa, wk, pk, wv, shift, rwkv_w):
    Tp, Ts = xp.shape[0], xs.shape[0]
    T = Tp + Ts
    tm = TOKEN_TILE
    n_p = Tp // tm
    rope_tiles = seq_tiles
    row = lambda i: (i, 0)
    full = lambda i: (0, 0)
    row_p, row_s = _split_rows(n_p)
    wide = MLA_HEADS * HEAD_PAD
    ng = tm // SHIFT_GROUP
    once = pl.Buffered(1)
    vec = pl.BlockSpec((1, RWKV_DIM), full, pipeline_mode=once)
    rwkv_specs = [pl.BlockSpec((1, RWKV_IN), full, pipeline_mode=once), vec, vec, vec, vec, vec,
                  pl.BlockSpec((DECAY_LORA, RWKV_DIM), full, pipeline_mode=once),
                  pl.BlockSpec((AAA_LORA, RWKV_DIM), full, pipeline_mode=once),
                  pl.BlockSpec((GATE_LORA, RWKV_DIM), full, pipeline_mode=once),
                  pl.BlockSpec((RWKV_DIM, RWKV_DIM), full, pipeline_mode=once)]
    tok = pl.BlockSpec((tm, RWKV_DIM), row)
    return pl.pallas_call(
        functools.partial(_proj_kernel, n_p=n_p, tm=tm, seq_tiles=seq_tiles),
        grid=(T // tm,),
        in_specs=[
            pl.BlockSpec((tm, D_MODEL), row_p),
            pl.BlockSpec((tm, D_MODEL), row_s),
            pl.BlockSpec((tm, MLA_ROPE), lambda i: (jnp.where(i < n_p, i % rope_tiles, rope_tiles), 0)),
            pl.BlockSpec((MLA_ROPE, 4 * LANE), full, pipeline_mode=once),
            pl.BlockSpec((2, 4 * LANE), full, pipeline_mode=once),
            pl.BlockSpec((D_MODEL, PROJ_W), full, pipeline_mode=once),
            pl.BlockSpec((1, Q_LORA), full, pipeline_mode=once),
            pl.BlockSpec((1, KV_LORA), full, pipeline_mode=once),
            pl.BlockSpec((Q_LORA, wide), full, pipeline_mode=once),
            pl.BlockSpec((KV_LORA, wide), full, pipeline_mode=once),
            pl.BlockSpec((MLA_ROPE, wide), full, pipeline_mode=once),
            pl.BlockSpec((KV_LORA, wide), full, pipeline_mode=once),
            pl.BlockSpec((ng, 1, RWKV_IN), lambda i: (jnp.maximum(i - n_p, 0), 0, 0)),
        ] + rwkv_specs,
        out_specs=[
            pl.BlockSpec((tm, wide), row),
            pl.BlockSpec((tm, wide), row),
            pl.BlockSpec((tm, wide), row),
            pl.BlockSpec((tm, KV_LORA), row_p),
            pl.BlockSpec((tm, KV_LORA), row_s),
            pl.BlockSpec((MLA_ROPE, tm), lambda i: (0, jnp.minimum(i, n_p - 1))),
            pl.BlockSpec((tm, MLA_ROPE), row_s),
            pl.BlockSpec((ng, RWKV_IN), row),
        ] + [tok] * 8,
        out_shape=[
            jax.ShapeDtypeStruct((T, wide), BF16),
            jax.ShapeDtypeStruct((T, wide), BF16),
            jax.ShapeDtypeStruct((T, wide), BF16),
            jax.ShapeDtypeStruct((Tp, KV_LORA), F32),
            jax.ShapeDtypeStruct((Ts, KV_LORA), F32),
            jax.ShapeDtypeStruct((MLA_ROPE, Tp), F32),
            jax.ShapeDtypeStruct((Ts, MLA_ROPE), F32),
            jax.ShapeDtypeStruct((T // SHIFT_GROUP, RWKV_IN), F32),
        ] + [jax.ShapeDtypeStruct((T, RWKV_DIM), F32)] * 8,
        scratch_shapes=[pltpu.VMEM((1, RWKV_IN), F32)],
        compiler_params=pltpu.CompilerParams(dimension_semantics=("arbitrary",), vmem_limit_bytes=PROJ_VMEM_LIMIT),
        name="proj",
    )(xp, xs, rope, rope_place, rope_rows, w1, gq, gkv, wqa, wk, pk, wv, shift, *rwkv_w)


def _attn_kernel(q_ref, k_ref, v_ref, o_ref, m_scr, acc_scr, *, tq, tk):
    qi = pl.program_id(2)
    m_scr[...] = jnp.full(m_scr.shape, NEG_INF, F32)
    acc_scr[...] = jnp.zeros(acc_scr.shape, F32)
    n_diag = tq // tk

    H = range(2)
    sls = [slice(h * HEAD_PAD, (h + 1) * HEAD_PAD) for h in H]

    def scores(k0, width, rows):
        return [_dot_nt(q_ref[rows, sl], k_ref[pl.ds(k0, width), sl]) for sl in sls]

    def accumulate(s, k0, width, rows):
        m_prev = [m_scr[h, rows, :] for h in H]
        m_new = [jnp.maximum(m_prev[h], jnp.max(s[h], axis=1, keepdims=True)) for h in H]
        pexp = [jnp.exp2(s[h] - jnp.tile(m_new[h], (1, width // LANE))).astype(BF16) for h in H]
        pv = [_dot(pexp[h], v_ref[pl.ds(k0, width), sls[h]]) for h in H]
        for h in H:
            acc_scr[h, rows, :] = jnp.exp2(m_prev[h] - m_new[h]) * acc_scr[h, rows, :] + pv[h]
            m_scr[h, rows, :] = m_new[h]

    def kv_block(k0, width, rows, masked):
        s = scores(k0, width, rows)
        if masked:
            n_rows = rows.stop - rows.start
            r = lax.broadcasted_iota(jnp.int32, (n_rows, width), 0) // CHUNK
            c = lax.broadcasted_iota(jnp.int32, (n_rows, width), 1) // CHUNK
            s = [jnp.where(c <= r, s[h], NEG_INF) for h in H]
        accumulate(s, k0, width, rows)

    all_rows = slice(0, tq)

    def pair(t, carry):
        k0a = pl.multiple_of(2 * t * tq, tq)
        k0b = pl.multiple_of(k0a + tq, tq)
        s_a = scores(k0a, tq, all_rows)
        s_b = scores(k0b, tq, all_rows)
        accumulate(s_a, k0a, tq, all_rows)
        accumulate(s_b, k0b, tq, all_rows)
        return carry

    lax.fori_loop(0, qi // 2, pair, 0)

    @pl.when(qi % 2 == 1)
    def _():
        kv_block(pl.multiple_of((qi - 1) * tq, tq), tq, all_rows, False)

    for d in range(n_diag):
        k0 = pl.multiple_of(qi * tq + d * tk, tk)
        kv_block(k0, tk, slice(d * tk, (d + 1) * tk), True)
        if (d + 1) * tk < tq:
            kv_block(k0, tk, slice((d + 1) * tk, tq), False)
    acc0, acc1 = acc_scr[0], acc_scr[1]
    lane = lax.broadcasted_iota(jnp.int32, acc0.shape, 1)
    l0 = acc0[:, V_ONE_LANE[0]:V_ONE_LANE[0] + 1]
    l1 = acc1[:, V_ONE_LANE[1]:V_ONE_LANE[1] + 1]
    o_ref[...] = jnp.where(lane < MLA_V, acc0 / l0, acc1 / l1).astype(o_ref.dtype)


def _attn_call(q, k, v, n_batch, seq):
    tq, tk = ATTN_TQ, ATTN_TK
    nq = seq // tq
    hp = MLA_HEADS // 2
    resident = pl.BlockSpec((seq, 2 * HEAD_PAD), lambda b, h, i: (b, h), pipeline_mode=pl.Buffered(1))
    return pl.pallas_call(
        functools.partial(_attn_kernel, tq=tq, tk=tk),
        grid=(n_batch, hp, nq),
        in_specs=[
            pl.BlockSpec((tq, 2 * HEAD_PAD), lambda b, h, i: (b * nq + i, h)),
            resident,
            resident,
        ],
        out_specs=pl.BlockSpec((tq, LANE), lambda b, h, i: (b * nq + i, h)),
        out_shape=jax.ShapeDtypeStruct((n_batch * seq, hp * LANE), BF16),
        scratch_shapes=[
            pltpu.VMEM((2, tq, LANE), F32),
            pltpu.VMEM((2, tq, LANE), F32),
        ],
        compiler_params=_cparams(("parallel", "parallel", "arbitrary")),
        name="attn",
    )(q, k, v)


def _mla_sample_kernel(q_ref, cn_ref, kn_ref, cp_ref, kp_ref, wuk_ref, wuv_ref, o_ref, *, past, dec):
    cp = cp_ref[0].astype(BF16)
    kp_t = kp_ref[0].astype(BF16)
    cn = cn_ref[...].astype(BF16)
    kn = kn_ref[...].astype(BF16)
    R = MLA_HEADS * dec
    qrow = (past + lax.broadcasted_iota(jnp.int32, (R, past), 0) % dec) // CHUNK
    vis_p = (lax.broadcasted_iota(jnp.int32, (R, past), 1) // CHUNK) <= qrow
    qrow_n = (past + lax.broadcasted_iota(jnp.int32, (R, dec), 0) % dec) // CHUNK
    vis_n = ((past + lax.broadcasted_iota(jnp.int32, (R, dec), 1)) // CHUNK) <= qrow_n
    q_lat, qp = [], []
    for h in range(MLA_HEADS):
        qn = q_ref[:, h * HEAD_PAD:h * HEAD_PAD + MLA_NOPE]
        qp.append(q_ref[:, h * HEAD_PAD + MLA_NOPE:h * HEAD_PAD + MLA_NOPE + MLA_ROPE])
        q_lat.append(_dot(qn, wuk_ref[h]).astype(BF16))
    q_lat = jnp.concatenate(q_lat, axis=0)
    qp = jnp.concatenate(qp, axis=0)
    s_p = jnp.where(vis_p, _dot_nt(q_lat, cp) + _dot(qp, kp_t), NEG_INF)
    s_n = jnp.where(vis_n, _dot_nt(q_lat, cn) + _dot_nt(qp, kn), NEG_INF)
    m = jnp.maximum(jnp.max(s_p, axis=1, keepdims=True), jnp.max(s_n, axis=1, keepdims=True))
    e_p = jnp.exp2(s_p - m)
    e_n = jnp.exp2(s_n - m)
    l = jnp.sum(e_p, axis=1, keepdims=True) + jnp.sum(e_n, axis=1, keepdims=True)
    o_lat = ((_dot(e_p.astype(BF16), cp) + _dot(e_n.astype(BF16), cn)) / l).astype(BF16)
    out = jnp.zeros((dec, MLA_HEADS * MLA_V), F32)
    for h in range(MLA_HEADS):
        out = out + _dot(o_lat[h * dec:(h + 1) * dec], wuv_ref[h])
    o_ref[...] = out.astype(o_ref.dtype)


def _mla_sample_call(q, ckv, kpe, cache_ckv, cache_kpe, wuk, wuv, row0, n_seq, dec):
    past = cache_ckv.shape[1]
    blk0 = row0 // dec
    wide = MLA_HEADS * HEAD_PAD
    return pl.pallas_call(
        functools.partial(_mla_sample_kernel, past=past, dec=dec),
        grid=(n_seq,),
        in_specs=[
            pl.BlockSpec((dec, wide), lambda b: (blk0 + b, 0)),
            pl.BlockSpec((dec, KV_LORA), lambda b: (b, 0)),
            pl.BlockSpec((dec, MLA_ROPE), lambda b: (b, 0)),
            pl.BlockSpec((1, past, KV_LORA), lambda b: (b, 0, 0)),
            pl.BlockSpec((1, MLA_ROPE, past), lambda b: (b, 0, 0)),
            pl.BlockSpec((MLA_HEADS, MLA_NOPE, KV_LORA), lambda b: (0, 0, 0)),
            pl.BlockSpec((MLA_HEADS, KV_LORA, MLA_HEADS * MLA_V), lambda b: (0, 0, 0)),
        ],
        out_specs=pl.BlockSpec((dec, MLA_HEADS * MLA_V), lambda b: (b, 0)),
        out_shape=jax.ShapeDtypeStruct((n_seq * dec, MLA_HEADS * MLA_V), BF16),
        compiler_params=_cparams(("parallel",)),
        name="mla_sample",
    )(q, ckv, kpe, cache_ckv, cache_kpe, wuk, wuv)


def _rwkv_heads(pr, prev, mu_ref, w0_ref, a0_ref, kk_ref, ka_ref, rk_ref, w2_ref, a2_ref, g2_ref,
                seg_ref, r_ref, lw_ref, kh_ref, v_ref, na_ref, b_ref, bonus_ref, g_ref):
    u = pr + mu_ref[...] * (prev - pr)
    o1, o2, o3 = RWKV_DIM, 2 * RWKV_DIM, 3 * RWKV_DIM
    o4, o5 = o3 + DECAY_LORA, o3 + DECAY_LORA + AAA_LORA
    r, k, v = u[:, :o1], u[:, o1:o2], u[:, o2:o3]
    w_lo, a_lo, g_lo = u[:, o3:o4], u[:, o4:o5], u[:, o5:]
    wl = w0_ref[...] + _dot(jnp.tanh(w_lo).astype(BF16), w2_ref[...])
    lw_ref[...] = -math.exp(-0.5) * _sigmoid(wl)
    a = _sigmoid(a0_ref[...] + _dot(a_lo.astype(BF16), a2_ref[...]))
    g_ref[...] = _dot(_sigmoid(g_lo).astype(BF16), g2_ref[...])
    seg = seg_ref[...]
    kk = k * kk_ref[...]
    kk = kk / jnp.maximum(jnp.sqrt(_dot_exact_rhs(kk * kk, seg)), 1e-12)
    kh = k * (1.0 + (a - 1.0) * ka_ref[...])
    r_ref[...] = r
    kh_ref[...] = kh
    v_ref[...] = v
    na_ref[...] = -kk
    b_ref[...] = kk * a
    bonus_ref[...] = _dot_exact_rhs(r * kh * rk_ref[...], seg) * v


def _wkv_kernel(*refs, C, n_sub, n_par):
    GW = WKV_GROUP * RWKV_N
    R = WKV_GROUP * C
    n_grp = RWKV_HEADS // WKV_GROUP
    n_lev = int(round(math.log2(C))) - 1
    c = pl.program_id(1)
    tok = [refs[6 * p:6 * p + 6] for p in range(n_par)]
    h0_ref, y_ref, hT_ref, h_scr = refs[6 * n_par:]
    n_state = n_par * n_grp

    def head_block(hh):
        return slice(hh * RWKV_N, (hh + 1) * RWKV_N)

    def transposed(x):
        n = x.shape[0]
        eye = jnp.where(lax.broadcasted_iota(jnp.int32, (n, n), 0) == lax.broadcasted_iota(jnp.int32, (n, n), 1),
                        1.0, 0.0).astype(BF16)
        return sum(_dot_nt(eye, part) for part in _split3(x))

    @pl.when(c == 0)
    def _():
        h_scr[...] = jnp.zeros(h_scr.shape, F32)
        for p in range(n_par):
            for hd in range(RWKV_HEADS):
                g, hh = divmod(hd, WKV_GROUP)
                h_scr[p * n_grp + g, head_block(hh), head_block(hh)] = transposed(h0_ref[p, hd])

    row = lax.broadcasted_iota(jnp.int32, (C, C), 0)
    col = lax.broadcasted_iota(jnp.int32, (C, C), 1)
    tri = jnp.where(col <= row, 1.0, 0.0).astype(BF16)
    rr = lax.broadcasted_iota(jnp.int32, (R, R), 0)
    cc = lax.broadcasted_iota(jnp.int32, (R, R), 1)
    same = (rr // C) == (cc // C)
    ti = lax.broadcasted_iota(jnp.int32, (C, R), 0)
    si = lax.broadcasted_iota(jnp.int32, (C, R), 1) % C
    strict4 = si < ti
    lower4 = si <= ti
    eye4 = jnp.where(si == ti, 1.0, 0.0)
    keep = (lax.broadcasted_iota(jnp.int32, (R, GW), 0) // C
            == lax.broadcasted_iota(jnp.int32, (R, GW), 1) // RWKV_N)
    gr = lax.broadcasted_iota(jnp.int32, (GW, GW), 0)
    gc = lax.broadcasted_iota(jnp.int32, (GW, GW), 1)
    eye_g = gr == gc
    same_head = (gr // RWKV_N) == (gc // RWKV_N)
    eye_g_bf = jnp.where(eye_g, 1.0, 0.0).astype(BF16)

    def rows4(x4):
        return jnp.concatenate([x4] * WKV_GROUP, axis=0)

    def stack(x4):
        return jnp.where(keep, rows4(x4), jnp.zeros((), x4.dtype))


    sls = [slice(g * GW, (g + 1) * GW) for g in range(n_grp)]
    J = [(ci, q) for ci in range(n_sub) for q in range(n_state)]
    ops, p_end = {}, {}
    for p, ci in [(p, ci) for p in range(n_par) for ci in range(n_sub)]:
        r_ref, lw_ref, k_ref, v_ref, a_ref, b_ref = tok[p]
        rows = slice(ci * C, (ci + 1) * C)
        lw = lw_ref[rows, :]
        cum = _dot_exact_lhs(tri, lw)
        cum_end = cum[C - 1:C, :]
        e_neg = jnp.exp(-cum)
        e_end = jnp.exp(cum_end - cum)
        b_in = b_ref[rows, :]
        k_in = k_ref[rows, :]
        full = ((a_ref[rows, :] * jnp.exp(cum - lw)).astype(BF16),
                (r_ref[rows, :] * jnp.exp(cum)).astype(BF16),
                (b_in * e_neg).astype(BF16), (k_in * e_neg).astype(BF16),
                (b_in * e_end).astype(BF16), (k_in * e_end).astype(BF16),
                v_ref[rows, :].astype(BF16))
        p_end[p, ci] = jnp.exp(cum_end)
        for g in range(n_grp):
            ops[ci, p * n_grp + g] = [t[:, sls[g]] for t in full]
    a4, r4, b4, k4, be4, ke4, v4 = [{j: ops[j][i] for j in J} for i in range(7)]
    v_s = {j: stack(v4[j]) for j in J}
    m = {j: _dot_nt(jnp.concatenate([a4[j], r4[j]], axis=0),
                    jnp.concatenate([stack(b4[j]), stack(k4[j])], axis=0)) for j in J}
    l4 = {j: jnp.where(strict4, m[j][:C, :R], 0.0) for j in J}
    a_ak = {j: jnp.where(strict4, m[j][:C, R:], 0.0).astype(BF16) for j in J}
    a_rb = {j: jnp.where(lower4, m[j][C:, :R], 0.0).astype(BF16) for j in J}
    a_rk = {j: jnp.where(lower4, m[j][C:, R:], 0.0).astype(BF16) for j in J}
    def block_diag(x4):
        return jnp.where(same, rows4(x4.astype(BF16)), jnp.zeros((), BF16))

    t4 = {j: eye4 + l4[j] for j in J}
    l_bd = {j: block_diag(l4[j]) for j in J}
    for _ in range(n_lev):
        l4 = {j: _dot(l4[j].astype(BF16), l_bd[j]) for j in J}
        l_bd = {j: block_diag(l4[j]) for j in J}
        t4 = {j: t4[j] + _dot(t4[j].astype(BF16), l_bd[j]) for j in J}
    t_b = {j: t4[j].astype(BF16) for j in J}
    bke_t = {j: _dot_nt(eye_g_bf, jnp.concatenate([be4[j], ke4[j]], axis=0)).astype(BF16) for j in J}

    G = range(n_state)
    h_cur = [h_scr[q] for q in G]
    for ci in range(n_sub):
        rows = slice(ci * C, (ci + 1) * C)
        h0_b = [h_cur[g].astype(BF16) for g in G]
        x4 = [_dot(a4[ci, g], h0_b[g]) + _dot(a_ak[ci, g], v_s[ci, g]) for g in G]
        u4 = [_dot(t_b[ci, g], stack(x4[g].astype(BF16))).astype(BF16) for g in G]
        y4 = [_dot(r4[ci, g], h0_b[g]) + _dot(a_rb[ci, g], stack(u4[g])) + _dot(a_rk[ci, g], v_s[ci, g])
              for g in G]
        uv4 = [jnp.concatenate([u4[g], v4[ci, g]], axis=0) for g in G]
        h_add = [jnp.where(same_head, _dot(bke_t[ci, g], uv4[g]), 0.0) for g in G]
        for q in G:
            p, g = divmod(q, n_grp)
            y_ref[p, rows, sls[g]] = y4[q]
            p_col = jnp.sum(jnp.where(eye_g, p_end[p, ci][:, sls[g]], 0.0), axis=1, keepdims=True)
            h_cur[q] = p_col * h_cur[q] + h_add[q]

    for q in G:
        h_scr[q] = h_cur[q]

    @pl.when(c == pl.num_programs(1) - 1)
    def _():
        for p in range(n_par):
            for hd in range(RWKV_HEADS):
                g, hh = divmod(hd, WKV_GROUP)
                hT_ref[p, hd] = transposed(h_scr[p * n_grp + g, head_block(hh), head_block(hh)])


def _wkv_call(arrs, h0, row0, n_seq, n_chunk, C, n_sub, n_par):
    rows = C * n_sub
    steps = n_chunk // n_sub
    blk0 = row0 // rows
    GW = WKV_GROUP * RWKV_N
    n_grp = RWKV_HEADS // WKV_GROUP
    assert n_seq % n_par == 0 and n_chunk % n_sub == 0
    tok = [pl.BlockSpec((rows, RWKV_DIM), lambda b, c, p=p: (blk0 + (b * n_par + p) * steps + c, 0))
           for p in range(n_par)]
    st = pl.BlockSpec((n_par, RWKV_HEADS, RWKV_N, RWKV_N), lambda b, c: (b, 0, 0, 0))
    y, h_fin = pl.pallas_call(
        functools.partial(_wkv_kernel, C=C, n_sub=n_sub, n_par=n_par),
        grid=(n_seq // n_par, steps),
        in_specs=[tok[p] for p in range(n_par) for _ in range(6)] + [st],
        out_specs=[pl.BlockSpec((n_par, rows, RWKV_DIM), lambda b, c: (b, c, 0)), st],
        out_shape=[
            jax.ShapeDtypeStruct((n_seq, n_chunk * C, RWKV_DIM), F32),
            jax.ShapeDtypeStruct((n_seq, RWKV_HEADS, RWKV_N, RWKV_N), F32),
        ],
        scratch_shapes=[pltpu.VMEM((n_par * n_grp, GW, GW), F32)],
        compiler_params=_cparams(("parallel", "arbitrary")),
        name="wkv_c%d" % C,
    )(*(list(arrs) * n_par), h0)
    return y.reshape(n_seq * n_chunk * C, RWKV_DIM), h_fin


def _mix_kernel(xp_ref, xs_ref, attnp_ref, attns_ref, yp_ref, ys_ref, bonus_ref, g_ref, seg_ref, lng_ref,
                lnb_ref, woa_ref, wob_ref, g1_ref, b1_ref, wr_ref, br_ref, tri_ref, h_ref, hpk_ref, route_ref,
                count_ref, *, n_p, tm):
    is_p = pl.program_id(0) < n_p
    P = range(MIX_PARTS)
    bands = [slice(k * tm // MIX_PARTS, (k + 1) * tm // MIX_PARTS) for k in P]
    seg = seg_ref[...]
    inv_n = 1.0 / RWKV_N
    y = [jnp.where(is_p, yp_ref[r, :], ys_ref[r, :]) for r in bands]
    yc = [y[k] - _dot_exact_rhs(y[k], seg) * inv_n for k in P]
    var = [_dot_exact_rhs(yc[k] * yc[k], seg) * inv_n for k in P]
    yn = [yc[k] * lax.rsqrt(var[k] + GN_EPS) * lng_ref[...] + lnb_ref[...] for k in P]
    rw = [((yn[k] + bonus_ref[bands[k], :]) * g_ref[bands[k], :]).astype(BF16) for k in P]
    attn = [jnp.where(is_p, attnp_ref[r, :], attns_ref[r, :]) for r in bands]
    m = [_dot(attn[k], woa_ref[...]) + _dot(rw[k], wob_ref[...]) for k in P]
    x = [jnp.where(is_p, xp_ref[r, :], xs_ref[r, :]) for r in bands]
    h = [_layer_norm(DN_ALPHA * x[k] + m[k], g1_ref[...], b1_ref[...]) for k in P]
    for k in P:
        h_ref[bands[k], :] = h[k]
        hpk_ref[bands[k], :] = _pack_bf16_pairs(h[k])

    h_hi = [h[k].astype(BF16) for k in P]
    h_lo = [(h[k] - h_hi[k].astype(F32)).astype(BF16) for k in P]
    logits = [_dot(h_hi[k], wr_ref[0]) + _dot(h_lo[k], wr_ref[0]) + _dot(h_hi[k], wr_ref[1]) + br_ref[...]
              for k in P]
    lane = lax.broadcasted_iota(jnp.int32, logits[0].shape, 1)
    big = jnp.int32(LANE)

    def route_band(lg):
        gl = jnp.where(lane < N_GROUPS, lg, NEG_INF)
        gmax = jnp.max(gl, axis=1, keepdims=True)
        grp = jnp.min(jnp.where(gl == gmax, lane, big), axis=1, keepdims=True)
        p_grp = 1.0 / jnp.sum(jnp.exp(gl - gmax), axis=1, keepdims=True)
        e_idx = lane - N_GROUPS
        in_grp = (lane >= N_GROUPS) & (lane < N_GROUPS + N_EXPERTS) & ((e_idx // EXPERTS_PER_GROUP) == grp)
        el = jnp.where(in_grp, lg, NEG_INF)
        m1 = jnp.max(el, axis=1, keepdims=True)
        i1 = jnp.min(jnp.where(el == m1, lane, big), axis=1, keepdims=True)
        el2 = jnp.where(lane == i1, NEG_INF, el)
        m2 = jnp.max(el2, axis=1, keepdims=True)
        i2 = jnp.min(jnp.where(el2 == m2, lane, big), axis=1, keepdims=True)
        t = jnp.exp(m2 - m1)
        g1 = p_grp / (1.0 + t)
        return i1, i2, g1, g1 * t

    routed = [route_band(logits[k]) for k in P]
    chosen = jnp.concatenate(
        [jnp.where(lane == routed[k][0], 1.0, jnp.where(lane == routed[k][1], 1.0, 0.0)) for k in P], axis=0)
    before = _dot(tri_ref[...], chosen.astype(BF16))
    count_ref[0] = jnp.sum(chosen, axis=0, keepdims=True)
    for k in P:
        i1, i2, g1, g2 = routed[k]
        bef = before[bands[k]]
        r1 = jnp.sum(jnp.where(lane == i1, bef, 0.0), axis=1, keepdims=True)
        r2 = jnp.sum(jnp.where(lane == i2, bef, 0.0), axis=1, keepdims=True)
        cols = ((i1 - N_GROUPS).astype(F32), (i2 - N_GROUPS).astype(F32), g1, g2, r1, r2)
        route = jnp.zeros(lane.shape, F32)
        for j, col in enumerate(cols):
            route = jnp.where(lane == j, col, route)
        route_ref[bands[k], :] = route


def _mix_call(xp, xs, attn_p, attn_s, y_p, y_s, bonus, g, seg, lnx_g, lnx_b, woa, wob, ln1_g, ln1_b, wr, br,
              tile0, n_tiles):
    tm = TOKEN_TILE
    T = n_tiles * tm
    n_p = xp.shape[0] // tm
    row = lambda i: (i, 0)
    full = lambda i: (0, 0)
    row_p, row_s = _split_rows(n_p, tile0)
    half = pl.BlockSpec((tm, RWKV_DIM), lambda i: (i + tile0, 0))
    vec5 = pl.BlockSpec((1, RWKV_DIM), full)
    vec10 = pl.BlockSpec((1, D_MODEL), full)
    idx = np.arange(tm)
    tri = jnp.asarray((idx[None, :] < idx[:, None]).astype(np.float32)).astype(BF16)
    return pl.pallas_call(
        functools.partial(_mix_kernel, n_p=n_p - tile0, tm=tm),
        grid=(n_tiles,),
        in_specs=[
            pl.BlockSpec((tm, D_MODEL), row_p), pl.BlockSpec((tm, D_MODEL), row_s),
            pl.BlockSpec((tm, RWKV_DIM), row_p), pl.BlockSpec((tm, RWKV_DIM), row_s),
            pl.BlockSpec((tm, RWKV_DIM), row_p), pl.BlockSpec((tm, RWKV_DIM), row_s),
            half, half,
            pl.BlockSpec((RWKV_DIM, RWKV_DIM), full), vec5, vec5,
            pl.BlockSpec((RWKV_DIM, D_MODEL), full), pl.BlockSpec((RWKV_DIM, D_MODEL), full),
            vec10, vec10,
            pl.BlockSpec((2, D_MODEL, LANE), lambda i: (0, 0, 0)), pl.BlockSpec((1, LANE), full),
            pl.BlockSpec((tm, tm), full),
        ],
        out_specs=[pl.BlockSpec((tm, D_MODEL), row), pl.BlockSpec((tm, D_MODEL // 2), row),
                   pl.BlockSpec((tm, LANE), row), pl.BlockSpec((1, 1, LANE), lambda i: (i, 0, 0))],
        out_shape=[jax.ShapeDtypeStruct((T, D_MODEL), F32), jax.ShapeDtypeStruct((T, D_MODEL // 2), jnp.int32),
                   jax.ShapeDtypeStruct((T, LANE), F32), jax.ShapeDtypeStruct((T // tm, 1, LANE), F32)],
        compiler_params=_cparams(("parallel",)),
        name="mix",
    )(xp, xs, attn_p, attn_s, y_p, y_s, bonus, g, seg, lnx_g, lnx_b, woa, wob, ln1_g, ln1_b, wr, br, tri)


def _expert_kernel(be_ref, nu_ref, xs_ref, wg_ref, wu_ref, wd_ref, ys_ref, wgu_b, wd_b):
    i = pl.program_id(0)

    @pl.when((i == 0) | (be_ref[i] != be_ref[jnp.maximum(i - 1, 0)]))
    def _():
        wgu_b[:, :D_EXPERT] = wg_ref[0].astype(BF16)
        wgu_b[:, D_EXPERT:] = wu_ref[0].astype(BF16)
        wd_b[...] = wd_ref[0].astype(BF16)

    @pl.when(i < nu_ref[0])
    def _():
        P = range(2)
        bands = [slice(k * MOE_BLK // 2, (k + 1) * MOE_BLK // 2) for k in P]
        xb = [_unpack_bf16_pairs(xs_ref[r, :]).astype(BF16) for r in bands]
        gu = [_dot(xb[k], wgu_b[...]) for k in P]
        act = [(gu[k][:, :D_EXPERT] * _sigmoid(gu[k][:, :D_EXPERT]) * gu[k][:, D_EXPERT:]).astype(BF16)
               for k in P]
        out = [_dot(act[k], wd_b[...]) for k in P]
        for k in P:
            ys_ref[bands[k], :] = _pack_bf16_pairs(out[k])

    @pl.when(i >= nu_ref[0])
    def _():
        ys_ref[...] = jnp.zeros(ys_ref.shape, ys_ref.dtype)


def _expert_call(block_e, n_used, xs, wg, wu, wd):
    n_blk = xs.shape[0] // MOE_BLK
    grid_spec = pltpu.PrefetchScalarGridSpec(
        num_scalar_prefetch=2,
        grid=(n_blk,),
        in_specs=[
            pl.BlockSpec((MOE_BLK, D_MODEL // 2), lambda i, be, nu: (i, 0)),
            pl.BlockSpec((1, D_MODEL, D_EXPERT), lambda i, be, nu: (be[i], 0, 0)),
            pl.BlockSpec((1, D_MODEL, D_EXPERT), lambda i, be, nu: (be[i], 0, 0)),
            pl.BlockSpec((1, D_EXPERT, D_MODEL), lambda i, be, nu: (be[i], 0, 0)),
        ],
        out_specs=pl.BlockSpec((MOE_BLK, D_MODEL // 2), lambda i, be, nu: (i, 0)),
        scratch_shapes=[pltpu.VMEM((D_MODEL, 2 * D_EXPERT), BF16), pltpu.VMEM((D_EXPERT, D_MODEL), BF16)],
    )
    return pl.pallas_call(
        _expert_kernel,
        grid_spec=grid_spec,
        out_shape=jax.ShapeDtypeStruct((n_blk * MOE_BLK, D_MODEL // 2), jnp.int32),
        compiler_params=_cparams(("arbitrary",)),
        name="experts",
    )(block_e, n_used, xs, wg, wu, wd)


def _combine_kernel(h_ref, ya_ref, yb_ref, route_ref, g2_ref, b2_ref, *rest, n_p, has_prev, has_prompt,
                    has_sample):
    outs = list(rest[1:] if has_prev else rest)
    i = pl.program_id(0)
    route = route_ref[...]
    f = _unpack_bf16_pairs(ya_ref[...]) * route[:, 2:3] + _unpack_bf16_pairs(yb_ref[...]) * route[:, 3:4]
    out = _layer_norm(DN_ALPHA * h_ref[...] + f, g2_ref[...], b2_ref[...])

    if has_prompt:
        op_ref = outs.pop(0)

        @pl.when(i < n_p)
        def _():
            op_ref[...] = out

    if has_sample:
        os_ref = outs.pop(0)

        @pl.when(i >= n_p)
        def _():
            os_ref[...] = out


def _combine_call(h, yab, route, ln2_g, ln2_b, t_prompt, t_sample, tile0, out_p_prev):
    tm = TOKEN_TILE
    n_t, n_p = h.shape[0] // tm, t_prompt // tm
    has_prompt = tile0 < n_p
    has_sample = tile0 + n_t > n_p
    has_prev = has_prompt and out_p_prev is not None
    row = lambda i: (i, 0)
    full = lambda i: (0, 0)
    row_p, row_s = _split_rows(n_p, tile0)
    big = pl.BlockSpec((tm, D_MODEL), row)
    in_specs = [big, pl.BlockSpec((tm, D_MODEL // 2), row),
                pl.BlockSpec((tm, D_MODEL // 2), lambda i: (i + n_t, 0)), pl.BlockSpec((tm, LANE), row),
                pl.BlockSpec((1, D_MODEL), full), pl.BlockSpec((1, D_MODEL), full)]
    args = [h, yab, yab, route, ln2_g, ln2_b]
    out_specs, out_shape, aliases = [], [], {}
    if has_prompt:
        out_specs.append(pl.BlockSpec((tm, D_MODEL), row_p))
        out_shape.append(jax.ShapeDtypeStruct((t_prompt, D_MODEL), F32))
    if has_prev:
        in_specs.append(pl.BlockSpec(memory_space=pl.ANY))
        args.append(out_p_prev)
        aliases = {len(args) - 1: 0}
    if has_sample:
        out_specs.append(pl.BlockSpec((tm, D_MODEL), row_s))
        out_shape.append(jax.ShapeDtypeStruct((t_sample, D_MODEL), F32))
    outs = pl.pallas_call(
        functools.partial(_combine_kernel, n_p=n_p - tile0, has_prev=has_prev, has_prompt=has_prompt,
                          has_sample=has_sample),
        grid=(n_t,),
        in_specs=in_specs,
        out_specs=out_specs,
        out_shape=out_shape,
        input_output_aliases=aliases,
        compiler_params=_cparams(("arbitrary",)),
        name="combine",
    )(*args)
    out_p = outs[0] if has_prompt else out_p_prev
    out_s = outs[-1] if has_sample else None
    return out_p, out_s


def _prep_weights(w_in, w_uq, w_ukv):
    half = MLA_ROPE // 2
    kpe_w = w_in[:, Q_LORA + KV_LORA:MLA_IN]
    kpe_b = jnp.concatenate([-kpe_w[:, half:], kpe_w[:, :half]], axis=1)
    w1 = jnp.concatenate([w_in[:, :Q_LORA + KV_LORA], kpe_w, kpe_b,
                          jnp.zeros((D_MODEL, LANE - 2 * MLA_ROPE), F32), w_in[:, MLA_IN:]], axis=1).astype(BF16)
    pad_q = jnp.zeros((Q_LORA, MLA_HEADS, HEAD_PAD - MLA_NOPE - MLA_ROPE), F32)
    wqa = jnp.concatenate([w_uq, pad_q], axis=2).reshape(Q_LORA, -1).astype(BF16)
    w_uk, w_uv = w_ukv[:, :, :MLA_NOPE], w_ukv[:, :, MLA_NOPE:]
    wk = jnp.concatenate([w_uk, jnp.zeros((KV_LORA, MLA_HEADS, HEAD_PAD - MLA_NOPE), F32)], axis=2)
    wk = wk.reshape(KV_LORA, -1).astype(BF16)
    pk_np = np.zeros((MLA_ROPE, MLA_HEADS * HEAD_PAD), np.float32)
    for h in range(MLA_HEADS):
        for i in range(MLA_ROPE):
            pk_np[i, h * HEAD_PAD + MLA_NOPE + i] = 1.0
    pk = jnp.asarray(pk_np).astype(BF16)
    zv = jnp.zeros((KV_LORA, MLA_HEADS // 2, MLA_V), F32)
    wv4 = w_uv.reshape(KV_LORA, MLA_HEADS // 2, 2, MLA_V)
    wv = jnp.stack([jnp.concatenate([wv4[:, :, 0], zv], axis=2),
                    jnp.concatenate([zv, wv4[:, :, 1]], axis=2)], axis=2)
    wv = wv.reshape(KV_LORA, -1).astype(BF16)
    wuk = jnp.transpose(w_uk, (1, 2, 0)).astype(BF16)
    wuv_np = np.zeros((MLA_HEADS, MLA_HEADS * MLA_V), np.float32)
    for h in range(MLA_HEADS):
        wuv_np[h, h * MLA_V:(h + 1) * MLA_V] = 1.0
    wuv = jnp.transpose(w_uv, (1, 0, 2))
    wuv = (jnp.tile(wuv, (1, 1, MLA_HEADS)) * jnp.asarray(wuv_np)[:, None, :]).astype(BF16)
    return w1, wqa, wk, pk, wv, wuk, wuv


def _rope_table(pos):
    inv = ROPE_BASE ** (-jnp.arange(0, MLA_ROPE, 2, dtype=F32) / MLA_ROPE)
    ang = pos.astype(F32)[:, None] * inv[None, :]
    return jnp.concatenate([jnp.cos(ang), jnp.sin(ang)], axis=1)


def _rope_placement():
    half = MLA_ROPE // 2
    place = np.zeros((MLA_ROPE, 4 * LANE), np.float32)
    rows = np.zeros((2, 4 * LANE), np.float32)
    for i in range(half):
        c, s = i, half + i
        place[c, MLA_NOPE + i] = place[c, MLA_NOPE + half + i] = 1.0
        place[s, LANE + MLA_NOPE + i] = -1.0
        place[s, 2 * LANE + MLA_NOPE + half + i] = 1.0
        place[c, 3 * LANE + i] = place[c, 3 * LANE + half + i] = 1.0
        place[s, 3 * LANE + 2 * half + i] = place[s, 3 * LANE + 3 * half + i] = 1.0
    rows[0, :MLA_NOPE] = 1.0
    rows[1, :3 * LANE] = MLA_SCALE * LOG2E
    rows[1, 3 * LANE:] = 1.0
    return jnp.asarray(place).astype(BF16), jnp.asarray(rows)


def _seg_ones():
    idx = np.arange(RWKV_DIM) // RWKV_N
    return jnp.asarray((idx[:, None] == idx[None, :]).astype(np.float32)).astype(BF16)


def _dispatch(route, tile_counts, t_total):
    A = t_total * TOP_K
    n_tiles = tile_counts.shape[0]
    counts_te = tile_counts[:, 0, N_GROUPS:N_GROUPS + N_EXPERTS].astype(jnp.int32)
    counts = jnp.sum(counts_te, axis=0)
    blocks_per_e = (counts + MOE_BLK - 1) // MOE_BLK
    blk_end = jnp.cumsum(blocks_per_e)
    blk_start = blk_end - blocks_per_e
    tile_off = jnp.cumsum(counts_te, axis=0) - counts_te
    base = blk_start[None, :] * MOE_BLK + tile_off
    base_tok = jnp.repeat(base, t_total // n_tiles, axis=0)
    experts = jnp.arange(N_EXPERTS, dtype=jnp.int32)[None, :]
    dest = []
    for k in range(TOP_K):
        e_k = route[:, k].astype(jnp.int32)
        rank_k = route[:, 4 + k].astype(jnp.int32)
        dest.append(jnp.sum(jnp.where(e_k[:, None] == experts, base_tok, 0), axis=-1) + rank_k)
    n_blk = -(-A // MOE_BLK) + N_EXPERTS
    blk = jnp.arange(n_blk, dtype=jnp.int32)
    block_e = jnp.minimum(jnp.sum((blk[:, None] >= blk_end[None, :]).astype(jnp.int32), axis=1),
                          N_EXPERTS - 1).astype(jnp.int32)
    n_used = blk_end[-1:].astype(jnp.int32)
    return dest, block_e, n_used


def _sc_gather_rows(table, idx):
    n_rows, width = idx.shape[0], table.shape[1]
    n_workers = SC_CORES * SC_SUBCORES
    per_worker = n_rows // n_workers
    assert n_rows % n_workers == 0 and per_worker % SC_WINDOW == 0
    mesh = plsc.VectorSubcoreMesh(core_axis_name="c", subcore_axis_name="s")

    @functools.partial(
        pl.kernel, mesh=mesh,
        out_type=jax.ShapeDtypeStruct((n_rows, width), table.dtype),
        scratch_types=[
            pltpu.VMEM((SC_WINDOW,), jnp.int32),
            pltpu.VMEM((SC_WINDOW, width), table.dtype),
            pltpu.SemaphoreType.DMA,
        ],
    )
    def gather(table_hbm, idx_hbm, out_hbm, idx_v, rows_v, sem):
        wid = lax.axis_index("s") * SC_CORES + lax.axis_index("c")
        base = wid * per_worker

        @pl.loop(0, per_worker // SC_WINDOW)
        def _(w):
            off = pl.multiple_of(base + w * SC_WINDOW, SC_WINDOW)
            pltpu.sync_copy(idx_hbm.at[pl.ds(off, SC_WINDOW)], idx_v)
            pltpu.async_copy(table_hbm.at[idx_v], rows_v, sem).wait()
            pltpu.sync_copy(rows_v, out_hbm.at[pl.ds(off, SC_WINDOW)])

    return gather(table, idx)


def _sc_scatter_rows(src, idx_a, idx_b, n_slots):
    n_rows, width = src.shape
    n_workers = SC_CORES * SC_SUBCORES
    per_worker = n_rows // n_workers
    assert n_rows % n_workers == 0 and per_worker % SC_WINDOW == 0
    n_win = per_worker // SC_WINDOW
    mesh = plsc.VectorSubcoreMesh(core_axis_name="c", subcore_axis_name="s")

    @functools.partial(
        pl.kernel, mesh=mesh,
        out_type=jax.ShapeDtypeStruct((n_slots, width), src.dtype),
        scratch_types=[
            pltpu.VMEM((1, SC_WINDOW), jnp.int32),
            pltpu.VMEM((1, SC_WINDOW), jnp.int32),
            pltpu.VMEM((SC_WINDOW, width), src.dtype),
        ],
    )
    def scatter(src_hbm, ia_hbm, ib_hbm, out_hbm, ia_v, ib_v, rows_v):
        wid = lax.axis_index("s") * SC_CORES + lax.axis_index("c")

        @pl.loop(0, n_win)
        def _(w):
            win = wid * n_win + w
            off = pl.multiple_of(win * SC_WINDOW, SC_WINDOW)
            pltpu.sync_copy(src_hbm.at[pl.ds(off, SC_WINDOW)], rows_v)
            pltpu.sync_copy(ia_hbm.at[pl.ds(win, 1)], ia_v)
            pltpu.sync_copy(ib_hbm.at[pl.ds(win, 1)], ib_v)
            pltpu.sync_copy(rows_v, out_hbm.at[ia_v.at[0]])
            pltpu.sync_copy(rows_v, out_hbm.at[ib_v.at[0]])

    return scatter(src, idx_a, idx_b)


def kernel(x_prompt, x_sample, cache_ckv, cache_kpe, state_wkv, state_shift, w_in, q_norm_g, kv_norm_g, w_uq,
           w_ukv, mu_shift, w0, w2, a0, a2, g2, k_k, k_a, r_k, lnx_g, lnx_b, w_o, ln1_g, ln1_b, w_gr, b_gr,
           w_er, b_er, w_eg, w_eu, w_ed, ln2_g, ln2_b):
    B, S, D = x_prompt.shape
    DB, DS, _ = x_sample.shape
    past = cache_ckv.shape[2]
    Tp, Ts = B * S, DB * DS
    T = Tp + Ts
    assert D == D_MODEL and DS == SHIFT_GROUP and S % ATTN_TQ == 0 and S % (CHUNK * WKV_SUB) == 0
    assert Tp % TOKEN_TILE == 0 and T % TOKEN_TILE == 0 and w_in.shape[0] == DEPTH

    l = 0
    xp, xs_in = x_prompt.reshape(Tp, D), x_sample.reshape(Ts, D)
    w1, wqa, wk, pk, wv, wuk, wuv = _prep_weights(w_in[l], w_uq[l], w_ukv[l])
    pos = jnp.concatenate([jnp.arange(S, dtype=jnp.int32),
                           jnp.tile(past + jnp.arange(DS, dtype=jnp.int32), TOKEN_TILE // DS)])
    rope = _rope_table(pos)

    seg = _seg_ones()
    vec = lambda a: a.reshape(1, -1)
    rwkv_w = (vec(mu_shift[l]), vec(w0[l]), vec(a0[l]), vec(k_k[l]), vec(k_a[l]), vec(r_k[l]),
              w2[l].astype(BF16), a2[l].astype(BF16), g2[l].astype(BF16), seg)
    (q, kcat, vcat, ckv_p, ckv_s, kpe_p, kpe_s, last_rows,
     r, lw, kh, v, na, b, bonus, g) = _proj_call(
        xp, xs_in, rope, S // TOKEN_TILE, *_rope_placement(), w1, q_norm_g[l][None], kv_norm_g[l][None],
        wqa, wk, pk, wv, state_shift[l][:, None, :], rwkv_w)

    attn_p = _attn_call(q, kcat, vcat, B, S)
    attn_s = _mla_sample_call(q, ckv_s, kpe_s, cache_ckv[l], jnp.swapaxes(cache_kpe[l], 1, 2), wuk, wuv,
                              Tp, DB, DS)

    scan_in = (r, lw, kh, v, na, b)
    h0_p = jnp.zeros((B, RWKV_HEADS, RWKV_N, RWKV_N), F32)
    y_p, hT_p = _wkv_call(scan_in, h0_p, 0, B, S // CHUNK, CHUNK, WKV_SUB, math.gcd(B, WKV_PAR))
    h0_s = state_wkv[l]
    y_s, hT_s = _wkv_call(scan_in, h0_s, Tp, DB, 1, DS, 1, math.gcd(DB, WKV_SUB * WKV_PAR))

    wo_b = w_o[l].astype(BF16)
    wr = jnp.concatenate([w_gr[l], w_er[l], jnp.zeros((D, LANE - N_GROUPS - N_EXPERTS), F32)], axis=1)
    wr_hi = wr.astype(BF16)
    wr_lo = (wr - wr_hi.astype(F32)).astype(BF16)
    br = jnp.concatenate([b_gr[l], b_er[l], jnp.zeros((LANE - N_GROUPS - N_EXPERTS,), F32)])[None]
    n_tiles = T // TOKEN_TILE
    wave_tiles = [(t0, min(t0 + MOE_WAVE_TILES, n_tiles) - t0) for t0 in range(0, n_tiles, MOE_WAVE_TILES)]
    out_p, out_s = None, None
    for tile0, nt in wave_tiles:
        t_w = nt * TOKEN_TILE
        h, hpk, route, tile_counts = _mix_call(
            xp, xs_in, attn_p, attn_s, y_p, y_s, bonus, g, seg, vec(lnx_g[l]), vec(lnx_b[l]),
            wo_b[:MLA_HEADS * MLA_V], wo_b[MLA_HEADS * MLA_V:], vec(ln1_g[l]), vec(ln1_b[l]),
            jnp.stack([wr_hi, wr_lo]), br, tile0, nt)
        dest, block_e, n_used = _dispatch(route, tile_counts, t_w)
        win = lambda a: a.reshape(t_w // SC_WINDOW, SC_WINDOW)
        xs = _sc_scatter_rows(hpk, win(dest[0]), win(dest[1]), block_e.shape[0] * MOE_BLK)
        ys = _expert_call(block_e, n_used, xs, w_eg[l], w_eu[l], w_ed[l])
        yab = _sc_gather_rows(ys, jnp.concatenate(dest))
        out_p, wave_s = _combine_call(h, yab, route, vec(ln2_g[l]), vec(ln2_b[l]), Tp, Ts, tile0, out_p)
        assert wave_s is None or out_s is None, "the sample tiles must fall inside one wave"
        out_s = wave_s if wave_s is not None else out_s

    y_prompt = out_p.reshape(B, S, D)
    y_sample = out_s.reshape(DB, DS, D)
    p_ckv = ckv_p.reshape(1, B, S, KV_LORA)
    p_kpe = jnp.transpose(kpe_p.reshape(MLA_ROPE, B, S), (1, 2, 0))[None]
    s_ckv = ckv_s.reshape(1, DB, DS, KV_LORA)
    s_kpe = kpe_s.reshape(1, DB, DS, MLA_ROPE)
    p_wkv = hT_p[None]
    s_wkv = hT_s[None]
    gp = S // SHIFT_GROUP
    p_sh = last_rows[gp - 1:B * gp:gp][None]
    s_sh = last_rows[B * gp:][None]
    return (y_prompt, y_sample, p_ckv, p_kpe, p_wkv, p_sh, s_ckv, s_kpe, s_wkv, s_sh)
```

```python
import functools
import math

import numpy as np
import jax
import jax.numpy as jnp
from jax import lax
from jax.experimental import pallas as pl
from jax.experimental.pallas import tpu as pltpu
from jax.experimental.pallas import tpu_sc as plsc

F32 = jnp.float32
BF16 = jnp.bfloat16

D_MODEL = 1024
CHUNK = 64
MLA_HEADS = 8
MLA_NOPE = 64
MLA_ROPE = 32
MLA_V = 64
Q_LORA = 384
KV_LORA = 256
ROPE_BASE = 10000.0
MLA_IN = Q_LORA + KV_LORA + MLA_ROPE
MLA_SCALE = (MLA_NOPE + MLA_ROPE) ** -0.5
RWKV_HEADS = 8
RWKV_N = 64
RWKV_DIM = RWKV_HEADS * RWKV_N
DECAY_LORA = 64
AAA_LORA = 64
GATE_LORA = 128
RWKV_IN = 3 * RWKV_DIM + DECAY_LORA + AAA_LORA + GATE_LORA
N_GROUPS = 4
EXPERTS_PER_GROUP = 8
N_EXPERTS = N_GROUPS * EXPERTS_PER_GROUP
TOP_K = 2
D_EXPERT = 256
MOE_BLK = 512
LN_EPS = 1e-5
RMS_EPS = 1e-6
GN_EPS = 64e-5
NEG_INF = -1e30
DEPTH = 1
DN_ALPHA = (2 * DEPTH) ** 0.25

LANE = 128
HEAD_PAD = 128
PROJ_KPE = Q_LORA + KV_LORA
PROJ_PR = PROJ_KPE + LANE
PROJ_W = PROJ_PR + RWKV_IN
SHIFT_GROUP = 32
TOKEN_TILE = 512
ATTN_TQ = 1024
ATTN_TK = 512
V_ONE_LANE = (MLA_V, 0)
LOG2E = math.log2(math.e)
VMEM_LIMIT = 48 * 1024 * 1024
PROJ_VMEM_LIMIT = 56 * 1024 * 1024
SC_CORES = 2
SC_SUBCORES = 16
SC_WINDOW = 32
WKV_GROUP = 4
WKV_SUB = 2
WKV_PAR = 2
MIX_PARTS = 2
MOE_WAVE_TILES = 34


def _cparams(sem):
    return pltpu.CompilerParams(dimension_semantics=sem, vmem_limit_bytes=VMEM_LIMIT)


def _split3(x):
    hi = x.astype(BF16)
    r1 = x - hi.astype(F32)
    mid = r1.astype(BF16)
    lo = (r1 - mid.astype(F32)).astype(BF16)
    return hi, mid, lo


def _dot(a, b):
    return jnp.dot(a, b, preferred_element_type=F32)


def _dot_nt(a, b):
    return lax.dot_general(a, b, (((1,), (1,)), ((), ())), preferred_element_type=F32)


def _dot_exact_rhs(x, w):
    hi = x.astype(BF16)
    lo = (x - hi.astype(F32)).astype(BF16)
    return _dot(hi, w) + _dot(lo, w)


def _dot_exact_lhs(w, x):
    hi, mid, lo = _split3(x)
    return _dot(w, hi) + _dot(w, mid) + _dot(w, lo)


def _pack_bf16_pairs(x):
    n = x.shape[1] // 2
    bits = pltpu.bitcast(x.astype(BF16).astype(F32), jnp.int32)
    return (bits[:, :n] & jnp.int32(-65536)) | lax.shift_right_logical(bits[:, n:], jnp.int32(16))


def _unpack_bf16_pairs(p):
    hi = pltpu.bitcast(p & jnp.int32(-65536), F32)
    lo = pltpu.bitcast(lax.shift_left(p, jnp.int32(16)), F32)
    return jnp.concatenate([hi, lo], axis=1)


def _sigmoid(x):
    return 1.0 / (1.0 + jnp.exp(-x))


def _layer_norm(x, g, b):
    xc = x - jnp.mean(x, -1, keepdims=True)
    var = jnp.mean(xc * xc, -1, keepdims=True)
    return xc * lax.rsqrt(var + LN_EPS) * g + b


def _proj_kernel(xp_ref, xs_ref, rope_ref, rp_ref, rc_ref, w1_ref, gq_ref, gkv_ref, wqa_ref, wk_ref, pk_ref, wv_ref,
                 shift_ref, *rest, n_p, tm, seq_tiles):
    rwkv_w = rest[:10]
    q_ref, k_ref, v_ref, ckvp_ref, ckvs_ref, kpep_ref, kpes_ref, last_ref = rest[10:18]
    rwkv_out = rest[18:26]
    carry_scr = rest[26]
    i = pl.program_id(0)
    is_p = i < n_p
    x = jnp.where(is_p, xp_ref[...], xs_ref[...]).astype(BF16)
    proj = _dot(x, w1_ref[...])
    c_q = proj[:, :Q_LORA]
    c_kv = proj[:, Q_LORA:Q_LORA + KV_LORA]
    kp = proj[:, PROJ_KPE:PROJ_PR]
    pr = proj[:, PROJ_PR:]

    ng = tm // SHIFT_GROUP
    pr3 = pr.reshape(ng, SHIFT_GROUP, RWKV_IN)
    last_ref[...] = pr3[:, SHIFT_GROUP - 1, :]
    rolled = pltpu.roll(pr, 1, 0).reshape(ng, SHIFT_GROUP, RWKV_IN)
    first_row = jnp.where(i % seq_tiles == 0, 0.0, carry_scr[...])
    bound = jnp.where(is_p, first_row[None], shift_ref[...])
    row_in_grp = lax.broadcasted_iota(jnp.int32, pr3.shape, 1)
    grp = lax.broadcasted_iota(jnp.int32, pr3.shape, 0)
    use_bound = (row_in_grp == 0) & (grp <= jnp.where(is_p, 0, ng))
    prev = jnp.where(use_bound, bound, rolled).reshape(tm, RWKV_IN)
    carry_scr[...] = pr[tm - 1:tm, :]

    cqn = c_q * lax.rsqrt(jnp.mean(c_q * c_q, -1, keepdims=True) + RMS_EPS) * gq_ref[...]
    ckv = c_kv * lax.rsqrt(jnp.mean(c_kv * c_kv, -1, keepdims=True) + RMS_EPS) * gkv_ref[...]

    rope = (_dot_exact_rhs(rope_ref[...], rp_ref[...]) + rc_ref[0:1, :]) * rc_ref[1:2, :]
    cq = rope[:, :LANE]
    sq_up = rope[:, LANE:2 * LANE]
    sq_dn = rope[:, 2 * LANE:3 * LANE]
    kt = rope[:, 3 * LANE:]
    prod = kp * kt
    kpe = prod[:, :MLA_ROPE] + prod[:, MLA_ROPE:2 * MLA_ROPE]

    qa = _dot(cqn.astype(BF16), wqa_ref[...])
    half = MLA_ROPE // 2
    for h in range(MLA_HEADS):
        qh = qa[:, h * HEAD_PAD:(h + 1) * HEAD_PAD]
        rot = pltpu.roll(qh, HEAD_PAD - half, 1) * sq_up + pltpu.roll(qh, half, 1) * sq_dn
        q_ref[:, h * HEAD_PAD:(h + 1) * HEAD_PAD] = (qh * cq + rot).astype(BF16)

    ckv_b = ckv.astype(BF16)
    k = _dot(ckv_b, wk_ref[...]) + _dot(kpe.astype(BF16), pk_ref[...])
    k_ref[...] = k.astype(BF16)
    lane = lax.broadcasted_iota(jnp.int32, (1, MLA_HEADS * HEAD_PAD), 1)
    odd = (lane // HEAD_PAD) % 2
    one_lane = jnp.where(odd == 1, V_ONE_LANE[1], V_ONE_LANE[0])
    v_one = jnp.where(lane % HEAD_PAD == one_lane, 1.0, 0.0)
    v_ref[...] = (_dot(ckv_b, wv_ref[...]) + v_one).astype(BF16)
    _rwkv_heads(pr, prev, *rwkv_w, *rwkv_out)

    @pl.when(is_p)
    def _():
        ckvp_ref[...] = ckv
        eye = jnp.where(lax.broadcasted_iota(jnp.int32, (MLA_ROPE, MLA_ROPE), 0)
                        == lax.broadcasted_iota(jnp.int32, (MLA_ROPE, MLA_ROPE), 1), 1.0, 0.0).astype(BF16)
        kpep_ref[...] = sum(_dot_nt(eye, part) for part in _split3(kpe))

    @pl.when(jnp.logical_not(is_p))
    def _():
        ckvs_ref[...] = ckv
        kpes_ref[...] = kpe


def _split_rows(n_p, tile0=0):
    return ((lambda i: (jnp.minimum(i + tile0, n_p - 1), 0)),
            (lambda i: (jnp.maximum(i + tile0 - n_p, 0), 0)))


def _proj_call(xp, xs, rope, seq_tiles, rope_place, rope_rows, w1, gq, gkv, wqa, wk, pk, wv, shift, rwkv_w):
    Tp, Ts = xp.shape[0], xs.shape[0]
    T = Tp + Ts
    tm = TOKEN_TILE
    n_p = Tp // tm
    rope_tiles = seq_tiles
    row = lambda i: (i, 0)
    full = lambda i: (0, 0)
    row_p, row_s = _split_rows(n_p)
    wide = MLA_HEADS * HEAD_PAD
    ng = tm // SHIFT_GROUP
    once = pl.Buffered(1)
    vec = pl.BlockSpec((1, RWKV_DIM), full, pipeline_mode=once)
    rwkv_specs = [pl.BlockSpec((1, RWKV_IN), full, pipeline_mode=once), vec, vec, vec, vec, vec,
                  pl.BlockSpec((DECAY_LORA, RWKV_DIM), full, pipeline_mode=once),
                  pl.BlockSpec((AAA_LORA, RWKV_DIM), full, pipeline_mode=once),
                  pl.BlockSpec((GATE_LORA, RWKV_DIM), full, pipeline_mode=once),
                  pl.BlockSpec((RWKV_DIM, RWKV_DIM), full, pipeline_mode=once)]
    tok = pl.BlockSpec((tm, RWKV_DIM), row)
    return pl.pallas_call(
        functools.partial(_proj_kernel, n_p=n_p, tm=tm, seq_tiles=seq_tiles),
        grid=(T // tm,),
        in_specs=[
            pl.BlockSpec((tm, D_MODEL), row_p),
            pl.BlockSpec((tm, D_MODEL), row_s),
            pl.BlockSpec((tm, MLA_ROPE), lambda i: (jnp.where(i < n_p, i % rope_tiles, rope_tiles), 0)),
            pl.BlockSpec((MLA_ROPE, 4 * LANE), full, pipeline_mode=once),
            pl.BlockSpec((2, 4 * LANE), full, pipeline_mode=once),
            pl.BlockSpec((D_MODEL, PROJ_W), full, pipeline_mode=once),
            pl.BlockSpec((1, Q_LORA), full, pipeline_mode=once),
            pl.BlockSpec((1, KV_LORA), full, pipeline_mode=once),
            pl.BlockSpec((Q_LORA, wide), full, pipeline_mode=once),
            pl.BlockSpec((KV_LORA, wide), full, pipeline_mode=once),
            pl.BlockSpec((MLA_ROPE, wide), full, pipeline_mode=once),
            pl.BlockSpec((KV_LORA, wide), full, pipeline_mode=once),
            pl.BlockSpec((ng, 1, RWKV_IN), lambda i: (jnp.maximum(i - n_p, 0), 0, 0)),
        ] + rwkv_specs,
        out_specs=[
            pl.BlockSpec((tm, wide), row),
            pl.BlockSpec((tm, wide), row),
            pl.BlockSpec((tm, wide), row),
            pl.BlockSpec((tm, KV_LORA), row_p),
            pl.BlockSpec((tm, KV_LORA), row_s),
            pl.BlockSpec((MLA_ROPE, tm), lambda i: (0, jnp.minimum(i, n_p - 1))),
            pl.BlockSpec((tm, MLA_ROPE), row_s),
            pl.BlockSpec((ng, RWKV_IN), row),
        ] + [tok] * 8,
        out_shape=[
            jax.ShapeDtypeStruct((T, wide), BF16),
            jax.ShapeDtypeStruct((T, wide), BF16),
            jax.ShapeDtypeStruct((T, wide), BF16),
            jax.ShapeDtypeStruct((Tp, KV_LORA), F32),
            jax.ShapeDtypeStruct((Ts, KV_LORA), F32),
            jax.ShapeDtypeStruct((MLA_ROPE, Tp), F32),
            jax.ShapeDtypeStruct((Ts, MLA_ROPE), F32),
            jax.ShapeDtypeStruct((T // SHIFT_GROUP, RWKV_IN), F32),
        ] + [jax.ShapeDtypeStruct((T, RWKV_DIM), F32)] * 8,
        scratch_shapes=[pltpu.VMEM((1, RWKV_IN), F32)],
        compiler_params=pltpu.CompilerParams(dimension_semantics=("arbitrary",), vmem_limit_bytes=PROJ_VMEM_LIMIT),
        name="proj",
    )(xp, xs, rope, rope_place, rope_rows, w1, gq, gkv, wqa, wk, pk, wv, shift, *rwkv_w)


def _attn_kernel(q_ref, k_ref, v_ref, o_ref, m_scr, acc_scr, *, tq, tk):
    qi = pl.program_id(2)
    m_scr[...] = jnp.full(m_scr.shape, NEG_INF, F32)
    acc_scr[...] = jnp.zeros(acc_scr.shape, F32)
    n_diag = tq // tk

    H = range(2)
    sls = [slice(h * HEAD_PAD, (h + 1) * HEAD_PAD) for h in H]

    def scores(k0, width, rows):
        return [_dot_nt(q_ref[rows, sl], k_ref[pl.ds(k0, width), sl]) for sl in sls]

    def accumulate(s, k0, width, rows):
        m_prev = [m_scr[h, rows, :] for h in H]
        m_new = [jnp.maximum(m_prev[h], jnp.max(s[h], axis=1, keepdims=True)) for h in H]
        pexp = [jnp.exp2(s[h] - jnp.tile(m_new[h], (1, width // LANE))).astype(BF16) for h in H]
        pv = [_dot(pexp[h], v_ref[pl.ds(k0, width), sls[h]]) for h in H]
        for h in H:
            acc_scr[h, rows, :] = jnp.exp2(m_prev[h] - m_new[h]) * acc_scr[h, rows, :] + pv[h]
            m_scr[h, rows, :] = m_new[h]

    def kv_block(k0, width, rows, masked):
        s = scores(k0, width, rows)
        if masked:
            n_rows = rows.stop - rows.start
            r = lax.broadcasted_iota(jnp.int32, (n_rows, width), 0) // CHUNK
            c = lax.broadcasted_iota(jnp.int32, (n_rows, width), 1) // CHUNK
            s = [jnp.where(c <= r, s[h], NEG_INF) for h in H]
        accumulate(s, k0, width, rows)

    all_rows = slice(0, tq)

    def pair(t, carry):
        k0a = pl.multiple_of(2 * t * tq, tq)
        k0b = pl.multiple_of(k0a + tq, tq)
        s_a = scores(k0a, tq, all_rows)
        s_b = scores(k0b, tq, all_rows)
        accumulate(s_a, k0a, tq, all_rows)
        accumulate(s_b, k0b, tq, all_rows)
        return carry

    lax.fori_loop(0, qi // 2, pair, 0)

    @pl.when(qi % 2 == 1)
    def _():
        kv_block(pl.multiple_of((qi - 1) * tq, tq), tq, all_rows, False)

    for d in range(n_diag):
        kv_block(pl.multiple_of(qi * tq + d * tk, tk), tk, slice(d * tk, tq), True)
    acc0, acc1 = acc_scr[0], acc_scr[1]
    lane = lax.broadcasted_iota(jnp.int32, acc0.shape, 1)
    l0 = acc0[:, V_ONE_LANE[0]:V_ONE_LANE[0] + 1]
    l1 = acc1[:, V_ONE_LANE[1]:V_ONE_LANE[1] + 1]
    o_ref[...] = jnp.where(lane < MLA_V, acc0 / l0, acc1 / l1).astype(o_ref.dtype)


def _attn_call(q, k, v, n_batch, seq):
    tq, tk = ATTN_TQ, ATTN_TK
    nq = seq // tq
    hp = MLA_HEADS // 2
    resident = pl.BlockSpec((seq, 2 * HEAD_PAD), lambda b, h, i: (b, h), pipeline_mode=pl.Buffered(1))
    return pl.pallas_call(
        functools.partial(_attn_kernel, tq=tq, tk=tk),
        grid=(n_batch, hp, nq),
        in_specs=[
            pl.BlockSpec((tq, 2 * HEAD_PAD), lambda b, h, i: (b * nq + i, h)),
            resident,
            resident,
        ],
        out_specs=pl.BlockSpec((tq, LANE), lambda b, h, i: (b * nq + i, h)),
        out_shape=jax.ShapeDtypeStruct((n_batch * seq, hp * LANE), BF16),
        scratch_shapes=[
            pltpu.VMEM((2, tq, LANE), F32),
            pltpu.VMEM((2, tq, LANE), F32),
        ],
        compiler_params=_cparams(("parallel", "parallel", "arbitrary")),
        name="attn",
    )(q, k, v)


def _mla_sample_kernel(q_ref, cn_ref, kn_ref, cp_ref, kp_ref, wuk_ref, wuv_ref, o_ref, *, past, dec):
    cp = cp_ref[0].astype(BF16)
    kp_t = kp_ref[0].astype(BF16)
    cn = cn_ref[...].astype(BF16)
    kn = kn_ref[...].astype(BF16)
    R = MLA_HEADS * dec
    qrow = (past + lax.broadcasted_iota(jnp.int32, (R, past), 0) % dec) // CHUNK
    vis_p = (lax.broadcasted_iota(jnp.int32, (R, past), 1) // CHUNK) <= qrow
    qrow_n = (past + lax.broadcasted_iota(jnp.int32, (R, dec), 0) % dec) // CHUNK
    vis_n = ((past + lax.broadcasted_iota(jnp.int32, (R, dec), 1)) // CHUNK) <= qrow_n
    q_lat, qp = [], []
    for h in range(MLA_HEADS):
        qn = q_ref[:, h * HEAD_PAD:h * HEAD_PAD + MLA_NOPE]
        qp.append(q_ref[:, h * HEAD_PAD + MLA_NOPE:h * HEAD_PAD + MLA_NOPE + MLA_ROPE])
        q_lat.append(_dot(qn, wuk_ref[h]).astype(BF16))
    q_lat = jnp.concatenate(q_lat, axis=0)
    qp = jnp.concatenate(qp, axis=0)
    s_p = jnp.where(vis_p, _dot_nt(q_lat, cp) + _dot(qp, kp_t), NEG_INF)
    s_n = jnp.where(vis_n, _dot_nt(q_lat, cn) + _dot_nt(qp, kn), NEG_INF)
    m = jnp.maximum(jnp.max(s_p, axis=1, keepdims=True), jnp.max(s_n, axis=1, keepdims=True))
    e_p = jnp.exp2(s_p - m)
    e_n = jnp.exp2(s_n - m)
    l = jnp.sum(e_p, axis=1, keepdims=True) + jnp.sum(e_n, axis=1, keepdims=True)
    o_lat = ((_dot(e_p.astype(BF16), cp) + _dot(e_n.astype(BF16), cn)) / l).astype(BF16)
    out = jnp.zeros((dec, MLA_HEADS * MLA_V), F32)
    for h in range(MLA_HEADS):
        out = out + _dot(o_lat[h * dec:(h + 1) * dec], wuv_ref[h])
    o_ref[...] = out.astype(o_ref.dtype)


def _mla_sample_call(q, ckv, kpe, cache_ckv, cache_kpe, wuk, wuv, row0, n_seq, dec):
    past = cache_ckv.shape[1]
    blk0 = row0 // dec
    wide = MLA_HEADS * HEAD_PAD
    return pl.pallas_call(
        functools.partial(_mla_sample_kernel, past=past, dec=dec),
        grid=(n_seq,),
        in_specs=[
            pl.BlockSpec((dec, wide), lambda b: (blk0 + b, 0)),
            pl.BlockSpec((dec, KV_LORA), lambda b: (b, 0)),
            pl.BlockSpec((dec, MLA_ROPE), lambda b: (b, 0)),
            pl.BlockSpec((1, past, KV_LORA), lambda b: (b, 0, 0)),
            pl.BlockSpec((1, MLA_ROPE, past), lambda b: (b, 0, 0)),
            pl.BlockSpec((MLA_HEADS, MLA_NOPE, KV_LORA), lambda b: (0, 0, 0)),
            pl.BlockSpec((MLA_HEADS, KV_LORA, MLA_HEADS * MLA_V), lambda b: (0, 0, 0)),
        ],
        out_specs=pl.BlockSpec((dec, MLA_HEADS * MLA_V), lambda b: (b, 0)),
        out_shape=jax.ShapeDtypeStruct((n_seq * dec, MLA_HEADS * MLA_V), BF16),
        compiler_params=_cparams(("parallel",)),
        name="mla_sample",
    )(q, ckv, kpe, cache_ckv, cache_kpe, wuk, wuv)


def _rwkv_heads(pr, prev, mu_ref, w0_ref, a0_ref, kk_ref, ka_ref, rk_ref, w2_ref, a2_ref, g2_ref,
                seg_ref, r_ref, lw_ref, kh_ref, v_ref, na_ref, b_ref, bonus_ref, g_ref):
    u = pr + mu_ref[...] * (prev - pr)
    o1, o2, o3 = RWKV_DIM, 2 * RWKV_DIM, 3 * RWKV_DIM
    o4, o5 = o3 + DECAY_LORA, o3 + DECAY_LORA + AAA_LORA
    r, k, v = u[:, :o1], u[:, o1:o2], u[:, o2:o3]
    w_lo, a_lo, g_lo = u[:, o3:o4], u[:, o4:o5], u[:, o5:]
    wl = w0_ref[...] + _dot(jnp.tanh(w_lo).astype(BF16), w2_ref[...])
    lw_ref[...] = -math.exp(-0.5) * _sigmoid(wl)
    a = _sigmoid(a0_ref[...] + _dot(a_lo.astype(BF16), a2_ref[...]))
    g_ref[...] = _dot(_sigmoid(g_lo).astype(BF16), g2_ref[...])
    seg = seg_ref[...]
    kk = k * kk_ref[...]
    kk = kk / jnp.maximum(jnp.sqrt(_dot_exact_rhs(kk * kk, seg)), 1e-12)
    kh = k * (1.0 + (a - 1.0) * ka_ref[...])
    r_ref[...] = r
    kh_ref[...] = kh
    v_ref[...] = v
    na_ref[...] = -kk
    b_ref[...] = kk * a
    bonus_ref[...] = _dot_exact_rhs(r * kh * rk_ref[...], seg) * v


def _wkv_kernel(*refs, C, n_sub, n_par):
    GW = WKV_GROUP * RWKV_N
    R = WKV_GROUP * C
    n_grp = RWKV_HEADS // WKV_GROUP
    n_lev = int(round(math.log2(C))) - 1
    c = pl.program_id(1)
    tok = [refs[6 * p:6 * p + 6] for p in range(n_par)]
    h0_ref, y_ref, hT_ref, h_scr = refs[6 * n_par:]
    n_state = n_par * n_grp

    def head_block(hh):
        return slice(hh * RWKV_N, (hh + 1) * RWKV_N)

    def transposed(x):
        n = x.shape[0]
        eye = jnp.where(lax.broadcasted_iota(jnp.int32, (n, n), 0) == lax.broadcasted_iota(jnp.int32, (n, n), 1),
                        1.0, 0.0).astype(BF16)
        return sum(_dot_nt(eye, part) for part in _split3(x))

    @pl.when(c == 0)
    def _():
        h_scr[...] = jnp.zeros(h_scr.shape, F32)
        for p in range(n_par):
            for hd in range(RWKV_HEADS):
                g, hh = divmod(hd, WKV_GROUP)
                h_scr[p * n_grp + g, head_block(hh), head_block(hh)] = transposed(h0_ref[p, hd])

    row = lax.broadcasted_iota(jnp.int32, (C, C), 0)
    col = lax.broadcasted_iota(jnp.int32, (C, C), 1)
    tri = jnp.where(col <= row, 1.0, 0.0).astype(BF16)
    rr = lax.broadcasted_iota(jnp.int32, (R, R), 0)
    cc = lax.broadcasted_iota(jnp.int32, (R, R), 1)
    same = (rr // C) == (cc // C)
    ti = lax.broadcasted_iota(jnp.int32, (C, R), 0)
    si = lax.broadcasted_iota(jnp.int32, (C, R), 1) % C
    strict4 = si < ti
    lower4 = si <= ti
    eye4 = jnp.where(si == ti, 1.0, 0.0)
    keep = (lax.broadcasted_iota(jnp.int32, (R, GW), 0) // C
            == lax.broadcasted_iota(jnp.int32, (R, GW), 1) // RWKV_N)
    gr = lax.broadcasted_iota(jnp.int32, (GW, GW), 0)
    gc = lax.broadcasted_iota(jnp.int32, (GW, GW), 1)
    eye_g = gr == gc
    same_head = (gr // RWKV_N) == (gc // RWKV_N)
    eye_g_bf = jnp.where(eye_g, 1.0, 0.0).astype(BF16)

    def rows4(x4):
        return jnp.concatenate([x4] * WKV_GROUP, axis=0)

    def stack(x4):
        return jnp.where(keep, rows4(x4), jnp.zeros((), x4.dtype))


    sls = [slice(g * GW, (g + 1) * GW) for g in range(n_grp)]
    J = [(ci, q) for ci in range(n_sub) for q in range(n_state)]
    ops, p_end = {}, {}
    for p, ci in [(p, ci) for p in range(n_par) for ci in range(n_sub)]:
        r_ref, lw_ref, k_ref, v_ref, a_ref, b_ref = tok[p]
        rows = slice(ci * C, (ci + 1) * C)
        lw = lw_ref[rows, :]
        cum = _dot_exact_lhs(tri, lw)
        cum_end = cum[C - 1:C, :]
        e_neg = jnp.exp(-cum)
        e_end = jnp.exp(cum_end - cum)
        b_in = b_ref[rows, :]
        k_in = k_ref[rows, :]
        full = ((a_ref[rows, :] * jnp.exp(cum - lw)).astype(BF16),
                (r_ref[rows, :] * jnp.exp(cum)).astype(BF16),
                (b_in * e_neg).astype(BF16), (k_in * e_neg).astype(BF16),
                (b_in * e_end).astype(BF16), (k_in * e_end).astype(BF16),
                v_ref[rows, :].astype(BF16))
        p_end[p, ci] = jnp.exp(cum_end)
        for g in range(n_grp):
            ops[ci, p * n_grp + g] = [t[:, sls[g]] for t in full]
    a4, r4, b4, k4, be4, ke4, v4 = [{j: ops[j][i] for j in J} for i in range(7)]
    v_s = {j: stack(v4[j]) for j in J}
    m = {j: _dot_nt(jnp.concatenate([a4[j], r4[j]], axis=0),
                    jnp.concatenate([stack(b4[j]), stack(k4[j])], axis=0)) for j in J}
    l4 = {j: jnp.where(strict4, m[j][:C, :R], 0.0) for j in J}
    a_ak = {j: jnp.where(strict4, m[j][:C, R:], 0.0).astype(BF16) for j in J}
    a_rb = {j: jnp.where(lower4, m[j][C:, :R], 0.0).astype(BF16) for j in J}
    a_rk = {j: jnp.where(lower4, m[j][C:, R:], 0.0).astype(BF16) for j in J}
    def block_diag(x4):
        return jnp.where(same, rows4(x4.astype(BF16)), jnp.zeros((), BF16))

    t4 = {j: eye4 + l4[j] for j in J}
    l_bd = {j: block_diag(l4[j]) for j in J}
    for _ in range(n_lev):
        l4 = {j: _dot(l4[j].astype(BF16), l_bd[j]) for j in J}
        l_bd = {j: block_diag(l4[j]) for j in J}
        t4 = {j: t4[j] + _dot(t4[j].astype(BF16), l_bd[j]) for j in J}
    t_b = {j: t4[j].astype(BF16) for j in J}
    bke_t = {j: _dot_nt(eye_g_bf, jnp.concatenate([be4[j], ke4[j]], axis=0)).astype(BF16) for j in J}

    G = range(n_state)
    h_cur = [h_scr[q] for q in G]
    for ci in range(n_sub):
        rows = slice(ci * C, (ci + 1) * C)
        h0_b = [h_cur[g].astype(BF16) for g in G]
        x4 = [_dot(a4[ci, g], h0_b[g]) + _dot(a_ak[ci, g], v_s[ci, g]) for g in G]
        u4 = [_dot(t_b[ci, g], stack(x4[g].astype(BF16))).astype(BF16) for g in G]
        y4 = [_dot(r4[ci, g], h0_b[g]) + _dot(a_rb[ci, g], stack(u4[g])) + _dot(a_rk[ci, g], v_s[ci, g])
              for g in G]
        uv4 = [jnp.concatenate([u4[g], v4[ci, g]], axis=0) for g in G]
        h_add = [jnp.where(same_head, _dot(bke_t[ci, g], uv4[g]), 0.0) for g in G]
        for q in G:
            p, g = divmod(q, n_grp)
            y_ref[p, rows, sls[g]] = y4[q]
            p_col = jnp.sum(jnp.where(eye_g, p_end[p, ci][:, sls[g]], 0.0), axis=1, keepdims=True)
            h_cur[q] = p_col * h_cur[q] + h_add[q]

    for q in G:
        h_scr[q] = h_cur[q]

    @pl.when(c == pl.num_programs(1) - 1)
    def _():
        for p in range(n_par):
            for hd in range(RWKV_HEADS):
                g, hh = divmod(hd, WKV_GROUP)
                hT_ref[p, hd] = transposed(h_scr[p * n_grp + g, head_block(hh), head_block(hh)])


def _wkv_call(arrs, h0, row0, n_seq, n_chunk, C, n_sub, n_par):
    rows = C * n_sub
    steps = n_chunk // n_sub
    blk0 = row0 // rows
    GW = WKV_GROUP * RWKV_N
    n_grp = RWKV_HEADS // WKV_GROUP
    assert n_seq % n_par == 0 and n_chunk % n_sub == 0
    tok = [pl.BlockSpec((rows, RWKV_DIM), lambda b, c, p=p: (blk0 + (b * n_par + p) * steps + c, 0))
           for p in range(n_par)]
    st = pl.BlockSpec((n_par, RWKV_HEADS, RWKV_N, RWKV_N), lambda b, c: (b, 0, 0, 0))
    y, h_fin = pl.pallas_call(
        functools.partial(_wkv_kernel, C=C, n_sub=n_sub, n_par=n_par),
        grid=(n_seq // n_par, steps),
        in_specs=[tok[p] for p in range(n_par) for _ in range(6)] + [st],
        out_specs=[pl.BlockSpec((n_par, rows, RWKV_DIM), lambda b, c: (b, c, 0)), st],
        out_shape=[
            jax.ShapeDtypeStruct((n_seq, n_chunk * C, RWKV_DIM), F32),
            jax.ShapeDtypeStruct((n_seq, RWKV_HEADS, RWKV_N, RWKV_N), F32),
        ],
        scratch_shapes=[pltpu.VMEM((n_par * n_grp, GW, GW), F32)],
        compiler_params=_cparams(("parallel", "arbitrary")),
        name="wkv_c%d" % C,
    )(*(list(arrs) * n_par), h0)
    return y.reshape(n_seq * n_chunk * C, RWKV_DIM), h_fin


def _mix_kernel(xp_ref, xs_ref, attnp_ref, attns_ref, yp_ref, ys_ref, bonus_ref, g_ref, seg_ref, lng_ref,
                lnb_ref, woa_ref, wob_ref, g1_ref, b1_ref, wr_ref, br_ref, tri_ref, h_ref, hpk_ref, route_ref,
                count_ref, *, n_p, tm):
    is_p = pl.program_id(0) < n_p
    P = range(MIX_PARTS)
    bands = [slice(k * tm // MIX_PARTS, (k + 1) * tm // MIX_PARTS) for k in P]
    seg = seg_ref[...]
    inv_n = 1.0 / RWKV_N
    y = [jnp.where(is_p, yp_ref[r, :], ys_ref[r, :]) for r in bands]
    yc = [y[k] - _dot_exact_rhs(y[k], seg) * inv_n for k in P]
    var = [_dot_exact_rhs(yc[k] * yc[k], seg) * inv_n for k in P]
    yn = [yc[k] * lax.rsqrt(var[k] + GN_EPS) * lng_ref[...] + lnb_ref[...] for k in P]
    rw = [((yn[k] + bonus_ref[bands[k], :]) * g_ref[bands[k], :]).astype(BF16) for k in P]
    attn = [jnp.where(is_p, attnp_ref[r, :], attns_ref[r, :]) for r in bands]
    m = [_dot(attn[k], woa_ref[...]) + _dot(rw[k], wob_ref[...]) for k in P]
    x = [jnp.where(is_p, xp_ref[r, :], xs_ref[r, :]) for r in bands]
    h = [_layer_norm(DN_ALPHA * x[k] + m[k], g1_ref[...], b1_ref[...]) for k in P]
    for k in P:
        h_ref[bands[k], :] = h[k]
        hpk_ref[bands[k], :] = _pack_bf16_pairs(h[k])

    h_hi = [h[k].astype(BF16) for k in P]
    h_lo = [(h[k] - h_hi[k].astype(F32)).astype(BF16) for k in P]
    logits = [_dot(h_hi[k], wr_ref[0]) + _dot(h_lo[k], wr_ref[0]) + _dot(h_hi[k], wr_ref[1]) + br_ref[...]
              for k in P]
    lane = lax.broadcasted_iota(jnp.int32, logits[0].shape, 1)
    big = jnp.int32(LANE)

    def route_band(lg):
        gl = jnp.where(lane < N_GROUPS, lg, NEG_INF)
        gmax = jnp.max(gl, axis=1, keepdims=True)
        grp = jnp.min(jnp.where(gl == gmax, lane, big), axis=1, keepdims=True)
        p_grp = 1.0 / jnp.sum(jnp.exp(gl - gmax), axis=1, keepdims=True)
        e_idx = lane - N_GROUPS
        in_grp = (lane >= N_GROUPS) & (lane < N_GROUPS + N_EXPERTS) & ((e_idx // EXPERTS_PER_GROUP) == grp)
        el = jnp.where(in_grp, lg, NEG_INF)
        m1 = jnp.max(el, axis=1, keepdims=True)
        i1 = jnp.min(jnp.where(el == m1, lane, big), axis=1, keepdims=True)
        el2 = jnp.where(lane == i1, NEG_INF, el)
        m2 = jnp.max(el2, axis=1, keepdims=True)
        i2 = jnp.min(jnp.where(el2 == m2, lane, big), axis=1, keepdims=True)
        t = jnp.exp(m2 - m1)
        g1 = p_grp / (1.0 + t)
        return i1, i2, g1, g1 * t

    routed = [route_band(logits[k]) for k in P]
    chosen = jnp.concatenate(
        [jnp.where(lane == routed[k][0], 1.0, jnp.where(lane == routed[k][1], 1.0, 0.0)) for k in P], axis=0)
    before = _dot(tri_ref[...], chosen.astype(BF16))
    count_ref[0] = jnp.sum(chosen, axis=0, keepdims=True)
    for k in P:
        i1, i2, g1, g2 = routed[k]
        bef = before[bands[k]]
        r1 = jnp.sum(jnp.where(lane == i1, bef, 0.0), axis=1, keepdims=True)
        r2 = jnp.sum(jnp.where(lane == i2, bef, 0.0), axis=1, keepdims=True)
        cols = ((i1 - N_GROUPS).astype(F32), (i2 - N_GROUPS).astype(F32), g1, g2, r1, r2)
        route = jnp.zeros(lane.shape, F32)
        for j, col in enumerate(cols):
            route = jnp.where(lane == j, col, route)
        route_ref[bands[k], :] = route


def _mix_call(xp, xs, attn_p, attn_s, y_p, y_s, bonus, g, seg, lnx_g, lnx_b, woa, wob, ln1_g, ln1_b, wr, br,
              tile0, n_tiles):
    tm = TOKEN_TILE
    T = n_tiles * tm
    n_p = xp.shape[0] // tm
    row = lambda i: (i, 0)
    full = lambda i: (0, 0)
    row_p, row_s = _split_rows(n_p, tile0)
    half = pl.BlockSpec((tm, RWKV_DIM), lambda i: (i + tile0, 0))
    vec5 = pl.BlockSpec((1, RWKV_DIM), full)
    vec10 = pl.BlockSpec((1, D_MODEL), full)
    idx = np.arange(tm)
    tri = jnp.asarray((idx[None, :] < idx[:, None]).astype(np.float32)).astype(BF16)
    return pl.pallas_call(
        functools.partial(_mix_kernel, n_p=n_p - tile0, tm=tm),
        grid=(n_tiles,),
        in_specs=[
            pl.BlockSpec((tm, D_MODEL), row_p), pl.BlockSpec((tm, D_MODEL), row_s),
            pl.BlockSpec((tm, RWKV_DIM), row_p), pl.BlockSpec((tm, RWKV_DIM), row_s),
            pl.BlockSpec((tm, RWKV_DIM), row_p), pl.BlockSpec((tm, RWKV_DIM), row_s),
            half, half,
            pl.BlockSpec((RWKV_DIM, RWKV_DIM), full), vec5, vec5,
            pl.BlockSpec((RWKV_DIM, D_MODEL), full), pl.BlockSpec((RWKV_DIM, D_MODEL), full),
            vec10, vec10,
            pl.BlockSpec((2, D_MODEL, LANE), lambda i: (0, 0, 0)), pl.BlockSpec((1, LANE), full),
            pl.BlockSpec((tm, tm), full),
        ],
        out_specs=[pl.BlockSpec((tm, D_MODEL), row), pl.BlockSpec((tm, D_MODEL // 2), row),
                   pl.BlockSpec((tm, LANE), row), pl.BlockSpec((1, 1, LANE), lambda i: (i, 0, 0))],
        out_shape=[jax.ShapeDtypeStruct((T, D_MODEL), F32), jax.ShapeDtypeStruct((T, D_MODEL // 2), jnp.int32),
                   jax.ShapeDtypeStruct((T, LANE), F32), jax.ShapeDtypeStruct((T // tm, 1, LANE), F32)],
        compiler_params=_cparams(("parallel",)),
        name="mix",
    )(xp, xs, attn_p, attn_s, y_p, y_s, bonus, g, seg, lnx_g, lnx_b, woa, wob, ln1_g, ln1_b, wr, br, tri)


def _expert_kernel(be_ref, nu_ref, xs_ref, wg_ref, wu_ref, wd_ref, ys_ref, wgu_b, wd_b):
    i = pl.program_id(0)

    @pl.when((i == 0) | (be_ref[i] != be_ref[jnp.maximum(i - 1, 0)]))
    def _():
        wgu_b[:, :D_EXPERT] = wg_ref[0].astype(BF16)
        wgu_b[:, D_EXPERT:] = wu_ref[0].astype(BF16)
        wd_b[...] = wd_ref[0].astype(BF16)

    @pl.when(i < nu_ref[0])
    def _():
        P = range(2)
        bands = [slice(k * MOE_BLK // 2, (k + 1) * MOE_BLK // 2) for k in P]
        xb = [_unpack_bf16_pairs(xs_ref[r, :]).astype(BF16) for r in bands]
        gu = [_dot(xb[k], wgu_b[...]) for k in P]
        act = [(gu[k][:, :D_EXPERT] * _sigmoid(gu[k][:, :D_EXPERT]) * gu[k][:, D_EXPERT:]).astype(BF16)
               for k in P]
        out = [_dot(act[k], wd_b[...]) for k in P]
        for k in P:
            ys_ref[bands[k], :] = _pack_bf16_pairs(out[k])

    @pl.when(i >= nu_ref[0])
    def _():
        ys_ref[...] = jnp.zeros(ys_ref.shape, ys_ref.dtype)


def _expert_call(block_e, n_used, xs, wg, wu, wd):
    n_blk = xs.shape[0] // MOE_BLK
    grid_spec = pltpu.PrefetchScalarGridSpec(
        num_scalar_prefetch=2,
        grid=(n_blk,),
        in_specs=[
            pl.BlockSpec((MOE_BLK, D_MODEL // 2), lambda i, be, nu: (i, 0)),
            pl.BlockSpec((1, D_MODEL, D_EXPERT), lambda i, be, nu: (be[i], 0, 0)),
            pl.BlockSpec((1, D_MODEL, D_EXPERT), lambda i, be, nu: (be[i], 0, 0)),
            pl.BlockSpec((1, D_EXPERT, D_MODEL), lambda i, be, nu: (be[i], 0, 0)),
        ],
        out_specs=pl.BlockSpec((MOE_BLK, D_MODEL // 2), lambda i, be, nu: (i, 0)),
        scratch_shapes=[pltpu.VMEM((D_MODEL, 2 * D_EXPERT), BF16), pltpu.VMEM((D_EXPERT, D_MODEL), BF16)],
    )
    return pl.pallas_call(
        _expert_kernel,
        grid_spec=grid_spec,
        out_shape=jax.ShapeDtypeStruct((n_blk * MOE_BLK, D_MODEL // 2), jnp.int32),
        compiler_params=_cparams(("arbitrary",)),
        name="experts",
    )(block_e, n_used, xs, wg, wu, wd)


def _combine_kernel(h_ref, ya_ref, yb_ref, route_ref, g2_ref, b2_ref, *rest, n_p, has_prev, has_prompt,
                    has_sample):
    outs = list(rest[1:] if has_prev else rest)
    i = pl.program_id(0)
    route = route_ref[...]
    f = _unpack_bf16_pairs(ya_ref[...]) * route[:, 2:3] + _unpack_bf16_pairs(yb_ref[...]) * route[:, 3:4]
    out = _layer_norm(DN_ALPHA * h_ref[...] + f, g2_ref[...], b2_ref[...])

    if has_prompt:
        op_ref = outs.pop(0)

        @pl.when(i < n_p)
        def _():
            op_ref[...] = out

    if has_sample:
        os_ref = outs.pop(0)

        @pl.when(i >= n_p)
        def _():
            os_ref[...] = out


def _combine_call(h, yab, route, ln2_g, ln2_b, t_prompt, t_sample, tile0, out_p_prev):
    tm = TOKEN_TILE
    n_t, n_p = h.shape[0] // tm, t_prompt // tm
    has_prompt = tile0 < n_p
    has_sample = tile0 + n_t > n_p
    has_prev = has_prompt and out_p_prev is not None
    row = lambda i: (i, 0)
    full = lambda i: (0, 0)
    row_p, row_s = _split_rows(n_p, tile0)
    big = pl.BlockSpec((tm, D_MODEL), row)
    in_specs = [big, pl.BlockSpec((tm, D_MODEL // 2), row),
                pl.BlockSpec((tm, D_MODEL // 2), lambda i: (i + n_t, 0)), pl.BlockSpec((tm, LANE), row),
                pl.BlockSpec((1, D_MODEL), full), pl.BlockSpec((1, D_MODEL), full)]
    args = [h, yab, yab, route, ln2_g, ln2_b]
    out_specs, out_shape, aliases = [], [], {}
    if has_prompt:
        out_specs.append(pl.BlockSpec((tm, D_MODEL), row_p))
        out_shape.append(jax.ShapeDtypeStruct((t_prompt, D_MODEL), F32))
    if has_prev:
        in_specs.append(pl.BlockSpec(memory_space=pl.ANY))
        args.append(out_p_prev)
        aliases = {len(args) - 1: 0}
    if has_sample:
        out_specs.append(pl.BlockSpec((tm, D_MODEL), row_s))
        out_shape.append(jax.ShapeDtypeStruct((t_sample, D_MODEL), F32))
    outs = pl.pallas_call(
        functools.partial(_combine_kernel, n_p=n_p - tile0, has_prev=has_prev, has_prompt=has_prompt,
                          has_sample=has_sample),
        grid=(n_t,),
        in_specs=in_specs,
        out_specs=out_specs,
        out_shape=out_shape,
        input_output_aliases=aliases,
        compiler_params=_cparams(("arbitrary",)),
        name="combine",
    )(*args)
    out_p = outs[0] if has_prompt else out_p_prev
    out_s = outs[-1] if has_sample else None
    return out_p, out_s


def _prep_weights(w_in, w_uq, w_ukv):
    half = MLA_ROPE // 2
    kpe_w = w_in[:, Q_LORA + KV_LORA:MLA_IN]
    kpe_b = jnp.concatenate([-kpe_w[:, half:], kpe_w[:, :half]], axis=1)
    w1 = jnp.concatenate([w_in[:, :Q_LORA + KV_LORA], kpe_w, kpe_b,
                          jnp.zeros((D_MODEL, LANE - 2 * MLA_ROPE), F32), w_in[:, MLA_IN:]], axis=1).astype(BF16)
    pad_q = jnp.zeros((Q_LORA, MLA_HEADS, HEAD_PAD - MLA_NOPE - MLA_ROPE), F32)
    wqa = jnp.concatenate([w_uq, pad_q], axis=2).reshape(Q_LORA, -1).astype(BF16)
    w_uk, w_uv = w_ukv[:, :, :MLA_NOPE], w_ukv[:, :, MLA_NOPE:]
    wk = jnp.concatenate([w_uk, jnp.zeros((KV_LORA, MLA_HEADS, HEAD_PAD - MLA_NOPE), F32)], axis=2)
    wk = wk.reshape(KV_LORA, -1).astype(BF16)
    pk_np = np.zeros((MLA_ROPE, MLA_HEADS * HEAD_PAD), np.float32)
    for h in range(MLA_HEADS):
        for i in range(MLA_ROPE):
            pk_np[i, h * HEAD_PAD + MLA_NOPE + i] = 1.0
    pk = jnp.asarray(pk_np).astype(BF16)
    zv = jnp.zeros((KV_LORA, MLA_HEADS // 2, MLA_V), F32)
    wv4 = w_uv.reshape(KV_LORA, MLA_HEADS // 2, 2, MLA_V)
    wv = jnp.stack([jnp.concatenate([wv4[:, :, 0], zv], axis=2),
                    jnp.concatenate([zv, wv4[:, :, 1]], axis=2)], axis=2)
    wv = wv.reshape(KV_LORA, -1).astype(BF16)
    wuk = jnp.transpose(w_uk, (1, 2, 0)).astype(BF16)
    wuv_np = np.zeros((MLA_HEADS, MLA_HEADS * MLA_V), np.float32)
    for h in range(MLA_HEADS):
        wuv_np[h, h * MLA_V:(h + 1) * MLA_V] = 1.0
    wuv = jnp.transpose(w_uv, (1, 0, 2))
    wuv = (jnp.tile(wuv, (1, 1, MLA_HEADS)) * jnp.asarray(wuv_np)[:, None, :]).astype(BF16)
    return w1, wqa, wk, pk, wv, wuk, wuv


def _rope_table(pos):
    inv = ROPE_BASE ** (-jnp.arange(0, MLA_ROPE, 2, dtype=F32) / MLA_ROPE)
    ang = pos.astype(F32)[:, None] * inv[None, :]
    return jnp.concatenate([jnp.cos(ang), jnp.sin(ang)], axis=1)


def _rope_placement():
    half = MLA_ROPE // 2
    place = np.zeros((MLA_ROPE, 4 * LANE), np.float32)
    rows = np.zeros((2, 4 * LANE), np.float32)
    for i in range(half):
        c, s = i, half + i
        place[c, MLA_NOPE + i] = place[c, MLA_NOPE + half + i] = 1.0
        place[s, LANE + MLA_NOPE + i] = -1.0
        place[s, 2 * LANE + MLA_NOPE + half + i] = 1.0
        place[c, 3 * LANE + i] = place[c, 3 * LANE + half + i] = 1.0
        place[s, 3 * LANE + 2 * half + i] = place[s, 3 * LANE + 3 * half + i] = 1.0
    rows[0, :MLA_NOPE] = 1.0
    rows[1, :3 * LANE] = MLA_SCALE * LOG2E
    rows[1, 3 * LANE:] = 1.0
    return jnp.asarray(place).astype(BF16), jnp.asarray(rows)


def _seg_ones():
    idx = np.arange(RWKV_DIM) // RWKV_N
    return jnp.asarray((idx[:, None] == idx[None, :]).astype(np.float32)).astype(BF16)


def _dispatch(route, tile_counts, t_total):
    A = t_total * TOP_K
    n_tiles = tile_counts.shape[0]
    counts_te = tile_counts[:, 0, N_GROUPS:N_GROUPS + N_EXPERTS].astype(jnp.int32)
    counts = jnp.sum(counts_te, axis=0)
    blocks_per_e = (counts + MOE_BLK - 1) // MOE_BLK
    blk_end = jnp.cumsum(blocks_per_e)
    blk_start = blk_end - blocks_per_e
    tile_off = jnp.cumsum(counts_te, axis=0) - counts_te
    base = blk_start[None, :] * MOE_BLK + tile_off
    base_tok = jnp.repeat(base, t_total // n_tiles, axis=0)
    experts = jnp.arange(N_EXPERTS, dtype=jnp.int32)[None, :]
    dest = []
    for k in range(TOP_K):
        e_k = route[:, k].astype(jnp.int32)
        rank_k = route[:, 4 + k].astype(jnp.int32)
        dest.append(jnp.sum(jnp.where(e_k[:, None] == experts, base_tok, 0), axis=-1) + rank_k)
    n_blk = -(-A // MOE_BLK) + N_EXPERTS
    blk = jnp.arange(n_blk, dtype=jnp.int32)
    block_e = jnp.minimum(jnp.sum((blk[:, None] >= blk_end[None, :]).astype(jnp.int32), axis=1),
                          N_EXPERTS - 1).astype(jnp.int32)
    n_used = blk_end[-1:].astype(jnp.int32)
    return dest, block_e, n_used


def _sc_gather_rows(table, idx):
    n_rows, width = idx.shape[0], table.shape[1]
    n_workers = SC_CORES * SC_SUBCORES
    per_worker = n_rows // n_workers
    assert n_rows % n_workers == 0 and per_worker % SC_WINDOW == 0
    mesh = plsc.VectorSubcoreMesh(core_axis_name="c", subcore_axis_name="s")

    @functools.partial(
        pl.kernel, mesh=mesh,
        out_type=jax.ShapeDtypeStruct((n_rows, width), table.dtype),
        scratch_types=[
            pltpu.VMEM((SC_WINDOW,), jnp.int32),
            pltpu.VMEM((SC_WINDOW, width), table.dtype),
            pltpu.SemaphoreType.DMA,
        ],
    )
    def gather(table_hbm, idx_hbm, out_hbm, idx_v, rows_v, sem):
        wid = lax.axis_index("s") * SC_CORES + lax.axis_index("c")
        base = wid * per_worker

        @pl.loop(0, per_worker // SC_WINDOW)
        def _(w):
            off = pl.multiple_of(base + w * SC_WINDOW, SC_WINDOW)
            pltpu.sync_copy(idx_hbm.at[pl.ds(off, SC_WINDOW)], idx_v)
            pltpu.async_copy(table_hbm.at[idx_v], rows_v, sem).wait()
            pltpu.sync_copy(rows_v, out_hbm.at[pl.ds(off, SC_WINDOW)])

    return gather(table, idx)


def _sc_scatter_rows(src, idx_a, idx_b, n_slots):
    n_rows, width = src.shape
    n_workers = SC_CORES * SC_SUBCORES
    per_worker = n_rows // n_workers
    assert n_rows % n_workers == 0 and per_worker % SC_WINDOW == 0
    n_win = per_worker // SC_WINDOW
    mesh = plsc.VectorSubcoreMesh(core_axis_name="c", subcore_axis_name="s")

    @functools.partial(
        pl.kernel, mesh=mesh,
        out_type=jax.ShapeDtypeStruct((n_slots, width), src.dtype),
        scratch_types=[
            pltpu.VMEM((1, SC_WINDOW), jnp.int32),
            pltpu.VMEM((1, SC_WINDOW), jnp.int32),
            pltpu.VMEM((SC_WINDOW, width), src.dtype),
        ],
    )
    def scatter(src_hbm, ia_hbm, ib_hbm, out_hbm, ia_v, ib_v, rows_v):
        wid = lax.axis_index("s") * SC_CORES + lax.axis_index("c")

        @pl.loop(0, n_win)
        def _(w):
            win = wid * n_win + w
            off = pl.multiple_of(win * SC_WINDOW, SC_WINDOW)
            pltpu.sync_copy(src_hbm.at[pl.ds(off, SC_WINDOW)], rows_v)
            pltpu.sync_copy(ia_hbm.at[pl.ds(win, 1)], ia_v)
            pltpu.sync_copy(ib_hbm.at[pl.ds(win, 1)], ib_v)
            pltpu.sync_copy(rows_v, out_hbm.at[ia_v.at[0]])
            pltpu.sync_copy(rows_v, out_hbm.at[ib_v.at[0]])

    return scatter(src, idx_a, idx_b)


def kernel(x_prompt, x_sample, cache_ckv, cache_kpe, state_wkv, state_shift, w_in, q_norm_g, kv_norm_g, w_uq,
           w_ukv, mu_shift, w0, w2, a0, a2, g2, k_k, k_a, r_k, lnx_g, lnx_b, w_o, ln1_g, ln1_b, w_gr, b_gr,
           w_er, b_er, w_eg, w_eu, w_ed, ln2_g, ln2_b):
    B, S, D = x_prompt.shape
    DB, DS, _ = x_sample.shape
    past = cache_ckv.shape[2]
    Tp, Ts = B * S, DB * DS
    T = Tp + Ts
    assert D == D_MODEL and DS == SHIFT_GROUP and S % ATTN_TQ == 0 and S % (CHUNK * WKV_SUB) == 0
    assert Tp % TOKEN_TILE == 0 and T % TOKEN_TILE == 0 and w_in.shape[0] == DEPTH

    l = 0
    xp, xs_in = x_prompt.reshape(Tp, D), x_sample.reshape(Ts, D)
    w1, wqa, wk, pk, wv, wuk, wuv = _prep_weights(w_in[l], w_uq[l], w_ukv[l])
    pos = jnp.concatenate([jnp.arange(S, dtype=jnp.int32),
                           jnp.tile(past + jnp.arange(DS, dtype=jnp.int32), TOKEN_TILE // DS)])
    rope = _rope_table(pos)

    seg = _seg_ones()
    vec = lambda a: a.reshape(1, -1)
    rwkv_w = (vec(mu_shift[l]), vec(w0[l]), vec(a0[l]), vec(k_k[l]), vec(k_a[l]), vec(r_k[l]),
              w2[l].astype(BF16), a2[l].astype(BF16), g2[l].astype(BF16), seg)
    (q, kcat, vcat, ckv_p, ckv_s, kpe_p, kpe_s, last_rows,
     r, lw, kh, v, na, b, bonus, g) = _proj_call(
        xp, xs_in, rope, S // TOKEN_TILE, *_rope_placement(), w1, q_norm_g[l][None], kv_norm_g[l][None],
        wqa, wk, pk, wv, state_shift[l][:, None, :], rwkv_w)

    attn_p = _attn_call(q, kcat, vcat, B, S)
    attn_s = _mla_sample_call(q, ckv_s, kpe_s, cache_ckv[l], jnp.swapaxes(cache_kpe[l], 1, 2), wuk, wuv,
                              Tp, DB, DS)

    scan_in = (r, lw, kh, v, na, b)
    h0_p = jnp.zeros((B, RWKV_HEADS, RWKV_N, RWKV_N), F32)
    y_p, hT_p = _wkv_call(scan_in, h0_p, 0, B, S // CHUNK, CHUNK, WKV_SUB, math.gcd(B, WKV_PAR))
    h0_s = state_wkv[l]
    y_s, hT_s = _wkv_call(scan_in, h0_s, Tp, DB, 1, DS, 1, math.gcd(DB, WKV_SUB * WKV_PAR))

    wo_b = w_o[l].astype(BF16)
    wr = jnp.concatenate([w_gr[l], w_er[l], jnp.zeros((D, LANE - N_GROUPS - N_EXPERTS), F32)], axis=1)
    wr_hi = wr.astype(BF16)
    wr_lo = (wr - wr_hi.astype(F32)).astype(BF16)
    br = jnp.concatenate([b_gr[l], b_er[l], jnp.zeros((LANE - N_GROUPS - N_EXPERTS,), F32)])[None]
    n_tiles = T // TOKEN_TILE
    wave_tiles = [(t0, min(t0 + MOE_WAVE_TILES, n_tiles) - t0) for t0 in range(0, n_tiles, MOE_WAVE_TILES)]
    out_p, out_s = None, None
    for tile0, nt in wave_tiles:
        t_w = nt * TOKEN_TILE
        h, hpk, route, tile_counts = _mix_call(
            xp, xs_in, attn_p, attn_s, y_p, y_s, bonus, g, seg, vec(lnx_g[l]), vec(lnx_b[l]),
            wo_b[:MLA_HEADS * MLA_V], wo_b[MLA_HEADS * MLA_V:], vec(ln1_g[l]), vec(ln1_b[l]),
            jnp.stack([wr_hi, wr_lo]), br, tile0, nt)
        dest, block_e, n_used = _dispatch(route, tile_counts, t_w)
        win = lambda a: a.reshape(t_w // SC_WINDOW, SC_WINDOW)
        xs = _sc_scatter_rows(hpk, win(dest[0]), win(dest[1]), block_e.shape[0] * MOE_BLK)
        ys = _expert_call(block_e, n_used, xs, w_eg[l], w_eu[l], w_ed[l])
        yab = _sc_gather_rows(ys, jnp.concatenate(dest))
        out_p, wave_s = _combine_call(h, yab, route, vec(ln2_g[l]), vec(ln2_b[l]), Tp, Ts, tile0, out_p)
        assert wave_s is None or out_s is None, "the sample tiles must fall inside one wave"
        out_s = wave_s if wave_s is not None else out_s

    y_prompt = out_p.reshape(B, S, D)
    y_sample = out_s.reshape(DB, DS, D)
    p_ckv = ckv_p.reshape(1, B, S, KV_LORA)
    p_kpe = jnp.transpose(kpe_p.reshape(MLA_ROPE, B, S), (1, 2, 0))[None]
    s_ckv = ckv_s.reshape(1, DB, DS, KV_LORA)
    s_kpe = kpe_s.reshape(1, DB, DS, MLA_ROPE)
    p_wkv = hT_p[None]
    s_wkv = hT_s[None]
    gp = S // SHIFT_GROUP
    p_sh = last_rows[gp - 1:B * gp:gp][None]
    s_sh = last_rows[B * gp:][None]
    return (y_prompt, y_sample, p_ckv, p_kpe, p_wkv, p_sh, s_ckv, s_kpe, s_wkv, s_sh)
```

```python
import functools
import math

import numpy as np
import jax
import jax.numpy as jnp
from jax import lax
from jax.experimental import pallas as pl
from jax.experimental.pallas import tpu as pltpu
from jax.experimental.pallas import tpu_sc as plsc

F32 = jnp.float32
BF16 = jnp.bfloat16

D_MODEL = 1024
CHUNK = 64
MLA_HEADS = 8
MLA_NOPE = 64
MLA_ROPE = 32
MLA_V = 64
Q_LORA = 384
KV_LORA = 256
ROPE_BASE = 10000.0
MLA_IN = Q_LORA + KV_LORA + MLA_ROPE
MLA_SCALE = (MLA_NOPE + MLA_ROPE) ** -0.5
RWKV_HEADS = 8
RWKV_N = 64
RWKV_DIM = RWKV_HEADS * RWKV_N
DECAY_LORA = 64
AAA_LORA = 64
GATE_LORA = 128
RWKV_IN = 3 * RWKV_DIM + DECAY_LORA + AAA_LORA + GATE_LORA
N_GROUPS = 4
EXPERTS_PER_GROUP = 8
N_EXPERTS = N_GROUPS * EXPERTS_PER_GROUP
TOP_K = 2
D_EXPERT = 256
MOE_BLK = 256
LN_EPS = 1e-5
RMS_EPS = 1e-6
GN_EPS = 64e-5
NEG_INF = -1e30
DEPTH = 1
DN_ALPHA = (2 * DEPTH) ** 0.25

LANE = 128
HEAD_PAD = 128
PROJ_KPE = Q_LORA + KV_LORA
PROJ_PR = PROJ_KPE + LANE
PROJ_W = PROJ_PR + RWKV_IN
SHIFT_GROUP = 32
TOKEN_TILE = 512
ATTN_TQ = 1024
ATTN_TK = 512
V_ONE_LANE = (MLA_V, 0)
LOG2E = math.log2(math.e)
VMEM_LIMIT = 48 * 1024 * 1024
PROJ_VMEM_LIMIT = 56 * 1024 * 1024
SC_CORES = 2
SC_SUBCORES = 16
SC_WINDOW = 32
WKV_GROUP = 4
WKV_SUB = 2
WKV_PAR = 2
MIX_PARTS = 2
MOE_WAVE_TILES = 34


def _cparams(sem):
    return pltpu.CompilerParams(dimension_semantics=sem, vmem_limit_bytes=VMEM_LIMIT)


def _split3(x):
    hi = x.astype(BF16)
    r1 = x - hi.astype(F32)
    mid = r1.astype(BF16)
    lo = (r1 - mid.astype(F32)).astype(BF16)
    return hi, mid, lo


def _dot(a, b):
    return jnp.dot(a, b, preferred_element_type=F32)


def _dot_nt(a, b):
    return lax.dot_general(a, b, (((1,), (1,)), ((), ())), preferred_element_type=F32)


def _dot_exact_rhs(x, w):
    hi = x.astype(BF16)
    lo = (x - hi.astype(F32)).astype(BF16)
    return _dot(hi, w) + _dot(lo, w)


def _dot_exact_lhs(w, x):
    hi, mid, lo = _split3(x)
    return _dot(w, hi) + _dot(w, mid) + _dot(w, lo)


def _pack_bf16_pairs(x):
    n = x.shape[1] // 2
    bits = pltpu.bitcast(x.astype(BF16).astype(F32), jnp.int32)
    return (bits[:, :n] & jnp.int32(-65536)) | lax.shift_right_logical(bits[:, n:], jnp.int32(16))


def _unpack_bf16_pairs(p):
    hi = pltpu.bitcast(p & jnp.int32(-65536), F32)
    lo = pltpu.bitcast(lax.shift_left(p, jnp.int32(16)), F32)
    return jnp.concatenate([hi, lo], axis=1)


def _sigmoid(x):
    return 1.0 / (1.0 + jnp.exp(-x))


def _layer_norm(x, g, b):
    xc = x - jnp.mean(x, -1, keepdims=True)
    var = jnp.mean(xc * xc, -1, keepdims=True)
    return xc * lax.rsqrt(var + LN_EPS) * g + b


def _proj_kernel(xp_ref, xs_ref, rope_ref, rp_ref, rc_ref, w1_ref, gq_ref, gkv_ref, wqa_ref, wk_ref, pk_ref, wv_ref,
                 shift_ref, *rest, n_p, tm, seq_tiles):
    rwkv_w = rest[:10]
    q_ref, k_ref, v_ref, ckvp_ref, ckvs_ref, kpep_ref, kpes_ref, last_ref = rest[10:18]
    rwkv_out = rest[18:26]
    carry_scr = rest[26]
    i = pl.program_id(0)
    is_p = i < n_p
    x = jnp.where(is_p, xp_ref[...], xs_ref[...]).astype(BF16)
    proj = _dot(x, w1_ref[...])
    c_q = proj[:, :Q_LORA]
    c_kv = proj[:, Q_LORA:Q_LORA + KV_LORA]
    kp = proj[:, PROJ_KPE:PROJ_PR]
    pr = proj[:, PROJ_PR:]

    ng = tm // SHIFT_GROUP
    pr3 = pr.reshape(ng, SHIFT_GROUP, RWKV_IN)
    last_ref[...] = pr3[:, SHIFT_GROUP - 1, :]
    rolled = pltpu.roll(pr, 1, 0).reshape(ng, SHIFT_GROUP, RWKV_IN)
    first_row = jnp.where(i % seq_tiles == 0, 0.0, carry_scr[...])
    bound = jnp.where(is_p, first_row[None], shift_ref[...])
    row_in_grp = lax.broadcasted_iota(jnp.int32, pr3.shape, 1)
    grp = lax.broadcasted_iota(jnp.int32, pr3.shape, 0)
    use_bound = (row_in_grp == 0) & (grp <= jnp.where(is_p, 0, ng))
    prev = jnp.where(use_bound, bound, rolled).reshape(tm, RWKV_IN)
    carry_scr[...] = pr[tm - 1:tm, :]

    cqn = c_q * lax.rsqrt(jnp.mean(c_q * c_q, -1, keepdims=True) + RMS_EPS) * gq_ref[...]
    ckv = c_kv * lax.rsqrt(jnp.mean(c_kv * c_kv, -1, keepdims=True) + RMS_EPS) * gkv_ref[...]

    rope = (_dot_exact_rhs(rope_ref[...], rp_ref[...]) + rc_ref[0:1, :]) * rc_ref[1:2, :]
    cq = rope[:, :LANE]
    sq_up = rope[:, LANE:2 * LANE]
    sq_dn = rope[:, 2 * LANE:3 * LANE]
    kt = rope[:, 3 * LANE:]
    prod = kp * kt
    kpe = prod[:, :MLA_ROPE] + prod[:, MLA_ROPE:2 * MLA_ROPE]

    qa = _dot(cqn.astype(BF16), wqa_ref[...])
    half = MLA_ROPE // 2
    for h in range(MLA_HEADS):
        qh = qa[:, h * HEAD_PAD:(h + 1) * HEAD_PAD]
        rot = pltpu.roll(qh, HEAD_PAD - half, 1) * sq_up + pltpu.roll(qh, half, 1) * sq_dn
        q_ref[:, h * HEAD_PAD:(h + 1) * HEAD_PAD] = (qh * cq + rot).astype(BF16)

    ckv_b = ckv.astype(BF16)
    k = _dot(ckv_b, wk_ref[...]) + _dot(kpe.astype(BF16), pk_ref[...])
    k_ref[...] = k.astype(BF16)
    lane = lax.broadcasted_iota(jnp.int32, (1, MLA_HEADS * HEAD_PAD), 1)
    odd = (lane // HEAD_PAD) % 2
    one_lane = jnp.where(odd == 1, V_ONE_LANE[1], V_ONE_LANE[0])
    v_one = jnp.where(lane % HEAD_PAD == one_lane, 1.0, 0.0)
    v_ref[...] = (_dot(ckv_b, wv_ref[...]) + v_one).astype(BF16)
    _rwkv_heads(pr, prev, *rwkv_w, *rwkv_out)

    @pl.when(is_p)
    def _():
        ckvp_ref[...] = ckv
        eye = jnp.where(lax.broadcasted_iota(jnp.int32, (MLA_ROPE, MLA_ROPE), 0)
                        == lax.broadcasted_iota(jnp.int32, (MLA_ROPE, MLA_ROPE), 1), 1.0, 0.0).astype(BF16)
        kpep_ref[...] = sum(_dot_nt(eye, part) for part in _split3(kpe))

    @pl.when(jnp.logical_not(is_p))
    def _():
        ckvs_ref[...] = ckv
        kpes_ref[...] = kpe


def _split_rows(n_p, tile0=0):
    return ((lambda i: (jnp.minimum(i + tile0, n_p - 1), 0)),
            (lambda i: (jnp.maximum(i + tile0 - n_p, 0), 0)))


def _proj_call(xp, xs, rope, seq_tiles, rope_place, rope_rows, w1, gq, gkv, wqa, wk, pk, wv, shift, rwkv_w):
    Tp, Ts = xp.shape[0], xs.shape[0]
    T = Tp + Ts
    tm = TOKEN_TILE
    n_p = Tp // tm
    rope_tiles = seq_tiles
    row = lambda i: (i, 0)
    full = lambda i: (0, 0)
    row_p, row_s = _split_rows(n_p)
    wide = MLA_HEADS * HEAD_PAD
    ng = tm // SHIFT_GROUP
    once = pl.Buffered(1)
    vec = pl.BlockSpec((1, RWKV_DIM), full, pipeline_mode=once)
    rwkv_specs = [pl.BlockSpec((1, RWKV_IN), full, pipeline_mode=once), vec, vec, vec, vec, vec,
                  pl.BlockSpec((DECAY_LORA, RWKV_DIM), full, pipeline_mode=once),
                  pl.BlockSpec((AAA_LORA, RWKV_DIM), full, pipeline_mode=once),
                  pl.BlockSpec((GATE_LORA, RWKV_DIM), full, pipeline_mode=once),
                  pl.BlockSpec((RWKV_DIM, RWKV_DIM), full, pipeline_mode=once)]
    tok = pl.BlockSpec((tm, RWKV_DIM), row)
    return pl.pallas_call(
        functools.partial(_proj_kernel, n_p=n_p, tm=tm, seq_tiles=seq_tiles),
        grid=(T // tm,),
        in_specs=[
            pl.BlockSpec((tm, D_MODEL), row_p),
            pl.BlockSpec((tm, D_MODEL), row_s),
            pl.BlockSpec((tm, MLA_ROPE), lambda i: (jnp.where(i < n_p, i % rope_tiles, rope_tiles), 0)),
            pl.BlockSpec((MLA_ROPE, 4 * LANE), full, pipeline_mode=once),
            pl.BlockSpec((2, 4 * LANE), full, pipeline_mode=once),
            pl.BlockSpec((D_MODEL, PROJ_W), full, pipeline_mode=once),
            pl.BlockSpec((1, Q_LORA), full, pipeline_mode=once),
            pl.BlockSpec((1, KV_LORA), full, pipeline_mode=once),
            pl.BlockSpec((Q_LORA, wide), full, pipeline_mode=once),
            pl.BlockSpec((KV_LORA, wide), full, pipeline_mode=once),
            pl.BlockSpec((MLA_ROPE, wide), full, pipeline_mode=once),
            pl.BlockSpec((KV_LORA, wide), full, pipeline_mode=once),
            pl.BlockSpec((ng, 1, RWKV_IN), lambda i: (jnp.maximum(i - n_p, 0), 0, 0)),
        ] + rwkv_specs,
        out_specs=[
            pl.BlockSpec((tm, wide), row),
            pl.BlockSpec((tm, wide), row),
            pl.BlockSpec((tm, wide), row),
            pl.BlockSpec((tm, KV_LORA), row_p),
            pl.BlockSpec((tm, KV_LORA), row_s),
            pl.BlockSpec((MLA_ROPE, tm), lambda i: (0, jnp.minimum(i, n_p - 1))),
            pl.BlockSpec((tm, MLA_ROPE), row_s),
            pl.BlockSpec((ng, RWKV_IN), row),
        ] + [tok] * 8,
        out_shape=[
            jax.ShapeDtypeStruct((T, wide), BF16),
            jax.ShapeDtypeStruct((T, wide), BF16),
            jax.ShapeDtypeStruct((T, wide), BF16),
            jax.ShapeDtypeStruct((Tp, KV_LORA), F32),
            jax.ShapeDtypeStruct((Ts, KV_LORA), F32),
            jax.ShapeDtypeStruct((MLA_ROPE, Tp), F32),
            jax.ShapeDtypeStruct((Ts, MLA_ROPE), F32),
            jax.ShapeDtypeStruct((T // SHIFT_GROUP, RWKV_IN), F32),
        ] + [jax.ShapeDtypeStruct((T, RWKV_DIM), F32)] * 8,
        scratch_shapes=[pltpu.VMEM((1, RWKV_IN), F32)],
        compiler_params=pltpu.CompilerParams(dimension_semantics=("arbitrary",), vmem_limit_bytes=PROJ_VMEM_LIMIT),
        name="proj",
    )(xp, xs, rope, rope_place, rope_rows, w1, gq, gkv, wqa, wk, pk, wv, shift, *rwkv_w)


def _attn_kernel(q_ref, k_ref, v_ref, o_ref, m_scr, acc_scr, *, tq, tk):
    qi = pl.program_id(2)
    m_scr[...] = jnp.full(m_scr.shape, NEG_INF, F32)
    acc_scr[...] = jnp.zeros(acc_scr.shape, F32)
    n_diag = tq // tk

    H = range(2)
    sls = [slice(h * HEAD_PAD, (h + 1) * HEAD_PAD) for h in H]

    def scores(k0, width, rows):
        return [_dot_nt(q_ref[rows, sl], k_ref[pl.ds(k0, width), sl]) for sl in sls]

    def accumulate(s, k0, width, rows):
        m_prev = [m_scr[h, rows, :] for h in H]
        m_new = [jnp.maximum(m_prev[h], jnp.max(s[h], axis=1, keepdims=True)) for h in H]
        pexp = [jnp.exp2(s[h] - jnp.tile(m_new[h], (1, width // LANE))).astype(BF16) for h in H]
        pv = [_dot(pexp[h], v_ref[pl.ds(k0, width), sls[h]]) for h in H]
        for h in H:
            acc_scr[h, rows, :] = jnp.exp2(m_prev[h] - m_new[h]) * acc_scr[h, rows, :] + pv[h]
            m_scr[h, rows, :] = m_new[h]

    def kv_block(k0, width, rows, masked):
        s = scores(k0, width, rows)
        if masked:
            n_rows = rows.stop - rows.start
            r = lax.broadcasted_iota(jnp.int32, (n_rows, width), 0) // CHUNK
            c = lax.broadcasted_iota(jnp.int32, (n_rows, width), 1) // CHUNK
            s = [jnp.where(c <= r, s[h], NEG_INF) for h in H]
        accumulate(s, k0, width, rows)

    all_rows = slice(0, tq)

    def pair(t, carry):
        k0a = pl.multiple_of(2 * t * tq, tq)
        k0b = pl.multiple_of(k0a + tq, tq)
        s_a = scores(k0a, tq, all_rows)
        s_b = scores(k0b, tq, all_rows)
        accumulate(s_a, k0a, tq, all_rows)
        accumulate(s_b, k0b, tq, all_rows)
        return carry

    lax.fori_loop(0, qi // 2, pair, 0)

    @pl.when(qi % 2 == 1)
    def _():
        kv_block(pl.multiple_of((qi - 1) * tq, tq), tq, all_rows, False)

    for d in range(n_diag):
        kv_block(pl.multiple_of(qi * tq + d * tk, tk), tk, slice(d * tk, tq), True)
    acc0, acc1 = acc_scr[0], acc_scr[1]
    lane = lax.broadcasted_iota(jnp.int32, acc0.shape, 1)
    l0 = acc0[:, V_ONE_LANE[0]:V_ONE_LANE[0] + 1]
    l1 = acc1[:, V_ONE_LANE[1]:V_ONE_LANE[1] + 1]
    o_ref[...] = jnp.where(lane < MLA_V, acc0 / l0, acc1 / l1).astype(o_ref.dtype)


def _attn_call(q, k, v, n_batch, seq):
    tq, tk = ATTN_TQ, ATTN_TK
    nq = seq // tq
    hp = MLA_HEADS // 2
    resident = pl.BlockSpec((seq, 2 * HEAD_PAD), lambda b, h, i: (b, h), pipeline_mode=pl.Buffered(1))
    return pl.pallas_call(
        functools.partial(_attn_kernel, tq=tq, tk=tk),
        grid=(n_batch, hp, nq),
        in_specs=[
            pl.BlockSpec((tq, 2 * HEAD_PAD), lambda b, h, i: (b * nq + i, h)),
            resident,
            resident,
        ],
        out_specs=pl.BlockSpec((tq, LANE), lambda b, h, i: (b * nq + i, h)),
        out_shape=jax.ShapeDtypeStruct((n_batch * seq, hp * LANE), BF16),
        scratch_shapes=[
            pltpu.VMEM((2, tq, LANE), F32),
            pltpu.VMEM((2, tq, LANE), F32),
        ],
        compiler_params=_cparams(("parallel", "parallel", "arbitrary")),
        name="attn",
    )(q, k, v)


def _mla_sample_kernel(q_ref, cn_ref, kn_ref, cp_ref, kp_ref, wuk_ref, wuv_ref, o_ref, *, past, dec):
    cp = cp_ref[0].astype(BF16)
    kp_t = kp_ref[0].astype(BF16)
    cn = cn_ref[...].astype(BF16)
    kn = kn_ref[...].astype(BF16)
    R = MLA_HEADS * dec
    qrow = (past + lax.broadcasted_iota(jnp.int32, (R, past), 0) % dec) // CHUNK
    vis_p = (lax.broadcasted_iota(jnp.int32, (R, past), 1) // CHUNK) <= qrow
    qrow_n = (past + lax.broadcasted_iota(jnp.int32, (R, dec), 0) % dec) // CHUNK
    vis_n = ((past + lax.broadcasted_iota(jnp.int32, (R, dec), 1)) // CHUNK) <= qrow_n
    q_lat, qp = [], []
    for h in range(MLA_HEADS):
        qn = q_ref[:, h * HEAD_PAD:h * HEAD_PAD + MLA_NOPE]
        qp.append(q_ref[:, h * HEAD_PAD + MLA_NOPE:h * HEAD_PAD + MLA_NOPE + MLA_ROPE])
        q_lat.append(_dot(qn, wuk_ref[h]).astype(BF16))
    q_lat = jnp.concatenate(q_lat, axis=0)
    qp = jnp.concatenate(qp, axis=0)
    s_p = jnp.where(vis_p, _dot_nt(q_lat, cp) + _dot(qp, kp_t), NEG_INF)
    s_n = jnp.where(vis_n, _dot_nt(q_lat, cn) + _dot_nt(qp, kn), NEG_INF)
    m = jnp.maximum(jnp.max(s_p, axis=1, keepdims=True), jnp.max(s_n, axis=1, keepdims=True))
    e_p = jnp.exp2(s_p - m)
    e_n = jnp.exp2(s_n - m)
    l = jnp.sum(e_p, axis=1, keepdims=True) + jnp.sum(e_n, axis=1, keepdims=True)
    o_lat = ((_dot(e_p.astype(BF16), cp) + _dot(e_n.astype(BF16), cn)) / l).astype(BF16)
    out = jnp.zeros((dec, MLA_HEADS * MLA_V), F32)
    for h in range(MLA_HEADS):
        out = out + _dot(o_lat[h * dec:(h + 1) * dec], wuv_ref[h])
    o_ref[...] = out.astype(o_ref.dtype)


def _mla_sample_call(q, ckv, kpe, cache_ckv, cache_kpe, wuk, wuv, row0, n_seq, dec):
    past = cache_ckv.shape[1]
    blk0 = row0 // dec
    wide = MLA_HEADS * HEAD_PAD
    return pl.pallas_call(
        functools.partial(_mla_sample_kernel, past=past, dec=dec),
        grid=(n_seq,),
        in_specs=[
            pl.BlockSpec((dec, wide), lambda b: (blk0 + b, 0)),
            pl.BlockSpec((dec, KV_LORA), lambda b: (b, 0)),
            pl.BlockSpec((dec, MLA_ROPE), lambda b: (b, 0)),
            pl.BlockSpec((1, past, KV_LORA), lambda b: (b, 0, 0)),
            pl.BlockSpec((1, MLA_ROPE, past), lambda b: (b, 0, 0)),
            pl.BlockSpec((MLA_HEADS, MLA_NOPE, KV_LORA), lambda b: (0, 0, 0)),
            pl.BlockSpec((MLA_HEADS, KV_LORA, MLA_HEADS * MLA_V), lambda b: (0, 0, 0)),
        ],
        out_specs=pl.BlockSpec((dec, MLA_HEADS * MLA_V), lambda b: (b, 0)),
        out_shape=jax.ShapeDtypeStruct((n_seq * dec, MLA_HEADS * MLA_V), BF16),
        compiler_params=_cparams(("parallel",)),
        name="mla_sample",
    )(q, ckv, kpe, cache_ckv, cache_kpe, wuk, wuv)


def _rwkv_heads(pr, prev, mu_ref, w0_ref, a0_ref, kk_ref, ka_ref, rk_ref, w2_ref, a2_ref, g2_ref,
                seg_ref, r_ref, lw_ref, kh_ref, v_ref, na_ref, b_ref, bonus_ref, g_ref):
    u = pr + mu_ref[...] * (prev - pr)
    o1, o2, o3 = RWKV_DIM, 2 * RWKV_DIM, 3 * RWKV_DIM
    o4, o5 = o3 + DECAY_LORA, o3 + DECAY_LORA + AAA_LORA
    r, k, v = u[:, :o1], u[:, o1:o2], u[:, o2:o3]
    w_lo, a_lo, g_lo = u[:, o3:o4], u[:, o4:o5], u[:, o5:]
    wl = w0_ref[...] + _dot(jnp.tanh(w_lo).astype(BF16), w2_ref[...])
    lw_ref[...] = -math.exp(-0.5) * _sigmoid(wl)
    a = _sigmoid(a0_ref[...] + _dot(a_lo.astype(BF16), a2_ref[...]))
    g_ref[...] = _dot(_sigmoid(g_lo).astype(BF16), g2_ref[...])
    seg = seg_ref[...]
    kk = k * kk_ref[...]
    kk = kk / jnp.maximum(jnp.sqrt(_dot_exact_rhs(kk * kk, seg)), 1e-12)
    kh = k * (1.0 + (a - 1.0) * ka_ref[...])
    r_ref[...] = r
    kh_ref[...] = kh
    v_ref[...] = v
    na_ref[...] = -kk
    b_ref[...] = kk * a
    bonus_ref[...] = _dot_exact_rhs(r * kh * rk_ref[...], seg) * v


def _wkv_kernel(*refs, C, n_sub, n_par):
    GW = WKV_GROUP * RWKV_N
    R = WKV_GROUP * C
    n_grp = RWKV_HEADS // WKV_GROUP
    n_lev = int(round(math.log2(C))) - 1
    c = pl.program_id(1)
    tok = [refs[6 * p:6 * p + 6] for p in range(n_par)]
    h0_ref, y_ref, hT_ref, h_scr = refs[6 * n_par:]
    n_state = n_par * n_grp

    def head_block(hh):
        return slice(hh * RWKV_N, (hh + 1) * RWKV_N)

    def transposed(x):
        n = x.shape[0]
        eye = jnp.where(lax.broadcasted_iota(jnp.int32, (n, n), 0) == lax.broadcasted_iota(jnp.int32, (n, n), 1),
                        1.0, 0.0).astype(BF16)
        return sum(_dot_nt(eye, part) for part in _split3(x))

    @pl.when(c == 0)
    def _():
        h_scr[...] = jnp.zeros(h_scr.shape, F32)
        for p in range(n_par):
            for hd in range(RWKV_HEADS):
                g, hh = divmod(hd, WKV_GROUP)
                h_scr[p * n_grp + g, head_block(hh), head_block(hh)] = transposed(h0_ref[p, hd])

    row = lax.broadcasted_iota(jnp.int32, (C, C), 0)
    col = lax.broadcasted_iota(jnp.int32, (C, C), 1)
    tri = jnp.where(col <= row, 1.0, 0.0).astype(BF16)
    rr = lax.broadcasted_iota(jnp.int32, (R, R), 0)
    cc = lax.broadcasted_iota(jnp.int32, (R, R), 1)
    same = (rr // C) == (cc // C)
    ti = lax.broadcasted_iota(jnp.int32, (C, R), 0)
    si = lax.broadcasted_iota(jnp.int32, (C, R), 1) % C
    strict4 = si < ti
    lower4 = si <= ti
    eye4 = jnp.where(si == ti, 1.0, 0.0)
    keep = (lax.broadcasted_iota(jnp.int32, (R, GW), 0) // C
            == lax.broadcasted_iota(jnp.int32, (R, GW), 1) // RWKV_N)
    gr = lax.broadcasted_iota(jnp.int32, (GW, GW), 0)
    gc = lax.broadcasted_iota(jnp.int32, (GW, GW), 1)
    eye_g = gr == gc
    same_head = (gr // RWKV_N) == (gc // RWKV_N)
    eye_g_bf = jnp.where(eye_g, 1.0, 0.0).astype(BF16)

    def rows4(x4):
        return jnp.concatenate([x4] * WKV_GROUP, axis=0)

    def stack(x4):
        return jnp.where(keep, rows4(x4), jnp.zeros((), x4.dtype))


    sls = [slice(g * GW, (g + 1) * GW) for g in range(n_grp)]
    J = [(ci, q) for ci in range(n_sub) for q in range(n_state)]
    ops, p_end = {}, {}
    for p, ci in [(p, ci) for p in range(n_par) for ci in range(n_sub)]:
        r_ref, lw_ref, k_ref, v_ref, a_ref, b_ref = tok[p]
        rows = slice(ci * C, (ci + 1) * C)
        lw = lw_ref[rows, :]
        cum = _dot_exact_lhs(tri, lw)
        cum_end = cum[C - 1:C, :]
        e_neg = jnp.exp(-cum)
        e_end = jnp.exp(cum_end - cum)
        b_in = b_ref[rows, :]
        k_in = k_ref[rows, :]
        full = ((a_ref[rows, :] * jnp.exp(cum - lw)).astype(BF16),
                (r_ref[rows, :] * jnp.exp(cum)).astype(BF16),
                (b_in * e_neg).astype(BF16), (k_in * e_neg).astype(BF16),
                (b_in * e_end).astype(BF16), (k_in * e_end).astype(BF16),
                v_ref[rows, :].astype(BF16))
        p_end[p, ci] = jnp.exp(cum_end)
        for g in range(n_grp):
            ops[ci, p * n_grp + g] = [t[:, sls[g]] for t in full]
    a4, r4, b4, k4, be4, ke4, v4 = [{j: ops[j][i] for j in J} for i in range(7)]
    v_s = {j: stack(v4[j]) for j in J}
    m = {j: _dot_nt(jnp.concatenate([a4[j], r4[j]], axis=0),
                    jnp.concatenate([stack(b4[j]), stack(k4[j])], axis=0)) for j in J}
    l4 = {j: jnp.where(strict4, m[j][:C, :R], 0.0) for j in J}
    a_ak = {j: jnp.where(strict4, m[j][:C, R:], 0.0).astype(BF16) for j in J}
    a_rb = {j: jnp.where(lower4, m[j][C:, :R], 0.0).astype(BF16) for j in J}
    a_rk = {j: jnp.where(lower4, m[j][C:, R:], 0.0).astype(BF16) for j in J}
    def block_diag(x4):
        return jnp.where(same, rows4(x4.astype(BF16)), jnp.zeros((), BF16))

    t4 = {j: eye4 + l4[j] for j in J}
    l_bd = {j: block_diag(l4[j]) for j in J}
    for _ in range(n_lev):
        l4 = {j: _dot(l4[j].astype(BF16), l_bd[j]) for j in J}
        l_bd = {j: block_diag(l4[j]) for j in J}
        t4 = {j: t4[j] + _dot(t4[j].astype(BF16), l_bd[j]) for j in J}
    t_b = {j: t4[j].astype(BF16) for j in J}
    bke_t = {j: _dot_nt(eye_g_bf, jnp.concatenate([be4[j], ke4[j]], axis=0)).astype(BF16) for j in J}

    G = range(n_state)
    h_cur = [h_scr[q] for q in G]
    for ci in range(n_sub):
        rows = slice(ci * C, (ci + 1) * C)
        h0_b = [h_cur[g].astype(BF16) for g in G]
        x4 = [_dot(a4[ci, g], h0_b[g]) + _dot(a_ak[ci, g], v_s[ci, g]) for g in G]
        u4 = [_dot(t_b[ci, g], stack(x4[g].astype(BF16))).astype(BF16) for g in G]
        y4 = [_dot(r4[ci, g], h0_b[g]) + _dot(a_rb[ci, g], stack(u4[g])) + _dot(a_rk[ci, g], v_s[ci, g])
              for g in G]
        uv4 = [jnp.concatenate([u4[g], v4[ci, g]], axis=0) for g in G]
        h_add = [jnp.where(same_head, _dot(bke_t[ci, g], uv4[g]), 0.0) for g in G]
        for q in G:
            p, g = divmod(q, n_grp)
            y_ref[p, rows, sls[g]] = y4[q]
            p_col = jnp.sum(jnp.where(eye_g, p_end[p, ci][:, sls[g]], 0.0), axis=1, keepdims=True)
            h_cur[q] = p_col * h_cur[q] + h_add[q]

    for q in G:
        h_scr[q] = h_cur[q]

    @pl.when(c == pl.num_programs(1) - 1)
    def _():
        for p in range(n_par):
            for hd in range(RWKV_HEADS):
                g, hh = divmod(hd, WKV_GROUP)
                hT_ref[p, hd] = transposed(h_scr[p * n_grp + g, head_block(hh), head_block(hh)])


def _wkv_call(arrs, h0, row0, n_seq, n_chunk, C, n_sub, n_par):
    rows = C * n_sub
    steps = n_chunk // n_sub
    blk0 = row0 // rows
    GW = WKV_GROUP * RWKV_N
    n_grp = RWKV_HEADS // WKV_GROUP
    assert n_seq % n_par == 0 and n_chunk % n_sub == 0
    tok = [pl.BlockSpec((rows, RWKV_DIM), lambda b, c, p=p: (blk0 + (b * n_par + p) * steps + c, 0))
           for p in range(n_par)]
    st = pl.BlockSpec((n_par, RWKV_HEADS, RWKV_N, RWKV_N), lambda b, c: (b, 0, 0, 0))
    y, h_fin = pl.pallas_call(
        functools.partial(_wkv_kernel, C=C, n_sub=n_sub, n_par=n_par),
        grid=(n_seq // n_par, steps),
        in_specs=[tok[p] for p in range(n_par) for _ in range(6)] + [st],
        out_specs=[pl.BlockSpec((n_par, rows, RWKV_DIM), lambda b, c: (b, c, 0)), st],
        out_shape=[
            jax.ShapeDtypeStruct((n_seq, n_chunk * C, RWKV_DIM), F32),
            jax.ShapeDtypeStruct((n_seq, RWKV_HEADS, RWKV_N, RWKV_N), F32),
        ],
        scratch_shapes=[pltpu.VMEM((n_par * n_grp, GW, GW), F32)],
        compiler_params=_cparams(("parallel", "arbitrary")),
        name="wkv_c%d" % C,
    )(*(list(arrs) * n_par), h0)
    return y.reshape(n_seq * n_chunk * C, RWKV_DIM), h_fin


def _mix_kernel(xp_ref, xs_ref, attnp_ref, attns_ref, yp_ref, ys_ref, bonus_ref, g_ref, seg_ref, lng_ref,
                lnb_ref, woa_ref, wob_ref, g1_ref, b1_ref, wr_ref, br_ref, tri_ref, h_ref, hpk_ref, route_ref,
                count_ref, *, n_p, tm):
    is_p = pl.program_id(0) < n_p
    P = range(MIX_PARTS)
    bands = [slice(k * tm // MIX_PARTS, (k + 1) * tm // MIX_PARTS) for k in P]
    seg = seg_ref[...]
    inv_n = 1.0 / RWKV_N
    y = [jnp.where(is_p, yp_ref[r, :], ys_ref[r, :]) for r in bands]
    yc = [y[k] - _dot_exact_rhs(y[k], seg) * inv_n for k in P]
    var = [_dot_exact_rhs(yc[k] * yc[k], seg) * inv_n for k in P]
    yn = [yc[k] * lax.rsqrt(var[k] + GN_EPS) * lng_ref[...] + lnb_ref[...] for k in P]
    rw = [((yn[k] + bonus_ref[bands[k], :]) * g_ref[bands[k], :]).astype(BF16) for k in P]
    attn = [jnp.where(is_p, attnp_ref[r, :], attns_ref[r, :]) for r in bands]
    m = [_dot(attn[k], woa_ref[...]) + _dot(rw[k], wob_ref[...]) for k in P]
    x = [jnp.where(is_p, xp_ref[r, :], xs_ref[r, :]) for r in bands]
    h = [_layer_norm(DN_ALPHA * x[k] + m[k], g1_ref[...], b1_ref[...]) for k in P]
    for k in P:
        h_ref[bands[k], :] = h[k]
        hpk_ref[bands[k], :] = _pack_bf16_pairs(h[k])

    h_hi = [h[k].astype(BF16) for k in P]
    h_lo = [(h[k] - h_hi[k].astype(F32)).astype(BF16) for k in P]
    logits = [_dot(h_hi[k], wr_ref[0]) + _dot(h_lo[k], wr_ref[0]) + _dot(h_hi[k], wr_ref[1]) + br_ref[...]
              for k in P]
    lane = lax.broadcasted_iota(jnp.int32, logits[0].shape, 1)
    big = jnp.int32(LANE)

    def route_band(lg):
        gl = jnp.where(lane < N_GROUPS, lg, NEG_INF)
        gmax = jnp.max(gl, axis=1, keepdims=True)
        grp = jnp.min(jnp.where(gl == gmax, lane, big), axis=1, keepdims=True)
        p_grp = 1.0 / jnp.sum(jnp.exp(gl - gmax), axis=1, keepdims=True)
        e_idx = lane - N_GROUPS
        in_grp = (lane >= N_GROUPS) & (lane < N_GROUPS + N_EXPERTS) & ((e_idx // EXPERTS_PER_GROUP) == grp)
        el = jnp.where(in_grp, lg, NEG_INF)
        m1 = jnp.max(el, axis=1, keepdims=True)
        i1 = jnp.min(jnp.where(el == m1, lane, big), axis=1, keepdims=True)
        el2 = jnp.where(lane == i1, NEG_INF, el)
        m2 = jnp.max(el2, axis=1, keepdims=True)
        i2 = jnp.min(jnp.where(el2 == m2, lane, big), axis=1, keepdims=True)
        t = jnp.exp(m2 - m1)
        g1 = p_grp / (1.0 + t)
        return i1, i2, g1, g1 * t

    routed = [route_band(logits[k]) for k in P]
    chosen = jnp.concatenate(
        [jnp.where(lane == routed[k][0], 1.0, jnp.where(lane == routed[k][1], 1.0, 0.0)) for k in P], axis=0)
    before = _dot(tri_ref[...], chosen.astype(BF16))
    count_ref[0] = jnp.sum(chosen, axis=0, keepdims=True)
    for k in P:
        i1, i2, g1, g2 = routed[k]
        bef = before[bands[k]]
        r1 = jnp.sum(jnp.where(lane == i1, bef, 0.0), axis=1, keepdims=True)
        r2 = jnp.sum(jnp.where(lane == i2, bef, 0.0), axis=1, keepdims=True)
        cols = ((i1 - N_GROUPS).astype(F32), (i2 - N_GROUPS).astype(F32), g1, g2, r1, r2)
        route = jnp.zeros(lane.shape, F32)
        for j, col in enumerate(cols):
            route = jnp.where(lane == j, col, route)
        route_ref[bands[k], :] = route


def _mix_call(xp, xs, attn_p, attn_s, y_p, y_s, bonus, g, seg, lnx_g, lnx_b, woa, wob, ln1_g, ln1_b, wr, br,
              tile0, n_tiles):
    tm = TOKEN_TILE
    T = n_tiles * tm
    n_p = xp.shape[0] // tm
    row = lambda i: (i, 0)
    full = lambda i: (0, 0)
    row_p, row_s = _split_rows(n_p, tile0)
    half = pl.BlockSpec((tm, RWKV_DIM), lambda i: (i + tile0, 0))
    vec5 = pl.BlockSpec((1, RWKV_DIM), full)
    vec10 = pl.BlockSpec((1, D_MODEL), full)
    idx = np.arange(tm)
    tri = jnp.asarray((idx[None, :] < idx[:, None]).astype(np.float32)).astype(BF16)
    return pl.pallas_call(
        functools.partial(_mix_kernel, n_p=n_p - tile0, tm=tm),
        grid=(n_tiles,),
        in_specs=[
            pl.BlockSpec((tm, D_MODEL), row_p), pl.BlockSpec((tm, D_MODEL), row_s),
            pl.BlockSpec((tm, RWKV_DIM), row_p), pl.BlockSpec((tm, RWKV_DIM), row_s),
            pl.BlockSpec((tm, RWKV_DIM), row_p), pl.BlockSpec((tm, RWKV_DIM), row_s),
            half, half,
            pl.BlockSpec((RWKV_DIM, RWKV_DIM), full), vec5, vec5,
            pl.BlockSpec((RWKV_DIM, D_MODEL), full), pl.BlockSpec((RWKV_DIM, D_MODEL), full),
            vec10, vec10,
            pl.BlockSpec((2, D_MODEL, LANE), lambda i: (0, 0, 0)), pl.BlockSpec((1, LANE), full),
            pl.BlockSpec((tm, tm), full),
        ],
        out_specs=[pl.BlockSpec((tm, D_MODEL), row), pl.BlockSpec((tm, D_MODEL // 2), row),
                   pl.BlockSpec((tm, LANE), row), pl.BlockSpec((1, 1, LANE), lambda i: (i, 0, 0))],
        out_shape=[jax.ShapeDtypeStruct((T, D_MODEL), F32), jax.ShapeDtypeStruct((T, D_MODEL // 2), jnp.int32),
                   jax.ShapeDtypeStruct((T, LANE), F32), jax.ShapeDtypeStruct((T // tm, 1, LANE), F32)],
        compiler_params=_cparams(("parallel",)),
        name="mix",
    )(xp, xs, attn_p, attn_s, y_p, y_s, bonus, g, seg, lnx_g, lnx_b, woa, wob, ln1_g, ln1_b, wr, br, tri)


def _expert_kernel(be_ref, nu_ref, xs_ref, wg_ref, wu_ref, wd_ref, ys_ref, wgu_b, wd_b):
    i = pl.program_id(0)

    @pl.when((i == 0) | (be_ref[i] != be_ref[jnp.maximum(i - 1, 0)]))
    def _():
        wgu_b[:, :D_EXPERT] = wg_ref[0].astype(BF16)
        wgu_b[:, D_EXPERT:] = wu_ref[0].astype(BF16)
        wd_b[...] = wd_ref[0].astype(BF16)

    @pl.when(i < nu_ref[0])
    def _():
        P = range(2)
        bands = [slice(k * MOE_BLK // 2, (k + 1) * MOE_BLK // 2) for k in P]
        xb = [_unpack_bf16_pairs(xs_ref[r, :]).astype(BF16) for r in bands]
        gu = [_dot(xb[k], wgu_b[...]) for k in P]
        act = [(gu[k][:, :D_EXPERT] * _sigmoid(gu[k][:, :D_EXPERT]) * gu[k][:, D_EXPERT:]).astype(BF16)
               for k in P]
        out = [_dot(act[k], wd_b[...]) for k in P]
        for k in P:
            ys_ref[bands[k], :] = _pack_bf16_pairs(out[k])

    @pl.when(i >= nu_ref[0])
    def _():
        ys_ref[...] = jnp.zeros(ys_ref.shape, ys_ref.dtype)


def _expert_call(block_e, n_used, xs, wg, wu, wd):
    n_blk = xs.shape[0] // MOE_BLK
    grid_spec = pltpu.PrefetchScalarGridSpec(
        num_scalar_prefetch=2,
        grid=(n_blk,),
        in_specs=[
            pl.BlockSpec((MOE_BLK, D_MODEL // 2), lambda i, be, nu: (i, 0)),
            pl.BlockSpec((1, D_MODEL, D_EXPERT), lambda i, be, nu: (be[i], 0, 0)),
            pl.BlockSpec((1, D_MODEL, D_EXPERT), lambda i, be, nu: (be[i], 0, 0)),
            pl.BlockSpec((1, D_EXPERT, D_MODEL), lambda i, be, nu: (be[i], 0, 0)),
        ],
        out_specs=pl.BlockSpec((MOE_BLK, D_MODEL // 2), lambda i, be, nu: (i, 0)),
        scratch_shapes=[pltpu.VMEM((D_MODEL, 2 * D_EXPERT), BF16), pltpu.VMEM((D_EXPERT, D_MODEL), BF16)],
    )
    return pl.pallas_call(
        _expert_kernel,
        grid_spec=grid_spec,
        out_shape=jax.ShapeDtypeStruct((n_blk * MOE_BLK, D_MODEL // 2), jnp.int32),
        compiler_params=_cparams(("arbitrary",)),
        name="experts",
    )(block_e, n_used, xs, wg, wu, wd)


def _combine_kernel(h_ref, ya_ref, yb_ref, route_ref, g2_ref, b2_ref, *rest, n_p, has_prev, has_prompt,
                    has_sample):
    outs = list(rest[1:] if has_prev else rest)
    i = pl.program_id(0)
    route = route_ref[...]
    f = _unpack_bf16_pairs(ya_ref[...]) * route[:, 2:3] + _unpack_bf16_pairs(yb_ref[...]) * route[:, 3:4]
    out = _layer_norm(DN_ALPHA * h_ref[...] + f, g2_ref[...], b2_ref[...])

    if has_prompt:
        op_ref = outs.pop(0)

        @pl.when(i < n_p)
        def _():
            op_ref[...] = out

    if has_sample:
        os_ref = outs.pop(0)

        @pl.when(i >= n_p)
        def _():
            os_ref[...] = out


def _combine_call(h, yab, route, ln2_g, ln2_b, t_prompt, t_sample, tile0, out_p_prev):
    tm = TOKEN_TILE
    n_t, n_p = h.shape[0] // tm, t_prompt // tm
    has_prompt = tile0 < n_p
    has_sample = tile0 + n_t > n_p
    has_prev = has_prompt and out_p_prev is not None
    row = lambda i: (i, 0)
    full = lambda i: (0, 0)
    row_p, row_s = _split_rows(n_p, tile0)
    big = pl.BlockSpec((tm, D_MODEL), row)
    in_specs = [big, pl.BlockSpec((tm, D_MODEL // 2), row),
                pl.BlockSpec((tm, D_MODEL // 2), lambda i: (i + n_t, 0)), pl.BlockSpec((tm, LANE), row),
                pl.BlockSpec((1, D_MODEL), full), pl.BlockSpec((1, D_MODEL), full)]
    args = [h, yab, yab, route, ln2_g, ln2_b]
    out_specs, out_shape, aliases = [], [], {}
    if has_prompt:
        out_specs.append(pl.BlockSpec((tm, D_MODEL), row_p))
        out_shape.append(jax.ShapeDtypeStruct((t_prompt, D_MODEL), F32))
    if has_prev:
        in_specs.append(pl.BlockSpec(memory_space=pl.ANY))
        args.append(out_p_prev)
        aliases = {len(args) - 1: 0}
    if has_sample:
        out_specs.append(pl.BlockSpec((tm, D_MODEL), row_s))
        out_shape.append(jax.ShapeDtypeStruct((t_sample, D_MODEL), F32))
    outs = pl.pallas_call(
        functools.partial(_combine_kernel, n_p=n_p - tile0, has_prev=has_prev, has_prompt=has_prompt,
                          has_sample=has_sample),
        grid=(n_t,),
        in_specs=in_specs,
        out_specs=out_specs,
        out_shape=out_shape,
        input_output_aliases=aliases,
        compiler_params=_cparams(("arbitrary",)),
        name="combine",
    )(*args)
    out_p = outs[0] if has_prompt else out_p_prev
    out_s = outs[-1] if has_sample else None
    return out_p, out_s


def _prep_weights(w_in, w_uq, w_ukv):
    half = MLA_ROPE // 2
    kpe_w = w_in[:, Q_LORA + KV_LORA:MLA_IN]
    kpe_b = jnp.concatenate([-kpe_w[:, half:], kpe_w[:, :half]], axis=1)
    w1 = jnp.concatenate([w_in[:, :Q_LORA + KV_LORA], kpe_w, kpe_b,
                          jnp.zeros((D_MODEL, LANE - 2 * MLA_ROPE), F32), w_in[:, MLA_IN:]], axis=1).astype(BF16)
    pad_q = jnp.zeros((Q_LORA, MLA_HEADS, HEAD_PAD - MLA_NOPE - MLA_ROPE), F32)
    wqa = jnp.concatenate([w_uq, pad_q], axis=2).reshape(Q_LORA, -1).astype(BF16)
    w_uk, w_uv = w_ukv[:, :, :MLA_NOPE], w_ukv[:, :, MLA_NOPE:]
    wk = jnp.concatenate([w_uk, jnp.zeros((KV_LORA, MLA_HEADS, HEAD_PAD - MLA_NOPE), F32)], axis=2)
    wk = wk.reshape(KV_LORA, -1).astype(BF16)
    pk_np = np.zeros((MLA_ROPE, MLA_HEADS * HEAD_PAD), np.float32)
    for h in range(MLA_HEADS):
        for i in range(MLA_ROPE):
            pk_np[i, h * HEAD_PAD + MLA_NOPE + i] = 1.0
    pk = jnp.asarray(pk_np).astype(BF16)
    zv = jnp.zeros((KV_LORA, MLA_HEADS // 2, MLA_V), F32)
    wv4 = w_uv.reshape(KV_LORA, MLA_HEADS // 2, 2, MLA_V)
    wv = jnp.stack([jnp.concatenate([wv4[:, :, 0], zv], axis=2),
                    jnp.concatenate([zv, wv4[:, :, 1]], axis=2)], axis=2)
    wv = wv.reshape(KV_LORA, -1).astype(BF16)
    wuk = jnp.transpose(w_uk, (1, 2, 0)).astype(BF16)
    wuv_np = np.zeros((MLA_HEADS, MLA_HEADS * MLA_V), np.float32)
    for h in range(MLA_HEADS):
        wuv_np[h, h * MLA_V:(h + 1) * MLA_V] = 1.0
    wuv = jnp.transpose(w_uv, (1, 0, 2))
    wuv = (jnp.tile(wuv, (1, 1, MLA_HEADS)) * jnp.asarray(wuv_np)[:, None, :]).astype(BF16)
    return w1, wqa, wk, pk, wv, wuk, wuv


def _rope_table(pos):
    inv = ROPE_BASE ** (-jnp.arange(0, MLA_ROPE, 2, dtype=F32) / MLA_ROPE)
    ang = pos.astype(F32)[:, None] * inv[None, :]
    return jnp.concatenate([jnp.cos(ang), jnp.sin(ang)], axis=1)


def _rope_placement():
    half = MLA_ROPE // 2
    place = np.zeros((MLA_ROPE, 4 * LANE), np.float32)
    rows = np.zeros((2, 4 * LANE), np.float32)
    for i in range(half):
        c, s = i, half + i
        place[c, MLA_NOPE + i] = place[c, MLA_NOPE + half + i] = 1.0
        place[s, LANE + MLA_NOPE + i] = -1.0
        place[s, 2 * LANE + MLA_NOPE + half + i] = 1.0
        place[c, 3 * LANE + i] = place[c, 3 * LANE + half + i] = 1.0
        place[s, 3 * LANE + 2 * half + i] = place[s, 3 * LANE + 3 * half + i] = 1.0
    rows[0, :MLA_NOPE] = 1.0
    rows[1, :3 * LANE] = MLA_SCALE * LOG2E
    rows[1, 3 * LANE:] = 1.0
    return jnp.asarray(place).astype(BF16), jnp.asarray(rows)


def _seg_ones():
    idx = np.arange(RWKV_DIM) // RWKV_N
    return jnp.asarray((idx[:, None] == idx[None, :]).astype(np.float32)).astype(BF16)


def _dispatch(route, tile_counts, t_total):
    A = t_total * TOP_K
    n_tiles = tile_counts.shape[0]
    counts_te = tile_counts[:, 0, N_GROUPS:N_GROUPS + N_EXPERTS].astype(jnp.int32)
    counts = jnp.sum(counts_te, axis=0)
    blocks_per_e = (counts + MOE_BLK - 1) // MOE_BLK
    blk_end = jnp.cumsum(blocks_per_e)
    blk_start = blk_end - blocks_per_e
    tile_off = jnp.cumsum(counts_te, axis=0) - counts_te
    base = blk_start[None, :] * MOE_BLK + tile_off
    base_tok = jnp.repeat(base, t_total // n_tiles, axis=0)
    experts = jnp.arange(N_EXPERTS, dtype=jnp.int32)[None, :]
    dest = []
    for k in range(TOP_K):
        e_k = route[:, k].astype(jnp.int32)
        rank_k = route[:, 4 + k].astype(jnp.int32)
        dest.append(jnp.sum(jnp.where(e_k[:, None] == experts, base_tok, 0), axis=-1) + rank_k)
    n_blk = -(-A // MOE_BLK) + N_EXPERTS
    blk = jnp.arange(n_blk, dtype=jnp.int32)
    block_e = jnp.minimum(jnp.sum((blk[:, None] >= blk_end[None, :]).astype(jnp.int32), axis=1),
                          N_EXPERTS - 1).astype(jnp.int32)
    n_used = blk_end[-1:].astype(jnp.int32)
    return dest, block_e, n_used


def _sc_gather_rows(table, idx):
    n_rows, width = idx.shape[0], table.shape[1]
    n_workers = SC_CORES * SC_SUBCORES
    per_worker = n_rows // n_workers
    assert n_rows % n_workers == 0 and per_worker % SC_WINDOW == 0
    mesh = plsc.VectorSubcoreMesh(core_axis_name="c", subcore_axis_name="s")

    @functools.partial(
        pl.kernel, mesh=mesh,
        out_type=jax.ShapeDtypeStruct((n_rows, width), table.dtype),
        scratch_types=[
            pltpu.VMEM((SC_WINDOW,), jnp.int32),
            pltpu.VMEM((SC_WINDOW, width), table.dtype),
            pltpu.SemaphoreType.DMA,
        ],
    )
    def gather(table_hbm, idx_hbm, out_hbm, idx_v, rows_v, sem):
        wid = lax.axis_index("s") * SC_CORES + lax.axis_index("c")
        base = wid * per_worker

        @pl.loop(0, per_worker // SC_WINDOW)
        def _(w):
            off = pl.multiple_of(base + w * SC_WINDOW, SC_WINDOW)
            pltpu.sync_copy(idx_hbm.at[pl.ds(off, SC_WINDOW)], idx_v)
            pltpu.async_copy(table_hbm.at[idx_v], rows_v, sem).wait()
            pltpu.sync_copy(rows_v, out_hbm.at[pl.ds(off, SC_WINDOW)])

    return gather(table, idx)


def _sc_scatter_rows(src, idx_a, idx_b, n_slots):
    n_rows, width = src.shape
    n_workers = SC_CORES * SC_SUBCORES
    per_worker = n_rows // n_workers
    assert n_rows % n_workers == 0 and per_worker % SC_WINDOW == 0
    n_win = per_worker // SC_WINDOW
    mesh = plsc.VectorSubcoreMesh(core_axis_name="c", subcore_axis_name="s")

    @functools.partial(
        pl.kernel, mesh=mesh,
        out_type=jax.ShapeDtypeStruct((n_slots, width), src.dtype),
        scratch_types=[
            pltpu.VMEM((1, SC_WINDOW), jnp.int32),
            pltpu.VMEM((1, SC_WINDOW), jnp.int32),
            pltpu.VMEM((SC_WINDOW, width), src.dtype),
        ],
    )
    def scatter(src_hbm, ia_hbm, ib_hbm, out_hbm, ia_v, ib_v, rows_v):
        wid = lax.axis_index("s") * SC_CORES + lax.axis_index("c")

        @pl.loop(0, n_win)
        def _(w):
            win = wid * n_win + w
            off = pl.multiple_of(win * SC_WINDOW, SC_WINDOW)
            pltpu.sync_copy(src_hbm.at[pl.ds(off, SC_WINDOW)], rows_v)
            pltpu.sync_copy(ia_hbm.at[pl.ds(win, 1)], ia_v)
            pltpu.sync_copy(ib_hbm.at[pl.ds(win, 1)], ib_v)
            pltpu.sync_copy(rows_v, out_hbm.at[ia_v.at[0]])
            pltpu.sync_copy(rows_v, out_hbm.at[ib_v.at[0]])

    return scatter(src, idx_a, idx_b)


def kernel(x_prompt, x_sample, cache_ckv, cache_kpe, state_wkv, state_shift, w_in, q_norm_g, kv_norm_g, w_uq,
           w_ukv, mu_shift, w0, w2, a0, a2, g2, k_k, k_a, r_k, lnx_g, lnx_b, w_o, ln1_g, ln1_b, w_gr, b_gr,
           w_er, b_er, w_eg, w_eu, w_ed, ln2_g, ln2_b):
    B, S, D = x_prompt.shape
    DB, DS, _ = x_sample.shape
    past = cache_ckv.shape[2]
    Tp, Ts = B * S, DB * DS
    T = Tp + Ts
    assert D == D_MODEL and DS == SHIFT_GROUP and S % ATTN_TQ == 0 and S % (CHUNK * WKV_SUB) == 0
    assert Tp % TOKEN_TILE == 0 and T % TOKEN_TILE == 0 and w_in.shape[0] == DEPTH

    l = 0
    xp, xs_in = x_prompt.reshape(Tp, D), x_sample.reshape(Ts, D)
    w1, wqa, wk, pk, wv, wuk, wuv = _prep_weights(w_in[l], w_uq[l], w_ukv[l])
    pos = jnp.concatenate([jnp.arange(S, dtype=jnp.int32),
                           jnp.tile(past + jnp.arange(DS, dtype=jnp.int32), TOKEN_TILE // DS)])
    rope = _rope_table(pos)

    seg = _seg_ones()
    vec = lambda a: a.reshape(1, -1)
    rwkv_w = (vec(mu_shift[l]), vec(w0[l]), vec(a0[l]), vec(k_k[l]), vec(k_a[l]), vec(r_k[l]),
              w2[l].astype(BF16), a2[l].astype(BF16), g2[l].astype(BF16), seg)
    (q, kcat, vcat, ckv_p, ckv_s, kpe_p, kpe_s, last_rows,
     r, lw, kh, v, na, b, bonus, g) = _proj_call(
        xp, xs_in, rope, S // TOKEN_TILE, *_rope_placement(), w1, q_norm_g[l][None], kv_norm_g[l][None],
        wqa, wk, pk, wv, state_shift[l][:, None, :], rwkv_w)

    attn_p = _attn_call(q, kcat, vcat, B, S)
    attn_s = _mla_sample_call(q, ckv_s, kpe_s, cache_ckv[l], jnp.swapaxes(cache_kpe[l], 1, 2), wuk, wuv,
                              Tp, DB, DS)

    scan_in = (r, lw, kh, v, na, b)
    h0_p = jnp.zeros((B, RWKV_HEADS, RWKV_N, RWKV_N), F32)
    y_p, hT_p = _wkv_call(scan_in, h0_p, 0, B, S // CHUNK, CHUNK, WKV_SUB, math.gcd(B, WKV_PAR))
    h0_s = state_wkv[l]
    y_s, hT_s = _wkv_call(scan_in, h0_s, Tp, DB, 1, DS, 1, math.gcd(DB, WKV_SUB * WKV_PAR))

    wo_b = w_o[l].astype(BF16)
    wr = jnp.concatenate([w_gr[l], w_er[l], jnp.zeros((D, LANE - N_GROUPS - N_EXPERTS), F32)], axis=1)
    wr_hi = wr.astype(BF16)
    wr_lo = (wr - wr_hi.astype(F32)).astype(BF16)
    br = jnp.concatenate([b_gr[l], b_er[l], jnp.zeros((LANE - N_GROUPS - N_EXPERTS,), F32)])[None]
    n_tiles = T // TOKEN_TILE
    wave_tiles = [(t0, min(t0 + MOE_WAVE_TILES, n_tiles) - t0) for t0 in range(0, n_tiles, MOE_WAVE_TILES)]
    out_p, out_s = None, None
    for tile0, nt in wave_tiles:
        t_w = nt * TOKEN_TILE
        h, hpk, route, tile_counts = _mix_call(
            xp, xs_in, attn_p, attn_s, y_p, y_s, bonus, g, seg, vec(lnx_g[l]), vec(lnx_b[l]),
            wo_b[:MLA_HEADS * MLA_V], wo_b[MLA_HEADS * MLA_V:], vec(ln1_g[l]), vec(ln1_b[l]),
            jnp.stack([wr_hi, wr_lo]), br, tile0, nt)
        dest, block_e, n_used = _dispatch(route, tile_counts, t_w)
        win = lambda a: a.reshape(t_w // SC_WINDOW, SC_WINDOW)
        xs = _sc_scatter_rows(hpk, win(dest[0]), win(dest[1]), block_e.shape[0] * MOE_BLK)
        ys = _expert_call(block_e, n_used, xs, w_eg[l], w_eu[l], w_ed[l])
        yab = _sc_gather_rows(ys, jnp.concatenate(dest))
        out_p, wave_s = _combine_call(h, yab, route, vec(ln2_g[l]), vec(ln2_b[l]), Tp, Ts, tile0, out_p)
        assert wave_s is None or out_s is None, "the sample tiles must fall inside one wave"
        out_s = wave_s if wave_s is not None else out_s

    y_prompt = out_p.reshape(B, S, D)
    y_sample = out_s.reshape(DB, DS, D)
    p_ckv = ckv_p.reshape(1, B, S, KV_LORA)
    p_kpe = jnp.transpose(kpe_p.reshape(MLA_ROPE, B, S), (1, 2, 0))[None]
    s_ckv = ckv_s.reshape(1, DB, DS, KV_LORA)
    s_kpe = kpe_s.reshape(1, DB, DS, MLA_ROPE)
    p_wkv = hT_p[None]
    s_wkv = hT_s[None]
    gp = S // SHIFT_GROUP
    p_sh = last_rows[gp - 1:B * gp:gp][None]
    s_sh = last_rows[B * gp:][None]
    return (y_prompt, y_sample, p_ckv, p_kpe, p_wkv, p_sh, s_ckv, s_kpe, s_wkv, s_sh)
```

```python
import functools
import math

import numpy as np
import jax
import jax.numpy as jnp
from jax import lax
from jax.experimental import pallas as pl
from jax.experimental.pallas import tpu as pltpu
from jax.experimental.pallas import tpu_sc as plsc

F32 = jnp.float32
BF16 = jnp.bfloat16

D_MODEL = 1024
CHUNK = 64
MLA_HEADS = 8
MLA_NOPE = 64
MLA_ROPE = 32
MLA_V = 64
Q_LORA = 384
KV_LORA = 256
ROPE_BASE = 10000.0
MLA_IN = Q_LORA + KV_LORA + MLA_ROPE
MLA_SCALE = (MLA_NOPE + MLA_ROPE) ** -0.5
RWKV_HEADS = 8
RWKV_N = 64
RWKV_DIM = RWKV_HEADS * RWKV_N
DECAY_LORA = 64
AAA_LORA = 64
GATE_LORA = 128
RWKV_IN = 3 * RWKV_DIM + DECAY_LORA + AAA_LORA + GATE_LORA
N_GROUPS = 4
EXPERTS_PER_GROUP = 8
N_EXPERTS = N_GROUPS * EXPERTS_PER_GROUP
TOP_K = 2
D_EXPERT = 256
MOE_BLK = 512
LN_EPS = 1e-5
RMS_EPS = 1e-6
GN_EPS = 64e-5
NEG_INF = -1e30
DEPTH = 1
DN_ALPHA = (2 * DEPTH) ** 0.25

LANE = 128
HEAD_PAD = 128
PROJ_KPE = Q_LORA + KV_LORA
PROJ_PR = PROJ_KPE + LANE
PROJ_W = PROJ_PR + RWKV_IN
SHIFT_GROUP = 32
TOKEN_TILE = 512
ATTN_TQ = 1024
ATTN_TK = 512
V_ONE_LANE = (MLA_V, 0)
LOG2E = math.log2(math.e)
VMEM_LIMIT = 48 * 1024 * 1024
PROJ_VMEM_LIMIT = 56 * 1024 * 1024
SC_CORES = 2
SC_SUBCORES = 16
SC_WINDOW = 32
WKV_GROUP = 4
WKV_SUB = 2
WKV_PAR = 2
MIX_PARTS = 2
MOE_WAVE_TILES = 22


def _cparams(sem):
    return pltpu.CompilerParams(dimension_semantics=sem, vmem_limit_bytes=VMEM_LIMIT)


def _split3(x):
    hi = x.astype(BF16)
    r1 = x - hi.astype(F32)
    mid = r1.astype(BF16)
    lo = (r1 - mid.astype(F32)).astype(BF16)
    return hi, mid, lo


def _dot(a, b):
    return jnp.dot(a, b, preferred_element_type=F32)


def _dot_nt(a, b):
    return lax.dot_general(a, b, (((1,), (1,)), ((), ())), preferred_element_type=F32)


def _dot_exact_rhs(x, w):
    hi = x.astype(BF16)
    lo = (x - hi.astype(F32)).astype(BF16)
    return _dot(hi, w) + _dot(lo, w)


def _dot_exact_lhs(w, x):
    hi, mid, lo = _split3(x)
    return _dot(w, hi) + _dot(w, mid) + _dot(w, lo)


def _pack_bf16_pairs(x):
    n = x.shape[1] // 2
    bits = pltpu.bitcast(x.astype(BF16).astype(F32), jnp.int32)
    return (bits[:, :n] & jnp.int32(-65536)) | lax.shift_right_logical(bits[:, n:], jnp.int32(16))


def _unpack_bf16_pairs(p):
    hi = pltpu.bitcast(p & jnp.int32(-65536), F32)
    lo = pltpu.bitcast(lax.shift_left(p, jnp.int32(16)), F32)
    return jnp.concatenate([hi, lo], axis=1)


def _sigmoid(x):
    return 1.0 / (1.0 + jnp.exp(-x))


def _layer_norm(x, g, b):
    xc = x - jnp.mean(x, -1, keepdims=True)
    var = jnp.mean(xc * xc, -1, keepdims=True)
    return xc * lax.rsqrt(var + LN_EPS) * g + b


def _proj_kernel(xp_ref, xs_ref, rope_ref, rp_ref, rc_ref, w1_ref, gq_ref, gkv_ref, wqa_ref, wk_ref, pk_ref, wv_ref,
                 shift_ref, *rest, n_p, tm, seq_tiles):
    rwkv_w = rest[:10]
    q_ref, k_ref, v_ref, ckvp_ref, ckvs_ref, kpep_ref, kpes_ref, last_ref = rest[10:18]
    rwkv_out = rest[18:26]
    carry_scr = rest[26]
    i = pl.program_id(0)
    is_p = i < n_p
    x = jnp.where(is_p, xp_ref[...], xs_ref[...]).astype(BF16)
    proj = _dot(x, w1_ref[...])
    c_q = proj[:, :Q_LORA]
    c_kv = proj[:, Q_LORA:Q_LORA + KV_LORA]
    kp = proj[:, PROJ_KPE:PROJ_PR]
    pr = proj[:, PROJ_PR:]

    ng = tm // SHIFT_GROUP
    pr3 = pr.reshape(ng, SHIFT_GROUP, RWKV_IN)
    last_ref[...] = pr3[:, SHIFT_GROUP - 1, :]
    rolled = pltpu.roll(pr, 1, 0).reshape(ng, SHIFT_GROUP, RWKV_IN)
    first_row = jnp.where(i % seq_tiles == 0, 0.0, carry_scr[...])
    bound = jnp.where(is_p, first_row[None], shift_ref[...])
    row_in_grp = lax.broadcasted_iota(jnp.int32, pr3.shape, 1)
    grp = lax.broadcasted_iota(jnp.int32, pr3.shape, 0)
    use_bound = (row_in_grp == 0) & (grp <= jnp.where(is_p, 0, ng))
    prev = jnp.where(use_bound, bound, rolled).reshape(tm, RWKV_IN)
    carry_scr[...] = pr[tm - 1:tm, :]

    cqn = c_q * lax.rsqrt(jnp.mean(c_q * c_q, -1, keepdims=True) + RMS_EPS) * gq_ref[...]
    ckv = c_kv * lax.rsqrt(jnp.mean(c_kv * c_kv, -1, keepdims=True) + RMS_EPS) * gkv_ref[...]

    rope = (_dot_exact_rhs(rope_ref[...], rp_ref[...]) + rc_ref[0:1, :]) * rc_ref[1:2, :]
    cq = rope[:, :LANE]
    sq_up = rope[:, LANE:2 * LANE]
    sq_dn = rope[:, 2 * LANE:3 * LANE]
    kt = rope[:, 3 * LANE:]
    prod = kp * kt
    kpe = prod[:, :MLA_ROPE] + prod[:, MLA_ROPE:2 * MLA_ROPE]

    qa = _dot(cqn.astype(BF16), wqa_ref[...])
    half = MLA_ROPE // 2
    for h in range(MLA_HEADS):
        qh = qa[:, h * HEAD_PAD:(h + 1) * HEAD_PAD]
        rot = pltpu.roll(qh, HEAD_PAD - half, 1) * sq_up + pltpu.roll(qh, half, 1) * sq_dn
        q_ref[:, h * HEAD_PAD:(h + 1) * HEAD_PAD] = (qh * cq + rot).astype(BF16)

    ckv_b = ckv.astype(BF16)
    k = _dot(ckv_b, wk_ref[...]) + _dot(kpe.astype(BF16), pk_ref[...])
    k_ref[...] = k.astype(BF16)
    lane = lax.broadcasted_iota(jnp.int32, (1, MLA_HEADS * HEAD_PAD), 1)
    odd = (lane // HEAD_PAD) % 2
    one_lane = jnp.where(odd == 1, V_ONE_LANE[1], V_ONE_LANE[0])
    v_one = jnp.where(lane % HEAD_PAD == one_lane, 1.0, 0.0)
    v_ref[...] = (_dot(ckv_b, wv_ref[...]) + v_one).astype(BF16)
    _rwkv_heads(pr, prev, *rwkv_w, *rwkv_out)

    @pl.when(is_p)
    def _():
        ckvp_ref[...] = ckv
        eye = jnp.where(lax.broadcasted_iota(jnp.int32, (MLA_ROPE, MLA_ROPE), 0)
                        == lax.broadcasted_iota(jnp.int32, (MLA_ROPE, MLA_ROPE), 1), 1.0, 0.0).astype(BF16)
        kpep_ref[...] = sum(_dot_nt(eye, part) for part in _split3(kpe))

    @pl.when(jnp.logical_not(is_p))
    def _():
        ckvs_ref[...] = ckv
        kpes_ref[...] = kpe


def _split_rows(n_p, tile0=0):
    return ((lambda i: (jnp.minimum(i + tile0, n_p - 1), 0)),
            (lambda i: (jnp.maximum(i + tile0 - n_p, 0), 0)))


def _proj_call(xp, xs, rope, seq_tiles, rope_place, rope_rows, w1, gq, gkv, wqa, wk, pk, wv, shift, rwkv_w):
    Tp, Ts = xp.shape[0], xs.shape[0]
    T = Tp + Ts
    tm = TOKEN_TILE
    n_p = Tp // tm
    rope_tiles = seq_tiles
    row = lambda i: (i, 0)
    full = lambda i: (0, 0)
    row_p, row_s = _split_rows(n_p)
    wide = MLA_HEADS * HEAD_PAD
    ng = tm // SHIFT_GROUP
    once = pl.Buffered(1)
    vec = pl.BlockSpec((1, RWKV_DIM), full, pipeline_mode=once)
    rwkv_specs = [pl.BlockSpec((1, RWKV_IN), full, pipeline_mode=once), vec, vec, vec, vec, vec,
                  pl.BlockSpec((DECAY_LORA, RWKV_DIM), full, pipeline_mode=once),
                  pl.BlockSpec((AAA_LORA, RWKV_DIM), full, pipeline_mode=once),
                  pl.BlockSpec((GATE_LORA, RWKV_DIM), full, pipeline_mode=once),
                  pl.BlockSpec((RWKV_DIM, RWKV_DIM), full, pipeline_mode=once)]
    tok = pl.BlockSpec((tm, RWKV_DIM), row)
    return pl.pallas_call(
        functools.partial(_proj_kernel, n_p=n_p, tm=tm, seq_tiles=seq_tiles),
        grid=(T // tm,),
        in_specs=[
            pl.BlockSpec((tm, D_MODEL), row_p),
            pl.BlockSpec((tm, D_MODEL), row_s),
            pl.BlockSpec((tm, MLA_ROPE), lambda i: (jnp.where(i < n_p, i % rope_tiles, rope_tiles), 0)),
            pl.BlockSpec((MLA_ROPE, 4 * LANE), full, pipeline_mode=once),
            pl.BlockSpec((2, 4 * LANE), full, pipeline_mode=once),
            pl.BlockSpec((D_MODEL, PROJ_W), full, pipeline_mode=once),
            pl.BlockSpec((1, Q_LORA), full, pipeline_mode=once),
            pl.BlockSpec((1, KV_LORA), full, pipeline_mode=once),
            pl.BlockSpec((Q_LORA, wide), full, pipeline_mode=once),
            pl.BlockSpec((KV_LORA, wide), full, pipeline_mode=once),
            pl.BlockSpec((MLA_ROPE, wide), full, pipeline_mode=once),
            pl.BlockSpec((KV_LORA, wide), full, pipeline_mode=once),
            pl.BlockSpec((ng, 1, RWKV_IN), lambda i: (jnp.maximum(i - n_p, 0), 0, 0)),
        ] + rwkv_specs,
        out_specs=[
            pl.BlockSpec((tm, wide), row),
            pl.BlockSpec((tm, wide), row),
            pl.BlockSpec((tm, wide), row),
            pl.BlockSpec((tm, KV_LORA), row_p),
            pl.BlockSpec((tm, KV_LORA), row_s),
            pl.BlockSpec((MLA_ROPE, tm), lambda i: (0, jnp.minimum(i, n_p - 1))),
            pl.BlockSpec((tm, MLA_ROPE), row_s),
            pl.BlockSpec((ng, RWKV_IN), row),
        ] + [tok] * 8,
        out_shape=[
            jax.ShapeDtypeStruct((T, wide), BF16),
            jax.ShapeDtypeStruct((T, wide), BF16),
            jax.ShapeDtypeStruct((T, wide), BF16),
            jax.ShapeDtypeStruct((Tp, KV_LORA), F32),
            jax.ShapeDtypeStruct((Ts, KV_LORA), F32),
            jax.ShapeDtypeStruct((MLA_ROPE, Tp), F32),
            jax.ShapeDtypeStruct((Ts, MLA_ROPE), F32),
            jax.ShapeDtypeStruct((T // SHIFT_GROUP, RWKV_IN), F32),
        ] + [jax.ShapeDtypeStruct((T, RWKV_DIM), F32)] * 8,
        scratch_shapes=[pltpu.VMEM((1, RWKV_IN), F32)],
        compiler_params=pltpu.CompilerParams(dimension_semantics=("arbitrary",), vmem_limit_bytes=PROJ_VMEM_LIMIT),
        name="proj",
    )(xp, xs, rope, rope_place, rope_rows, w1, gq, gkv, wqa, wk, pk, wv, shift, *rwkv_w)


def _attn_kernel(q_ref, k_ref, v_ref, o_ref, m_scr, acc_scr, *, tq, tk):
    qi = pl.program_id(2)
    m_scr[...] = jnp.full(m_scr.shape, NEG_INF, F32)
    acc_scr[...] = jnp.zeros(acc_scr.shape, F32)
    n_diag = tq // tk

    H = range(2)
    sls = [slice(h * HEAD_PAD, (h + 1) * HEAD_PAD) for h in H]

    def scores(k0, width, rows):
        return [_dot_nt(q_ref[rows, sl], k_ref[pl.ds(k0, width), sl]) for sl in sls]

    def accumulate(s, k0, width, rows):
        m_prev = [m_scr[h, rows, :] for h in H]
        m_new = [jnp.maximum(m_prev[h], jnp.max(s[h], axis=1, keepdims=True)) for h in H]
        pexp = [jnp.exp2(s[h] - jnp.tile(m_new[h], (1, width // LANE))).astype(BF16) for h in H]
        pv = [_dot(pexp[h], v_ref[pl.ds(k0, width), sls[h]]) for h in H]
        for h in H:
            acc_scr[h, rows, :] = jnp.exp2(m_prev[h] - m_new[h]) * acc_scr[h, rows, :] + pv[h]
            m_scr[h, rows, :] = m_new[h]

    def kv_block(k0, width, rows, masked):
        s = scores(k0, width, rows)
        if masked:
            n_rows = rows.stop - rows.start
            r = lax.broadcasted_iota(jnp.int32, (n_rows, width), 0) // CHUNK
            c = lax.broadcasted_iota(jnp.int32, (n_rows, width), 1) // CHUNK
            s = [jnp.where(c <= r, s[h], NEG_INF) for h in H]
        accumulate(s, k0, width, rows)

    all_rows = slice(0, tq)

    def pair(t, carry):
        k0a = pl.multiple_of(2 * t * tq, tq)
        k0b = pl.multiple_of(k0a + tq, tq)
        s_a = scores(k0a, tq, all_rows)
        s_b = scores(k0b, tq, all_rows)
        accumulate(s_a, k0a, tq, all_rows)
        accumulate(s_b, k0b, tq, all_rows)
        return carry

    lax.fori_loop(0, qi // 2, pair, 0)

    @pl.when(qi % 2 == 1)
    def _():
        kv_block(pl.multiple_of((qi - 1) * tq, tq), tq, all_rows, False)

    for d in range(n_diag):
        kv_block(pl.multiple_of(qi * tq + d * tk, tk), tk, slice(d * tk, tq), True)
    acc0, acc1 = acc_scr[0], acc_scr[1]
    lane = lax.broadcasted_iota(jnp.int32, acc0.shape, 1)
    l0 = acc0[:, V_ONE_LANE[0]:V_ONE_LANE[0] + 1]
    l1 = acc1[:, V_ONE_LANE[1]:V_ONE_LANE[1] + 1]
    o_ref[...] = jnp.where(lane < MLA_V, acc0 / l0, acc1 / l1).astype(o_ref.dtype)


def _attn_call(q, k, v, n_batch, seq):
    tq, tk = ATTN_TQ, ATTN_TK
    nq = seq // tq
    hp = MLA_HEADS // 2
    resident = pl.BlockSpec((seq, 2 * HEAD_PAD), lambda b, h, i: (b, h), pipeline_mode=pl.Buffered(1))
    return pl.pallas_call(
        functools.partial(_attn_kernel, tq=tq, tk=tk),
        grid=(n_batch, hp, nq),
        in_specs=[
            pl.BlockSpec((tq, 2 * HEAD_PAD), lambda b, h, i: (b * nq + i, h)),
            resident,
            resident,
        ],
        out_specs=pl.BlockSpec((tq, LANE), lambda b, h, i: (b * nq + i, h)),
        out_shape=jax.ShapeDtypeStruct((n_batch * seq, hp * LANE), BF16),
        scratch_shapes=[
            pltpu.VMEM((2, tq, LANE), F32),
            pltpu.VMEM((2, tq, LANE), F32),
        ],
        compiler_params=_cparams(("parallel", "parallel", "arbitrary")),
        name="attn",
    )(q, k, v)


def _mla_sample_kernel(q_ref, cn_ref, kn_ref, cp_ref, kp_ref, wuk_ref, wuv_ref, o_ref, *, past, dec):
    cp = cp_ref[0].astype(BF16)
    kp_t = kp_ref[0].astype(BF16)
    cn = cn_ref[...].astype(BF16)
    kn = kn_ref[...].astype(BF16)
    R = MLA_HEADS * dec
    qrow = (past + lax.broadcasted_iota(jnp.int32, (R, past), 0) % dec) // CHUNK
    vis_p = (lax.broadcasted_iota(jnp.int32, (R, past), 1) // CHUNK) <= qrow
    qrow_n = (past + lax.broadcasted_iota(jnp.int32, (R, dec), 0) % dec) // CHUNK
    vis_n = ((past + lax.broadcasted_iota(jnp.int32, (R, dec), 1)) // CHUNK) <= qrow_n
    q_lat, qp = [], []
    for h in range(MLA_HEADS):
        qn = q_ref[:, h * HEAD_PAD:h * HEAD_PAD + MLA_NOPE]
        qp.append(q_ref[:, h * HEAD_PAD + MLA_NOPE:h * HEAD_PAD + MLA_NOPE + MLA_ROPE])
        q_lat.append(_dot(qn, wuk_ref[h]).astype(BF16))
    q_lat = jnp.concatenate(q_lat, axis=0)
    qp = jnp.concatenate(qp, axis=0)
    s_p = jnp.where(vis_p, _dot_nt(q_lat, cp) + _dot(qp, kp_t), NEG_INF)
    s_n = jnp.where(vis_n, _dot_nt(q_lat, cn) + _dot_nt(qp, kn), NEG_INF)
    m = jnp.maximum(jnp.max(s_p, axis=1, keepdims=True), jnp.max(s_n, axis=1, keepdims=True))
    e_p = jnp.exp2(s_p - m)
    e_n = jnp.exp2(s_n - m)
    l = jnp.sum(e_p, axis=1, keepdims=True) + jnp.sum(e_n, axis=1, keepdims=True)
    o_lat = ((_dot(e_p.astype(BF16), cp) + _dot(e_n.astype(BF16), cn)) / l).astype(BF16)
    out = jnp.zeros((dec, MLA_HEADS * MLA_V), F32)
    for h in range(MLA_HEADS):
        out = out + _dot(o_lat[h * dec:(h + 1) * dec], wuv_ref[h])
    o_ref[...] = out.astype(o_ref.dtype)


def _mla_sample_call(q, ckv, kpe, cache_ckv, cache_kpe, wuk, wuv, row0, n_seq, dec):
    past = cache_ckv.shape[1]
    blk0 = row0 // dec
    wide = MLA_HEADS * HEAD_PAD
    return pl.pallas_call(
        functools.partial(_mla_sample_kernel, past=past, dec=dec),
        grid=(n_seq,),
        in_specs=[
            pl.BlockSpec((dec, wide), lambda b: (blk0 + b, 0)),
            pl.BlockSpec((dec, KV_LORA), lambda b: (b, 0)),
            pl.BlockSpec((dec, MLA_ROPE), lambda b: (b, 0)),
            pl.BlockSpec((1, past, KV_LORA), lambda b: (b, 0, 0)),
            pl.BlockSpec((1, MLA_ROPE, past), lambda b: (b, 0, 0)),
            pl.BlockSpec((MLA_HEADS, MLA_NOPE, KV_LORA), lambda b: (0, 0, 0)),
            pl.BlockSpec((MLA_HEADS, KV_LORA, MLA_HEADS * MLA_V), lambda b: (0, 0, 0)),
        ],
        out_specs=pl.BlockSpec((dec, MLA_HEADS * MLA_V), lambda b: (b, 0)),
        out_shape=jax.ShapeDtypeStruct((n_seq * dec, MLA_HEADS * MLA_V), BF16),
        compiler_params=_cparams(("parallel",)),
        name="mla_sample",
    )(q, ckv, kpe, cache_ckv, cache_kpe, wuk, wuv)


def _rwkv_heads(pr, prev, mu_ref, w0_ref, a0_ref, kk_ref, ka_ref, rk_ref, w2_ref, a2_ref, g2_ref,
                seg_ref, r_ref, lw_ref, kh_ref, v_ref, na_ref, b_ref, bonus_ref, g_ref):
    u = pr + mu_ref[...] * (prev - pr)
    o1, o2, o3 = RWKV_DIM, 2 * RWKV_DIM, 3 * RWKV_DIM
    o4, o5 = o3 + DECAY_LORA, o3 + DECAY_LORA + AAA_LORA
    r, k, v = u[:, :o1], u[:, o1:o2], u[:, o2:o3]
    w_lo, a_lo, g_lo = u[:, o3:o4], u[:, o4:o5], u[:, o5:]
    wl = w0_ref[...] + _dot(jnp.tanh(w_lo).astype(BF16), w2_ref[...])
    lw_ref[...] = -math.exp(-0.5) * _sigmoid(wl)
    a = _sigmoid(a0_ref[...] + _dot(a_lo.astype(BF16), a2_ref[...]))
    g_ref[...] = _dot(_sigmoid(g_lo).astype(BF16), g2_ref[...])
    seg = seg_ref[...]
    kk = k * kk_ref[...]
    kk = kk / jnp.maximum(jnp.sqrt(_dot_exact_rhs(kk * kk, seg)), 1e-12)
    kh = k * (1.0 + (a - 1.0) * ka_ref[...])
    r_ref[...] = r
    kh_ref[...] = kh
    v_ref[...] = v
    na_ref[...] = -kk
    b_ref[...] = kk * a
    bonus_ref[...] = _dot_exact_rhs(r * kh * rk_ref[...], seg) * v


def _wkv_kernel(*refs, C, n_sub, n_par):
    GW = WKV_GROUP * RWKV_N
    R = WKV_GROUP * C
    n_grp = RWKV_HEADS // WKV_GROUP
    n_lev = int(round(math.log2(C))) - 1
    c = pl.program_id(1)
    tok = [refs[6 * p:6 * p + 6] for p in range(n_par)]
    h0_ref, y_ref, hT_ref, h_scr = refs[6 * n_par:]
    n_state = n_par * n_grp

    def head_block(hh):
        return slice(hh * RWKV_N, (hh + 1) * RWKV_N)

    def transposed(x):
        n = x.shape[0]
        eye = jnp.where(lax.broadcasted_iota(jnp.int32, (n, n), 0) == lax.broadcasted_iota(jnp.int32, (n, n), 1),
                        1.0, 0.0).astype(BF16)
        return sum(_dot_nt(eye, part) for part in _split3(x))

    @pl.when(c == 0)
    def _():
        h_scr[...] = jnp.zeros(h_scr.shape, F32)
        for p in range(n_par):
            for hd in range(RWKV_HEADS):
                g, hh = divmod(hd, WKV_GROUP)
                h_scr[p * n_grp + g, head_block(hh), head_block(hh)] = transposed(h0_ref[p, hd])

    row = lax.broadcasted_iota(jnp.int32, (C, C), 0)
    col = lax.broadcasted_iota(jnp.int32, (C, C), 1)
    tri = jnp.where(col <= row, 1.0, 0.0).astype(BF16)
    rr = lax.broadcasted_iota(jnp.int32, (R, R), 0)
    cc = lax.broadcasted_iota(jnp.int32, (R, R), 1)
    same = (rr // C) == (cc // C)
    ti = lax.broadcasted_iota(jnp.int32, (C, R), 0)
    si = lax.broadcasted_iota(jnp.int32, (C, R), 1) % C
    strict4 = si < ti
    lower4 = si <= ti
    eye4 = jnp.where(si == ti, 1.0, 0.0)
    keep = (lax.broadcasted_iota(jnp.int32, (R, GW), 0) // C
            == lax.broadcasted_iota(jnp.int32, (R, GW), 1) // RWKV_N)
    gr = lax.broadcasted_iota(jnp.int32, (GW, GW), 0)
    gc = lax.broadcasted_iota(jnp.int32, (GW, GW), 1)
    eye_g = gr == gc
    same_head = (gr // RWKV_N) == (gc // RWKV_N)
    eye_g_bf = jnp.where(eye_g, 1.0, 0.0).astype(BF16)

    def rows4(x4):
        return jnp.concatenate([x4] * WKV_GROUP, axis=0)

    def stack(x4):
        return jnp.where(keep, rows4(x4), jnp.zeros((), x4.dtype))


    sls = [slice(g * GW, (g + 1) * GW) for g in range(n_grp)]
    J = [(ci, q) for ci in range(n_sub) for q in range(n_state)]
    ops, p_end = {}, {}
    for p, ci in [(p, ci) for p in range(n_par) for ci in range(n_sub)]:
        r_ref, lw_ref, k_ref, v_ref, a_ref, b_ref = tok[p]
        rows = slice(ci * C, (ci + 1) * C)
        lw = lw_ref[rows, :]
        cum = _dot_exact_lhs(tri, lw)
        cum_end = cum[C - 1:C, :]
        e_neg = jnp.exp(-cum)
        e_end = jnp.exp(cum_end - cum)
        b_in = b_ref[rows, :]
        k_in = k_ref[rows, :]
        full = ((a_ref[rows, :] * jnp.exp(cum - lw)).astype(BF16),
                (r_ref[rows, :] * jnp.exp(cum)).astype(BF16),
                (b_in * e_neg).astype(BF16), (k_in * e_neg).astype(BF16),
                (b_in * e_end).astype(BF16), (k_in * e_end).astype(BF16),
                v_ref[rows, :].astype(BF16))
        p_end[p, ci] = jnp.exp(cum_end)
        for g in range(n_grp):
            ops[ci, p * n_grp + g] = [t[:, sls[g]] for t in full]
    a4, r4, b4, k4, be4, ke4, v4 = [{j: ops[j][i] for j in J} for i in range(7)]
    v_s = {j: stack(v4[j]) for j in J}
    m = {j: _dot_nt(jnp.concatenate([a4[j], r4[j]], axis=0),
                    jnp.concatenate([stack(b4[j]), stack(k4[j])], axis=0)) for j in J}
    l4 = {j: jnp.where(strict4, m[j][:C, :R], 0.0) for j in J}
    a_ak = {j: jnp.where(strict4, m[j][:C, R:], 0.0).astype(BF16) for j in J}
    a_rb = {j: jnp.where(lower4, m[j][C:, :R], 0.0).astype(BF16) for j in J}
    a_rk = {j: jnp.where(lower4, m[j][C:, R:], 0.0).astype(BF16) for j in J}
    def block_diag(x4):
        return jnp.where(same, rows4(x4.astype(BF16)), jnp.zeros((), BF16))

    t4 = {j: eye4 + l4[j] for j in J}
    l_bd = {j: block_diag(l4[j]) for j in J}
    for _ in range(n_lev):
        l4 = {j: _dot(l4[j].astype(BF16), l_bd[j]) for j in J}
        l_bd = {j: block_diag(l4[j]) for j in J}
        t4 = {j: t4[j] + _dot(t4[j].astype(BF16), l_bd[j]) for j in J}
    t_b = {j: t4[j].astype(BF16) for j in J}
    bke_t = {j: _dot_nt(eye_g_bf, jnp.concatenate([be4[j], ke4[j]], axis=0)).astype(BF16) for j in J}

    G = range(n_state)
    h_cur = [h_scr[q] for q in G]
    for ci in range(n_sub):
        rows = slice(ci * C, (ci + 1) * C)
        h0_b = [h_cur[g].astype(BF16) for g in G]
        x4 = [_dot(a4[ci, g], h0_b[g]) + _dot(a_ak[ci, g], v_s[ci, g]) for g in G]
        u4 = [_dot(t_b[ci, g], stack(x4[g].astype(BF16))).astype(BF16) for g in G]
        y4 = [_dot(r4[ci, g], h0_b[g]) + _dot(a_rb[ci, g], stack(u4[g])) + _dot(a_rk[ci, g], v_s[ci, g])
              for g in G]
        uv4 = [jnp.concatenate([u4[g], v4[ci, g]], axis=0) for g in G]
        h_add = [jnp.where(same_head, _dot(bke_t[ci, g], uv4[g]), 0.0) for g in G]
        for q in G:
            p, g = divmod(q, n_grp)
            y_ref[p, rows, sls[g]] = y4[q]
            p_col = jnp.sum(jnp.where(eye_g, p_end[p, ci][:, sls[g]], 0.0), axis=1, keepdims=True)
            h_cur[q] = p_col * h_cur[q] + h_add[q]

    for q in G:
        h_scr[q] = h_cur[q]

    @pl.when(c == pl.num_programs(1) - 1)
    def _():
        for p in range(n_par):
            for hd in range(RWKV_HEADS):
                g, hh = divmod(hd, WKV_GROUP)
                hT_ref[p, hd] = transposed(h_scr[p * n_grp + g, head_block(hh), head_block(hh)])


def _wkv_call(arrs, h0, row0, n_seq, n_chunk, C, n_sub, n_par):
    rows = C * n_sub
    steps = n_chunk // n_sub
    blk0 = row0 // rows
    GW = WKV_GROUP * RWKV_N
    n_grp = RWKV_HEADS // WKV_GROUP
    assert n_seq % n_par == 0 and n_chunk % n_sub == 0
    tok = [pl.BlockSpec((rows, RWKV_DIM), lambda b, c, p=p: (blk0 + (b * n_par + p) * steps + c, 0))
           for p in range(n_par)]
    st = pl.BlockSpec((n_par, RWKV_HEADS, RWKV_N, RWKV_N), lambda b, c: (b, 0, 0, 0))
    y, h_fin = pl.pallas_call(
        functools.partial(_wkv_kernel, C=C, n_sub=n_sub, n_par=n_par),
        grid=(n_seq // n_par, steps),
        in_specs=[tok[p] for p in range(n_par) for _ in range(6)] + [st],
        out_specs=[pl.BlockSpec((n_par, rows, RWKV_DIM), lambda b, c: (b, c, 0)), st],
        out_shape=[
            jax.ShapeDtypeStruct((n_seq, n_chunk * C, RWKV_DIM), F32),
            jax.ShapeDtypeStruct((n_seq, RWKV_HEADS, RWKV_N, RWKV_N), F32),
        ],
        scratch_shapes=[pltpu.VMEM((n_par * n_grp, GW, GW), F32)],
        compiler_params=_cparams(("parallel", "arbitrary")),
        name="wkv_c%d" % C,
    )(*(list(arrs) * n_par), h0)
    return y.reshape(n_seq * n_chunk * C, RWKV_DIM), h_fin


def _mix_kernel(xp_ref, xs_ref, attnp_ref, attns_ref, yp_ref, ys_ref, bonus_ref, g_ref, seg_ref, lng_ref,
                lnb_ref, woa_ref, wob_ref, g1_ref, b1_ref, wr_ref, br_ref, tri_ref, h_ref, hpk_ref, route_ref,
                count_ref, *, n_p, tm):
    is_p = pl.program_id(0) < n_p
    P = range(MIX_PARTS)
    bands = [slice(k * tm // MIX_PARTS, (k + 1) * tm // MIX_PARTS) for k in P]
    seg = seg_ref[...]
    inv_n = 1.0 / RWKV_N
    y = [jnp.where(is_p, yp_ref[r, :], ys_ref[r, :]) for r in bands]
    yc = [y[k] - _dot_exact_rhs(y[k], seg) * inv_n for k in P]
    var = [_dot_exact_rhs(yc[k] * yc[k], seg) * inv_n for k in P]
    yn = [yc[k] * lax.rsqrt(var[k] + GN_EPS) * lng_ref[...] + lnb_ref[...] for k in P]
    rw = [((yn[k] + bonus_ref[bands[k], :]) * g_ref[bands[k], :]).astype(BF16) for k in P]
    attn = [jnp.where(is_p, attnp_ref[r, :], attns_ref[r, :]) for r in bands]
    m = [_dot(attn[k], woa_ref[...]) + _dot(rw[k], wob_ref[...]) for k in P]
    x = [jnp.where(is_p, xp_ref[r, :], xs_ref[r, :]) for r in bands]
    h = [_layer_norm(DN_ALPHA * x[k] + m[k], g1_ref[...], b1_ref[...]) for k in P]
    for k in P:
        h_ref[bands[k], :] = h[k]
        hpk_ref[bands[k], :] = _pack_bf16_pairs(h[k])

    h_hi = [h[k].astype(BF16) for k in P]
    h_lo = [(h[k] - h_hi[k].astype(F32)).astype(BF16) for k in P]
    logits = [_dot(h_hi[k], wr_ref[0]) + _dot(h_lo[k], wr_ref[0]) + _dot(h_hi[k], wr_ref[1]) + br_ref[...]
              for k in P]
    lane = lax.broadcasted_iota(jnp.int32, logits[0].shape, 1)
    big = jnp.int32(LANE)

    def route_band(lg):
        gl = jnp.where(lane < N_GROUPS, lg, NEG_INF)
        gmax = jnp.max(gl, axis=1, keepdims=True)
        grp = jnp.min(jnp.where(gl == gmax, lane, big), axis=1, keepdims=True)
        p_grp = 1.0 / jnp.sum(jnp.exp(gl - gmax), axis=1, keepdims=True)
        e_idx = lane - N_GROUPS
        in_grp = (lane >= N_GROUPS) & (lane < N_GROUPS + N_EXPERTS) & ((e_idx // EXPERTS_PER_GROUP) == grp)
        el = jnp.where(in_grp, lg, NEG_INF)
        m1 = jnp.max(el, axis=1, keepdims=True)
        i1 = jnp.min(jnp.where(el == m1, lane, big), axis=1, keepdims=True)
        el2 = jnp.where(lane == i1, NEG_INF, el)
        m2 = jnp.max(el2, axis=1, keepdims=True)
        i2 = jnp.min(jnp.where(el2 == m2, lane, big), axis=1, keepdims=True)
        t = jnp.exp(m2 - m1)
        g1 = p_grp / (1.0 + t)
        return i1, i2, g1, g1 * t

    routed = [route_band(logits[k]) for k in P]
    chosen = jnp.concatenate(
        [jnp.where(lane == routed[k][0], 1.0, jnp.where(lane == routed[k][1], 1.0, 0.0)) for k in P], axis=0)
    before = _dot(tri_ref[...], chosen.astype(BF16))
    count_ref[0] = jnp.sum(chosen, axis=0, keepdims=True)
    for k in P:
        i1, i2, g1, g2 = routed[k]
        bef = before[bands[k]]
        r1 = jnp.sum(jnp.where(lane == i1, bef, 0.0), axis=1, keepdims=True)
        r2 = jnp.sum(jnp.where(lane == i2, bef, 0.0), axis=1, keepdims=True)
        cols = ((i1 - N_GROUPS).astype(F32), (i2 - N_GROUPS).astype(F32), g1, g2, r1, r2)
        route = jnp.zeros(lane.shape, F32)
        for j, col in enumerate(cols):
            route = jnp.where(lane == j, col, route)
        route_ref[bands[k], :] = route


def _mix_call(xp, xs, attn_p, attn_s, y_p, y_s, bonus, g, seg, lnx_g, lnx_b, woa, wob, ln1_g, ln1_b, wr, br,
              tile0, n_tiles):
    tm = TOKEN_TILE
    T = n_tiles * tm
    n_p = xp.shape[0] // tm
    row = lambda i: (i, 0)
    full = lambda i: (0, 0)
    row_p, row_s = _split_rows(n_p, tile0)
    half = pl.BlockSpec((tm, RWKV_DIM), lambda i: (i + tile0, 0))
    vec5 = pl.BlockSpec((1, RWKV_DIM), full)
    vec10 = pl.BlockSpec((1, D_MODEL), full)
    idx = np.arange(tm)
    tri = jnp.asarray((idx[None, :] < idx[:, None]).astype(np.float32)).astype(BF16)
    return pl.pallas_call(
        functools.partial(_mix_kernel, n_p=n_p - tile0, tm=tm),
        grid=(n_tiles,),
        in_specs=[
            pl.BlockSpec((tm, D_MODEL), row_p), pl.BlockSpec((tm, D_MODEL), row_s),
            pl.BlockSpec((tm, RWKV_DIM), row_p), pl.BlockSpec((tm, RWKV_DIM), row_s),
            pl.BlockSpec((tm, RWKV_DIM), row_p), pl.BlockSpec((tm, RWKV_DIM), row_s),
            half, half,
            pl.BlockSpec((RWKV_DIM, RWKV_DIM), full), vec5, vec5,
            pl.BlockSpec((RWKV_DIM, D_MODEL), full), pl.BlockSpec((RWKV_DIM, D_MODEL), full),
            vec10, vec10,
            pl.BlockSpec((2, D_MODEL, LANE), lambda i: (0, 0, 0)), pl.BlockSpec((1, LANE), full),
            pl.BlockSpec((tm, tm), full),
        ],
        out_specs=[pl.BlockSpec((tm, D_MODEL), row), pl.BlockSpec((tm, D_MODEL // 2), row),
                   pl.BlockSpec((tm, LANE), row), pl.BlockSpec((1, 1, LANE), lambda i: (i, 0, 0))],
        out_shape=[jax.ShapeDtypeStruct((T, D_MODEL), F32), jax.ShapeDtypeStruct((T, D_MODEL // 2), jnp.int32),
                   jax.ShapeDtypeStruct((T, LANE), F32), jax.ShapeDtypeStruct((T // tm, 1, LANE), F32)],
        compiler_params=_cparams(("parallel",)),
        name="mix",
    )(xp, xs, attn_p, attn_s, y_p, y_s, bonus, g, seg, lnx_g, lnx_b, woa, wob, ln1_g, ln1_b, wr, br, tri)


def _expert_kernel(be_ref, nu_ref, xs_ref, wg_ref, wu_ref, wd_ref, ys_ref, wgu_b, wd_b):
    i = pl.program_id(0)

    @pl.when((i == 0) | (be_ref[i] != be_ref[jnp.maximum(i - 1, 0)]))
    def _():
        wgu_b[:, :D_EXPERT] = wg_ref[0].astype(BF16)
        wgu_b[:, D_EXPERT:] = wu_ref[0].astype(BF16)
        wd_b[...] = wd_ref[0].astype(BF16)

    @pl.when(i < nu_ref[0])
    def _():
        P = range(2)
        bands = [slice(k * MOE_BLK // 2, (k + 1) * MOE_BLK // 2) for k in P]
        xb = [_unpack_bf16_pairs(xs_ref[r, :]).astype(BF16) for r in bands]
        gu = [_dot(xb[k], wgu_b[...]) for k in P]
        act = [(gu[k][:, :D_EXPERT] * _sigmoid(gu[k][:, :D_EXPERT]) * gu[k][:, D_EXPERT:]).astype(BF16)
               for k in P]
        out = [_dot(act[k], wd_b[...]) for k in P]
        for k in P:
            ys_ref[bands[k], :] = _pack_bf16_pairs(out[k])

    @pl.when(i >= nu_ref[0])
    def _():
        ys_ref[...] = jnp.zeros(ys_ref.shape, ys_ref.dtype)


def _expert_call(block_e, n_used, xs, wg, wu, wd):
    n_blk = xs.shape[0] // MOE_BLK
    grid_spec = pltpu.PrefetchScalarGridSpec(
        num_scalar_prefetch=2,
        grid=(n_blk,),
        in_specs=[
            pl.BlockSpec((MOE_BLK, D_MODEL // 2), lambda i, be, nu: (i, 0)),
            pl.BlockSpec((1, D_MODEL, D_EXPERT), lambda i, be, nu: (be[i], 0, 0)),
            pl.BlockSpec((1, D_MODEL, D_EXPERT), lambda i, be, nu: (be[i], 0, 0)),
            pl.BlockSpec((1, D_EXPERT, D_MODEL), lambda i, be, nu: (be[i], 0, 0)),
        ],
        out_specs=pl.BlockSpec((MOE_BLK, D_MODEL // 2), lambda i, be, nu: (i, 0)),
        scratch_shapes=[pltpu.VMEM((D_MODEL, 2 * D_EXPERT), BF16), pltpu.VMEM((D_EXPERT, D_MODEL), BF16)],
    )
    return pl.pallas_call(
        _expert_kernel,
        grid_spec=grid_spec,
        out_shape=jax.ShapeDtypeStruct((n_blk * MOE_BLK, D_MODEL // 2), jnp.int32),
        compiler_params=_cparams(("arbitrary",)),
        name="experts",
    )(block_e, n_used, xs, wg, wu, wd)


def _combine_kernel(h_ref, ya_ref, yb_ref, route_ref, g2_ref, b2_ref, *rest, n_p, has_prev, has_prompt,
                    has_sample):
    outs = list(rest[1:] if has_prev else rest)
    i = pl.program_id(0)
    route = route_ref[...]
    f = _unpack_bf16_pairs(ya_ref[...]) * route[:, 2:3] + _unpack_bf16_pairs(yb_ref[...]) * route[:, 3:4]
    out = _layer_norm(DN_ALPHA * h_ref[...] + f, g2_ref[...], b2_ref[...])

    if has_prompt:
        op_ref = outs.pop(0)

        @pl.when(i < n_p)
        def _():
            op_ref[...] = out

    if has_sample:
        os_ref = outs.pop(0)

        @pl.when(i >= n_p)
        def _():
            os_ref[...] = out


def _combine_call(h, yab, route, ln2_g, ln2_b, t_prompt, t_sample, tile0, out_p_prev):
    tm = TOKEN_TILE
    n_t, n_p = h.shape[0] // tm, t_prompt // tm
    has_prompt = tile0 < n_p
    has_sample = tile0 + n_t > n_p
    has_prev = has_prompt and out_p_prev is not None
    row = lambda i: (i, 0)
    full = lambda i: (0, 0)
    row_p, row_s = _split_rows(n_p, tile0)
    big = pl.BlockSpec((tm, D_MODEL), row)
    in_specs = [big, pl.BlockSpec((tm, D_MODEL // 2), row),
                pl.BlockSpec((tm, D_MODEL // 2), lambda i: (i + n_t, 0)), pl.BlockSpec((tm, LANE), row),
                pl.BlockSpec((1, D_MODEL), full), pl.BlockSpec((1, D_MODEL), full)]
    args = [h, yab, yab, route, ln2_g, ln2_b]
    out_specs, out_shape, aliases = [], [], {}
    if has_prompt:
        out_specs.append(pl.BlockSpec((tm, D_MODEL), row_p))
        out_shape.append(jax.ShapeDtypeStruct((t_prompt, D_MODEL), F32))
    if has_prev:
        in_specs.append(pl.BlockSpec(memory_space=pl.ANY))
        args.append(out_p_prev)
        aliases = {len(args) - 1: 0}
    if has_sample:
        out_specs.append(pl.BlockSpec((tm, D_MODEL), row_s))
        out_shape.append(jax.ShapeDtypeStruct((t_sample, D_MODEL), F32))
    outs = pl.pallas_call(
        functools.partial(_combine_kernel, n_p=n_p - tile0, has_prev=has_prev, has_prompt=has_prompt,
                          has_sample=has_sample),
        grid=(n_t,),
        in_specs=in_specs,
        out_specs=out_specs,
        out_shape=out_shape,
        input_output_aliases=aliases,
        compiler_params=_cparams(("arbitrary",)),
        name="combine",
    )(*args)
    out_p = outs[0] if has_prompt else out_p_prev
    out_s = outs[-1] if has_sample else None
    return out_p, out_s


def _prep_weights(w_in, w_uq, w_ukv):
    half = MLA_ROPE // 2
    kpe_w = w_in[:, Q_LORA + KV_LORA:MLA_IN]
    kpe_b = jnp.concatenate([-kpe_w[:, half:], kpe_w[:, :half]], axis=1)
    w1 = jnp.concatenate([w_in[:, :Q_LORA + KV_LORA], kpe_w, kpe_b,
                          jnp.zeros((D_MODEL, LANE - 2 * MLA_ROPE), F32), w_in[:, MLA_IN:]], axis=1).astype(BF16)
    pad_q = jnp.zeros((Q_LORA, MLA_HEADS, HEAD_PAD - MLA_NOPE - MLA_ROPE), F32)
    wqa = jnp.concatenate([w_uq, pad_q], axis=2).reshape(Q_LORA, -1).astype(BF16)
    w_uk, w_uv = w_ukv[:, :, :MLA_NOPE], w_ukv[:, :, MLA_NOPE:]
    wk = jnp.concatenate([w_uk, jnp.zeros((KV_LORA, MLA_HEADS, HEAD_PAD - MLA_NOPE), F32)], axis=2)
    wk = wk.reshape(KV_LORA, -1).astype(BF16)
    pk_np = np.zeros((MLA_ROPE, MLA_HEADS * HEAD_PAD), np.float32)
    for h in range(MLA_HEADS):
        for i in range(MLA_ROPE):
            pk_np[i, h * HEAD_PAD + MLA_NOPE + i] = 1.0
    pk = jnp.asarray(pk_np).astype(BF16)
    zv = jnp.zeros((KV_LORA, MLA_HEADS // 2, MLA_V), F32)
    wv4 = w_uv.reshape(KV_LORA, MLA_HEADS // 2, 2, MLA_V)
    wv = jnp.stack([jnp.concatenate([wv4[:, :, 0], zv], axis=2),
                    jnp.concatenate([zv, wv4[:, :, 1]], axis=2)], axis=2)
    wv = wv.reshape(KV_LORA, -1).astype(BF16)
    wuk = jnp.transpose(w_uk, (1, 2, 0)).astype(BF16)
    wuv_np = np.zeros((MLA_HEADS, MLA_HEADS * MLA_V), np.float32)
    for h in range(MLA_HEADS):
        wuv_np[h, h * MLA_V:(h + 1) * MLA_V] = 1.0
    wuv = jnp.transpose(w_uv, (1, 0, 2))
    wuv = (jnp.tile(wuv, (1, 1, MLA_HEADS)) * jnp.asarray(wuv_np)[:, None, :]).astype(BF16)
    return w1, wqa, wk, pk, wv, wuk, wuv


def _rope_table(pos):
    inv = ROPE_BASE ** (-jnp.arange(0, MLA_ROPE, 2, dtype=F32) / MLA_ROPE)
    ang = pos.astype(F32)[:, None] * inv[None, :]
    return jnp.concatenate([jnp.cos(ang), jnp.sin(ang)], axis=1)


def _rope_placement():
    half = MLA_ROPE // 2
    place = np.zeros((MLA_ROPE, 4 * LANE), np.float32)
    rows = np.zeros((2, 4 * LANE), np.float32)
    for i in range(half):
        c, s = i, half + i
        place[c, MLA_NOPE + i] = place[c, MLA_NOPE + half + i] = 1.0
        place[s, LANE + MLA_NOPE + i] = -1.0
        place[s, 2 * LANE + MLA_NOPE + half + i] = 1.0
        place[c, 3 * LANE + i] = place[c, 3 * LANE + half + i] = 1.0
        place[s, 3 * LANE + 2 * half + i] = place[s, 3 * LANE + 3 * half + i] = 1.0
    rows[0, :MLA_NOPE] = 1.0
    rows[1, :3 * LANE] = MLA_SCALE * LOG2E
    rows[1, 3 * LANE:] = 1.0
    return jnp.asarray(place).astype(BF16), jnp.asarray(rows)


def _seg_ones():
    idx = np.arange(RWKV_DIM) // RWKV_N
    return jnp.asarray((idx[:, None] == idx[None, :]).astype(np.float32)).astype(BF16)


def _dispatch(route, tile_counts, t_total):
    A = t_total * TOP_K
    n_tiles = tile_counts.shape[0]
    counts_te = tile_counts[:, 0, N_GROUPS:N_GROUPS + N_EXPERTS].astype(jnp.int32)
    counts = jnp.sum(counts_te, axis=0)
    blocks_per_e = (counts + MOE_BLK - 1) // MOE_BLK
    blk_end = jnp.cumsum(blocks_per_e)
    blk_start = blk_end - blocks_per_e
    tile_off = jnp.cumsum(counts_te, axis=0) - counts_te
    base = blk_start[None, :] * MOE_BLK + tile_off
    base_tok = jnp.repeat(base, t_total // n_tiles, axis=0)
    experts = jnp.arange(N_EXPERTS, dtype=jnp.int32)[None, :]
    dest = []
    for k in range(TOP_K):
        e_k = route[:, k].astype(jnp.int32)
        rank_k = route[:, 4 + k].astype(jnp.int32)
        dest.append(jnp.sum(jnp.where(e_k[:, None] == experts, base_tok, 0), axis=-1) + rank_k)
    n_blk = -(-A // MOE_BLK) + N_EXPERTS
    blk = jnp.arange(n_blk, dtype=jnp.int32)
    block_e = jnp.minimum(jnp.sum((blk[:, None] >= blk_end[None, :]).astype(jnp.int32), axis=1),
                          N_EXPERTS - 1).astype(jnp.int32)
    n_used = blk_end[-1:].astype(jnp.int32)
    return dest, block_e, n_used


def _sc_gather_rows(table, idx):
    n_rows, width = idx.shape[0], table.shape[1]
    n_workers = SC_CORES * SC_SUBCORES
    per_worker = n_rows // n_workers
    assert n_rows % n_workers == 0 and per_worker % SC_WINDOW == 0
    mesh = plsc.VectorSubcoreMesh(core_axis_name="c", subcore_axis_name="s")

    @functools.partial(
        pl.kernel, mesh=mesh,
        out_type=jax.ShapeDtypeStruct((n_rows, width), table.dtype),
        scratch_types=[
            pltpu.VMEM((SC_WINDOW,), jnp.int32),
            pltpu.VMEM((SC_WINDOW, width), table.dtype),
            pltpu.SemaphoreType.DMA,
        ],
    )
    def gather(table_hbm, idx_hbm, out_hbm, idx_v, rows_v, sem):
        wid = lax.axis_index("s") * SC_CORES + lax.axis_index("c")
        base = wid * per_worker

        @pl.loop(0, per_worker // SC_WINDOW)
        def _(w):
            off = pl.multiple_of(base + w * SC_WINDOW, SC_WINDOW)
            pltpu.sync_copy(idx_hbm.at[pl.ds(off, SC_WINDOW)], idx_v)
            pltpu.async_copy(table_hbm.at[idx_v], rows_v, sem).wait()
            pltpu.sync_copy(rows_v, out_hbm.at[pl.ds(off, SC_WINDOW)])

    return gather(table, idx)


def _sc_scatter_rows(src, idx_a, idx_b, n_slots):
    n_rows, width = src.shape
    n_workers = SC_CORES * SC_SUBCORES
    per_worker = n_rows // n_workers
    assert n_rows % n_workers == 0 and per_worker % SC_WINDOW == 0
    n_win = per_worker // SC_WINDOW
    mesh = plsc.VectorSubcoreMesh(core_axis_name="c", subcore_axis_name="s")

    @functools.partial(
        pl.kernel, mesh=mesh,
        out_type=jax.ShapeDtypeStruct((n_slots, width), src.dtype),
        scratch_types=[
            pltpu.VMEM((1, SC_WINDOW), jnp.int32),
            pltpu.VMEM((1, SC_WINDOW), jnp.int32),
            pltpu.VMEM((SC_WINDOW, width), src.dtype),
        ],
    )
    def scatter(src_hbm, ia_hbm, ib_hbm, out_hbm, ia_v, ib_v, rows_v):
        wid = lax.axis_index("s") * SC_CORES + lax.axis_index("c")

        @pl.loop(0, n_win)
        def _(w):
            win = wid * n_win + w
            off = pl.multiple_of(win * SC_WINDOW, SC_WINDOW)
            pltpu.sync_copy(src_hbm.at[pl.ds(off, SC_WINDOW)], rows_v)
            pltpu.sync_copy(ia_hbm.at[pl.ds(win, 1)], ia_v)
            pltpu.sync_copy(ib_hbm.at[pl.ds(win, 1)], ib_v)
            pltpu.sync_copy(rows_v, out_hbm.at[ia_v.at[0]])
            pltpu.sync_copy(rows_v, out_hbm.at[ib_v.at[0]])

    return scatter(src, idx_a, idx_b)


def kernel(x_prompt, x_sample, cache_ckv, cache_kpe, state_wkv, state_shift, w_in, q_norm_g, kv_norm_g, w_uq,
           w_ukv, mu_shift, w0, w2, a0, a2, g2, k_k, k_a, r_k, lnx_g, lnx_b, w_o, ln1_g, ln1_b, w_gr, b_gr,
           w_er, b_er, w_eg, w_eu, w_ed, ln2_g, ln2_b):
    B, S, D = x_prompt.shape
    DB, DS, _ = x_sample.shape
    past = cache_ckv.shape[2]
    Tp, Ts = B * S, DB * DS
    T = Tp + Ts
    assert D == D_MODEL and DS == SHIFT_GROUP and S % ATTN_TQ == 0 and S % (CHUNK * WKV_SUB) == 0
    assert Tp % TOKEN_TILE == 0 and T % TOKEN_TILE == 0 and w_in.shape[0] == DEPTH

    l = 0
    xp, xs_in = x_prompt.reshape(Tp, D), x_sample.reshape(Ts, D)
    w1, wqa, wk, pk, wv, wuk, wuv = _prep_weights(w_in[l], w_uq[l], w_ukv[l])
    pos = jnp.concatenate([jnp.arange(S, dtype=jnp.int32),
                           jnp.tile(past + jnp.arange(DS, dtype=jnp.int32), TOKEN_TILE // DS)])
    rope = _rope_table(pos)

    seg = _seg_ones()
    vec = lambda a: a.reshape(1, -1)
    rwkv_w = (vec(mu_shift[l]), vec(w0[l]), vec(a0[l]), vec(k_k[l]), vec(k_a[l]), vec(r_k[l]),
              w2[l].astype(BF16), a2[l].astype(BF16), g2[l].astype(BF16), seg)
    (q, kcat, vcat, ckv_p, ckv_s, kpe_p, kpe_s, last_rows,
     r, lw, kh, v, na, b, bonus, g) = _proj_call(
        xp, xs_in, rope, S // TOKEN_TILE, *_rope_placement(), w1, q_norm_g[l][None], kv_norm_g[l][None],
        wqa, wk, pk, wv, state_shift[l][:, None, :], rwkv_w)

    attn_p = _attn_call(q, kcat, vcat, B, S)
    attn_s = _mla_sample_call(q, ckv_s, kpe_s, cache_ckv[l], jnp.swapaxes(cache_kpe[l], 1, 2), wuk, wuv,
                              Tp, DB, DS)

    scan_in = (r, lw, kh, v, na, b)
    h0_p = jnp.zeros((B, RWKV_HEADS, RWKV_N, RWKV_N), F32)
    y_p, hT_p = _wkv_call(scan_in, h0_p, 0, B, S // CHUNK, CHUNK, WKV_SUB, math.gcd(B, WKV_PAR))
    h0_s = state_wkv[l]
    y_s, hT_s = _wkv_call(scan_in, h0_s, Tp, DB, 1, DS, 1, math.gcd(DB, WKV_SUB * WKV_PAR))

    wo_b = w_o[l].astype(BF16)
    wr = jnp.concatenate([w_gr[l], w_er[l], jnp.zeros((D, LANE - N_GROUPS - N_EXPERTS), F32)], axis=1)
    wr_hi = wr.astype(BF16)
    wr_lo = (wr - wr_hi.astype(F32)).astype(BF16)
    br = jnp.concatenate([b_gr[l], b_er[l], jnp.zeros((LANE - N_GROUPS - N_EXPERTS,), F32)])[None]
    n_tiles = T // TOKEN_TILE
    wave_tiles = [(t0, min(t0 + MOE_WAVE_TILES, n_tiles) - t0) for t0 in range(0, n_tiles, MOE_WAVE_TILES)]
    out_p, out_s = None, None
    for tile0, nt in wave_tiles:
        t_w = nt * TOKEN_TILE
        h, hpk, route, tile_counts = _mix_call(
            xp, xs_in, attn_p, attn_s, y_p, y_s, bonus, g, seg, vec(lnx_g[l]), vec(lnx_b[l]),
            wo_b[:MLA_HEADS * MLA_V], wo_b[MLA_HEADS * MLA_V:], vec(ln1_g[l]), vec(ln1_b[l]),
            jnp.stack([wr_hi, wr_lo]), br, tile0, nt)
        dest, block_e, n_used = _dispatch(route, tile_counts, t_w)
        win = lambda a: a.reshape(t_w // SC_WINDOW, SC_WINDOW)
        xs = _sc_scatter_rows(hpk, win(dest[0]), win(dest[1]), block_e.shape[0] * MOE_BLK)
        ys = _expert_call(block_e, n_used, xs, w_eg[l], w_eu[l], w_ed[l])
        yab = _sc_gather_rows(ys, jnp.concatenate(dest))
        out_p, wave_s = _combine_call(h, yab, route, vec(ln2_g[l]), vec(ln2_b[l]), Tp, Ts, tile0, out_p)
        assert wave_s is None or out_s is None, "the sample tiles must fall inside one wave"
        out_s = wave_s if wave_s is not None else out_s

    y_prompt = out_p.reshape(B, S, D)
    y_sample = out_s.reshape(DB, DS, D)
    p_ckv = ckv_p.reshape(1, B, S, KV_LORA)
    p_kpe = jnp.transpose(kpe_p.reshape(MLA_ROPE, B, S), (1, 2, 0))[None]
    s_ckv = ckv_s.reshape(1, DB, DS, KV_LORA)
    s_kpe = kpe_s.reshape(1, DB, DS, MLA_ROPE)
    p_wkv = hT_p[None]
    s_wkv = hT_s[None]
    gp = S // SHIFT_GROUP
    p_sh = last_rows[gp - 1:B * gp:gp][None]
    s_sh = last_rows[B * gp:][None]
    return (y_prompt, y_sample, p_ckv, p_kpe, p_wkv, p_sh, s_ckv, s_kpe, s_wkv, s_sh)
```

```python
import functools
import math

import numpy as np
import jax
import jax.numpy as jnp
from jax import lax
from jax.experimental import pallas as pl
from jax.experimental.pallas import tpu as pltpu
from jax.experimental.pallas import tpu_sc as plsc

F32 = jnp.float32
BF16 = jnp.bfloat16

D_MODEL = 1024
CHUNK = 64
MLA_HEADS = 8
MLA_NOPE = 64
MLA_ROPE = 32
MLA_V = 64
Q_LORA = 384
KV_LORA = 256
ROPE_BASE = 10000.0
MLA_IN = Q_LORA + KV_LORA + MLA_ROPE
MLA_SCALE = (MLA_NOPE + MLA_ROPE) ** -0.5
RWKV_HEADS = 8
RWKV_N = 64
RWKV_DIM = RWKV_HEADS * RWKV_N
DECAY_LORA = 64
AAA_LORA = 64
GATE_LORA = 128
RWKV_IN = 3 * RWKV_DIM + DECAY_LORA + AAA_LORA + GATE_LORA
N_GROUPS = 4
EXPERTS_PER_GROUP = 8
N_EXPERTS = N_GROUPS * EXPERTS_PER_GROUP
TOP_K = 2
D_EXPERT = 256
MOE_BLK = 512
LN_EPS = 1e-5
RMS_EPS = 1e-6
GN_EPS = 64e-5
NEG_INF = -1e30
DEPTH = 1
DN_ALPHA = (2 * DEPTH) ** 0.25

LANE = 128
HEAD_PAD = 128
PROJ_KPE = Q_LORA + KV_LORA
PROJ_PR = PROJ_KPE + LANE
PROJ_W = PROJ_PR + RWKV_IN
SHIFT_GROUP = 32
TOKEN_TILE = 512
ATTN_TQ = 1024
ATTN_TK = 512
V_ONE_LANE = (MLA_V, 0)
LOG2E = math.log2(math.e)
VMEM_LIMIT = 48 * 1024 * 1024
PROJ_VMEM_LIMIT = 56 * 1024 * 1024
SC_CORES = 2
SC_SUBCORES = 16
SC_WINDOW = 32
WKV_GROUP = 4
WKV_SUB = 2
WKV_PAR = 2
MIX_PARTS = 2
MOE_WAVE_TILES = 34


def _cparams(sem):
    return pltpu.CompilerParams(dimension_semantics=sem, vmem_limit_bytes=VMEM_LIMIT)


def _split3(x):
    hi = x.astype(BF16)
    r1 = x - hi.astype(F32)
    mid = r1.astype(BF16)
    lo = (r1 - mid.astype(F32)).astype(BF16)
    return hi, mid, lo


def _dot(a, b):
    return jnp.dot(a, b, preferred_element_type=F32)


def _dot_nt(a, b):
    return lax.dot_general(a, b, (((1,), (1,)), ((), ())), preferred_element_type=F32)


def _dot_exact_rhs(x, w):
    hi = x.astype(BF16)
    lo = (x - hi.astype(F32)).astype(BF16)
    return _dot(hi, w) + _dot(lo, w)


def _dot_exact_lhs(w, x):
    hi, mid, lo = _split3(x)
    return _dot(w, hi) + _dot(w, mid) + _dot(w, lo)


def _pack_bf16_pairs(x):
    n = x.shape[1] // 2
    bits = pltpu.bitcast(x.astype(BF16).astype(F32), jnp.int32)
    return (bits[:, :n] & jnp.int32(-65536)) | lax.shift_right_logical(bits[:, n:], jnp.int32(16))


def _unpack_bf16_pairs(p):
    hi = pltpu.bitcast(p & jnp.int32(-65536), F32)
    lo = pltpu.bitcast(lax.shift_left(p, jnp.int32(16)), F32)
    return jnp.concatenate([hi, lo], axis=1)


def _sigmoid(x):
    return 1.0 / (1.0 + jnp.exp(-x))


def _layer_norm(x, g, b):
    xc = x - jnp.mean(x, -1, keepdims=True)
    var = jnp.mean(xc * xc, -1, keepdims=True)
    return xc * lax.rsqrt(var + LN_EPS) * g + b


def _proj_kernel(xp_ref, xs_ref, rope_ref, rp_ref, rc_ref, w1_ref, gq_ref, gkv_ref, wqa_ref, wk_ref, pk_ref, wv_ref,
                 shift_ref, *rest, n_p, tm, seq_tiles):
    rwkv_w = rest[:10]
    q_ref, k_ref, v_ref, ckvp_ref, ckvs_ref, kpep_ref, kpes_ref, last_ref = rest[10:18]
    rwkv_out = rest[18:26]
    carry_scr = rest[26]
    i = pl.program_id(0)
    is_p = i < n_p
    x = jnp.where(is_p, xp_ref[...], xs_ref[...]).astype(BF16)
    proj = _dot(x, w1_ref[...])
    c_q = proj[:, :Q_LORA]
    c_kv = proj[:, Q_LORA:Q_LORA + KV_LORA]
    kp = proj[:, PROJ_KPE:PROJ_PR]
    pr = proj[:, PROJ_PR:]

    ng = tm // SHIFT_GROUP
    pr3 = pr.reshape(ng, SHIFT_GROUP, RWKV_IN)
    last_ref[...] = pr3[:, SHIFT_GROUP - 1, :]
    rolled = pltpu.roll(pr, 1, 0).reshape(ng, SHIFT_GROUP, RWKV_IN)
    first_row = jnp.where(i % seq_tiles == 0, 0.0, carry_scr[...])
    bound = jnp.where(is_p, first_row[None], shift_ref[...])
    row_in_grp = lax.broadcasted_iota(jnp.int32, pr3.shape, 1)
    grp = lax.broadcasted_iota(jnp.int32, pr3.shape, 0)
    use_bound = (row_in_grp == 0) & (grp <= jnp.where(is_p, 0, ng))
    prev = jnp.where(use_bound, bound, rolled).reshape(tm, RWKV_IN)
    carry_scr[...] = pr[tm - 1:tm, :]

    cqn = c_q * lax.rsqrt(jnp.mean(c_q * c_q, -1, keepdims=True) + RMS_EPS) * gq_ref[...]
    ckv = c_kv * lax.rsqrt(jnp.mean(c_kv * c_kv, -1, keepdims=True) + RMS_EPS) * gkv_ref[...]

    rope = (_dot_exact_rhs(rope_ref[...], rp_ref[...]) + rc_ref[0:1, :]) * rc_ref[1:2, :]
    cq = rope[:, :LANE]
    sq_up = rope[:, LANE:2 * LANE]
    sq_dn = rope[:, 2 * LANE:3 * LANE]
    kt = rope[:, 3 * LANE:]
    prod = kp * kt
    kpe = prod[:, :MLA_ROPE] + prod[:, MLA_ROPE:2 * MLA_ROPE]

    qa = _dot(cqn.astype(BF16), wqa_ref[...])
    half = MLA_ROPE // 2
    for h in range(MLA_HEADS):
        qh = qa[:, h * HEAD_PAD:(h + 1) * HEAD_PAD]
        rot = pltpu.roll(qh, HEAD_PAD - half, 1) * sq_up + pltpu.roll(qh, half, 1) * sq_dn
        q_ref[:, h * HEAD_PAD:(h + 1) * HEAD_PAD] = (qh * cq + rot).astype(BF16)

    ckv_b = ckv.astype(BF16)
    k = _dot(ckv_b, wk_ref[...]) + _dot(kpe.astype(BF16), pk_ref[...])
    k_ref[...] = k.astype(BF16)
    lane = lax.broadcasted_iota(jnp.int32, (1, MLA_HEADS * HEAD_PAD), 1)
    odd = (lane // HEAD_PAD) % 2
    one_lane = jnp.where(odd == 1, V_ONE_LANE[1], V_ONE_LANE[0])
    v_one = jnp.where(lane % HEAD_PAD == one_lane, 1.0, 0.0)
    v_ref[...] = (_dot(ckv_b, wv_ref[...]) + v_one).astype(BF16)
    _rwkv_heads(pr, prev, *rwkv_w, *rwkv_out)

    @pl.when(is_p)
    def _():
        ckvp_ref[...] = ckv
        eye = jnp.where(lax.broadcasted_iota(jnp.int32, (MLA_ROPE, MLA_ROPE), 0)
                        == lax.broadcasted_iota(jnp.int32, (MLA_ROPE, MLA_ROPE), 1), 1.0, 0.0).astype(BF16)
        kpep_ref[...] = sum(_dot_nt(eye, part) for part in _split3(kpe))

    @pl.when(jnp.logical_not(is_p))
    def _():
        ckvs_ref[...] = ckv
        kpes_ref[...] = kpe


def _split_rows(n_p, tile0=0):
    return ((lambda i: (jnp.minimum(i + tile0, n_p - 1), 0)),
            (lambda i: (jnp.maximum(i + tile0 - n_p, 0), 0)))


def _proj_call(xp, xs, rope, seq_tiles, rope_place, rope_rows, w1, gq, gkv, wqa, wk, pk, wv, shift, rwkv_w):
    Tp, Ts = xp.shape[0], xs.shape[0]
    T = Tp + Ts
    tm = TOKEN_TILE
    n_p = Tp // tm
    rope_tiles = seq_tiles
    row = lambda i: (i, 0)
    full = lambda i: (0, 0)
    row_p, row_s = _split_rows(n_p)
    wide = MLA_HEADS * HEAD_PAD
    ng = tm // SHIFT_GROUP
    once = pl.Buffered(1)
    vec = pl.BlockSpec((1, RWKV_DIM), full, pipeline_mode=once)
    rwkv_specs = [pl.BlockSpec((1, RWKV_IN), full, pipeline_mode=once), vec, vec, vec, vec, vec,
                  pl.BlockSpec((DECAY_LORA, RWKV_DIM), full, pipeline_mode=once),
                  pl.BlockSpec((AAA_LORA, RWKV_DIM), full, pipeline_mode=once),
                  pl.BlockSpec((GATE_LORA, RWKV_DIM), full, pipeline_mode=once),
                  pl.BlockSpec((RWKV_DIM, RWKV_DIM), full, pipeline_mode=once)]
    tok = pl.BlockSpec((tm, RWKV_DIM), row)
    return pl.pallas_call(
        functools.partial(_proj_kernel, n_p=n_p, tm=tm, seq_tiles=seq_tiles),
        grid=(T // tm,),
        in_specs=[
            pl.BlockSpec((tm, D_MODEL), row_p),
            pl.BlockSpec((tm, D_MODEL), row_s),
            pl.BlockSpec((tm, MLA_ROPE), lambda i: (jnp.where(i < n_p, i % rope_tiles, rope_tiles), 0)),
            pl.BlockSpec((MLA_ROPE, 4 * LANE), full, pipeline_mode=once),
            pl.BlockSpec((2, 4 * LANE), full, pipeline_mode=once),
            pl.BlockSpec((D_MODEL, PROJ_W), full, pipeline_mode=once),
            pl.BlockSpec((1, Q_LORA), full, pipeline_mode=once),
            pl.BlockSpec((1, KV_LORA), full, pipeline_mode=once),
            pl.BlockSpec((Q_LORA, wide), full, pipeline_mode=once),
            pl.BlockSpec((KV_LORA, wide), full, pipeline_mode=once),
            pl.BlockSpec((MLA_ROPE, wide), full, pipeline_mode=once),
            pl.BlockSpec((KV_LORA, wide), full, pipeline_mode=once),
            pl.BlockSpec((ng, 1, RWKV_IN), lambda i: (jnp.maximum(i - n_p, 0), 0, 0)),
        ] + rwkv_specs,
        out_specs=[
            pl.BlockSpec((tm, wide), row),
            pl.BlockSpec((tm, wide), row),
            pl.BlockSpec((tm, wide), row),
            pl.BlockSpec((tm, KV_LORA), row_p),
            pl.BlockSpec((tm, KV_LORA), row_s),
            pl.BlockSpec((MLA_ROPE, tm), lambda i: (0, jnp.minimum(i, n_p - 1))),
            pl.BlockSpec((tm, MLA_ROPE), row_s),
            pl.BlockSpec((ng, RWKV_IN), row),
        ] + [tok] * 8,
        out_shape=[
            jax.ShapeDtypeStruct((T, wide), BF16),
            jax.ShapeDtypeStruct((T, wide), BF16),
            jax.ShapeDtypeStruct((T, wide), BF16),
            jax.ShapeDtypeStruct((Tp, KV_LORA), F32),
            jax.ShapeDtypeStruct((Ts, KV_LORA), F32),
            jax.ShapeDtypeStruct((MLA_ROPE, Tp), F32),
            jax.ShapeDtypeStruct((Ts, MLA_ROPE), F32),
            jax.ShapeDtypeStruct((T // SHIFT_GROUP, RWKV_IN), F32),
        ] + [jax.ShapeDtypeStruct((T, RWKV_DIM), F32)] * 8,
        scratch_shapes=[pltpu.VMEM((1, RWKV_IN), F32)],
        compiler_params=pltpu.CompilerParams(dimension_semantics=("arbitrary",), vmem_limit_bytes=PROJ_VMEM_LIMIT),
        name="proj",
    )(xp, xs, rope, rope_place, rope_rows, w1, gq, gkv, wqa, wk, pk, wv, shift, *rwkv_w)


def _attn_kernel(q_ref, k_ref, v_ref, o_ref, m_scr, acc_scr, *, tq, tk):
    qi = pl.program_id(2)
    m_scr[...] = jnp.full(m_scr.shape, NEG_INF, F32)
    acc_scr[...] = jnp.zeros(acc_scr.shape, F32)
    n_diag = tq // tk

    H = range(2)
    sls = [slice(h * HEAD_PAD, (h + 1) * HEAD_PAD) for h in H]

    def scores(k0, width, rows):
        return [_dot_nt(q_ref[rows, sl], k_ref[pl.ds(k0, width), sl]) for sl in sls]

    def accumulate(s, k0, width, rows):
        m_prev = [m_scr[h, rows, :] for h in H]
        m_new = [jnp.maximum(m_prev[h], jnp.max(s[h], axis=1, keepdims=True)) for h in H]
        pexp = [jnp.exp2((s[h] - jnp.tile(m_new[h], (1, width // LANE))).astype(BF16)) for h in H]
        pv = [_dot(pexp[h], v_ref[pl.ds(k0, width), sls[h]]) for h in H]
        for h in H:
            acc_scr[h, rows, :] = jnp.exp2(m_prev[h] - m_new[h]) * acc_scr[h, rows, :] + pv[h]
            m_scr[h, rows, :] = m_new[h]

    def kv_block(k0, width, rows, masked):
        s = scores(k0, width, rows)
        if masked:
            n_rows = rows.stop - rows.start
            r = lax.broadcasted_iota(jnp.int32, (n_rows, width), 0) // CHUNK
            c = lax.broadcasted_iota(jnp.int32, (n_rows, width), 1) // CHUNK
            s = [jnp.where(c <= r, s[h], NEG_INF) for h in H]
        accumulate(s, k0, width, rows)

    all_rows = slice(0, tq)

    def pair(t, carry):
        k0a = pl.multiple_of(2 * t * tq, tq)
        k0b = pl.multiple_of(k0a + tq, tq)
        s_a = scores(k0a, tq, all_rows)
        s_b = scores(k0b, tq, all_rows)
        accumulate(s_a, k0a, tq, all_rows)
        accumulate(s_b, k0b, tq, all_rows)
        return carry

    lax.fori_loop(0, qi // 2, pair, 0)

    @pl.when(qi % 2 == 1)
    def _():
        kv_block(pl.multiple_of((qi - 1) * tq, tq), tq, all_rows, False)

    for d in range(n_diag):
        kv_block(pl.multiple_of(qi * tq + d * tk, tk), tk, slice(d * tk, tq), True)
    acc0, acc1 = acc_scr[0], acc_scr[1]
    lane = lax.broadcasted_iota(jnp.int32, acc0.shape, 1)
    l0 = acc0[:, V_ONE_LANE[0]:V_ONE_LANE[0] + 1]
    l1 = acc1[:, V_ONE_LANE[1]:V_ONE_LANE[1] + 1]
    o_ref[...] = jnp.where(lane < MLA_V, acc0 / l0, acc1 / l1).astype(o_ref.dtype)


def _attn_call(q, k, v, n_batch, seq):
    tq, tk = ATTN_TQ, ATTN_TK
    nq = seq // tq
    hp = MLA_HEADS // 2
    resident = pl.BlockSpec((seq, 2 * HEAD_PAD), lambda b, h, i: (b, h), pipeline_mode=pl.Buffered(1))
    return pl.pallas_call(
        functools.partial(_attn_kernel, tq=tq, tk=tk),
        grid=(n_batch, hp, nq),
        in_specs=[
            pl.BlockSpec((tq, 2 * HEAD_PAD), lambda b, h, i: (b * nq + i, h)),
            resident,
            resident,
        ],
        out_specs=pl.BlockSpec((tq, LANE), lambda b, h, i: (b * nq + i, h)),
        out_shape=jax.ShapeDtypeStruct((n_batch * seq, hp * LANE), BF16),
        scratch_shapes=[
            pltpu.VMEM((2, tq, LANE), F32),
            pltpu.VMEM((2, tq, LANE), F32),
        ],
        compiler_params=_cparams(("parallel", "parallel", "arbitrary")),
        name="attn",
    )(q, k, v)


def _mla_sample_kernel(q_ref, cn_ref, kn_ref, cp_ref, kp_ref, wuk_ref, wuv_ref, o_ref, *, past, dec):
    cp = cp_ref[0].astype(BF16)
    kp_t = kp_ref[0].astype(BF16)
    cn = cn_ref[...].astype(BF16)
    kn = kn_ref[...].astype(BF16)
    R = MLA_HEADS * dec
    qrow = (past + lax.broadcasted_iota(jnp.int32, (R, past), 0) % dec) // CHUNK
    vis_p = (lax.broadcasted_iota(jnp.int32, (R, past), 1) // CHUNK) <= qrow
    qrow_n = (past + lax.broadcasted_iota(jnp.int32, (R, dec), 0) % dec) // CHUNK
    vis_n = ((past + lax.broadcasted_iota(jnp.int32, (R, dec), 1)) // CHUNK) <= qrow_n
    q_lat, qp = [], []
    for h in range(MLA_HEADS):
        qn = q_ref[:, h * HEAD_PAD:h * HEAD_PAD + MLA_NOPE]
        qp.append(q_ref[:, h * HEAD_PAD + MLA_NOPE:h * HEAD_PAD + MLA_NOPE + MLA_ROPE])
        q_lat.append(_dot(qn, wuk_ref[h]).astype(BF16))
    q_lat = jnp.concatenate(q_lat, axis=0)
    qp = jnp.concatenate(qp, axis=0)
    s_p = jnp.where(vis_p, _dot_nt(q_lat, cp) + _dot(qp, kp_t), NEG_INF)
    s_n = jnp.where(vis_n, _dot_nt(q_lat, cn) + _dot_nt(qp, kn), NEG_INF)
    m = jnp.maximum(jnp.max(s_p, axis=1, keepdims=True), jnp.max(s_n, axis=1, keepdims=True))
    e_p = jnp.exp2(s_p - m)
    e_n = jnp.exp2(s_n - m)
    l = jnp.sum(e_p, axis=1, keepdims=True) + jnp.sum(e_n, axis=1, keepdims=True)
    o_lat = ((_dot(e_p.astype(BF16), cp) + _dot(e_n.astype(BF16), cn)) / l).astype(BF16)
    out = jnp.zeros((dec, MLA_HEADS * MLA_V), F32)
    for h in range(MLA_HEADS):
        out = out + _dot(o_lat[h * dec:(h + 1) * dec], wuv_ref[h])
    o_ref[...] = out.astype(o_ref.dtype)


def _mla_sample_call(q, ckv, kpe, cache_ckv, cache_kpe, wuk, wuv, row0, n_seq, dec):
    past = cache_ckv.shape[1]
    blk0 = row0 // dec
    wide = MLA_HEADS * HEAD_PAD
    return pl.pallas_call(
        functools.partial(_mla_sample_kernel, past=past, dec=dec),
        grid=(n_seq,),
        in_specs=[
            pl.BlockSpec((dec, wide), lambda b: (blk0 + b, 0)),
            pl.BlockSpec((dec, KV_LORA), lambda b: (b, 0)),
            pl.BlockSpec((dec, MLA_ROPE), lambda b: (b, 0)),
            pl.BlockSpec((1, past, KV_LORA), lambda b: (b, 0, 0)),
            pl.BlockSpec((1, MLA_ROPE, past), lambda b: (b, 0, 0)),
            pl.BlockSpec((MLA_HEADS, MLA_NOPE, KV_LORA), lambda b: (0, 0, 0)),
            pl.BlockSpec((MLA_HEADS, KV_LORA, MLA_HEADS * MLA_V), lambda b: (0, 0, 0)),
        ],
        out_specs=pl.BlockSpec((dec, MLA_HEADS * MLA_V), lambda b: (b, 0)),
        out_shape=jax.ShapeDtypeStruct((n_seq * dec, MLA_HEADS * MLA_V), BF16),
        compiler_params=_cparams(("parallel",)),
        name="mla_sample",
    )(q, ckv, kpe, cache_ckv, cache_kpe, wuk, wuv)


def _rwkv_heads(pr, prev, mu_ref, w0_ref, a0_ref, kk_ref, ka_ref, rk_ref, w2_ref, a2_ref, g2_ref,
                seg_ref, r_ref, lw_ref, kh_ref, v_ref, na_ref, b_ref, bonus_ref, g_ref):
    u = pr + mu_ref[...] * (prev - pr)
    o1, o2, o3 = RWKV_DIM, 2 * RWKV_DIM, 3 * RWKV_DIM
    o4, o5 = o3 + DECAY_LORA, o3 + DECAY_LORA + AAA_LORA
    r, k, v = u[:, :o1], u[:, o1:o2], u[:, o2:o3]
    w_lo, a_lo, g_lo = u[:, o3:o4], u[:, o4:o5], u[:, o5:]
    wl = w0_ref[...] + _dot(jnp.tanh(w_lo).astype(BF16), w2_ref[...])
    lw_ref[...] = -math.exp(-0.5) * _sigmoid(wl)
    a = _sigmoid(a0_ref[...] + _dot(a_lo.astype(BF16), a2_ref[...]))
    g_ref[...] = _dot(_sigmoid(g_lo).astype(BF16), g2_ref[...])
    seg = seg_ref[...]
    kk = k * kk_ref[...]
    kk = kk / jnp.maximum(jnp.sqrt(_dot_exact_rhs(kk * kk, seg)), 1e-12)
    kh = k * (1.0 + (a - 1.0) * ka_ref[...])
    r_ref[...] = r
    kh_ref[...] = kh
    v_ref[...] = v
    na_ref[...] = -kk
    b_ref[...] = kk * a
    bonus_ref[...] = _dot_exact_rhs(r * kh * rk_ref[...], seg) * v


def _wkv_kernel(*refs, C, n_sub, n_par):
    GW = WKV_GROUP * RWKV_N
    R = WKV_GROUP * C
    n_grp = RWKV_HEADS // WKV_GROUP
    n_lev = int(round(math.log2(C))) - 1
    c = pl.program_id(1)
    tok = [refs[6 * p:6 * p + 6] for p in range(n_par)]
    h0_ref, y_ref, hT_ref, h_scr = refs[6 * n_par:]
    n_state = n_par * n_grp

    def head_block(hh):
        return slice(hh * RWKV_N, (hh + 1) * RWKV_N)

    def transposed(x):
        n = x.shape[0]
        eye = jnp.where(lax.broadcasted_iota(jnp.int32, (n, n), 0) == lax.broadcasted_iota(jnp.int32, (n, n), 1),
                        1.0, 0.0).astype(BF16)
        return sum(_dot_nt(eye, part) for part in _split3(x))

    @pl.when(c == 0)
    def _():
        h_scr[...] = jnp.zeros(h_scr.shape, F32)
        for p in range(n_par):
            for hd in range(RWKV_HEADS):
                g, hh = divmod(hd, WKV_GROUP)
                h_scr[p * n_grp + g, head_block(hh), head_block(hh)] = transposed(h0_ref[p, hd])

    row = lax.broadcasted_iota(jnp.int32, (C, C), 0)
    col = lax.broadcasted_iota(jnp.int32, (C, C), 1)
    tri = jnp.where(col <= row, 1.0, 0.0).astype(BF16)
    rr = lax.broadcasted_iota(jnp.int32, (R, R), 0)
    cc = lax.broadcasted_iota(jnp.int32, (R, R), 1)
    same = (rr // C) == (cc // C)
    ti = lax.broadcasted_iota(jnp.int32, (C, R), 0)
    si = lax.broadcasted_iota(jnp.int32, (C, R), 1) % C
    strict4 = si < ti
    lower4 = si <= ti
    eye4 = jnp.where(si == ti, 1.0, 0.0)
    keep = (lax.broadcasted_iota(jnp.int32, (R, GW), 0) // C
            == lax.broadcasted_iota(jnp.int32, (R, GW), 1) // RWKV_N)
    gr = lax.broadcasted_iota(jnp.int32, (GW, GW), 0)
    gc = lax.broadcasted_iota(jnp.int32, (GW, GW), 1)
    eye_g = gr == gc
    same_head = (gr // RWKV_N) == (gc // RWKV_N)
    eye_g_bf = jnp.where(eye_g, 1.0, 0.0).astype(BF16)

    def rows4(x4):
        return jnp.concatenate([x4] * WKV_GROUP, axis=0)

    def stack(x4):
        return jnp.where(keep, rows4(x4), jnp.zeros((), x4.dtype))


    sls = [slice(g * GW, (g + 1) * GW) for g in range(n_grp)]
    J = [(ci, q) for ci in range(n_sub) for q in range(n_state)]
    ops, p_end = {}, {}
    for p, ci in [(p, ci) for p in range(n_par) for ci in range(n_sub)]:
        r_ref, lw_ref, k_ref, v_ref, a_ref, b_ref = tok[p]
        rows = slice(ci * C, (ci + 1) * C)
        lw = lw_ref[rows, :]
        cum = _dot_exact_lhs(tri, lw)
        cum_end = cum[C - 1:C, :]
        e_neg = jnp.exp(-cum)
        e_end = jnp.exp(cum_end - cum)
        b_in = b_ref[rows, :]
        k_in = k_ref[rows, :]
        full = ((a_ref[rows, :] * jnp.exp(cum - lw)).astype(BF16),
                (r_ref[rows, :] * jnp.exp(cum)).astype(BF16),
                (b_in * e_neg).astype(BF16), (k_in * e_neg).astype(BF16),
                (b_in * e_end).astype(BF16), (k_in * e_end).astype(BF16),
                v_ref[rows, :].astype(BF16))
        p_end[p, ci] = jnp.exp(cum_end)
        for g in range(n_grp):
            ops[ci, p * n_grp + g] = [t[:, sls[g]] for t in full]
    a4, r4, b4, k4, be4, ke4, v4 = [{j: ops[j][i] for j in J} for i in range(7)]
    v_s = {j: stack(v4[j]) for j in J}
    m = {j: _dot_nt(jnp.concatenate([a4[j], r4[j]], axis=0),
                    jnp.concatenate([stack(b4[j]), stack(k4[j])], axis=0)) for j in J}
    l4 = {j: jnp.where(strict4, m[j][:C, :R], 0.0) for j in J}
    a_ak = {j: jnp.where(strict4, m[j][:C, R:], 0.0).astype(BF16) for j in J}
    a_rb = {j: jnp.where(lower4, m[j][C:, :R], 0.0).astype(BF16) for j in J}
    a_rk = {j: jnp.where(lower4, m[j][C:, R:], 0.0).astype(BF16) for j in J}
    def block_diag(x4):
        return jnp.where(same, rows4(x4.astype(BF16)), jnp.zeros((), BF16))

    t4 = {j: eye4 + l4[j] for j in J}
    l_bd = {j: block_diag(l4[j]) for j in J}
    for _ in range(n_lev):
        l4 = {j: _dot(l4[j].astype(BF16), l_bd[j]) for j in J}
        l_bd = {j: block_diag(l4[j]) for j in J}
        t4 = {j: t4[j] + _dot(t4[j].astype(BF16), l_bd[j]) for j in J}
    t_b = {j: t4[j].astype(BF16) for j in J}
    bke_t = {j: _dot_nt(eye_g_bf, jnp.concatenate([be4[j], ke4[j]], axis=0)).astype(BF16) for j in J}

    G = range(n_state)
    h_cur = [h_scr[q] for q in G]
    for ci in range(n_sub):
        rows = slice(ci * C, (ci + 1) * C)
        h0_b = [h_cur[g].astype(BF16) for g in G]
        x4 = [_dot(a4[ci, g], h0_b[g]) + _dot(a_ak[ci, g], v_s[ci, g]) for g in G]
        u4 = [_dot(t_b[ci, g], stack(x4[g].astype(BF16))).astype(BF16) for g in G]
        y4 = [_dot(r4[ci, g], h0_b[g]) + _dot(a_rb[ci, g], stack(u4[g])) + _dot(a_rk[ci, g], v_s[ci, g])
              for g in G]
        uv4 = [jnp.concatenate([u4[g], v4[ci, g]], axis=0) for g in G]
        h_add = [jnp.where(same_head, _dot(bke_t[ci, g], uv4[g]), 0.0) for g in G]
        for q in G:
            p, g = divmod(q, n_grp)
            y_ref[p, rows, sls[g]] = y4[q]
            p_col = jnp.sum(jnp.where(eye_g, p_end[p, ci][:, sls[g]], 0.0), axis=1, keepdims=True)
            h_cur[q] = p_col * h_cur[q] + h_add[q]

    for q in G:
        h_scr[q] = h_cur[q]

    @pl.when(c == pl.num_programs(1) - 1)
    def _():
        for p in range(n_par):
            for hd in range(RWKV_HEADS):
                g, hh = divmod(hd, WKV_GROUP)
                hT_ref[p, hd] = transposed(h_scr[p * n_grp + g, head_block(hh), head_block(hh)])


def _wkv_call(arrs, h0, row0, n_seq, n_chunk, C, n_sub, n_par):
    rows = C * n_sub
    steps = n_chunk // n_sub
    blk0 = row0 // rows
    GW = WKV_GROUP * RWKV_N
    n_grp = RWKV_HEADS // WKV_GROUP
    assert n_seq % n_par == 0 and n_chunk % n_sub == 0
    tok = [pl.BlockSpec((rows, RWKV_DIM), lambda b, c, p=p: (blk0 + (b * n_par + p) * steps + c, 0))
           for p in range(n_par)]
    st = pl.BlockSpec((n_par, RWKV_HEADS, RWKV_N, RWKV_N), lambda b, c: (b, 0, 0, 0))
    y, h_fin = pl.pallas_call(
        functools.partial(_wkv_kernel, C=C, n_sub=n_sub, n_par=n_par),
        grid=(n_seq // n_par, steps),
        in_specs=[tok[p] for p in range(n_par) for _ in range(6)] + [st],
        out_specs=[pl.BlockSpec((n_par, rows, RWKV_DIM), lambda b, c: (b, c, 0)), st],
        out_shape=[
            jax.ShapeDtypeStruct((n_seq, n_chunk * C, RWKV_DIM), F32),
            jax.ShapeDtypeStruct((n_seq, RWKV_HEADS, RWKV_N, RWKV_N), F32),
        ],
        scratch_shapes=[pltpu.VMEM((n_par * n_grp, GW, GW), F32)],
        compiler_params=_cparams(("parallel", "arbitrary")),
        name="wkv_c%d" % C,
    )(*(list(arrs) * n_par), h0)
    return y.reshape(n_seq * n_chunk * C, RWKV_DIM), h_fin


def _mix_kernel(xp_ref, xs_ref, attnp_ref, attns_ref, yp_ref, ys_ref, bonus_ref, g_ref, seg_ref, lng_ref,
                lnb_ref, woa_ref, wob_ref, g1_ref, b1_ref, wr_ref, br_ref, tri_ref, h_ref, hpk_ref, route_ref,
                count_ref, *, n_p, tm):
    is_p = pl.program_id(0) < n_p
    P = range(MIX_PARTS)
    bands = [slice(k * tm // MIX_PARTS, (k + 1) * tm // MIX_PARTS) for k in P]
    seg = seg_ref[...]
    inv_n = 1.0 / RWKV_N
    y = [jnp.where(is_p, yp_ref[r, :], ys_ref[r, :]) for r in bands]
    yc = [y[k] - _dot_exact_rhs(y[k], seg) * inv_n for k in P]
    var = [_dot_exact_rhs(yc[k] * yc[k], seg) * inv_n for k in P]
    yn = [yc[k] * lax.rsqrt(var[k] + GN_EPS) * lng_ref[...] + lnb_ref[...] for k in P]
    rw = [((yn[k] + bonus_ref[bands[k], :]) * g_ref[bands[k], :]).astype(BF16) for k in P]
    attn = [jnp.where(is_p, attnp_ref[r, :], attns_ref[r, :]) for r in bands]
    m = [_dot(attn[k], woa_ref[...]) + _dot(rw[k], wob_ref[...]) for k in P]
    x = [jnp.where(is_p, xp_ref[r, :], xs_ref[r, :]) for r in bands]
    h = [_layer_norm(DN_ALPHA * x[k] + m[k], g1_ref[...], b1_ref[...]) for k in P]
    for k in P:
        h_ref[bands[k], :] = h[k]
        hpk_ref[bands[k], :] = _pack_bf16_pairs(h[k])

    h_hi = [h[k].astype(BF16) for k in P]
    h_lo = [(h[k] - h_hi[k].astype(F32)).astype(BF16) for k in P]
    logits = [_dot(h_hi[k], wr_ref[0]) + _dot(h_lo[k], wr_ref[0]) + _dot(h_hi[k], wr_ref[1]) + br_ref[...]
              for k in P]
    lane = lax.broadcasted_iota(jnp.int32, logits[0].shape, 1)
    big = jnp.int32(LANE)

    def route_band(lg):
        gl = jnp.where(lane < N_GROUPS, lg, NEG_INF)
        gmax = jnp.max(gl, axis=1, keepdims=True)
        grp = jnp.min(jnp.where(gl == gmax, lane, big), axis=1, keepdims=True)
        p_grp = 1.0 / jnp.sum(jnp.exp(gl - gmax), axis=1, keepdims=True)
        e_idx = lane - N_GROUPS
        in_grp = (lane >= N_GROUPS) & (lane < N_GROUPS + N_EXPERTS) & ((e_idx // EXPERTS_PER_GROUP) == grp)
        el = jnp.where(in_grp, lg, NEG_INF)
        m1 = jnp.max(el, axis=1, keepdims=True)
        i1 = jnp.min(jnp.where(el == m1, lane, big), axis=1, keepdims=True)
        el2 = jnp.where(lane == i1, NEG_INF, el)
        m2 = jnp.max(el2, axis=1, keepdims=True)
        i2 = jnp.min(jnp.where(el2 == m2, lane, big), axis=1, keepdims=True)
        t = jnp.exp(m2 - m1)
        g1 = p_grp / (1.0 + t)
        return i1, i2, g1, g1 * t

    routed = [route_band(logits[k]) for k in P]
    chosen = jnp.concatenate(
        [jnp.where(lane == routed[k][0], 1.0, jnp.where(lane == routed[k][1], 1.0, 0.0)) for k in P], axis=0)
    before = _dot(tri_ref[...], chosen.astype(BF16))
    count_ref[0] = jnp.sum(chosen, axis=0, keepdims=True)
    for k in P:
        i1, i2, g1, g2 = routed[k]
        bef = before[bands[k]]
        r1 = jnp.sum(jnp.where(lane == i1, bef, 0.0), axis=1, keepdims=True)
        r2 = jnp.sum(jnp.where(lane == i2, bef, 0.0), axis=1, keepdims=True)
        cols = ((i1 - N_GROUPS).astype(F32), (i2 - N_GROUPS).astype(F32), g1, g2, r1, r2)
        route = jnp.zeros(lane.shape, F32)
        for j, col in enumerate(cols):
            route = jnp.where(lane == j, col, route)
        route_ref[bands[k], :] = route


def _mix_call(xp, xs, attn_p, attn_s, y_p, y_s, bonus, g, seg, lnx_g, lnx_b, woa, wob, ln1_g, ln1_b, wr, br,
              tile0, n_tiles):
    tm = TOKEN_TILE
    T = n_tiles * tm
    n_p = xp.shape[0] // tm
    row = lambda i: (i, 0)
    full = lambda i: (0, 0)
    row_p, row_s = _split_rows(n_p, tile0)
    half = pl.BlockSpec((tm, RWKV_DIM), lambda i: (i + tile0, 0))
    vec5 = pl.BlockSpec((1, RWKV_DIM), full)
    vec10 = pl.BlockSpec((1, D_MODEL), full)
    idx = np.arange(tm)
    tri = jnp.asarray((idx[None, :] < idx[:, None]).astype(np.float32)).astype(BF16)
    return pl.pallas_call(
        functools.partial(_mix_kernel, n_p=n_p - tile0, tm=tm),
        grid=(n_tiles,),
        in_specs=[
            pl.BlockSpec((tm, D_MODEL), row_p), pl.BlockSpec((tm, D_MODEL), row_s),
            pl.BlockSpec((tm, RWKV_DIM), row_p), pl.BlockSpec((tm, RWKV_DIM), row_s),
            pl.BlockSpec((tm, RWKV_DIM), row_p), pl.BlockSpec((tm, RWKV_DIM), row_s),
            half, half,
            pl.BlockSpec((RWKV_DIM, RWKV_DIM), full), vec5, vec5,
            pl.BlockSpec((RWKV_DIM, D_MODEL), full), pl.BlockSpec((RWKV_DIM, D_MODEL), full),
            vec10, vec10,
            pl.BlockSpec((2, D_MODEL, LANE), lambda i: (0, 0, 0)), pl.BlockSpec((1, LANE), full),
            pl.BlockSpec((tm, tm), full),
        ],
        out_specs=[pl.BlockSpec((tm, D_MODEL), row), pl.BlockSpec((tm, D_MODEL // 2), row),
                   pl.BlockSpec((tm, LANE), row), pl.BlockSpec((1, 1, LANE), lambda i: (i, 0, 0))],
        out_shape=[jax.ShapeDtypeStruct((T, D_MODEL), F32), jax.ShapeDtypeStruct((T, D_MODEL // 2), jnp.int32),
                   jax.ShapeDtypeStruct((T, LANE), F32), jax.ShapeDtypeStruct((T // tm, 1, LANE), F32)],
        compiler_params=_cparams(("parallel",)),
        name="mix",
    )(xp, xs, attn_p, attn_s, y_p, y_s, bonus, g, seg, lnx_g, lnx_b, woa, wob, ln1_g, ln1_b, wr, br, tri)


def _expert_kernel(be_ref, nu_ref, xs_ref, wg_ref, wu_ref, wd_ref, ys_ref, wgu_b, wd_b):
    i = pl.program_id(0)

    @pl.when((i == 0) | (be_ref[i] != be_ref[jnp.maximum(i - 1, 0)]))
    def _():
        wgu_b[:, :D_EXPERT] = wg_ref[0].astype(BF16)
        wgu_b[:, D_EXPERT:] = wu_ref[0].astype(BF16)
        wd_b[...] = wd_ref[0].astype(BF16)

    @pl.when(i < nu_ref[0])
    def _():
        P = range(2)
        bands = [slice(k * MOE_BLK // 2, (k + 1) * MOE_BLK // 2) for k in P]
        xb = [_unpack_bf16_pairs(xs_ref[r, :]).astype(BF16) for r in bands]
        gu = [_dot(xb[k], wgu_b[...]) for k in P]
        act = [(gu[k][:, :D_EXPERT] * _sigmoid(gu[k][:, :D_EXPERT]) * gu[k][:, D_EXPERT:]).astype(BF16)
               for k in P]
        out = [_dot(act[k], wd_b[...]) for k in P]
        for k in P:
            ys_ref[bands[k], :] = _pack_bf16_pairs(out[k])

    @pl.when(i >= nu_ref[0])
    def _():
        ys_ref[...] = jnp.zeros(ys_ref.shape, ys_ref.dtype)


def _expert_call(block_e, n_used, xs, wg, wu, wd):
    n_blk = xs.shape[0] // MOE_BLK
    grid_spec = pltpu.PrefetchScalarGridSpec(
        num_scalar_prefetch=2,
        grid=(n_blk,),
        in_specs=[
            pl.BlockSpec((MOE_BLK, D_MODEL // 2), lambda i, be, nu: (i, 0)),
            pl.BlockSpec((1, D_MODEL, D_EXPERT), lambda i, be, nu: (be[i], 0, 0)),
            pl.BlockSpec((1, D_MODEL, D_EXPERT), lambda i, be, nu: (be[i], 0, 0)),
            pl.BlockSpec((1, D_EXPERT, D_MODEL), lambda i, be, nu: (be[i], 0, 0)),
        ],
        out_specs=pl.BlockSpec((MOE_BLK, D_MODEL // 2), lambda i, be, nu: (i, 0)),
        scratch_shapes=[pltpu.VMEM((D_MODEL, 2 * D_EXPERT), BF16), pltpu.VMEM((D_EXPERT, D_MODEL), BF16)],
    )
    return pl.pallas_call(
        _expert_kernel,
        grid_spec=grid_spec,
        out_shape=jax.ShapeDtypeStruct((n_blk * MOE_BLK, D_MODEL // 2), jnp.int32),
        compiler_params=_cparams(("arbitrary",)),
        name="experts",
    )(block_e, n_used, xs, wg, wu, wd)


def _combine_kernel(h_ref, ya_ref, yb_ref, route_ref, g2_ref, b2_ref, *rest, n_p, has_prev, has_prompt,
                    has_sample):
    outs = list(rest[1:] if has_prev else rest)
    i = pl.program_id(0)
    route = route_ref[...]
    f = _unpack_bf16_pairs(ya_ref[...]) * route[:, 2:3] + _unpack_bf16_pairs(yb_ref[...]) * route[:, 3:4]
    out = _layer_norm(DN_ALPHA * h_ref[...] + f, g2_ref[...], b2_ref[...])

    if has_prompt:
        op_ref = outs.pop(0)

        @pl.when(i < n_p)
        def _():
            op_ref[...] = out

    if has_sample:
        os_ref = outs.pop(0)

        @pl.when(i >= n_p)
        def _():
            os_ref[...] = out


def _combine_call(h, yab, route, ln2_g, ln2_b, t_prompt, t_sample, tile0, out_p_prev):
    tm = TOKEN_TILE
    n_t, n_p = h.shape[0] // tm, t_prompt // tm
    has_prompt = tile0 < n_p
    has_sample = tile0 + n_t > n_p
    has_prev = has_prompt and out_p_prev is not None
    row = lambda i: (i, 0)
    full = lambda i: (0, 0)
    row_p, row_s = _split_rows(n_p, tile0)
    big = pl.BlockSpec((tm, D_MODEL), row)
    in_specs = [big, pl.BlockSpec((tm, D_MODEL // 2), row),
                pl.BlockSpec((tm, D_MODEL // 2), lambda i: (i + n_t, 0)), pl.BlockSpec((tm, LANE), row),
                pl.BlockSpec((1, D_MODEL), full), pl.BlockSpec((1, D_MODEL), full)]
    args = [h, yab, yab, route, ln2_g, ln2_b]
    out_specs, out_shape, aliases = [], [], {}
    if has_prompt:
        out_specs.append(pl.BlockSpec((tm, D_MODEL), row_p))
        out_shape.append(jax.ShapeDtypeStruct((t_prompt, D_MODEL), F32))
    if has_prev:
        in_specs.append(pl.BlockSpec(memory_space=pl.ANY))
        args.append(out_p_prev)
        aliases = {len(args) - 1: 0}
    if has_sample:
        out_specs.append(pl.BlockSpec((tm, D_MODEL), row_s))
        out_shape.append(jax.ShapeDtypeStruct((t_sample, D_MODEL), F32))
    outs = pl.pallas_call(
        functools.partial(_combine_kernel, n_p=n_p - tile0, has_prev=has_prev, has_prompt=has_prompt,
                          has_sample=has_sample),
        grid=(n_t,),
        in_specs=in_specs,
        out_specs=out_specs,
        out_shape=out_shape,
        input_output_aliases=aliases,
        compiler_params=_cparams(("arbitrary",)),
        name="combine",
    )(*args)
    out_p = outs[0] if has_prompt else out_p_prev
    out_s = outs[-1] if has_sample else None
    return out_p, out_s


def _prep_weights(w_in, w_uq, w_ukv):
    half = MLA_ROPE // 2
    kpe_w = w_in[:, Q_LORA + KV_LORA:MLA_IN]
    kpe_b = jnp.concatenate([-kpe_w[:, half:], kpe_w[:, :half]], axis=1)
    w1 = jnp.concatenate([w_in[:, :Q_LORA + KV_LORA], kpe_w, kpe_b,
                          jnp.zeros((D_MODEL, LANE - 2 * MLA_ROPE), F32), w_in[:, MLA_IN:]], axis=1).astype(BF16)
    pad_q = jnp.zeros((Q_LORA, MLA_HEADS, HEAD_PAD - MLA_NOPE - MLA_ROPE), F32)
    wqa = jnp.concatenate([w_uq, pad_q], axis=2).reshape(Q_LORA, -1).astype(BF16)
    w_uk, w_uv = w_ukv[:, :, :MLA_NOPE], w_ukv[:, :, MLA_NOPE:]
    wk = jnp.concatenate([w_uk, jnp.zeros((KV_LORA, MLA_HEADS, HEAD_PAD - MLA_NOPE), F32)], axis=2)
    wk = wk.reshape(KV_LORA, -1).astype(BF16)
    pk_np = np.zeros((MLA_ROPE, MLA_HEADS * HEAD_PAD), np.float32)
    for h in range(MLA_HEADS):
        for i in range(MLA_ROPE):
            pk_np[i, h * HEAD_PAD + MLA_NOPE + i] = 1.0
    pk = jnp.asarray(pk_np).astype(BF16)
    zv = jnp.zeros((KV_LORA, MLA_HEADS // 2, MLA_V), F32)
    wv4 = w_uv.reshape(KV_LORA, MLA_HEADS // 2, 2, MLA_V)
    wv = jnp.stack([jnp.concatenate([wv4[:, :, 0], zv], axis=2),
                    jnp.concatenate([zv, wv4[:, :, 1]], axis=2)], axis=2)
    wv = wv.reshape(KV_LORA, -1).astype(BF16)
    wuk = jnp.transpose(w_uk, (1, 2, 0)).astype(BF16)
    wuv_np = np.zeros((MLA_HEADS, MLA_HEADS * MLA_V), np.float32)
    for h in range(MLA_HEADS):
        wuv_np[h, h * MLA_V:(h + 1) * MLA_V] = 1.0
    wuv = jnp.transpose(w_uv, (1, 0, 2))
    wuv = (jnp.tile(wuv, (1, 1, MLA_HEADS)) * jnp.asarray(wuv_np)[:, None, :]).astype(BF16)
    return w1, wqa, wk, pk, wv, wuk, wuv


def _rope_table(pos):
    inv = ROPE_BASE ** (-jnp.arange(0, MLA_ROPE, 2, dtype=F32) / MLA_ROPE)
    ang = pos.astype(F32)[:, None] * inv[None, :]
    return jnp.concatenate([jnp.cos(ang), jnp.sin(ang)], axis=1)


def _rope_placement():
    half = MLA_ROPE // 2
    place = np.zeros((MLA_ROPE, 4 * LANE), np.float32)
    rows = np.zeros((2, 4 * LANE), np.float32)
    for i in range(half):
        c, s = i, half + i
        place[c, MLA_NOPE + i] = place[c, MLA_NOPE + half + i] = 1.0
        place[s, LANE + MLA_NOPE + i] = -1.0
        place[s, 2 * LANE + MLA_NOPE + half + i] = 1.0
        place[c, 3 * LANE + i] = place[c, 3 * LANE + half + i] = 1.0
        place[s, 3 * LANE + 2 * half + i] = place[s, 3 * LANE + 3 * half + i] = 1.0
    rows[0, :MLA_NOPE] = 1.0
    rows[1, :3 * LANE] = MLA_SCALE * LOG2E
    rows[1, 3 * LANE:] = 1.0
    return jnp.asarray(place).astype(BF16), jnp.asarray(rows)


def _seg_ones():
    idx = np.arange(RWKV_DIM) // RWKV_N
    return jnp.asarray((idx[:, None] == idx[None, :]).astype(np.float32)).astype(BF16)


def _dispatch(route, tile_counts, t_total):
    A = t_total * TOP_K
    n_tiles = tile_counts.shape[0]
    counts_te = tile_counts[:, 0, N_GROUPS:N_GROUPS + N_EXPERTS].astype(jnp.int32)
    counts = jnp.sum(counts_te, axis=0)
    blocks_per_e = (counts + MOE_BLK - 1) // MOE_BLK
    blk_end = jnp.cumsum(blocks_per_e)
    blk_start = blk_end - blocks_per_e
    tile_off = jnp.cumsum(counts_te, axis=0) - counts_te
    base = blk_start[None, :] * MOE_BLK + tile_off
    base_tok = jnp.repeat(base, t_total // n_tiles, axis=0)
    experts = jnp.arange(N_EXPERTS, dtype=jnp.int32)[None, :]
    dest = []
    for k in range(TOP_K):
        e_k = route[:, k].astype(jnp.int32)
        rank_k = route[:, 4 + k].astype(jnp.int32)
        dest.append(jnp.sum(jnp.where(e_k[:, None] == experts, base_tok, 0), axis=-1) + rank_k)
    n_blk = -(-A // MOE_BLK) + N_EXPERTS
    blk = jnp.arange(n_blk, dtype=jnp.int32)
    block_e = jnp.minimum(jnp.sum((blk[:, None] >= blk_end[None, :]).astype(jnp.int32), axis=1),
                          N_EXPERTS - 1).astype(jnp.int32)
    n_used = blk_end[-1:].astype(jnp.int32)
    return dest, block_e, n_used


def _sc_gather_rows(table, idx):
    n_rows, width = idx.shape[0], table.shape[1]
    n_workers = SC_CORES * SC_SUBCORES
    per_worker = n_rows // n_workers
    assert n_rows % n_workers == 0 and per_worker % SC_WINDOW == 0
    mesh = plsc.VectorSubcoreMesh(core_axis_name="c", subcore_axis_name="s")

    @functools.partial(
        pl.kernel, mesh=mesh,
        out_type=jax.ShapeDtypeStruct((n_rows, width), table.dtype),
        scratch_types=[
            pltpu.VMEM((SC_WINDOW,), jnp.int32),
            pltpu.VMEM((SC_WINDOW, width), table.dtype),
            pltpu.SemaphoreType.DMA,
        ],
    )
    def gather(table_hbm, idx_hbm, out_hbm, idx_v, rows_v, sem):
        wid = lax.axis_index("s") * SC_CORES + lax.axis_index("c")
        base = wid * per_worker

        @pl.loop(0, per_worker // SC_WINDOW)
        def _(w):
            off = pl.multiple_of(base + w * SC_WINDOW, SC_WINDOW)
            pltpu.sync_copy(idx_hbm.at[pl.ds(off, SC_WINDOW)], idx_v)
            pltpu.async_copy(table_hbm.at[idx_v], rows_v, sem).wait()
            pltpu.sync_copy(rows_v, out_hbm.at[pl.ds(off, SC_WINDOW)])

    return gather(table, idx)


def _sc_scatter_rows(src, idx_a, idx_b, n_slots):
    n_rows, width = src.shape
    n_workers = SC_CORES * SC_SUBCORES
    per_worker = n_rows // n_workers
    assert n_rows % n_workers == 0 and per_worker % SC_WINDOW == 0
    n_win = per_worker // SC_WINDOW
    mesh = plsc.VectorSubcoreMesh(core_axis_name="c", subcore_axis_name="s")

    @functools.partial(
        pl.kernel, mesh=mesh,
        out_type=jax.ShapeDtypeStruct((n_slots, width), src.dtype),
        scratch_types=[
            pltpu.VMEM((1, SC_WINDOW), jnp.int32),
            pltpu.VMEM((1, SC_WINDOW), jnp.int32),
            pltpu.VMEM((SC_WINDOW, width), src.dtype),
        ],
    )
    def scatter(src_hbm, ia_hbm, ib_hbm, out_hbm, ia_v, ib_v, rows_v):
        wid = lax.axis_index("s") * SC_CORES + lax.axis_index("c")

        @pl.loop(0, n_win)
        def _(w):
            win = wid * n_win + w
            off = pl.multiple_of(win * SC_WINDOW, SC_WINDOW)
            pltpu.sync_copy(src_hbm.at[pl.ds(off, SC_WINDOW)], rows_v)
            pltpu.sync_copy(ia_hbm.at[pl.ds(win, 1)], ia_v)
            pltpu.sync_copy(ib_hbm.at[pl.ds(win, 1)], ib_v)
            pltpu.sync_copy(rows_v, out_hbm.at[ia_v.at[0]])
            pltpu.sync_copy(rows_v, out_hbm.at[ib_v.at[0]])

    return scatter(src, idx_a, idx_b)


def kernel(x_prompt, x_sample, cache_ckv, cache_kpe, state_wkv, state_shift, w_in, q_norm_g, kv_norm_g, w_uq,
           w_ukv, mu_shift, w0, w2, a0, a2, g2, k_k, k_a, r_k, lnx_g, lnx_b, w_o, ln1_g, ln1_b, w_gr, b_gr,
           w_er, b_er, w_eg, w_eu, w_ed, ln2_g, ln2_b):
    B, S, D = x_prompt.shape
    DB, DS, _ = x_sample.shape
    past = cache_ckv.shape[2]
    Tp, Ts = B * S, DB * DS
    T = Tp + Ts
    assert D == D_MODEL and DS == SHIFT_GROUP and S % ATTN_TQ == 0 and S % (CHUNK * WKV_SUB) == 0
    assert Tp % TOKEN_TILE == 0 and T % TOKEN_TILE == 0 and w_in.shape[0] == DEPTH

    l = 0
    xp, xs_in = x_prompt.reshape(Tp, D), x_sample.reshape(Ts, D)
    w1, wqa, wk, pk, wv, wuk, wuv = _prep_weights(w_in[l], w_uq[l], w_ukv[l])
    pos = jnp.concatenate([jnp.arange(S, dtype=jnp.int32),
                           jnp.tile(past + jnp.arange(DS, dtype=jnp.int32), TOKEN_TILE // DS)])
    rope = _rope_table(pos)

    seg = _seg_ones()
    vec = lambda a: a.reshape(1, -1)
    rwkv_w = (vec(mu_shift[l]), vec(w0[l]), vec(a0[l]), vec(k_k[l]), vec(k_a[l]), vec(r_k[l]),
              w2[l].astype(BF16), a2[l].astype(BF16), g2[l].astype(BF16), seg)
    (q, kcat, vcat, ckv_p, ckv_s, kpe_p, kpe_s, last_rows,
     r, lw, kh, v, na, b, bonus, g) = _proj_call(
        xp, xs_in, rope, S // TOKEN_TILE, *_rope_placement(), w1, q_norm_g[l][None], kv_norm_g[l][None],
        wqa, wk, pk, wv, state_shift[l][:, None, :], rwkv_w)

    attn_p = _attn_call(q, kcat, vcat, B, S)
    attn_s = _mla_sample_call(q, ckv_s, kpe_s, cache_ckv[l], jnp.swapaxes(cache_kpe[l], 1, 2), wuk, wuv,
                              Tp, DB, DS)

    scan_in = (r, lw, kh, v, na, b)
    h0_p = jnp.zeros((B, RWKV_HEADS, RWKV_N, RWKV_N), F32)
    y_p, hT_p = _wkv_call(scan_in, h0_p, 0, B, S // CHUNK, CHUNK, WKV_SUB, math.gcd(B, WKV_PAR))
    h0_s = state_wkv[l]
    y_s, hT_s = _wkv_call(scan_in, h0_s, Tp, DB, 1, DS, 1, math.gcd(DB, WKV_SUB * WKV_PAR))

    wo_b = w_o[l].astype(BF16)
    wr = jnp.concatenate([w_gr[l], w_er[l], jnp.zeros((D, LANE - N_GROUPS - N_EXPERTS), F32)], axis=1)
    wr_hi = wr.astype(BF16)
    wr_lo = (wr - wr_hi.astype(F32)).astype(BF16)
    br = jnp.concatenate([b_gr[l], b_er[l], jnp.zeros((LANE - N_GROUPS - N_EXPERTS,), F32)])[None]
    n_tiles = T // TOKEN_TILE
    wave_tiles = [(t0, min(t0 + MOE_WAVE_TILES, n_tiles) - t0) for t0 in range(0, n_tiles, MOE_WAVE_TILES)]
    out_p, out_s = None, None
    for tile0, nt in wave_tiles:
        t_w = nt * TOKEN_TILE
        h, hpk, route, tile_counts = _mix_call(
            xp, xs_in, attn_p, attn_s, y_p, y_s, bonus, g, seg, vec(lnx_g[l]), vec(lnx_b[l]),
            wo_b[:MLA_HEADS * MLA_V], wo_b[MLA_HEADS * MLA_V:], vec(ln1_g[l]), vec(ln1_b[l]),
            jnp.stack([wr_hi, wr_lo]), br, tile0, nt)
        dest, block_e, n_used = _dispatch(route, tile_counts, t_w)
        win = lambda a: a.reshape(t_w // SC_WINDOW, SC_WINDOW)
        xs = _sc_scatter_rows(hpk, win(dest[0]), win(dest[1]), block_e.shape[0] * MOE_BLK)
        ys = _expert_call(block_e, n_used, xs, w_eg[l], w_eu[l], w_ed[l])
        yab = _sc_gather_rows(ys, jnp.concatenate(dest))
        out_p, wave_s = _combine_call(h, yab, route, vec(ln2_g[l]), vec(ln2_b[l]), Tp, Ts, tile0, out_p)
        assert wave_s is None or out_s is None, "the sample tiles must fall inside one wave"
        out_s = wave_s if wave_s is not None else out_s

    y_prompt = out_p.reshape(B, S, D)
    y_sample = out_s.reshape(DB, DS, D)
    p_ckv = ckv_p.reshape(1, B, S, KV_LORA)
    p_kpe = jnp.transpose(kpe_p.reshape(MLA_ROPE, B, S), (1, 2, 0))[None]
    s_ckv = ckv_s.reshape(1, DB, DS, KV_LORA)
    s_kpe = kpe_s.reshape(1, DB, DS, MLA_ROPE)
    p_wkv = hT_p[None]
    s_wkv = hT_s[None]
    gp = S // SHIFT_GROUP
    p_sh = last_rows[gp - 1:B * gp:gp][None]
    s_sh = last_rows[B * gp:][None]
    return (y_prompt, y_sample, p_ckv, p_kpe, p_wkv, p_sh, s_ckv, s_kpe, s_wkv, s_sh)
```
